```python
import jax, jax.numpy as jnp
from jax import lax
import numpy as np

D_MODEL = 1024
BATCH = 8
SEQ = 8192
DEPTH = 1

CHUNK = 64
D_MIX = D_MODEL
D_GMLP = D_MIX // 2
D_HGRN = D_MIX - D_GMLP
GMLP_HEADS = 4
GMLP_HEAD_DIM = D_GMLP // GMLP_HEADS
GMLP_BLOCK = 128
HGRN_HEADS = 4
HGRN_HEAD_DIM = D_HGRN // HGRN_HEADS
D_FF = -(-(8 * D_MODEL) // (3 * 256)) * 256
N_ADA = 6
D_IN = 2 * D_GMLP + 4 * D_HGRN
EPS = 1e-6

kernel_name = "hybrid_gmlp_hgrn2_adaln_block"


def rmsnorm(x, w):
    xf = x.astype(jnp.float32)
    y = xf * lax.rsqrt(jnp.mean(xf * xf, axis=-1, keepdims=True) + EPS)
    return (y * w.astype(jnp.float32)).astype(x.dtype)


def layernorm(x, w, b):
    xf = x.astype(jnp.float32)
    mu = jnp.mean(xf, axis=-1, keepdims=True)
    var = jnp.mean(jnp.square(xf - mu), axis=-1, keepdims=True)
    y = (xf - mu) * lax.rsqrt(var + EPS)
    return (y * w.astype(jnp.float32) + b.astype(jnp.float32)).astype(x.dtype)


def modulate(h, shift, scale):
    return h * (1 + scale[:, None, :]) + shift[:, None, :]


def gmlp_spatial_gating(u, v, w_s, b_s, ln_w, ln_b):
    bsz, seq, _ = u.shape
    nb = seq // GMLP_BLOCK
    u = jax.nn.gelu(u, approximate=False)
    v = layernorm(jax.nn.gelu(v, approximate=False), ln_w, ln_b)
    vb = v.reshape(bsz, nb, GMLP_BLOCK, GMLP_HEADS, GMLP_HEAD_DIM)
    cid = jnp.arange(GMLP_BLOCK) // CHUNK
    mask = cid[:, None] >= cid[None, :]
    ws = jnp.where(mask[None], w_s, 0).astype(v.dtype)
    mixed = jnp.einsum('hts,bnshc->bnthc', ws, vb) + b_s.T.astype(v.dtype)[None, None, :, :, None]
    return u * mixed.reshape(bsz, seq, D_GMLP)


def hgrn2_recurrence(q, f_logit, inp, g, lb, gn_w):
    dtype = q.dtype
    bsz, seq, _ = q.shape
    nc = seq // CHUNK
    qf = jax.nn.silu(q.astype(jnp.float32))
    f = lb + (1.0 - lb) * jax.nn.sigmoid(f_logit.astype(jnp.float32))
    logf = jnp.log(f)
    k = 1.0 - f
    vf = inp.astype(jnp.float32)

    def to_chunks(t):
        return t.reshape(bsz, nc, CHUNK, HGRN_HEADS, HGRN_HEAD_DIM).transpose(1, 0, 3, 2, 4)

    tri = jnp.arange(CHUNK)[:, None] >= jnp.arange(CHUNK)[None, :]

    def step(state, xs):
        qc, kc, vc, lc = xs
        b = jnp.cumsum(lc, axis=2)
        inter = jnp.einsum('bhtd,bhde->bhte', qc * jnp.exp(b), state)
        diff = b[:, :, :, None, :] - b[:, :, None, :, :]
        decay = jnp.where(tri[:, :, None], jnp.exp(jnp.minimum(diff, 0.0)), 0.0)
        attn = jnp.einsum('bhtd,bhsd,bhtsd->bhts', qc, kc, decay)
        intra = jnp.einsum('bhts,bhse->bhte', attn, vc)
        b_last = b[:, :, -1, :]
        state = state * jnp.exp(b_last)[..., None] + jnp.einsum(
            'bhsd,bhse->bhde', kc * jnp.exp(b_last[:, :, None, :] - b), vc)
        return state, inter + intra

    s0 = jnp.zeros((bsz, HGRN_HEADS, HGRN_HEAD_DIM, HGRN_HEAD_DIM), jnp.float32)
    _, o = lax.scan(step, s0, (to_chunks(qf), to_chunks(k), to_chunks(vf), to_chunks(logf)))
    o = o.transpose(1, 0, 3, 2, 4).reshape(bsz, seq, HGRN_HEADS, HGRN_HEAD_DIM)
    gate = jax.nn.silu(g.astype(jnp.float32)).reshape(bsz, seq, HGRN_HEADS, HGRN_HEAD_DIM)
    o = rmsnorm(o, gn_w) * gate
    return o.reshape(bsz, seq, D_HGRN).astype(dtype)


def _fwd_setup_inputs(seed: int = 0) -> dict:
    key = jax.random.key(seed)
    ks = jax.random.split(key, 20)
    nrm = lambda k, shape, s: jax.random.normal(k, shape, jnp.float32) * s
    return {
        "x": nrm(ks[0], (BATCH, SEQ, D_MODEL), 1.0),
        "c": nrm(ks[1], (BATCH, D_MODEL), 1.0),
        "w_ada": nrm(ks[2], (DEPTH, D_MODEL, N_ADA * D_MODEL), 0.5 * D_MODEL ** -0.5),
        "b_ada": nrm(ks[3], (DEPTH, N_ADA * D_MODEL), 0.02),
        "norm1_w": 1.0 + nrm(ks[4], (DEPTH, D_MODEL), 0.02),
        "w_in": nrm(ks[5], (DEPTH, D_MODEL, D_IN), D_MODEL ** -0.5),
        "w_s": nrm(ks[6], (DEPTH, GMLP_HEADS, GMLP_BLOCK, GMLP_BLOCK), GMLP_BLOCK ** -0.5),
        "b_s": 1.0 + nrm(ks[7], (DEPTH, GMLP_HEADS, GMLP_BLOCK), 0.02),
        "v_ln_w": 1.0 + nrm(ks[8], (DEPTH, D_GMLP), 0.02),
        "v_ln_b": nrm(ks[9], (DEPTH, D_GMLP), 0.02),
        "lower_bounds": nrm(ks[10], (DEPTH + 1, D_HGRN), 0.5),
        "gn_w": 1.0 + nrm(ks[11], (DEPTH, HGRN_HEAD_DIM), 0.02),
        "w_out": nrm(ks[12], (DEPTH, D_MIX, D_MODEL), D_MIX ** -0.5),
        "norm2_w": 1.0 + nrm(ks[13], (DEPTH, D_MODEL), 0.02),
        "w_ffn_in": nrm(ks[14], (DEPTH, D_MODEL, 2 * D_FF), D_MODEL ** -0.5),
        "w_ffn_out": nrm(ks[15], (DEPTH, D_FF, D_MODEL), D_FF ** -0.5),
        "final_norm_w": 1.0 + nrm(ks[16], (D_MODEL,), 0.02),
    }


def _fwd_reference(x, c, w_ada, b_ada, norm1_w, w_in, w_s, b_s, v_ln_w, v_ln_b,
              lower_bounds, gn_w, w_out, norm2_w, w_ffn_in, w_ffn_out, final_norm_w):
    lb_all = jnp.cumsum(jax.nn.softmax(lower_bounds.astype(jnp.float32), axis=0), axis=0)
    c_act = jax.nn.silu(c)
    split_at = [D_GMLP, 2 * D_GMLP, 2 * D_GMLP + D_HGRN,
                2 * D_GMLP + 2 * D_HGRN, 2 * D_GMLP + 3 * D_HGRN]
    for l in range(DEPTH):
        ada = (c_act @ w_ada[l] + b_ada[l]).astype(x.dtype)
        sh1, sc1, g1, sh2, sc2, g2 = jnp.split(ada, N_ADA, axis=-1)

        h = modulate(rmsnorm(x, norm1_w[l]), sh1, sc1)
        proj = h @ w_in[l]
        u, v, q, f_logit, inp, g = jnp.split(proj, split_at, axis=-1)
        y_a = gmlp_spatial_gating(u, v, w_s[l], b_s[l], v_ln_w[l], v_ln_b[l])
        y_b = hgrn2_recurrence(q, f_logit, inp, g, lb_all[l], gn_w[l])
        mix = jnp.concatenate([y_a, y_b], axis=-1) @ w_out[l]
        x = x + g1[:, None, :] * mix

        h = modulate(rmsnorm(x, norm2_w[l]), sh2, sc2)
        gate, up = jnp.split(h @ w_ffn_in[l], 2, axis=-1)
        x = x + g2[:, None, :] * ((jax.nn.silu(gate) * up) @ w_ffn_out[l])
    return rmsnorm(x, final_norm_w)


import jax as _jax
import jax.numpy as _jnp

TWIN_FORMAT = 'train_step'
FWD_PARAMS = ['x', 'c', 'w_ada', 'b_ada', 'norm1_w', 'w_in', 'w_s', 'b_s', 'v_ln_w', 'v_ln_b', 'lower_bounds', 'gn_w', 'w_out', 'norm2_w', 'w_ffn_in', 'w_ffn_out', 'final_norm_w']
TWIN_WEIGHTS = ['w_ada', 'b_ada', 'norm1_w', 'w_in', 'w_s', 'b_s', 'v_ln_w', 'v_ln_b', 'lower_bounds', 'gn_w', 'w_out', 'norm2_w', 'w_ffn_in', 'w_ffn_out', 'final_norm_w']
TWIN_DIFF_INPUT = 'x'
TWIN_INPUTS = ['x', 'c', 'w_ada', 'b_ada', 'norm1_w', 'w_in', 'w_s', 'b_s', 'v_ln_w', 'v_ln_b', 'lower_bounds', 'gn_w', 'w_out', 'norm2_w', 'w_ffn_in', 'w_ffn_out', 'final_norm_w', 'loss_target', 'm_w_ada', 'm_b_ada', 'm_norm1_w', 'm_w_in', 'm_w_s', 'm_b_s', 'm_v_ln_w', 'm_v_ln_b', 'm_lower_bounds', 'm_gn_w', 'm_w_out', 'm_norm2_w', 'm_w_ffn_in', 'm_w_ffn_out', 'm_final_norm_w', 'v_w_ada', 'v_b_ada', 'v_norm1_w', 'v_w_in', 'v_w_s', 'v_b_s', 'v_v_ln_w', 'v_v_ln_b', 'v_lower_bounds', 'v_gn_w', 'v_w_out', 'v_norm2_w', 'v_w_ffn_in', 'v_w_ffn_out', 'v_final_norm_w']
TWIN_OUTPUTS = ['loss', 'grad_x', 'grad_w_ada', 'grad_b_ada', 'grad_norm1_w', 'grad_w_in', 'grad_w_s', 'grad_b_s', 'grad_v_ln_w', 'grad_v_ln_b', 'grad_lower_bounds', 'grad_gn_w', 'grad_w_out', 'grad_norm2_w', 'grad_w_ffn_in', 'grad_w_ffn_out', 'grad_final_norm_w', 'delta_w_ada', 'delta_b_ada', 'delta_norm1_w', 'delta_w_in', 'delta_w_s', 'delta_b_s', 'delta_v_ln_w', 'delta_v_ln_b', 'delta_lower_bounds', 'delta_gn_w', 'delta_w_out', 'delta_norm2_w', 'delta_w_ffn_in', 'delta_w_ffn_out', 'delta_final_norm_w', 'new_m_w_ada', 'new_m_b_ada', 'new_m_norm1_w', 'new_m_w_in', 'new_m_w_s', 'new_m_b_s', 'new_m_v_ln_w', 'new_m_v_ln_b', 'new_m_lower_bounds', 'new_m_gn_w', 'new_m_w_out', 'new_m_norm2_w', 'new_m_w_ffn_in', 'new_m_w_ffn_out', 'new_m_final_norm_w', 'new_v_w_ada', 'new_v_b_ada', 'new_v_norm1_w', 'new_v_w_in', 'new_v_w_s', 'new_v_b_s', 'new_v_v_ln_w', 'new_v_v_ln_b', 'new_v_lower_bounds', 'new_v_gn_w', 'new_v_w_out', 'new_v_norm2_w', 'new_v_w_ffn_in', 'new_v_w_ffn_out', 'new_v_final_norm_w']
TWIN_LEAF_KINDS = {'loss': 'loss', 'grad_x': 'grad_x', 'grad_w_ada': 'grad_w', 'grad_b_ada': 'grad_w', 'grad_norm1_w': 'grad_w', 'grad_w_in': 'grad_w', 'grad_w_s': 'grad_w', 'grad_b_s': 'grad_w', 'grad_v_ln_w': 'grad_w', 'grad_v_ln_b': 'grad_w', 'grad_lower_bounds': 'grad_w', 'grad_gn_w': 'grad_w', 'grad_w_out': 'grad_w', 'grad_norm2_w': 'grad_w', 'grad_w_ffn_in': 'grad_w', 'grad_w_ffn_out': 'grad_w', 'grad_final_norm_w': 'grad_w', 'delta_w_ada': 'delta_w', 'delta_b_ada': 'delta_w', 'delta_norm1_w': 'delta_w', 'delta_w_in': 'delta_w', 'delta_w_s': 'delta_w', 'delta_b_s': 'delta_w', 'delta_v_ln_w': 'delta_w', 'delta_v_ln_b': 'delta_w', 'delta_lower_bounds': 'delta_w', 'delta_gn_w': 'delta_w', 'delta_w_out': 'delta_w', 'delta_norm2_w': 'delta_w', 'delta_w_ffn_in': 'delta_w', 'delta_w_ffn_out': 'delta_w', 'delta_final_norm_w': 'delta_w', 'new_m_w_ada': 'new_m', 'new_m_b_ada': 'new_m', 'new_m_norm1_w': 'new_m', 'new_m_w_in': 'new_m', 'new_m_w_s': 'new_m', 'new_m_b_s': 'new_m', 'new_m_v_ln_w': 'new_m', 'new_m_v_ln_b': 'new_m', 'new_m_lower_bounds': 'new_m', 'new_m_gn_w': 'new_m', 'new_m_w_out': 'new_m', 'new_m_norm2_w': 'new_m', 'new_m_w_ffn_in': 'new_m', 'new_m_w_ffn_out': 'new_m', 'new_m_final_norm_w': 'new_m', 'new_v_w_ada': 'new_v', 'new_v_b_ada': 'new_v', 'new_v_norm1_w': 'new_v', 'new_v_w_in': 'new_v', 'new_v_w_s': 'new_v', 'new_v_b_s': 'new_v', 'new_v_v_ln_w': 'new_v', 'new_v_v_ln_b': 'new_v', 'new_v_lower_bounds': 'new_v', 'new_v_gn_w': 'new_v', 'new_v_w_out': 'new_v', 'new_v_norm2_w': 'new_v', 'new_v_w_ffn_in': 'new_v', 'new_v_w_ffn_out': 'new_v', 'new_v_final_norm_w': 'new_v'}


def _forward(args):
    return _fwd_reference(*[args[k] for k in FWD_PARAMS])


def _output_shape():
    def fwd():
        inp = _fwd_setup_inputs(0)
        return _fwd_reference(*[inp[k] for k in FWD_PARAMS])
    out = _jax.eval_shape(fwd)
    return out.shape, out.dtype

N_MICROBATCH = 1
ADAM_LR = 0.001
ADAM_B1 = 0.9
ADAM_B2 = 0.999
ADAM_EPS = 1e-08
ADAM_WD = 0.01
ADAM_STEP = 10
PER_EXAMPLE_BATCH_AXIS = {'x': 0, 'c': 0, 'loss_target': 0}
SHARED_INPUTS = []
_WEIGHT_DTYPES = {'w_ada': _jnp.float32, 'b_ada': _jnp.float32, 'norm1_w': _jnp.float32, 'w_in': _jnp.float32, 'w_s': _jnp.float32, 'b_s': _jnp.float32, 'v_ln_w': _jnp.float32, 'v_ln_b': _jnp.float32, 'lower_bounds': _jnp.float32, 'gn_w': _jnp.float32, 'w_out': _jnp.float32, 'norm2_w': _jnp.float32, 'w_ffn_in': _jnp.float32, 'w_ffn_out': _jnp.float32, 'final_norm_w': _jnp.float32}
MOMENT_SCALE = {'w_ada': 8.084327e-02, 'b_ada': 1.417339e-01, 'norm1_w': 7.904185e-02, 'w_in': 4.743729e-02, 'w_s': 4.711275e-02, 'b_s': 6.343617e-02, 'v_ln_w': 4.805302e-02, 'v_ln_b': 5.833660e-02, 'lower_bounds': 3.892622e-03, 'gn_w': 1.098437e-01, 'w_out': 6.166740e-02, 'norm2_w': 7.553289e-02, 'w_ffn_in': 3.217091e-02, 'w_ffn_out': 5.249439e-02, 'final_norm_w': 6.410971e+01}


def _to_microbatches(a, axis):
    t = _jnp.moveaxis(a, axis, 0)
    t = t.reshape((N_MICROBATCH, t.shape[0] // N_MICROBATCH) + t.shape[1:])
    return _jnp.moveaxis(t, 1, axis + 1)


def setup_inputs(seed: int = 0) -> dict:
    inp = _fwd_setup_inputs(seed)
    key = _jax.random.fold_in(_jax.random.key(seed), 7919)
    shape, _ = _output_shape()
    out = dict(inp)
    out["loss_target"] = _jax.random.normal(_jax.random.fold_in(key, 0), shape, _jnp.float32)
    for i, name in enumerate(TWIN_WEIGHTS):
        w = inp[name].astype(_jnp.float32)
        if MOMENT_SCALE is None:
            s = _jnp.sqrt(_jnp.mean(_jnp.square(w)) + 1e-30)
        else:
            s = MOMENT_SCALE[name]
        km, kv = _jax.random.split(_jax.random.fold_in(key, i + 1))
        out[name] = w
        out["m_" + name] = s * _jax.random.normal(km, w.shape, _jnp.float32)
        out["v_" + name] = (s * s) * _jax.random.uniform(kv, w.shape, _jnp.float32, 0.5, 1.5)
    if N_MICROBATCH > 1:
        for name, axis in PER_EXAMPLE_BATCH_AXIS.items():
            out[name] = _to_microbatches(out[name], axis)
    return {'x': out['x'], 'c': out['c'], 'w_ada': out['w_ada'], 'b_ada': out['b_ada'], 'norm1_w': out['norm1_w'], 'w_in': out['w_in'], 'w_s': out['w_s'], 'b_s': out['b_s'], 'v_ln_w': out['v_ln_w'], 'v_ln_b': out['v_ln_b'], 'lower_bounds': out['lower_bounds'], 'gn_w': out['gn_w'], 'w_out': out['w_out'], 'norm2_w': out['norm2_w'], 'w_ffn_in': out['w_ffn_in'], 'w_ffn_out': out['w_ffn_out'], 'final_norm_w': out['final_norm_w'], 'loss_target': out['loss_target'], 'm_w_ada': out['m_w_ada'], 'm_b_ada': out['m_b_ada'], 'm_norm1_w': out['m_norm1_w'], 'm_w_in': out['m_w_in'], 'm_w_s': out['m_w_s'], 'm_b_s': out['m_b_s'], 'm_v_ln_w': out['m_v_ln_w'], 'm_v_ln_b': out['m_v_ln_b'], 'm_lower_bounds': out['m_lower_bounds'], 'm_gn_w': out['m_gn_w'], 'm_w_out': out['m_w_out'], 'm_norm2_w': out['m_norm2_w'], 'm_w_ffn_in': out['m_w_ffn_in'], 'm_w_ffn_out': out['m_w_ffn_out'], 'm_final_norm_w': out['m_final_norm_w'], 'v_w_ada': out['v_w_ada'], 'v_b_ada': out['v_b_ada'], 'v_norm1_w': out['v_norm1_w'], 'v_w_in': out['v_w_in'], 'v_w_s': out['v_w_s'], 'v_b_s': out['v_b_s'], 'v_v_ln_w': out['v_v_ln_w'], 'v_v_ln_b': out['v_v_ln_b'], 'v_lower_bounds': out['v_lower_bounds'], 'v_gn_w': out['v_gn_w'], 'v_w_out': out['v_w_out'], 'v_norm2_w': out['v_norm2_w'], 'v_w_ffn_in': out['v_w_ffn_in'], 'v_w_ffn_out': out['v_w_ffn_out'], 'v_final_norm_w': out['v_final_norm_w']}


def _loss(weights, diff, rest, loss_target):
    with _jax.named_scope("forward"):
        args = {**rest, TWIN_DIFF_INPUT: diff, **{k: w.astype(_WEIGHT_DTYPES[k]) for k, w in weights.items()}}
        y = _forward(args)
    with _jax.named_scope("loss_head"):
        err = _jnp.square(y.astype(_jnp.float32) - loss_target)
        return 0.5 * _jnp.sum(_jnp.mean(err, axis=-1)) if err.ndim else 0.5 * err


def _adamw(w, g, m, v):
    m = ADAM_B1 * m + (1.0 - ADAM_B1) * g
    v = ADAM_B2 * v + (1.0 - ADAM_B2) * _jnp.square(g)
    m_hat = m / (1.0 - ADAM_B1 ** ADAM_STEP)
    v_hat = v / (1.0 - ADAM_B2 ** ADAM_STEP)
    delta = -ADAM_LR * (m_hat / (_jnp.sqrt(v_hat) + ADAM_EPS) + ADAM_WD * w)
    return delta, m, v


def reference(x, c, w_ada, b_ada, norm1_w, w_in, w_s, b_s, v_ln_w, v_ln_b, lower_bounds, gn_w, w_out, norm2_w, w_ffn_in, w_ffn_out, final_norm_w, loss_target, m_w_ada, m_b_ada, m_norm1_w, m_w_in, m_w_s, m_b_s, m_v_ln_w, m_v_ln_b, m_lower_bounds, m_gn_w, m_w_out, m_norm2_w, m_w_ffn_in, m_w_ffn_out, m_final_norm_w, v_w_ada, v_b_ada, v_norm1_w, v_w_in, v_w_s, v_b_s, v_v_ln_w, v_v_ln_b, v_lower_bounds, v_gn_w, v_w_out, v_norm2_w, v_w_ffn_in, v_w_ffn_out, v_final_norm_w):
    given = dict(x=x, c=c, w_ada=w_ada, b_ada=b_ada, norm1_w=norm1_w, w_in=w_in, w_s=w_s, b_s=b_s, v_ln_w=v_ln_w, v_ln_b=v_ln_b, lower_bounds=lower_bounds, gn_w=gn_w, w_out=w_out, norm2_w=norm2_w, w_ffn_in=w_ffn_in, w_ffn_out=w_ffn_out, final_norm_w=final_norm_w, loss_target=loss_target, m_w_ada=m_w_ada, m_b_ada=m_b_ada, m_norm1_w=m_norm1_w, m_w_in=m_w_in, m_w_s=m_w_s, m_b_s=m_b_s, m_v_ln_w=m_v_ln_w, m_v_ln_b=m_v_ln_b, m_lower_bounds=m_lower_bounds, m_gn_w=m_gn_w, m_w_out=m_w_out, m_norm2_w=m_norm2_w, m_w_ffn_in=m_w_ffn_in, m_w_ffn_out=m_w_ffn_out, m_final_norm_w=m_final_norm_w, v_w_ada=v_w_ada, v_b_ada=v_b_ada, v_norm1_w=v_norm1_w, v_w_in=v_w_in, v_w_s=v_w_s, v_b_s=v_b_s, v_v_ln_w=v_v_ln_w, v_v_ln_b=v_v_ln_b, v_lower_bounds=v_lower_bounds, v_gn_w=v_gn_w, v_w_out=v_w_out, v_norm2_w=v_norm2_w, v_w_ffn_in=v_w_ffn_in, v_w_ffn_out=v_w_ffn_out, v_final_norm_w=v_final_norm_w)
    weights = {n: given[n] for n in TWIN_WEIGHTS}
    shared = {n: given[n] for n in SHARED_INPUTS}
    per_example = {n: given[n] for n in ['x', 'c']}
    grad_fn = _jax.value_and_grad(_loss, argnums=(0, 1))

    def one_microbatch(ex, loss_target):
        ex = dict(ex)
        diff = ex.pop(TWIN_DIFF_INPUT)
        return grad_fn(weights, diff, {**shared, **ex}, loss_target)

    if N_MICROBATCH == 1:
        loss, (grad_w, grad_x) = one_microbatch(per_example, given["loss_target"])
    else:
        def body(carry, xs):
            loss_sum, grad_sum = carry
            l_k, (gw_k, gx_k) = one_microbatch(xs[0], xs[1])
            with _jax.named_scope("update"):
                return (loss_sum + l_k, _jax.tree.map(_jnp.add, grad_sum, gw_k)), gx_k

        init = (_jnp.zeros((), _jnp.float32), _jax.tree.map(_jnp.zeros_like, weights))
        (loss, grad_w), grad_x = _jax.lax.scan(body, init, (per_example, given["loss_target"]))
    with _jax.named_scope("update"):
        delta_w, new_m, new_v = {}, {}, {}
        for n in TWIN_WEIGHTS:
            delta_w[n], new_m[n], new_v[n] = _adamw(weights[n], grad_w[n], given["m_" + n], given["v_" + n])
    return (loss, grad_x, *[grad_w[n] for n in TWIN_WEIGHTS], *[delta_w[n] for n in TWIN_WEIGHTS],
            *[new_m[n] for n in TWIN_WEIGHTS], *[new_v[n] for n in TWIN_WEIGHTS])
```

```python
import functools

import numpy as np
import jax
import jax.numpy as jnp
from jax import lax
from jax.experimental import pallas as pl
from jax.experimental.pallas import tpu as pltpu

F32 = jnp.float32
BF = jnp.bfloat16
MESH = pl.DeviceIdType.MESH

N_DEV = 8
D = 1024
D_IN = 3072
D_FF = 2816
FF_BLK = D_FF // 4
CH = 64
BLK = 128
NH = 4
HD = 128
EPS = 1e-6
LEVELS = (32, 16, 8, 4, 2, 1)
N_TAB = 2 + 2 * len(LEVELS)

ADAM_LR = 0.001
ADAM_B1 = 0.9
ADAM_B2 = 0.999
ADAM_EPS = 1e-08
ADAM_WD = 0.01
ADAM_STEP = 10

VMEM_LIMIT = 56 * 1024 * 1024


def _params(**kw):
    return pltpu.CompilerParams(vmem_limit_bytes=VMEM_LIMIT, **kw)


def _whole_vmem():
    return pl.BlockSpec(memory_space=pltpu.VMEM)


def _any():
    return pl.BlockSpec(memory_space=pl.ANY)


def _exchange(arrays, name, scatter, in_vmem):
    n = len(arrays)

    def body(*refs):
        ins = refs[:n]
        outs = refs[n:2 * n]
        send_sems, recv_sems, local_sems = refs[2 * n:]
        x, y, c = lax.axis_index("x"), lax.axis_index("y"), lax.axis_index("c")
        me = 4 * x + 2 * y + c

        def flip(v, bit):
            return 1 - v if bit else v

        def src(a, idx):
            return ins[a].at[idx] if scatter else ins[a]

        local = [pltpu.make_async_copy(src(a, me), outs[a].at[me], local_sems.at[a]) for a in range(n)]
        for cp in local:
            cp.start()
        for k in range(1, N_DEV):
            peer = (flip(x, (k >> 2) & 1), flip(y, (k >> 1) & 1), flip(c, k & 1))
            peer_idx = 4 * peer[0] + 2 * peer[1] + peer[2]
            for a in range(n):
                pltpu.make_async_remote_copy(
                    src_ref=src(a, peer_idx), dst_ref=outs[a].at[me],
                    send_sem=send_sems.at[a, k - 1], recv_sem=recv_sems.at[a, k - 1],
                    device_id=peer, device_id_type=MESH).start()
        for k in range(1, N_DEV):
            peer = (flip(x, (k >> 2) & 1), flip(y, (k >> 1) & 1), flip(c, k & 1))
            peer_idx = 4 * peer[0] + 2 * peer[1] + peer[2]
            for a in range(n):
                pltpu.make_async_remote_copy(
                    src_ref=src(a, peer_idx), dst_ref=outs[a].at[peer_idx],
                    send_sem=send_sems.at[a, k - 1], recv_sem=recv_sems.at[a, k - 1],
                    device_id=peer, device_id_type=MESH).wait()
        for cp in local:
            cp.wait()

    spec = _whole_vmem if in_vmem else _any
    out_shape = []
    for arr in arrays:
        shp = arr.shape if scatter else (N_DEV,) + arr.shape
        out_shape.append(jax.ShapeDtypeStruct(shp, arr.dtype))
    return pl.pallas_call(
        body, name=name,
        out_shape=tuple(out_shape),
        in_specs=[spec() for _ in arrays],
        out_specs=tuple(spec() for _ in arrays),
        scratch_shapes=[pltpu.SemaphoreType.DMA((n, N_DEV - 1)), pltpu.SemaphoreType.DMA((n, N_DEV - 1)),
                        pltpu.SemaphoreType.DMA((n,))],
        compiler_params=_params(has_side_effects=True),
    )(*arrays)


def _cast_bf16(arrays, name):
    n = len(arrays)

    def body(*refs):
        for i in range(n):
            refs[n + i][...] = refs[i][...].astype(BF)

    return pl.pallas_call(
        body, name=name,
        out_shape=tuple(jax.ShapeDtypeStruct(a.shape, BF) for a in arrays),
        in_specs=[_whole_vmem() for _ in arrays],
        out_specs=tuple(_whole_vmem() for _ in arrays),
        compiler_params=_params(),
    )(*arrays)


def _restack_columns(stacked, name):
    _, r, c = stacked.shape

    def body(in_ref, out_ref, sems):
        cps = [pltpu.make_async_copy(in_ref.at[d], out_ref.at[:, pl.ds(d * c, c)], sems.at[d]) for d in range(N_DEV)]
        for cp in cps:
            cp.start()
        for cp in cps:
            cp.wait()

    return pl.pallas_call(
        body, name=name,
        out_shape=jax.ShapeDtypeStruct((r, N_DEV * c), stacked.dtype),
        in_specs=[_any()], out_specs=_any(),
        scratch_shapes=[pltpu.SemaphoreType.DMA((N_DEV,))],
        compiler_params=_params(),
    )(stacked)


def _silu(v):
    return v * jax.nn.sigmoid(v)


def _gelu(v):
    return 0.5 * v * (1.0 + lax.erf(v * 0.7071067811865476))


def _nt(a, b):
    return lax.dot_general(a, b, (((1,), (1,)), ((), ())), preferred_element_type=F32)


def _tn(a, b):
    return lax.dot_general(a, b, (((0,), (0,)), ((), ())), preferred_element_type=F32)


def _mm(a, b):
    return jnp.dot(a, b, preferred_element_type=F32)


def _ada_block(c_all, w_ada, b_cols):
    def body(c_ref, w_ref, b_ref, ada_ref, cact_ref):
        ca = _silu(c_ref[...])
        cact_ref[...] = ca
        ada_ref[...] = _mm(ca.astype(BF), w_ref[...].astype(BF)) + b_ref[...]

    return pl.pallas_call(
        body, name="ada_block",
        out_shape=(jax.ShapeDtypeStruct((N_DEV, w_ada.shape[1]), F32), jax.ShapeDtypeStruct(c_all.shape, F32)),
        in_specs=[_whole_vmem()] * 3, out_specs=(_whole_vmem(), _whole_vmem()),
        compiler_params=_params(),
    )(c_all, w_ada, b_cols)


def _decay_tables():
    t = np.arange(CH)
    j = np.arange(CH)
    mats = [(j[None, :] <= t[:, None]), (j[None, :] > t[:, None])]
    eq, ek, masks = [], [], []
    for h in LEVELS:
        m = (t // (2 * h)) * (2 * h) + h
        upper = t >= m
        eq.append((j[None, :] >= m[:, None]) & (j[None, :] <= t[:, None]) & upper[:, None])
        ek.append((j[None, :] > t[:, None]) & (j[None, :] <= (m[:, None] - 1)) & (~upper)[:, None])
        same = (t[:, None] // (2 * h)) == (t[None, :] // (2 * h))
        masks.append(same & upper[:, None] & (~upper)[None, :])
    masks.append(np.eye(CH, dtype=bool))
    m_all = np.concatenate(mats + eq + ek, axis=0).astype(np.float32)
    cid = np.arange(BLK) // CH
    gmask = (cid[:, None] >= cid[None, :]).astype(np.float32)
    return jnp.asarray(m_all, BF), jnp.asarray(np.stack(masks).astype(np.float32)), jnp.asarray(gmask)


def _split3(v):
    v1 = v.astype(BF)
    r1 = v - v1.astype(F32)
    v2 = r1.astype(BF)
    v3 = (r1 - v2.astype(F32)).astype(BF)
    return v1, v2, v3


@jax.custom_vjp
def _exact_mm(m, v):
    p1, p2, p3 = _split3(v)
    return _mm(m, p1) + _mm(m, p2) + _mm(m, p3)


def _exact_mm_fwd(m, v):
    return _exact_mm(m, v), m


def _exact_mm_bwd(m, g):
    p1, p2, p3 = _split3(g)
    return jnp.zeros_like(m), _tn(m, p1) + _tn(m, p2) + _tn(m, p3)


_exact_mm.defvjp(_exact_mm_fwd, _exact_mm_bwd)


def _mix_tile(proj, state, w_s, b_s_t, ln_w, ln_b, lower, gn_w, m_all, masks, gmask):
    u = proj[:, 0:512]
    v = proj[:, 512:1024]
    q = proj[:, 1024:1536]
    fl = proj[:, 1536:2048]
    inp = proj[:, 2048:2560]
    g = proj[:, 2560:3072]

    ug = _gelu(u)
    vg = _gelu(v)
    mu = jnp.mean(vg, axis=-1, keepdims=True)
    vc = vg - mu
    var = jnp.mean(vc * vc, axis=-1, keepdims=True)
    vnb = (vc * lax.rsqrt(var + EPS) * ln_w + ln_b).astype(BF)
    ya = []
    for h in range(NH):
        wsm = (w_s[h] * gmask).astype(BF)
        mixed = _mm(wsm, vnb[:, h * HD:(h + 1) * HD]) + b_s_t[:, h:h + 1]
        ya.append(ug[:, h * HD:(h + 1) * HD] * mixed)

    l0 = lower[0:1, :]
    l1 = lower[1:2, :]
    mx = jnp.maximum(l0, l1)
    e0 = jnp.exp(l0 - mx)
    e1 = jnp.exp(l1 - mx)
    lb = e0 / (e0 + e1)
    qf = _silu(q)
    f = lb + (1.0 - lb) * jax.nn.sigmoid(fl)
    logf = jnp.log(f)
    kk = 1.0 - f
    gate = _silu(g)
    nl = len(LEVELS)
    st = [state[h] for h in range(NH)]
    yb = [[None] * NH for _ in range(BLK // CH)]
    for ci in range(BLK // CH):
        rows = slice(ci * CH, (ci + 1) * CH)
        ex = jnp.exp(_exact_mm(m_all, logf[rows]))
        for h in range(NH):
            cols = slice(h * HD, (h + 1) * HD)
            qh = qf[rows, cols]
            kh = kk[rows, cols]
            vh = inp[rows, cols].astype(BF)
            eb = ex[0:CH, cols]
            einv = ex[CH:2 * CH, cols]
            inter = _nt((qh * eb).astype(BF), st[h].astype(BF))
            attn = masks[nl] * _nt(qh.astype(BF), kh.astype(BF))
            for li in range(nl):
                eq = ex[(2 + li) * CH:(3 + li) * CH, cols]
                ek = ex[(2 + nl + li) * CH:(3 + nl + li) * CH, cols]
                attn = attn + masks[li] * _nt((qh * eq).astype(BF), (kh * ek).astype(BF))
            o = inter + _mm(attn.astype(BF), vh)
            st[h] = st[h] * eb[CH - 1:CH, :] + _tn(vh, (kh * einv).astype(BF))
            rs = lax.rsqrt(jnp.mean(o * o, axis=-1, keepdims=True) + EPS)
            yb[ci][h] = o * rs * gn_w * gate[rows, cols]
    yb_full = jnp.concatenate([jnp.concatenate(r, axis=1) for r in yb], axis=0)
    return jnp.concatenate(ya + [yb_full], axis=1), jnp.stack(st)


def _fwd_mix(proj, w_s, b_s_t, ln_w, ln_b, lower, gn_w, tables):
    t = proj.shape[0]
    nt_ = t // BLK
    m_all, masks, gmask = tables

    def body(proj_ref, ws_ref, bs_ref, lw_ref, lb_ref, lo_ref, gn_ref, m_ref, mk_ref, gm_ref, y_ref, st_ref, state):
        @pl.when(pl.program_id(0) == 0)
        def _():
            state[...] = jnp.zeros_like(state)

        st = state[...]
        st_ref[...] = st
        y, new = _mix_tile(proj_ref[...], st, ws_ref[...], bs_ref[...], lw_ref[...], lb_ref[...], lo_ref[...], gn_ref[...],
                           m_ref[...], mk_ref[...], gm_ref[...])
        y_ref[...] = y.astype(BF)
        state[...] = new

    full = lambda a: pl.BlockSpec(a.shape, lambda i, nd=a.ndim: (0,) * nd)
    return pl.pallas_call(
        body, name="fwd_mix", grid=(nt_,),
        out_shape=(jax.ShapeDtypeStruct((t, D), BF), jax.ShapeDtypeStruct((nt_, NH, HD, HD), F32)),
        in_specs=[pl.BlockSpec((BLK, D_IN), lambda i: (i, 0)), full(w_s), full(b_s_t), full(ln_w), full(ln_b), full(lower),
                  full(gn_w), full(m_all), full(masks), full(gmask)],
        out_specs=(pl.BlockSpec((BLK, D), lambda i: (i, 0)), pl.BlockSpec((None, NH, HD, HD), lambda i: (i, 0, 0, 0))),
        scratch_shapes=[pltpu.VMEM((NH, HD, HD), F32)],
        compiler_params=_params(dimension_semantics=("arbitrary",)),
    )(proj, w_s, b_s_t, ln_w, ln_b, lower, gn_w, m_all, masks, gmask)


def _bwd_mix(proj, dycat, states, w_s, b_s_t, ln_w, ln_b, lower, gn_w, tables):
    t = proj.shape[0]
    nt_ = t // BLK
    m_all, masks, gmask = tables

    def body(proj_ref, dy_ref, st_ref, ws_ref, bs_ref, lw_ref, lb_ref, lo_ref, gn_ref, m_ref, mk_ref, gm_ref,
             dproj_ref, dws_ref, dbs_ref, dlw_ref, dlb_ref, dlo_ref, dgn_ref, dstate):
        acc = (dws_ref, dbs_ref, dlw_ref, dlb_ref, dlo_ref, dgn_ref)

        @pl.when(pl.program_id(0) == 0)
        def _():
            dstate[...] = jnp.zeros_like(dstate)
            for r in acc:
                r[...] = jnp.zeros_like(r)

        def f(p, s, ws, bs, lw, lb_, lo, gn):
            return _mix_tile(p, s, ws, bs, lw, lb_, lo, gn, m_ref[...], mk_ref[...], gm_ref[...])

        _, vjp = jax.vjp(f, proj_ref[...], st_ref[...], ws_ref[...], bs_ref[...], lw_ref[...], lb_ref[...], lo_ref[...],
                         gn_ref[...])
        grads = vjp((dy_ref[...], dstate[...]))
        dproj_ref[...] = grads[0].astype(BF)
        dstate[...] = grads[1]
        for r, gval in zip(acc, grads[2:]):
            r[...] += gval

    full = lambda a: pl.BlockSpec(a.shape, lambda i, nd=a.ndim: (0,) * nd)
    rev = lambda i: (nt_ - 1 - i, 0)
    smalls = (w_s, b_s_t, ln_w, ln_b, lower, gn_w)
    return pl.pallas_call(
        body, name="bwd_mix", grid=(nt_,),
        out_shape=(jax.ShapeDtypeStruct((t, D_IN), BF),) + tuple(jax.ShapeDtypeStruct(a.shape, F32) for a in smalls),
        in_specs=[pl.BlockSpec((BLK, D_IN), rev), pl.BlockSpec((BLK, D), rev),
                  pl.BlockSpec((None, NH, HD, HD), lambda i: (nt_ - 1 - i, 0, 0, 0))]
        + [full(a) for a in smalls] + [full(m_all), full(masks), full(gmask)],
        out_specs=(pl.BlockSpec((BLK, D_IN), rev),) + tuple(full(a) for a in smalls),
        scratch_shapes=[pltpu.VMEM((NH, HD, HD), F32)],
        compiler_params=_params(dimension_semantics=("arbitrary",)),
    )(proj, dycat, states, w_s, b_s_t, ln_w, ln_b, lower, gn_w, m_all, masks, gmask)


def _row_tile(t, want):
    return want if t % want == 0 else t


def _rms(v):
    rstd = lax.rsqrt(jnp.mean(v * v, axis=-1, keepdims=True) + EPS)
    return v * rstd, rstd


def _rms_bwd(dxhat, xhat, rstd):
    return rstd * (dxhat - xhat * jnp.mean(dxhat * xhat, axis=-1, keepdims=True))


def _colsum(v):
    return jnp.sum(v, axis=0, keepdims=True)


def _fwd_in(x, ada, n1w, w_in):
    t = x.shape[0]
    tm = _row_tile(t, 256)

    def body(x_ref, ada_ref, n1_ref, w_ref, proj_ref, h1_ref):
        xh, _ = _rms(x_ref[...])
        h1 = (xh * n1_ref[...]) * (1.0 + ada_ref[1:2, :]) + ada_ref[0:1, :]
        h1b = h1.astype(BF)
        h1_ref[...] = h1b
        proj_ref[...] = _mm(h1b, w_ref[...])

    full = lambda a: pl.BlockSpec(a.shape, lambda i, nd=a.ndim: (0,) * nd)
    return pl.pallas_call(
        body, name="fwd_in", grid=(t // tm,),
        out_shape=(jax.ShapeDtypeStruct((t, D_IN), F32), jax.ShapeDtypeStruct((t, D), BF)),
        in_specs=[pl.BlockSpec((tm, D), lambda i: (i, 0)), full(ada), full(n1w), _whole_vmem()],
        out_specs=(pl.BlockSpec((tm, D_IN), lambda i: (i, 0)), pl.BlockSpec((tm, D), lambda i: (i, 0))),
        compiler_params=_params(dimension_semantics=("arbitrary",)),
    )(x, ada, n1w, w_in)


def _fwd_ffn(x, ycat, tgt, ada, n2w, fw, w_out, w_fi, w_fo):
    t = x.shape[0]
    tm = _row_tile(t, 256)

    def body(x_ref, y_ref, t_ref, ada_ref, n2_ref, fw_ref, wo_ref, wi_ref, wf_ref,
             x1_ref, h2_ref, act_ref, dffn_ref, dx2_ref, part_ref):
        @pl.when(pl.program_id(0) == 0)
        def _():
            part_ref[...] = jnp.zeros_like(part_ref)

        g1, sh2, sc2, g2 = ada_ref[2:3, :], ada_ref[3:4, :], ada_ref[4:5, :], ada_ref[5:6, :]
        x1 = x_ref[...] + g1 * _mm(y_ref[...], wo_ref[...])
        x1_ref[...] = x1
        xh2, _ = _rms(x1)
        h2b = ((xh2 * n2_ref[...]) * (1.0 + sc2) + sh2).astype(BF)
        h2_ref[...] = h2b
        ffn = jnp.zeros((tm, D), F32)
        for j in range(4):
            gate = _mm(h2b, wi_ref[j])
            up = _mm(h2b, wi_ref[j + 4])
            a = (_silu(gate) * up).astype(BF)
            act_ref[j] = a
            ffn = ffn + _mm(a, wf_ref[j * FF_BLK:(j + 1) * FF_BLK, :])
        x2 = x1 + g2 * ffn
        xh3, rstd3 = _rms(x2)
        err = xh3 * fw_ref[...] - t_ref[...]
        dy = err * (1.0 / D)
        dx2 = _rms_bwd(dy * fw_ref[...], xh3, rstd3)
        dx2_ref[...] = dx2
        dffn_ref[...] = (g2 * dx2).astype(BF)
        part_ref[0:1, :] += _colsum(dx2 * ffn)
        part_ref[1:2, :] += _colsum(dy * xh3)
        part_ref[2:3, :] += jnp.zeros((1, D), F32) + (0.5 / D) * jnp.sum(err * err)

    full = lambda a: pl.BlockSpec(a.shape, lambda i, nd=a.ndim: (0,) * nd)
    row = lambda w: pl.BlockSpec((tm, w), lambda i: (i, 0))
    return pl.pallas_call(
        body, name="fwd_ffn", grid=(t // tm,),
        out_shape=(jax.ShapeDtypeStruct((t, D), F32), jax.ShapeDtypeStruct((t, D), BF),
                   jax.ShapeDtypeStruct((4, t, FF_BLK), BF), jax.ShapeDtypeStruct((t, D), BF),
                   jax.ShapeDtypeStruct((t, D), F32), jax.ShapeDtypeStruct((8, D), F32)),
        in_specs=[row(D), row(D), row(D), full(ada), full(n2w), full(fw), _whole_vmem(), _whole_vmem(), _whole_vmem()],
        out_specs=(row(D), row(D), pl.BlockSpec((4, tm, FF_BLK), lambda i: (0, i, 0)), row(D), row(D),
                   pl.BlockSpec((8, D), lambda i: (0, 0))),
        compiler_params=_params(dimension_semantics=("arbitrary",)),
    )(x, ycat, tgt, ada, n2w, fw, w_out, w_fi, w_fo)


def _bwd_ffn(x1, h2, dffn, dx2, ycat, ada, n2w, w_out, w_fi, w_fo):
    t = x1.shape[0]
    tm = _row_tile(t, 256)

    def body(x1_ref, h2_ref, dffn_ref, dx2_ref, y_ref, ada_ref, n2_ref, wo_ref, wi_ref, wf_ref,
             dgu_ref, dx1_ref, dmix_ref, dycat_ref, part_ref):
        @pl.when(pl.program_id(0) == 0)
        def _():
            part_ref[...] = jnp.zeros_like(part_ref)

        g1, sc2 = ada_ref[2:3, :], ada_ref[4:5, :]
        h2b = h2_ref[...]
        dffn = dffn_ref[...]
        dh2 = jnp.zeros((tm, D), F32)
        for j in range(4):
            gate = _mm(h2b, wi_ref[j])
            up = _mm(h2b, wi_ref[j + 4])
            dact = _nt(dffn, wf_ref[j * FF_BLK:(j + 1) * FF_BLK, :])
            sg = jax.nn.sigmoid(gate)
            dgate = (dact * up * (sg * (1.0 + gate * (1.0 - sg)))).astype(BF)
            dup = (dact * (gate * sg)).astype(BF)
            dgu_ref[j] = dgate
            dgu_ref[j + 4] = dup
            dh2 = dh2 + _nt(dgate, wi_ref[j]) + _nt(dup, wi_ref[j + 4])
        x1 = x1_ref[...]
        xh2, rstd2 = _rms(x1)
        xn2 = xh2 * n2_ref[...]
        dxn2 = dh2 * (1.0 + sc2)
        dx1 = dx2_ref[...] + _rms_bwd(dxn2 * n2_ref[...], xh2, rstd2)
        dx1_ref[...] = dx1
        mix = _mm(y_ref[...], wo_ref[...])
        dmix = (g1 * dx1).astype(BF)
        dmix_ref[...] = dmix
        dycat_ref[...] = _nt(dmix, wo_ref[...])
        part_ref[0:1, :] += _colsum(dh2)
        part_ref[1:2, :] += _colsum(dh2 * xn2)
        part_ref[2:3, :] += _colsum(dxn2 * xh2)
        part_ref[3:4, :] += _colsum(dx1 * mix)

    full = lambda a: pl.BlockSpec(a.shape, lambda i, nd=a.ndim: (0,) * nd)
    row = lambda w: pl.BlockSpec((tm, w), lambda i: (i, 0))
    return pl.pallas_call(
        body, name="bwd_ffn", grid=(t // tm,),
        out_shape=(jax.ShapeDtypeStruct((N_DEV, t, FF_BLK), BF), jax.ShapeDtypeStruct((t, D), F32),
                   jax.ShapeDtypeStruct((t, D), BF), jax.ShapeDtypeStruct((t, D), F32), jax.ShapeDtypeStruct((8, D), F32)),
        in_specs=[row(D), row(D), row(D), row(D), row(D), full(ada), full(n2w), _whole_vmem(), _whole_vmem(), _whole_vmem()],
        out_specs=(pl.BlockSpec((N_DEV, tm, FF_BLK), lambda i: (0, i, 0)), row(D), row(D), row(D),
                   pl.BlockSpec((8, D), lambda i: (0, 0))),
        compiler_params=_params(dimension_semantics=("arbitrary",)),
    )(x1, h2, dffn, dx2, ycat, ada, n2w, w_out, w_fi, w_fo)


def _bwd_in(x, dproj, dx1, ada, n1w, w_in):
    t = x.shape[0]
    tm = _row_tile(t, 256)

    def body(x_ref, dp_ref, dx1_ref, ada_ref, n1_ref, w_ref, gx_ref, part_ref):
        @pl.when(pl.program_id(0) == 0)
        def _():
            part_ref[...] = jnp.zeros_like(part_ref)

        dh1 = _nt(dp_ref[...], w_ref[...])
        xh, rstd = _rms(x_ref[...])
        xn = xh * n1_ref[...]
        dxn = dh1 * (1.0 + ada_ref[1:2, :])
        gx_ref[...] = dx1_ref[...] + _rms_bwd(dxn * n1_ref[...], xh, rstd)
        part_ref[0:1, :] += _colsum(dh1)
        part_ref[1:2, :] += _colsum(dh1 * xn)
        part_ref[2:3, :] += _colsum(dxn * xh)

    full = lambda a: pl.BlockSpec(a.shape, lambda i, nd=a.ndim: (0,) * nd)
    row = lambda w: pl.BlockSpec((tm, w), lambda i: (i, 0))
    return pl.pallas_call(
        body, name="bwd_in", grid=(t // tm,),
        out_shape=(jax.ShapeDtypeStruct((t, D), F32), jax.ShapeDtypeStruct((8, D), F32)),
        in_specs=[row(D), row(D_IN), row(D), full(ada), full(n1w), _whole_vmem()],
        out_specs=(row(D), pl.BlockSpec((8, D), lambda i: (0, 0))),
        compiler_params=_params(dimension_semantics=("arbitrary",)),
    )(x, dproj, dx1, ada, n1w, w_in)


def _wgrad(a, b, name, a_spec, b_spec, out_shape, out_spec, grid, acc_shape):
    n_t = grid[1]

    def body(a_ref, b_ref, o_ref, acc):
        @pl.when(pl.program_id(1) == 0)
        def _():
            acc[...] = jnp.zeros_like(acc)

        acc[...] += _tn(a_ref[...], b_ref[...])

        @pl.when(pl.program_id(1) == n_t - 1)
        def _():
            o_ref[...] = acc[...].astype(BF)

    return pl.pallas_call(
        body, name=name, grid=grid,
        out_shape=jax.ShapeDtypeStruct(out_shape, BF),
        in_specs=[a_spec, b_spec], out_specs=out_spec,
        scratch_shapes=[pltpu.VMEM(acc_shape, F32)],
        compiler_params=_params(dimension_semantics=("arbitrary", "arbitrary")),
    )(a, b)


def _adamw_math(w, g, m, v):
    m = ADAM_B1 * m + (1.0 - ADAM_B1) * g
    v = ADAM_B2 * v + (1.0 - ADAM_B2) * (g * g)
    m_hat = m / (1.0 - ADAM_B1 ** ADAM_STEP)
    v_hat = v / (1.0 - ADAM_B2 ** ADAM_STEP)
    delta = -ADAM_LR * (m_hat / (jnp.sqrt(v_hat) + ADAM_EPS) + ADAM_WD * w)
    return delta, m, v


def _adamw_recv(w, m, v, recv, name, tr):
    r, c = w.shape

    def body(w_ref, m_ref, v_ref, r_ref, g_ref, d_ref, nm_ref, nv_ref):
        g = r_ref[0].astype(F32)
        for k in range(1, N_DEV):
            g = g + r_ref[k].astype(F32)
        g_ref[...] = g
        d_ref[...], nm_ref[...], nv_ref[...] = _adamw_math(w_ref[...], g, m_ref[...], v_ref[...])

    row = pl.BlockSpec((tr, c), lambda i: (i, 0))
    return pl.pallas_call(
        body, name=name, grid=(r // tr,),
        out_shape=tuple(jax.ShapeDtypeStruct((r, c), F32) for _ in range(4)),
        in_specs=[row, row, row, pl.BlockSpec((N_DEV, tr, c), lambda i: (0, i, 0))],
        out_specs=(row, row, row, row),
        compiler_params=_params(dimension_semantics=("arbitrary",)),
    )(w, m, v, recv)


def _adamw_ada(w, m, v, cact, dada_cols):
    r, c = w.shape
    tr = 256

    def body(w_ref, m_ref, v_ref, ca_ref, da_ref, g_ref, d_ref, nm_ref, nv_ref):
        g = _tn(ca_ref[...].astype(BF), da_ref[...].astype(BF))
        g_ref[...] = g
        d_ref[...], nm_ref[...], nv_ref[...] = _adamw_math(w_ref[...], g, m_ref[...], v_ref[...])

    row = pl.BlockSpec((tr, c), lambda i: (i, 0))
    return pl.pallas_call(
        body, name="adamw_ada", grid=(r // tr,),
        out_shape=tuple(jax.ShapeDtypeStruct((r, c), F32) for _ in range(4)),
        in_specs=[row, row, row, pl.BlockSpec((N_DEV, tr), lambda i: (0, i)), pl.BlockSpec(dada_cols.shape, lambda i: (0, 0))],
        out_specs=(row, row, row, row),
        compiler_params=_params(dimension_semantics=("arbitrary",)),
    )(w, m, v, cact, dada_cols)


def _adamw_small(gathered, wmv):
    n_g = len(gathered)
    n_p = len(wmv)
    flat = [a for trip in wmv for a in trip]

    def body(*refs):
        g_refs = refs[:n_g]
        p_refs = refs[n_g:n_g + 3 * n_p]
        o_refs = refs[n_g + 3 * n_p:]

        def total(ref):
            s = ref[0]
            for k in range(1, N_DEV):
                s = s + ref[k]
            return s

        f3, b3, b1, dws, dbs, dlnw, dlnb, dlo, dgn = [total(r) for r in g_refs]
        dada_rows = [b1[0:1], b1[1:2], b3[3:4], b3[0:1], b3[1:2], f3[0:1]]
        for r, g in enumerate(dada_rows):
            w, m, v = p_refs[0][r:r + 1, :], p_refs[1][r:r + 1, :], p_refs[2][r:r + 1, :]
            o_refs[0][r:r + 1, :] = g
            o_refs[1][r:r + 1, :], o_refs[2][r:r + 1, :], o_refs[3][r:r + 1, :] = _adamw_math(w, g, m, v)
        grads = [None, b1[2:3], dws, dbs, dlnw, dlnb, dlo, dgn, b3[2:3], f3[1:2]]
        for i, g in enumerate(grads):
            if g is None:
                continue
            w, m, v = p_refs[3 * i][...], p_refs[3 * i + 1][...], p_refs[3 * i + 2][...]
            o_refs[4 * i][...] = g
            o_refs[4 * i + 1][...], o_refs[4 * i + 2][...], o_refs[4 * i + 3][...] = _adamw_math(w, g, m, v)
        o_refs[4 * n_p][...] = jnp.zeros((8, 128), F32) + f3[2:3, 0:128]

    out_shape = []
    for w, _, _ in wmv:
        out_shape += [jax.ShapeDtypeStruct(w.shape, F32)] * 4
    out_shape.append(jax.ShapeDtypeStruct((8, 128), F32))
    n_in = n_g + 3 * n_p
    return pl.pallas_call(
        body, name="adamw_small",
        out_shape=tuple(out_shape),
        in_specs=[_whole_vmem()] * n_in, out_specs=tuple(_whole_vmem() for _ in out_shape),
        compiler_params=_params(),
    )(*gathered, *flat)


def kernel(x, c, w_ada, b_ada, norm1_w, w_in, w_s, b_s, v_ln_w, v_ln_b, lower_bounds, gn_w, w_out, norm2_w, w_ffn_in, w_ffn_out, final_norm_w, loss_target, m_w_ada, m_b_ada, m_norm1_w, m_w_in, m_w_s, m_b_s, m_v_ln_w, m_v_ln_b, m_lower_bounds, m_gn_w, m_w_out, m_norm2_w, m_w_ffn_in, m_w_ffn_out, m_final_norm_w, v_w_ada, v_b_ada, v_norm1_w, v_w_in, v_w_s, v_b_s, v_v_ln_w, v_v_ln_b, v_lower_bounds, v_gn_w, v_w_out, v_norm2_w, v_w_ffn_in, v_w_ffn_out, v_final_norm_w):
    me = 4 * lax.axis_index("x") + 2 * lax.axis_index("y") + lax.axis_index("c")
    t = x.shape[1]
    x2d = x.reshape(t, D)
    tgt = loss_target.reshape(t, D)
    ada_cols = w_ada.shape[2]

    win_b, wout_b, wfi_b, wfo_b = _cast_bf16([w_in[0], w_out[0], w_ffn_in[0], w_ffn_out[0]], "cast_weights")
    win_st, wout_st, wfi_st, wfo_st, c_st = _exchange([win_b, wout_b, wfi_b, wfo_b, c], "gather_weights", False, False)
    w_in_full = _restack_columns(win_st, "restack_w_in")
    w_out_full = wout_st.reshape(D, D)
    w_fo_full = wfo_st.reshape(D_FF, D)
    c_all = c_st.reshape(N_DEV, D)

    b_cols = lax.dynamic_slice(b_ada, (0, me * ada_cols), (1, ada_cols))
    ada_blk, cact = _ada_block(c_all, w_ada[0], b_cols)
    (ada_st,) = _exchange([ada_blk], "gather_ada", False, True)
    ada = lax.dynamic_index_in_dim(ada_st, me, axis=1, keepdims=False).reshape(6, D)

    tables = _decay_tables()
    ws3 = w_s[0]
    bs_t = b_s[0].T

    proj, h1 = _fwd_in(x2d, ada, norm1_w, w_in_full)
    ycat, states = _fwd_mix(proj, ws3, bs_t, v_ln_w, v_ln_b, lower_bounds, gn_w, tables)
    x1, h2, act, dffn, dx2, part_f = _fwd_ffn(x2d, ycat, tgt, ada, norm2_w, final_norm_w.reshape(1, D),
                                              w_out_full, wfi_st, w_fo_full)
    dgu, dx1, dmix, dycat, part_b3 = _bwd_ffn(x1, h2, dffn, dx2, ycat, ada, norm2_w, w_out_full, wfi_st, w_fo_full)
    dproj, dws, dbs_t, dlnw, dlnb, dlower, dgnw = _bwd_mix(proj, dycat, states, ws3, bs_t, v_ln_w, v_ln_b, lower_bounds,
                                                         gn_w, tables)
    grad_x, part_b1 = _bwd_in(x2d, dproj, dx1, ada, norm1_w, w_in_full)

    tk = _row_tile(t, 512)
    n_t = t // tk
    win_cols = D_IN // N_DEV
    dwin = _wgrad(h1, dproj, "wgrad_in",
                  pl.BlockSpec((tk, D), lambda j, i: (i, 0)), pl.BlockSpec((tk, win_cols), lambda j, i: (i, j)),
                  (N_DEV, D, win_cols), pl.BlockSpec((None, D, win_cols), lambda j, i: (j, 0, 0)), (N_DEV, n_t), (D, win_cols))
    dwout = _wgrad(ycat, dmix, "wgrad_out",
                   pl.BlockSpec((tk, D), lambda j, i: (i, 0)), pl.BlockSpec((tk, D), lambda j, i: (i, 0)),
                   (D, D), pl.BlockSpec((D, D), lambda j, i: (0, 0)), (1, n_t), (D, D))
    dwfi = _wgrad(h2, dgu, "wgrad_ffn_in",
                  pl.BlockSpec((tk, D), lambda j, i: (i, 0)), pl.BlockSpec((None, tk, FF_BLK), lambda j, i: (j, i, 0)),
                  (N_DEV, D, FF_BLK), pl.BlockSpec((None, D, FF_BLK), lambda j, i: (j, 0, 0)), (N_DEV, n_t), (D, FF_BLK))
    dwfo = _wgrad(act, dffn, "wgrad_ffn_out",
                  pl.BlockSpec((None, tk, FF_BLK), lambda j, i: (j, i, 0)), pl.BlockSpec((tk, D), lambda j, i: (i, 0)),
                  (4, FF_BLK, D), pl.BlockSpec((None, FF_BLK, D), lambda j, i: (j, 0, 0)), (4, n_t), (FF_BLK, D))
    r_in, r_out, r_fi, r_fo = _exchange(
        [dwin, dwout.reshape(N_DEV, D // N_DEV, D), dwfi, dwfo.reshape(N_DEV, D_FF // N_DEV, D)],
        "scatter_wgrads", True, False)

    g_w_in, d_w_in, nm_w_in, nv_w_in = _adamw_recv(w_in[0], m_w_in[0], v_w_in[0], r_in, "adamw_w_in", 256)
    g_w_out, d_w_out, nm_w_out, nv_w_out = _adamw_recv(w_out[0], m_w_out[0], v_w_out[0], r_out, "adamw_w_out", 128)
    g_w_fi, d_w_fi, nm_w_fi, nv_w_fi = _adamw_recv(w_ffn_in[0], m_w_ffn_in[0], v_w_ffn_in[0], r_fi, "adamw_w_ffn_in", 256)
    g_w_fo, d_w_fo, nm_w_fo, nv_w_fo = _adamw_recv(w_ffn_out[0], m_w_ffn_out[0], v_w_ffn_out[0], r_fo, "adamw_w_ffn_out", 176)

    gathered = _exchange([part_f, part_b3, part_b1, dws, dbs_t, dlnw, dlnb, dlower, dgnw], "gather_small", False, True)
    f3_all, b3_all, b1_all = gathered[0], gathered[1], gathered[2]
    dada_all = jnp.stack([b1_all[:, 0], b1_all[:, 1], b3_all[:, 3], b3_all[:, 0], b3_all[:, 1], f3_all[:, 0]], axis=1)
    dada_cols = lax.dynamic_slice(dada_all.reshape(N_DEV, 6 * D), (0, me * ada_cols), (N_DEV, ada_cols))
    g_w_ada, d_w_ada, nm_w_ada, nv_w_ada = _adamw_ada(w_ada[0], m_w_ada[0], v_w_ada[0], cact, dada_cols)

    r6 = lambda a: a.reshape(6, D)
    r1 = lambda a: a.reshape(1, D)
    tr = lambda a: a[0].T
    wmv = [
        (r6(b_ada), r6(m_b_ada), r6(v_b_ada)),
        (norm1_w, m_norm1_w, v_norm1_w),
        (w_s[0], m_w_s[0], v_w_s[0]),
        (tr(b_s), tr(m_b_s), tr(v_b_s)),
        (v_ln_w, m_v_ln_w, v_v_ln_w),
        (v_ln_b, m_v_ln_b, v_v_ln_b),
        (lower_bounds, m_lower_bounds, v_lower_bounds),
        (gn_w, m_gn_w, v_gn_w),
        (norm2_w, m_norm2_w, v_norm2_w),
        (r1(final_norm_w), r1(m_final_norm_w), r1(v_final_norm_w)),
    ]
    small = _adamw_small(gathered, wmv)
    loss = small[-1][0, 0]

    def unshape(i, a):
        if i == 0:
            return a.reshape(1, 6 * D)
        if i == 2:
            return a.reshape(1, NH, BLK, BLK)
        if i == 3:
            return a.T.reshape(1, NH, BLK)
        if i == 9:
            return a.reshape(D)
        return a

    def small_out(kind):
        return [unshape(i, small[4 * i + kind]) for i in range(len(wmv))]

    e3 = lambda a: a[None]
    big = {
        0: (e3(g_w_ada), e3(g_w_in), e3(g_w_out), e3(g_w_fi), e3(g_w_fo)),
        1: (e3(d_w_ada), e3(d_w_in), e3(d_w_out), e3(d_w_fi), e3(d_w_fo)),
        2: (e3(nm_w_ada), e3(nm_w_in), e3(nm_w_out), e3(nm_w_fi), e3(nm_w_fo)),
        3: (e3(nv_w_ada), e3(nv_w_in), e3(nv_w_out), e3(nv_w_fi), e3(nv_w_fo)),
    }

    def ordered(kind):
        s = small_out(kind)
        b_ = big[kind]
        return [b_[0], s[0], s[1], b_[1], s[2], s[3], s[4], s[5], s[6], s[7], b_[2], s[8], b_[3], b_[4], s[9]]

    return (loss, grad_x.reshape(1, t, D), *ordered(0), *ordered(1), *ordered(2), *ordered(3))
```

```python
import functools

import numpy as np
import jax
import jax.numpy as jnp
from jax import lax
from jax.experimental import pallas as pl
from jax.experimental.pallas import tpu as pltpu

F32 = jnp.float32
BF = jnp.bfloat16
MESH = pl.DeviceIdType.MESH

N_DEV = 8
D = 1024
D_IN = 3072
D_FF = 2816
FF_BLK = D_FF // 4
CH = 64
BLK = 128
NH = 4
HD = 128
EPS = 1e-6
LEVELS = (32, 16, 8, 4, 2, 1)
N_TAB = 2 + len(LEVELS)

ADAM_LR = 0.001
ADAM_B1 = 0.9
ADAM_B2 = 0.999
ADAM_EPS = 1e-08
ADAM_WD = 0.01
ADAM_STEP = 10

VMEM_LIMIT = 56 * 1024 * 1024


def _params(**kw):
    return pltpu.CompilerParams(vmem_limit_bytes=VMEM_LIMIT, **kw)


def _whole_vmem():
    return pl.BlockSpec(memory_space=pltpu.VMEM)


def _any():
    return pl.BlockSpec(memory_space=pl.ANY)


class _Exchange:
    def __init__(self, ins, outs, send_sems, recv_sems, local_sems, scatter):
        self.ins, self.outs, self.scatter = ins, outs, scatter
        self.send_sems, self.recv_sems, self.local_sems = send_sems, recv_sems, local_sems
        x, y, c = lax.axis_index("x"), lax.axis_index("y"), lax.axis_index("c")
        self.me = 4 * x + 2 * y + c
        self.peers = []
        for k in range(1, N_DEV):
            peer = (1 - x if (k >> 2) & 1 else x, 1 - y if (k >> 1) & 1 else y, 1 - c if k & 1 else c)
            self.peers.append((peer, 4 * peer[0] + 2 * peer[1] + peer[2]))

    def _src(self, a, idx):
        return self.ins[a].at[idx] if self.scatter else self.ins[a]

    def _local(self):
        return [pltpu.make_async_copy(self._src(a, self.me), self.outs[a].at[self.me], self.local_sems.at[a])
                for a in range(len(self.ins))]

    def _remote(self, a, k, dst_slot):
        peer, peer_idx = self.peers[k]
        return pltpu.make_async_remote_copy(
            src_ref=self._src(a, peer_idx), dst_ref=self.outs[a].at[dst_slot],
            send_sem=self.send_sems.at[a, k], recv_sem=self.recv_sems.at[a, k],
            device_id=peer, device_id_type=MESH)

    def start(self):
        for cp in self._local():
            cp.start()
        for k in range(N_DEV - 1):
            for a in range(len(self.ins)):
                self._remote(a, k, self.me).start()

    def wait(self):
        for k in range(N_DEV - 1):
            for a in range(len(self.ins)):
                self._remote(a, k, self.peers[k][1]).wait()
        for cp in self._local():
            cp.wait()


def _exchange_sems(n):
    return [pltpu.SemaphoreType.DMA((n, N_DEV - 1)), pltpu.SemaphoreType.DMA((n, N_DEV - 1)), pltpu.SemaphoreType.DMA((n,))]


def _exchange_out_shapes(arrays, scatter):
    return [jax.ShapeDtypeStruct(a.shape if scatter else (N_DEV,) + a.shape, a.dtype) for a in arrays]


def _exchange(arrays, name, scatter, in_vmem):
    n = len(arrays)

    def body(*refs):
        ex = _Exchange(refs[:n], refs[n:2 * n], *refs[2 * n:], scatter)
        ex.start()
        ex.wait()

    spec = _whole_vmem if in_vmem else _any
    return pl.pallas_call(
        body, name=name,
        out_shape=tuple(_exchange_out_shapes(arrays, scatter)),
        in_specs=[spec() for _ in arrays],
        out_specs=tuple(spec() for _ in arrays),
        scratch_shapes=_exchange_sems(n),
        compiler_params=_params(has_side_effects=True),
    )(*arrays)


def _cast_bf16(arrays, name):
    n = len(arrays)

    def body(*refs):
        for i in range(n):
            refs[n + i][...] = refs[i][...].astype(BF)

    return pl.pallas_call(
        body, name=name,
        out_shape=tuple(jax.ShapeDtypeStruct(a.shape, BF) for a in arrays),
        in_specs=[_whole_vmem() for _ in arrays],
        out_specs=tuple(_whole_vmem() for _ in arrays),
        compiler_params=_params(),
    )(*arrays)


def _load_columns(stacked_ref, full_ref, sems):
    c = stacked_ref.shape[2]
    cps = [pltpu.make_async_copy(stacked_ref.at[d], full_ref.at[:, pl.ds(d * c, c)], sems.at[d]) for d in range(N_DEV)]
    for cp in cps:
        cp.start()
    for cp in cps:
        cp.wait()


def _silu(v):
    return v * jax.nn.sigmoid(v)


def _gelu(v):
    return 0.5 * v * (1.0 + lax.erf(v * 0.7071067811865476))


def _nt(a, b):
    return lax.dot_general(a, b, (((1,), (1,)), ((), ())), preferred_element_type=F32)


def _tn(a, b):
    return lax.dot_general(a, b, (((0,), (0,)), ((), ())), preferred_element_type=F32)


def _mm(a, b):
    return jnp.dot(a, b, preferred_element_type=F32)


def _ada_block(c_all, w_ada, b_cols):
    def body(c_ref, w_ref, b_ref, ada_ref, cact_ref):
        ca = _silu(c_ref[...])
        cact_ref[...] = ca
        ada_ref[...] = _mm(ca.astype(BF), w_ref[...].astype(BF)) + b_ref[...]

    return pl.pallas_call(
        body, name="ada_block",
        out_shape=(jax.ShapeDtypeStruct((N_DEV, w_ada.shape[1]), F32), jax.ShapeDtypeStruct(c_all.shape, F32)),
        in_specs=[_whole_vmem()] * 3, out_specs=(_whole_vmem(), _whole_vmem()),
        compiler_params=_params(),
    )(c_all, w_ada, b_cols)


def _decay_tables():
    t = np.arange(CH)
    j = np.arange(CH)
    mats = [(j[None, :] <= t[:, None]), (j[None, :] > t[:, None])]
    masks = []
    for h in LEVELS:
        m = (t // (2 * h)) * (2 * h) + h
        upper = t >= m
        eq = (j[None, :] >= m[:, None]) & (j[None, :] <= t[:, None]) & upper[:, None]
        ek = (j[None, :] > t[:, None]) & (j[None, :] <= (m[:, None] - 1)) & (~upper)[:, None]
        mats.append(eq | ek)
        same = (t[:, None] // (2 * h)) == (t[None, :] // (2 * h))
        masks.append(same & upper[:, None] & (~upper)[None, :])
    masks.append(np.eye(CH, dtype=bool))
    m_all = np.concatenate(mats, axis=0).astype(np.float32)
    cid = np.arange(BLK) // CH
    gmask = (cid[:, None] >= cid[None, :]).astype(np.float32)
    return jnp.asarray(m_all, BF), jnp.asarray(np.stack(masks).astype(np.float32)), jnp.asarray(gmask)


def _split3(v):
    v1 = v.astype(BF)
    r1 = v - v1.astype(F32)
    v2 = r1.astype(BF)
    v3 = (r1 - v2.astype(F32)).astype(BF)
    return v1, v2, v3


@jax.custom_vjp
def _exact_mm(m, v):
    p1, p2, p3 = _split3(v)
    return _mm(m, p1) + _mm(m, p2) + _mm(m, p3)


def _exact_mm_fwd(m, v):
    return _exact_mm(m, v), m


def _exact_mm_bwd(m, g):
    p1, p2, p3 = _split3(g)
    return jnp.zeros_like(m), _tn(m, p1) + _tn(m, p2) + _tn(m, p3)


_exact_mm.defvjp(_exact_mm_fwd, _exact_mm_bwd)


def _mix_tile(proj, state, w_s, b_s_t, ln_w, ln_b, lower, gn_w, m_all, masks, gmask):
    u = proj[:, 0:512]
    v = proj[:, 512:1024]
    q = proj[:, 1024:1536]
    fl = proj[:, 1536:2048]
    inp = proj[:, 2048:2560]
    g = proj[:, 2560:3072]

    ug = _gelu(u)
    vg = _gelu(v)
    mu = jnp.mean(vg, axis=-1, keepdims=True)
    vc = vg - mu
    var = jnp.mean(vc * vc, axis=-1, keepdims=True)
    vnb = (vc * lax.rsqrt(var + EPS) * ln_w + ln_b).astype(BF)
    ya = []
    for h in range(NH):
        wsm = (w_s[h] * gmask).astype(BF)
        mixed = _mm(wsm, vnb[:, h * HD:(h + 1) * HD]) + b_s_t[:, h:h + 1]
        ya.append(ug[:, h * HD:(h + 1) * HD] * mixed)

    l0 = lower[0:1, :]
    l1 = lower[1:2, :]
    mx = jnp.maximum(l0, l1)
    e0 = jnp.exp(l0 - mx)
    e1 = jnp.exp(l1 - mx)
    lb = e0 / (e0 + e1)
    qf = _silu(q)
    f = lb + (1.0 - lb) * jax.nn.sigmoid(fl)
    logf = jnp.log(f)
    kk = 1.0 - f
    gate = _silu(g)
    nl = len(LEVELS)
    st = list(state)
    yb = [[None] * NH for _ in range(BLK // CH)]
    for ci in range(BLK // CH):
        rows = slice(ci * CH, (ci + 1) * CH)
        ex = jnp.exp(_exact_mm(m_all, logf[rows]))
        for h in range(NH):
            cols = slice(h * HD, (h + 1) * HD)
            qh = qf[rows, cols]
            kh = kk[rows, cols]
            vh = inp[rows, cols].astype(BF)
            eb = ex[0:CH, cols]
            einv = ex[CH:2 * CH, cols]
            inter = _nt((qh * eb).astype(BF), st[h].astype(BF))
            attn = masks[nl] * _nt(qh.astype(BF), kh.astype(BF))
            for li in range(nl):
                el = ex[(2 + li) * CH:(3 + li) * CH, cols]
                attn = attn + masks[li] * _nt((qh * el).astype(BF), (kh * el).astype(BF))
            o = inter + _mm(attn.astype(BF), vh)
            st[h] = st[h] * eb[CH - 1:CH, :] + _tn(vh, (kh * einv).astype(BF))
            rs = lax.rsqrt(jnp.mean(o * o, axis=-1, keepdims=True) + EPS)
            yb[ci][h] = o * rs * gn_w * gate[rows, cols]
    yb_full = jnp.concatenate([jnp.concatenate(r, axis=1) for r in yb], axis=0)
    return jnp.concatenate(ya + [yb_full], axis=1), tuple(st)


def _fwd_mix(proj, w_s, b_s_t, ln_w, ln_b, lower, gn_w, tables, to_gather):
    t = proj.shape[0]
    nt_ = t // BLK
    m_all, masks, gmask = tables
    ng = len(to_gather)

    def body(*refs):
        proj_ref, ws_ref, bs_ref, lw_ref, lb_ref, lo_ref, gn_ref, m_ref, mk_ref, gm_ref = refs[:10]
        g_in = refs[10:10 + ng]
        y_ref, st_ref = refs[10 + ng:12 + ng]
        g_out = refs[12 + ng:12 + 2 * ng]
        state = refs[12 + 2 * ng]
        ex = _Exchange(g_in, g_out, *refs[13 + 2 * ng:], False)

        @pl.when(pl.program_id(0) == 0)
        def _():
            state[...] = jnp.zeros_like(state)
            ex.start()

        st = tuple(state[h] for h in range(NH))
        for h in range(NH):
            st_ref[h] = st[h]
        y, new = _mix_tile(proj_ref[...], st, ws_ref[...], bs_ref[...], lw_ref[...], lb_ref[...], lo_ref[...], gn_ref[...],
                           m_ref[...], mk_ref[...], gm_ref[...])
        y_ref[...] = y.astype(BF)
        for h in range(NH):
            state[h] = new[h]

        @pl.when(pl.program_id(0) == nt_ - 1)
        def _():
            ex.wait()

    full = lambda a: pl.BlockSpec(a.shape, lambda i, nd=a.ndim: (0,) * nd)
    outs = pl.pallas_call(
        body, name="fwd_mix", grid=(nt_,),
        out_shape=(jax.ShapeDtypeStruct((t, D), BF), jax.ShapeDtypeStruct((nt_, NH, HD, HD), F32),
                   *_exchange_out_shapes(to_gather, False)),
        in_specs=[pl.BlockSpec((BLK, D_IN), lambda i: (i, 0)), full(w_s), full(b_s_t), full(ln_w), full(ln_b), full(lower),
                  full(gn_w), full(m_all), full(masks), full(gmask)] + [_any() for _ in to_gather],
        out_specs=(pl.BlockSpec((BLK, D), lambda i: (i, 0)), pl.BlockSpec((None, NH, HD, HD), lambda i: (i, 0, 0, 0)),
                   *[_any() for _ in to_gather]),
        scratch_shapes=[pltpu.VMEM((NH, HD, HD), F32)] + _exchange_sems(ng),
        compiler_params=_params(dimension_semantics=("arbitrary",), has_side_effects=True),
    )(proj, w_s, b_s_t, ln_w, ln_b, lower, gn_w, m_all, masks, gmask, *to_gather)
    return outs[0], outs[1], outs[2:]


def _bwd_mix(proj, dycat, states, w_s, b_s_t, ln_w, ln_b, lower, gn_w, tables, to_scatter):
    t = proj.shape[0]
    nt_ = t // BLK
    m_all, masks, gmask = tables
    ns = len(to_scatter)

    def body(*refs):
        proj_ref, dy_ref, st_ref, ws_ref, bs_ref, lw_ref, lb_ref, lo_ref, gn_ref, m_ref, mk_ref, gm_ref = refs[:12]
        s_in = refs[12:12 + ns]
        dproj_ref = refs[12 + ns]
        acc = refs[13 + ns:19 + ns]
        s_out = refs[19 + ns:19 + 2 * ns]
        dstate = refs[19 + 2 * ns]
        ex = _Exchange(s_in, s_out, *refs[20 + 2 * ns:], True)

        @pl.when(pl.program_id(0) == 0)
        def _():
            dstate[...] = jnp.zeros_like(dstate)
            for r in acc:
                r[...] = jnp.zeros_like(r)
            ex.start()

        def f(p, s, ws, bs, lw, lb_, lo, gn):
            return _mix_tile(p, s, ws, bs, lw, lb_, lo, gn, m_ref[...], mk_ref[...], gm_ref[...])

        st = tuple(st_ref[h] for h in range(NH))
        _, vjp = jax.vjp(f, proj_ref[...], st, ws_ref[...], bs_ref[...], lw_ref[...], lb_ref[...], lo_ref[...], gn_ref[...])
        grads = vjp((dy_ref[...], tuple(dstate[h] for h in range(NH))))
        dproj_ref[...] = grads[0].astype(BF)
        for h in range(NH):
            dstate[h] = grads[1][h]
        for r, gval in zip(acc, grads[2:]):
            r[...] += gval

        @pl.when(pl.program_id(0) == nt_ - 1)
        def _():
            ex.wait()

    full = lambda a: pl.BlockSpec(a.shape, lambda i, nd=a.ndim: (0,) * nd)
    rev = lambda i: (nt_ - 1 - i, 0)
    smalls = (w_s, b_s_t, ln_w, ln_b, lower, gn_w)
    outs = pl.pallas_call(
        body, name="bwd_mix", grid=(nt_,),
        out_shape=(jax.ShapeDtypeStruct((t, D_IN), BF), *[jax.ShapeDtypeStruct(a.shape, F32) for a in smalls],
                   *_exchange_out_shapes(to_scatter, True)),
        in_specs=[pl.BlockSpec((BLK, D_IN), rev), pl.BlockSpec((BLK, D), rev),
                  pl.BlockSpec((None, NH, HD, HD), lambda i: (nt_ - 1 - i, 0, 0, 0))]
        + [full(a) for a in smalls] + [full(m_all), full(masks), full(gmask)] + [_any() for _ in to_scatter],
        out_specs=(pl.BlockSpec((BLK, D_IN), rev), *[full(a) for a in smalls], *[_any() for _ in to_scatter]),
        scratch_shapes=[pltpu.VMEM((NH, HD, HD), F32)] + _exchange_sems(ns),
        compiler_params=_params(dimension_semantics=("arbitrary",), has_side_effects=True),
    )(proj, dycat, states, w_s, b_s_t, ln_w, ln_b, lower, gn_w, m_all, masks, gmask, *to_scatter)
    return outs[0], outs[1:7], outs[7:]


def _row_tile(t, want):
    return want if t % want == 0 else t


def _rms(v):
    rstd = lax.rsqrt(jnp.mean(v * v, axis=-1, keepdims=True) + EPS)
    return v * rstd, rstd


def _rms_bwd(dxhat, xhat, rstd):
    return rstd * (dxhat - xhat * jnp.mean(dxhat * xhat, axis=-1, keepdims=True))


def _colsum(v):
    return jnp.sum(v, axis=0, keepdims=True)


def _fwd_in(x, ada, n1w, w_in_st):
    t = x.shape[0]
    tm = _row_tile(t, 256)

    def body(x_ref, ada_ref, n1_ref, wst_ref, proj_ref, h1_ref, w_ref, sems):
        @pl.when(pl.program_id(0) == 0)
        def _():
            _load_columns(wst_ref, w_ref, sems)

        xh, _ = _rms(x_ref[...])
        h1 = (xh * n1_ref[...]) * (1.0 + ada_ref[1:2, :]) + ada_ref[0:1, :]
        h1b = h1.astype(BF)
        h1_ref[...] = h1b
        proj_ref[...] = _mm(h1b, w_ref[...])

    full = lambda a: pl.BlockSpec(a.shape, lambda i, nd=a.ndim: (0,) * nd)
    return pl.pallas_call(
        body, name="fwd_in", grid=(t // tm,),
        out_shape=(jax.ShapeDtypeStruct((t, D_IN), F32), jax.ShapeDtypeStruct((t, D), BF)),
        in_specs=[pl.BlockSpec((tm, D), lambda i: (i, 0)), full(ada), full(n1w), _any()],
        out_specs=(pl.BlockSpec((tm, D_IN), lambda i: (i, 0)), pl.BlockSpec((tm, D), lambda i: (i, 0))),
        scratch_shapes=[pltpu.VMEM((D, D_IN), BF), pltpu.SemaphoreType.DMA((N_DEV,))],
        compiler_params=_params(dimension_semantics=("arbitrary",)),
    )(x, ada, n1w, w_in_st)


def _fwd_ffn(x, ycat, tgt, ada, n2w, fw, w_out, w_fi, w_fo):
    t = x.shape[0]
    tm = _row_tile(t, 256)

    def body(x_ref, y_ref, t_ref, ada_ref, n2_ref, fw_ref, wo_ref, wi_ref, wf_ref,
             x1_ref, h2_ref, act_ref, dffn_ref, dx2_ref, part_ref):
        @pl.when(pl.program_id(0) == 0)
        def _():
            part_ref[...] = jnp.zeros_like(part_ref)

        g1, sh2, sc2, g2 = ada_ref[2:3, :], ada_ref[3:4, :], ada_ref[4:5, :], ada_ref[5:6, :]
        x1 = x_ref[...] + g1 * _mm(y_ref[...], wo_ref[...])
        x1_ref[...] = x1
        xh2, _ = _rms(x1)
        h2b = ((xh2 * n2_ref[...]) * (1.0 + sc2) + sh2).astype(BF)
        h2_ref[...] = h2b
        ffn = jnp.zeros((tm, D), F32)
        for j in range(4):
            gate = _mm(h2b, wi_ref[j])
            up = _mm(h2b, wi_ref[j + 4])
            a = (_silu(gate) * up).astype(BF)
            act_ref[j] = a
            ffn = ffn + _mm(a, wf_ref[j * FF_BLK:(j + 1) * FF_BLK, :])
        x2 = x1 + g2 * ffn
        xh3, rstd3 = _rms(x2)
        err = xh3 * fw_ref[...] - t_ref[...]
        dy = err * (1.0 / D)
        dx2 = _rms_bwd(dy * fw_ref[...], xh3, rstd3)
        dx2_ref[...] = dx2
        dffn_ref[...] = (g2 * dx2).astype(BF)
        part_ref[0:1, :] += _colsum(dx2 * ffn)
        part_ref[1:2, :] += _colsum(dy * xh3)
        part_ref[2:3, :] += jnp.zeros((1, D), F32) + (0.5 / D) * jnp.sum(err * err)

    full = lambda a: pl.BlockSpec(a.shape, lambda i, nd=a.ndim: (0,) * nd)
    row = lambda w: pl.BlockSpec((tm, w), lambda i: (i, 0))
    return pl.pallas_call(
        body, name="fwd_ffn", grid=(t // tm,),
        out_shape=(jax.ShapeDtypeStruct((t, D), F32), jax.ShapeDtypeStruct((t, D), BF),
                   jax.ShapeDtypeStruct((4, t, FF_BLK), BF), jax.ShapeDtypeStruct((t, D), BF),
                   jax.ShapeDtypeStruct((t, D), F32), jax.ShapeDtypeStruct((8, D), F32)),
        in_specs=[row(D), row(D), row(D), full(ada), full(n2w), full(fw), _whole_vmem(), _whole_vmem(), _whole_vmem()],
        out_specs=(row(D), row(D), pl.BlockSpec((4, tm, FF_BLK), lambda i: (0, i, 0)), row(D), row(D),
                   pl.BlockSpec((8, D), lambda i: (0, 0))),
        compiler_params=_params(dimension_semantics=("arbitrary",)),
    )(x, ycat, tgt, ada, n2w, fw, w_out, w_fi, w_fo)


def _bwd_ffn(x1, h2, dffn, dx2, ycat, ada, n2w, w_out, w_fi, w_fo):
    t = x1.shape[0]
    tm = _row_tile(t, 256)

    def body(x1_ref, h2_ref, dffn_ref, dx2_ref, y_ref, ada_ref, n2_ref, wo_ref, wi_ref, wf_ref,
             dgu_ref, dx1_ref, dmix_ref, dycat_ref, part_ref):
        @pl.when(pl.program_id(0) == 0)
        def _():
            part_ref[...] = jnp.zeros_like(part_ref)

        g1, sc2 = ada_ref[2:3, :], ada_ref[4:5, :]
        h2b = h2_ref[...]
        dffn = dffn_ref[...]
        dh2 = jnp.zeros((tm, D), F32)
        for j in range(4):
            gate = _mm(h2b, wi_ref[j])
            up = _mm(h2b, wi_ref[j + 4])
            dact = _nt(dffn, wf_ref[j * FF_BLK:(j + 1) * FF_BLK, :])
            sg = jax.nn.sigmoid(gate)
            dgate = (dact * up * (sg * (1.0 + gate * (1.0 - sg)))).astype(BF)
            dup = (dact * (gate * sg)).astype(BF)
            dgu_ref[j] = dgate
            dgu_ref[j + 4] = dup
            dh2 = dh2 + _nt(dgate, wi_ref[j]) + _nt(dup, wi_ref[j + 4])
        x1 = x1_ref[...]
        xh2, rstd2 = _rms(x1)
        xn2 = xh2 * n2_ref[...]
        dxn2 = dh2 * (1.0 + sc2)
        dx1 = dx2_ref[...] + _rms_bwd(dxn2 * n2_ref[...], xh2, rstd2)
        dx1_ref[...] = dx1
        mix = _mm(y_ref[...], wo_ref[...])
        dmix = (g1 * dx1).astype(BF)
        dmix_ref[...] = dmix
        dycat_ref[...] = _nt(dmix, wo_ref[...])
        part_ref[0:1, :] += _colsum(dh2)
        part_ref[1:2, :] += _colsum(dh2 * xn2)
        part_ref[2:3, :] += _colsum(dxn2 * xh2)
        part_ref[3:4, :] += _colsum(dx1 * mix)

    full = lambda a: pl.BlockSpec(a.shape, lambda i, nd=a.ndim: (0,) * nd)
    row = lambda w: pl.BlockSpec((tm, w), lambda i: (i, 0))
    return pl.pallas_call(
        body, name="bwd_ffn", grid=(t // tm,),
        out_shape=(jax.ShapeDtypeStruct((N_DEV, t, FF_BLK), BF), jax.ShapeDtypeStruct((t, D), F32),
                   jax.ShapeDtypeStruct((t, D), BF), jax.ShapeDtypeStruct((t, D), F32), jax.ShapeDtypeStruct((8, D), F32)),
        in_specs=[row(D), row(D), row(D), row(D), row(D), full(ada), full(n2w), _whole_vmem(), _whole_vmem(), _whole_vmem()],
        out_specs=(pl.BlockSpec((N_DEV, tm, FF_BLK), lambda i: (0, i, 0)), row(D), row(D), row(D),
                   pl.BlockSpec((8, D), lambda i: (0, 0))),
        compiler_params=_params(dimension_semantics=("arbitrary",)),
    )(x1, h2, dffn, dx2, ycat, ada, n2w, w_out, w_fi, w_fo)


def _bwd_in(x, dproj, dx1, ada, n1w, w_in_st, to_scatter):
    t = x.shape[0]
    tm = _row_tile(t, 256)
    n_t = t // tm
    ns = len(to_scatter)

    def body(*refs):
        x_ref, dp_ref, dx1_ref, ada_ref, n1_ref, wst_ref = refs[:6]
        s_in = refs[6:6 + ns]
        gx_ref, part_ref = refs[6 + ns:8 + ns]
        s_out = refs[8 + ns:8 + 2 * ns]
        w_ref, sems = refs[8 + 2 * ns:10 + 2 * ns]
        ex = _Exchange(s_in, s_out, *refs[10 + 2 * ns:], True)

        @pl.when(pl.program_id(0) == 0)
        def _():
            ex.start()
            part_ref[...] = jnp.zeros_like(part_ref)
            _load_columns(wst_ref, w_ref, sems)

        dh1 = _nt(dp_ref[...], w_ref[...])
        xh, rstd = _rms(x_ref[...])
        xn = xh * n1_ref[...]
        dxn = dh1 * (1.0 + ada_ref[1:2, :])
        gx_ref[...] = dx1_ref[...] + _rms_bwd(dxn * n1_ref[...], xh, rstd)
        part_ref[0:1, :] += _colsum(dh1)
        part_ref[1:2, :] += _colsum(dh1 * xn)
        part_ref[2:3, :] += _colsum(dxn * xh)

        @pl.when(pl.program_id(0) == n_t - 1)
        def _():
            ex.wait()

    full = lambda a: pl.BlockSpec(a.shape, lambda i, nd=a.ndim: (0,) * nd)
    row = lambda w: pl.BlockSpec((tm, w), lambda i: (i, 0))
    outs = pl.pallas_call(
        body, name="bwd_in", grid=(n_t,),
        out_shape=(jax.ShapeDtypeStruct((t, D), F32), jax.ShapeDtypeStruct((8, D), F32),
                   *_exchange_out_shapes(to_scatter, True)),
        in_specs=[row(D), row(D_IN), row(D), full(ada), full(n1w), _any()] + [_any() for _ in to_scatter],
        out_specs=(row(D), pl.BlockSpec((8, D), lambda i: (0, 0)), *[_any() for _ in to_scatter]),
        scratch_shapes=[pltpu.VMEM((D, D_IN), BF), pltpu.SemaphoreType.DMA((N_DEV,))] + _exchange_sems(ns),
        compiler_params=_params(dimension_semantics=("arbitrary",), has_side_effects=True),
    )(x, dproj, dx1, ada, n1w, w_in_st, *to_scatter)
    return outs[0], outs[1], outs[2:]


def _wgrad(a, b, name, a_spec, b_spec, out_shape, out_spec, grid, acc_shape):
    n_t = grid[1]

    def body(a_ref, b_ref, o_ref, acc):
        @pl.when(pl.program_id(1) == 0)
        def _():
            acc[...] = jnp.zeros_like(acc)

        acc[...] += _tn(a_ref[...], b_ref[...])

        @pl.when(pl.program_id(1) == n_t - 1)
        def _():
            o_ref[...] = acc[...].astype(BF)

    return pl.pallas_call(
        body, name=name, grid=grid,
        out_shape=jax.ShapeDtypeStruct(out_shape, BF),
        in_specs=[a_spec, b_spec], out_specs=out_spec,
        scratch_shapes=[pltpu.VMEM(acc_shape, F32)],
        compiler_params=_params(dimension_semantics=("arbitrary", "arbitrary")),
    )(a, b)


def _adamw_math(w, g, m, v):
    m = ADAM_B1 * m + (1.0 - ADAM_B1) * g
    v = ADAM_B2 * v + (1.0 - ADAM_B2) * (g * g)
    m_hat = m / (1.0 - ADAM_B1 ** ADAM_STEP)
    v_hat = v / (1.0 - ADAM_B2 ** ADAM_STEP)
    delta = -ADAM_LR * (m_hat / (jnp.sqrt(v_hat) + ADAM_EPS) + ADAM_WD * w)
    return delta, m, v


def _adamw_recv(w, m, v, recv, name, tr):
    r, c = w.shape

    def body(w_ref, m_ref, v_ref, r_ref, g_ref, d_ref, nm_ref, nv_ref):
        g = r_ref[0].astype(F32)
        for k in range(1, N_DEV):
            g = g + r_ref[k].astype(F32)
        g_ref[...] = g
        d_ref[...], nm_ref[...], nv_ref[...] = _adamw_math(w_ref[...], g, m_ref[...], v_ref[...])

    row = pl.BlockSpec((tr, c), lambda i: (i, 0))
    return pl.pallas_call(
        body, name=name, grid=(r // tr,),
        out_shape=tuple(jax.ShapeDtypeStruct((r, c), F32) for _ in range(4)),
        in_specs=[row, row, row, pl.BlockSpec((N_DEV, tr, c), lambda i: (0, i, 0))],
        out_specs=(row, row, row, row),
        compiler_params=_params(dimension_semantics=("arbitrary",)),
    )(w, m, v, recv)


def _adamw_ada(w, m, v, cact, dada_cols):
    r, c = w.shape
    tr = 256

    def body(w_ref, m_ref, v_ref, ca_ref, da_ref, g_ref, d_ref, nm_ref, nv_ref):
        g = _tn(ca_ref[...].astype(BF), da_ref[...].astype(BF))
        g_ref[...] = g
        d_ref[...], nm_ref[...], nv_ref[...] = _adamw_math(w_ref[...], g, m_ref[...], v_ref[...])

    row = pl.BlockSpec((tr, c), lambda i: (i, 0))
    return pl.pallas_call(
        body, name="adamw_ada", grid=(r // tr,),
        out_shape=tuple(jax.ShapeDtypeStruct((r, c), F32) for _ in range(4)),
        in_specs=[row, row, row, pl.BlockSpec((N_DEV, tr), lambda i: (0, i)), pl.BlockSpec(dada_cols.shape, lambda i: (0, 0))],
        out_specs=(row, row, row, row),
        compiler_params=_params(dimension_semantics=("arbitrary",)),
    )(w, m, v, cact, dada_cols)


def _adamw_small(gathered, wmv):
    n_g = len(gathered)
    n_p = len(wmv)
    flat = [a for trip in wmv for a in trip]

    def body(*refs):
        g_refs = refs[:n_g]
        p_refs = refs[n_g:n_g + 3 * n_p]
        o_refs = refs[n_g + 3 * n_p:]

        def total(ref):
            s = ref[0]
            for k in range(1, N_DEV):
                s = s + ref[k]
            return s

        f3, b3, b1, dws, dbs, dlnw, dlnb, dlo, dgn = [total(r) for r in g_refs]
        dada_rows = [b1[0:1], b1[1:2], b3[3:4], b3[0:1], b3[1:2], f3[0:1]]
        for r, g in enumerate(dada_rows):
            w, m, v = p_refs[0][r:r + 1, :], p_refs[1][r:r + 1, :], p_refs[2][r:r + 1, :]
            o_refs[0][r:r + 1, :] = g
            o_refs[1][r:r + 1, :], o_refs[2][r:r + 1, :], o_refs[3][r:r + 1, :] = _adamw_math(w, g, m, v)
        grads = [None, b1[2:3], dws, dbs, dlnw, dlnb, dlo, dgn, b3[2:3], f3[1:2]]
        for i, g in enumerate(grads):
            if g is None:
                continue
            w, m, v = p_refs[3 * i][...], p_refs[3 * i + 1][...], p_refs[3 * i + 2][...]
            o_refs[4 * i][...] = g
            o_refs[4 * i + 1][...], o_refs[4 * i + 2][...], o_refs[4 * i + 3][...] = _adamw_math(w, g, m, v)
        o_refs[4 * n_p][...] = jnp.zeros((8, 128), F32) + f3[2:3, 0:128]

    out_shape = []
    for w, _, _ in wmv:
        out_shape += [jax.ShapeDtypeStruct(w.shape, F32)] * 4
    out_shape.append(jax.ShapeDtypeStruct((8, 128), F32))
    n_in = n_g + 3 * n_p
    return pl.pallas_call(
        body, name="adamw_small",
        out_shape=tuple(out_shape),
        in_specs=[_whole_vmem()] * n_in, out_specs=tuple(_whole_vmem() for _ in out_shape),
        compiler_params=_params(),
    )(*gathered, *flat)


def kernel(x, c, w_ada, b_ada, norm1_w, w_in, w_s, b_s, v_ln_w, v_ln_b, lower_bounds, gn_w, w_out, norm2_w, w_ffn_in, w_ffn_out, final_norm_w, loss_target, m_w_ada, m_b_ada, m_norm1_w, m_w_in, m_w_s, m_b_s, m_v_ln_w, m_v_ln_b, m_lower_bounds, m_gn_w, m_w_out, m_norm2_w, m_w_ffn_in, m_w_ffn_out, m_final_norm_w, v_w_ada, v_b_ada, v_norm1_w, v_w_in, v_w_s, v_b_s, v_v_ln_w, v_v_ln_b, v_lower_bounds, v_gn_w, v_w_out, v_norm2_w, v_w_ffn_in, v_w_ffn_out, v_final_norm_w):
    me = 4 * lax.axis_index("x") + 2 * lax.axis_index("y") + lax.axis_index("c")
    t = x.shape[1]
    x2d = x.reshape(t, D)
    tgt = loss_target.reshape(t, D)
    ada_cols = w_ada.shape[2]

    win_b, wout_b, wfi_b, wfo_b = _cast_bf16([w_in[0], w_out[0], w_ffn_in[0], w_ffn_out[0]], "cast_weights")
    win_st, c_st = _exchange([win_b, c], "gather_w_in", False, False)
    c_all = c_st.reshape(N_DEV, D)

    b_cols = lax.dynamic_slice(b_ada, (0, me * ada_cols), (1, ada_cols))
    ada_blk, cact = _ada_block(c_all, w_ada[0], b_cols)
    (ada_st,) = _exchange([ada_blk], "gather_ada", False, True)
    ada = lax.dynamic_index_in_dim(ada_st, me, axis=1, keepdims=False).reshape(6, D)

    tables = _decay_tables()
    ws3 = w_s[0]
    bs_t = b_s[0].T

    proj, h1 = _fwd_in(x2d, ada, norm1_w, win_st)
    ycat, states, (wout_st, wfi_st, wfo_st) = _fwd_mix(proj, ws3, bs_t, v_ln_w, v_ln_b, lower_bounds, gn_w, tables,
                                                       [wout_b, wfi_b, wfo_b])
    w_out_full = wout_st.reshape(D, D)
    w_fo_full = wfo_st.reshape(D_FF, D)
    x1, h2, act, dffn, dx2, part_f = _fwd_ffn(x2d, ycat, tgt, ada, norm2_w, final_norm_w.reshape(1, D),
                                              w_out_full, wfi_st, w_fo_full)
    dgu, dx1, dmix, dycat, part_b3 = _bwd_ffn(x1, h2, dffn, dx2, ycat, ada, norm2_w, w_out_full, wfi_st, w_fo_full)
    tk = _row_tile(t, 512)
    n_t = t // tk
    win_cols = D_IN // N_DEV
    dwout = _wgrad(ycat, dmix, "wgrad_out",
                   pl.BlockSpec((tk, D), lambda j, i: (i, 0)), pl.BlockSpec((tk, D), lambda j, i: (i, 0)),
                   (D, D), pl.BlockSpec((D, D), lambda j, i: (0, 0)), (1, n_t), (D, D))
    dwfi = _wgrad(h2, dgu, "wgrad_ffn_in",
                  pl.BlockSpec((tk, D), lambda j, i: (i, 0)), pl.BlockSpec((None, tk, FF_BLK), lambda j, i: (j, i, 0)),
                  (N_DEV, D, FF_BLK), pl.BlockSpec((None, D, FF_BLK), lambda j, i: (j, 0, 0)), (N_DEV, n_t), (D, FF_BLK))
    dwfo = _wgrad(act, dffn, "wgrad_ffn_out",
                  pl.BlockSpec((None, tk, FF_BLK), lambda j, i: (j, i, 0)), pl.BlockSpec((tk, D), lambda j, i: (i, 0)),
                  (4, FF_BLK, D), pl.BlockSpec((None, FF_BLK, D), lambda j, i: (j, 0, 0)), (4, n_t), (FF_BLK, D))
    dproj, (dws, dbs_t, dlnw, dlnb, dlower, dgnw), (r_out, r_fi, r_fo) = _bwd_mix(
        proj, dycat, states, ws3, bs_t, v_ln_w, v_ln_b, lower_bounds, gn_w, tables,
        [dwout.reshape(N_DEV, D // N_DEV, D), dwfi, dwfo.reshape(N_DEV, D_FF // N_DEV, D)])
    dwin = _wgrad(h1, dproj, "wgrad_in",
                  pl.BlockSpec((tk, D), lambda j, i: (i, 0)), pl.BlockSpec((tk, win_cols), lambda j, i: (i, j)),
                  (N_DEV, D, win_cols), pl.BlockSpec((None, D, win_cols), lambda j, i: (j, 0, 0)), (N_DEV, n_t), (D, win_cols))
    grad_x, part_b1, (r_in,) = _bwd_in(x2d, dproj, dx1, ada, norm1_w, win_st, [dwin])

    g_w_in, d_w_in, nm_w_in, nv_w_in = _adamw_recv(w_in[0], m_w_in[0], v_w_in[0], r_in, "adamw_w_in", 256)
    g_w_out, d_w_out, nm_w_out, nv_w_out = _adamw_recv(w_out[0], m_w_out[0], v_w_out[0], r_out, "adamw_w_out", 128)
    g_w_fi, d_w_fi, nm_w_fi, nv_w_fi = _adamw_recv(w_ffn_in[0], m_w_ffn_in[0], v_w_ffn_in[0], r_fi, "adamw_w_ffn_in", 256)
    g_w_fo, d_w_fo, nm_w_fo, nv_w_fo = _adamw_recv(w_ffn_out[0], m_w_ffn_out[0], v_w_ffn_out[0], r_fo, "adamw_w_ffn_out", 176)

    gathered = _exchange([part_f, part_b3, part_b1, dws, dbs_t, dlnw, dlnb, dlower, dgnw], "gather_small", False, True)
    f3_all, b3_all, b1_all = gathered[0], gathered[1], gathered[2]
    dada_all = jnp.stack([b1_all[:, 0], b1_all[:, 1], b3_all[:, 3], b3_all[:, 0], b3_all[:, 1], f3_all[:, 0]], axis=1)
    dada_cols = lax.dynamic_slice(dada_all.reshape(N_DEV, 6 * D), (0, me * ada_cols), (N_DEV, ada_cols))
    g_w_ada, d_w_ada, nm_w_ada, nv_w_ada = _adamw_ada(w_ada[0], m_w_ada[0], v_w_ada[0], cact, dada_cols)

    r6 = lambda a: a.reshape(6, D)
    r1 = lambda a: a.reshape(1, D)
    tr = lambda a: a[0].T
    wmv = [
        (r6(b_ada), r6(m_b_ada), r6(v_b_ada)),
        (norm1_w, m_norm1_w, v_norm1_w),
        (w_s[0], m_w_s[0], v_w_s[0]),
        (tr(b_s), tr(m_b_s), tr(v_b_s)),
        (v_ln_w, m_v_ln_w, v_v_ln_w),
        (v_ln_b, m_v_ln_b, v_v_ln_b),
        (lower_bounds, m_lower_bounds, v_lower_bounds),
        (gn_w, m_gn_w, v_gn_w),
        (norm2_w, m_norm2_w, v_norm2_w),
        (r1(final_norm_w), r1(m_final_norm_w), r1(v_final_norm_w)),
    ]
    small = _adamw_small(gathered, wmv)
    loss = small[-1][0, 0]

    def unshape(i, a):
        if i == 0:
            return a.reshape(1, 6 * D)
        if i == 2:
            return a.reshape(1, NH, BLK, BLK)
        if i == 3:
            return a.T.reshape(1, NH, BLK)
        if i == 9:
            return a.reshape(D)
        return a

    def small_out(kind):
        return [unshape(i, small[4 * i + kind]) for i in range(len(wmv))]

    e3 = lambda a: a[None]
    big = {
        0: (e3(g_w_ada), e3(g_w_in), e3(g_w_out), e3(g_w_fi), e3(g_w_fo)),
        1: (e3(d_w_ada), e3(d_w_in), e3(d_w_out), e3(d_w_fi), e3(d_w_fo)),
        2: (e3(nm_w_ada), e3(nm_w_in), e3(nm_w_out), e3(nm_w_fi), e3(nm_w_fo)),
        3: (e3(nv_w_ada), e3(nv_w_in), e3(nv_w_out), e3(nv_w_fi), e3(nv_w_fo)),
    }

    def ordered(kind):
        s = small_out(kind)
        b_ = big[kind]
        return [b_[0], s[0], s[1], b_[1], s[2], s[3], s[4], s[5], s[6], s[7], b_[2], s[8], b_[3], b_[4], s[9]]

    return (loss, grad_x.reshape(1, t, D), *ordered(0), *ordered(1), *ordered(2), *ordered(3))
```

```python
import functools

import numpy as np
import jax
import jax.numpy as jnp
from jax import lax
from jax.experimental import pallas as pl
from jax.experimental.pallas import tpu as pltpu

F32 = jnp.float32
BF = jnp.bfloat16
MESH = pl.DeviceIdType.MESH

N_DEV = 8
D = 1024
D_IN = 3072
D_FF = 2816
FF_BLK = D_FF // 4
CH = 64
BLK = 128
MIX_TILE = 256
NH = 4
HD = 128
EPS = 1e-6
LEVELS = (32, 16, 8, 4, 2, 1)
N_TAB = 2 + len(LEVELS)

ADAM_LR = 0.001
ADAM_B1 = 0.9
ADAM_B2 = 0.999
ADAM_EPS = 1e-08
ADAM_WD = 0.01
ADAM_STEP = 10

VMEM_LIMIT = 56 * 1024 * 1024


def _params(**kw):
    return pltpu.CompilerParams(vmem_limit_bytes=VMEM_LIMIT, **kw)


def _whole_vmem():
    return pl.BlockSpec(memory_space=pltpu.VMEM)


def _any():
    return pl.BlockSpec(memory_space=pl.ANY)


class _Exchange:
    def __init__(self, ins, outs, send_sems, recv_sems, local_sems, scatter):
        self.ins, self.outs, self.scatter = ins, outs, scatter
        self.send_sems, self.recv_sems, self.local_sems = send_sems, recv_sems, local_sems
        x, y, c = lax.axis_index("x"), lax.axis_index("y"), lax.axis_index("c")
        self.me = 4 * x + 2 * y + c
        self.peers = []
        for k in range(1, N_DEV):
            peer = (1 - x if (k >> 2) & 1 else x, 1 - y if (k >> 1) & 1 else y, 1 - c if k & 1 else c)
            self.peers.append((peer, 4 * peer[0] + 2 * peer[1] + peer[2]))

    def _src(self, a, idx):
        return self.ins[a].at[idx] if self.scatter else self.ins[a]

    def _local(self):
        return [pltpu.make_async_copy(self._src(a, self.me), self.outs[a].at[self.me], self.local_sems.at[a])
                for a in range(len(self.ins))]

    def _remote(self, a, k, dst_slot):
        peer, peer_idx = self.peers[k]
        return pltpu.make_async_remote_copy(
            src_ref=self._src(a, peer_idx), dst_ref=self.outs[a].at[dst_slot],
            send_sem=self.send_sems.at[a, k], recv_sem=self.recv_sems.at[a, k],
            device_id=peer, device_id_type=MESH)

    def start(self):
        for cp in self._local():
            cp.start()
        for k in range(N_DEV - 1):
            for a in range(len(self.ins)):
                self._remote(a, k, self.me).start()

    def wait(self):
        for k in range(N_DEV - 1):
            for a in range(len(self.ins)):
                self._remote(a, k, self.peers[k][1]).wait()
        for cp in self._local():
            cp.wait()


def _exchange_sems(n):
    return [pltpu.SemaphoreType.DMA((n, N_DEV - 1)), pltpu.SemaphoreType.DMA((n, N_DEV - 1)), pltpu.SemaphoreType.DMA((n,))]


def _exchange_out_shapes(arrays, scatter):
    return [jax.ShapeDtypeStruct(a.shape if scatter else (N_DEV,) + a.shape, a.dtype) for a in arrays]


def _exchange(arrays, name, scatter, in_vmem):
    n = len(arrays)

    def body(*refs):
        ex = _Exchange(refs[:n], refs[n:2 * n], *refs[2 * n:], scatter)
        ex.start()
        ex.wait()

    spec = _whole_vmem if in_vmem else _any
    return pl.pallas_call(
        body, name=name,
        out_shape=tuple(_exchange_out_shapes(arrays, scatter)),
        in_specs=[spec() for _ in arrays],
        out_specs=tuple(spec() for _ in arrays),
        scratch_shapes=_exchange_sems(n),
        compiler_params=_params(has_side_effects=True),
    )(*arrays)


def _cast_bf16(arrays, name):
    n = len(arrays)

    def body(*refs):
        for i in range(n):
            refs[n + i][...] = refs[i][...].astype(BF)

    return pl.pallas_call(
        body, name=name,
        out_shape=tuple(jax.ShapeDtypeStruct(a.shape, BF) for a in arrays),
        in_specs=[_whole_vmem() for _ in arrays],
        out_specs=tuple(_whole_vmem() for _ in arrays),
        compiler_params=_params(),
    )(*arrays)


def _load_columns(stacked_ref, full_ref, sems):
    c = stacked_ref.shape[2]
    cps = [pltpu.make_async_copy(stacked_ref.at[d], full_ref.at[:, pl.ds(d * c, c)], sems.at[d]) for d in range(N_DEV)]
    for cp in cps:
        cp.start()
    for cp in cps:
        cp.wait()


def _silu(v):
    return v * jax.nn.sigmoid(v)


def _gelu(v):
    return 0.5 * v * (1.0 + lax.erf(v * 0.7071067811865476))


def _nt(a, b):
    return lax.dot_general(a, b, (((1,), (1,)), ((), ())), preferred_element_type=F32)


def _tn(a, b):
    return lax.dot_general(a, b, (((0,), (0,)), ((), ())), preferred_element_type=F32)


def _mm(a, b):
    return jnp.dot(a, b, preferred_element_type=F32)


def _ada_block(c_all, w_ada, b_cols):
    def body(c_ref, w_ref, b_ref, ada_ref, cact_ref):
        ca = _silu(c_ref[...])
        cact_ref[...] = ca
        ada_ref[...] = _mm(ca.astype(BF), w_ref[...].astype(BF)) + b_ref[...]

    return pl.pallas_call(
        body, name="ada_block",
        out_shape=(jax.ShapeDtypeStruct((N_DEV, w_ada.shape[1]), F32), jax.ShapeDtypeStruct(c_all.shape, F32)),
        in_specs=[_whole_vmem()] * 3, out_specs=(_whole_vmem(), _whole_vmem()),
        compiler_params=_params(),
    )(c_all, w_ada, b_cols)


def _decay_tables():
    t = np.arange(CH)
    j = np.arange(CH)
    mats = [(j[None, :] <= t[:, None]), (j[None, :] > t[:, None])]
    masks = []
    for h in LEVELS:
        m = (t // (2 * h)) * (2 * h) + h
        upper = t >= m
        eq = (j[None, :] >= m[:, None]) & (j[None, :] <= t[:, None]) & upper[:, None]
        ek = (j[None, :] > t[:, None]) & (j[None, :] <= (m[:, None] - 1)) & (~upper)[:, None]
        mats.append(eq | ek)
        same = (t[:, None] // (2 * h)) == (t[None, :] // (2 * h))
        masks.append(same & upper[:, None] & (~upper)[None, :])
    masks.append(np.eye(CH, dtype=bool))
    m_all = np.concatenate(mats, axis=0).astype(np.float32)
    cid = np.arange(BLK) // CH
    gmask = (cid[:, None] >= cid[None, :]).astype(np.float32)
    return jnp.asarray(m_all, BF), jnp.asarray(np.stack(masks).astype(np.float32)), jnp.asarray(gmask)


def _split3(v):
    v1 = v.astype(BF)
    r1 = v - v1.astype(F32)
    v2 = r1.astype(BF)
    v3 = (r1 - v2.astype(F32)).astype(BF)
    return v1, v2, v3


@jax.custom_vjp
def _exact_mm(m, v):
    p1, p2, p3 = _split3(v)
    return _mm(m, p1) + _mm(m, p2) + _mm(m, p3)


def _exact_mm_fwd(m, v):
    return _exact_mm(m, v), m


def _exact_mm_bwd(m, g):
    p1, p2, p3 = _split3(g)
    return jnp.zeros_like(m), _tn(m, p1) + _tn(m, p2) + _tn(m, p3)


_exact_mm.defvjp(_exact_mm_fwd, _exact_mm_bwd)


def _mix_tile(proj, state, w_s, b_s_t, ln_w, ln_b, lower, gn_w, m_all, masks, gmask):
    mt = proj.shape[0]
    u = proj[:, 0:512]
    v = proj[:, 512:1024]
    q = proj[:, 1024:1536]
    fl = proj[:, 1536:2048]
    inp = proj[:, 2048:2560]
    g = proj[:, 2560:3072]

    ug = _gelu(u)
    vg = _gelu(v)
    mu = jnp.mean(vg, axis=-1, keepdims=True)
    vc = vg - mu
    var = jnp.mean(vc * vc, axis=-1, keepdims=True)
    vnb = (vc * lax.rsqrt(var + EPS) * ln_w + ln_b).astype(BF)
    wsm = [(w_s[h] * gmask).astype(BF) for h in range(NH)]
    ya = [[None] * NH for _ in range(mt // BLK)]
    for bi in range(mt // BLK):
        rows = slice(bi * BLK, (bi + 1) * BLK)
        for h in range(NH):
            cols = slice(h * HD, (h + 1) * HD)
            ya[bi][h] = ug[rows, cols] * (_mm(wsm[h], vnb[rows, cols]) + b_s_t[:, h:h + 1])
    ya_full = jnp.concatenate([jnp.concatenate(r, axis=1) for r in ya], axis=0)

    l0 = lower[0:1, :]
    l1 = lower[1:2, :]
    mx = jnp.maximum(l0, l1)
    e0 = jnp.exp(l0 - mx)
    e1 = jnp.exp(l1 - mx)
    lb = e0 / (e0 + e1)
    qf = _silu(q)
    f = lb + (1.0 - lb) * jax.nn.sigmoid(fl)
    logf = jnp.log(f)
    kk = 1.0 - f
    gate = _silu(g)
    nl = len(LEVELS)
    st = list(state)
    yb = [[None] * NH for _ in range(mt // CH)]
    for ci in range(mt // CH):
        rows = slice(ci * CH, (ci + 1) * CH)
        ex = jnp.exp(_exact_mm(m_all, logf[rows]))
        for h in range(NH):
            cols = slice(h * HD, (h + 1) * HD)
            qh = qf[rows, cols]
            kh = kk[rows, cols]
            vh = inp[rows, cols].astype(BF)
            eb = ex[0:CH, cols]
            einv = ex[CH:2 * CH, cols]
            inter = _nt((qh * eb).astype(BF), st[h].astype(BF))
            attn = masks[nl] * _nt(qh.astype(BF), kh.astype(BF))
            for li in range(nl):
                el = ex[(2 + li) * CH:(3 + li) * CH, cols]
                attn = attn + masks[li] * _nt((qh * el).astype(BF), (kh * el).astype(BF))
            o = inter + _mm(attn.astype(BF), vh)
            st[h] = st[h] * eb[CH - 1:CH, :] + _tn(vh, (kh * einv).astype(BF))
            rs = lax.rsqrt(jnp.mean(o * o, axis=-1, keepdims=True) + EPS)
            yb[ci][h] = o * rs * gn_w * gate[rows, cols]
    yb_full = jnp.concatenate([jnp.concatenate(r, axis=1) for r in yb], axis=0)
    return jnp.concatenate([ya_full, yb_full], axis=1), tuple(st)


def _fwd_mix(proj, w_s, b_s_t, ln_w, ln_b, lower, gn_w, tables, to_gather):
    t = proj.shape[0]
    mt = _row_tile(t, MIX_TILE)
    nt_ = t // mt
    m_all, masks, gmask = tables
    ng = len(to_gather)

    def body(*refs):
        proj_ref, ws_ref, bs_ref, lw_ref, lb_ref, lo_ref, gn_ref, m_ref, mk_ref, gm_ref = refs[:10]
        g_in = refs[10:10 + ng]
        y_ref, st_ref = refs[10 + ng:12 + ng]
        g_out = refs[12 + ng:12 + 2 * ng]
        state = refs[12 + 2 * ng]
        ex = _Exchange(g_in, g_out, *refs[13 + 2 * ng:], False)

        @pl.when(pl.program_id(0) == 0)
        def _():
            state[...] = jnp.zeros_like(state)
            ex.start()

        st = tuple(state[h] for h in range(NH))
        for h in range(NH):
            st_ref[h] = st[h]
        y, new = _mix_tile(proj_ref[...], st, ws_ref[...], bs_ref[...], lw_ref[...], lb_ref[...], lo_ref[...], gn_ref[...],
                           m_ref[...], mk_ref[...], gm_ref[...])
        y_ref[...] = y.astype(BF)
        for h in range(NH):
            state[h] = new[h]

        @pl.when(pl.program_id(0) == nt_ - 1)
        def _():
            ex.wait()

    full = lambda a: pl.BlockSpec(a.shape, lambda i, nd=a.ndim: (0,) * nd)
    outs = pl.pallas_call(
        body, name="fwd_mix", grid=(nt_,),
        out_shape=(jax.ShapeDtypeStruct((t, D), BF), jax.ShapeDtypeStruct((nt_, NH, HD, HD), F32),
                   *_exchange_out_shapes(to_gather, False)),
        in_specs=[pl.BlockSpec((mt, D_IN), lambda i: (i, 0)), full(w_s), full(b_s_t), full(ln_w), full(ln_b), full(lower),
                  full(gn_w), full(m_all), full(masks), full(gmask)] + [_any() for _ in to_gather],
        out_specs=(pl.BlockSpec((mt, D), lambda i: (i, 0)), pl.BlockSpec((None, NH, HD, HD), lambda i: (i, 0, 0, 0)),
                   *[_any() for _ in to_gather]),
        scratch_shapes=[pltpu.VMEM((NH, HD, HD), F32)] + _exchange_sems(ng),
        compiler_params=_params(dimension_semantics=("arbitrary",), has_side_effects=True),
    )(proj, w_s, b_s_t, ln_w, ln_b, lower, gn_w, m_all, masks, gmask, *to_gather)
    return outs[0], outs[1], outs[2:]


def _bwd_mix(proj, dycat, states, w_s, b_s_t, ln_w, ln_b, lower, gn_w, tables, to_scatter):
    t = proj.shape[0]
    mt = _row_tile(t, MIX_TILE)
    nt_ = t // mt
    m_all, masks, gmask = tables
    ns = len(to_scatter)

    def body(*refs):
        proj_ref, dy_ref, st_ref, ws_ref, bs_ref, lw_ref, lb_ref, lo_ref, gn_ref, m_ref, mk_ref, gm_ref = refs[:12]
        s_in = refs[12:12 + ns]
        dproj_ref = refs[12 + ns]
        acc = refs[13 + ns:19 + ns]
        s_out = refs[19 + ns:19 + 2 * ns]
        dstate = refs[19 + 2 * ns]
        ex = _Exchange(s_in, s_out, *refs[20 + 2 * ns:], True)

        @pl.when(pl.program_id(0) == 0)
        def _():
            dstate[...] = jnp.zeros_like(dstate)
            for r in acc:
                r[...] = jnp.zeros_like(r)
            ex.start()

        def f(p, s, ws, bs, lw, lb_, lo, gn):
            return _mix_tile(p, s, ws, bs, lw, lb_, lo, gn, m_ref[...], mk_ref[...], gm_ref[...])

        st = tuple(st_ref[h] for h in range(NH))
        _, vjp = jax.vjp(f, proj_ref[...], st, ws_ref[...], bs_ref[...], lw_ref[...], lb_ref[...], lo_ref[...], gn_ref[...])
        grads = vjp((dy_ref[...], tuple(dstate[h] for h in range(NH))))
        dproj_ref[...] = grads[0].astype(BF)
        for h in range(NH):
            dstate[h] = grads[1][h]
        for r, gval in zip(acc, grads[2:]):
            r[...] += gval

        @pl.when(pl.program_id(0) == nt_ - 1)
        def _():
            ex.wait()

    full = lambda a: pl.BlockSpec(a.shape, lambda i, nd=a.ndim: (0,) * nd)
    rev = lambda i: (nt_ - 1 - i, 0)
    smalls = (w_s, b_s_t, ln_w, ln_b, lower, gn_w)
    outs = pl.pallas_call(
        body, name="bwd_mix", grid=(nt_,),
        out_shape=(jax.ShapeDtypeStruct((t, D_IN), BF), *[jax.ShapeDtypeStruct(a.shape, F32) for a in smalls],
                   *_exchange_out_shapes(to_scatter, True)),
        in_specs=[pl.BlockSpec((mt, D_IN), rev), pl.BlockSpec((mt, D), rev),
                  pl.BlockSpec((None, NH, HD, HD), lambda i: (nt_ - 1 - i, 0, 0, 0))]
        + [full(a) for a in smalls] + [full(m_all), full(masks), full(gmask)] + [_any() for _ in to_scatter],
        out_specs=(pl.BlockSpec((mt, D_IN), rev), *[full(a) for a in smalls], *[_any() for _ in to_scatter]),
        scratch_shapes=[pltpu.VMEM((NH, HD, HD), F32)] + _exchange_sems(ns),
        compiler_params=_params(dimension_semantics=("arbitrary",), has_side_effects=True),
    )(proj, dycat, states, w_s, b_s_t, ln_w, ln_b, lower, gn_w, m_all, masks, gmask, *to_scatter)
    return outs[0], outs[1:7], outs[7:]


def _row_tile(t, want):
    return want if t % want == 0 else t


def _rms(v):
    rstd = lax.rsqrt(jnp.mean(v * v, axis=-1, keepdims=True) + EPS)
    return v * rstd, rstd


def _rms_bwd(dxhat, xhat, rstd):
    return rstd * (dxhat - xhat * jnp.mean(dxhat * xhat, axis=-1, keepdims=True))


def _colsum(v):
    return jnp.sum(v, axis=0, keepdims=True)


def _fwd_in(x, ada, n1w, w_in_st):
    t = x.shape[0]
    tm = _row_tile(t, 512)

    def body(x_ref, ada_ref, n1_ref, wst_ref, proj_ref, h1_ref, w_ref, sems):
        @pl.when(pl.program_id(0) == 0)
        def _():
            _load_columns(wst_ref, w_ref, sems)

        xh, _ = _rms(x_ref[...])
        h1 = (xh * n1_ref[...]) * (1.0 + ada_ref[1:2, :]) + ada_ref[0:1, :]
        h1b = h1.astype(BF)
        h1_ref[...] = h1b
        proj_ref[...] = _mm(h1b, w_ref[...])

    full = lambda a: pl.BlockSpec(a.shape, lambda i, nd=a.ndim: (0,) * nd)
    return pl.pallas_call(
        body, name="fwd_in", grid=(t // tm,),
        out_shape=(jax.ShapeDtypeStruct((t, D_IN), F32), jax.ShapeDtypeStruct((t, D), BF)),
        in_specs=[pl.BlockSpec((tm, D), lambda i: (i, 0)), full(ada), full(n1w), _any()],
        out_specs=(pl.BlockSpec((tm, D_IN), lambda i: (i, 0)), pl.BlockSpec((tm, D), lambda i: (i, 0))),
        scratch_shapes=[pltpu.VMEM((D, D_IN), BF), pltpu.SemaphoreType.DMA((N_DEV,))],
        compiler_params=_params(dimension_semantics=("arbitrary",)),
    )(x, ada, n1w, w_in_st)


def _fwd_ffn(x, ycat, tgt, ada, n2w, fw, w_out, w_fi, w_fo):
    t = x.shape[0]
    tm = _row_tile(t, 256)

    def body(x_ref, y_ref, t_ref, ada_ref, n2_ref, fw_ref, wo_ref, wi_ref, wf_ref,
             x1_ref, h2_ref, act_ref, dffn_ref, dx2_ref, part_ref):
        @pl.when(pl.program_id(0) == 0)
        def _():
            part_ref[...] = jnp.zeros_like(part_ref)

        g1, sh2, sc2, g2 = ada_ref[2:3, :], ada_ref[3:4, :], ada_ref[4:5, :], ada_ref[5:6, :]
        x1 = x_ref[...] + g1 * _mm(y_ref[...], wo_ref[...])
        x1_ref[...] = x1
        xh2, _ = _rms(x1)
        h2b = ((xh2 * n2_ref[...]) * (1.0 + sc2) + sh2).astype(BF)
        h2_ref[...] = h2b
        ffn = jnp.zeros((tm, D), F32)
        for j in range(4):
            gate = _mm(h2b, wi_ref[j])
            up = _mm(h2b, wi_ref[j + 4])
            a = (_silu(gate) * up).astype(BF)
            act_ref[j] = a
            ffn = ffn + _mm(a, wf_ref[j * FF_BLK:(j + 1) * FF_BLK, :])
        x2 = x1 + g2 * ffn
        xh3, rstd3 = _rms(x2)
        err = xh3 * fw_ref[...] - t_ref[...]
        dy = err * (1.0 / D)
        dx2 = _rms_bwd(dy * fw_ref[...], xh3, rstd3)
        dx2_ref[...] = dx2
        dffn_ref[...] = (g2 * dx2).astype(BF)
        part_ref[0:1, :] += _colsum(dx2 * ffn)
        part_ref[1:2, :] += _colsum(dy * xh3)
        part_ref[2:3, :] += jnp.zeros((1, D), F32) + (0.5 / D) * jnp.sum(err * err)

    full = lambda a: pl.BlockSpec(a.shape, lambda i, nd=a.ndim: (0,) * nd)
    row = lambda w: pl.BlockSpec((tm, w), lambda i: (i, 0))
    return pl.pallas_call(
        body, name="fwd_ffn", grid=(t // tm,),
        out_shape=(jax.ShapeDtypeStruct((t, D), F32), jax.ShapeDtypeStruct((t, D), BF),
                   jax.ShapeDtypeStruct((4, t, FF_BLK), BF), jax.ShapeDtypeStruct((t, D), BF),
                   jax.ShapeDtypeStruct((t, D), F32), jax.ShapeDtypeStruct((8, D), F32)),
        in_specs=[row(D), row(D), row(D), full(ada), full(n2w), full(fw), _whole_vmem(), _whole_vmem(), _whole_vmem()],
        out_specs=(row(D), row(D), pl.BlockSpec((4, tm, FF_BLK), lambda i: (0, i, 0)), row(D), row(D),
                   pl.BlockSpec((8, D), lambda i: (0, 0))),
        compiler_params=_params(dimension_semantics=("arbitrary",)),
    )(x, ycat, tgt, ada, n2w, fw, w_out, w_fi, w_fo)


def _bwd_ffn(x1, h2, dffn, dx2, ycat, ada, n2w, w_out, w_fi, w_fo):
    t = x1.shape[0]
    tm = _row_tile(t, 256)

    def body(x1_ref, h2_ref, dffn_ref, dx2_ref, y_ref, ada_ref, n2_ref, wo_ref, wi_ref, wf_ref,
             dgu_ref, dx1_ref, dmix_ref, dycat_ref, part_ref):
        @pl.when(pl.program_id(0) == 0)
        def _():
            part_ref[...] = jnp.zeros_like(part_ref)

        g1, sc2 = ada_ref[2:3, :], ada_ref[4:5, :]
        h2b = h2_ref[...]
        dffn = dffn_ref[...]
        dh2 = jnp.zeros((tm, D), F32)
        for j in range(4):
            gate = _mm(h2b, wi_ref[j])
            up = _mm(h2b, wi_ref[j + 4])
            dact = _nt(dffn, wf_ref[j * FF_BLK:(j + 1) * FF_BLK, :])
            sg = jax.nn.sigmoid(gate)
            dgate = (dact * up * (sg * (1.0 + gate * (1.0 - sg)))).astype(BF)
            dup = (dact * (gate * sg)).astype(BF)
            dgu_ref[j] = dgate
            dgu_ref[j + 4] = dup
            dh2 = dh2 + _nt(dgate, wi_ref[j]) + _nt(dup, wi_ref[j + 4])
        x1 = x1_ref[...]
        xh2, rstd2 = _rms(x1)
        xn2 = xh2 * n2_ref[...]
        dxn2 = dh2 * (1.0 + sc2)
        dx1 = dx2_ref[...] + _rms_bwd(dxn2 * n2_ref[...], xh2, rstd2)
        dx1_ref[...] = dx1
        mix = _mm(y_ref[...], wo_ref[...])
        dmix = (g1 * dx1).astype(BF)
        dmix_ref[...] = dmix
        dycat_ref[...] = _nt(dmix, wo_ref[...])
        part_ref[0:1, :] += _colsum(dh2)
        part_ref[1:2, :] += _colsum(dh2 * xn2)
        part_ref[2:3, :] += _colsum(dxn2 * xh2)
        part_ref[3:4, :] += _colsum(dx1 * mix)

    full = lambda a: pl.BlockSpec(a.shape, lambda i, nd=a.ndim: (0,) * nd)
    row = lambda w: pl.BlockSpec((tm, w), lambda i: (i, 0))
    return pl.pallas_call(
        body, name="bwd_ffn", grid=(t // tm,),
        out_shape=(jax.ShapeDtypeStruct((N_DEV, t, FF_BLK), BF), jax.ShapeDtypeStruct((t, D), F32),
                   jax.ShapeDtypeStruct((t, D), BF), jax.ShapeDtypeStruct((t, D), F32), jax.ShapeDtypeStruct((8, D), F32)),
        in_specs=[row(D), row(D), row(D), row(D), row(D), full(ada), full(n2w), _whole_vmem(), _whole_vmem(), _whole_vmem()],
        out_specs=(pl.BlockSpec((N_DEV, tm, FF_BLK), lambda i: (0, i, 0)), row(D), row(D), row(D),
                   pl.BlockSpec((8, D), lambda i: (0, 0))),
        compiler_params=_params(dimension_semantics=("arbitrary",)),
    )(x1, h2, dffn, dx2, ycat, ada, n2w, w_out, w_fi, w_fo)


def _bwd_in(x, dproj, dx1, ada, n1w, w_in_st, to_scatter):
    t = x.shape[0]
    tm = _row_tile(t, 1024)
    n_t = t // tm
    ns = len(to_scatter)

    def body(*refs):
        x_ref, dp_ref, dx1_ref, ada_ref, n1_ref, wst_ref = refs[:6]
        s_in = refs[6:6 + ns]
        gx_ref, part_ref = refs[6 + ns:8 + ns]
        s_out = refs[8 + ns:8 + 2 * ns]
        w_ref, sems = refs[8 + 2 * ns:10 + 2 * ns]
        ex = _Exchange(s_in, s_out, *refs[10 + 2 * ns:], True)

        @pl.when(pl.program_id(0) == 0)
        def _():
            ex.start()
            part_ref[...] = jnp.zeros_like(part_ref)
            _load_columns(wst_ref, w_ref, sems)

        dh1 = _nt(dp_ref[...], w_ref[...])
        xh, rstd = _rms(x_ref[...])
        xn = xh * n1_ref[...]
        dxn = dh1 * (1.0 + ada_ref[1:2, :])
        gx_ref[...] = dx1_ref[...] + _rms_bwd(dxn * n1_ref[...], xh, rstd)
        part_ref[0:1, :] += _colsum(dh1)
        part_ref[1:2, :] += _colsum(dh1 * xn)
        part_ref[2:3, :] += _colsum(dxn * xh)

        @pl.when(pl.program_id(0) == n_t - 1)
        def _():
            ex.wait()

    full = lambda a: pl.BlockSpec(a.shape, lambda i, nd=a.ndim: (0,) * nd)
    row = lambda w: pl.BlockSpec((tm, w), lambda i: (i, 0))
    outs = pl.pallas_call(
        body, name="bwd_in", grid=(n_t,),
        out_shape=(jax.ShapeDtypeStruct((t, D), F32), jax.ShapeDtypeStruct((8, D), F32),
                   *_exchange_out_shapes(to_scatter, True)),
        in_specs=[row(D), row(D_IN), row(D), full(ada), full(n1w), _any()] + [_any() for _ in to_scatter],
        out_specs=(row(D), pl.BlockSpec((8, D), lambda i: (0, 0)), *[_any() for _ in to_scatter]),
        scratch_shapes=[pltpu.VMEM((D, D_IN), BF), pltpu.SemaphoreType.DMA((N_DEV,))] + _exchange_sems(ns),
        compiler_params=_params(dimension_semantics=("arbitrary",), has_side_effects=True),
    )(x, dproj, dx1, ada, n1w, w_in_st, *to_scatter)
    return outs[0], outs[1], outs[2:]


def _wgrad(a, b, name, a_spec, b_spec, out_shape, out_spec, grid, acc_shape, split=1):
    n_t = grid[1]

    def body(a_ref, b_ref, o_ref, acc):
        @pl.when(pl.program_id(1) == 0)
        def _():
            acc[...] = jnp.zeros_like(acc)

        acc[...] += _tn(a_ref[...], b_ref[...])

        @pl.when(pl.program_id(1) == n_t - 1)
        def _():
            if split == 1:
                o_ref[...] = acc[...].astype(BF)
            else:
                w = acc_shape[1] // split
                for s in range(split):
                    o_ref[s] = acc[:, s * w:(s + 1) * w].astype(BF)

    return pl.pallas_call(
        body, name=name, grid=grid,
        out_shape=jax.ShapeDtypeStruct(out_shape, BF),
        in_specs=[a_spec, b_spec], out_specs=out_spec,
        scratch_shapes=[pltpu.VMEM(acc_shape, F32)],
        compiler_params=_params(dimension_semantics=("arbitrary", "arbitrary")),
    )(a, b)


def _adamw_math(w, g, m, v):
    m = ADAM_B1 * m + (1.0 - ADAM_B1) * g
    v = ADAM_B2 * v + (1.0 - ADAM_B2) * (g * g)
    m_hat = m / (1.0 - ADAM_B1 ** ADAM_STEP)
    v_hat = v / (1.0 - ADAM_B2 ** ADAM_STEP)
    delta = -ADAM_LR * (m_hat / (jnp.sqrt(v_hat) + ADAM_EPS) + ADAM_WD * w)
    return delta, m, v


def _adamw_recv(w, m, v, recv, name, tr):
    r, c = w.shape

    def body(w_ref, m_ref, v_ref, r_ref, g_ref, d_ref, nm_ref, nv_ref):
        g = r_ref[0].astype(F32)
        for k in range(1, N_DEV):
            g = g + r_ref[k].astype(F32)
        g_ref[...] = g
        d_ref[...], nm_ref[...], nv_ref[...] = _adamw_math(w_ref[...], g, m_ref[...], v_ref[...])

    row = pl.BlockSpec((tr, c), lambda i: (i, 0))
    return pl.pallas_call(
        body, name=name, grid=(r // tr,),
        out_shape=tuple(jax.ShapeDtypeStruct((r, c), F32) for _ in range(4)),
        in_specs=[row, row, row, pl.BlockSpec((N_DEV, tr, c), lambda i: (0, i, 0))],
        out_specs=(row, row, row, row),
        compiler_params=_params(dimension_semantics=("arbitrary",)),
    )(w, m, v, recv)


def _adamw_ada(w, m, v, cact, dada_cols):
    r, c = w.shape
    tr = 256

    def body(w_ref, m_ref, v_ref, ca_ref, da_ref, g_ref, d_ref, nm_ref, nv_ref):
        g = _tn(ca_ref[...].astype(BF), da_ref[...].astype(BF))
        g_ref[...] = g
        d_ref[...], nm_ref[...], nv_ref[...] = _adamw_math(w_ref[...], g, m_ref[...], v_ref[...])

    row = pl.BlockSpec((tr, c), lambda i: (i, 0))
    return pl.pallas_call(
        body, name="adamw_ada", grid=(r // tr,),
        out_shape=tuple(jax.ShapeDtypeStruct((r, c), F32) for _ in range(4)),
        in_specs=[row, row, row, pl.BlockSpec((N_DEV, tr), lambda i: (0, i)), pl.BlockSpec(dada_cols.shape, lambda i: (0, 0))],
        out_specs=(row, row, row, row),
        compiler_params=_params(dimension_semantics=("arbitrary",)),
    )(w, m, v, cact, dada_cols)


def _adamw_small(gathered, wmv):
    n_g = len(gathered)
    n_p = len(wmv)
    flat = [a for trip in wmv for a in trip]

    def body(*refs):
        g_refs = refs[:n_g]
        p_refs = refs[n_g:n_g + 3 * n_p]
        o_refs = refs[n_g + 3 * n_p:]

        def total(ref):
            s = ref[0]
            for k in range(1, N_DEV):
                s = s + ref[k]
            return s

        f3, b3, b1, dws, dbs, dlnw, dlnb, dlo, dgn = [total(r) for r in g_refs]
        dada_rows = [b1[0:1], b1[1:2], b3[3:4], b3[0:1], b3[1:2], f3[0:1]]
        for r, g in enumerate(dada_rows):
            w, m, v = p_refs[0][r:r + 1, :], p_refs[1][r:r + 1, :], p_refs[2][r:r + 1, :]
            o_refs[0][r:r + 1, :] = g
            o_refs[1][r:r + 1, :], o_refs[2][r:r + 1, :], o_refs[3][r:r + 1, :] = _adamw_math(w, g, m, v)
        grads = [None, b1[2:3], dws, dbs, dlnw, dlnb, dlo, dgn, b3[2:3], f3[1:2]]
        for i, g in enumerate(grads):
            if g is None:
                continue
            w, m, v = p_refs[3 * i][...], p_refs[3 * i + 1][...], p_refs[3 * i + 2][...]
            o_refs[4 * i][...] = g
            o_refs[4 * i + 1][...], o_refs[4 * i + 2][...], o_refs[4 * i + 3][...] = _adamw_math(w, g, m, v)
        o_refs[4 * n_p][...] = jnp.zeros((8, 128), F32) + f3[2:3, 0:128]

    out_shape = []
    for w, _, _ in wmv:
        out_shape += [jax.ShapeDtypeStruct(w.shape, F32)] * 4
    out_shape.append(jax.ShapeDtypeStruct((8, 128), F32))
    n_in = n_g + 3 * n_p
    return pl.pallas_call(
        body, name="adamw_small",
        out_shape=tuple(out_shape),
        in_specs=[_whole_vmem()] * n_in, out_specs=tuple(_whole_vmem() for _ in out_shape),
        compiler_params=_params(),
    )(*gathered, *flat)


def kernel(x, c, w_ada, b_ada, norm1_w, w_in, w_s, b_s, v_ln_w, v_ln_b, lower_bounds, gn_w, w_out, norm2_w, w_ffn_in, w_ffn_out, final_norm_w, loss_target, m_w_ada, m_b_ada, m_norm1_w, m_w_in, m_w_s, m_b_s, m_v_ln_w, m_v_ln_b, m_lower_bounds, m_gn_w, m_w_out, m_norm2_w, m_w_ffn_in, m_w_ffn_out, m_final_norm_w, v_w_ada, v_b_ada, v_norm1_w, v_w_in, v_w_s, v_b_s, v_v_ln_w, v_v_ln_b, v_lower_bounds, v_gn_w, v_w_out, v_norm2_w, v_w_ffn_in, v_w_ffn_out, v_final_norm_w):
    me = 4 * lax.axis_index("x") + 2 * lax.axis_index("y") + lax.axis_index("c")
    t = x.shape[1]
    x2d = x.reshape(t, D)
    tgt = loss_target.reshape(t, D)
    ada_cols = w_ada.shape[2]

    win_b, wout_b, wfi_b, wfo_b = _cast_bf16([w_in[0], w_out[0], w_ffn_in[0], w_ffn_out[0]], "cast_weights")
    win_st, c_st = _exchange([win_b, c], "gather_w_in", False, False)
    c_all = c_st.reshape(N_DEV, D)

    b_cols = lax.dynamic_slice(b_ada, (0, me * ada_cols), (1, ada_cols))
    ada_blk, cact = _ada_block(c_all, w_ada[0], b_cols)
    (ada_st,) = _exchange([ada_blk], "gather_ada", False, True)
    ada = lax.dynamic_index_in_dim(ada_st, me, axis=1, keepdims=False).reshape(6, D)

    tables = _decay_tables()
    ws3 = w_s[0]
    bs_t = b_s[0].T

    proj, h1 = _fwd_in(x2d, ada, norm1_w, win_st)
    ycat, states, (wout_st, wfi_st, wfo_st) = _fwd_mix(proj, ws3, bs_t, v_ln_w, v_ln_b, lower_bounds, gn_w, tables,
                                                       [wout_b, wfi_b, wfo_b])
    w_out_full = wout_st.reshape(D, D)
    w_fo_full = wfo_st.reshape(D_FF, D)
    x1, h2, act, dffn, dx2, part_f = _fwd_ffn(x2d, ycat, tgt, ada, norm2_w, final_norm_w.reshape(1, D),
                                              w_out_full, wfi_st, w_fo_full)
    dgu, dx1, dmix, dycat, part_b3 = _bwd_ffn(x1, h2, dffn, dx2, ycat, ada, norm2_w, w_out_full, wfi_st, w_fo_full)
    tk = _row_tile(t, 2048)
    n_t = t // tk
    win_cols = D_IN // N_DEV
    dwout = _wgrad(ycat, dmix, "wgrad_out",
                   pl.BlockSpec((tk, D), lambda j, i: (i, 0)), pl.BlockSpec((tk, D), lambda j, i: (i, 0)),
                   (D, D), pl.BlockSpec((D, D), lambda j, i: (0, 0)), (1, n_t), (D, D))
    dwfi = _wgrad(h2, dgu, "wgrad_ffn_in",
                  pl.BlockSpec((tk, D), lambda j, i: (i, 0)), pl.BlockSpec((None, tk, FF_BLK), lambda j, i: (j, i, 0)),
                  (N_DEV, D, FF_BLK), pl.BlockSpec((None, D, FF_BLK), lambda j, i: (j, 0, 0)), (N_DEV, n_t), (D, FF_BLK))
    dwfo = _wgrad(act, dffn, "wgrad_ffn_out",
                  pl.BlockSpec((None, tk, FF_BLK), lambda j, i: (j, i, 0)), pl.BlockSpec((tk, D), lambda j, i: (i, 0)),
                  (4, FF_BLK, D), pl.BlockSpec((None, FF_BLK, D), lambda j, i: (j, 0, 0)), (4, n_t), (FF_BLK, D))
    dproj, (dws, dbs_t, dlnw, dlnb, dlower, dgnw), (r_out, r_fi, r_fo) = _bwd_mix(
        proj, dycat, states, ws3, bs_t, v_ln_w, v_ln_b, lower_bounds, gn_w, tables,
        [dwout.reshape(N_DEV, D // N_DEV, D), dwfi, dwfo.reshape(N_DEV, D_FF // N_DEV, D)])
    dwin = _wgrad(h1, dproj, "wgrad_in",
                  pl.BlockSpec((tk, D), lambda j, i: (i, 0)), pl.BlockSpec((tk, 4 * win_cols), lambda j, i: (i, j)),
                  (N_DEV, D, win_cols), pl.BlockSpec((4, D, win_cols), lambda j, i: (j, 0, 0)), (N_DEV // 4, n_t),
                  (D, 4 * win_cols), split=4)
    grad_x, part_b1, (r_in,) = _bwd_in(x2d, dproj, dx1, ada, norm1_w, win_st, [dwin])

    g_w_in, d_w_in, nm_w_in, nv_w_in = _adamw_recv(w_in[0], m_w_in[0], v_w_in[0], r_in, "adamw_w_in", 256)
    g_w_out, d_w_out, nm_w_out, nv_w_out = _adamw_recv(w_out[0], m_w_out[0], v_w_out[0], r_out, "adamw_w_out", 128)
    g_w_fi, d_w_fi, nm_w_fi, nv_w_fi = _adamw_recv(w_ffn_in[0], m_w_ffn_in[0], v_w_ffn_in[0], r_fi, "adamw_w_ffn_in", 256)
    g_w_fo, d_w_fo, nm_w_fo, nv_w_fo = _adamw_recv(w_ffn_out[0], m_w_ffn_out[0], v_w_ffn_out[0], r_fo, "adamw_w_ffn_out", 176)

    gathered = _exchange([part_f, part_b3, part_b1, dws, dbs_t, dlnw, dlnb, dlower, dgnw], "gather_small", False, True)
    f3_all, b3_all, b1_all = gathered[0], gathered[1], gathered[2]
    dada_all = jnp.stack([b1_all[:, 0], b1_all[:, 1], b3_all[:, 3], b3_all[:, 0], b3_all[:, 1], f3_all[:, 0]], axis=1)
    dada_cols = lax.dynamic_slice(dada_all.reshape(N_DEV, 6 * D), (0, me * ada_cols), (N_DEV, ada_cols))
    g_w_ada, d_w_ada, nm_w_ada, nv_w_ada = _adamw_ada(w_ada[0], m_w_ada[0], v_w_ada[0], cact, dada_cols)

    r6 = lambda a: a.reshape(6, D)
    r1 = lambda a: a.reshape(1, D)
    tr = lambda a: a[0].T
    wmv = [
        (r6(b_ada), r6(m_b_ada), r6(v_b_ada)),
        (norm1_w, m_norm1_w, v_norm1_w),
        (w_s[0], m_w_s[0], v_w_s[0]),
        (tr(b_s), tr(m_b_s), tr(v_b_s)),
        (v_ln_w, m_v_ln_w, v_v_ln_w),
        (v_ln_b, m_v_ln_b, v_v_ln_b),
        (lower_bounds, m_lower_bounds, v_lower_bounds),
        (gn_w, m_gn_w, v_gn_w),
        (norm2_w, m_norm2_w, v_norm2_w),
        (r1(final_norm_w), r1(m_final_norm_w), r1(v_final_norm_w)),
    ]
    small = _adamw_small(gathered, wmv)
    loss = small[-1][0, 0]

    def unshape(i, a):
        if i == 0:
            return a.reshape(1, 6 * D)
        if i == 2:
            return a.reshape(1, NH, BLK, BLK)
        if i == 3:
            return a.T.reshape(1, NH, BLK)
        if i == 9:
            return a.reshape(D)
        return a

    def small_out(kind):
        return [unshape(i, small[4 * i + kind]) for i in range(len(wmv))]

    e3 = lambda a: a[None]
    big = {
        0: (e3(g_w_ada), e3(g_w_in), e3(g_w_out), e3(g_w_fi), e3(g_w_fo)),
        1: (e3(d_w_ada), e3(d_w_in), e3(d_w_out), e3(d_w_fi), e3(d_w_fo)),
        2: (e3(nm_w_ada), e3(nm_w_in), e3(nm_w_out), e3(nm_w_fi), e3(nm_w_fo)),
        3: (e3(nv_w_ada), e3(nv_w_in), e3(nv_w_out), e3(nv_w_fi), e3(nv_w_fo)),
    }

    def ordered(kind):
        s = small_out(kind)
        b_ = big[kind]
        return [b_[0], s[0], s[1], b_[1], s[2], s[3], s[4], s[5], s[6], s[7], b_[2], s[8], b_[3], b_[4], s[9]]

    return (loss, grad_x.reshape(1, t, D), *ordered(0), *ordered(1), *ordered(2), *ordered(3))
```

```python
import functools

import numpy as np
import jax
import jax.numpy as jnp
from jax import lax
from jax.experimental import pallas as pl
from jax.experimental.pallas import tpu as pltpu

F32 = jnp.float32
BF = jnp.bfloat16
MESH = pl.DeviceIdType.MESH

N_DEV = 8
D = 1024
D_IN = 3072
D_FF = 2816
FF_BLK = D_FF // 4
CH = 64
BLK = 128
MIX_TILE = 256
NH = 4
HD = 128
EPS = 1e-6
LEVELS = (32, 16, 8, 4, 2, 1)
N_TAB = 2 + len(LEVELS)

ADAM_LR = 0.001
ADAM_B1 = 0.9
ADAM_B2 = 0.999
ADAM_EPS = 1e-08
ADAM_WD = 0.01
ADAM_STEP = 10

VMEM_LIMIT = 56 * 1024 * 1024


def _params(**kw):
    return pltpu.CompilerParams(vmem_limit_bytes=VMEM_LIMIT, **kw)


def _whole_vmem():
    return pl.BlockSpec(memory_space=pltpu.VMEM)


def _any():
    return pl.BlockSpec(memory_space=pl.ANY)


class _Exchange:
    def __init__(self, ins, outs, send_sems, recv_sems, local_sems, scatter):
        self.ins, self.outs, self.scatter = ins, outs, scatter
        self.send_sems, self.recv_sems, self.local_sems = send_sems, recv_sems, local_sems
        x, y, c = lax.axis_index("x"), lax.axis_index("y"), lax.axis_index("c")
        self.me = 4 * x + 2 * y + c
        self.peers = []
        for k in range(1, N_DEV):
            peer = (1 - x if (k >> 2) & 1 else x, 1 - y if (k >> 1) & 1 else y, 1 - c if k & 1 else c)
            self.peers.append((peer, 4 * peer[0] + 2 * peer[1] + peer[2]))

    def _src(self, a, idx):
        return self.ins[a].at[idx] if self.scatter else self.ins[a]

    def _local(self):
        return [pltpu.make_async_copy(self._src(a, self.me), self.outs[a].at[self.me], self.local_sems.at[a])
                for a in range(len(self.ins))]

    def _remote(self, a, k, dst_slot):
        peer, peer_idx = self.peers[k]
        return pltpu.make_async_remote_copy(
            src_ref=self._src(a, peer_idx), dst_ref=self.outs[a].at[dst_slot],
            send_sem=self.send_sems.at[a, k], recv_sem=self.recv_sems.at[a, k],
            device_id=peer, device_id_type=MESH)

    def start(self):
        for cp in self._local():
            cp.start()
        for k in range(N_DEV - 1):
            for a in range(len(self.ins)):
                self._remote(a, k, self.me).start()

    def wait(self):
        for k in range(N_DEV - 1):
            for a in range(len(self.ins)):
                self._remote(a, k, self.peers[k][1]).wait()
        for cp in self._local():
            cp.wait()


class _Gather2:
    def __init__(self, ins, outs, send_sems, recv_sems, local_sems):
        self.ins, self.outs = ins, outs
        self.send_sems, self.recv_sems, self.local_sems = send_sems, recv_sems, local_sems
        x, y, c = lax.axis_index("x"), lax.axis_index("y"), lax.axis_index("c")
        self.c = c
        self.me = 4 * x + 2 * y + c
        self.sibling = (x, y, 1 - c)
        self.chips = [(1 - x, y), (x, 1 - y), (1 - x, 1 - y)]

    @staticmethod
    def _idx(px, py, pc):
        return 4 * px + 2 * py + pc

    def _copy(self, a, k, slot, to, own):
        src = self.ins[a] if own else self.outs[a].at[slot]
        return pltpu.make_async_remote_copy(
            src_ref=src, dst_ref=self.outs[a].at[slot],
            send_sem=self.send_sems.at[a, k], recv_sem=self.recv_sems.at[a, k],
            device_id=to, device_id_type=MESH)

    def _local(self):
        return [pltpu.make_async_copy(self.ins[a], self.outs[a].at[self.me], self.local_sems.at[a])
                for a in range(len(self.ins))]

    def start(self):
        for cp in self._local():
            cp.start()
        for a in range(len(self.ins)):
            self._copy(a, 0, self.me, self.sibling, True).start()
            for j, chip in enumerate(self.chips):
                self._copy(a, 1 + j, self.me, (*chip, self.c), True).start()

    def forward(self):
        for j, chip in enumerate(self.chips):
            for a in range(len(self.ins)):
                slot = self._idx(*chip, self.c)
                self._copy(a, 1 + j, slot, (*chip, self.c), True).wait_recv()
                self._copy(a, 4 + j, slot, self.sibling, False).start()

    def finish(self):
        for a in range(len(self.ins)):
            self._copy(a, 0, self._idx(*self.sibling), self.sibling, True).wait_recv()
            for j, chip in enumerate(self.chips):
                self._copy(a, 4 + j, self._idx(*chip, 1 - self.c), self.sibling, False).wait_recv()
            self._copy(a, 0, self.me, self.sibling, True).wait_send()
            for j, chip in enumerate(self.chips):
                self._copy(a, 1 + j, self.me, (*chip, self.c), True).wait_send()
                self._copy(a, 4 + j, self._idx(*chip, self.c), self.sibling, False).wait_send()
        for cp in self._local():
            cp.wait()


def _exchange_sems(n):
    return [pltpu.SemaphoreType.DMA((n, N_DEV - 1)), pltpu.SemaphoreType.DMA((n, N_DEV - 1)), pltpu.SemaphoreType.DMA((n,))]


def _exchange_out_shapes(arrays, scatter):
    return [jax.ShapeDtypeStruct(a.shape if scatter else (N_DEV,) + a.shape, a.dtype) for a in arrays]


def _exchange(arrays, name, scatter, in_vmem, two_level=False):
    n = len(arrays)

    def body(*refs):
        if two_level:
            ga = _Gather2(refs[:n], refs[n:2 * n], *refs[2 * n:])
            ga.start()
            ga.forward()
            ga.finish()
        else:
            ex = _Exchange(refs[:n], refs[n:2 * n], *refs[2 * n:], scatter)
            ex.start()
            ex.wait()

    spec = _whole_vmem if in_vmem else _any
    return pl.pallas_call(
        body, name=name,
        out_shape=tuple(_exchange_out_shapes(arrays, scatter)),
        in_specs=[spec() for _ in arrays],
        out_specs=tuple(spec() for _ in arrays),
        scratch_shapes=_exchange_sems(n),
        compiler_params=_params(has_side_effects=True),
    )(*arrays)


def _cast_bf16(arrays, name):
    n = len(arrays)

    def body(*refs):
        for i in range(n):
            refs[n + i][...] = refs[i][...].astype(BF)

    return pl.pallas_call(
        body, name=name,
        out_shape=tuple(jax.ShapeDtypeStruct(a.shape, BF) for a in arrays),
        in_specs=[_whole_vmem() for _ in arrays],
        out_specs=tuple(_whole_vmem() for _ in arrays),
        compiler_params=_params(),
    )(*arrays)


def _load_columns(stacked_ref, full_ref, sems):
    c = stacked_ref.shape[2]
    cps = [pltpu.make_async_copy(stacked_ref.at[d], full_ref.at[:, pl.ds(d * c, c)], sems.at[d]) for d in range(N_DEV)]
    for cp in cps:
        cp.start()
    for cp in cps:
        cp.wait()


def _silu(v):
    return v * jax.nn.sigmoid(v)


def _gelu(v):
    return 0.5 * v * (1.0 + lax.erf(v * 0.7071067811865476))


def _nt(a, b):
    return lax.dot_general(a, b, (((1,), (1,)), ((), ())), preferred_element_type=F32)


def _tn(a, b):
    return lax.dot_general(a, b, (((0,), (0,)), ((), ())), preferred_element_type=F32)


def _mm(a, b):
    return jnp.dot(a, b, preferred_element_type=F32)


def _ada_block(c_all, w_ada, b_cols):
    def body(c_ref, w_ref, b_ref, ada_ref, cact_ref):
        ca = _silu(c_ref[...])
        cact_ref[...] = ca
        ada_ref[...] = _mm(ca.astype(BF), w_ref[...].astype(BF)) + b_ref[...]

    return pl.pallas_call(
        body, name="ada_block",
        out_shape=(jax.ShapeDtypeStruct((N_DEV, w_ada.shape[1]), F32), jax.ShapeDtypeStruct(c_all.shape, F32)),
        in_specs=[_whole_vmem()] * 3, out_specs=(_whole_vmem(), _whole_vmem()),
        compiler_params=_params(),
    )(c_all, w_ada, b_cols)


def _decay_tables():
    t = np.arange(CH)
    j = np.arange(CH)
    mats = [(j[None, :] <= t[:, None]), (j[None, :] > t[:, None])]
    masks = []
    for h in LEVELS:
        m = (t // (2 * h)) * (2 * h) + h
        upper = t >= m
        eq = (j[None, :] >= m[:, None]) & (j[None, :] <= t[:, None]) & upper[:, None]
        ek = (j[None, :] > t[:, None]) & (j[None, :] <= (m[:, None] - 1)) & (~upper)[:, None]
        mats.append(eq | ek)
        same = (t[:, None] // (2 * h)) == (t[None, :] // (2 * h))
        masks.append(same & upper[:, None] & (~upper)[None, :])
    masks.append(np.eye(CH, dtype=bool))
    m_all = np.concatenate(mats, axis=0).astype(np.float32)
    cid = np.arange(BLK) // CH
    gmask = (cid[:, None] >= cid[None, :]).astype(np.float32)
    return jnp.asarray(m_all, BF), jnp.asarray(np.stack(masks).astype(np.float32)), jnp.asarray(gmask)


def _split3(v):
    v1 = v.astype(BF)
    r1 = v - v1.astype(F32)
    v2 = r1.astype(BF)
    v3 = (r1 - v2.astype(F32)).astype(BF)
    return v1, v2, v3


@jax.custom_vjp
def _exact_mm(m, v):
    p1, p2, p3 = _split3(v)
    return _mm(m, p1) + _mm(m, p2) + _mm(m, p3)


def _exact_mm_fwd(m, v):
    return _exact_mm(m, v), m


def _exact_mm_bwd(m, g):
    p1, p2, p3 = _split3(g)
    return jnp.zeros_like(m), _tn(m, p1) + _tn(m, p2) + _tn(m, p3)


_exact_mm.defvjp(_exact_mm_fwd, _exact_mm_bwd)


def _mix_tile(proj, state, w_s, b_s_t, ln_w, ln_b, lower, gn_w, m_all, masks, gmask):
    mt = proj.shape[0]
    u = proj[:, 0:512]
    v = proj[:, 512:1024]
    q = proj[:, 1024:1536]
    fl = proj[:, 1536:2048]
    inp = proj[:, 2048:2560]
    g = proj[:, 2560:3072]

    ug = _gelu(u)
    vg = _gelu(v)
    mu = jnp.mean(vg, axis=-1, keepdims=True)
    vc = vg - mu
    var = jnp.mean(vc * vc, axis=-1, keepdims=True)
    vnb = (vc * lax.rsqrt(var + EPS) * ln_w + ln_b).astype(BF)
    wsm = [(w_s[h] * gmask).astype(BF) for h in range(NH)]
    ya = [[None] * NH for _ in range(mt // BLK)]
    for bi in range(mt // BLK):
        rows = slice(bi * BLK, (bi + 1) * BLK)
        for h in range(NH):
            cols = slice(h * HD, (h + 1) * HD)
            ya[bi][h] = ug[rows, cols] * (_mm(wsm[h], vnb[rows, cols]) + b_s_t[:, h:h + 1])
    ya_full = jnp.concatenate([jnp.concatenate(r, axis=1) for r in ya], axis=0)

    l0 = lower[0:1, :]
    l1 = lower[1:2, :]
    mx = jnp.maximum(l0, l1)
    e0 = jnp.exp(l0 - mx)
    e1 = jnp.exp(l1 - mx)
    lb = e0 / (e0 + e1)
    qf = _silu(q)
    f = lb + (1.0 - lb) * jax.nn.sigmoid(fl)
    logf = jnp.log(f)
    kk = 1.0 - f
    gate = _silu(g)
    nl = len(LEVELS)
    st = list(state)
    yb = [[None] * NH for _ in range(mt // CH)]
    for ci in range(mt // CH):
        rows = slice(ci * CH, (ci + 1) * CH)
        ex = jnp.exp(_exact_mm(m_all, logf[rows]))
        for h in range(NH):
            cols = slice(h * HD, (h + 1) * HD)
            qh = qf[rows, cols]
            kh = kk[rows, cols]
            vh = inp[rows, cols].astype(BF)
            eb = ex[0:CH, cols]
            einv = ex[CH:2 * CH, cols]
            inter = _nt((qh * eb).astype(BF), st[h].astype(BF))
            attn = masks[nl] * _nt(qh.astype(BF), kh.astype(BF))
            for li in range(nl):
                el = ex[(2 + li) * CH:(3 + li) * CH, cols]
                attn = attn + masks[li] * _nt((qh * el).astype(BF), (kh * el).astype(BF))
            o = inter + _mm(attn.astype(BF), vh)
            st[h] = st[h] * eb[CH - 1:CH, :] + _tn(vh, (kh * einv).astype(BF))
            rs = lax.rsqrt(jnp.mean(o * o, axis=-1, keepdims=True) + EPS)
            yb[ci][h] = o * rs * gn_w * gate[rows, cols]
    yb_full = jnp.concatenate([jnp.concatenate(r, axis=1) for r in yb], axis=0)
    return jnp.concatenate([ya_full, yb_full], axis=1), tuple(st)


def _fwd_mix(proj, w_s, b_s_t, ln_w, ln_b, lower, gn_w, tables, to_gather):
    t = proj.shape[0]
    mt = _row_tile(t, MIX_TILE)
    nt_ = t // mt
    m_all, masks, gmask = tables
    ng = len(to_gather)

    def body(*refs):
        proj_ref, ws_ref, bs_ref, lw_ref, lb_ref, lo_ref, gn_ref, m_ref, mk_ref, gm_ref = refs[:10]
        g_in = refs[10:10 + ng]
        y_ref, st_ref = refs[10 + ng:12 + ng]
        g_out = refs[12 + ng:12 + 2 * ng]
        state = refs[12 + 2 * ng]
        ga = _Gather2(g_in, g_out, *refs[13 + 2 * ng:])

        @pl.when(pl.program_id(0) == 0)
        def _():
            state[...] = jnp.zeros_like(state)
            ga.start()

        @pl.when(pl.program_id(0) == (3 * nt_) // 4)
        def _():
            ga.forward()

        st = tuple(state[h] for h in range(NH))
        for h in range(NH):
            st_ref[h] = st[h]
        y, new = _mix_tile(proj_ref[...], st, ws_ref[...], bs_ref[...], lw_ref[...], lb_ref[...], lo_ref[...], gn_ref[...],
                           m_ref[...], mk_ref[...], gm_ref[...])
        y_ref[...] = y.astype(BF)
        for h in range(NH):
            state[h] = new[h]

        @pl.when(pl.program_id(0) == nt_ - 1)
        def _():
            ga.finish()

    full = lambda a: pl.BlockSpec(a.shape, lambda i, nd=a.ndim: (0,) * nd)
    outs = pl.pallas_call(
        body, name="fwd_mix", grid=(nt_,),
        out_shape=(jax.ShapeDtypeStruct((t, D), BF), jax.ShapeDtypeStruct((nt_, NH, HD, HD), F32),
                   *_exchange_out_shapes(to_gather, False)),
        in_specs=[pl.BlockSpec((mt, D_IN), lambda i: (i, 0)), full(w_s), full(b_s_t), full(ln_w), full(ln_b), full(lower),
                  full(gn_w), full(m_all), full(masks), full(gmask)] + [_any() for _ in to_gather],
        out_specs=(pl.BlockSpec((mt, D), lambda i: (i, 0)), pl.BlockSpec((None, NH, HD, HD), lambda i: (i, 0, 0, 0)),
                   *[_any() for _ in to_gather]),
        scratch_shapes=[pltpu.VMEM((NH, HD, HD), F32)] + _exchange_sems(ng),
        compiler_params=_params(dimension_semantics=("arbitrary",), has_side_effects=True),
    )(proj, w_s, b_s_t, ln_w, ln_b, lower, gn_w, m_all, masks, gmask, *to_gather)
    return outs[0], outs[1], outs[2:]


def _bwd_mix(proj, dycat, states, w_s, b_s_t, ln_w, ln_b, lower, gn_w, tables, to_scatter):
    t = proj.shape[0]
    mt = _row_tile(t, MIX_TILE)
    nt_ = t // mt
    m_all, masks, gmask = tables
    ns = len(to_scatter)

    def body(*refs):
        proj_ref, dy_ref, st_ref, ws_ref, bs_ref, lw_ref, lb_ref, lo_ref, gn_ref, m_ref, mk_ref, gm_ref = refs[:12]
        s_in = refs[12:12 + ns]
        dproj_ref = refs[12 + ns]
        acc = refs[13 + ns:19 + ns]
        s_out = refs[19 + ns:19 + 2 * ns]
        dstate = refs[19 + 2 * ns]
        ex = _Exchange(s_in, s_out, *refs[20 + 2 * ns:], True)

        @pl.when(pl.program_id(0) == 0)
        def _():
            dstate[...] = jnp.zeros_like(dstate)
            for r in acc:
                r[...] = jnp.zeros_like(r)
            ex.start()

        def f(p, s, ws, bs, lw, lb_, lo, gn):
            return _mix_tile(p, s, ws, bs, lw, lb_, lo, gn, m_ref[...], mk_ref[...], gm_ref[...])

        st = tuple(st_ref[h] for h in range(NH))
        _, vjp = jax.vjp(f, proj_ref[...], st, ws_ref[...], bs_ref[...], lw_ref[...], lb_ref[...], lo_ref[...], gn_ref[...])
        grads = vjp((dy_ref[...], tuple(dstate[h] for h in range(NH))))
        dproj_ref[...] = grads[0].astype(BF)
        for h in range(NH):
            dstate[h] = grads[1][h]
        for r, gval in zip(acc, grads[2:]):
            r[...] += gval

        @pl.when(pl.program_id(0) == nt_ - 1)
        def _():
            ex.wait()

    full = lambda a: pl.BlockSpec(a.shape, lambda i, nd=a.ndim: (0,) * nd)
    rev = lambda i: (nt_ - 1 - i, 0)
    smalls = (w_s, b_s_t, ln_w, ln_b, lower, gn_w)
    outs = pl.pallas_call(
        body, name="bwd_mix", grid=(nt_,),
        out_shape=(jax.ShapeDtypeStruct((t, D_IN), BF), *[jax.ShapeDtypeStruct(a.shape, F32) for a in smalls],
                   *_exchange_out_shapes(to_scatter, True)),
        in_specs=[pl.BlockSpec((mt, D_IN), rev), pl.BlockSpec((mt, D), rev),
                  pl.BlockSpec((None, NH, HD, HD), lambda i: (nt_ - 1 - i, 0, 0, 0))]
        + [full(a) for a in smalls] + [full(m_all), full(masks), full(gmask)] + [_any() for _ in to_scatter],
        out_specs=(pl.BlockSpec((mt, D_IN), rev), *[full(a) for a in smalls], *[_any() for _ in to_scatter]),
        scratch_shapes=[pltpu.VMEM((NH, HD, HD), F32)] + _exchange_sems(ns),
        compiler_params=_params(dimension_semantics=("arbitrary",), has_side_effects=True),
    )(proj, dycat, states, w_s, b_s_t, ln_w, ln_b, lower, gn_w, m_all, masks, gmask, *to_scatter)
    return outs[0], outs[1:7], outs[7:]


def _row_tile(t, want):
    return want if t % want == 0 else t


def _rms(v):
    rstd = lax.rsqrt(jnp.mean(v * v, axis=-1, keepdims=True) + EPS)
    return v * rstd, rstd


def _rms_bwd(dxhat, xhat, rstd):
    return rstd * (dxhat - xhat * jnp.mean(dxhat * xhat, axis=-1, keepdims=True))


def _colsum(v):
    return jnp.sum(v, axis=0, keepdims=True)


def _fwd_in(x, ada, n1w, w_in_st):
    t = x.shape[0]
    tm = _row_tile(t, 512)

    def body(x_ref, ada_ref, n1_ref, wst_ref, proj_ref, h1_ref, w_ref, sems):
        @pl.when(pl.program_id(0) == 0)
        def _():
            _load_columns(wst_ref, w_ref, sems)

        xh, _ = _rms(x_ref[...])
        h1 = (xh * n1_ref[...]) * (1.0 + ada_ref[1:2, :]) + ada_ref[0:1, :]
        h1b = h1.astype(BF)
        h1_ref[...] = h1b
        proj_ref[...] = _mm(h1b, w_ref[...])

    full = lambda a: pl.BlockSpec(a.shape, lambda i, nd=a.ndim: (0,) * nd)
    return pl.pallas_call(
        body, name="fwd_in", grid=(t // tm,),
        out_shape=(jax.ShapeDtypeStruct((t, D_IN), F32), jax.ShapeDtypeStruct((t, D), BF)),
        in_specs=[pl.BlockSpec((tm, D), lambda i: (i, 0)), full(ada), full(n1w), _any()],
        out_specs=(pl.BlockSpec((tm, D_IN), lambda i: (i, 0)), pl.BlockSpec((tm, D), lambda i: (i, 0))),
        scratch_shapes=[pltpu.VMEM((D, D_IN), BF), pltpu.SemaphoreType.DMA((N_DEV,))],
        compiler_params=_params(dimension_semantics=("arbitrary",)),
    )(x, ada, n1w, w_in_st)


def _fwd_ffn(x, ycat, tgt, ada, n2w, fw, w_out, w_fi, w_fo):
    t = x.shape[0]
    tm = _row_tile(t, 256)

    def body(x_ref, y_ref, t_ref, ada_ref, n2_ref, fw_ref, wo_ref, wi_ref, wf_ref,
             x1_ref, h2_ref, act_ref, gu_ref, dffn_ref, dx2_ref, part_ref):
        @pl.when(pl.program_id(0) == 0)
        def _():
            part_ref[...] = jnp.zeros_like(part_ref)

        g1, sh2, sc2, g2 = ada_ref[2:3, :], ada_ref[3:4, :], ada_ref[4:5, :], ada_ref[5:6, :]
        x1 = x_ref[...] + g1 * _mm(y_ref[...], wo_ref[...])
        x1_ref[...] = x1
        xh2, _ = _rms(x1)
        h2b = ((xh2 * n2_ref[...]) * (1.0 + sc2) + sh2).astype(BF)
        h2_ref[...] = h2b
        ffn = jnp.zeros((tm, D), F32)
        for j in range(4):
            gate = _mm(h2b, wi_ref[j])
            up = _mm(h2b, wi_ref[j + 4])
            gu_ref[j] = gate.astype(BF)
            gu_ref[j + 4] = up.astype(BF)
            a = (_silu(gate) * up).astype(BF)
            act_ref[j] = a
            ffn = ffn + _mm(a, wf_ref[j * FF_BLK:(j + 1) * FF_BLK, :])
        x2 = x1 + g2 * ffn
        xh3, rstd3 = _rms(x2)
        err = xh3 * fw_ref[...] - t_ref[...]
        dy = err * (1.0 / D)
        dx2 = _rms_bwd(dy * fw_ref[...], xh3, rstd3)
        dx2_ref[...] = dx2
        dffn_ref[...] = (g2 * dx2).astype(BF)
        part_ref[0:1, :] += _colsum(dx2 * ffn)
        part_ref[1:2, :] += _colsum(dy * xh3)
        part_ref[2:3, :] += jnp.zeros((1, D), F32) + (0.5 / D) * jnp.sum(err * err)

    full = lambda a: pl.BlockSpec(a.shape, lambda i, nd=a.ndim: (0,) * nd)
    row = lambda w: pl.BlockSpec((tm, w), lambda i: (i, 0))
    return pl.pallas_call(
        body, name="fwd_ffn", grid=(t // tm,),
        out_shape=(jax.ShapeDtypeStruct((t, D), F32), jax.ShapeDtypeStruct((t, D), BF),
                   jax.ShapeDtypeStruct((4, t, FF_BLK), BF), jax.ShapeDtypeStruct((N_DEV, t, FF_BLK), BF),
                   jax.ShapeDtypeStruct((t, D), BF), jax.ShapeDtypeStruct((t, D), F32), jax.ShapeDtypeStruct((8, D), F32)),
        in_specs=[row(D), row(D), row(D), full(ada), full(n2w), full(fw), _whole_vmem(), _whole_vmem(), _whole_vmem()],
        out_specs=(row(D), row(D), pl.BlockSpec((4, tm, FF_BLK), lambda i: (0, i, 0)),
                   pl.BlockSpec((N_DEV, tm, FF_BLK), lambda i: (0, i, 0)), row(D), row(D),
                   pl.BlockSpec((8, D), lambda i: (0, 0))),
        compiler_params=_params(dimension_semantics=("arbitrary",)),
    )(x, ycat, tgt, ada, n2w, fw, w_out, w_fi, w_fo)


def _bwd_ffn(x1, gu, dffn, dx2, ycat, ada, n2w, w_out, w_fi, w_fo):
    t = x1.shape[0]
    tm = _row_tile(t, 256)

    def body(x1_ref, gu_ref, dffn_ref, dx2_ref, y_ref, ada_ref, n2_ref, wo_ref, wi_ref, wf_ref,
             dgu_ref, dx1_ref, dmix_ref, dycat_ref, part_ref):
        @pl.when(pl.program_id(0) == 0)
        def _():
            part_ref[...] = jnp.zeros_like(part_ref)

        g1, sc2 = ada_ref[2:3, :], ada_ref[4:5, :]
        dffn = dffn_ref[...]
        dh2 = jnp.zeros((tm, D), F32)
        for j in range(4):
            gate = gu_ref[j].astype(F32)
            up = gu_ref[j + 4].astype(F32)
            dact = _nt(dffn, wf_ref[j * FF_BLK:(j + 1) * FF_BLK, :])
            sg = jax.nn.sigmoid(gate)
            dgate = (dact * up * (sg * (1.0 + gate * (1.0 - sg)))).astype(BF)
            dup = (dact * (gate * sg)).astype(BF)
            dgu_ref[j] = dgate
            dgu_ref[j + 4] = dup
            dh2 = dh2 + _nt(dgate, wi_ref[j]) + _nt(dup, wi_ref[j + 4])
        x1 = x1_ref[...]
        xh2, rstd2 = _rms(x1)
        xn2 = xh2 * n2_ref[...]
        dxn2 = dh2 * (1.0 + sc2)
        dx1 = dx2_ref[...] + _rms_bwd(dxn2 * n2_ref[...], xh2, rstd2)
        dx1_ref[...] = dx1
        mix = _mm(y_ref[...], wo_ref[...])
        dmix = (g1 * dx1).astype(BF)
        dmix_ref[...] = dmix
        dycat_ref[...] = _nt(dmix, wo_ref[...])
        part_ref[0:1, :] += _colsum(dh2)
        part_ref[1:2, :] += _colsum(dh2 * xn2)
        part_ref[2:3, :] += _colsum(dxn2 * xh2)
        part_ref[3:4, :] += _colsum(dx1 * mix)

    full = lambda a: pl.BlockSpec(a.shape, lambda i, nd=a.ndim: (0,) * nd)
    row = lambda w: pl.BlockSpec((tm, w), lambda i: (i, 0))
    return pl.pallas_call(
        body, name="bwd_ffn", grid=(t // tm,),
        out_shape=(jax.ShapeDtypeStruct((N_DEV, t, FF_BLK), BF), jax.ShapeDtypeStruct((t, D), F32),
                   jax.ShapeDtypeStruct((t, D), BF), jax.ShapeDtypeStruct((t, D), F32), jax.ShapeDtypeStruct((8, D), F32)),
        in_specs=[row(D), pl.BlockSpec((N_DEV, tm, FF_BLK), lambda i: (0, i, 0)), row(D), row(D), row(D), full(ada), full(n2w),
                  _whole_vmem(), _whole_vmem(), _whole_vmem()],
        out_specs=(pl.BlockSpec((N_DEV, tm, FF_BLK), lambda i: (0, i, 0)), row(D), row(D), row(D),
                   pl.BlockSpec((8, D), lambda i: (0, 0))),
        compiler_params=_params(dimension_semantics=("arbitrary",)),
    )(x1, gu, dffn, dx2, ycat, ada, n2w, w_out, w_fi, w_fo)


def _bwd_in(x, dproj, dx1, ada, n1w, w_in_st, to_scatter, to_gather):
    t = x.shape[0]
    tm = _row_tile(t, 1024)
    n_t = t // tm
    ns = len(to_scatter)
    ng = len(to_gather)
    nx = ns + ng

    def body(*refs):
        x_ref, dp_ref, dx1_ref, ada_ref, n1_ref, wst_ref = refs[:6]
        x_in = refs[6:6 + nx]
        gx_ref, part_ref = refs[6 + nx:8 + nx]
        x_out = refs[8 + nx:8 + 2 * nx]
        w_ref, sems = refs[8 + 2 * nx:10 + 2 * nx]
        sem_refs = refs[10 + 2 * nx:]
        ex = _Exchange(x_in[:ns], x_out[:ns], *sem_refs[:3], True)
        gx = _Exchange(x_in[ns:], x_out[ns:], *sem_refs[3:], False)

        @pl.when(pl.program_id(0) == 0)
        def _():
            ex.start()
            gx.start()
            part_ref[...] = jnp.zeros_like(part_ref)
            _load_columns(wst_ref, w_ref, sems)

        dh1 = _nt(dp_ref[...], w_ref[...])
        xh, rstd = _rms(x_ref[...])
        xn = xh * n1_ref[...]
        dxn = dh1 * (1.0 + ada_ref[1:2, :])
        gx_ref[...] = dx1_ref[...] + _rms_bwd(dxn * n1_ref[...], xh, rstd)
        part_ref[0:1, :] += _colsum(dh1)
        part_ref[1:2, :] += _colsum(dh1 * xn)
        part_ref[2:3, :] += _colsum(dxn * xh)

        @pl.when(pl.program_id(0) == n_t - 1)
        def _():
            ex.wait()
            gx.wait()

    full = lambda a: pl.BlockSpec(a.shape, lambda i, nd=a.ndim: (0,) * nd)
    row = lambda w: pl.BlockSpec((tm, w), lambda i: (i, 0))
    outs = pl.pallas_call(
        body, name="bwd_in", grid=(n_t,),
        out_shape=(jax.ShapeDtypeStruct((t, D), F32), jax.ShapeDtypeStruct((8, D), F32),
                   *_exchange_out_shapes(to_scatter, True), *_exchange_out_shapes(to_gather, False)),
        in_specs=[row(D), row(D_IN), row(D), full(ada), full(n1w), _any()] + [_any() for _ in range(nx)],
        out_specs=(row(D), pl.BlockSpec((8, D), lambda i: (0, 0)), *[_any() for _ in range(nx)]),
        scratch_shapes=[pltpu.VMEM((D, D_IN), BF), pltpu.SemaphoreType.DMA((N_DEV,))] + _exchange_sems(ns) + _exchange_sems(ng),
        compiler_params=_params(dimension_semantics=("arbitrary",), has_side_effects=True),
    )(x, dproj, dx1, ada, n1w, w_in_st, *to_scatter, *to_gather)
    return outs[0], outs[1], outs[2:2 + ns], outs[2 + ns:]


def _wgrad(a, b, name, a_spec, b_spec, out_shape, out_spec, grid, acc_shape, split=1):
    n_t = grid[1]

    def body(a_ref, b_ref, o_ref, acc):
        @pl.when(pl.program_id(1) == 0)
        def _():
            acc[...] = jnp.zeros_like(acc)

        acc[...] += _tn(a_ref[...], b_ref[...])

        @pl.when(pl.program_id(1) == n_t - 1)
        def _():
            if split == 1:
                o_ref[...] = acc[...].astype(BF)
            else:
                w = acc_shape[1] // split
                for s in range(split):
                    o_ref[s] = acc[:, s * w:(s + 1) * w].astype(BF)

    return pl.pallas_call(
        body, name=name, grid=grid,
        out_shape=jax.ShapeDtypeStruct(out_shape, BF),
        in_specs=[a_spec, b_spec], out_specs=out_spec,
        scratch_shapes=[pltpu.VMEM(acc_shape, F32)],
        compiler_params=_params(dimension_semantics=("arbitrary", "arbitrary")),
    )(a, b)


def _adamw_math(w, g, m, v):
    m = ADAM_B1 * m + (1.0 - ADAM_B1) * g
    v = ADAM_B2 * v + (1.0 - ADAM_B2) * (g * g)
    m_hat = m / (1.0 - ADAM_B1 ** ADAM_STEP)
    v_hat = v / (1.0 - ADAM_B2 ** ADAM_STEP)
    delta = -ADAM_LR * (m_hat / (jnp.sqrt(v_hat) + ADAM_EPS) + ADAM_WD * w)
    return delta, m, v


def _adamw_recv(w, m, v, recv, name, tr):
    r, c = w.shape

    def body(w_ref, m_ref, v_ref, r_ref, g_ref, d_ref, nm_ref, nv_ref):
        g = r_ref[0].astype(F32)
        for k in range(1, N_DEV):
            g = g + r_ref[k].astype(F32)
        g_ref[...] = g
        d_ref[...], nm_ref[...], nv_ref[...] = _adamw_math(w_ref[...], g, m_ref[...], v_ref[...])

    row = pl.BlockSpec((tr, c), lambda i: (i, 0))
    return pl.pallas_call(
        body, name=name, grid=(r // tr,),
        out_shape=tuple(jax.ShapeDtypeStruct((r, c), F32) for _ in range(4)),
        in_specs=[row, row, row, pl.BlockSpec((N_DEV, tr, c), lambda i: (0, i, 0))],
        out_specs=(row, row, row, row),
        compiler_params=_params(dimension_semantics=("arbitrary",)),
    )(w, m, v, recv)


def _adamw_ada(w, m, v, cact, dada_cols):
    r, c = w.shape
    tr = 256

    def body(w_ref, m_ref, v_ref, ca_ref, da_ref, g_ref, d_ref, nm_ref, nv_ref):
        g = _tn(ca_ref[...].astype(BF), da_ref[...].astype(BF))
        g_ref[...] = g
        d_ref[...], nm_ref[...], nv_ref[...] = _adamw_math(w_ref[...], g, m_ref[...], v_ref[...])

    row = pl.BlockSpec((tr, c), lambda i: (i, 0))
    return pl.pallas_call(
        body, name="adamw_ada", grid=(r // tr,),
        out_shape=tuple(jax.ShapeDtypeStruct((r, c), F32) for _ in range(4)),
        in_specs=[row, row, row, pl.BlockSpec((N_DEV, tr), lambda i: (0, i)), pl.BlockSpec(dada_cols.shape, lambda i: (0, 0))],
        out_specs=(row, row, row, row),
        compiler_params=_params(dimension_semantics=("arbitrary",)),
    )(w, m, v, cact, dada_cols)


def _adamw_small(gathered, wmv):
    n_g = len(gathered)
    n_p = len(wmv)
    flat = [a for trip in wmv for a in trip]

    def body(*refs):
        g_refs = refs[:n_g]
        p_refs = refs[n_g:n_g + 3 * n_p]
        o_refs = refs[n_g + 3 * n_p:]

        def total(ref):
            s = ref[0]
            for k in range(1, N_DEV):
                s = s + ref[k]
            return s

        f3, b3, b1, dws, dbs, dlnw, dlnb, dlo, dgn = [total(r) for r in g_refs]
        dada_rows = [b1[0:1], b1[1:2], b3[3:4], b3[0:1], b3[1:2], f3[0:1]]
        for r, g in enumerate(dada_rows):
            w, m, v = p_refs[0][r:r + 1, :], p_refs[1][r:r + 1, :], p_refs[2][r:r + 1, :]
            o_refs[0][r:r + 1, :] = g
            o_refs[1][r:r + 1, :], o_refs[2][r:r + 1, :], o_refs[3][r:r + 1, :] = _adamw_math(w, g, m, v)
        grads = [None, b1[2:3], dws, dbs, dlnw, dlnb, dlo, dgn, b3[2:3], f3[1:2]]
        for i, g in enumerate(grads):
            if g is None:
                continue
            w, m, v = p_refs[3 * i][...], p_refs[3 * i + 1][...], p_refs[3 * i + 2][...]
            o_refs[4 * i][...] = g
            o_refs[4 * i + 1][...], o_refs[4 * i + 2][...], o_refs[4 * i + 3][...] = _adamw_math(w, g, m, v)
        o_refs[4 * n_p][...] = jnp.zeros((8, 128), F32) + f3[2:3, 0:128]

    out_shape = []
    for w, _, _ in wmv:
        out_shape += [jax.ShapeDtypeStruct(w.shape, F32)] * 4
    out_shape.append(jax.ShapeDtypeStruct((8, 128), F32))
    n_in = n_g + 3 * n_p
    return pl.pallas_call(
        body, name="adamw_small",
        out_shape=tuple(out_shape),
        in_specs=[_whole_vmem()] * n_in, out_specs=tuple(_whole_vmem() for _ in out_shape),
        compiler_params=_params(),
    )(*gathered, *flat)


def kernel(x, c, w_ada, b_ada, norm1_w, w_in, w_s, b_s, v_ln_w, v_ln_b, lower_bounds, gn_w, w_out, norm2_w, w_ffn_in, w_ffn_out, final_norm_w, loss_target, m_w_ada, m_b_ada, m_norm1_w, m_w_in, m_w_s, m_b_s, m_v_ln_w, m_v_ln_b, m_lower_bounds, m_gn_w, m_w_out, m_norm2_w, m_w_ffn_in, m_w_ffn_out, m_final_norm_w, v_w_ada, v_b_ada, v_norm1_w, v_w_in, v_w_s, v_b_s, v_v_ln_w, v_v_ln_b, v_lower_bounds, v_gn_w, v_w_out, v_norm2_w, v_w_ffn_in, v_w_ffn_out, v_final_norm_w):
    me = 4 * lax.axis_index("x") + 2 * lax.axis_index("y") + lax.axis_index("c")
    t = x.shape[1]
    x2d = x.reshape(t, D)
    tgt = loss_target.reshape(t, D)
    ada_cols = w_ada.shape[2]

    win_b, wout_b, wfi_b, wfo_b = _cast_bf16([w_in[0], w_out[0], w_ffn_in[0], w_ffn_out[0]], "cast_weights")
    win_st, c_st = _exchange([win_b, c], "gather_w_in", False, False, two_level=True)
    c_all = c_st.reshape(N_DEV, D)

    b_cols = lax.dynamic_slice(b_ada, (0, me * ada_cols), (1, ada_cols))
    ada_blk, cact = _ada_block(c_all, w_ada[0], b_cols)
    (ada_st,) = _exchange([ada_blk], "gather_ada", False, True)
    ada = lax.dynamic_index_in_dim(ada_st, me, axis=1, keepdims=False).reshape(6, D)

    tables = _decay_tables()
    ws3 = w_s[0]
    bs_t = b_s[0].T

    proj, h1 = _fwd_in(x2d, ada, norm1_w, win_st)
    ycat, states, (wout_st, wfi_st, wfo_st) = _fwd_mix(proj, ws3, bs_t, v_ln_w, v_ln_b, lower_bounds, gn_w, tables,
                                                       [wout_b, wfi_b, wfo_b])
    w_out_full = wout_st.reshape(D, D)
    w_fo_full = wfo_st.reshape(D_FF, D)
    x1, h2, act, gu, dffn, dx2, part_f = _fwd_ffn(x2d, ycat, tgt, ada, norm2_w, final_norm_w.reshape(1, D),
                                                  w_out_full, wfi_st, w_fo_full)
    dgu, dx1, dmix, dycat, part_b3 = _bwd_ffn(x1, gu, dffn, dx2, ycat, ada, norm2_w, w_out_full, wfi_st, w_fo_full)
    tk = _row_tile(t, 2048)
    n_t = t // tk
    win_cols = D_IN // N_DEV
    dwout = _wgrad(ycat, dmix, "wgrad_out",
                   pl.BlockSpec((tk, D), lambda j, i: (i, 0)), pl.BlockSpec((tk, D), lambda j, i: (i, 0)),
                   (D, D), pl.BlockSpec((D, D), lambda j, i: (0, 0)), (1, n_t), (D, D))
    dwfi = _wgrad(h2, dgu, "wgrad_ffn_in",
                  pl.BlockSpec((tk, D), lambda j, i: (i, 0)), pl.BlockSpec((None, tk, FF_BLK), lambda j, i: (j, i, 0)),
                  (N_DEV, D, FF_BLK), pl.BlockSpec((None, D, FF_BLK), lambda j, i: (j, 0, 0)), (N_DEV, n_t), (D, FF_BLK))
    dwfo = _wgrad(act, dffn, "wgrad_ffn_out",
                  pl.BlockSpec((None, tk, FF_BLK), lambda j, i: (j, i, 0)), pl.BlockSpec((tk, D), lambda j, i: (i, 0)),
                  (4, FF_BLK, D), pl.BlockSpec((None, FF_BLK, D), lambda j, i: (j, 0, 0)), (4, n_t), (FF_BLK, D))
    dproj, (dws, dbs_t, dlnw, dlnb, dlower, dgnw), (r_out, r_fi, r_fo) = _bwd_mix(
        proj, dycat, states, ws3, bs_t, v_ln_w, v_ln_b, lower_bounds, gn_w, tables,
        [dwout.reshape(N_DEV, D // N_DEV, D), dwfi, dwfo.reshape(N_DEV, D_FF // N_DEV, D)])
    dwin = _wgrad(h1, dproj, "wgrad_in",
                  pl.BlockSpec((tk, D), lambda j, i: (i, 0)), pl.BlockSpec((tk, 4 * win_cols), lambda j, i: (i, j)),
                  (N_DEV, D, win_cols), pl.BlockSpec((4, D, win_cols), lambda j, i: (j, 0, 0)), (N_DEV // 4, n_t),
                  (D, 4 * win_cols), split=4)
    grad_x, part_b1, (r_in,), early = _bwd_in(x2d, dproj, dx1, ada, norm1_w, win_st, [dwin],
                                              [part_f, part_b3, dws, dbs_t, dlnw, dlnb, dlower, dgnw])

    g_w_in, d_w_in, nm_w_in, nv_w_in = _adamw_recv(w_in[0], m_w_in[0], v_w_in[0], r_in, "adamw_w_in", 256)
    g_w_out, d_w_out, nm_w_out, nv_w_out = _adamw_recv(w_out[0], m_w_out[0], v_w_out[0], r_out, "adamw_w_out", 128)
    g_w_fi, d_w_fi, nm_w_fi, nv_w_fi = _adamw_recv(w_ffn_in[0], m_w_ffn_in[0], v_w_ffn_in[0], r_fi, "adamw_w_ffn_in", 256)
    g_w_fo, d_w_fo, nm_w_fo, nv_w_fo = _adamw_recv(w_ffn_out[0], m_w_ffn_out[0], v_w_ffn_out[0], r_fo, "adamw_w_ffn_out", 176)

    (b1_all,) = _exchange([part_b1], "gather_small", False, True)
    gathered = [early[0], early[1], b1_all, *early[2:]]
    f3_all, b3_all = gathered[0], gathered[1]
    dada_all = jnp.stack([b1_all[:, 0], b1_all[:, 1], b3_all[:, 3], b3_all[:, 0], b3_all[:, 1], f3_all[:, 0]], axis=1)
    dada_cols = lax.dynamic_slice(dada_all.reshape(N_DEV, 6 * D), (0, me * ada_cols), (N_DEV, ada_cols))
    g_w_ada, d_w_ada, nm_w_ada, nv_w_ada = _adamw_ada(w_ada[0], m_w_ada[0], v_w_ada[0], cact, dada_cols)

    r6 = lambda a: a.reshape(6, D)
    r1 = lambda a: a.reshape(1, D)
    tr = lambda a: a[0].T
    wmv = [
        (r6(b_ada), r6(m_b_ada), r6(v_b_ada)),
        (norm1_w, m_norm1_w, v_norm1_w),
        (w_s[0], m_w_s[0], v_w_s[0]),
        (tr(b_s), tr(m_b_s), tr(v_b_s)),
        (v_ln_w, m_v_ln_w, v_v_ln_w),
        (v_ln_b, m_v_ln_b, v_v_ln_b),
        (lower_bounds, m_lower_bounds, v_lower_bounds),
        (gn_w, m_gn_w, v_gn_w),
        (norm2_w, m_norm2_w, v_norm2_w),
        (r1(final_norm_w), r1(m_final_norm_w), r1(v_final_norm_w)),
    ]
    small = _adamw_small(gathered, wmv)
    loss = small[-1][0, 0]

    def unshape(i, a):
        if i == 0:
            return a.reshape(1, 6 * D)
        if i == 2:
            return a.reshape(1, NH, BLK, BLK)
        if i == 3:
            return a.T.reshape(1, NH, BLK)
        if i == 9:
            return a.reshape(D)
        return a

    def small_out(kind):
        return [unshape(i, small[4 * i + kind]) for i in range(len(wmv))]

    e3 = lambda a: a[None]
    big = {
        0: (e3(g_w_ada), e3(g_w_in), e3(g_w_out), e3(g_w_fi), e3(g_w_fo)),
        1: (e3(d_w_ada), e3(d_w_in), e3(d_w_out), e3(d_w_fi), e3(d_w_fo)),
        2: (e3(nm_w_ada), e3(nm_w_in), e3(nm_w_out), e3(nm_w_fi), e3(nm_w_fo)),
        3: (e3(nv_w_ada), e3(nv_w_in), e3(nv_w_out), e3(nv_w_fi), e3(nv_w_fo)),
    }

    def ordered(kind):
        s = small_out(kind)
        b_ = big[kind]
        return [b_[0], s[0], s[1], b_[1], s[2], s[3], s[4], s[5], s[6], s[7], b_[2], s[8], b_[3], b_[4], s[9]]

    return (loss, grad_x.reshape(1, t, D), *ordered(0), *ordered(1), *ordered(2), *ordered(3))
```

```python
import functools

import numpy as np
import jax
import jax.numpy as jnp
from jax import lax
from jax.experimental import pallas as pl
from jax.experimental.pallas import tpu as pltpu

F32 = jnp.float32
BF = jnp.bfloat16
MESH = pl.DeviceIdType.MESH

N_DEV = 8
D = 1024
D_IN = 3072
D_FF = 2816
FF_BLK = D_FF // 4
CH = 64
BLK = 128
MIX_TILE = 256
NH = 4
HD = 128
EPS = 1e-6
LEVELS = (32, 16, 8, 4, 2, 1)

ADAM_LR = 0.001
ADAM_B1 = 0.9
ADAM_B2 = 0.999
ADAM_EPS = 1e-08
ADAM_WD = 0.01
ADAM_STEP = 10

VMEM_LIMIT = 56 * 1024 * 1024


def _params(**kw):
    return pltpu.CompilerParams(vmem_limit_bytes=VMEM_LIMIT, **kw)


def _whole_vmem():
    return pl.BlockSpec(memory_space=pltpu.VMEM)


def _any():
    return pl.BlockSpec(memory_space=pl.ANY)


class _Exchange:
    def __init__(self, ins, outs, send_sems, recv_sems, local_sems, scatter):
        self.ins, self.outs, self.scatter = ins, outs, scatter
        self.send_sems, self.recv_sems, self.local_sems = send_sems, recv_sems, local_sems
        x, y, c = lax.axis_index("x"), lax.axis_index("y"), lax.axis_index("c")
        self.me = 4 * x + 2 * y + c
        self.peers = []
        for k in range(1, N_DEV):
            peer = (1 - x if (k >> 2) & 1 else x, 1 - y if (k >> 1) & 1 else y, 1 - c if k & 1 else c)
            self.peers.append((peer, 4 * peer[0] + 2 * peer[1] + peer[2]))

    def _src(self, a, idx):
        return self.ins[a].at[idx] if self.scatter else self.ins[a]

    def _local(self):
        return [pltpu.make_async_copy(self._src(a, self.me), self.outs[a].at[self.me], self.local_sems.at[a])
                for a in range(len(self.ins))]

    def _remote(self, a, k, dst_slot):
        peer, peer_idx = self.peers[k]
        return pltpu.make_async_remote_copy(
            src_ref=self._src(a, peer_idx), dst_ref=self.outs[a].at[dst_slot],
            send_sem=self.send_sems.at[a, k], recv_sem=self.recv_sems.at[a, k],
            device_id=peer, device_id_type=MESH)

    def start(self):
        for cp in self._local():
            cp.start()
        for k in range(N_DEV - 1):
            for a in range(len(self.ins)):
                self._remote(a, k, self.me).start()

    def wait(self):
        for k in range(N_DEV - 1):
            for a in range(len(self.ins)):
                self._remote(a, k, self.peers[k][1]).wait()
        for cp in self._local():
            cp.wait()


class _Gather2:
    def __init__(self, ins, outs, send_sems, recv_sems, local_sems):
        self.ins, self.outs = ins, outs
        self.send_sems, self.recv_sems, self.local_sems = send_sems, recv_sems, local_sems
        x, y, c = lax.axis_index("x"), lax.axis_index("y"), lax.axis_index("c")
        self.c = c
        self.me = 4 * x + 2 * y + c
        self.sibling = (x, y, 1 - c)
        self.chips = [(1 - x, y), (x, 1 - y), (1 - x, 1 - y)]

    @staticmethod
    def _idx(px, py, pc):
        return 4 * px + 2 * py + pc

    def _copy(self, a, k, slot, to, own):
        src = self.ins[a] if own else self.outs[a].at[slot]
        return pltpu.make_async_remote_copy(
            src_ref=src, dst_ref=self.outs[a].at[slot],
            send_sem=self.send_sems.at[a, k], recv_sem=self.recv_sems.at[a, k],
            device_id=to, device_id_type=MESH)

    def _local(self):
        return [pltpu.make_async_copy(self.ins[a], self.outs[a].at[self.me], self.local_sems.at[a])
                for a in range(len(self.ins))]

    def start(self):
        for cp in self._local():
            cp.start()
        for a in range(len(self.ins)):
            self._copy(a, 0, self.me, self.sibling, True).start()
            for j, chip in enumerate(self.chips):
                self._copy(a, 1 + j, self.me, (*chip, self.c), True).start()

    def forward(self):
        for j, chip in enumerate(self.chips):
            for a in range(len(self.ins)):
                slot = self._idx(*chip, self.c)
                self._copy(a, 1 + j, slot, (*chip, self.c), True).wait_recv()
                self._copy(a, 4 + j, slot, self.sibling, False).start()

    def finish(self):
        for a in range(len(self.ins)):
            self._copy(a, 0, self._idx(*self.sibling), self.sibling, True).wait_recv()
            for j, chip in enumerate(self.chips):
                self._copy(a, 4 + j, self._idx(*chip, 1 - self.c), self.sibling, False).wait_recv()
            self._copy(a, 0, self.me, self.sibling, True).wait_send()
            for j, chip in enumerate(self.chips):
                self._copy(a, 1 + j, self.me, (*chip, self.c), True).wait_send()
                self._copy(a, 4 + j, self._idx(*chip, self.c), self.sibling, False).wait_send()
        for cp in self._local():
            cp.wait()


def _exchange_sems(n):
    return [pltpu.SemaphoreType.DMA((n, N_DEV - 1)), pltpu.SemaphoreType.DMA((n, N_DEV - 1)), pltpu.SemaphoreType.DMA((n,))]


def _exchange_out_shapes(arrays, scatter):
    return [jax.ShapeDtypeStruct(a.shape if scatter else (N_DEV,) + a.shape, a.dtype) for a in arrays]


def _exchange(arrays, name, scatter, in_vmem, two_level=False):
    n = len(arrays)

    def body(*refs):
        if two_level:
            ga = _Gather2(refs[:n], refs[n:2 * n], *refs[2 * n:])
            ga.start()
            ga.forward()
            ga.finish()
        else:
            ex = _Exchange(refs[:n], refs[n:2 * n], *refs[2 * n:], scatter)
            ex.start()
            ex.wait()

    spec = _whole_vmem if in_vmem else _any
    return pl.pallas_call(
        body, name=name,
        out_shape=tuple(_exchange_out_shapes(arrays, scatter)),
        in_specs=[spec() for _ in arrays],
        out_specs=tuple(spec() for _ in arrays),
        scratch_shapes=_exchange_sems(n),
        compiler_params=_params(has_side_effects=True),
    )(*arrays)


def _cast_bf16(arrays, name):
    n = len(arrays)

    def body(*refs):
        for i in range(n):
            refs[n + i][...] = refs[i][...].astype(BF)

    return pl.pallas_call(
        body, name=name,
        out_shape=tuple(jax.ShapeDtypeStruct(a.shape, BF) for a in arrays),
        in_specs=[_whole_vmem() for _ in arrays],
        out_specs=tuple(_whole_vmem() for _ in arrays),
        compiler_params=_params(),
    )(*arrays)


def _load_columns(stacked_ref, full_ref, sems):
    c = stacked_ref.shape[2]
    cps = [pltpu.make_async_copy(stacked_ref.at[d], full_ref.at[:, pl.ds(d * c, c)], sems.at[d]) for d in range(N_DEV)]
    for cp in cps:
        cp.start()
    for cp in cps:
        cp.wait()


def _sig(v):
    return 0.5 * jnp.tanh(0.5 * v) + 0.5


@jax.custom_vjp
def _silu(v):
    return v * _sig(v)


def _silu_fwd(v):
    return _silu(v), v


def _silu_bwd(v, g):
    s = _sig(v)
    return (g * (s * (1.0 + v * (1.0 - s))),)


_silu.defvjp(_silu_fwd, _silu_bwd)


@jax.custom_vjp
def _sigmoid_rel(v):
    e = jnp.exp(-jnp.abs(v))
    d = 1.0 + e
    r = pl.reciprocal(d, approx=True)
    r = r * (2.0 - d * r)
    r = r * (2.0 - d * r)
    return jnp.where(v >= 0.0, r, e * r)


def _sigmoid_rel_fwd(v):
    s = _sigmoid_rel(v)
    return s, s


def _sigmoid_rel_bwd(s, g):
    return (g * (s * (1.0 - s)),)


_sigmoid_rel.defvjp(_sigmoid_rel_fwd, _sigmoid_rel_bwd)


def _gelu(v):
    return 0.5 * v * (1.0 + lax.erf(v * 0.7071067811865476))


def _dot(a, b, ca, cb):
    return lax.dot_general(a, b, (((ca,), (cb,)), ((), ())), preferred_element_type=F32)


@jax.custom_vjp
def _mm(a, b):
    return _dot(a, b, 1, 0)


def _mm_fwd(a, b):
    return _dot(a, b, 1, 0), (a, b)


def _mm_bwd(res, g):
    a, b = res
    gb = g.astype(BF)
    return _dot(gb, b, 1, 1).astype(a.dtype), _dot(a, gb, 0, 0).astype(b.dtype)


_mm.defvjp(_mm_fwd, _mm_bwd)


@jax.custom_vjp
def _nt(a, b):
    return _dot(a, b, 1, 1)


def _nt_fwd(a, b):
    return _dot(a, b, 1, 1), (a, b)


def _nt_bwd(res, g):
    a, b = res
    gb = g.astype(BF)
    return _dot(gb, b, 1, 0).astype(a.dtype), _dot(gb, a, 0, 0).astype(b.dtype)


_nt.defvjp(_nt_fwd, _nt_bwd)


@jax.custom_vjp
def _tn(a, b):
    return _dot(a, b, 0, 0)


def _tn_fwd(a, b):
    return _dot(a, b, 0, 0), (a, b)


def _tn_bwd(res, g):
    a, b = res
    gb = g.astype(BF)
    return _dot(b, gb, 1, 1).astype(a.dtype), _dot(a, gb, 1, 0).astype(b.dtype)


_tn.defvjp(_tn_fwd, _tn_bwd)


def _ada_block(c_all, w_ada, b_cols):
    def body(c_ref, w_ref, b_ref, ada_ref, cact_ref):
        ca = _silu(c_ref[...])
        cact_ref[...] = ca
        ada_ref[...] = _mm(ca.astype(BF), w_ref[...].astype(BF)) + b_ref[...]

    return pl.pallas_call(
        body, name="ada_block",
        out_shape=(jax.ShapeDtypeStruct((N_DEV, w_ada.shape[1]), F32), jax.ShapeDtypeStruct(c_all.shape, F32)),
        in_specs=[_whole_vmem()] * 3, out_specs=(_whole_vmem(), _whole_vmem()),
        compiler_params=_params(),
    )(c_all, w_ada, b_cols)


def _decay_tables():
    t = np.arange(CH)
    tri = (t[None, :] <= t[:, None]).astype(np.float32)
    masks = []
    for h in LEVELS:
        m = (t // (2 * h)) * (2 * h) + h
        upper = t >= m
        same = (t[:, None] // (2 * h)) == (t[None, :] // (2 * h))
        masks.append(same & upper[:, None] & (~upper)[None, :])
    masks.append(np.eye(CH, dtype=bool))
    lv = [np.where((t % (2 * h)) >= h, 1.0, -1.0) for h in LEVELS[:4]]
    m4 = t % 4
    lv += [(m4 == 0) * 1.0, (m4 >= 2) * 1.0, (m4 == 3) * 1.0, (t % 2 == 1) * 1.0]
    lvl = np.broadcast_to(np.stack(lv)[:, :, None], (8, CH, D // 2)).astype(np.float32)
    cid = np.arange(BLK) // CH
    gmask = (cid[:, None] >= cid[None, :]).astype(np.float32)
    return (jnp.asarray(tri, BF), jnp.asarray(np.stack(masks).astype(np.float32)), jnp.asarray(gmask), jnp.asarray(lvl))


def _split2(v):
    v1 = v.astype(BF)
    return v1, (v - v1.astype(F32)).astype(BF)


@jax.custom_vjp
def _cumsum_mm(tri, v):
    p1, p2 = _split2(v)
    return _dot(tri, p1, 1, 0) + _dot(tri, p2, 1, 0)


def _cumsum_mm_fwd(tri, v):
    return _cumsum_mm(tri, v), tri


def _cumsum_mm_bwd(tri, g):
    p1, p2 = _split2(g)
    return jnp.zeros_like(tri), _dot(tri, p1, 0, 0) + _dot(tri, p2, 0, 0)


_cumsum_mm.defvjp(_cumsum_mm_fwd, _cumsum_mm_bwd)


def _make_row_roll(shift):
    @jax.custom_vjp
    def roll(x):
        return pltpu.roll(x, shift % CH, 0)

    def fwd(x):
        return roll(x), None

    def bwd(_, g):
        return (pltpu.roll(g, (-shift) % CH, 0),)

    roll.defvjp(fwd, bwd)
    return roll


_prev_row = _make_row_roll(1)
_next_row = _make_row_roll(-1)


def _mix_tile(proj, state, w_s, b_s_t, ln_w, ln_b, lower, gn_w, consts):
    tri, masks, gmask, lvl = consts
    mt = proj.shape[0]
    u = proj[:, 0:512]
    v = proj[:, 512:1024]
    q = proj[:, 1024:1536]
    fl = proj[:, 1536:2048]
    inp = proj[:, 2048:2560]
    g = proj[:, 2560:3072]

    ug = _gelu(u)
    vg = _gelu(v)
    mu = jnp.mean(vg, axis=-1, keepdims=True)
    vc = vg - mu
    var = jnp.mean(vc * vc, axis=-1, keepdims=True)
    vnb = (vc * lax.rsqrt(var + EPS) * ln_w + ln_b).astype(BF)
    wsm = [(w_s[h] * gmask).astype(BF) for h in range(NH)]
    ya = [[None] * NH for _ in range(mt // BLK)]
    for bi in range(mt // BLK):
        rows = slice(bi * BLK, (bi + 1) * BLK)
        for h in range(NH):
            cols = slice(h * HD, (h + 1) * HD)
            ya[bi][h] = ug[rows, cols] * (_mm(wsm[h], vnb[rows, cols]) + b_s_t[:, h:h + 1])
    ya_full = jnp.concatenate([jnp.concatenate(r, axis=1) for r in ya], axis=0)

    l0 = lower[0:1, :]
    l1 = lower[1:2, :]
    mx = jnp.maximum(l0, l1)
    e0 = jnp.exp(l0 - mx)
    e1 = jnp.exp(l1 - mx)
    lb = e0 / (e0 + e1)
    qf = _silu(q)
    f = lb + (1.0 - lb) * _sigmoid_rel(fl)
    logf = jnp.log(f)
    kk = 1.0 - f
    gate = _silu(g)
    nl = len(LEVELS)
    half = D // 2
    keep = [masks[i] > 0.5 for i in range(nl + 1)]
    st = list(state)
    yb = [[None] * NH for _ in range(mt // CH)]
    for ci in range(mt // CH):
        rows = slice(ci * CH, (ci + 1) * CH)
        lc = logf[rows]
        b = _cumsum_mm(tri, lc)
        xb = jnp.exp(b)
        xinv = jnp.exp(b[CH - 1:CH, :] - b)
        xl = []
        for i, hs in enumerate(LEVELS[:3]):
            refs = [jnp.broadcast_to(b[r:r + 1, :], (2 * hs, half)) for r in range(hs - 1, CH, 2 * hs)]
            bref = refs[0] if len(refs) == 1 else jnp.concatenate(refs, axis=0)
            xl.append(jnp.exp(lvl[i] * (b - bref)))
        b3 = b.reshape(CH // 8, 8, half)
        bref = jnp.broadcast_to(b3[:, 3:4, :], (CH // 8, 8, half)).reshape(CH, half)
        xl.append(jnp.exp(lvl[3] * (b - bref)))
        xl.append(jnp.exp(lvl[4] * _next_row(lc) + lvl[5] * lc + lvl[6] * _prev_row(lc)))
        xl.append(jnp.exp(lvl[7] * lc))
        qc = qf[rows]
        kc = kk[rows]
        zsrc = [jnp.concatenate([(qc if (r0 // hs) % 2 == 1 else kc)[r0:r0 + hs] for r0 in range(0, CH, hs)], axis=0)
                for hs in LEVELS[:3]]
        for h in range(NH):
            cols = slice(h * HD, (h + 1) * HD)
            qh = qc[:, cols]
            kh = kc[:, cols]
            vh = inp[rows, cols].astype(BF)
            inter = _nt((qh * xb[:, cols]).astype(BF), st[h].astype(BF))
            attn = jnp.where(keep[nl], _nt(qh.astype(BF), kh.astype(BF)), 0.0)
            for li in range(nl):
                if li < 3:
                    z = (zsrc[li][:, cols] * xl[li][:, cols]).astype(BF)
                    pairs = _nt(z, z)
                else:
                    xx = xl[li][:, cols]
                    pairs = _nt((qh * xx).astype(BF), (kh * xx).astype(BF))
                attn = jnp.where(keep[li], pairs, attn)
            o = inter + _mm(attn.astype(BF), vh)
            st[h] = st[h] * xb[CH - 1:CH, cols] + _tn(vh, (kh * xinv[:, cols]).astype(BF))
            rs = lax.rsqrt(jnp.mean(o * o, axis=-1, keepdims=True) + EPS)
            yb[ci][h] = o * rs * gn_w * gate[rows, cols]
    yb_full = jnp.concatenate([jnp.concatenate(r, axis=1) for r in yb], axis=0)
    return jnp.concatenate([ya_full, yb_full], axis=1), tuple(st)


def _fwd_mix(proj, w_s, b_s_t, ln_w, ln_b, lower, gn_w, tables, to_gather):
    t = proj.shape[0]
    mt = _row_tile(t, MIX_TILE)
    nt_ = t // mt
    nc = len(tables)
    ng = len(to_gather)

    def body(*refs):
        proj_ref, ws_ref, bs_ref, lw_ref, lb_ref, lo_ref, gn_ref = refs[:7]
        c_refs = refs[7:7 + nc]
        refs = refs[7 + nc:]
        g_in = refs[:ng]
        y_ref, st_ref = refs[ng:2 + ng]
        g_out = refs[2 + ng:2 + 2 * ng]
        state = refs[2 + 2 * ng]
        ga = _Gather2(g_in, g_out, *refs[3 + 2 * ng:])

        @pl.when(pl.program_id(0) == 0)
        def _():
            state[...] = jnp.zeros_like(state)
            ga.start()

        @pl.when(pl.program_id(0) == (3 * nt_) // 4)
        def _():
            ga.forward()

        st = tuple(state[h] for h in range(NH))
        for h in range(NH):
            st_ref[h] = st[h]
        y, new = _mix_tile(proj_ref[...], st, ws_ref[...], bs_ref[...], lw_ref[...], lb_ref[...], lo_ref[...], gn_ref[...],
                           tuple(r[...] for r in c_refs))
        y_ref[...] = y.astype(BF)
        for h in range(NH):
            state[h] = new[h]

        @pl.when(pl.program_id(0) == nt_ - 1)
        def _():
            ga.finish()

    full = lambda a: pl.BlockSpec(a.shape, lambda i, nd=a.ndim: (0,) * nd)
    outs = pl.pallas_call(
        body, name="fwd_mix", grid=(nt_,),
        out_shape=(jax.ShapeDtypeStruct((t, D), BF), jax.ShapeDtypeStruct((nt_, NH, HD, HD), F32),
                   *_exchange_out_shapes(to_gather, False)),
        in_specs=[pl.BlockSpec((mt, D_IN), lambda i: (i, 0)), full(w_s), full(b_s_t), full(ln_w), full(ln_b), full(lower),
                  full(gn_w)] + [full(a) for a in tables] + [_any() for _ in to_gather],
        out_specs=(pl.BlockSpec((mt, D), lambda i: (i, 0)), pl.BlockSpec((None, NH, HD, HD), lambda i: (i, 0, 0, 0)),
                   *[_any() for _ in to_gather]),
        scratch_shapes=[pltpu.VMEM((NH, HD, HD), F32)] + _exchange_sems(ng),
        compiler_params=_params(dimension_semantics=("arbitrary",), has_side_effects=True),
    )(proj, w_s, b_s_t, ln_w, ln_b, lower, gn_w, *tables, *to_gather)
    return outs[0], outs[1], outs[2:]


def _bwd_mix(proj, dycat, states, w_s, b_s_t, ln_w, ln_b, lower, gn_w, tables, to_scatter):
    t = proj.shape[0]
    mt = _row_tile(t, MIX_TILE)
    nt_ = t // mt
    nc = len(tables)
    ns = len(to_scatter)

    def body(*refs):
        proj_ref, dy_ref, st_ref, ws_ref, bs_ref, lw_ref, lb_ref, lo_ref, gn_ref = refs[:9]
        c_refs = refs[9:9 + nc]
        refs = refs[9 + nc:]
        s_in = refs[:ns]
        dproj_ref = refs[ns]
        acc = refs[1 + ns:7 + ns]
        s_out = refs[7 + ns:7 + 2 * ns]
        dstate = refs[7 + 2 * ns]
        ex = _Exchange(s_in, s_out, *refs[8 + 2 * ns:], True)

        @pl.when(pl.program_id(0) == 0)
        def _():
            dstate[...] = jnp.zeros_like(dstate)
            for r in acc:
                r[...] = jnp.zeros_like(r)
            ex.start()

        consts = tuple(r[...] for r in c_refs)

        def f(p, s, ws, bs, lw, lb_, lo, gn):
            return _mix_tile(p, s, ws, bs, lw, lb_, lo, gn, consts)

        st = tuple(st_ref[h] for h in range(NH))
        _, vjp = jax.vjp(f, proj_ref[...], st, ws_ref[...], bs_ref[...], lw_ref[...], lb_ref[...], lo_ref[...], gn_ref[...])
        grads = vjp((dy_ref[...], tuple(dstate[h] for h in range(NH))))
        dproj_ref[...] = grads[0].astype(BF)
        for h in range(NH):
            dstate[h] = grads[1][h]
        for r, gval in zip(acc, grads[2:]):
            r[...] += gval

        @pl.when(pl.program_id(0) == nt_ - 1)
        def _():
            ex.wait()

    full = lambda a: pl.BlockSpec(a.shape, lambda i, nd=a.ndim: (0,) * nd)
    rev = lambda i: (nt_ - 1 - i, 0)
    smalls = (w_s, b_s_t, ln_w, ln_b, lower, gn_w)
    outs = pl.pallas_call(
        body, name="bwd_mix", grid=(nt_,),
        out_shape=(jax.ShapeDtypeStruct((t, D_IN), BF), *[jax.ShapeDtypeStruct(a.shape, F32) for a in smalls],
                   *_exchange_out_shapes(to_scatter, True)),
        in_specs=[pl.BlockSpec((mt, D_IN), rev), pl.BlockSpec((mt, D), rev),
                  pl.BlockSpec((None, NH, HD, HD), lambda i: (nt_ - 1 - i, 0, 0, 0))]
        + [full(a) for a in smalls] + [full(a) for a in tables] + [_any() for _ in to_scatter],
        out_specs=(pl.BlockSpec((mt, D_IN), rev), *[full(a) for a in smalls], *[_any() for _ in to_scatter]),
        scratch_shapes=[pltpu.VMEM((NH, HD, HD), F32)] + _exchange_sems(ns),
        compiler_params=_params(dimension_semantics=("arbitrary",), has_side_effects=True),
    )(proj, dycat, states, w_s, b_s_t, ln_w, ln_b, lower, gn_w, *tables, *to_scatter)
    return outs[0], outs[1:7], outs[7:]


def _row_tile(t, want):
    return want if t % want == 0 else t


def _rms(v):
    rstd = lax.rsqrt(jnp.mean(v * v, axis=-1, keepdims=True) + EPS)
    return v * rstd, rstd


def _rms_bwd(dxhat, xhat, rstd):
    return rstd * (dxhat - xhat * jnp.mean(dxhat * xhat, axis=-1, keepdims=True))


def _colsum(v):
    return jnp.sum(v, axis=0, keepdims=True)


def _fwd_in(x, ada, n1w, w_in_st):
    t = x.shape[0]
    tm = _row_tile(t, 512)

    def body(x_ref, ada_ref, n1_ref, wst_ref, proj_ref, h1_ref, w_ref, sems):
        @pl.when(pl.program_id(0) == 0)
        def _():
            _load_columns(wst_ref, w_ref, sems)

        xh, _ = _rms(x_ref[...])
        h1 = (xh * n1_ref[...]) * (1.0 + ada_ref[1:2, :]) + ada_ref[0:1, :]
        h1b = h1.astype(BF)
        h1_ref[...] = h1b
        proj_ref[...] = _mm(h1b, w_ref[...])

    full = lambda a: pl.BlockSpec(a.shape, lambda i, nd=a.ndim: (0,) * nd)
    return pl.pallas_call(
        body, name="fwd_in", grid=(t // tm,),
        out_shape=(jax.ShapeDtypeStruct((t, D_IN), F32), jax.ShapeDtypeStruct((t, D), BF)),
        in_specs=[pl.BlockSpec((tm, D), lambda i: (i, 0)), full(ada), full(n1w), _any()],
        out_specs=(pl.BlockSpec((tm, D_IN), lambda i: (i, 0)), pl.BlockSpec((tm, D), lambda i: (i, 0))),
        scratch_shapes=[pltpu.VMEM((D, D_IN), BF), pltpu.SemaphoreType.DMA((N_DEV,))],
        compiler_params=_params(dimension_semantics=("arbitrary",)),
    )(x, ada, n1w, w_in_st)


def _fwd_ffn(x, ycat, tgt, ada, n2w, fw, w_out, w_fi, w_fo):
    t = x.shape[0]
    tm = _row_tile(t, 256)

    def body(x_ref, y_ref, t_ref, ada_ref, n2_ref, fw_ref, wo_ref, wi_ref, wf_ref,
             x1_ref, h2_ref, act_ref, gu_ref, dffn_ref, dx2_ref, part_ref):
        @pl.when(pl.program_id(0) == 0)
        def _():
            part_ref[...] = jnp.zeros_like(part_ref)

        g1, sh2, sc2, g2 = ada_ref[2:3, :], ada_ref[3:4, :], ada_ref[4:5, :], ada_ref[5:6, :]
        x1 = x_ref[...] + g1 * _mm(y_ref[...], wo_ref[...])
        x1_ref[...] = x1
        xh2, _ = _rms(x1)
        h2b = ((xh2 * n2_ref[...]) * (1.0 + sc2) + sh2).astype(BF)
        h2_ref[...] = h2b
        ffn = jnp.zeros((tm, D), F32)
        for j in range(4):
            gate = _mm(h2b, wi_ref[j])
            up = _mm(h2b, wi_ref[j + 4])
            gu_ref[j] = gate.astype(BF)
            gu_ref[j + 4] = up.astype(BF)
            a = (_silu(gate) * up).astype(BF)
            act_ref[j] = a
            ffn = ffn + _mm(a, wf_ref[j * FF_BLK:(j + 1) * FF_BLK, :])
        x2 = x1 + g2 * ffn
        xh3, rstd3 = _rms(x2)
        err = xh3 * fw_ref[...] - t_ref[...]
        dy = err * (1.0 / D)
        dx2 = _rms_bwd(dy * fw_ref[...], xh3, rstd3)
        dx2_ref[...] = dx2
        dffn_ref[...] = (g2 * dx2).astype(BF)
        part_ref[0:1, :] += _colsum(dx2 * ffn)
        part_ref[1:2, :] += _colsum(dy * xh3)
        part_ref[2:3, :] += jnp.zeros((1, D), F32) + (0.5 / D) * jnp.sum(err * err)

    full = lambda a: pl.BlockSpec(a.shape, lambda i, nd=a.ndim: (0,) * nd)
    row = lambda w: pl.BlockSpec((tm, w), lambda i: (i, 0))
    return pl.pallas_call(
        body, name="fwd_ffn", grid=(t // tm,),
        out_shape=(jax.ShapeDtypeStruct((t, D), F32), jax.ShapeDtypeStruct((t, D), BF),
                   jax.ShapeDtypeStruct((4, t, FF_BLK), BF), jax.ShapeDtypeStruct((N_DEV, t, FF_BLK), BF),
                   jax.ShapeDtypeStruct((t, D), BF), jax.ShapeDtypeStruct((t, D), F32), jax.ShapeDtypeStruct((8, D), F32)),
        in_specs=[row(D), row(D), row(D), full(ada), full(n2w), full(fw), _whole_vmem(), _whole_vmem(), _whole_vmem()],
        out_specs=(row(D), row(D), pl.BlockSpec((4, tm, FF_BLK), lambda i: (0, i, 0)),
                   pl.BlockSpec((N_DEV, tm, FF_BLK), lambda i: (0, i, 0)), row(D), row(D),
                   pl.BlockSpec((8, D), lambda i: (0, 0))),
        compiler_params=_params(dimension_semantics=("arbitrary",)),
    )(x, ycat, tgt, ada, n2w, fw, w_out, w_fi, w_fo)


def _bwd_ffn(x1, gu, dffn, dx2, ycat, ada, n2w, w_out, w_fi, w_fo):
    t = x1.shape[0]
    tm = _row_tile(t, 256)

    def body(x1_ref, gu_ref, dffn_ref, dx2_ref, y_ref, ada_ref, n2_ref, wo_ref, wi_ref, wf_ref,
             dgu_ref, dx1_ref, dmix_ref, dycat_ref, part_ref):
        @pl.when(pl.program_id(0) == 0)
        def _():
            part_ref[...] = jnp.zeros_like(part_ref)

        g1, sc2 = ada_ref[2:3, :], ada_ref[4:5, :]
        dffn = dffn_ref[...]
        dh2 = jnp.zeros((tm, D), F32)
        for j in range(4):
            gate = gu_ref[j].astype(F32)
            up = gu_ref[j + 4].astype(F32)
            dact = _nt(dffn, wf_ref[j * FF_BLK:(j + 1) * FF_BLK, :])
            sg = _sig(gate)
            dgate = (dact * up * (sg * (1.0 + gate * (1.0 - sg)))).astype(BF)
            dup = (dact * (gate * sg)).astype(BF)
            dgu_ref[j] = dgate
            dgu_ref[j + 4] = dup
            dh2 = dh2 + _nt(dgate, wi_ref[j]) + _nt(dup, wi_ref[j + 4])
        x1 = x1_ref[...]
        xh2, rstd2 = _rms(x1)
        xn2 = xh2 * n2_ref[...]
        dxn2 = dh2 * (1.0 + sc2)
        dx1 = dx2_ref[...] + _rms_bwd(dxn2 * n2_ref[...], xh2, rstd2)
        dx1_ref[...] = dx1
        mix = _mm(y_ref[...], wo_ref[...])
        dmix = (g1 * dx1).astype(BF)
        dmix_ref[...] = dmix
        dycat_ref[...] = _nt(dmix, wo_ref[...])
        part_ref[0:1, :] += _colsum(dh2)
        part_ref[1:2, :] += _colsum(dh2 * xn2)
        part_ref[2:3, :] += _colsum(dxn2 * xh2)
        part_ref[3:4, :] += _colsum(dx1 * mix)

    full = lambda a: pl.BlockSpec(a.shape, lambda i, nd=a.ndim: (0,) * nd)
    row = lambda w: pl.BlockSpec((tm, w), lambda i: (i, 0))
    return pl.pallas_call(
        body, name="bwd_ffn", grid=(t // tm,),
        out_shape=(jax.ShapeDtypeStruct((N_DEV, t, FF_BLK), BF), jax.ShapeDtypeStruct((t, D), F32),
                   jax.ShapeDtypeStruct((t, D), BF), jax.ShapeDtypeStruct((t, D), F32), jax.ShapeDtypeStruct((8, D), F32)),
        in_specs=[row(D), pl.BlockSpec((N_DEV, tm, FF_BLK), lambda i: (0, i, 0)), row(D), row(D), row(D), full(ada), full(n2w),
                  _whole_vmem(), _whole_vmem(), _whole_vmem()],
        out_specs=(pl.BlockSpec((N_DEV, tm, FF_BLK), lambda i: (0, i, 0)), row(D), row(D), row(D),
                   pl.BlockSpec((8, D), lambda i: (0, 0))),
        compiler_params=_params(dimension_semantics=("arbitrary",)),
    )(x1, gu, dffn, dx2, ycat, ada, n2w, w_out, w_fi, w_fo)


def _bwd_in(x, dproj, dx1, ada, n1w, w_in_st, to_scatter, to_gather):
    t = x.shape[0]
    tm = _row_tile(t, 1024)
    n_t = t // tm
    ns = len(to_scatter)
    ng = len(to_gather)
    nx = ns + ng

    def body(*refs):
        x_ref, dp_ref, dx1_ref, ada_ref, n1_ref, wst_ref = refs[:6]
        x_in = refs[6:6 + nx]
        gx_ref, part_ref = refs[6 + nx:8 + nx]
        x_out = refs[8 + nx:8 + 2 * nx]
        w_ref, sems = refs[8 + 2 * nx:10 + 2 * nx]
        sem_refs = refs[10 + 2 * nx:]
        ex = _Exchange(x_in[:ns], x_out[:ns], *sem_refs[:3], True)
        gx = _Exchange(x_in[ns:], x_out[ns:], *sem_refs[3:], False) if ng else None

        @pl.when(pl.program_id(0) == 0)
        def _():
            ex.start()
            if ng:
                gx.start()
            part_ref[...] = jnp.zeros_like(part_ref)
            _load_columns(wst_ref, w_ref, sems)

        dh1 = _nt(dp_ref[...], w_ref[...])
        xh, rstd = _rms(x_ref[...])
        xn = xh * n1_ref[...]
        dxn = dh1 * (1.0 + ada_ref[1:2, :])
        gx_ref[...] = dx1_ref[...] + _rms_bwd(dxn * n1_ref[...], xh, rstd)
        part_ref[0:1, :] += _colsum(dh1)
        part_ref[1:2, :] += _colsum(dh1 * xn)
        part_ref[2:3, :] += _colsum(dxn * xh)

        @pl.when(pl.program_id(0) == n_t - 1)
        def _():
            ex.wait()
            if ng:
                gx.wait()

    full = lambda a: pl.BlockSpec(a.shape, lambda i, nd=a.ndim: (0,) * nd)
    row = lambda w: pl.BlockSpec((tm, w), lambda i: (i, 0))
    outs = pl.pallas_call(
        body, name="bwd_in", grid=(n_t,),
        out_shape=(jax.ShapeDtypeStruct((t, D), F32), jax.ShapeDtypeStruct((8, D), F32),
                   *_exchange_out_shapes(to_scatter, True), *_exchange_out_shapes(to_gather, False)),
        in_specs=[row(D), row(D_IN), row(D), full(ada), full(n1w), _any()] + [_any() for _ in range(nx)],
        out_specs=(row(D), pl.BlockSpec((8, D), lambda i: (0, 0)), *[_any() for _ in range(nx)]),
        scratch_shapes=[pltpu.VMEM((D, D_IN), BF), pltpu.SemaphoreType.DMA((N_DEV,))] + _exchange_sems(ns)
        + (_exchange_sems(ng) if ng else []),
        compiler_params=_params(dimension_semantics=("arbitrary",), has_side_effects=True),
    )(x, dproj, dx1, ada, n1w, w_in_st, *to_scatter, *to_gather)
    return outs[0], outs[1], outs[2:2 + ns], outs[2 + ns:]


def _wgrad(a, b, name, a_spec, b_spec, out_shape, out_spec, grid, acc_shape, split=1, to_gather=()):
    n_j, n_t = grid
    ng = len(to_gather)

    def body(*refs):
        a_ref, b_ref = refs[:2]
        g_in = refs[2:2 + ng]
        o_ref = refs[2 + ng]
        g_out = refs[3 + ng:3 + 2 * ng]
        acc = refs[3 + 2 * ng]
        first = (pl.program_id(0) == 0) & (pl.program_id(1) == 0)
        last = (pl.program_id(0) == n_j - 1) & (pl.program_id(1) == n_t - 1)
        if ng:
            gx = _Exchange(g_in, g_out, *refs[4 + 2 * ng:], False)

            @pl.when(first)
            def _():
                gx.start()

        @pl.when(pl.program_id(1) == 0)
        def _():
            acc[...] = jnp.zeros_like(acc)

        acc[...] += _tn(a_ref[...], b_ref[...])

        @pl.when(pl.program_id(1) == n_t - 1)
        def _():
            if split == 1:
                o_ref[...] = acc[...].astype(BF)
            else:
                w = acc_shape[1] // split
                for s in range(split):
                    o_ref[s] = acc[:, s * w:(s + 1) * w].astype(BF)

        if ng:
            @pl.when(last)
            def _():
                gx.wait()

    outs = pl.pallas_call(
        body, name=name, grid=grid,
        out_shape=(jax.ShapeDtypeStruct(out_shape, BF), *_exchange_out_shapes(to_gather, False)),
        in_specs=[a_spec, b_spec] + [_any() for _ in to_gather], out_specs=(out_spec, *[_any() for _ in to_gather]),
        scratch_shapes=[pltpu.VMEM(acc_shape, F32)] + (_exchange_sems(ng) if ng else []),
        compiler_params=_params(dimension_semantics=("arbitrary", "arbitrary"), has_side_effects=bool(ng)),
    )(a, b, *to_gather)
    return (outs[0], outs[1:]) if ng else outs[0]


def _adamw_math(w, g, m, v):
    m = ADAM_B1 * m + (1.0 - ADAM_B1) * g
    v = ADAM_B2 * v + (1.0 - ADAM_B2) * (g * g)
    m_hat = m / (1.0 - ADAM_B1 ** ADAM_STEP)
    v_hat = v / (1.0 - ADAM_B2 ** ADAM_STEP)
    delta = -ADAM_LR * (m_hat / (jnp.sqrt(v_hat) + ADAM_EPS) + ADAM_WD * w)
    return delta, m, v


def _adamw_recv(w, m, v, recv, name, tr):
    r, c = w.shape

    def body(w_ref, m_ref, v_ref, r_ref, g_ref, d_ref, nm_ref, nv_ref):
        g = r_ref[0].astype(F32)
        for k in range(1, N_DEV):
            g = g + r_ref[k].astype(F32)
        g_ref[...] = g
        d_ref[...], nm_ref[...], nv_ref[...] = _adamw_math(w_ref[...], g, m_ref[...], v_ref[...])

    row = pl.BlockSpec((tr, c), lambda i: (i, 0))
    return pl.pallas_call(
        body, name=name, grid=(r // tr,),
        out_shape=tuple(jax.ShapeDtypeStruct((r, c), F32) for _ in range(4)),
        in_specs=[row, row, row, pl.BlockSpec((N_DEV, tr, c), lambda i: (0, i, 0))],
        out_specs=(row, row, row, row),
        compiler_params=_params(dimension_semantics=("arbitrary",)),
    )(w, m, v, recv)


def _adamw_ada(w, m, v, cact, dada_cols):
    r, c = w.shape
    tr = 256

    def body(w_ref, m_ref, v_ref, ca_ref, da_ref, g_ref, d_ref, nm_ref, nv_ref):
        g = _tn(ca_ref[...].astype(BF), da_ref[...].astype(BF))
        g_ref[...] = g
        d_ref[...], nm_ref[...], nv_ref[...] = _adamw_math(w_ref[...], g, m_ref[...], v_ref[...])

    row = pl.BlockSpec((tr, c), lambda i: (i, 0))
    return pl.pallas_call(
        body, name="adamw_ada", grid=(r // tr,),
        out_shape=tuple(jax.ShapeDtypeStruct((r, c), F32) for _ in range(4)),
        in_specs=[row, row, row, pl.BlockSpec((N_DEV, tr), lambda i: (0, i)), pl.BlockSpec(dada_cols.shape, lambda i: (0, 0))],
        out_specs=(row, row, row, row),
        compiler_params=_params(dimension_semantics=("arbitrary",)),
    )(w, m, v, cact, dada_cols)


def _adamw_small(gathered, wmv):
    n_g = len(gathered)
    n_p = len(wmv)
    flat = [a for trip in wmv for a in trip]

    def body(*refs):
        g_refs = refs[:n_g]
        p_refs = refs[n_g:n_g + 3 * n_p]
        o_refs = refs[n_g + 3 * n_p:]

        def total(ref):
            s = ref[0]
            for k in range(1, N_DEV):
                s = s + ref[k]
            return s

        f3, b3, b1, dws, dbs, dlnw, dlnb, dlo, dgn = [total(r) for r in g_refs]
        dada_rows = [b1[0:1], b1[1:2], b3[3:4], b3[0:1], b3[1:2], f3[0:1]]
        for r, g in enumerate(dada_rows):
            w, m, v = p_refs[0][r:r + 1, :], p_refs[1][r:r + 1, :], p_refs[2][r:r + 1, :]
            o_refs[0][r:r + 1, :] = g
            o_refs[1][r:r + 1, :], o_refs[2][r:r + 1, :], o_refs[3][r:r + 1, :] = _adamw_math(w, g, m, v)
        grads = [None, b1[2:3], dws, dbs, dlnw, dlnb, dlo, dgn, b3[2:3], f3[1:2]]
        for i, g in enumerate(grads):
            if g is None:
                continue
            w, m, v = p_refs[3 * i][...], p_refs[3 * i + 1][...], p_refs[3 * i + 2][...]
            o_refs[4 * i][...] = g
            o_refs[4 * i + 1][...], o_refs[4 * i + 2][...], o_refs[4 * i + 3][...] = _adamw_math(w, g, m, v)
        o_refs[4 * n_p][...] = jnp.zeros((8, 128), F32) + f3[2:3, 0:128]

    out_shape = []
    for w, _, _ in wmv:
        out_shape += [jax.ShapeDtypeStruct(w.shape, F32)] * 4
    out_shape.append(jax.ShapeDtypeStruct((8, 128), F32))
    n_in = n_g + 3 * n_p
    return pl.pallas_call(
        body, name="adamw_small",
        out_shape=tuple(out_shape),
        in_specs=[_whole_vmem()] * n_in, out_specs=tuple(_whole_vmem() for _ in out_shape),
        compiler_params=_params(),
    )(*gathered, *flat)


def kernel(x, c, w_ada, b_ada, norm1_w, w_in, w_s, b_s, v_ln_w, v_ln_b, lower_bounds, gn_w, w_out, norm2_w, w_ffn_in, w_ffn_out, final_norm_w, loss_target, m_w_ada, m_b_ada, m_norm1_w, m_w_in, m_w_s, m_b_s, m_v_ln_w, m_v_ln_b, m_lower_bounds, m_gn_w, m_w_out, m_norm2_w, m_w_ffn_in, m_w_ffn_out, m_final_norm_w, v_w_ada, v_b_ada, v_norm1_w, v_w_in, v_w_s, v_b_s, v_v_ln_w, v_v_ln_b, v_lower_bounds, v_gn_w, v_w_out, v_norm2_w, v_w_ffn_in, v_w_ffn_out, v_final_norm_w):
    me = 4 * lax.axis_index("x") + 2 * lax.axis_index("y") + lax.axis_index("c")
    t = x.shape[1]
    x2d = x.reshape(t, D)
    tgt = loss_target.reshape(t, D)
    ada_cols = w_ada.shape[2]

    win_b, wout_b, wfi_b, wfo_b = _cast_bf16([w_in[0], w_out[0], w_ffn_in[0], w_ffn_out[0]], "cast_weights")
    win_st, c_st = _exchange([win_b, c], "gather_w_in", False, False, two_level=True)
    c_all = c_st.reshape(N_DEV, D)

    b_cols = lax.dynamic_slice(b_ada, (0, me * ada_cols), (1, ada_cols))
    ada_blk, cact = _ada_block(c_all, w_ada[0], b_cols)
    (ada_st,) = _exchange([ada_blk], "gather_ada", False, True)
    ada = lax.dynamic_index_in_dim(ada_st, me, axis=1, keepdims=False).reshape(6, D)

    tables = _decay_tables()
    ws3 = w_s[0]
    bs_t = b_s[0].T

    proj, h1 = _fwd_in(x2d, ada, norm1_w, win_st)
    ycat, states, (wout_st, wfi_st, wfo_st) = _fwd_mix(proj, ws3, bs_t, v_ln_w, v_ln_b, lower_bounds, gn_w, tables,
                                                       [wout_b, wfi_b, wfo_b])
    w_out_full = wout_st.reshape(D, D)
    w_fo_full = wfo_st.reshape(D_FF, D)
    x1, h2, act, gu, dffn, dx2, part_f = _fwd_ffn(x2d, ycat, tgt, ada, norm2_w, final_norm_w.reshape(1, D),
                                                  w_out_full, wfi_st, w_fo_full)
    dgu, dx1, dmix, dycat, part_b3 = _bwd_ffn(x1, gu, dffn, dx2, ycat, ada, norm2_w, w_out_full, wfi_st, w_fo_full)
    tk = _row_tile(t, 2048)
    n_t = t // tk
    win_cols = D_IN // N_DEV
    dwout = _wgrad(ycat, dmix, "wgrad_out",
                   pl.BlockSpec((tk, D), lambda j, i: (i, 0)), pl.BlockSpec((tk, D), lambda j, i: (i, 0)),
                   (D, D), pl.BlockSpec((D, D), lambda j, i: (0, 0)), (1, n_t), (D, D))
    dwfi = _wgrad(h2, dgu, "wgrad_ffn_in",
                  pl.BlockSpec((tk, D), lambda j, i: (i, 0)), pl.BlockSpec((None, tk, FF_BLK), lambda j, i: (j, i, 0)),
                  (N_DEV, D, FF_BLK), pl.BlockSpec((None, D, FF_BLK), lambda j, i: (j, 0, 0)), (N_DEV, n_t), (D, FF_BLK))
    dwfo = _wgrad(act, dffn, "wgrad_ffn_out",
                  pl.BlockSpec((None, tk, FF_BLK), lambda j, i: (j, i, 0)), pl.BlockSpec((tk, D), lambda j, i: (i, 0)),
                  (4, FF_BLK, D), pl.BlockSpec((None, FF_BLK, D), lambda j, i: (j, 0, 0)), (4, n_t), (FF_BLK, D))
    dproj, (dws, dbs_t, dlnw, dlnb, dlower, dgnw), (r_out, r_fi, r_fo) = _bwd_mix(
        proj, dycat, states, ws3, bs_t, v_ln_w, v_ln_b, lower_bounds, gn_w, tables,
        [dwout.reshape(N_DEV, D // N_DEV, D), dwfi, dwfo.reshape(N_DEV, D_FF // N_DEV, D)])
    dwin, early = _wgrad(h1, dproj, "wgrad_in",
                         pl.BlockSpec((tk, D), lambda j, i: (i, 0)), pl.BlockSpec((tk, 4 * win_cols), lambda j, i: (i, j)),
                         (N_DEV, D, win_cols), pl.BlockSpec((4, D, win_cols), lambda j, i: (j, 0, 0)), (N_DEV // 4, n_t),
                         (D, 4 * win_cols), split=4, to_gather=[part_f, part_b3, dws, dbs_t, dlnw, dlnb, dlower, dgnw])
    grad_x, part_b1, (r_in,), _ = _bwd_in(x2d, dproj, dx1, ada, norm1_w, win_st, [dwin], [])

    g_w_in, d_w_in, nm_w_in, nv_w_in = _adamw_recv(w_in[0], m_w_in[0], v_w_in[0], r_in, "adamw_w_in", 256)
    g_w_out, d_w_out, nm_w_out, nv_w_out = _adamw_recv(w_out[0], m_w_out[0], v_w_out[0], r_out, "adamw_w_out", 128)
    g_w_fi, d_w_fi, nm_w_fi, nv_w_fi = _adamw_recv(w_ffn_in[0], m_w_ffn_in[0], v_w_ffn_in[0], r_fi, "adamw_w_ffn_in", 256)
    g_w_fo, d_w_fo, nm_w_fo, nv_w_fo = _adamw_recv(w_ffn_out[0], m_w_ffn_out[0], v_w_ffn_out[0], r_fo, "adamw_w_ffn_out", 176)

    (b1_all,) = _exchange([part_b1], "gather_small", False, True)
    gathered = [early[0], early[1], b1_all, *early[2:]]
    f3_all, b3_all = gathered[0], gathered[1]
    dada_all = jnp.stack([b1_all[:, 0], b1_all[:, 1], b3_all[:, 3], b3_all[:, 0], b3_all[:, 1], f3_all[:, 0]], axis=1)
    dada_cols = lax.dynamic_slice(dada_all.reshape(N_DEV, 6 * D), (0, me * ada_cols), (N_DEV, ada_cols))
    g_w_ada, d_w_ada, nm_w_ada, nv_w_ada = _adamw_ada(w_ada[0], m_w_ada[0], v_w_ada[0], cact, dada_cols)

    r6 = lambda a: a.reshape(6, D)
    r1 = lambda a: a.reshape(1, D)
    tr = lambda a: a[0].T
    wmv = [
        (r6(b_ada), r6(m_b_ada), r6(v_b_ada)),
        (norm1_w, m_norm1_w, v_norm1_w),
        (w_s[0], m_w_s[0], v_w_s[0]),
        (tr(b_s), tr(m_b_s), tr(v_b_s)),
        (v_ln_w, m_v_ln_w, v_v_ln_w),
        (v_ln_b, m_v_ln_b, v_v_ln_b),
        (lower_bounds, m_lower_bounds, v_lower_bounds),
        (gn_w, m_gn_w, v_gn_w),
        (norm2_w, m_norm2_w, v_norm2_w),
        (r1(final_norm_w), r1(m_final_norm_w), r1(v_final_norm_w)),
    ]
    small = _adamw_small(gathered, wmv)
    loss = small[-1][0, 0]

    def unshape(i, a):
        if i == 0:
            return a.reshape(1, 6 * D)
        if i == 2:
            return a.reshape(1, NH, BLK, BLK)
        if i == 3:
            return a.T.reshape(1, NH, BLK)
        if i == 9:
            return a.reshape(D)
        return a

    def small_out(kind):
        return [unshape(i, small[4 * i + kind]) for i in range(len(wmv))]

    e3 = lambda a: a[None]
    big = {
        0: (e3(g_w_ada), e3(g_w_in), e3(g_w_out), e3(g_w_fi), e3(g_w_fo)),
        1: (e3(d_w_ada), e3(d_w_in), e3(d_w_out), e3(d_w_fi), e3(d_w_fo)),
        2: (e3(nm_w_ada), e3(nm_w_in), e3(nm_w_out), e3(nm_w_fi), e3(nm_w_fo)),
        3: (e3(nv_w_ada), e3(nv_w_in), e3(nv_w_out), e3(nv_w_fi), e3(nv_w_fo)),
    }

    def ordered(kind):
        s = small_out(kind)
        b_ = big[kind]
        return [b_[0], s[0], s[1], b_[1], s[2], s[3], s[4], s[5], s[6], s[7], b_[2], s[8], b_[3], b_[4], s[9]]

    return (loss, grad_x.reshape(1, t, D), *ordered(0), *ordered(1), *ordered(2), *ordered(3))
```

```python
import functools

import numpy as np
import jax
import jax.numpy as jnp
from jax import lax
from jax.experimental import pallas as pl
from jax.experimental.pallas import tpu as pltpu

F32 = jnp.float32
BF = jnp.bfloat16
MESH = pl.DeviceIdType.MESH

N_DEV = 8
D = 1024
D_IN = 3072
D_FF = 2816
FF_BLK = D_FF // 4
CH = 64
BLK = 128
MIX_TILE = 512
NH = 4
HD = 128
EPS = 1e-6
LEVELS = (32, 16, 8, 4, 2, 1)

ADAM_LR = 0.001
ADAM_B1 = 0.9
ADAM_B2 = 0.999
ADAM_EPS = 1e-08
ADAM_WD = 0.01
ADAM_STEP = 10

VMEM_LIMIT = 56 * 1024 * 1024


def _params(**kw):
    return pltpu.CompilerParams(vmem_limit_bytes=VMEM_LIMIT, **kw)


def _whole_vmem():
    return pl.BlockSpec(memory_space=pltpu.VMEM)


def _any():
    return pl.BlockSpec(memory_space=pl.ANY)


class _Exchange:
    def __init__(self, ins, outs, send_sems, recv_sems, local_sems, scatter):
        self.ins, self.outs, self.scatter = ins, outs, scatter
        self.send_sems, self.recv_sems, self.local_sems = send_sems, recv_sems, local_sems
        x, y, c = lax.axis_index("x"), lax.axis_index("y"), lax.axis_index("c")
        self.me = 4 * x + 2 * y + c
        self.peers = []
        for k in range(1, N_DEV):
            peer = (1 - x if (k >> 2) & 1 else x, 1 - y if (k >> 1) & 1 else y, 1 - c if k & 1 else c)
            self.peers.append((peer, 4 * peer[0] + 2 * peer[1] + peer[2]))

    def _src(self, a, idx):
        return self.ins[a].at[idx] if self.scatter else self.ins[a]

    def _local(self):
        return [pltpu.make_async_copy(self._src(a, self.me), self.outs[a].at[self.me], self.local_sems.at[a])
                for a in range(len(self.ins))]

    def _remote(self, a, k, dst_slot):
        peer, peer_idx = self.peers[k]
        return pltpu.make_async_remote_copy(
            src_ref=self._src(a, peer_idx), dst_ref=self.outs[a].at[dst_slot],
            send_sem=self.send_sems.at[a, k], recv_sem=self.recv_sems.at[a, k],
            device_id=peer, device_id_type=MESH)

    def start(self):
        for cp in self._local():
            cp.start()
        for k in range(N_DEV - 1):
            for a in range(len(self.ins)):
                self._remote(a, k, self.me).start()

    def wait(self):
        for k in range(N_DEV - 1):
            for a in range(len(self.ins)):
                self._remote(a, k, self.peers[k][1]).wait()
        for cp in self._local():
            cp.wait()


class _Gather2:
    def __init__(self, ins, outs, send_sems, recv_sems, local_sems):
        self.ins, self.outs = ins, outs
        self.send_sems, self.recv_sems, self.local_sems = send_sems, recv_sems, local_sems
        x, y, c = lax.axis_index("x"), lax.axis_index("y"), lax.axis_index("c")
        self.c = c
        self.me = 4 * x + 2 * y + c
        self.sibling = (x, y, 1 - c)
        self.chips = [(1 - x, y), (x, 1 - y), (1 - x, 1 - y)]

    @staticmethod
    def _idx(px, py, pc):
        return 4 * px + 2 * py + pc

    def _copy(self, a, k, slot, to, own):
        src = self.ins[a] if own else self.outs[a].at[slot]
        return pltpu.make_async_remote_copy(
            src_ref=src, dst_ref=self.outs[a].at[slot],
            send_sem=self.send_sems.at[a, k], recv_sem=self.recv_sems.at[a, k],
            device_id=to, device_id_type=MESH)

    def _local(self):
        return [pltpu.make_async_copy(self.ins[a], self.outs[a].at[self.me], self.local_sems.at[a])
                for a in range(len(self.ins))]

    def start(self):
        for cp in self._local():
            cp.start()
        for a in range(len(self.ins)):
            self._copy(a, 0, self.me, self.sibling, True).start()
            for j, chip in enumerate(self.chips):
                self._copy(a, 1 + j, self.me, (*chip, self.c), True).start()

    def forward(self):
        for j, chip in enumerate(self.chips):
            for a in range(len(self.ins)):
                slot = self._idx(*chip, self.c)
                self._copy(a, 1 + j, slot, (*chip, self.c), True).wait_recv()
                self._copy(a, 4 + j, slot, self.sibling, False).start()

    def finish(self):
        for a in range(len(self.ins)):
            self._copy(a, 0, self._idx(*self.sibling), self.sibling, True).wait_recv()
            for j, chip in enumerate(self.chips):
                self._copy(a, 4 + j, self._idx(*chip, 1 - self.c), self.sibling, False).wait_recv()
            self._copy(a, 0, self.me, self.sibling, True).wait_send()
            for j, chip in enumerate(self.chips):
                self._copy(a, 1 + j, self.me, (*chip, self.c), True).wait_send()
                self._copy(a, 4 + j, self._idx(*chip, self.c), self.sibling, False).wait_send()
        for cp in self._local():
            cp.wait()


def _exchange_sems(n):
    return [pltpu.SemaphoreType.DMA((n, N_DEV - 1)), pltpu.SemaphoreType.DMA((n, N_DEV - 1)), pltpu.SemaphoreType.DMA((n,))]


def _exchange_out_shapes(arrays, scatter):
    return [jax.ShapeDtypeStruct(a.shape if scatter else (N_DEV,) + a.shape, a.dtype) for a in arrays]


def _exchange(arrays, name, scatter, in_vmem, two_level=False):
    n = len(arrays)

    def body(*refs):
        if two_level:
            ga = _Gather2(refs[:n], refs[n:2 * n], *refs[2 * n:])
            ga.start()
            ga.forward()
            ga.finish()
        else:
            ex = _Exchange(refs[:n], refs[n:2 * n], *refs[2 * n:], scatter)
            ex.start()
            ex.wait()

    spec = _whole_vmem if in_vmem else _any
    return pl.pallas_call(
        body, name=name,
        out_shape=tuple(_exchange_out_shapes(arrays, scatter)),
        in_specs=[spec() for _ in arrays],
        out_specs=tuple(spec() for _ in arrays),
        scratch_shapes=_exchange_sems(n),
        compiler_params=_params(has_side_effects=True),
    )(*arrays)


def _cast_bf16(arrays, name):
    n = len(arrays)

    def body(*refs):
        for i in range(n):
            refs[n + i][...] = refs[i][...].astype(BF)

    return pl.pallas_call(
        body, name=name,
        out_shape=tuple(jax.ShapeDtypeStruct(a.shape, BF) for a in arrays),
        in_specs=[_whole_vmem() for _ in arrays],
        out_specs=tuple(_whole_vmem() for _ in arrays),
        compiler_params=_params(),
    )(*arrays)


def _load_columns(stacked_ref, full_ref, sems):
    c = stacked_ref.shape[2]
    cps = [pltpu.make_async_copy(stacked_ref.at[d], full_ref.at[:, pl.ds(d * c, c)], sems.at[d]) for d in range(N_DEV)]
    for cp in cps:
        cp.start()
    for cp in cps:
        cp.wait()


def _sig(v):
    return 0.5 * jnp.tanh(0.5 * v) + 0.5


@jax.custom_vjp
def _silu(v):
    return v * _sig(v)


def _silu_fwd(v):
    return _silu(v), v


def _silu_bwd(v, g):
    s = _sig(v)
    return (g * (s * (1.0 + v * (1.0 - s))),)


_silu.defvjp(_silu_fwd, _silu_bwd)


@jax.custom_vjp
def _sigmoid_rel(v):
    e = jnp.exp(-jnp.abs(v))
    d = 1.0 + e
    r = pl.reciprocal(d, approx=True)
    r = r * (2.0 - d * r)
    r = r * (2.0 - d * r)
    return jnp.where(v >= 0.0, r, e * r)


def _sigmoid_rel_fwd(v):
    s = _sigmoid_rel(v)
    return s, s


def _sigmoid_rel_bwd(s, g):
    return (g * (s * (1.0 - s)),)


_sigmoid_rel.defvjp(_sigmoid_rel_fwd, _sigmoid_rel_bwd)


def _gelu(v):
    return 0.5 * v * (1.0 + lax.erf(v * 0.7071067811865476))


def _dot(a, b, ca, cb):
    return lax.dot_general(a, b, (((ca,), (cb,)), ((), ())), preferred_element_type=F32)


@jax.custom_vjp
def _mm(a, b):
    return _dot(a, b, 1, 0)


def _mm_fwd(a, b):
    return _dot(a, b, 1, 0), (a, b)


def _mm_bwd(res, g):
    a, b = res
    gb = g.astype(BF)
    return _dot(gb, b, 1, 1).astype(a.dtype), _dot(a, gb, 0, 0).astype(b.dtype)


_mm.defvjp(_mm_fwd, _mm_bwd)


@jax.custom_vjp
def _nt(a, b):
    return _dot(a, b, 1, 1)


def _nt_fwd(a, b):
    return _dot(a, b, 1, 1), (a, b)


def _nt_bwd(res, g):
    a, b = res
    gb = g.astype(BF)
    return _dot(gb, b, 1, 0).astype(a.dtype), _dot(gb, a, 0, 0).astype(b.dtype)


_nt.defvjp(_nt_fwd, _nt_bwd)


@jax.custom_vjp
def _tn(a, b):
    return _dot(a, b, 0, 0)


def _tn_fwd(a, b):
    return _dot(a, b, 0, 0), (a, b)


def _tn_bwd(res, g):
    a, b = res
    gb = g.astype(BF)
    return _dot(b, gb, 1, 1).astype(a.dtype), _dot(a, gb, 1, 0).astype(b.dtype)


_tn.defvjp(_tn_fwd, _tn_bwd)


def _ada_block(c_all, w_ada, b_cols):
    def body(c_ref, w_ref, b_ref, ada_ref, cact_ref):
        ca = _silu(c_ref[...])
        cact_ref[...] = ca
        ada_ref[...] = _mm(ca.astype(BF), w_ref[...].astype(BF)) + b_ref[...]

    return pl.pallas_call(
        body, name="ada_block",
        out_shape=(jax.ShapeDtypeStruct((N_DEV, w_ada.shape[1]), F32), jax.ShapeDtypeStruct(c_all.shape, F32)),
        in_specs=[_whole_vmem()] * 3, out_specs=(_whole_vmem(), _whole_vmem()),
        compiler_params=_params(),
    )(c_all, w_ada, b_cols)


def _decay_tables():
    t = np.arange(CH)
    tri = (t[None, :] <= t[:, None]).astype(np.float32)
    masks = []
    for h in LEVELS:
        m = (t // (2 * h)) * (2 * h) + h
        upper = t >= m
        same = (t[:, None] // (2 * h)) == (t[None, :] // (2 * h))
        masks.append(same & upper[:, None] & (~upper)[None, :])
    masks.append(np.eye(CH, dtype=bool))
    lv = [np.where((t % (2 * h)) >= h, 1.0, -1.0) for h in LEVELS[:4]]
    m4 = t % 4
    lv += [(m4 == 0) * 1.0, (m4 >= 2) * 1.0, (m4 == 3) * 1.0, (t % 2 == 1) * 1.0]
    lvl = np.broadcast_to(np.stack(lv)[:, :, None], (8, CH, D // 2)).astype(np.float32)
    cid = np.arange(BLK) // CH
    gmask = (cid[:, None] >= cid[None, :]).astype(np.float32)
    return (jnp.asarray(tri, BF), jnp.asarray(np.stack(masks).astype(np.float32)), jnp.asarray(gmask), jnp.asarray(lvl))


def _split2(v):
    v1 = v.astype(BF)
    return v1, (v - v1.astype(F32)).astype(BF)


@jax.custom_vjp
def _cumsum_mm(tri, v):
    p1, p2 = _split2(v)
    return _dot(tri, p1, 1, 0) + _dot(tri, p2, 1, 0)


def _cumsum_mm_fwd(tri, v):
    return _cumsum_mm(tri, v), tri


def _cumsum_mm_bwd(tri, g):
    p1, p2 = _split2(g)
    return jnp.zeros_like(tri), _dot(tri, p1, 0, 0) + _dot(tri, p2, 0, 0)


_cumsum_mm.defvjp(_cumsum_mm_fwd, _cumsum_mm_bwd)


def _make_row_roll(shift):
    @jax.custom_vjp
    def roll(x):
        return pltpu.roll(x, shift % CH, 0)

    def fwd(x):
        return roll(x), None

    def bwd(_, g):
        return (pltpu.roll(g, (-shift) % CH, 0),)

    roll.defvjp(fwd, bwd)
    return roll


_prev_row = _make_row_roll(1)
_next_row = _make_row_roll(-1)


def _mix_tile(proj, state, w_s, b_s_t, ln_w, ln_b, lower, gn_w, consts):
    tri, masks, gmask, lvl = consts
    mt = proj.shape[0]
    u = proj[:, 0:512]
    v = proj[:, 512:1024]
    q = proj[:, 1024:1536]
    fl = proj[:, 1536:2048]
    inp = proj[:, 2048:2560]
    g = proj[:, 2560:3072]

    ug = _gelu(u)
    vg = _gelu(v)
    mu = jnp.mean(vg, axis=-1, keepdims=True)
    vc = vg - mu
    var = jnp.mean(vc * vc, axis=-1, keepdims=True)
    vnb = (vc * lax.rsqrt(var + EPS) * ln_w + ln_b).astype(BF)
    wsm = [(w_s[h] * gmask).astype(BF) for h in range(NH)]
    ya = [[None] * NH for _ in range(mt // BLK)]
    for bi in range(mt // BLK):
        rows = slice(bi * BLK, (bi + 1) * BLK)
        for h in range(NH):
            cols = slice(h * HD, (h + 1) * HD)
            ya[bi][h] = ug[rows, cols] * (_mm(wsm[h], vnb[rows, cols]) + b_s_t[:, h:h + 1])
    ya_full = jnp.concatenate([jnp.concatenate(r, axis=1) for r in ya], axis=0)

    l0 = lower[0:1, :]
    l1 = lower[1:2, :]
    mx = jnp.maximum(l0, l1)
    e0 = jnp.exp(l0 - mx)
    e1 = jnp.exp(l1 - mx)
    lb = e0 / (e0 + e1)
    qf = _silu(q)
    f = lb + (1.0 - lb) * _sigmoid_rel(fl)
    logf = jnp.log(f)
    kk = 1.0 - f
    gate = _silu(g)
    nl = len(LEVELS)
    half = D // 2
    keep = [masks[i] > 0.5 for i in range(nl + 1)]
    st = list(state)
    yb = [[None] * NH for _ in range(mt // CH)]
    for ci in range(mt // CH):
        rows = slice(ci * CH, (ci + 1) * CH)
        lc = logf[rows]
        b = _cumsum_mm(tri, lc)
        xb = jnp.exp(b)
        xinv = jnp.exp(b[CH - 1:CH, :] - b)
        xl = []
        for i, hs in enumerate(LEVELS[:3]):
            refs = [jnp.broadcast_to(b[r:r + 1, :], (2 * hs, half)) for r in range(hs - 1, CH, 2 * hs)]
            bref = refs[0] if len(refs) == 1 else jnp.concatenate(refs, axis=0)
            xl.append(jnp.exp(lvl[i] * (b - bref)))
        b3 = b.reshape(CH // 8, 8, half)
        bref = jnp.broadcast_to(b3[:, 3:4, :], (CH // 8, 8, half)).reshape(CH, half)
        xl.append(jnp.exp(lvl[3] * (b - bref)))
        xl.append(jnp.exp(lvl[4] * _next_row(lc) + lvl[5] * lc + lvl[6] * _prev_row(lc)))
        xl.append(jnp.exp(lvl[7] * lc))
        qc = qf[rows]
        kc = kk[rows]
        zsrc = [jnp.concatenate([(qc if (r0 // hs) % 2 == 1 else kc)[r0:r0 + hs] for r0 in range(0, CH, hs)], axis=0)
                for hs in LEVELS[:3]]
        for h in range(NH):
            cols = slice(h * HD, (h + 1) * HD)
            qh = qc[:, cols]
            kh = kc[:, cols]
            vh = inp[rows, cols].astype(BF)
            inter = _nt((qh * xb[:, cols]).astype(BF), st[h].astype(BF))
            attn = jnp.where(keep[nl], _nt(qh.astype(BF), kh.astype(BF)), 0.0)
            for li in range(nl):
                if li < 3:
                    z = (zsrc[li][:, cols] * xl[li][:, cols]).astype(BF)
                    pairs = _nt(z, z)
                else:
                    xx = xl[li][:, cols]
                    pairs = _nt((qh * xx).astype(BF), (kh * xx).astype(BF))
                attn = jnp.where(keep[li], pairs, attn)
            o = inter + _mm(attn.astype(BF), vh)
            st[h] = st[h] * xb[CH - 1:CH, cols] + _tn(vh, (kh * xinv[:, cols]).astype(BF))
            rs = lax.rsqrt(jnp.mean(o * o, axis=-1, keepdims=True) + EPS)
            yb[ci][h] = o * rs * gn_w * gate[rows, cols]
    yb_full = jnp.concatenate([jnp.concatenate(r, axis=1) for r in yb], axis=0)
    return jnp.concatenate([ya_full, yb_full], axis=1), tuple(st)


def _fwd_mix(proj, w_s, b_s_t, ln_w, ln_b, lower, gn_w, tables, to_gather):
    t = proj.shape[0]
    mt = _row_tile(t, MIX_TILE)
    nt_ = t // mt
    nc = len(tables)
    ng = len(to_gather)

    def body(*refs):
        proj_ref, ws_ref, bs_ref, lw_ref, lb_ref, lo_ref, gn_ref = refs[:7]
        c_refs = refs[7:7 + nc]
        refs = refs[7 + nc:]
        g_in = refs[:ng]
        y_ref, st_ref = refs[ng:2 + ng]
        g_out = refs[2 + ng:2 + 2 * ng]
        state = refs[2 + 2 * ng]
        ga = _Gather2(g_in, g_out, *refs[3 + 2 * ng:])

        @pl.when(pl.program_id(0) == 0)
        def _():
            state[...] = jnp.zeros_like(state)
            ga.start()

        @pl.when(pl.program_id(0) == (3 * nt_) // 4)
        def _():
            ga.forward()

        st = tuple(state[h] for h in range(NH))
        for h in range(NH):
            st_ref[h] = st[h]
        y, new = _mix_tile(proj_ref[...], st, ws_ref[...], bs_ref[...], lw_ref[...], lb_ref[...], lo_ref[...], gn_ref[...],
                           tuple(r[...] for r in c_refs))
        y_ref[...] = y.astype(BF)
        for h in range(NH):
            state[h] = new[h]

        @pl.when(pl.program_id(0) == nt_ - 1)
        def _():
            ga.finish()

    full = lambda a: pl.BlockSpec(a.shape, lambda i, nd=a.ndim: (0,) * nd)
    outs = pl.pallas_call(
        body, name="fwd_mix", grid=(nt_,),
        out_shape=(jax.ShapeDtypeStruct((t, D), BF), jax.ShapeDtypeStruct((nt_, NH, HD, HD), F32),
                   *_exchange_out_shapes(to_gather, False)),
        in_specs=[pl.BlockSpec((mt, D_IN), lambda i: (i, 0)), full(w_s), full(b_s_t), full(ln_w), full(ln_b), full(lower),
                  full(gn_w)] + [full(a) for a in tables] + [_any() for _ in to_gather],
        out_specs=(pl.BlockSpec((mt, D), lambda i: (i, 0)), pl.BlockSpec((None, NH, HD, HD), lambda i: (i, 0, 0, 0)),
                   *[_any() for _ in to_gather]),
        scratch_shapes=[pltpu.VMEM((NH, HD, HD), F32)] + _exchange_sems(ng),
        compiler_params=_params(dimension_semantics=("arbitrary",), has_side_effects=True),
    )(proj, w_s, b_s_t, ln_w, ln_b, lower, gn_w, *tables, *to_gather)
    return outs[0], outs[1], outs[2:]


def _bwd_mix(proj, dycat, states, w_s, b_s_t, ln_w, ln_b, lower, gn_w, tables, to_scatter):
    t = proj.shape[0]
    mt = _row_tile(t, MIX_TILE)
    nt_ = t // mt
    nc = len(tables)
    ns = len(to_scatter)

    def body(*refs):
        proj_ref, dy_ref, st_ref, ws_ref, bs_ref, lw_ref, lb_ref, lo_ref, gn_ref = refs[:9]
        c_refs = refs[9:9 + nc]
        refs = refs[9 + nc:]
        s_in = refs[:ns]
        dproj_ref = refs[ns]
        acc = refs[1 + ns:7 + ns]
        s_out = refs[7 + ns:7 + 2 * ns]
        dstate = refs[7 + 2 * ns]
        ex = _Exchange(s_in, s_out, *refs[8 + 2 * ns:], True)

        @pl.when(pl.program_id(0) == 0)
        def _():
            dstate[...] = jnp.zeros_like(dstate)
            for r in acc:
                r[...] = jnp.zeros_like(r)
            ex.start()

        consts = tuple(r[...] for r in c_refs)

        def f(p, s, ws, bs, lw, lb_, lo, gn):
            return _mix_tile(p, s, ws, bs, lw, lb_, lo, gn, consts)

        st = tuple(st_ref[h] for h in range(NH))
        _, vjp = jax.vjp(f, proj_ref[...], st, ws_ref[...], bs_ref[...], lw_ref[...], lb_ref[...], lo_ref[...], gn_ref[...])
        grads = vjp((dy_ref[...], tuple(dstate[h] for h in range(NH))))
        dproj_ref[...] = grads[0].astype(BF)
        for h in range(NH):
            dstate[h] = grads[1][h]
        for r, gval in zip(acc, grads[2:]):
            r[...] += gval

        @pl.when(pl.program_id(0) == nt_ - 1)
        def _():
            ex.wait()

    full = lambda a: pl.BlockSpec(a.shape, lambda i, nd=a.ndim: (0,) * nd)
    rev = lambda i: (nt_ - 1 - i, 0)
    smalls = (w_s, b_s_t, ln_w, ln_b, lower, gn_w)
    outs = pl.pallas_call(
        body, name="bwd_mix", grid=(nt_,),
        out_shape=(jax.ShapeDtypeStruct((t, D_IN), BF), *[jax.ShapeDtypeStruct(a.shape, F32) for a in smalls],
                   *_exchange_out_shapes(to_scatter, True)),
        in_specs=[pl.BlockSpec((mt, D_IN), rev), pl.BlockSpec((mt, D), rev),
                  pl.BlockSpec((None, NH, HD, HD), lambda i: (nt_ - 1 - i, 0, 0, 0))]
        + [full(a) for a in smalls] + [full(a) for a in tables] + [_any() for _ in to_scatter],
        out_specs=(pl.BlockSpec((mt, D_IN), rev), *[full(a) for a in smalls], *[_any() for _ in to_scatter]),
        scratch_shapes=[pltpu.VMEM((NH, HD, HD), F32)] + _exchange_sems(ns),
        compiler_params=_params(dimension_semantics=("arbitrary",), has_side_effects=True),
    )(proj, dycat, states, w_s, b_s_t, ln_w, ln_b, lower, gn_w, *tables, *to_scatter)
    return outs[0], outs[1:7], outs[7:]


def _row_tile(t, want):
    return want if t % want == 0 else t


def _rms(v):
    rstd = lax.rsqrt(jnp.mean(v * v, axis=-1, keepdims=True) + EPS)
    return v * rstd, rstd


def _rms_bwd(dxhat, xhat, rstd):
    return rstd * (dxhat - xhat * jnp.mean(dxhat * xhat, axis=-1, keepdims=True))


def _colsum(v):
    return jnp.sum(v, axis=0, keepdims=True)


def _fwd_in(x, ada, n1w, w_in_st):
    t = x.shape[0]
    tm = _row_tile(t, 512)

    def body(x_ref, ada_ref, n1_ref, wst_ref, proj_ref, h1_ref, w_ref, sems):
        @pl.when(pl.program_id(0) == 0)
        def _():
            _load_columns(wst_ref, w_ref, sems)

        xh, _ = _rms(x_ref[...])
        h1 = (xh * n1_ref[...]) * (1.0 + ada_ref[1:2, :]) + ada_ref[0:1, :]
        h1b = h1.astype(BF)
        h1_ref[...] = h1b
        proj_ref[...] = _mm(h1b, w_ref[...])

    full = lambda a: pl.BlockSpec(a.shape, lambda i, nd=a.ndim: (0,) * nd)
    return pl.pallas_call(
        body, name="fwd_in", grid=(t // tm,),
        out_shape=(jax.ShapeDtypeStruct((t, D_IN), F32), jax.ShapeDtypeStruct((t, D), BF)),
        in_specs=[pl.BlockSpec((tm, D), lambda i: (i, 0)), full(ada), full(n1w), _any()],
        out_specs=(pl.BlockSpec((tm, D_IN), lambda i: (i, 0)), pl.BlockSpec((tm, D), lambda i: (i, 0))),
        scratch_shapes=[pltpu.VMEM((D, D_IN), BF), pltpu.SemaphoreType.DMA((N_DEV,))],
        compiler_params=_params(dimension_semantics=("arbitrary",)),
    )(x, ada, n1w, w_in_st)


def _fwd_ffn(x, ycat, tgt, ada, n2w, fw, w_out, w_fi, w_fo):
    t = x.shape[0]
    tm = _row_tile(t, 256)

    def body(x_ref, y_ref, t_ref, ada_ref, n2_ref, fw_ref, wo_ref, wi_ref, wf_ref,
             x1_ref, h2_ref, act_ref, gu_ref, dffn_ref, dx2_ref, part_ref):
        @pl.when(pl.program_id(0) == 0)
        def _():
            part_ref[...] = jnp.zeros_like(part_ref)

        g1, sh2, sc2, g2 = ada_ref[2:3, :], ada_ref[3:4, :], ada_ref[4:5, :], ada_ref[5:6, :]
        x1 = x_ref[...] + g1 * _mm(y_ref[...], wo_ref[...])
        x1_ref[...] = x1
        xh2, _ = _rms(x1)
        h2b = ((xh2 * n2_ref[...]) * (1.0 + sc2) + sh2).astype(BF)
        h2_ref[...] = h2b
        ffn = jnp.zeros((tm, D), F32)
        for j in range(4):
            gate = _nt(h2b, wi_ref[j])
            up = _nt(h2b, wi_ref[j + 4])
            gu_ref[j] = gate.astype(BF)
            gu_ref[j + 4] = up.astype(BF)
            a = (_silu(gate) * up).astype(BF)
            act_ref[j] = a
            ffn = ffn + _mm(a, wf_ref[j * FF_BLK:(j + 1) * FF_BLK, :])
        x2 = x1 + g2 * ffn
        xh3, rstd3 = _rms(x2)
        err = xh3 * fw_ref[...] - t_ref[...]
        dy = err * (1.0 / D)
        dx2 = _rms_bwd(dy * fw_ref[...], xh3, rstd3)
        dx2_ref[...] = dx2
        dffn_ref[...] = (g2 * dx2).astype(BF)
        part_ref[0:1, :] += _colsum(dx2 * ffn)
        part_ref[1:2, :] += _colsum(dy * xh3)
        part_ref[2:3, :] += jnp.zeros((1, D), F32) + (0.5 / D) * jnp.sum(err * err)

    full = lambda a: pl.BlockSpec(a.shape, lambda i, nd=a.ndim: (0,) * nd)
    row = lambda w: pl.BlockSpec((tm, w), lambda i: (i, 0))
    return pl.pallas_call(
        body, name="fwd_ffn", grid=(t // tm,),
        out_shape=(jax.ShapeDtypeStruct((t, D), F32), jax.ShapeDtypeStruct((t, D), BF),
                   jax.ShapeDtypeStruct((4, t, FF_BLK), BF), jax.ShapeDtypeStruct((N_DEV, t, FF_BLK), BF),
                   jax.ShapeDtypeStruct((t, D), BF), jax.ShapeDtypeStruct((t, D), F32), jax.ShapeDtypeStruct((8, D), F32)),
        in_specs=[row(D), row(D), row(D), full(ada), full(n2w), full(fw), _whole_vmem(), _whole_vmem(), _whole_vmem()],
        out_specs=(row(D), row(D), pl.BlockSpec((4, tm, FF_BLK), lambda i: (0, i, 0)),
                   pl.BlockSpec((N_DEV, tm, FF_BLK), lambda i: (0, i, 0)), row(D), row(D),
                   pl.BlockSpec((8, D), lambda i: (0, 0))),
        compiler_params=_params(dimension_semantics=("arbitrary",)),
    )(x, ycat, tgt, ada, n2w, fw, w_out, w_fi, w_fo)


def _bwd_ffn(x1, gu, dffn, dx2, ycat, ada, n2w, w_out, w_fi, w_fo):
    t = x1.shape[0]
    tm = _row_tile(t, 256)

    def body(x1_ref, gu_ref, dffn_ref, dx2_ref, y_ref, ada_ref, n2_ref, wo_ref, wi_ref, wf_ref,
             dgu_ref, dx1_ref, dmix_ref, dycat_ref, part_ref):
        @pl.when(pl.program_id(0) == 0)
        def _():
            part_ref[...] = jnp.zeros_like(part_ref)

        g1, sc2 = ada_ref[2:3, :], ada_ref[4:5, :]
        dffn = dffn_ref[...]
        dh2 = jnp.zeros((tm, D), F32)
        for j in range(4):
            gate = gu_ref[j].astype(F32)
            up = gu_ref[j + 4].astype(F32)
            dact = _nt(dffn, wf_ref[j * FF_BLK:(j + 1) * FF_BLK, :])
            sg = _sig(gate)
            dgate = (dact * up * (sg * (1.0 + gate * (1.0 - sg)))).astype(BF)
            dup = (dact * (gate * sg)).astype(BF)
            dgu_ref[j] = dgate
            dgu_ref[j + 4] = dup
            dh2 = dh2 + _mm(dgate, wi_ref[j]) + _mm(dup, wi_ref[j + 4])
        x1 = x1_ref[...]
        xh2, rstd2 = _rms(x1)
        xn2 = xh2 * n2_ref[...]
        dxn2 = dh2 * (1.0 + sc2)
        dx1 = dx2_ref[...] + _rms_bwd(dxn2 * n2_ref[...], xh2, rstd2)
        dx1_ref[...] = dx1
        mix = _mm(y_ref[...], wo_ref[...])
        dmix = (g1 * dx1).astype(BF)
        dmix_ref[...] = dmix
        dycat_ref[...] = _nt(dmix, wo_ref[...])
        part_ref[0:1, :] += _colsum(dh2)
        part_ref[1:2, :] += _colsum(dh2 * xn2)
        part_ref[2:3, :] += _colsum(dxn2 * xh2)
        part_ref[3:4, :] += _colsum(dx1 * mix)

    full = lambda a: pl.BlockSpec(a.shape, lambda i, nd=a.ndim: (0,) * nd)
    row = lambda w: pl.BlockSpec((tm, w), lambda i: (i, 0))
    return pl.pallas_call(
        body, name="bwd_ffn", grid=(t // tm,),
        out_shape=(jax.ShapeDtypeStruct((N_DEV, t, FF_BLK), BF), jax.ShapeDtypeStruct((t, D), F32),
                   jax.ShapeDtypeStruct((t, D), BF), jax.ShapeDtypeStruct((t, D), F32), jax.ShapeDtypeStruct((8, D), F32)),
        in_specs=[row(D), pl.BlockSpec((N_DEV, tm, FF_BLK), lambda i: (0, i, 0)), row(D), row(D), row(D), full(ada), full(n2w),
                  _whole_vmem(), _whole_vmem(), _whole_vmem()],
        out_specs=(pl.BlockSpec((N_DEV, tm, FF_BLK), lambda i: (0, i, 0)), row(D), row(D), row(D),
                   pl.BlockSpec((8, D), lambda i: (0, 0))),
        compiler_params=_params(dimension_semantics=("arbitrary",)),
    )(x1, gu, dffn, dx2, ycat, ada, n2w, w_out, w_fi, w_fo)


def _bwd_in(x, dproj, dx1, ada, n1w, w_in_st, to_scatter, to_gather):
    t = x.shape[0]
    tm = _row_tile(t, 1024)
    n_t = t // tm
    ns = len(to_scatter)
    ng = len(to_gather)
    nx = ns + ng

    def body(*refs):
        x_ref, dp_ref, dx1_ref, ada_ref, n1_ref, wst_ref = refs[:6]
        x_in = refs[6:6 + nx]
        gx_ref, part_ref = refs[6 + nx:8 + nx]
        x_out = refs[8 + nx:8 + 2 * nx]
        w_ref, sems = refs[8 + 2 * nx:10 + 2 * nx]
        sem_refs = refs[10 + 2 * nx:]
        ex = _Exchange(x_in[:ns], x_out[:ns], *sem_refs[:3], True)
        gx = _Exchange(x_in[ns:], x_out[ns:], *sem_refs[3:], False) if ng else None

        @pl.when(pl.program_id(0) == 0)
        def _():
            ex.start()
            if ng:
                gx.start()
            part_ref[...] = jnp.zeros_like(part_ref)
            _load_columns(wst_ref, w_ref, sems)

        dh1 = _nt(dp_ref[...], w_ref[...])
        xh, rstd = _rms(x_ref[...])
        xn = xh * n1_ref[...]
        dxn = dh1 * (1.0 + ada_ref[1:2, :])
        gx_ref[...] = dx1_ref[...] + _rms_bwd(dxn * n1_ref[...], xh, rstd)
        part_ref[0:1, :] += _colsum(dh1)
        part_ref[1:2, :] += _colsum(dh1 * xn)
        part_ref[2:3, :] += _colsum(dxn * xh)

        @pl.when(pl.program_id(0) == n_t - 1)
        def _():
            ex.wait()
            if ng:
                gx.wait()

    full = lambda a: pl.BlockSpec(a.shape, lambda i, nd=a.ndim: (0,) * nd)
    row = lambda w: pl.BlockSpec((tm, w), lambda i: (i, 0))
    outs = pl.pallas_call(
        body, name="bwd_in", grid=(n_t,),
        out_shape=(jax.ShapeDtypeStruct((t, D), F32), jax.ShapeDtypeStruct((8, D), F32),
                   *_exchange_out_shapes(to_scatter, True), *_exchange_out_shapes(to_gather, False)),
        in_specs=[row(D), row(D_IN), row(D), full(ada), full(n1w), _any()] + [_any() for _ in range(nx)],
        out_specs=(row(D), pl.BlockSpec((8, D), lambda i: (0, 0)), *[_any() for _ in range(nx)]),
        scratch_shapes=[pltpu.VMEM((D, D_IN), BF), pltpu.SemaphoreType.DMA((N_DEV,))] + _exchange_sems(ns)
        + (_exchange_sems(ng) if ng else []),
        compiler_params=_params(dimension_semantics=("arbitrary",), has_side_effects=True),
    )(x, dproj, dx1, ada, n1w, w_in_st, *to_scatter, *to_gather)
    return outs[0], outs[1], outs[2:2 + ns], outs[2 + ns:]


def _wgrad(a, b, name, a_spec, b_spec, out_shape, out_spec, grid, acc_shape, split=1, to_gather=()):
    n_j, n_t = grid
    ng = len(to_gather)

    def body(*refs):
        a_ref, b_ref = refs[:2]
        g_in = refs[2:2 + ng]
        o_ref = refs[2 + ng]
        g_out = refs[3 + ng:3 + 2 * ng]
        acc = refs[3 + 2 * ng]
        first = (pl.program_id(0) == 0) & (pl.program_id(1) == 0)
        last = (pl.program_id(0) == n_j - 1) & (pl.program_id(1) == n_t - 1)
        if ng:
            gx = _Exchange(g_in, g_out, *refs[4 + 2 * ng:], False)

            @pl.when(first)
            def _():
                gx.start()

        @pl.when(pl.program_id(1) == 0)
        def _():
            acc[...] = jnp.zeros_like(acc)

        acc[...] += _tn(a_ref[...], b_ref[...])

        @pl.when(pl.program_id(1) == n_t - 1)
        def _():
            if split == 1:
                o_ref[...] = acc[...].astype(BF)
            else:
                w = acc_shape[1] // split
                for s in range(split):
                    o_ref[s] = acc[:, s * w:(s + 1) * w].astype(BF)

        if ng:
            @pl.when(last)
            def _():
                gx.wait()

    outs = pl.pallas_call(
        body, name=name, grid=grid,
        out_shape=(jax.ShapeDtypeStruct(out_shape, BF), *_exchange_out_shapes(to_gather, False)),
        in_specs=[a_spec, b_spec] + [_any() for _ in to_gather], out_specs=(out_spec, *[_any() for _ in to_gather]),
        scratch_shapes=[pltpu.VMEM(acc_shape, F32)] + (_exchange_sems(ng) if ng else []),
        compiler_params=_params(dimension_semantics=("arbitrary", "arbitrary"), has_side_effects=bool(ng)),
    )(a, b, *to_gather)
    return (outs[0], outs[1:]) if ng else outs[0]


def _adamw_math(w, g, m, v):
    m = ADAM_B1 * m + (1.0 - ADAM_B1) * g
    v = ADAM_B2 * v + (1.0 - ADAM_B2) * (g * g)
    m_hat = m / (1.0 - ADAM_B1 ** ADAM_STEP)
    v_hat = v / (1.0 - ADAM_B2 ** ADAM_STEP)
    delta = -ADAM_LR * (m_hat / (jnp.sqrt(v_hat) + ADAM_EPS) + ADAM_WD * w)
    return delta, m, v


def _adamw_recv(w, m, v, recv, name, tr):
    r, c = w.shape

    def body(w_ref, m_ref, v_ref, r_ref, g_ref, d_ref, nm_ref, nv_ref):
        g = r_ref[0].astype(F32)
        for k in range(1, N_DEV):
            g = g + r_ref[k].astype(F32)
        g_ref[...] = g
        d_ref[...], nm_ref[...], nv_ref[...] = _adamw_math(w_ref[...], g, m_ref[...], v_ref[...])

    row = pl.BlockSpec((tr, c), lambda i: (i, 0))
    return pl.pallas_call(
        body, name=name, grid=(r // tr,),
        out_shape=tuple(jax.ShapeDtypeStruct((r, c), F32) for _ in range(4)),
        in_specs=[row, row, row, pl.BlockSpec((N_DEV, tr, c), lambda i: (0, i, 0))],
        out_specs=(row, row, row, row),
        compiler_params=_params(dimension_semantics=("arbitrary",)),
    )(w, m, v, recv)


def _adamw_ada(w, m, v, cact, dada_cols):
    r, c = w.shape
    tr = 256

    def body(w_ref, m_ref, v_ref, ca_ref, da_ref, g_ref, d_ref, nm_ref, nv_ref):
        g = _tn(ca_ref[...].astype(BF), da_ref[...].astype(BF))
        g_ref[...] = g
        d_ref[...], nm_ref[...], nv_ref[...] = _adamw_math(w_ref[...], g, m_ref[...], v_ref[...])

    row = pl.BlockSpec((tr, c), lambda i: (i, 0))
    return pl.pallas_call(
        body, name="adamw_ada", grid=(r // tr,),
        out_shape=tuple(jax.ShapeDtypeStruct((r, c), F32) for _ in range(4)),
        in_specs=[row, row, row, pl.BlockSpec((N_DEV, tr), lambda i: (0, i)), pl.BlockSpec(dada_cols.shape, lambda i: (0, 0))],
        out_specs=(row, row, row, row),
        compiler_params=_params(dimension_semantics=("arbitrary",)),
    )(w, m, v, cact, dada_cols)


def _adamw_small(gathered, wmv):
    n_g = len(gathered)
    n_p = len(wmv)
    flat = [a for trip in wmv for a in trip]

    def body(*refs):
        g_refs = refs[:n_g]
        p_refs = refs[n_g:n_g + 3 * n_p]
        o_refs = refs[n_g + 3 * n_p:]

        def total(ref):
            s = ref[0]
            for k in range(1, N_DEV):
                s = s + ref[k]
            return s

        f3, b3, b1, dws, dbs, dlnw, dlnb, dlo, dgn = [total(r) for r in g_refs]
        dada_rows = [b1[0:1], b1[1:2], b3[3:4], b3[0:1], b3[1:2], f3[0:1]]
        for r, g in enumerate(dada_rows):
            cs = slice(r * D, (r + 1) * D)
            w, m, v = p_refs[0][:, cs], p_refs[1][:, cs], p_refs[2][:, cs]
            o_refs[0][:, cs] = g
            o_refs[1][:, cs], o_refs[2][:, cs], o_refs[3][:, cs] = _adamw_math(w, g, m, v)
        grads = [None, b1[2:3], dws, dbs, dlnw, dlnb, dlo, dgn, b3[2:3], f3[1:2]]
        for i, g in enumerate(grads):
            if g is None:
                continue
            w, m, v = p_refs[3 * i][...], p_refs[3 * i + 1][...], p_refs[3 * i + 2][...]
            o_refs[4 * i][...] = g
            o_refs[4 * i + 1][...], o_refs[4 * i + 2][...], o_refs[4 * i + 3][...] = _adamw_math(w, g, m, v)
        o_refs[4 * n_p][...] = jnp.zeros((8, 128), F32) + f3[2:3, 0:128]

    out_shape = []
    for w, _, _ in wmv:
        out_shape += [jax.ShapeDtypeStruct(w.shape, F32)] * 4
    out_shape.append(jax.ShapeDtypeStruct((8, 128), F32))
    n_in = n_g + 3 * n_p
    return pl.pallas_call(
        body, name="adamw_small",
        out_shape=tuple(out_shape),
        in_specs=[_whole_vmem()] * n_in, out_specs=tuple(_whole_vmem() for _ in out_shape),
        compiler_params=_params(),
    )(*gathered, *flat)


def kernel(x, c, w_ada, b_ada, norm1_w, w_in, w_s, b_s, v_ln_w, v_ln_b, lower_bounds, gn_w, w_out, norm2_w, w_ffn_in, w_ffn_out, final_norm_w, loss_target, m_w_ada, m_b_ada, m_norm1_w, m_w_in, m_w_s, m_b_s, m_v_ln_w, m_v_ln_b, m_lower_bounds, m_gn_w, m_w_out, m_norm2_w, m_w_ffn_in, m_w_ffn_out, m_final_norm_w, v_w_ada, v_b_ada, v_norm1_w, v_w_in, v_w_s, v_b_s, v_v_ln_w, v_v_ln_b, v_lower_bounds, v_gn_w, v_w_out, v_norm2_w, v_w_ffn_in, v_w_ffn_out, v_final_norm_w):
    me = 4 * lax.axis_index("x") + 2 * lax.axis_index("y") + lax.axis_index("c")
    t = x.shape[1]
    x2d = x.reshape(t, D)
    tgt = loss_target.reshape(t, D)
    ada_cols = w_ada.shape[2]

    tp = lambda a: jnp.swapaxes(a[0], 0, 1)
    win_b, wout_b, wfi_b, wfo_b = _cast_bf16([w_in[0], w_out[0], tp(w_ffn_in), w_ffn_out[0]], "cast_weights")
    win_st, c_st = _exchange([win_b, c], "gather_w_in", False, False, two_level=True)
    c_all = c_st.reshape(N_DEV, D)

    b_cols = lax.dynamic_slice(b_ada, (0, me * ada_cols), (1, ada_cols))
    ada_blk, cact = _ada_block(c_all, w_ada[0], b_cols)
    (ada_st,) = _exchange([ada_blk], "gather_ada", False, True)
    ada = lax.dynamic_index_in_dim(ada_st, me, axis=1, keepdims=False).reshape(6, D)

    tables = _decay_tables()
    ws3 = w_s[0]
    bs_t = b_s[0].T

    proj, h1 = _fwd_in(x2d, ada, norm1_w, win_st)
    ycat, states, (wout_st, wfi_st, wfo_st) = _fwd_mix(proj, ws3, bs_t, v_ln_w, v_ln_b, lower_bounds, gn_w, tables,
                                                       [wout_b, wfi_b, wfo_b])
    w_out_full = wout_st.reshape(D, D)
    w_fo_full = wfo_st.reshape(D_FF, D)
    x1, h2, act, gu, dffn, dx2, part_f = _fwd_ffn(x2d, ycat, tgt, ada, norm2_w, final_norm_w.reshape(1, D),
                                                  w_out_full, wfi_st, w_fo_full)
    dgu, dx1, dmix, dycat, part_b3 = _bwd_ffn(x1, gu, dffn, dx2, ycat, ada, norm2_w, w_out_full, wfi_st, w_fo_full)
    tk = _row_tile(t, 2048)
    n_t = t // tk
    win_cols = D_IN // N_DEV
    dwout = _wgrad(ycat, dmix, "wgrad_out",
                   pl.BlockSpec((tk, D), lambda j, i: (i, 0)), pl.BlockSpec((tk, D), lambda j, i: (i, 0)),
                   (D, D), pl.BlockSpec((D, D), lambda j, i: (0, 0)), (1, n_t), (D, D))
    dwfi = _wgrad(dgu, h2, "wgrad_ffn_in",
                  pl.BlockSpec((None, tk, FF_BLK), lambda j, i: (j, i, 0)), pl.BlockSpec((tk, D), lambda j, i: (i, 0)),
                  (N_DEV, FF_BLK, D), pl.BlockSpec((None, FF_BLK, D), lambda j, i: (j, 0, 0)), (N_DEV, n_t), (FF_BLK, D))
    dwfo = _wgrad(act, dffn, "wgrad_ffn_out",
                  pl.BlockSpec((None, tk, FF_BLK), lambda j, i: (j, i, 0)), pl.BlockSpec((tk, D), lambda j, i: (i, 0)),
                  (4, FF_BLK, D), pl.BlockSpec((None, FF_BLK, D), lambda j, i: (j, 0, 0)), (4, n_t), (FF_BLK, D))
    dproj, (dws, dbs_t, dlnw, dlnb, dlower, dgnw), (r_out, r_fi, r_fo) = _bwd_mix(
        proj, dycat, states, ws3, bs_t, v_ln_w, v_ln_b, lower_bounds, gn_w, tables,
        [dwout.reshape(N_DEV, D // N_DEV, D), dwfi, dwfo.reshape(N_DEV, D_FF // N_DEV, D)])
    dwin, early = _wgrad(h1, dproj, "wgrad_in",
                         pl.BlockSpec((tk, D), lambda j, i: (i, 0)), pl.BlockSpec((tk, 4 * win_cols), lambda j, i: (i, j)),
                         (N_DEV, D, win_cols), pl.BlockSpec((4, D, win_cols), lambda j, i: (j, 0, 0)), (N_DEV // 4, n_t),
                         (D, 4 * win_cols), split=4, to_gather=[part_f, part_b3, dws, dbs_t, dlnw, dlnb, dlower, dgnw])
    grad_x, part_b1, (r_in,), _ = _bwd_in(x2d, dproj, dx1, ada, norm1_w, win_st, [dwin], [])

    g_w_in, d_w_in, nm_w_in, nv_w_in = _adamw_recv(w_in[0], m_w_in[0], v_w_in[0], r_in, "adamw_w_in", 256)
    g_w_out, d_w_out, nm_w_out, nv_w_out = _adamw_recv(w_out[0], m_w_out[0], v_w_out[0], r_out, "adamw_w_out", 128)
    fi_t = _adamw_recv(tp(w_ffn_in), tp(m_w_ffn_in), tp(v_w_ffn_in), r_fi, "adamw_w_ffn_in", 176)
    g_w_fi, d_w_fi, nm_w_fi, nv_w_fi = [jnp.swapaxes(a, 0, 1) for a in fi_t]
    g_w_fo, d_w_fo, nm_w_fo, nv_w_fo = _adamw_recv(w_ffn_out[0], m_w_ffn_out[0], v_w_ffn_out[0], r_fo, "adamw_w_ffn_out", 176)

    (b1_all,) = _exchange([part_b1], "gather_small", False, True)
    gathered = [early[0], early[1], b1_all, *early[2:]]
    f3_all, b3_all = gathered[0], gathered[1]
    dada_all = jnp.stack([b1_all[:, 0], b1_all[:, 1], b3_all[:, 3], b3_all[:, 0], b3_all[:, 1], f3_all[:, 0]], axis=1)
    dada_cols = lax.dynamic_slice(dada_all.reshape(N_DEV, 6 * D), (0, me * ada_cols), (N_DEV, ada_cols))
    g_w_ada, d_w_ada, nm_w_ada, nv_w_ada = _adamw_ada(w_ada[0], m_w_ada[0], v_w_ada[0], cact, dada_cols)

    r1 = lambda a: a.reshape(1, D)
    tr = lambda a: a[0].T
    wmv = [
        (b_ada, m_b_ada, v_b_ada),
        (norm1_w, m_norm1_w, v_norm1_w),
        (w_s[0], m_w_s[0], v_w_s[0]),
        (tr(b_s), tr(m_b_s), tr(v_b_s)),
        (v_ln_w, m_v_ln_w, v_v_ln_w),
        (v_ln_b, m_v_ln_b, v_v_ln_b),
        (lower_bounds, m_lower_bounds, v_lower_bounds),
        (gn_w, m_gn_w, v_gn_w),
        (norm2_w, m_norm2_w, v_norm2_w),
        (r1(final_norm_w), r1(m_final_norm_w), r1(v_final_norm_w)),
    ]
    small = _adamw_small(gathered, wmv)
    loss = small[-1][0, 0]

    def unshape(i, a):
        if i == 2:
            return a.reshape(1, NH, BLK, BLK)
        if i == 3:
            return a.T.reshape(1, NH, BLK)
        if i == 9:
            return a.reshape(D)
        return a

    def small_out(kind):
        return [unshape(i, small[4 * i + kind]) for i in range(len(wmv))]

    e3 = lambda a: a[None]
    big = {
        0: (e3(g_w_ada), e3(g_w_in), e3(g_w_out), e3(g_w_fi), e3(g_w_fo)),
        1: (e3(d_w_ada), e3(d_w_in), e3(d_w_out), e3(d_w_fi), e3(d_w_fo)),
        2: (e3(nm_w_ada), e3(nm_w_in), e3(nm_w_out), e3(nm_w_fi), e3(nm_w_fo)),
        3: (e3(nv_w_ada), e3(nv_w_in), e3(nv_w_out), e3(nv_w_fi), e3(nv_w_fo)),
    }

    def ordered(kind):
        s = small_out(kind)
        b_ = big[kind]
        return [b_[0], s[0], s[1], b_[1], s[2], s[3], s[4], s[5], s[6], s[7], b_[2], s[8], b_[3], b_[4], s[9]]

    return (loss, grad_x.reshape(1, t, D), *ordered(0), *ordered(1), *ordered(2), *ordered(3))
```

```python
import functools

import numpy as np
import jax
import jax.numpy as jnp
from jax import lax
from jax.experimental import pallas as pl
from jax.experimental.pallas import tpu as pltpu

F32 = jnp.float32
BF = jnp.bfloat16
MESH = pl.DeviceIdType.MESH

N_DEV = 8
D = 1024
D_IN = 3072
D_FF = 2816
FF_BLK = D_FF // 4
FF_CHUNKS = ((0, 1536), (1536, D_FF))
FF_PAIR = 2 * FF_BLK
CH = 64
BLK = 128
MIX_TILE = 512
NH = 4
HD = 128
EPS = 1e-6
LEVELS = (32, 16, 8, 4, 2, 1)

ADAM_LR = 0.001
ADAM_B1 = 0.9
ADAM_B2 = 0.999
ADAM_EPS = 1e-08
ADAM_WD = 0.01
ADAM_STEP = 10

VMEM_LIMIT = 56 * 1024 * 1024


def _params(**kw):
    return pltpu.CompilerParams(vmem_limit_bytes=VMEM_LIMIT, **kw)


def _whole_vmem():
    return pl.BlockSpec(memory_space=pltpu.VMEM)


def _any():
    return pl.BlockSpec(memory_space=pl.ANY)


class _Exchange:
    def __init__(self, ins, outs, send_sems, recv_sems, local_sems, scatter):
        self.ins, self.outs, self.scatter = ins, outs, scatter
        self.send_sems, self.recv_sems, self.local_sems = send_sems, recv_sems, local_sems
        x, y, c = lax.axis_index("x"), lax.axis_index("y"), lax.axis_index("c")
        self.me = 4 * x + 2 * y + c
        self.peers = []
        for k in range(1, N_DEV):
            peer = (1 - x if (k >> 2) & 1 else x, 1 - y if (k >> 1) & 1 else y, 1 - c if k & 1 else c)
            self.peers.append((peer, 4 * peer[0] + 2 * peer[1] + peer[2]))

    def _src(self, a, idx):
        return self.ins[a].at[idx] if self.scatter else self.ins[a]

    def _local(self):
        return [pltpu.make_async_copy(self._src(a, self.me), self.outs[a].at[self.me], self.local_sems.at[a])
                for a in range(len(self.ins))]

    def _remote(self, a, k, dst_slot):
        peer, peer_idx = self.peers[k]
        return pltpu.make_async_remote_copy(
            src_ref=self._src(a, peer_idx), dst_ref=self.outs[a].at[dst_slot],
            send_sem=self.send_sems.at[a, k], recv_sem=self.recv_sems.at[a, k],
            device_id=peer, device_id_type=MESH)

    def start(self):
        for cp in self._local():
            cp.start()
        for k in range(N_DEV - 1):
            for a in range(len(self.ins)):
                self._remote(a, k, self.me).start()

    def wait(self):
        for k in range(N_DEV - 1):
            for a in range(len(self.ins)):
                self._remote(a, k, self.peers[k][1]).wait()
        for cp in self._local():
            cp.wait()


class _Gather2:
    def __init__(self, ins, outs, send_sems, recv_sems, local_sems):
        self.ins, self.outs = ins, outs
        self.send_sems, self.recv_sems, self.local_sems = send_sems, recv_sems, local_sems
        x, y, c = lax.axis_index("x"), lax.axis_index("y"), lax.axis_index("c")
        self.c = c
        self.me = 4 * x + 2 * y + c
        self.sibling = (x, y, 1 - c)
        self.chips = [(1 - x, y), (x, 1 - y), (1 - x, 1 - y)]

    @staticmethod
    def _idx(px, py, pc):
        return 4 * px + 2 * py + pc

    def _copy(self, a, k, slot, to, own):
        src = self.ins[a] if own else self.outs[a].at[slot]
        return pltpu.make_async_remote_copy(
            src_ref=src, dst_ref=self.outs[a].at[slot],
            send_sem=self.send_sems.at[a, k], recv_sem=self.recv_sems.at[a, k],
            device_id=to, device_id_type=MESH)

    def _local(self):
        return [pltpu.make_async_copy(self.ins[a], self.outs[a].at[self.me], self.local_sems.at[a])
                for a in range(len(self.ins))]

    def start(self):
        for cp in self._local():
            cp.start()
        for a in range(len(self.ins)):
            self._copy(a, 0, self.me, self.sibling, True).start()
            for j, chip in enumerate(self.chips):
                self._copy(a, 1 + j, self.me, (*chip, self.c), True).start()

    def forward(self):
        for j, chip in enumerate(self.chips):
            for a in range(len(self.ins)):
                slot = self._idx(*chip, self.c)
                self._copy(a, 1 + j, slot, (*chip, self.c), True).wait_recv()
                self._copy(a, 4 + j, slot, self.sibling, False).start()

    def finish(self):
        for a in range(len(self.ins)):
            self._copy(a, 0, self._idx(*self.sibling), self.sibling, True).wait_recv()
            for j, chip in enumerate(self.chips):
                self._copy(a, 4 + j, self._idx(*chip, 1 - self.c), self.sibling, False).wait_recv()
            self._copy(a, 0, self.me, self.sibling, True).wait_send()
            for j, chip in enumerate(self.chips):
                self._copy(a, 1 + j, self.me, (*chip, self.c), True).wait_send()
                self._copy(a, 4 + j, self._idx(*chip, self.c), self.sibling, False).wait_send()
        for cp in self._local():
            cp.wait()


def _exchange_sems(n):
    return [pltpu.SemaphoreType.DMA((n, N_DEV - 1)), pltpu.SemaphoreType.DMA((n, N_DEV - 1)), pltpu.SemaphoreType.DMA((n,))]


def _exchange_out_shapes(arrays, scatter):
    return [jax.ShapeDtypeStruct(a.shape if scatter else (N_DEV,) + a.shape, a.dtype) for a in arrays]


def _exchange(arrays, name, scatter, in_vmem, two_level=False):
    n = len(arrays)

    def body(*refs):
        if two_level:
            ga = _Gather2(refs[:n], refs[n:2 * n], *refs[2 * n:])
            ga.start()
            ga.forward()
            ga.finish()
        else:
            ex = _Exchange(refs[:n], refs[n:2 * n], *refs[2 * n:], scatter)
            ex.start()
            ex.wait()

    spec = _whole_vmem if in_vmem else _any
    return pl.pallas_call(
        body, name=name,
        out_shape=tuple(_exchange_out_shapes(arrays, scatter)),
        in_specs=[spec() for _ in arrays],
        out_specs=tuple(spec() for _ in arrays),
        scratch_shapes=_exchange_sems(n),
        compiler_params=_params(has_side_effects=True),
    )(*arrays)


def _cast_bf16(arrays, name):
    n = len(arrays)

    def body(*refs):
        for i in range(n):
            refs[n + i][...] = refs[i][...].astype(BF)

    return pl.pallas_call(
        body, name=name,
        out_shape=tuple(jax.ShapeDtypeStruct(a.shape, BF) for a in arrays),
        in_specs=[_whole_vmem() for _ in arrays],
        out_specs=tuple(_whole_vmem() for _ in arrays),
        compiler_params=_params(),
    )(*arrays)


def _load_columns(stacked_ref, full_ref, sems):
    c = stacked_ref.shape[2]
    cps = [pltpu.make_async_copy(stacked_ref.at[d], full_ref.at[:, pl.ds(d * c, c)], sems.at[d]) for d in range(N_DEV)]
    for cp in cps:
        cp.start()
    for cp in cps:
        cp.wait()


def _sig(v):
    return 0.5 * jnp.tanh(0.5 * v) + 0.5


@jax.custom_vjp
def _silu(v):
    return v * _sig(v)


def _silu_fwd(v):
    return _silu(v), v


def _silu_bwd(v, g):
    s = _sig(v)
    return (g * (s * (1.0 + v * (1.0 - s))),)


_silu.defvjp(_silu_fwd, _silu_bwd)


@jax.custom_vjp
def _sigmoid_rel(v):
    e = jnp.exp(-jnp.abs(v))
    d = 1.0 + e
    r = pl.reciprocal(d, approx=True)
    r = r * (2.0 - d * r)
    r = r * (2.0 - d * r)
    return jnp.where(v >= 0.0, r, e * r)


def _sigmoid_rel_fwd(v):
    s = _sigmoid_rel(v)
    return s, s


def _sigmoid_rel_bwd(s, g):
    return (g * (s * (1.0 - s)),)


_sigmoid_rel.defvjp(_sigmoid_rel_fwd, _sigmoid_rel_bwd)


def _gelu(v):
    return 0.5 * v * (1.0 + lax.erf(v * 0.7071067811865476))


def _dot(a, b, ca, cb):
    return lax.dot_general(a, b, (((ca,), (cb,)), ((), ())), preferred_element_type=F32)


@jax.custom_vjp
def _mm(a, b):
    return _dot(a, b, 1, 0)


def _mm_fwd(a, b):
    return _dot(a, b, 1, 0), (a, b)


def _mm_bwd(res, g):
    a, b = res
    gb = g.astype(BF)
    return _dot(gb, b, 1, 1).astype(a.dtype), _dot(a, gb, 0, 0).astype(b.dtype)


_mm.defvjp(_mm_fwd, _mm_bwd)


@jax.custom_vjp
def _nt(a, b):
    return _dot(a, b, 1, 1)


def _nt_fwd(a, b):
    return _dot(a, b, 1, 1), (a, b)


def _nt_bwd(res, g):
    a, b = res
    gb = g.astype(BF)
    return _dot(gb, b, 1, 0).astype(a.dtype), _dot(gb, a, 0, 0).astype(b.dtype)


_nt.defvjp(_nt_fwd, _nt_bwd)


@jax.custom_vjp
def _tn(a, b):
    return _dot(a, b, 0, 0)


def _tn_fwd(a, b):
    return _dot(a, b, 0, 0), (a, b)


def _tn_bwd(res, g):
    a, b = res
    gb = g.astype(BF)
    return _dot(b, gb, 1, 1).astype(a.dtype), _dot(a, gb, 1, 0).astype(b.dtype)


_tn.defvjp(_tn_fwd, _tn_bwd)


def _ada_block(c_all, w_ada, b_cols):
    def body(c_ref, w_ref, b_ref, ada_ref, cact_ref):
        ca = _silu(c_ref[...])
        cact_ref[...] = ca
        ada_ref[...] = _mm(ca.astype(BF), w_ref[...].astype(BF)) + b_ref[...]

    return pl.pallas_call(
        body, name="ada_block",
        out_shape=(jax.ShapeDtypeStruct((N_DEV, w_ada.shape[1]), F32), jax.ShapeDtypeStruct(c_all.shape, F32)),
        in_specs=[_whole_vmem()] * 3, out_specs=(_whole_vmem(), _whole_vmem()),
        compiler_params=_params(),
    )(c_all, w_ada, b_cols)


def _decay_tables():
    t = np.arange(CH)
    tri = (t[None, :] <= t[:, None]).astype(np.float32)
    masks = []
    for h in LEVELS:
        m = (t // (2 * h)) * (2 * h) + h
        upper = t >= m
        same = (t[:, None] // (2 * h)) == (t[None, :] // (2 * h))
        masks.append(same & upper[:, None] & (~upper)[None, :])
    masks.append(np.eye(CH, dtype=bool))
    lv = [np.where((t % (2 * h)) >= h, 1.0, -1.0) for h in LEVELS[:4]]
    m4 = t % 4
    lv += [(m4 == 0) * 1.0, (m4 >= 2) * 1.0, (m4 == 3) * 1.0, (t % 2 == 1) * 1.0]
    lvl = np.broadcast_to(np.stack(lv)[:, :, None], (8, CH, D // 2)).astype(np.float32)
    cid = np.arange(BLK) // CH
    gmask = (cid[:, None] >= cid[None, :]).astype(np.float32)
    return (jnp.asarray(tri, BF), jnp.asarray(np.stack(masks).astype(np.float32)), jnp.asarray(gmask), jnp.asarray(lvl))


def _split2(v):
    v1 = v.astype(BF)
    return v1, (v - v1.astype(F32)).astype(BF)


@jax.custom_vjp
def _cumsum_mm(tri, v):
    p1, p2 = _split2(v)
    return _dot(tri, p1, 1, 0) + _dot(tri, p2, 1, 0)


def _cumsum_mm_fwd(tri, v):
    return _cumsum_mm(tri, v), tri


def _cumsum_mm_bwd(tri, g):
    p1, p2 = _split2(g)
    return jnp.zeros_like(tri), _dot(tri, p1, 0, 0) + _dot(tri, p2, 0, 0)


_cumsum_mm.defvjp(_cumsum_mm_fwd, _cumsum_mm_bwd)


def _make_row_roll(shift):
    @jax.custom_vjp
    def roll(x):
        return pltpu.roll(x, shift % CH, 0)

    def fwd(x):
        return roll(x), None

    def bwd(_, g):
        return (pltpu.roll(g, (-shift) % CH, 0),)

    roll.defvjp(fwd, bwd)
    return roll


_prev_row = _make_row_roll(1)
_next_row = _make_row_roll(-1)


def _mix_tile(proj, state, w_s, b_s_t, ln_w, ln_b, lower, gn_w, consts):
    tri, masks, gmask, lvl = consts
    mt = proj.shape[0]
    u = proj[:, 0:512]
    v = proj[:, 512:1024]
    q = proj[:, 1024:1536]
    fl = proj[:, 1536:2048]
    inp = proj[:, 2048:2560]
    g = proj[:, 2560:3072]

    ug = _gelu(u)
    vg = _gelu(v)
    mu = jnp.mean(vg, axis=-1, keepdims=True)
    vc = vg - mu
    var = jnp.mean(vc * vc, axis=-1, keepdims=True)
    vnb = (vc * lax.rsqrt(var + EPS) * ln_w + ln_b).astype(BF)
    wsm = [(w_s[h] * gmask).astype(BF) for h in range(NH)]
    ya = [[None] * NH for _ in range(mt // BLK)]
    for bi in range(mt // BLK):
        rows = slice(bi * BLK, (bi + 1) * BLK)
        for h in range(NH):
            cols = slice(h * HD, (h + 1) * HD)
            ya[bi][h] = ug[rows, cols] * (_mm(wsm[h], vnb[rows, cols]) + b_s_t[:, h:h + 1])
    ya_full = jnp.concatenate([jnp.concatenate(r, axis=1) for r in ya], axis=0)

    l0 = lower[0:1, :]
    l1 = lower[1:2, :]
    mx = jnp.maximum(l0, l1)
    e0 = jnp.exp(l0 - mx)
    e1 = jnp.exp(l1 - mx)
    lb = e0 / (e0 + e1)
    qf = _silu(q)
    f = lb + (1.0 - lb) * _sigmoid_rel(fl)
    logf = jnp.log(f)
    kk = 1.0 - f
    gate = _silu(g)
    nl = len(LEVELS)
    half = D // 2
    keep = [masks[i] > 0.5 for i in range(nl + 1)]
    st = list(state)
    yb = [[None] * NH for _ in range(mt // CH)]
    for ci in range(mt // CH):
        rows = slice(ci * CH, (ci + 1) * CH)
        lc = logf[rows]
        b = _cumsum_mm(tri, lc)
        xb = jnp.exp(b)
        xinv = jnp.exp(b[CH - 1:CH, :] - b)
        xl = []
        for i, hs in enumerate(LEVELS[:3]):
            refs = [jnp.broadcast_to(b[r:r + 1, :], (2 * hs, half)) for r in range(hs - 1, CH, 2 * hs)]
            bref = refs[0] if len(refs) == 1 else jnp.concatenate(refs, axis=0)
            xl.append(jnp.exp(lvl[i] * (b - bref)))
        b3 = b.reshape(CH // 8, 8, half)
        bref = jnp.broadcast_to(b3[:, 3:4, :], (CH // 8, 8, half)).reshape(CH, half)
        xl.append(jnp.exp(lvl[3] * (b - bref)))
        xl.append(jnp.exp(lvl[4] * _next_row(lc) + lvl[5] * lc + lvl[6] * _prev_row(lc)))
        xl.append(jnp.exp(lvl[7] * lc))
        qc = qf[rows]
        kc = kk[rows]
        zsrc = [jnp.concatenate([(qc if (r0 // hs) % 2 == 1 else kc)[r0:r0 + hs] for r0 in range(0, CH, hs)], axis=0)
                for hs in LEVELS[:3]]
        for h in range(NH):
            cols = slice(h * HD, (h + 1) * HD)
            qh = qc[:, cols]
            kh = kc[:, cols]
            vh = inp[rows, cols].astype(BF)
            inter = _nt((qh * xb[:, cols]).astype(BF), st[h].astype(BF))
            attn = jnp.where(keep[nl], _nt(qh.astype(BF), kh.astype(BF)), 0.0)
            for li in range(nl):
                if li < 3:
                    z = (zsrc[li][:, cols] * xl[li][:, cols]).astype(BF)
                    pairs = _nt(z, z)
                else:
                    xx = xl[li][:, cols]
                    pairs = _nt((qh * xx).astype(BF), (kh * xx).astype(BF))
                attn = jnp.where(keep[li], pairs, attn)
            o = inter + _mm(attn.astype(BF), vh)
            st[h] = st[h] * xb[CH - 1:CH, cols] + _tn(vh, (kh * xinv[:, cols]).astype(BF))
            rs = lax.rsqrt(jnp.mean(o * o, axis=-1, keepdims=True) + EPS)
            yb[ci][h] = o * rs * gn_w * gate[rows, cols]
    yb_full = jnp.concatenate([jnp.concatenate(r, axis=1) for r in yb], axis=0)
    return jnp.concatenate([ya_full, yb_full], axis=1), tuple(st)


def _fwd_mix(proj, w_s, b_s_t, ln_w, ln_b, lower, gn_w, tables, to_gather):
    t = proj.shape[0]
    mt = _row_tile(t, MIX_TILE)
    nt_ = t // mt
    nc = len(tables)
    ng = len(to_gather)

    def body(*refs):
        proj_ref, ws_ref, bs_ref, lw_ref, lb_ref, lo_ref, gn_ref = refs[:7]
        c_refs = refs[7:7 + nc]
        refs = refs[7 + nc:]
        g_in = refs[:ng]
        y_ref, st_ref = refs[ng:2 + ng]
        g_out = refs[2 + ng:2 + 2 * ng]
        state = refs[2 + 2 * ng]
        ga = _Gather2(g_in, g_out, *refs[3 + 2 * ng:])

        @pl.when(pl.program_id(0) == 0)
        def _():
            state[...] = jnp.zeros_like(state)
            ga.start()

        @pl.when(pl.program_id(0) == (3 * nt_) // 4)
        def _():
            ga.forward()

        st = tuple(state[h] for h in range(NH))
        for h in range(NH):
            st_ref[h] = st[h]
        y, new = _mix_tile(proj_ref[...], st, ws_ref[...], bs_ref[...], lw_ref[...], lb_ref[...], lo_ref[...], gn_ref[...],
                           tuple(r[...] for r in c_refs))
        y_ref[...] = y.astype(BF)
        for h in range(NH):
            state[h] = new[h]

        @pl.when(pl.program_id(0) == nt_ - 1)
        def _():
            ga.finish()

    full = lambda a: pl.BlockSpec(a.shape, lambda i, nd=a.ndim: (0,) * nd)
    outs = pl.pallas_call(
        body, name="fwd_mix", grid=(nt_,),
        out_shape=(jax.ShapeDtypeStruct((t, D), BF), jax.ShapeDtypeStruct((nt_, NH, HD, HD), F32),
                   *_exchange_out_shapes(to_gather, False)),
        in_specs=[pl.BlockSpec((mt, D_IN), lambda i: (i, 0)), full(w_s), full(b_s_t), full(ln_w), full(ln_b), full(lower),
                  full(gn_w)] + [full(a) for a in tables] + [_any() for _ in to_gather],
        out_specs=(pl.BlockSpec((mt, D), lambda i: (i, 0)), pl.BlockSpec((None, NH, HD, HD), lambda i: (i, 0, 0, 0)),
                   *[_any() for _ in to_gather]),
        scratch_shapes=[pltpu.VMEM((NH, HD, HD), F32)] + _exchange_sems(ng),
        compiler_params=_params(dimension_semantics=("arbitrary",), has_side_effects=True),
    )(proj, w_s, b_s_t, ln_w, ln_b, lower, gn_w, *tables, *to_gather)
    return outs[0], outs[1], outs[2:]


def _bwd_mix(proj, dycat, states, w_s, b_s_t, ln_w, ln_b, lower, gn_w, tables, to_scatter):
    t = proj.shape[0]
    mt = _row_tile(t, MIX_TILE)
    nt_ = t // mt
    nc = len(tables)
    ns = len(to_scatter)

    def body(*refs):
        proj_ref, dy_ref, st_ref, ws_ref, bs_ref, lw_ref, lb_ref, lo_ref, gn_ref = refs[:9]
        c_refs = refs[9:9 + nc]
        refs = refs[9 + nc:]
        s_in = refs[:ns]
        dproj_ref = refs[ns]
        acc = refs[1 + ns:7 + ns]
        s_out = refs[7 + ns:7 + 2 * ns]
        dstate = refs[7 + 2 * ns]
        ex = _Exchange(s_in, s_out, *refs[8 + 2 * ns:], True)

        @pl.when(pl.program_id(0) == 0)
        def _():
            dstate[...] = jnp.zeros_like(dstate)
            for r in acc:
                r[...] = jnp.zeros_like(r)
            ex.start()

        consts = tuple(r[...] for r in c_refs)

        def f(p, s, ws, bs, lw, lb_, lo, gn):
            return _mix_tile(p, s, ws, bs, lw, lb_, lo, gn, consts)

        st = tuple(st_ref[h] for h in range(NH))
        _, vjp = jax.vjp(f, proj_ref[...], st, ws_ref[...], bs_ref[...], lw_ref[...], lb_ref[...], lo_ref[...], gn_ref[...])
        grads = vjp((dy_ref[...], tuple(dstate[h] for h in range(NH))))
        dproj_ref[...] = grads[0].astype(BF)
        for h in range(NH):
            dstate[h] = grads[1][h]
        for r, gval in zip(acc, grads[2:]):
            r[...] += gval

        @pl.when(pl.program_id(0) == nt_ - 1)
        def _():
            ex.wait()

    full = lambda a: pl.BlockSpec(a.shape, lambda i, nd=a.ndim: (0,) * nd)
    rev = lambda i: (nt_ - 1 - i, 0)
    smalls = (w_s, b_s_t, ln_w, ln_b, lower, gn_w)
    outs = pl.pallas_call(
        body, name="bwd_mix", grid=(nt_,),
        out_shape=(jax.ShapeDtypeStruct((t, D_IN), BF), *[jax.ShapeDtypeStruct(a.shape, F32) for a in smalls],
                   *_exchange_out_shapes(to_scatter, True)),
        in_specs=[pl.BlockSpec((mt, D_IN), rev), pl.BlockSpec((mt, D), rev),
                  pl.BlockSpec((None, NH, HD, HD), lambda i: (nt_ - 1 - i, 0, 0, 0))]
        + [full(a) for a in smalls] + [full(a) for a in tables] + [_any() for _ in to_scatter],
        out_specs=(pl.BlockSpec((mt, D_IN), rev), *[full(a) for a in smalls], *[_any() for _ in to_scatter]),
        scratch_shapes=[pltpu.VMEM((NH, HD, HD), F32)] + _exchange_sems(ns),
        compiler_params=_params(dimension_semantics=("arbitrary",), has_side_effects=True),
    )(proj, dycat, states, w_s, b_s_t, ln_w, ln_b, lower, gn_w, *tables, *to_scatter)
    return outs[0], outs[1:7], outs[7:]


def _row_tile(t, want):
    return want if t % want == 0 else t


def _rms(v):
    rstd = lax.rsqrt(jnp.mean(v * v, axis=-1, keepdims=True) + EPS)
    return v * rstd, rstd


def _rms_bwd(dxhat, xhat, rstd):
    return rstd * (dxhat - xhat * jnp.mean(dxhat * xhat, axis=-1, keepdims=True))


def _colsum(v):
    return jnp.sum(v, axis=0, keepdims=True)


def _fwd_in(x, ada, n1w, w_in_st):
    t = x.shape[0]
    tm = _row_tile(t, 512)

    def body(x_ref, ada_ref, n1_ref, wst_ref, proj_ref, h1_ref, w_ref, sems):
        @pl.when(pl.program_id(0) == 0)
        def _():
            _load_columns(wst_ref, w_ref, sems)

        xh, _ = _rms(x_ref[...])
        h1 = (xh * n1_ref[...]) * (1.0 + ada_ref[1:2, :]) + ada_ref[0:1, :]
        h1b = h1.astype(BF)
        h1_ref[...] = h1b
        proj_ref[...] = _mm(h1b, w_ref[...])

    full = lambda a: pl.BlockSpec(a.shape, lambda i, nd=a.ndim: (0,) * nd)
    return pl.pallas_call(
        body, name="fwd_in", grid=(t // tm,),
        out_shape=(jax.ShapeDtypeStruct((t, D_IN), F32), jax.ShapeDtypeStruct((t, D), BF)),
        in_specs=[pl.BlockSpec((tm, D), lambda i: (i, 0)), full(ada), full(n1w), _any()],
        out_specs=(pl.BlockSpec((tm, D_IN), lambda i: (i, 0)), pl.BlockSpec((tm, D), lambda i: (i, 0))),
        scratch_shapes=[pltpu.VMEM((D, D_IN), BF), pltpu.SemaphoreType.DMA((N_DEV,))],
        compiler_params=_params(dimension_semantics=("arbitrary",)),
    )(x, ada, n1w, w_in_st)


def _fwd_ffn(x, ycat, tgt, ada, n2w, fw, w_out, w_fi, w_fo):
    t = x.shape[0]
    tm = _row_tile(t, 256)

    def body(x_ref, y_ref, t_ref, ada_ref, n2_ref, fw_ref, wo_ref, wi_ref, wf_ref,
             x1_ref, mix_ref, h2_ref, act_ref, gu_ref, dffn_ref, dx2_ref, part_ref):
        @pl.when(pl.program_id(0) == 0)
        def _():
            part_ref[...] = jnp.zeros_like(part_ref)

        g1, sh2, sc2, g2 = ada_ref[2:3, :], ada_ref[3:4, :], ada_ref[4:5, :], ada_ref[5:6, :]
        mix = _mm(y_ref[...], wo_ref[...])
        mix_ref[...] = mix.astype(BF)
        x1 = x_ref[...] + g1 * mix
        x1_ref[...] = x1
        xh2, _ = _rms(x1)
        h2b = ((xh2 * n2_ref[...]) * (1.0 + sc2) + sh2).astype(BF)
        h2_ref[...] = h2b
        ffn = jnp.zeros((tm, D), F32)
        for lo, hi in FF_CHUNKS:
            gate = _nt(h2b, wi_ref[lo:hi, :])
            up = _nt(h2b, wi_ref[D_FF + lo:D_FF + hi, :])
            gu_ref[:, lo:hi] = gate.astype(BF)
            gu_ref[:, D_FF + lo:D_FF + hi] = up.astype(BF)
            a = (_silu(gate) * up).astype(BF)
            act_ref[:, lo:hi] = a
            ffn = ffn + _mm(a, wf_ref[lo:hi, :])
        x2 = x1 + g2 * ffn
        xh3, rstd3 = _rms(x2)
        err = xh3 * fw_ref[...] - t_ref[...]
        dy = err * (1.0 / D)
        dx2 = _rms_bwd(dy * fw_ref[...], xh3, rstd3)
        dx2_ref[...] = dx2
        dffn_ref[...] = (g2 * dx2).astype(BF)
        part_ref[0:1, :] += _colsum(dx2 * ffn)
        part_ref[1:2, :] += _colsum(dy * xh3)
        part_ref[2:3, :] += jnp.zeros((1, D), F32) + (0.5 / D) * jnp.sum(err * err)

    full = lambda a: pl.BlockSpec(a.shape, lambda i, nd=a.ndim: (0,) * nd)
    row = lambda w: pl.BlockSpec((tm, w), lambda i: (i, 0))
    return pl.pallas_call(
        body, name="fwd_ffn", grid=(t // tm,),
        out_shape=(jax.ShapeDtypeStruct((t, D), F32), jax.ShapeDtypeStruct((t, D), BF), jax.ShapeDtypeStruct((t, D), BF),
                   jax.ShapeDtypeStruct((t, D_FF), BF), jax.ShapeDtypeStruct((t, 2 * D_FF), BF),
                   jax.ShapeDtypeStruct((t, D), BF), jax.ShapeDtypeStruct((t, D), F32), jax.ShapeDtypeStruct((8, D), F32)),
        in_specs=[row(D), row(D), row(D), full(ada), full(n2w), full(fw), _whole_vmem(), _whole_vmem(), _whole_vmem()],
        out_specs=(row(D), row(D), row(D), row(D_FF), row(2 * D_FF), row(D), row(D),
                   pl.BlockSpec((8, D), lambda i: (0, 0))),
        compiler_params=_params(dimension_semantics=("arbitrary",)),
    )(x, ycat, tgt, ada, n2w, fw, w_out, w_fi, w_fo)


def _bwd_ffn(x1, gu, dffn, dx2, mix, ada, n2w, w_out, w_fi, w_fo):
    t = x1.shape[0]
    tm = _row_tile(t, 256)

    def body(x1_ref, gu_ref, dffn_ref, dx2_ref, mix_ref, ada_ref, n2_ref, wo_ref, wi_ref, wf_ref,
             dgu_ref, dx1_ref, dmix_ref, dycat_ref, part_ref):
        @pl.when(pl.program_id(0) == 0)
        def _():
            part_ref[...] = jnp.zeros_like(part_ref)

        g1, sc2 = ada_ref[2:3, :], ada_ref[4:5, :]
        dffn = dffn_ref[...]
        dh2 = jnp.zeros((tm, D), F32)
        for lo, hi in FF_CHUNKS:
            gate = gu_ref[:, lo:hi].astype(F32)
            up = gu_ref[:, D_FF + lo:D_FF + hi].astype(F32)
            dact = _nt(dffn, wf_ref[lo:hi, :])
            sg = _sig(gate)
            dgate = (dact * up * (sg * (1.0 + gate * (1.0 - sg)))).astype(BF)
            dup = (dact * (gate * sg)).astype(BF)
            dgu_ref[:, lo:hi] = dgate
            dgu_ref[:, D_FF + lo:D_FF + hi] = dup
            dh2 = dh2 + _mm(dgate, wi_ref[lo:hi, :]) + _mm(dup, wi_ref[D_FF + lo:D_FF + hi, :])
        x1 = x1_ref[...]
        xh2, rstd2 = _rms(x1)
        xn2 = xh2 * n2_ref[...]
        dxn2 = dh2 * (1.0 + sc2)
        dx1 = dx2_ref[...] + _rms_bwd(dxn2 * n2_ref[...], xh2, rstd2)
        dx1_ref[...] = dx1
        dmix = (g1 * dx1).astype(BF)
        dmix_ref[...] = dmix
        dycat_ref[...] = _nt(dmix, wo_ref[...])
        part_ref[0:1, :] += _colsum(dh2)
        part_ref[1:2, :] += _colsum(dh2 * xn2)
        part_ref[2:3, :] += _colsum(dxn2 * xh2)
        part_ref[3:4, :] += _colsum(dx1 * mix_ref[...].astype(F32))

    full = lambda a: pl.BlockSpec(a.shape, lambda i, nd=a.ndim: (0,) * nd)
    row = lambda w: pl.BlockSpec((tm, w), lambda i: (i, 0))
    return pl.pallas_call(
        body, name="bwd_ffn", grid=(t // tm,),
        out_shape=(jax.ShapeDtypeStruct((t, 2 * D_FF), BF), jax.ShapeDtypeStruct((t, D), F32),
                   jax.ShapeDtypeStruct((t, D), BF), jax.ShapeDtypeStruct((t, D), F32), jax.ShapeDtypeStruct((8, D), F32)),
        in_specs=[row(D), row(2 * D_FF), row(D), row(D), row(D), full(ada), full(n2w),
                  _whole_vmem(), _whole_vmem(), _whole_vmem()],
        out_specs=(row(2 * D_FF), row(D), row(D), row(D), pl.BlockSpec((8, D), lambda i: (0, 0))),
        compiler_params=_params(dimension_semantics=("arbitrary",)),
    )(x1, gu, dffn, dx2, mix, ada, n2w, w_out, w_fi, w_fo)


def _bwd_in(x, dproj, dx1, ada, n1w, w_in_st, to_scatter, to_gather):
    t = x.shape[0]
    tm = _row_tile(t, 1024)
    n_t = t // tm
    ns = len(to_scatter)
    ng = len(to_gather)
    nx = ns + ng

    def body(*refs):
        x_ref, dp_ref, dx1_ref, ada_ref, n1_ref, wst_ref = refs[:6]
        x_in = refs[6:6 + nx]
        gx_ref, part_ref = refs[6 + nx:8 + nx]
        x_out = refs[8 + nx:8 + 2 * nx]
        w_ref, sems = refs[8 + 2 * nx:10 + 2 * nx]
        sem_refs = refs[10 + 2 * nx:]
        ex = _Exchange(x_in[:ns], x_out[:ns], *sem_refs[:3], True)
        gx = _Exchange(x_in[ns:], x_out[ns:], *sem_refs[3:], False) if ng else None

        @pl.when(pl.program_id(0) == 0)
        def _():
            ex.start()
            if ng:
                gx.start()
            part_ref[...] = jnp.zeros_like(part_ref)
            _load_columns(wst_ref, w_ref, sems)

        dh1 = _nt(dp_ref[...], w_ref[...])
        xh, rstd = _rms(x_ref[...])
        xn = xh * n1_ref[...]
        dxn = dh1 * (1.0 + ada_ref[1:2, :])
        gx_ref[...] = dx1_ref[...] + _rms_bwd(dxn * n1_ref[...], xh, rstd)
        part_ref[0:1, :] += _colsum(dh1)
        part_ref[1:2, :] += _colsum(dh1 * xn)
        part_ref[2:3, :] += _colsum(dxn * xh)

        @pl.when(pl.program_id(0) == n_t - 1)
        def _():
            ex.wait()
            if ng:
                gx.wait()

    full = lambda a: pl.BlockSpec(a.shape, lambda i, nd=a.ndim: (0,) * nd)
    row = lambda w: pl.BlockSpec((tm, w), lambda i: (i, 0))
    outs = pl.pallas_call(
        body, name="bwd_in", grid=(n_t,),
        out_shape=(jax.ShapeDtypeStruct((t, D), F32), jax.ShapeDtypeStruct((8, D), F32),
                   *_exchange_out_shapes(to_scatter, True), *_exchange_out_shapes(to_gather, False)),
        in_specs=[row(D), row(D_IN), row(D), full(ada), full(n1w), _any()] + [_any() for _ in range(nx)],
        out_specs=(row(D), pl.BlockSpec((8, D), lambda i: (0, 0)), *[_any() for _ in range(nx)]),
        scratch_shapes=[pltpu.VMEM((D, D_IN), BF), pltpu.SemaphoreType.DMA((N_DEV,))] + _exchange_sems(ns)
        + (_exchange_sems(ng) if ng else []),
        compiler_params=_params(dimension_semantics=("arbitrary",), has_side_effects=True),
    )(x, dproj, dx1, ada, n1w, w_in_st, *to_scatter, *to_gather)
    return outs[0], outs[1], outs[2:2 + ns], outs[2 + ns:]


def _wgrad(a, b, name, a_spec, b_spec, out_shape, out_spec, grid, acc_shape, split=1, to_gather=()):
    n_j, n_t = grid
    ng = len(to_gather)

    def body(*refs):
        a_ref, b_ref = refs[:2]
        g_in = refs[2:2 + ng]
        o_ref = refs[2 + ng]
        g_out = refs[3 + ng:3 + 2 * ng]
        acc = refs[3 + 2 * ng]
        first = (pl.program_id(0) == 0) & (pl.program_id(1) == 0)
        last = (pl.program_id(0) == n_j - 1) & (pl.program_id(1) == n_t - 1)
        if ng:
            gx = _Exchange(g_in, g_out, *refs[4 + 2 * ng:], False)

            @pl.when(first)
            def _():
                gx.start()

        @pl.when(pl.program_id(1) == 0)
        def _():
            acc[...] = jnp.zeros_like(acc)

        acc[...] += _tn(a_ref[...], b_ref[...])

        @pl.when(pl.program_id(1) == n_t - 1)
        def _():
            if split == 1:
                o_ref[...] = acc[...].astype(BF)
            else:
                w = acc_shape[1] // split
                for s in range(split):
                    o_ref[s] = acc[:, s * w:(s + 1) * w].astype(BF)

        if ng:
            @pl.when(last)
            def _():
                gx.wait()

    outs = pl.pallas_call(
        body, name=name, grid=grid,
        out_shape=(jax.ShapeDtypeStruct(out_shape, BF), *_exchange_out_shapes(to_gather, False)),
        in_specs=[a_spec, b_spec] + [_any() for _ in to_gather], out_specs=(out_spec, *[_any() for _ in to_gather]),
        scratch_shapes=[pltpu.VMEM(acc_shape, F32)] + (_exchange_sems(ng) if ng else []),
        compiler_params=_params(dimension_semantics=("arbitrary", "arbitrary"), has_side_effects=bool(ng)),
    )(a, b, *to_gather)
    return (outs[0], outs[1:]) if ng else outs[0]


def _adamw_math(w, g, m, v):
    m = ADAM_B1 * m + (1.0 - ADAM_B1) * g
    v = ADAM_B2 * v + (1.0 - ADAM_B2) * (g * g)
    m_hat = m / (1.0 - ADAM_B1 ** ADAM_STEP)
    v_hat = v / (1.0 - ADAM_B2 ** ADAM_STEP)
    delta = -ADAM_LR * (m_hat / (jnp.sqrt(v_hat) + ADAM_EPS) + ADAM_WD * w)
    return delta, m, v


def _adamw_recv(w, m, v, recv, name, tr):
    r, c = w.shape

    def body(w_ref, m_ref, v_ref, r_ref, g_ref, d_ref, nm_ref, nv_ref):
        g = r_ref[0].astype(F32)
        for k in range(1, N_DEV):
            g = g + r_ref[k].astype(F32)
        g_ref[...] = g
        d_ref[...], nm_ref[...], nv_ref[...] = _adamw_math(w_ref[...], g, m_ref[...], v_ref[...])

    row = pl.BlockSpec((tr, c), lambda i: (i, 0))
    return pl.pallas_call(
        body, name=name, grid=(r // tr,),
        out_shape=tuple(jax.ShapeDtypeStruct((r, c), F32) for _ in range(4)),
        in_specs=[row, row, row, pl.BlockSpec((N_DEV, tr, c), lambda i: (0, i, 0))],
        out_specs=(row, row, row, row),
        compiler_params=_params(dimension_semantics=("arbitrary",)),
    )(w, m, v, recv)


def _adamw_ada(w, m, v, cact, dada_cols):
    r, c = w.shape
    tr = 256

    def body(w_ref, m_ref, v_ref, ca_ref, da_ref, g_ref, d_ref, nm_ref, nv_ref):
        g = _tn(ca_ref[...].astype(BF), da_ref[...].astype(BF))
        g_ref[...] = g
        d_ref[...], nm_ref[...], nv_ref[...] = _adamw_math(w_ref[...], g, m_ref[...], v_ref[...])

    row = pl.BlockSpec((tr, c), lambda i: (i, 0))
    return pl.pallas_call(
        body, name="adamw_ada", grid=(r // tr,),
        out_shape=tuple(jax.ShapeDtypeStruct((r, c), F32) for _ in range(4)),
        in_specs=[row, row, row, pl.BlockSpec((N_DEV, tr), lambda i: (0, i)), pl.BlockSpec(dada_cols.shape, lambda i: (0, 0))],
        out_specs=(row, row, row, row),
        compiler_params=_params(dimension_semantics=("arbitrary",)),
    )(w, m, v, cact, dada_cols)


def _adamw_small(gathered, wmv):
    n_g = len(gathered)
    n_p = len(wmv)
    flat = [a for trip in wmv for a in trip]

    def body(*refs):
        g_refs = refs[:n_g]
        p_refs = refs[n_g:n_g + 3 * n_p]
        o_refs = refs[n_g + 3 * n_p:]

        def total(ref):
            s = ref[0]
            for k in range(1, N_DEV):
                s = s + ref[k]
            return s

        f3, b3, b1, dws, dbs, dlnw, dlnb, dlo, dgn = [total(r) for r in g_refs]
        dada_rows = [b1[0:1], b1[1:2], b3[3:4], b3[0:1], b3[1:2], f3[0:1]]
        for r, g in enumerate(dada_rows):
            cs = slice(r * D, (r + 1) * D)
            w, m, v = p_refs[0][:, cs], p_refs[1][:, cs], p_refs[2][:, cs]
            o_refs[0][:, cs] = g
            o_refs[1][:, cs], o_refs[2][:, cs], o_refs[3][:, cs] = _adamw_math(w, g, m, v)
        grads = [None, b1[2:3], dws, dbs, dlnw, dlnb, dlo, dgn, b3[2:3], f3[1:2]]
        for i, g in enumerate(grads):
            if g is None:
                continue
            w, m, v = p_refs[3 * i][...], p_refs[3 * i + 1][...], p_refs[3 * i + 2][...]
            o_refs[4 * i][...] = g
            o_refs[4 * i + 1][...], o_refs[4 * i + 2][...], o_refs[4 * i + 3][...] = _adamw_math(w, g, m, v)
        o_refs[4 * n_p][...] = jnp.zeros((8, 128), F32) + f3[2:3, 0:128]

    out_shape = []
    for w, _, _ in wmv:
        out_shape += [jax.ShapeDtypeStruct(w.shape, F32)] * 4
    out_shape.append(jax.ShapeDtypeStruct((8, 128), F32))
    n_in = n_g + 3 * n_p
    return pl.pallas_call(
        body, name="adamw_small",
        out_shape=tuple(out_shape),
        in_specs=[_whole_vmem()] * n_in, out_specs=tuple(_whole_vmem() for _ in out_shape),
        compiler_params=_params(),
    )(*gathered, *flat)


def kernel(x, c, w_ada, b_ada, norm1_w, w_in, w_s, b_s, v_ln_w, v_ln_b, lower_bounds, gn_w, w_out, norm2_w, w_ffn_in, w_ffn_out, final_norm_w, loss_target, m_w_ada, m_b_ada, m_norm1_w, m_w_in, m_w_s, m_b_s, m_v_ln_w, m_v_ln_b, m_lower_bounds, m_gn_w, m_w_out, m_norm2_w, m_w_ffn_in, m_w_ffn_out, m_final_norm_w, v_w_ada, v_b_ada, v_norm1_w, v_w_in, v_w_s, v_b_s, v_v_ln_w, v_v_ln_b, v_lower_bounds, v_gn_w, v_w_out, v_norm2_w, v_w_ffn_in, v_w_ffn_out, v_final_norm_w):
    me = 4 * lax.axis_index("x") + 2 * lax.axis_index("y") + lax.axis_index("c")
    t = x.shape[1]
    x2d = x.reshape(t, D)
    tgt = loss_target.reshape(t, D)
    ada_cols = w_ada.shape[2]

    tp = lambda a: jnp.swapaxes(a[0], 0, 1)
    win_b, wout_b, wfi_b, wfo_b = _cast_bf16([w_in[0], w_out[0], tp(w_ffn_in), w_ffn_out[0]], "cast_weights")
    win_st, c_st = _exchange([win_b, c], "gather_w_in", False, False, two_level=True)
    c_all = c_st.reshape(N_DEV, D)

    b_cols = lax.dynamic_slice(b_ada, (0, me * ada_cols), (1, ada_cols))
    ada_blk, cact = _ada_block(c_all, w_ada[0], b_cols)
    (ada_st,) = _exchange([ada_blk], "gather_ada", False, True)
    ada = lax.dynamic_index_in_dim(ada_st, me, axis=1, keepdims=False).reshape(6, D)

    tables = _decay_tables()
    ws3 = w_s[0]
    bs_t = b_s[0].T

    proj, h1 = _fwd_in(x2d, ada, norm1_w, win_st)
    ycat, states, (wout_st, wfi_st, wfo_st) = _fwd_mix(proj, ws3, bs_t, v_ln_w, v_ln_b, lower_bounds, gn_w, tables,
                                                       [wout_b, wfi_b, wfo_b])
    w_out_full = wout_st.reshape(D, D)
    w_fo_full = wfo_st.reshape(D_FF, D)
    w_fi_full = wfi_st.reshape(2 * D_FF, D)
    x1, mixb, h2, act, gu, dffn, dx2, part_f = _fwd_ffn(x2d, ycat, tgt, ada, norm2_w, final_norm_w.reshape(1, D),
                                                        w_out_full, w_fi_full, w_fo_full)
    dgu, dx1, dmix, dycat, part_b3 = _bwd_ffn(x1, gu, dffn, dx2, mixb, ada, norm2_w, w_out_full, w_fi_full, w_fo_full)
    tk = _row_tile(t, 2048)
    n_t = t // tk
    win_cols = D_IN // N_DEV
    dwout = _wgrad(ycat, dmix, "wgrad_out",
                   pl.BlockSpec((tk, D), lambda j, i: (i, 0)), pl.BlockSpec((tk, D), lambda j, i: (i, 0)),
                   (D, D), pl.BlockSpec((D, D), lambda j, i: (0, 0)), (1, n_t), (D, D))
    dwfi = _wgrad(dgu, h2, "wgrad_ffn_in",
                  pl.BlockSpec((tk, FF_PAIR), lambda j, i: (i, j)), pl.BlockSpec((tk, D), lambda j, i: (i, 0)),
                  (4, FF_PAIR, D), pl.BlockSpec((None, FF_PAIR, D), lambda j, i: (j, 0, 0)), (4, n_t), (FF_PAIR, D))
    dwfo = _wgrad(act, dffn, "wgrad_ffn_out",
                  pl.BlockSpec((tk, FF_PAIR), lambda j, i: (i, j)), pl.BlockSpec((tk, D), lambda j, i: (i, 0)),
                  (2, FF_PAIR, D), pl.BlockSpec((None, FF_PAIR, D), lambda j, i: (j, 0, 0)), (2, n_t), (FF_PAIR, D))
    dproj, (dws, dbs_t, dlnw, dlnb, dlower, dgnw), (r_out, r_fi, r_fo) = _bwd_mix(
        proj, dycat, states, ws3, bs_t, v_ln_w, v_ln_b, lower_bounds, gn_w, tables,
        [dwout.reshape(N_DEV, D // N_DEV, D), dwfi.reshape(N_DEV, FF_BLK, D), dwfo.reshape(N_DEV, D_FF // N_DEV, D)])
    dwin, early = _wgrad(h1, dproj, "wgrad_in",
                         pl.BlockSpec((tk, D), lambda j, i: (i, 0)), pl.BlockSpec((tk, 4 * win_cols), lambda j, i: (i, j)),
                         (N_DEV, D, win_cols), pl.BlockSpec((4, D, win_cols), lambda j, i: (j, 0, 0)), (N_DEV // 4, n_t),
                         (D, 4 * win_cols), split=4, to_gather=[part_f, part_b3, dws, dbs_t, dlnw, dlnb, dlower, dgnw])
    grad_x, part_b1, (r_in,), _ = _bwd_in(x2d, dproj, dx1, ada, norm1_w, win_st, [dwin], [])

    g_w_in, d_w_in, nm_w_in, nv_w_in = _adamw_recv(w_in[0], m_w_in[0], v_w_in[0], r_in, "adamw_w_in", 256)
    g_w_out, d_w_out, nm_w_out, nv_w_out = _adamw_recv(w_out[0], m_w_out[0], v_w_out[0], r_out, "adamw_w_out", 128)
    fi_t = _adamw_recv(tp(w_ffn_in), tp(m_w_ffn_in), tp(v_w_ffn_in), r_fi, "adamw_w_ffn_in", 176)
    g_w_fi, d_w_fi, nm_w_fi, nv_w_fi = [jnp.swapaxes(a, 0, 1) for a in fi_t]
    g_w_fo, d_w_fo, nm_w_fo, nv_w_fo = _adamw_recv(w_ffn_out[0], m_w_ffn_out[0], v_w_ffn_out[0], r_fo, "adamw_w_ffn_out", 176)

    (b1_all,) = _exchange([part_b1], "gather_small", False, True)
    gathered = [early[0], early[1], b1_all, *early[2:]]
    f3_all, b3_all = gathered[0], gathered[1]
    dada_all = jnp.stack([b1_all[:, 0], b1_all[:, 1], b3_all[:, 3], b3_all[:, 0], b3_all[:, 1], f3_all[:, 0]], axis=1)
    dada_cols = lax.dynamic_slice(dada_all.reshape(N_DEV, 6 * D), (0, me * ada_cols), (N_DEV, ada_cols))
    g_w_ada, d_w_ada, nm_w_ada, nv_w_ada = _adamw_ada(w_ada[0], m_w_ada[0], v_w_ada[0], cact, dada_cols)

    r1 = lambda a: a.reshape(1, D)
    tr = lambda a: a[0].T
    wmv = [
        (b_ada, m_b_ada, v_b_ada),
        (norm1_w, m_norm1_w, v_norm1_w),
        (w_s[0], m_w_s[0], v_w_s[0]),
        (tr(b_s), tr(m_b_s), tr(v_b_s)),
        (v_ln_w, m_v_ln_w, v_v_ln_w),
        (v_ln_b, m_v_ln_b, v_v_ln_b),
        (lower_bounds, m_lower_bounds, v_lower_bounds),
        (gn_w, m_gn_w, v_gn_w),
        (norm2_w, m_norm2_w, v_norm2_w),
        (r1(final_norm_w), r1(m_final_norm_w), r1(v_final_norm_w)),
    ]
    small = _adamw_small(gathered, wmv)
    loss = small[-1][0, 0]

    def unshape(i, a):
        if i == 2:
            return a.reshape(1, NH, BLK, BLK)
        if i == 3:
            return a.T.reshape(1, NH, BLK)
        if i == 9:
            return a.reshape(D)
        return a

    def small_out(kind):
        return [unshape(i, small[4 * i + kind]) for i in range(len(wmv))]

    e3 = lambda a: a[None]
    big = {
        0: (e3(g_w_ada), e3(g_w_in), e3(g_w_out), e3(g_w_fi), e3(g_w_fo)),
        1: (e3(d_w_ada), e3(d_w_in), e3(d_w_out), e3(d_w_fi), e3(d_w_fo)),
        2: (e3(nm_w_ada), e3(nm_w_in), e3(nm_w_out), e3(nm_w_fi), e3(nm_w_fo)),
        3: (e3(nv_w_ada), e3(nv_w_in), e3(nv_w_out), e3(nv_w_fi), e3(nv_w_fo)),
    }

    def ordered(kind):
        s = small_out(kind)
        b_ = big[kind]
        return [b_[0], s[0], s[1], b_[1], s[2], s[3], s[4], s[5], s[6], s[7], b_[2], s[8], b_[3], b_[4], s[9]]

    return (loss, grad_x.reshape(1, t, D), *ordered(0), *ordered(1), *ordered(2), *ordered(3))
```

```python
import functools

import numpy as np
import jax
import jax.numpy as jnp
from jax import lax
from jax.experimental import pallas as pl
from jax.experimental.pallas import tpu as pltpu

F32 = jnp.float32
BF = jnp.bfloat16
MESH = pl.DeviceIdType.MESH

N_DEV = 8
D = 1024
D_IN = 3072
D_FF = 2816
FF_BLK = D_FF // 4
FF_CHUNKS = ((0, 1536), (1536, D_FF))
FF_PAIR = 2 * FF_BLK
CH = 64
BLK = 128
MIX_TILE = 512
NH = 4
HD = 128
EPS = 1e-6
LEVELS = (32, 16, 8, 4, 2, 1)

ADAM_LR = 0.001
ADAM_B1 = 0.9
ADAM_B2 = 0.999
ADAM_EPS = 1e-08
ADAM_WD = 0.01
ADAM_STEP = 10

VMEM_LIMIT = 56 * 1024 * 1024


def _params(**kw):
    return pltpu.CompilerParams(vmem_limit_bytes=VMEM_LIMIT, **kw)


def _whole_vmem():
    return pl.BlockSpec(memory_space=pltpu.VMEM)


def _any():
    return pl.BlockSpec(memory_space=pl.ANY)


class _Exchange:
    def __init__(self, ins, outs, send_sems, recv_sems, local_sems, scatter):
        self.ins, self.outs, self.scatter = ins, outs, scatter
        self.send_sems, self.recv_sems, self.local_sems = send_sems, recv_sems, local_sems
        x, y, c = lax.axis_index("x"), lax.axis_index("y"), lax.axis_index("c")
        self.me = 4 * x + 2 * y + c
        self.peers = []
        for k in range(1, N_DEV):
            peer = (1 - x if (k >> 2) & 1 else x, 1 - y if (k >> 1) & 1 else y, 1 - c if k & 1 else c)
            self.peers.append((peer, 4 * peer[0] + 2 * peer[1] + peer[2]))

    def _src(self, a, idx):
        return self.ins[a].at[idx] if self.scatter else self.ins[a]

    def _local(self):
        return [pltpu.make_async_copy(self._src(a, self.me), self.outs[a].at[self.me], self.local_sems.at[a])
                for a in range(len(self.ins))]

    def _remote(self, a, k, dst_slot):
        peer, peer_idx = self.peers[k]
        return pltpu.make_async_remote_copy(
            src_ref=self._src(a, peer_idx), dst_ref=self.outs[a].at[dst_slot],
            send_sem=self.send_sems.at[a, k], recv_sem=self.recv_sems.at[a, k],
            device_id=peer, device_id_type=MESH)

    def start(self):
        for cp in self._local():
            cp.start()
        for k in range(N_DEV - 1):
            for a in range(len(self.ins)):
                self._remote(a, k, self.me).start()

    def wait(self):
        for k in range(N_DEV - 1):
            for a in range(len(self.ins)):
                self._remote(a, k, self.peers[k][1]).wait()
        for cp in self._local():
            cp.wait()


class _Gather2:
    def __init__(self, ins, outs, send_sems, recv_sems, local_sems):
        self.ins, self.outs = ins, outs
        self.send_sems, self.recv_sems, self.local_sems = send_sems, recv_sems, local_sems
        x, y, c = lax.axis_index("x"), lax.axis_index("y"), lax.axis_index("c")
        self.c = c
        self.me = 4 * x + 2 * y + c
        self.sibling = (x, y, 1 - c)
        self.chips = [(1 - x, y), (x, 1 - y), (1 - x, 1 - y)]

    @staticmethod
    def _idx(px, py, pc):
        return 4 * px + 2 * py + pc

    def _copy(self, a, k, slot, to, own):
        src = self.ins[a] if own else self.outs[a].at[slot]
        return pltpu.make_async_remote_copy(
            src_ref=src, dst_ref=self.outs[a].at[slot],
            send_sem=self.send_sems.at[a, k], recv_sem=self.recv_sems.at[a, k],
            device_id=to, device_id_type=MESH)

    def _local(self):
        return [pltpu.make_async_copy(self.ins[a], self.outs[a].at[self.me], self.local_sems.at[a])
                for a in range(len(self.ins))]

    def start(self):
        for cp in self._local():
            cp.start()
        for a in range(len(self.ins)):
            self._copy(a, 0, self.me, self.sibling, True).start()
            for j, chip in enumerate(self.chips):
                self._copy(a, 1 + j, self.me, (*chip, self.c), True).start()

    def forward(self):
        for j, chip in enumerate(self.chips):
            for a in range(len(self.ins)):
                slot = self._idx(*chip, self.c)
                self._copy(a, 1 + j, slot, (*chip, self.c), True).wait_recv()
                self._copy(a, 4 + j, slot, self.sibling, False).start()

    def finish(self):
        for a in range(len(self.ins)):
            self._copy(a, 0, self._idx(*self.sibling), self.sibling, True).wait_recv()
            for j, chip in enumerate(self.chips):
                self._copy(a, 4 + j, self._idx(*chip, 1 - self.c), self.sibling, False).wait_recv()
            self._copy(a, 0, self.me, self.sibling, True).wait_send()
            for j, chip in enumerate(self.chips):
                self._copy(a, 1 + j, self.me, (*chip, self.c), True).wait_send()
                self._copy(a, 4 + j, self._idx(*chip, self.c), self.sibling, False).wait_send()
        for cp in self._local():
            cp.wait()


def _exchange_sems(n):
    return [pltpu.SemaphoreType.DMA((n, N_DEV - 1)), pltpu.SemaphoreType.DMA((n, N_DEV - 1)), pltpu.SemaphoreType.DMA((n,))]


def _exchange_out_shapes(arrays, scatter):
    return [jax.ShapeDtypeStruct(a.shape if scatter else (N_DEV,) + a.shape, a.dtype) for a in arrays]


def _exchange(arrays, name, scatter, in_vmem, two_level=False):
    n = len(arrays)

    def body(*refs):
        if two_level:
            ga = _Gather2(refs[:n], refs[n:2 * n], *refs[2 * n:])
            ga.start()
            ga.forward()
            ga.finish()
        else:
            ex = _Exchange(refs[:n], refs[n:2 * n], *refs[2 * n:], scatter)
            ex.start()
            ex.wait()

    spec = _whole_vmem if in_vmem else _any
    return pl.pallas_call(
        body, name=name,
        out_shape=tuple(_exchange_out_shapes(arrays, scatter)),
        in_specs=[spec() for _ in arrays],
        out_specs=tuple(spec() for _ in arrays),
        scratch_shapes=_exchange_sems(n),
        compiler_params=_params(has_side_effects=True),
    )(*arrays)


def _cast_bf16(arrays, name):
    n = len(arrays)

    def body(*refs):
        for i in range(n):
            refs[n + i][...] = refs[i][...].astype(BF)

    return pl.pallas_call(
        body, name=name,
        out_shape=tuple(jax.ShapeDtypeStruct(a.shape, BF) for a in arrays),
        in_specs=[_whole_vmem() for _ in arrays],
        out_specs=tuple(_whole_vmem() for _ in arrays),
        compiler_params=_params(),
    )(*arrays)


def _load_columns(stacked_ref, full_ref, sems):
    c = stacked_ref.shape[2]
    cps = [pltpu.make_async_copy(stacked_ref.at[d], full_ref.at[:, pl.ds(d * c, c)], sems.at[d]) for d in range(N_DEV)]
    for cp in cps:
        cp.start()
    for cp in cps:
        cp.wait()


def _sig(v):
    return 0.5 * jnp.tanh(0.5 * v) + 0.5


@jax.custom_vjp
def _silu(v):
    return v * _sig(v)


def _silu_fwd(v):
    return _silu(v), v


def _silu_bwd(v, g):
    s = _sig(v)
    return (g * (s * (1.0 + v * (1.0 - s))),)


_silu.defvjp(_silu_fwd, _silu_bwd)


@jax.custom_vjp
def _sigmoid_rel(v):
    e = jnp.exp(-jnp.abs(v))
    d = 1.0 + e
    r = pl.reciprocal(d, approx=True)
    r = r * (2.0 - d * r)
    r = r * (2.0 - d * r)
    return jnp.where(v >= 0.0, r, e * r)


def _sigmoid_rel_fwd(v):
    s = _sigmoid_rel(v)
    return s, s


def _sigmoid_rel_bwd(s, g):
    return (g * (s * (1.0 - s)),)


_sigmoid_rel.defvjp(_sigmoid_rel_fwd, _sigmoid_rel_bwd)


def _gelu(v):
    return 0.5 * v * (1.0 + lax.erf(v * 0.7071067811865476))


def _dot(a, b, ca, cb):
    return lax.dot_general(a, b, (((ca,), (cb,)), ((), ())), preferred_element_type=F32)


@jax.custom_vjp
def _mm(a, b):
    return _dot(a, b, 1, 0)


def _mm_fwd(a, b):
    return _dot(a, b, 1, 0), (a, b)


def _mm_bwd(res, g):
    a, b = res
    gb = g.astype(BF)
    return _dot(gb, b, 1, 1).astype(a.dtype), _dot(a, gb, 0, 0).astype(b.dtype)


_mm.defvjp(_mm_fwd, _mm_bwd)


@jax.custom_vjp
def _nt(a, b):
    return _dot(a, b, 1, 1)


def _nt_fwd(a, b):
    return _dot(a, b, 1, 1), (a, b)


def _nt_bwd(res, g):
    a, b = res
    gb = g.astype(BF)
    return _dot(gb, b, 1, 0).astype(a.dtype), _dot(gb, a, 0, 0).astype(b.dtype)


_nt.defvjp(_nt_fwd, _nt_bwd)


@jax.custom_vjp
def _tn(a, b):
    return _dot(a, b, 0, 0)


def _tn_fwd(a, b):
    return _dot(a, b, 0, 0), (a, b)


def _tn_bwd(res, g):
    a, b = res
    gb = g.astype(BF)
    return _dot(b, gb, 1, 1).astype(a.dtype), _dot(a, gb, 1, 0).astype(b.dtype)


_tn.defvjp(_tn_fwd, _tn_bwd)


def _make_cast_dot(ca, cb, da_dims, db_dims):
    @jax.custom_vjp
    def dot(a, b):
        return _dot(a.astype(BF), b.astype(BF), ca, cb)

    def fwd(a, b):
        ab, bb = a.astype(BF), b.astype(BF)
        return _dot(ab, bb, ca, cb), (ab, bb)

    def bwd(res, g):
        ops = {"a": res[0], "b": res[1], "g": g.astype(BF)}
        return (_dot(ops[da_dims[0]], ops[da_dims[1]], da_dims[2], da_dims[3]),
                _dot(ops[db_dims[0]], ops[db_dims[1]], db_dims[2], db_dims[3]))

    dot.defvjp(fwd, bwd)
    return dot


_mm_c = _make_cast_dot(1, 0, ("g", "b", 1, 1), ("a", "g", 0, 0))
_nt_c = _make_cast_dot(1, 1, ("g", "b", 1, 0), ("g", "a", 0, 0))
_tn_c = _make_cast_dot(0, 0, ("b", "g", 1, 1), ("a", "g", 1, 0))


def _startup(win_b, c, w_ada, b_cols):
    ncol = w_ada.shape[1]

    def body(win_ref, c_ref, w_ref, b_ref, winst_ref, adast_ref, cact_ref, call_ref, blk_ref,
             ws, wr, wl, cs, cr, cl, as_, ar, al):
        big = _Gather2([win_ref], [winst_ref], ws, wr, wl)
        big.start()
        gc = _Exchange([c_ref], [call_ref], cs, cr, cl, False)
        gc.start()
        gc.wait()
        ca = _silu(call_ref[...].reshape(N_DEV, D))
        cact_ref[...] = ca
        blk_ref[...] = _mm(ca.astype(BF), w_ref[...].astype(BF)) + b_ref[...]
        ga = _Exchange([blk_ref], [adast_ref], as_, ar, al, False)
        ga.start()
        ga.wait()
        big.forward()
        big.finish()

    return pl.pallas_call(
        body, name="startup",
        out_shape=(jax.ShapeDtypeStruct((N_DEV,) + win_b.shape, BF), jax.ShapeDtypeStruct((N_DEV, N_DEV, ncol), F32),
                   jax.ShapeDtypeStruct((N_DEV, D), F32)),
        in_specs=[_any(), _whole_vmem(), _whole_vmem(), _whole_vmem()],
        out_specs=(_any(), _whole_vmem(), _whole_vmem()),
        scratch_shapes=[pltpu.VMEM((N_DEV, 1, D), F32), pltpu.VMEM((N_DEV, ncol), F32)]
        + _exchange_sems(1) + _exchange_sems(1) + _exchange_sems(1),
        compiler_params=_params(has_side_effects=True),
    )(win_b, c, w_ada, b_cols)


def _decay_tables():
    t = np.arange(CH)
    tri = (t[None, :] <= t[:, None]).astype(np.float32)
    masks = []
    for h in LEVELS:
        m = (t // (2 * h)) * (2 * h) + h
        upper = t >= m
        same = (t[:, None] // (2 * h)) == (t[None, :] // (2 * h))
        masks.append(same & upper[:, None] & (~upper)[None, :])
    masks.append(np.eye(CH, dtype=bool))
    lv = [np.where((t % (2 * h)) >= h, 1.0, -1.0) for h in LEVELS[:4]]
    m4 = t % 4
    lv += [(m4 == 0) * 1.0, (m4 >= 2) * 1.0, (m4 == 3) * 1.0, (t % 2 == 1) * 1.0]
    lvl = np.broadcast_to(np.stack(lv)[:, :, None], (8, CH, D // 2)).astype(np.float32)
    cid = np.arange(BLK) // CH
    gmask = (cid[:, None] >= cid[None, :]).astype(np.float32)
    return (jnp.asarray(tri, BF), jnp.asarray(np.stack(masks).astype(np.float32)), jnp.asarray(gmask), jnp.asarray(lvl))


def _split2(v):
    v1 = v.astype(BF)
    return v1, (v - v1.astype(F32)).astype(BF)


@jax.custom_vjp
def _cumsum_mm(tri, v):
    p1, p2 = _split2(v)
    return _dot(tri, p1, 1, 0) + _dot(tri, p2, 1, 0)


def _cumsum_mm_fwd(tri, v):
    return _cumsum_mm(tri, v), tri


def _cumsum_mm_bwd(tri, g):
    p1, p2 = _split2(g)
    return jnp.zeros_like(tri), _dot(tri, p1, 0, 0) + _dot(tri, p2, 0, 0)


_cumsum_mm.defvjp(_cumsum_mm_fwd, _cumsum_mm_bwd)


def _make_row_roll(shift):
    @jax.custom_vjp
    def roll(x):
        return pltpu.roll(x, shift % CH, 0)

    def fwd(x):
        return roll(x), None

    def bwd(_, g):
        return (pltpu.roll(g, (-shift) % CH, 0),)

    roll.defvjp(fwd, bwd)
    return roll


_prev_row = _make_row_roll(1)
_next_row = _make_row_roll(-1)


def _mix_tile(proj, state, w_s, b_s_t, ln_w, ln_b, lower, gn_w, consts):
    tri, masks, gmask, lvl = consts
    mt = proj.shape[0]
    u = proj[:, 0:512]
    v = proj[:, 512:1024]
    q = proj[:, 1024:1536]
    fl = proj[:, 1536:2048]
    inp = proj[:, 2048:2560]
    g = proj[:, 2560:3072]

    ug = _gelu(u)
    vg = _gelu(v)
    mu = jnp.mean(vg, axis=-1, keepdims=True)
    vc = vg - mu
    var = jnp.mean(vc * vc, axis=-1, keepdims=True)
    vn = vc * lax.rsqrt(var + EPS) * ln_w + ln_b
    wsm = [w_s[h] * gmask for h in range(NH)]
    ya = [[None] * NH for _ in range(mt // BLK)]
    for bi in range(mt // BLK):
        rows = slice(bi * BLK, (bi + 1) * BLK)
        for h in range(NH):
            cols = slice(h * HD, (h + 1) * HD)
            ya[bi][h] = ug[rows, cols] * (_mm_c(wsm[h], vn[rows, cols]) + b_s_t[:, h:h + 1])
    ya_full = jnp.concatenate([jnp.concatenate(r, axis=1) for r in ya], axis=0)

    l0 = lower[0:1, :]
    l1 = lower[1:2, :]
    mx = jnp.maximum(l0, l1)
    e0 = jnp.exp(l0 - mx)
    e1 = jnp.exp(l1 - mx)
    lb = e0 / (e0 + e1)
    qf = _silu(q)
    f = lb + (1.0 - lb) * _sigmoid_rel(fl)
    logf = jnp.log(f)
    kk = 1.0 - f
    gate = _silu(g)
    nl = len(LEVELS)
    half = D // 2
    st = list(state)
    yb = [[None] * NH for _ in range(mt // CH)]
    for ci in range(mt // CH):
        rows = slice(ci * CH, (ci + 1) * CH)
        lc = logf[rows]
        b = _cumsum_mm(tri, lc)
        xb = jnp.exp(b)
        xinv = jnp.exp(b[CH - 1:CH, :] - b)
        xl = []
        for i, hs in enumerate(LEVELS[:3]):
            refs = [jnp.broadcast_to(b[r:r + 1, :], (2 * hs, half)) for r in range(hs - 1, CH, 2 * hs)]
            bref = refs[0] if len(refs) == 1 else jnp.concatenate(refs, axis=0)
            xl.append(jnp.exp(lvl[i] * (b - bref)))
        b3 = b.reshape(CH // 8, 8, half)
        bref = jnp.broadcast_to(b3[:, 3:4, :], (CH // 8, 8, half)).reshape(CH, half)
        xl.append(jnp.exp(lvl[3] * (b - bref)))
        xl.append(jnp.exp(lvl[4] * _next_row(lc) + lvl[5] * lc + lvl[6] * _prev_row(lc)))
        xl.append(jnp.exp(lvl[7] * lc))
        qc = qf[rows]
        kc = kk[rows]
        zsrc = [jnp.concatenate([(qc if (r0 // hs) % 2 == 1 else kc)[r0:r0 + hs] for r0 in range(0, CH, hs)], axis=0)
                for hs in LEVELS[:3]]
        for h in range(NH):
            cols = slice(h * HD, (h + 1) * HD)
            qh = qc[:, cols]
            kh = kc[:, cols]
            vh = inp[rows, cols]
            inter = _nt_c(qh * xb[:, cols], st[h])
            attn = masks[nl] * _nt_c(qh, kh)
            for li in range(nl):
                if li < 3:
                    z = zsrc[li][:, cols] * xl[li][:, cols]
                    pairs = _nt_c(z, z)
                else:
                    xx = xl[li][:, cols]
                    pairs = _nt_c(qh * xx, kh * xx)
                attn = attn + masks[li] * pairs
            o = inter + _mm_c(attn, vh)
            st[h] = st[h] * xb[CH - 1:CH, cols] + _tn_c(vh, kh * xinv[:, cols])
            rs = lax.rsqrt(jnp.mean(o * o, axis=-1, keepdims=True) + EPS)
            yb[ci][h] = o * rs * gn_w * gate[rows, cols]
    yb_full = jnp.concatenate([jnp.concatenate(r, axis=1) for r in yb], axis=0)
    return jnp.concatenate([ya_full, yb_full], axis=1), tuple(st)


def _fwd_mix(proj, w_s, b_s_t, ln_w, ln_b, lower, gn_w, tables, to_gather):
    t = proj.shape[0]
    mt = _row_tile(t, MIX_TILE)
    nt_ = t // mt
    nc = len(tables)
    ng = len(to_gather)

    def body(*refs):
        proj_ref, ws_ref, bs_ref, lw_ref, lb_ref, lo_ref, gn_ref = refs[:7]
        c_refs = refs[7:7 + nc]
        refs = refs[7 + nc:]
        g_in = refs[:ng]
        y_ref, st_ref = refs[ng:2 + ng]
        g_out = refs[2 + ng:2 + 2 * ng]
        state = refs[2 + 2 * ng]
        ga = _Gather2(g_in, g_out, *refs[3 + 2 * ng:])

        @pl.when(pl.program_id(0) == 0)
        def _():
            state[...] = jnp.zeros_like(state)
            ga.start()

        @pl.when(pl.program_id(0) == (3 * nt_) // 4)
        def _():
            ga.forward()

        st = tuple(state[h] for h in range(NH))
        for h in range(NH):
            st_ref[h] = st[h]
        y, new = _mix_tile(proj_ref[...], st, ws_ref[...], bs_ref[...], lw_ref[...], lb_ref[...], lo_ref[...], gn_ref[...],
                           tuple(r[...] for r in c_refs))
        y_ref[...] = y.astype(BF)
        for h in range(NH):
            state[h] = new[h]

        @pl.when(pl.program_id(0) == nt_ - 1)
        def _():
            ga.finish()

    full = lambda a: pl.BlockSpec(a.shape, lambda i, nd=a.ndim: (0,) * nd)
    outs = pl.pallas_call(
        body, name="fwd_mix", grid=(nt_,),
        out_shape=(jax.ShapeDtypeStruct((t, D), BF), jax.ShapeDtypeStruct((nt_, NH, HD, HD), F32),
                   *_exchange_out_shapes(to_gather, False)),
        in_specs=[pl.BlockSpec((mt, D_IN), lambda i: (i, 0)), full(w_s), full(b_s_t), full(ln_w), full(ln_b), full(lower),
                  full(gn_w)] + [full(a) for a in tables] + [_any() for _ in to_gather],
        out_specs=(pl.BlockSpec((mt, D), lambda i: (i, 0)), pl.BlockSpec((None, NH, HD, HD), lambda i: (i, 0, 0, 0)),
                   *[_any() for _ in to_gather]),
        scratch_shapes=[pltpu.VMEM((NH, HD, HD), F32)] + _exchange_sems(ng),
        compiler_params=_params(dimension_semantics=("arbitrary",), has_side_effects=True),
    )(proj, w_s, b_s_t, ln_w, ln_b, lower, gn_w, *tables, *to_gather)
    return outs[0], outs[1], outs[2:]


def _bwd_mix(proj, dycat, states, w_s, b_s_t, ln_w, ln_b, lower, gn_w, tables, to_scatter):
    t = proj.shape[0]
    mt = _row_tile(t, MIX_TILE)
    nt_ = t // mt
    nc = len(tables)
    ns = len(to_scatter)

    def body(*refs):
        proj_ref, dy_ref, st_ref, ws_ref, bs_ref, lw_ref, lb_ref, lo_ref, gn_ref = refs[:9]
        c_refs = refs[9:9 + nc]
        refs = refs[9 + nc:]
        s_in = refs[:ns]
        dproj_ref = refs[ns]
        acc = refs[1 + ns:7 + ns]
        s_out = refs[7 + ns:7 + 2 * ns]
        dstate = refs[7 + 2 * ns]
        ex = _Exchange(s_in, s_out, *refs[8 + 2 * ns:], True)

        @pl.when(pl.program_id(0) == 0)
        def _():
            dstate[...] = jnp.zeros_like(dstate)
            for r in acc:
                r[...] = jnp.zeros_like(r)
            ex.start()

        consts = tuple(r[...] for r in c_refs)

        def f(p, s, ws, bs, lw, lb_, lo, gn):
            return _mix_tile(p, s, ws, bs, lw, lb_, lo, gn, consts)

        st = tuple(st_ref[h] for h in range(NH))
        _, vjp = jax.vjp(f, proj_ref[...], st, ws_ref[...], bs_ref[...], lw_ref[...], lb_ref[...], lo_ref[...], gn_ref[...])
        grads = vjp((dy_ref[...], tuple(dstate[h] for h in range(NH))))
        dproj_ref[...] = grads[0].astype(BF)
        for h in range(NH):
            dstate[h] = grads[1][h]
        for r, gval in zip(acc, grads[2:]):
            r[...] += gval

        @pl.when(pl.program_id(0) == nt_ - 1)
        def _():
            ex.wait()

    full = lambda a: pl.BlockSpec(a.shape, lambda i, nd=a.ndim: (0,) * nd)
    rev = lambda i: (nt_ - 1 - i, 0)
    smalls = (w_s, b_s_t, ln_w, ln_b, lower, gn_w)
    outs = pl.pallas_call(
        body, name="bwd_mix", grid=(nt_,),
        out_shape=(jax.ShapeDtypeStruct((t, D_IN), BF), *[jax.ShapeDtypeStruct(a.shape, F32) for a in smalls],
                   *_exchange_out_shapes(to_scatter, True)),
        in_specs=[pl.BlockSpec((mt, D_IN), rev), pl.BlockSpec((mt, D), rev),
                  pl.BlockSpec((None, NH, HD, HD), lambda i: (nt_ - 1 - i, 0, 0, 0))]
        + [full(a) for a in smalls] + [full(a) for a in tables] + [_any() for _ in to_scatter],
        out_specs=(pl.BlockSpec((mt, D_IN), rev), *[full(a) for a in smalls], *[_any() for _ in to_scatter]),
        scratch_shapes=[pltpu.VMEM((NH, HD, HD), F32)] + _exchange_sems(ns),
        compiler_params=_params(dimension_semantics=("arbitrary",), has_side_effects=True),
    )(proj, dycat, states, w_s, b_s_t, ln_w, ln_b, lower, gn_w, *tables, *to_scatter)
    return outs[0], outs[1:7], outs[7:]


def _row_tile(t, want):
    return want if t % want == 0 else t


def _rms(v):
    rstd = lax.rsqrt(jnp.mean(v * v, axis=-1, keepdims=True) + EPS)
    return v * rstd, rstd


def _rms_bwd(dxhat, xhat, rstd):
    return rstd * (dxhat - xhat * jnp.mean(dxhat * xhat, axis=-1, keepdims=True))


def _colsum(v):
    return jnp.sum(v, axis=0, keepdims=True)


def _fwd_in(x, ada, n1w, w_in_st):
    t = x.shape[0]
    tm = _row_tile(t, 512)

    def body(x_ref, ada_ref, n1_ref, wst_ref, proj_ref, h1_ref, w_ref, sems):
        @pl.when(pl.program_id(0) == 0)
        def _():
            _load_columns(wst_ref, w_ref, sems)

        xh, _ = _rms(x_ref[...])
        h1 = (xh * n1_ref[...]) * (1.0 + ada_ref[1:2, :]) + ada_ref[0:1, :]
        h1b = h1.astype(BF)
        h1_ref[...] = h1b
        proj_ref[...] = _mm(h1b, w_ref[...])

    full = lambda a: pl.BlockSpec(a.shape, lambda i, nd=a.ndim: (0,) * nd)
    return pl.pallas_call(
        body, name="fwd_in", grid=(t // tm,),
        out_shape=(jax.ShapeDtypeStruct((t, D_IN), F32), jax.ShapeDtypeStruct((t, D), BF)),
        in_specs=[pl.BlockSpec((tm, D), lambda i: (i, 0)), full(ada), full(n1w), _any()],
        out_specs=(pl.BlockSpec((tm, D_IN), lambda i: (i, 0)), pl.BlockSpec((tm, D), lambda i: (i, 0))),
        scratch_shapes=[pltpu.VMEM((D, D_IN), BF), pltpu.SemaphoreType.DMA((N_DEV,))],
        compiler_params=_params(dimension_semantics=("arbitrary",)),
    )(x, ada, n1w, w_in_st)


def _fwd_ffn(x, ycat, tgt, ada, n2w, fw, w_out, w_fi, w_fo):
    t = x.shape[0]
    tm = _row_tile(t, 256)

    def body(x_ref, y_ref, t_ref, ada_ref, n2_ref, fw_ref, wo_ref, wi_ref, wf_ref,
             x1_ref, mix_ref, h2_ref, act_ref, gu_ref, dffn_ref, dx2_ref, part_ref):
        @pl.when(pl.program_id(0) == 0)
        def _():
            part_ref[...] = jnp.zeros_like(part_ref)

        g1, sh2, sc2, g2 = ada_ref[2:3, :], ada_ref[3:4, :], ada_ref[4:5, :], ada_ref[5:6, :]
        mix = _mm(y_ref[...], wo_ref[...])
        mix_ref[...] = mix.astype(BF)
        x1 = x_ref[...] + g1 * mix
        x1_ref[...] = x1
        xh2, _ = _rms(x1)
        h2b = ((xh2 * n2_ref[...]) * (1.0 + sc2) + sh2).astype(BF)
        h2_ref[...] = h2b
        ffn = jnp.zeros((tm, D), F32)
        for lo, hi in FF_CHUNKS:
            gate = _nt(h2b, wi_ref[lo:hi, :])
            up = _nt(h2b, wi_ref[D_FF + lo:D_FF + hi, :])
            gu_ref[:, lo:hi] = gate.astype(BF)
            gu_ref[:, D_FF + lo:D_FF + hi] = up.astype(BF)
            a = (_silu(gate) * up).astype(BF)
            act_ref[:, lo:hi] = a
            ffn = ffn + _mm(a, wf_ref[lo:hi, :])
        x2 = x1 + g2 * ffn
        xh3, rstd3 = _rms(x2)
        err = xh3 * fw_ref[...] - t_ref[...]
        dy = err * (1.0 / D)
        dx2 = _rms_bwd(dy * fw_ref[...], xh3, rstd3)
        dx2_ref[...] = dx2
        dffn_ref[...] = (g2 * dx2).astype(BF)
        part_ref[0:1, :] += _colsum(dx2 * ffn)
        part_ref[1:2, :] += _colsum(dy * xh3)
        part_ref[2:3, :] += jnp.zeros((1, D), F32) + (0.5 / D) * jnp.sum(err * err)

    full = lambda a: pl.BlockSpec(a.shape, lambda i, nd=a.ndim: (0,) * nd)
    row = lambda w: pl.BlockSpec((tm, w), lambda i: (i, 0))
    return pl.pallas_call(
        body, name="fwd_ffn", grid=(t // tm,),
        out_shape=(jax.ShapeDtypeStruct((t, D), F32), jax.ShapeDtypeStruct((t, D), BF), jax.ShapeDtypeStruct((t, D), BF),
                   jax.ShapeDtypeStruct((t, D_FF), BF), jax.ShapeDtypeStruct((t, 2 * D_FF), BF),
                   jax.ShapeDtypeStruct((t, D), BF), jax.ShapeDtypeStruct((t, D), F32), jax.ShapeDtypeStruct((8, D), F32)),
        in_specs=[row(D), row(D), row(D), full(ada), full(n2w), full(fw), _whole_vmem(), _whole_vmem(), _whole_vmem()],
        out_specs=(row(D), row(D), row(D), row(D_FF), row(2 * D_FF), row(D), row(D),
                   pl.BlockSpec((8, D), lambda i: (0, 0))),
        compiler_params=_params(dimension_semantics=("arbitrary",)),
    )(x, ycat, tgt, ada, n2w, fw, w_out, w_fi, w_fo)


def _bwd_ffn(x1, gu, dffn, dx2, mix, ada, n2w, w_out, w_fi, w_fo):
    t = x1.shape[0]
    tm = _row_tile(t, 256)

    def body(x1_ref, gu_ref, dffn_ref, dx2_ref, mix_ref, ada_ref, n2_ref, wo_ref, wi_ref, wf_ref,
             dgu_ref, dx1_ref, dmix_ref, dycat_ref, part_ref):
        @pl.when(pl.program_id(0) == 0)
        def _():
            part_ref[...] = jnp.zeros_like(part_ref)

        g1, sc2 = ada_ref[2:3, :], ada_ref[4:5, :]
        dffn = dffn_ref[...]
        dh2 = jnp.zeros((tm, D), F32)
        for lo, hi in FF_CHUNKS:
            gate = gu_ref[:, lo:hi].astype(F32)
            up = gu_ref[:, D_FF + lo:D_FF + hi].astype(F32)
            dact = _nt(dffn, wf_ref[lo:hi, :])
            sg = _sig(gate)
            dgate = (dact * up * (sg * (1.0 + gate * (1.0 - sg)))).astype(BF)
            dup = (dact * (gate * sg)).astype(BF)
            dgu_ref[:, lo:hi] = dgate
            dgu_ref[:, D_FF + lo:D_FF + hi] = dup
            dh2 = dh2 + _mm(dgate, wi_ref[lo:hi, :]) + _mm(dup, wi_ref[D_FF + lo:D_FF + hi, :])
        x1 = x1_ref[...]
        xh2, rstd2 = _rms(x1)
        xn2 = xh2 * n2_ref[...]
        dxn2 = dh2 * (1.0 + sc2)
        dx1 = dx2_ref[...] + _rms_bwd(dxn2 * n2_ref[...], xh2, rstd2)
        dx1_ref[...] = dx1
        dmix = (g1 * dx1).astype(BF)
        dmix_ref[...] = dmix
        dycat_ref[...] = _nt(dmix, wo_ref[...])
        part_ref[0:1, :] += _colsum(dh2)
        part_ref[1:2, :] += _colsum(dh2 * xn2)
        part_ref[2:3, :] += _colsum(dxn2 * xh2)
        part_ref[3:4, :] += _colsum(dx1 * mix_ref[...].astype(F32))

    full = lambda a: pl.BlockSpec(a.shape, lambda i, nd=a.ndim: (0,) * nd)
    row = lambda w: pl.BlockSpec((tm, w), lambda i: (i, 0))
    return pl.pallas_call(
        body, name="bwd_ffn", grid=(t // tm,),
        out_shape=(jax.ShapeDtypeStruct((t, 2 * D_FF), BF), jax.ShapeDtypeStruct((t, D), F32),
                   jax.ShapeDtypeStruct((t, D), BF), jax.ShapeDtypeStruct((t, D), F32), jax.ShapeDtypeStruct((8, D), F32)),
        in_specs=[row(D), row(2 * D_FF), row(D), row(D), row(D), full(ada), full(n2w),
                  _whole_vmem(), _whole_vmem(), _whole_vmem()],
        out_specs=(row(2 * D_FF), row(D), row(D), row(D), pl.BlockSpec((8, D), lambda i: (0, 0))),
        compiler_params=_params(dimension_semantics=("arbitrary",)),
    )(x1, gu, dffn, dx2, mix, ada, n2w, w_out, w_fi, w_fo)


def _bwd_in(x, dproj, dx1, ada, n1w, w_in_st, to_scatter, to_gather):
    t = x.shape[0]
    tm = _row_tile(t, 1024)
    n_t = t // tm
    ns = len(to_scatter)
    ng = len(to_gather)
    nx = ns + ng

    def body(*refs):
        x_ref, dp_ref, dx1_ref, ada_ref, n1_ref, wst_ref = refs[:6]
        x_in = refs[6:6 + nx]
        gx_ref, part_ref = refs[6 + nx:8 + nx]
        x_out = refs[8 + nx:8 + 2 * nx]
        w_ref, sems = refs[8 + 2 * nx:10 + 2 * nx]
        sem_refs = refs[10 + 2 * nx:]
        ex = _Exchange(x_in[:ns], x_out[:ns], *sem_refs[:3], True)
        gx = _Exchange(x_in[ns:], x_out[ns:], *sem_refs[3:], False) if ng else None

        @pl.when(pl.program_id(0) == 0)
        def _():
            ex.start()
            if ng:
                gx.start()
            part_ref[...] = jnp.zeros_like(part_ref)
            _load_columns(wst_ref, w_ref, sems)

        dh1 = _nt(dp_ref[...], w_ref[...])
        xh, rstd = _rms(x_ref[...])
        xn = xh * n1_ref[...]
        dxn = dh1 * (1.0 + ada_ref[1:2, :])
        gx_ref[...] = dx1_ref[...] + _rms_bwd(dxn * n1_ref[...], xh, rstd)
        part_ref[0:1, :] += _colsum(dh1)
        part_ref[1:2, :] += _colsum(dh1 * xn)
        part_ref[2:3, :] += _colsum(dxn * xh)

        @pl.when(pl.program_id(0) == n_t - 1)
        def _():
            ex.wait()
            if ng:
                gx.wait()

    full = lambda a: pl.BlockSpec(a.shape, lambda i, nd=a.ndim: (0,) * nd)
    row = lambda w: pl.BlockSpec((tm, w), lambda i: (i, 0))
    outs = pl.pallas_call(
        body, name="bwd_in", grid=(n_t,),
        out_shape=(jax.ShapeDtypeStruct((t, D), F32), jax.ShapeDtypeStruct((8, D), F32),
                   *_exchange_out_shapes(to_scatter, True), *_exchange_out_shapes(to_gather, False)),
        in_specs=[row(D), row(D_IN), row(D), full(ada), full(n1w), _any()] + [_any() for _ in range(nx)],
        out_specs=(row(D), pl.BlockSpec((8, D), lambda i: (0, 0)), *[_any() for _ in range(nx)]),
        scratch_shapes=[pltpu.VMEM((D, D_IN), BF), pltpu.SemaphoreType.DMA((N_DEV,))] + _exchange_sems(ns)
        + (_exchange_sems(ng) if ng else []),
        compiler_params=_params(dimension_semantics=("arbitrary",), has_side_effects=True),
    )(x, dproj, dx1, ada, n1w, w_in_st, *to_scatter, *to_gather)
    return outs[0], outs[1], outs[2:2 + ns], outs[2 + ns:]


def _wgrad(a, b, name, a_spec, b_spec, out_shape, out_spec, grid, acc_shape, split=1, to_gather=()):
    n_j, n_t = grid
    ng = len(to_gather)

    def body(*refs):
        a_ref, b_ref = refs[:2]
        g_in = refs[2:2 + ng]
        o_ref = refs[2 + ng]
        g_out = refs[3 + ng:3 + 2 * ng]
        acc = refs[3 + 2 * ng]
        first = (pl.program_id(0) == 0) & (pl.program_id(1) == 0)
        last = (pl.program_id(0) == n_j - 1) & (pl.program_id(1) == n_t - 1)
        if ng:
            gx = _Exchange(g_in, g_out, *refs[4 + 2 * ng:], False)

            @pl.when(first)
            def _():
                gx.start()

        @pl.when(pl.program_id(1) == 0)
        def _():
            acc[...] = jnp.zeros_like(acc)

        acc[...] += _tn(a_ref[...], b_ref[...])

        @pl.when(pl.program_id(1) == n_t - 1)
        def _():
            if split == 1:
                o_ref[...] = acc[...].astype(BF)
            else:
                w = acc_shape[1] // split
                for s in range(split):
                    o_ref[s] = acc[:, s * w:(s + 1) * w].astype(BF)

        if ng:
            @pl.when(last)
            def _():
                gx.wait()

    outs = pl.pallas_call(
        body, name=name, grid=grid,
        out_shape=(jax.ShapeDtypeStruct(out_shape, BF), *_exchange_out_shapes(to_gather, False)),
        in_specs=[a_spec, b_spec] + [_any() for _ in to_gather], out_specs=(out_spec, *[_any() for _ in to_gather]),
        scratch_shapes=[pltpu.VMEM(acc_shape, F32)] + (_exchange_sems(ng) if ng else []),
        compiler_params=_params(dimension_semantics=("arbitrary", "arbitrary"), has_side_effects=bool(ng)),
    )(a, b, *to_gather)
    return (outs[0], outs[1:]) if ng else outs[0]


def _adamw_math(w, g, m, v):
    m = ADAM_B1 * m + (1.0 - ADAM_B1) * g
    v = ADAM_B2 * v + (1.0 - ADAM_B2) * (g * g)
    m_hat = m / (1.0 - ADAM_B1 ** ADAM_STEP)
    v_hat = v / (1.0 - ADAM_B2 ** ADAM_STEP)
    delta = -ADAM_LR * (m_hat / (jnp.sqrt(v_hat) + ADAM_EPS) + ADAM_WD * w)
    return delta, m, v


def _adamw_recv(w, m, v, recv, name, tr):
    r, c = w.shape

    def body(w_ref, m_ref, v_ref, r_ref, g_ref, d_ref, nm_ref, nv_ref):
        g = r_ref[0].astype(F32)
        for k in range(1, N_DEV):
            g = g + r_ref[k].astype(F32)
        g_ref[...] = g
        d_ref[...], nm_ref[...], nv_ref[...] = _adamw_math(w_ref[...], g, m_ref[...], v_ref[...])

    row = pl.BlockSpec((tr, c), lambda i: (i, 0))
    return pl.pallas_call(
        body, name=name, grid=(r // tr,),
        out_shape=tuple(jax.ShapeDtypeStruct((r, c), F32) for _ in range(4)),
        in_specs=[row, row, row, pl.BlockSpec((N_DEV, tr, c), lambda i: (0, i, 0))],
        out_specs=(row, row, row, row),
        compiler_params=_params(dimension_semantics=("arbitrary",)),
    )(w, m, v, recv)


def _adamw_ada(w, m, v, cact, dada_cols):
    r, c = w.shape
    tr = 256

    def body(w_ref, m_ref, v_ref, ca_ref, da_ref, g_ref, d_ref, nm_ref, nv_ref):
        g = _tn(ca_ref[...].astype(BF), da_ref[...].astype(BF))
        g_ref[...] = g
        d_ref[...], nm_ref[...], nv_ref[...] = _adamw_math(w_ref[...], g, m_ref[...], v_ref[...])

    row = pl.BlockSpec((tr, c), lambda i: (i, 0))
    return pl.pallas_call(
        body, name="adamw_ada", grid=(r // tr,),
        out_shape=tuple(jax.ShapeDtypeStruct((r, c), F32) for _ in range(4)),
        in_specs=[row, row, row, pl.BlockSpec((N_DEV, tr), lambda i: (0, i)), pl.BlockSpec(dada_cols.shape, lambda i: (0, 0))],
        out_specs=(row, row, row, row),
        compiler_params=_params(dimension_semantics=("arbitrary",)),
    )(w, m, v, cact, dada_cols)


def _adamw_small(gathered, wmv):
    n_g = len(gathered)
    n_p = len(wmv)
    flat = [a for trip in wmv for a in trip]

    def body(*refs):
        g_refs = refs[:n_g]
        p_refs = refs[n_g:n_g + 3 * n_p]
        o_refs = refs[n_g + 3 * n_p:]

        def total(ref):
            s = ref[0]
            for k in range(1, N_DEV):
                s = s + ref[k]
            return s

        f3, b3, b1, dws, dbs, dlnw, dlnb, dlo, dgn = [total(r) for r in g_refs]
        dada_rows = [b1[0:1], b1[1:2], b3[3:4], b3[0:1], b3[1:2], f3[0:1]]
        for r, g in enumerate(dada_rows):
            cs = slice(r * D, (r + 1) * D)
            w, m, v = p_refs[0][:, cs], p_refs[1][:, cs], p_refs[2][:, cs]
            o_refs[0][:, cs] = g
            o_refs[1][:, cs], o_refs[2][:, cs], o_refs[3][:, cs] = _adamw_math(w, g, m, v)
        grads = [None, b1[2:3], dws, dbs, dlnw, dlnb, dlo, dgn, b3[2:3], f3[1:2]]
        for i, g in enumerate(grads):
            if g is None:
                continue
            w, m, v = p_refs[3 * i][...], p_refs[3 * i + 1][...], p_refs[3 * i + 2][...]
            o_refs[4 * i][...] = g
            o_refs[4 * i + 1][...], o_refs[4 * i + 2][...], o_refs[4 * i + 3][...] = _adamw_math(w, g, m, v)
        o_refs[4 * n_p][...] = jnp.zeros((8, 128), F32) + f3[2:3, 0:128]

    out_shape = []
    for w, _, _ in wmv:
        out_shape += [jax.ShapeDtypeStruct(w.shape, F32)] * 4
    out_shape.append(jax.ShapeDtypeStruct((8, 128), F32))
    n_in = n_g + 3 * n_p
    return pl.pallas_call(
        body, name="adamw_small",
        out_shape=tuple(out_shape),
        in_specs=[_whole_vmem()] * n_in, out_specs=tuple(_whole_vmem() for _ in out_shape),
        compiler_params=_params(),
    )(*gathered, *flat)


def kernel(x, c, w_ada, b_ada, norm1_w, w_in, w_s, b_s, v_ln_w, v_ln_b, lower_bounds, gn_w, w_out, norm2_w, w_ffn_in, w_ffn_out, final_norm_w, loss_target, m_w_ada, m_b_ada, m_norm1_w, m_w_in, m_w_s, m_b_s, m_v_ln_w, m_v_ln_b, m_lower_bounds, m_gn_w, m_w_out, m_norm2_w, m_w_ffn_in, m_w_ffn_out, m_final_norm_w, v_w_ada, v_b_ada, v_norm1_w, v_w_in, v_w_s, v_b_s, v_v_ln_w, v_v_ln_b, v_lower_bounds, v_gn_w, v_w_out, v_norm2_w, v_w_ffn_in, v_w_ffn_out, v_final_norm_w):
    me = 4 * lax.axis_index("x") + 2 * lax.axis_index("y") + lax.axis_index("c")
    t = x.shape[1]
    x2d = x.reshape(t, D)
    tgt = loss_target.reshape(t, D)
    ada_cols = w_ada.shape[2]

    tp = lambda a: jnp.swapaxes(a[0], 0, 1)
    win_b, wout_b, wfi_b, wfo_b = _cast_bf16([w_in[0], w_out[0], tp(w_ffn_in), w_ffn_out[0]], "cast_weights")

    b_cols = lax.dynamic_slice(b_ada, (0, me * ada_cols), (1, ada_cols))
    win_st, ada_st, cact = _startup(win_b, c, w_ada[0], b_cols)
    ada = lax.dynamic_index_in_dim(ada_st, me, axis=1, keepdims=False).reshape(6, D)

    tables = _decay_tables()
    ws3 = w_s[0]
    bs_t = b_s[0].T

    proj, h1 = _fwd_in(x2d, ada, norm1_w, win_st)
    ycat, states, (wout_st, wfi_st, wfo_st) = _fwd_mix(proj, ws3, bs_t, v_ln_w, v_ln_b, lower_bounds, gn_w, tables,
                                                       [wout_b, wfi_b, wfo_b])
    w_out_full = wout_st.reshape(D, D)
    w_fo_full = wfo_st.reshape(D_FF, D)
    w_fi_full = wfi_st.reshape(2 * D_FF, D)
    x1, mixb, h2, act, gu, dffn, dx2, part_f = _fwd_ffn(x2d, ycat, tgt, ada, norm2_w, final_norm_w.reshape(1, D),
                                                        w_out_full, w_fi_full, w_fo_full)
    dgu, dx1, dmix, dycat, part_b3 = _bwd_ffn(x1, gu, dffn, dx2, mixb, ada, norm2_w, w_out_full, w_fi_full, w_fo_full)
    tk = _row_tile(t, 2048)
    n_t = t // tk
    win_cols = D_IN // N_DEV
    dwout = _wgrad(ycat, dmix, "wgrad_out",
                   pl.BlockSpec((tk, D), lambda j, i: (i, 0)), pl.BlockSpec((tk, D), lambda j, i: (i, 0)),
                   (D, D), pl.BlockSpec((D, D), lambda j, i: (0, 0)), (1, n_t), (D, D))
    dwfi = _wgrad(dgu, h2, "wgrad_ffn_in",
                  pl.BlockSpec((tk, FF_PAIR), lambda j, i: (i, j)), pl.BlockSpec((tk, D), lambda j, i: (i, 0)),
                  (4, FF_PAIR, D), pl.BlockSpec((None, FF_PAIR, D), lambda j, i: (j, 0, 0)), (4, n_t), (FF_PAIR, D))
    dwfo = _wgrad(act, dffn, "wgrad_ffn_out",
                  pl.BlockSpec((tk, FF_PAIR), lambda j, i: (i, j)), pl.BlockSpec((tk, D), lambda j, i: (i, 0)),
                  (2, FF_PAIR, D), pl.BlockSpec((None, FF_PAIR, D), lambda j, i: (j, 0, 0)), (2, n_t), (FF_PAIR, D))
    dproj, (dws, dbs_t, dlnw, dlnb, dlower, dgnw), (r_out, r_fi, r_fo) = _bwd_mix(
        proj, dycat, states, ws3, bs_t, v_ln_w, v_ln_b, lower_bounds, gn_w, tables,
        [dwout.reshape(N_DEV, D // N_DEV, D), dwfi.reshape(N_DEV, FF_BLK, D), dwfo.reshape(N_DEV, D_FF // N_DEV, D)])
    dwin, early = _wgrad(h1, dproj, "wgrad_in",
                         pl.BlockSpec((tk, D), lambda j, i: (i, 0)), pl.BlockSpec((tk, 4 * win_cols), lambda j, i: (i, j)),
                         (N_DEV, D, win_cols), pl.BlockSpec((4, D, win_cols), lambda j, i: (j, 0, 0)), (N_DEV // 4, n_t),
                         (D, 4 * win_cols), split=4, to_gather=[part_f, part_b3, dws, dbs_t, dlnw, dlnb, dlower, dgnw])
    grad_x, part_b1, (r_in,), _ = _bwd_in(x2d, dproj, dx1, ada, norm1_w, win_st, [dwin], [])

    g_w_in, d_w_in, nm_w_in, nv_w_in = _adamw_recv(w_in[0], m_w_in[0], v_w_in[0], r_in, "adamw_w_in", 256)
    g_w_out, d_w_out, nm_w_out, nv_w_out = _adamw_recv(w_out[0], m_w_out[0], v_w_out[0], r_out, "adamw_w_out", 128)
    fi_t = _adamw_recv(tp(w_ffn_in), tp(m_w_ffn_in), tp(v_w_ffn_in), r_fi, "adamw_w_ffn_in", 176)
    g_w_fi, d_w_fi, nm_w_fi, nv_w_fi = [jnp.swapaxes(a, 0, 1) for a in fi_t]
    g_w_fo, d_w_fo, nm_w_fo, nv_w_fo = _adamw_recv(w_ffn_out[0], m_w_ffn_out[0], v_w_ffn_out[0], r_fo, "adamw_w_ffn_out", 176)

    (b1_all,) = _exchange([part_b1], "gather_small", False, True)
    gathered = [early[0], early[1], b1_all, *early[2:]]
    f3_all, b3_all = gathered[0], gathered[1]
    dada_all = jnp.stack([b1_all[:, 0], b1_all[:, 1], b3_all[:, 3], b3_all[:, 0], b3_all[:, 1], f3_all[:, 0]], axis=1)
    dada_cols = lax.dynamic_slice(dada_all.reshape(N_DEV, 6 * D), (0, me * ada_cols), (N_DEV, ada_cols))
    g_w_ada, d_w_ada, nm_w_ada, nv_w_ada = _adamw_ada(w_ada[0], m_w_ada[0], v_w_ada[0], cact, dada_cols)

    r1 = lambda a: a.reshape(1, D)
    tr = lambda a: a[0].T
    wmv = [
        (b_ada, m_b_ada, v_b_ada),
        (norm1_w, m_norm1_w, v_norm1_w),
        (w_s[0], m_w_s[0], v_w_s[0]),
        (tr(b_s), tr(m_b_s), tr(v_b_s)),
        (v_ln_w, m_v_ln_w, v_v_ln_w),
        (v_ln_b, m_v_ln_b, v_v_ln_b),
        (lower_bounds, m_lower_bounds, v_lower_bounds),
        (gn_w, m_gn_w, v_gn_w),
        (norm2_w, m_norm2_w, v_norm2_w),
        (r1(final_norm_w), r1(m_final_norm_w), r1(v_final_norm_w)),
    ]
    small = _adamw_small(gathered, wmv)
    loss = small[-1][0, 0]

    def unshape(i, a):
        if i == 2:
            return a.reshape(1, NH, BLK, BLK)
        if i == 3:
            return a.T.reshape(1, NH, BLK)
        if i == 9:
            return a.reshape(D)
        return a

    def small_out(kind):
        return [unshape(i, small[4 * i + kind]) for i in range(len(wmv))]

    e3 = lambda a: a[None]
    big = {
        0: (e3(g_w_ada), e3(g_w_in), e3(g_w_out), e3(g_w_fi), e3(g_w_fo)),
        1: (e3(d_w_ada), e3(d_w_in), e3(d_w_out), e3(d_w_fi), e3(d_w_fo)),
        2: (e3(nm_w_ada), e3(nm_w_in), e3(nm_w_out), e3(nm_w_fi), e3(nm_w_fo)),
        3: (e3(nv_w_ada), e3(nv_w_in), e3(nv_w_out), e3(nv_w_fi), e3(nv_w_fo)),
    }

    def ordered(kind):
        s = small_out(kind)
        b_ = big[kind]
        return [b_[0], s[0], s[1], b_[1], s[2], s[3], s[4], s[5], s[6], s[7], b_[2], s[8], b_[3], b_[4], s[9]]

    return (loss, grad_x.reshape(1, t, D), *ordered(0), *ordered(1), *ordered(2), *ordered(3))
```

```python
import numpy as np
import jax
import jax.numpy as jnp
from jax import lax
from jax.experimental import pallas as pl
from jax.experimental.pallas import tpu as pltpu

F32 = jnp.float32
BF = jnp.bfloat16
MESH = pl.DeviceIdType.MESH

N_DEV = 8
D = 1024
D_IN = 3072
D_FF = 2816
FF_BLK = D_FF // 4
FF_CHUNKS = ((0, D_FF),)
FF_PAIR = 2 * FF_BLK
CH = 64
BLK = 128
MIX_TILE = 512
NH = 4
HD = 128
EPS = 1e-6
LEVELS = (32, 16, 8, 4, 2, 1)

ADAM_LR = 0.001
ADAM_B1 = 0.9
ADAM_B2 = 0.999
ADAM_EPS = 1e-08
ADAM_WD = 0.01
ADAM_STEP = 10

VMEM_LIMIT = 56 * 1024 * 1024


def _params(**kw):
    return pltpu.CompilerParams(vmem_limit_bytes=VMEM_LIMIT, **kw)


def _whole_vmem():
    return pl.BlockSpec(memory_space=pltpu.VMEM)


def _any():
    return pl.BlockSpec(memory_space=pl.ANY)


class _Exchange:
    def __init__(self, ins, outs, send_sems, recv_sems, local_sems, scatter):
        self.ins, self.outs, self.scatter = ins, outs, scatter
        self.send_sems, self.recv_sems, self.local_sems = send_sems, recv_sems, local_sems
        x, y, c = lax.axis_index("x"), lax.axis_index("y"), lax.axis_index("c")
        self.me = 4 * x + 2 * y + c
        self.peers = []
        for k in range(1, N_DEV):
            peer = (1 - x if (k >> 2) & 1 else x, 1 - y if (k >> 1) & 1 else y, 1 - c if k & 1 else c)
            self.peers.append((peer, 4 * peer[0] + 2 * peer[1] + peer[2]))

    def _src(self, a, idx):
        return self.ins[a].at[idx] if self.scatter else self.ins[a]

    def _local(self):
        return [pltpu.make_async_copy(self._src(a, self.me), self.outs[a].at[self.me], self.local_sems.at[a])
                for a in range(len(self.ins))]

    def _remote(self, a, k, dst_slot):
        peer, peer_idx = self.peers[k]
        return pltpu.make_async_remote_copy(
            src_ref=self._src(a, peer_idx), dst_ref=self.outs[a].at[dst_slot],
            send_sem=self.send_sems.at[a, k], recv_sem=self.recv_sems.at[a, k],
            device_id=peer, device_id_type=MESH)

    def start(self):
        for cp in self._local():
            cp.start()
        for k in range(N_DEV - 1):
            for a in range(len(self.ins)):
                self._remote(a, k, self.me).start()

    def wait(self):
        for k in range(N_DEV - 1):
            for a in range(len(self.ins)):
                self._remote(a, k, self.peers[k][1]).wait()
        for cp in self._local():
            cp.wait()


class _Gather2:
    def __init__(self, ins, outs, send_sems, recv_sems, local_sems):
        self.ins, self.outs = ins, outs
        self.send_sems, self.recv_sems, self.local_sems = send_sems, recv_sems, local_sems
        x, y, c = lax.axis_index("x"), lax.axis_index("y"), lax.axis_index("c")
        self.c = c
        self.me = 4 * x + 2 * y + c
        self.sibling = (x, y, 1 - c)
        self.chips = [(1 - x, y), (x, 1 - y), (1 - x, 1 - y)]

    @staticmethod
    def _idx(px, py, pc):
        return 4 * px + 2 * py + pc

    def _copy(self, a, k, slot, to, own):
        src = self.ins[a] if own else self.outs[a].at[slot]
        return pltpu.make_async_remote_copy(
            src_ref=src, dst_ref=self.outs[a].at[slot],
            send_sem=self.send_sems.at[a, k], recv_sem=self.recv_sems.at[a, k],
            device_id=to, device_id_type=MESH)

    def _local(self):
        return [pltpu.make_async_copy(self.ins[a], self.outs[a].at[self.me], self.local_sems.at[a])
                for a in range(len(self.ins))]

    def start(self):
        for cp in self._local():
            cp.start()
        for a in range(len(self.ins)):
            self._copy(a, 0, self.me, self.sibling, True).start()
            for j, chip in enumerate(self.chips):
                self._copy(a, 1 + j, self.me, (*chip, self.c), True).start()

    def forward(self):
        for j, chip in enumerate(self.chips):
            for a in range(len(self.ins)):
                slot = self._idx(*chip, self.c)
                self._copy(a, 1 + j, slot, (*chip, self.c), True).wait_recv()
                self._copy(a, 4 + j, slot, self.sibling, False).start()

    def finish(self):
        for a in range(len(self.ins)):
            self._copy(a, 0, self._idx(*self.sibling), self.sibling, True).wait_recv()
            for j, chip in enumerate(self.chips):
                self._copy(a, 4 + j, self._idx(*chip, 1 - self.c), self.sibling, False).wait_recv()
            self._copy(a, 0, self.me, self.sibling, True).wait_send()
            for j, chip in enumerate(self.chips):
                self._copy(a, 1 + j, self.me, (*chip, self.c), True).wait_send()
                self._copy(a, 4 + j, self._idx(*chip, self.c), self.sibling, False).wait_send()
        for cp in self._local():
            cp.wait()


def _exchange_sems(n):
    return [pltpu.SemaphoreType.DMA((n, N_DEV - 1)), pltpu.SemaphoreType.DMA((n, N_DEV - 1)), pltpu.SemaphoreType.DMA((n,))]


def _exchange_out_shapes(arrays, scatter):
    return [jax.ShapeDtypeStruct(a.shape if scatter else (N_DEV,) + a.shape, a.dtype) for a in arrays]


def _exchange(arrays, name, scatter, in_vmem):
    n = len(arrays)

    def body(*refs):
        ex = _Exchange(refs[:n], refs[n:2 * n], *refs[2 * n:], scatter)
        ex.start()
        ex.wait()

    spec = _whole_vmem if in_vmem else _any
    return pl.pallas_call(
        body, name=name,
        out_shape=tuple(_exchange_out_shapes(arrays, scatter)),
        in_specs=[spec() for _ in arrays],
        out_specs=tuple(spec() for _ in arrays),
        scratch_shapes=_exchange_sems(n),
        compiler_params=_params(has_side_effects=True),
    )(*arrays)


def _cast_bf16(arrays, name):
    n = len(arrays)

    def body(*refs):
        for i in range(n):
            refs[n + i][...] = refs[i][...].astype(BF)

    return pl.pallas_call(
        body, name=name,
        out_shape=tuple(jax.ShapeDtypeStruct(a.shape, BF) for a in arrays),
        in_specs=[_whole_vmem() for _ in arrays],
        out_specs=tuple(_whole_vmem() for _ in arrays),
        compiler_params=_params(),
    )(*arrays)


def _load_columns(stacked_ref, full_ref, sems):
    c = stacked_ref.shape[2]
    cps = [pltpu.make_async_copy(stacked_ref.at[d], full_ref.at[:, pl.ds(d * c, c)], sems.at[d]) for d in range(N_DEV)]
    for cp in cps:
        cp.start()
    for cp in cps:
        cp.wait()


def _sig(v):
    return 0.5 * jnp.tanh(0.5 * v) + 0.5


@jax.custom_vjp
def _silu(v):
    return v * _sig(v)


def _silu_fwd(v):
    return _silu(v), v


def _silu_bwd(v, g):
    s = _sig(v)
    return (g * (s * (1.0 + v * (1.0 - s))),)


_silu.defvjp(_silu_fwd, _silu_bwd)


@jax.custom_vjp
def _sigmoid_rel(v):
    e = jnp.exp(-jnp.abs(v))
    d = 1.0 + e
    r = pl.reciprocal(d, approx=True)
    r = r * (2.0 - d * r)
    r = r * (2.0 - d * r)
    return jnp.where(v >= 0.0, r, e * r)


def _sigmoid_rel_fwd(v):
    s = _sigmoid_rel(v)
    return s, s


def _sigmoid_rel_bwd(s, g):
    return (g * (s * (1.0 - s)),)


_sigmoid_rel.defvjp(_sigmoid_rel_fwd, _sigmoid_rel_bwd)


def _gelu(v):
    return 0.5 * v * (1.0 + lax.erf(v * 0.7071067811865476))


def _dot(a, b, ca, cb):
    return lax.dot_general(a, b, (((ca,), (cb,)), ((), ())), preferred_element_type=F32)


@jax.custom_vjp
def _mm(a, b):
    return _dot(a, b, 1, 0)


def _mm_fwd(a, b):
    return _dot(a, b, 1, 0), (a, b)


def _mm_bwd(res, g):
    a, b = res
    gb = g.astype(BF)
    return _dot(gb, b, 1, 1).astype(a.dtype), _dot(a, gb, 0, 0).astype(b.dtype)


_mm.defvjp(_mm_fwd, _mm_bwd)


@jax.custom_vjp
def _nt(a, b):
    return _dot(a, b, 1, 1)


def _nt_fwd(a, b):
    return _dot(a, b, 1, 1), (a, b)


def _nt_bwd(res, g):
    a, b = res
    gb = g.astype(BF)
    return _dot(gb, b, 1, 0).astype(a.dtype), _dot(gb, a, 0, 0).astype(b.dtype)


_nt.defvjp(_nt_fwd, _nt_bwd)


@jax.custom_vjp
def _tn(a, b):
    return _dot(a, b, 0, 0)


def _tn_fwd(a, b):
    return _dot(a, b, 0, 0), (a, b)


def _tn_bwd(res, g):
    a, b = res
    gb = g.astype(BF)
    return _dot(b, gb, 1, 1).astype(a.dtype), _dot(a, gb, 1, 0).astype(b.dtype)


_tn.defvjp(_tn_fwd, _tn_bwd)


def _make_cast_dot(ca, cb, da_dims, db_dims):
    @jax.custom_vjp
    def dot(a, b):
        return _dot(a.astype(BF), b.astype(BF), ca, cb)

    def fwd(a, b):
        ab, bb = a.astype(BF), b.astype(BF)
        return _dot(ab, bb, ca, cb), (ab, bb)

    def bwd(res, g):
        ops = {"a": res[0], "b": res[1], "g": g.astype(BF)}
        return (_dot(ops[da_dims[0]], ops[da_dims[1]], da_dims[2], da_dims[3]),
                _dot(ops[db_dims[0]], ops[db_dims[1]], db_dims[2], db_dims[3]))

    dot.defvjp(fwd, bwd)
    return dot


_mm_c = _make_cast_dot(1, 0, ("g", "b", 1, 1), ("a", "g", 0, 0))
_nt_c = _make_cast_dot(1, 1, ("g", "b", 1, 0), ("g", "a", 0, 0))
_tn_c = _make_cast_dot(0, 0, ("b", "g", 1, 1), ("a", "g", 1, 0))


def _startup(win_b, c, w_ada, b_cols):
    ncol = w_ada.shape[1]

    def body(win_ref, c_ref, w_ref, b_ref, winst_ref, adast_ref, cact_ref, call_ref, blk_ref,
             ws, wr, wl, cs, cr, cl, as_, ar, al):
        big = _Gather2([win_ref], [winst_ref], ws, wr, wl)
        big.start()
        gc = _Exchange([c_ref], [call_ref], cs, cr, cl, False)
        gc.start()
        gc.wait()
        ca = _silu(call_ref[...].reshape(N_DEV, D))
        cact_ref[...] = ca
        blk_ref[...] = _mm(ca.astype(BF), w_ref[...].astype(BF)) + b_ref[...]
        ga = _Exchange([blk_ref], [adast_ref], as_, ar, al, False)
        ga.start()
        ga.wait()
        big.forward()
        big.finish()

    return pl.pallas_call(
        body, name="startup",
        out_shape=(jax.ShapeDtypeStruct((N_DEV,) + win_b.shape, BF), jax.ShapeDtypeStruct((N_DEV, N_DEV, ncol), F32),
                   jax.ShapeDtypeStruct((N_DEV, D), F32)),
        in_specs=[_any(), _whole_vmem(), _whole_vmem(), _whole_vmem()],
        out_specs=(_any(), _whole_vmem(), _whole_vmem()),
        scratch_shapes=[pltpu.VMEM((N_DEV, 1, D), F32), pltpu.VMEM((N_DEV, ncol), F32)]
        + _exchange_sems(1) + _exchange_sems(1) + _exchange_sems(1),
        compiler_params=_params(has_side_effects=True),
    )(win_b, c, w_ada, b_cols)


def _decay_tables():
    t = np.arange(CH)
    tri = (t[None, :] <= t[:, None]).astype(np.float32)
    masks = []
    for h in LEVELS:
        m = (t // (2 * h)) * (2 * h) + h
        upper = t >= m
        same = (t[:, None] // (2 * h)) == (t[None, :] // (2 * h))
        masks.append(same & upper[:, None] & (~upper)[None, :])
    masks.append(np.eye(CH, dtype=bool))
    lv = [np.where((t % (2 * h)) >= h, 1.0, -1.0) for h in LEVELS[:4]]
    m4 = t % 4
    lv += [(m4 == 0) * 1.0, (m4 >= 2) * 1.0, (m4 == 3) * 1.0, (t % 2 == 1) * 1.0]
    lvl = np.broadcast_to(np.stack(lv)[:, :, None], (8, CH, D // 2)).astype(np.float32)
    cid = np.arange(BLK) // CH
    gmask = (cid[:, None] >= cid[None, :]).astype(np.float32)
    return (jnp.asarray(tri, BF), jnp.asarray(np.stack(masks).astype(np.float32)), jnp.asarray(gmask), jnp.asarray(lvl))


def _split2(v):
    v1 = v.astype(BF)
    return v1, (v - v1.astype(F32)).astype(BF)


@jax.custom_vjp
def _cumsum_mm(tri, v):
    p1, p2 = _split2(v)
    return _dot(tri, p1, 1, 0) + _dot(tri, p2, 1, 0)


def _cumsum_mm_fwd(tri, v):
    return _cumsum_mm(tri, v), tri


def _cumsum_mm_bwd(tri, g):
    p1, p2 = _split2(g)
    return jnp.zeros_like(tri), _dot(tri, p1, 0, 0) + _dot(tri, p2, 0, 0)


_cumsum_mm.defvjp(_cumsum_mm_fwd, _cumsum_mm_bwd)


def _make_row_roll(shift):
    @jax.custom_vjp
    def roll(x):
        return pltpu.roll(x, shift % CH, 0)

    def fwd(x):
        return roll(x), None

    def bwd(_, g):
        return (pltpu.roll(g, (-shift) % CH, 0),)

    roll.defvjp(fwd, bwd)
    return roll


_prev_row = _make_row_roll(1)
_next_row = _make_row_roll(-1)


def _mix_tile(proj, state, w_s, b_s_t, ln_w, ln_b, lower, gn_w, consts):
    tri, masks, gmask, lvl = consts
    mt = proj.shape[0]
    u = proj[:, 0:512]
    v = proj[:, 512:1024]
    q = proj[:, 1024:1536]
    fl = proj[:, 1536:2048]
    inp = proj[:, 2048:2560]
    g = proj[:, 2560:3072]

    ug = _gelu(u)
    vg = _gelu(v)
    mu = jnp.mean(vg, axis=-1, keepdims=True)
    vc = vg - mu
    var = jnp.mean(vc * vc, axis=-1, keepdims=True)
    vn = vc * lax.rsqrt(var + EPS) * ln_w + ln_b
    wsm = [w_s[h] * gmask for h in range(NH)]
    ya = [[None] * NH for _ in range(mt // BLK)]
    for bi in range(mt // BLK):
        rows = slice(bi * BLK, (bi + 1) * BLK)
        for h in range(NH):
            cols = slice(h * HD, (h + 1) * HD)
            ya[bi][h] = ug[rows, cols] * (_mm_c(wsm[h], vn[rows, cols]) + b_s_t[:, h:h + 1])
    ya_full = jnp.concatenate([jnp.concatenate(r, axis=1) for r in ya], axis=0)

    l0 = lower[0:1, :]
    l1 = lower[1:2, :]
    mx = jnp.maximum(l0, l1)
    e0 = jnp.exp(l0 - mx)
    e1 = jnp.exp(l1 - mx)
    lb = e0 / (e0 + e1)
    qf = _silu(q)
    f = lb + (1.0 - lb) * _sigmoid_rel(fl)
    logf = jnp.log(f)
    kk = 1.0 - f
    gate = _silu(g)
    nl = len(LEVELS)
    half = D // 2
    st = list(state)
    yb = [[None] * NH for _ in range(mt // CH)]
    for ci in range(mt // CH):
        rows = slice(ci * CH, (ci + 1) * CH)
        lc = logf[rows]
        b = _cumsum_mm(tri, lc)
        xb = jnp.exp(b)
        xinv = jnp.exp(b[CH - 1:CH, :] - b)
        xl = []
        for i, hs in enumerate(LEVELS[:3]):
            refs = [jnp.broadcast_to(b[r:r + 1, :], (2 * hs, half)) for r in range(hs - 1, CH, 2 * hs)]
            bref = refs[0] if len(refs) == 1 else jnp.concatenate(refs, axis=0)
            xl.append(jnp.exp(lvl[i] * (b - bref)))
        b3 = b.reshape(CH // 8, 8, half)
        bref = jnp.broadcast_to(b3[:, 3:4, :], (CH // 8, 8, half)).reshape(CH, half)
        xl.append(jnp.exp(lvl[3] * (b - bref)))
        xl.append(jnp.exp(lvl[4] * _next_row(lc) + lvl[5] * lc + lvl[6] * _prev_row(lc)))
        xl.append(jnp.exp(lvl[7] * lc))
        qc = qf[rows]
        kc = kk[rows]
        zsrc = [jnp.concatenate([(qc if (r0 // hs) % 2 == 1 else kc)[r0:r0 + hs] for r0 in range(0, CH, hs)], axis=0)
                for hs in LEVELS[:3]]
        for h in range(NH):
            cols = slice(h * HD, (h + 1) * HD)
            qh = qc[:, cols]
            kh = kc[:, cols]
            vh = inp[rows, cols]
            inter = _nt_c(qh * xb[:, cols], st[h])
            attn = masks[nl] * _nt_c(qh, kh)
            for li in range(nl):
                if li < 3:
                    z = zsrc[li][:, cols] * xl[li][:, cols]
                    pairs = _nt_c(z, z)
                else:
                    xx = xl[li][:, cols]
                    pairs = _nt_c(qh * xx, kh * xx)
                attn = attn + masks[li] * pairs
            o = inter + _mm_c(attn, vh)
            st[h] = st[h] * xb[CH - 1:CH, cols] + _tn_c(vh, kh * xinv[:, cols])
            rs = lax.rsqrt(jnp.mean(o * o, axis=-1, keepdims=True) + EPS)
            yb[ci][h] = o * rs * gn_w * gate[rows, cols]
    yb_full = jnp.concatenate([jnp.concatenate(r, axis=1) for r in yb], axis=0)
    return jnp.concatenate([ya_full, yb_full], axis=1), tuple(st)


def _fwd_mix(proj, w_s, b_s_t, ln_w, ln_b, lower, gn_w, tables, to_gather):
    t = proj.shape[0]
    mt = _row_tile(t, MIX_TILE)
    nt_ = t // mt
    nc = len(tables)
    ng = len(to_gather)

    def body(*refs):
        proj_ref, ws_ref, bs_ref, lw_ref, lb_ref, lo_ref, gn_ref = refs[:7]
        c_refs = refs[7:7 + nc]
        refs = refs[7 + nc:]
        g_in = refs[:ng]
        y_ref, st_ref = refs[ng:2 + ng]
        g_out = refs[2 + ng:2 + 2 * ng]
        state = refs[2 + 2 * ng]
        ga = _Gather2(g_in, g_out, *refs[3 + 2 * ng:])

        @pl.when(pl.program_id(0) == 0)
        def _():
            state[...] = jnp.zeros_like(state)
            ga.start()

        @pl.when(pl.program_id(0) == (3 * nt_) // 4)
        def _():
            ga.forward()

        st = tuple(state[h] for h in range(NH))
        for h in range(NH):
            st_ref[h] = st[h]
        y, new = _mix_tile(proj_ref[...], st, ws_ref[...], bs_ref[...], lw_ref[...], lb_ref[...], lo_ref[...], gn_ref[...],
                           tuple(r[...] for r in c_refs))
        y_ref[...] = y.astype(BF)
        for h in range(NH):
            state[h] = new[h]

        @pl.when(pl.program_id(0) == nt_ - 1)
        def _():
            ga.finish()

    full = lambda a: pl.BlockSpec(a.shape, lambda i, nd=a.ndim: (0,) * nd)
    outs = pl.pallas_call(
        body, name="fwd_mix", grid=(nt_,),
        out_shape=(jax.ShapeDtypeStruct((t, D), BF), jax.ShapeDtypeStruct((nt_, NH, HD, HD), F32),
                   *_exchange_out_shapes(to_gather, False)),
        in_specs=[pl.BlockSpec((mt, D_IN), lambda i: (i, 0)), full(w_s), full(b_s_t), full(ln_w), full(ln_b), full(lower),
                  full(gn_w)] + [full(a) for a in tables] + [_any() for _ in to_gather],
        out_specs=(pl.BlockSpec((mt, D), lambda i: (i, 0)), pl.BlockSpec((None, NH, HD, HD), lambda i: (i, 0, 0, 0)),
                   *[_any() for _ in to_gather]),
        scratch_shapes=[pltpu.VMEM((NH, HD, HD), F32)] + _exchange_sems(ng),
        compiler_params=_params(dimension_semantics=("arbitrary",), has_side_effects=True),
    )(proj, w_s, b_s_t, ln_w, ln_b, lower, gn_w, *tables, *to_gather)
    return outs[0], outs[1], outs[2:]


def _bwd_mix(proj, dycat, states, w_s, b_s_t, ln_w, ln_b, lower, gn_w, tables, to_scatter):
    t = proj.shape[0]
    mt = _row_tile(t, MIX_TILE)
    nt_ = t // mt
    nc = len(tables)
    ns = len(to_scatter)

    def body(*refs):
        proj_ref, dy_ref, st_ref, ws_ref, bs_ref, lw_ref, lb_ref, lo_ref, gn_ref = refs[:9]
        c_refs = refs[9:9 + nc]
        refs = refs[9 + nc:]
        s_in = refs[:ns]
        dproj_ref = refs[ns]
        acc = refs[1 + ns:7 + ns]
        s_out = refs[7 + ns:7 + 2 * ns]
        dstate = refs[7 + 2 * ns]
        ex = _Exchange(s_in, s_out, *refs[8 + 2 * ns:], True)

        @pl.when(pl.program_id(0) == 0)
        def _():
            dstate[...] = jnp.zeros_like(dstate)
            for r in acc:
                r[...] = jnp.zeros_like(r)
            ex.start()

        consts = tuple(r[...] for r in c_refs)

        def f(p, s, ws, bs, lw, lb_, lo, gn):
            return _mix_tile(p, s, ws, bs, lw, lb_, lo, gn, consts)

        st = tuple(st_ref[h] for h in range(NH))
        _, vjp = jax.vjp(f, proj_ref[...], st, ws_ref[...], bs_ref[...], lw_ref[...], lb_ref[...], lo_ref[...], gn_ref[...])
        grads = vjp((dy_ref[...], tuple(dstate[h] for h in range(NH))))
        dproj_ref[...] = grads[0].astype(BF)
        for h in range(NH):
            dstate[h] = grads[1][h]
        for r, gval in zip(acc, grads[2:]):
            r[...] += gval

        @pl.when(pl.program_id(0) == nt_ - 1)
        def _():
            ex.wait()

    full = lambda a: pl.BlockSpec(a.shape, lambda i, nd=a.ndim: (0,) * nd)
    rev = lambda i: (nt_ - 1 - i, 0)
    smalls = (w_s, b_s_t, ln_w, ln_b, lower, gn_w)
    outs = pl.pallas_call(
        body, name="bwd_mix", grid=(nt_,),
        out_shape=(jax.ShapeDtypeStruct((t, D_IN), BF), *[jax.ShapeDtypeStruct(a.shape, F32) for a in smalls],
                   *_exchange_out_shapes(to_scatter, True)),
        in_specs=[pl.BlockSpec((mt, D_IN), rev), pl.BlockSpec((mt, D), rev),
                  pl.BlockSpec((None, NH, HD, HD), lambda i: (nt_ - 1 - i, 0, 0, 0))]
        + [full(a) for a in smalls] + [full(a) for a in tables] + [_any() for _ in to_scatter],
        out_specs=(pl.BlockSpec((mt, D_IN), rev), *[full(a) for a in smalls], *[_any() for _ in to_scatter]),
        scratch_shapes=[pltpu.VMEM((NH, HD, HD), F32)] + _exchange_sems(ns),
        compiler_params=_params(dimension_semantics=("arbitrary",), has_side_effects=True),
    )(proj, dycat, states, w_s, b_s_t, ln_w, ln_b, lower, gn_w, *tables, *to_scatter)
    return outs[0], outs[1:7], outs[7:]


def _row_tile(t, want):
    return want if t % want == 0 else t


def _rms(v):
    rstd = lax.rsqrt(jnp.mean(v * v, axis=-1, keepdims=True) + EPS)
    return v * rstd, rstd


def _rms_bwd(dxhat, xhat, rstd):
    return rstd * (dxhat - xhat * jnp.mean(dxhat * xhat, axis=-1, keepdims=True))


def _colsum(v):
    return jnp.sum(v, axis=0, keepdims=True)


def _fwd_in(x, ada, n1w, w_in_st):
    t = x.shape[0]
    tm = _row_tile(t, 512)

    def body(x_ref, ada_ref, n1_ref, wst_ref, proj_ref, h1_ref, w_ref, sems):
        @pl.when(pl.program_id(0) == 0)
        def _():
            _load_columns(wst_ref, w_ref, sems)

        xh, _ = _rms(x_ref[...])
        h1 = (xh * n1_ref[...]) * (1.0 + ada_ref[1:2, :]) + ada_ref[0:1, :]
        h1b = h1.astype(BF)
        h1_ref[...] = h1b
        proj_ref[...] = _mm(h1b, w_ref[...])

    full = lambda a: pl.BlockSpec(a.shape, lambda i, nd=a.ndim: (0,) * nd)
    return pl.pallas_call(
        body, name="fwd_in", grid=(t // tm,),
        out_shape=(jax.ShapeDtypeStruct((t, D_IN), F32), jax.ShapeDtypeStruct((t, D), BF)),
        in_specs=[pl.BlockSpec((tm, D), lambda i: (i, 0)), full(ada), full(n1w), _any()],
        out_specs=(pl.BlockSpec((tm, D_IN), lambda i: (i, 0)), pl.BlockSpec((tm, D), lambda i: (i, 0))),
        scratch_shapes=[pltpu.VMEM((D, D_IN), BF), pltpu.SemaphoreType.DMA((N_DEV,))],
        compiler_params=_params(dimension_semantics=("arbitrary",)),
    )(x, ada, n1w, w_in_st)


def _fwd_ffn(x, ycat, tgt, ada, n2w, fw, w_out, w_fi, w_fo):
    t = x.shape[0]
    tm = _row_tile(t, 256)

    def body(x_ref, y_ref, t_ref, ada_ref, n2_ref, fw_ref, wo_ref, wi_ref, wf_ref,
             x1_ref, mix_ref, h2_ref, act_ref, gu_ref, dffn_ref, dx2_ref, part_ref):
        @pl.when(pl.program_id(0) == 0)
        def _():
            part_ref[...] = jnp.zeros_like(part_ref)

        g1, sh2, sc2, g2 = ada_ref[2:3, :], ada_ref[3:4, :], ada_ref[4:5, :], ada_ref[5:6, :]
        mix = _mm(y_ref[...], wo_ref[...])
        mix_ref[...] = mix.astype(BF)
        x1 = x_ref[...] + g1 * mix
        x1_ref[...] = x1
        xh2, _ = _rms(x1)
        h2b = ((xh2 * n2_ref[...]) * (1.0 + sc2) + sh2).astype(BF)
        h2_ref[...] = h2b
        ffn = jnp.zeros((tm, D), F32)
        for lo, hi in FF_CHUNKS:
            gate = _nt(h2b, wi_ref[lo:hi, :])
            up = _nt(h2b, wi_ref[D_FF + lo:D_FF + hi, :])
            gu_ref[:, lo:hi] = gate.astype(BF)
            gu_ref[:, D_FF + lo:D_FF + hi] = up.astype(BF)
            a = (_silu(gate) * up).astype(BF)
            act_ref[:, lo:hi] = a
            ffn = ffn + _mm(a, wf_ref[lo:hi, :])
        x2 = x1 + g2 * ffn
        xh3, rstd3 = _rms(x2)
        err = xh3 * fw_ref[...] - t_ref[...]
        dy = err * (1.0 / D)
        dx2 = _rms_bwd(dy * fw_ref[...], xh3, rstd3)
        dx2_ref[...] = dx2
        dffn_ref[...] = (g2 * dx2).astype(BF)
        part_ref[0:1, :] += _colsum(dx2 * ffn)
        part_ref[1:2, :] += _colsum(dy * xh3)
        part_ref[2:3, :] += jnp.zeros((1, D), F32) + (0.5 / D) * jnp.sum(err * err)

    full = lambda a: pl.BlockSpec(a.shape, lambda i, nd=a.ndim: (0,) * nd)
    row = lambda w: pl.BlockSpec((tm, w), lambda i: (i, 0))
    return pl.pallas_call(
        body, name="fwd_ffn", grid=(t // tm,),
        out_shape=(jax.ShapeDtypeStruct((t, D), F32), jax.ShapeDtypeStruct((t, D), BF), jax.ShapeDtypeStruct((t, D), BF),
                   jax.ShapeDtypeStruct((t, D_FF), BF), jax.ShapeDtypeStruct((t, 2 * D_FF), BF),
                   jax.ShapeDtypeStruct((t, D), BF), jax.ShapeDtypeStruct((t, D), F32), jax.ShapeDtypeStruct((8, D), F32)),
        in_specs=[row(D), row(D), row(D), full(ada), full(n2w), full(fw), _whole_vmem(), _whole_vmem(), _whole_vmem()],
        out_specs=(row(D), row(D), row(D), row(D_FF), row(2 * D_FF), row(D), row(D),
                   pl.BlockSpec((8, D), lambda i: (0, 0))),
        compiler_params=_params(dimension_semantics=("arbitrary",)),
    )(x, ycat, tgt, ada, n2w, fw, w_out, w_fi, w_fo)


def _bwd_ffn(x1, gu, dffn, dx2, mix, ada, n2w, w_out, w_fi, w_fo):
    t = x1.shape[0]
    tm = _row_tile(t, 256)

    def body(x1_ref, gu_ref, dffn_ref, dx2_ref, mix_ref, ada_ref, n2_ref, wo_ref, wi_ref, wf_ref,
             dgu_ref, dx1_ref, dmix_ref, dycat_ref, part_ref):
        @pl.when(pl.program_id(0) == 0)
        def _():
            part_ref[...] = jnp.zeros_like(part_ref)

        g1, sc2 = ada_ref[2:3, :], ada_ref[4:5, :]
        dffn = dffn_ref[...]
        dh2 = jnp.zeros((tm, D), F32)
        for lo, hi in FF_CHUNKS:
            gate = gu_ref[:, lo:hi].astype(F32)
            up = gu_ref[:, D_FF + lo:D_FF + hi].astype(F32)
            dact = _nt(dffn, wf_ref[lo:hi, :])
            sg = _sig(gate)
            dgate = (dact * up * (sg * (1.0 + gate * (1.0 - sg)))).astype(BF)
            dup = (dact * (gate * sg)).astype(BF)
            dgu_ref[:, lo:hi] = dgate
            dgu_ref[:, D_FF + lo:D_FF + hi] = dup
            dh2 = dh2 + _mm(dgate, wi_ref[lo:hi, :]) + _mm(dup, wi_ref[D_FF + lo:D_FF + hi, :])
        x1 = x1_ref[...]
        xh2, rstd2 = _rms(x1)
        xn2 = xh2 * n2_ref[...]
        dxn2 = dh2 * (1.0 + sc2)
        dx1 = dx2_ref[...] + _rms_bwd(dxn2 * n2_ref[...], xh2, rstd2)
        dx1_ref[...] = dx1
        dmix = (g1 * dx1).astype(BF)
        dmix_ref[...] = dmix
        dycat_ref[...] = _nt(dmix, wo_ref[...])
        part_ref[0:1, :] += _colsum(dh2)
        part_ref[1:2, :] += _colsum(dh2 * xn2)
        part_ref[2:3, :] += _colsum(dxn2 * xh2)
        part_ref[3:4, :] += _colsum(dx1 * mix_ref[...].astype(F32))

    full = lambda a: pl.BlockSpec(a.shape, lambda i, nd=a.ndim: (0,) * nd)
    row = lambda w: pl.BlockSpec((tm, w), lambda i: (i, 0))
    return pl.pallas_call(
        body, name="bwd_ffn", grid=(t // tm,),
        out_shape=(jax.ShapeDtypeStruct((t, 2 * D_FF), BF), jax.ShapeDtypeStruct((t, D), F32),
                   jax.ShapeDtypeStruct((t, D), BF), jax.ShapeDtypeStruct((t, D), F32), jax.ShapeDtypeStruct((8, D), F32)),
        in_specs=[row(D), row(2 * D_FF), row(D), row(D), row(D), full(ada), full(n2w),
                  _whole_vmem(), _whole_vmem(), _whole_vmem()],
        out_specs=(row(2 * D_FF), row(D), row(D), row(D), pl.BlockSpec((8, D), lambda i: (0, 0))),
        compiler_params=_params(dimension_semantics=("arbitrary",)),
    )(x1, gu, dffn, dx2, mix, ada, n2w, w_out, w_fi, w_fo)


def _bwd_in(x, dproj, dx1, ada, n1w, w_in_st, to_scatter, to_gather):
    t = x.shape[0]
    tm = _row_tile(t, 1024)
    n_t = t // tm
    ns = len(to_scatter)
    ng = len(to_gather)
    nx = ns + ng

    def body(*refs):
        x_ref, dp_ref, dx1_ref, ada_ref, n1_ref, wst_ref = refs[:6]
        x_in = refs[6:6 + nx]
        gx_ref, part_ref = refs[6 + nx:8 + nx]
        x_out = refs[8 + nx:8 + 2 * nx]
        w_ref, sems = refs[8 + 2 * nx:10 + 2 * nx]
        sem_refs = refs[10 + 2 * nx:]
        ex = _Exchange(x_in[:ns], x_out[:ns], *sem_refs[:3], True)
        gx = _Exchange(x_in[ns:], x_out[ns:], *sem_refs[3:], False) if ng else None

        @pl.when(pl.program_id(0) == 0)
        def _():
            ex.start()
            if ng:
                gx.start()
            part_ref[...] = jnp.zeros_like(part_ref)
            _load_columns(wst_ref, w_ref, sems)

        dh1 = _nt(dp_ref[...], w_ref[...])
        xh, rstd = _rms(x_ref[...])
        xn = xh * n1_ref[...]
        dxn = dh1 * (1.0 + ada_ref[1:2, :])
        gx_ref[...] = dx1_ref[...] + _rms_bwd(dxn * n1_ref[...], xh, rstd)
        part_ref[0:1, :] += _colsum(dh1)
        part_ref[1:2, :] += _colsum(dh1 * xn)
        part_ref[2:3, :] += _colsum(dxn * xh)

        @pl.when(pl.program_id(0) == n_t - 1)
        def _():
            ex.wait()
            if ng:
                gx.wait()

    full = lambda a: pl.BlockSpec(a.shape, lambda i, nd=a.ndim: (0,) * nd)
    row = lambda w: pl.BlockSpec((tm, w), lambda i: (i, 0))
    outs = pl.pallas_call(
        body, name="bwd_in", grid=(n_t,),
        out_shape=(jax.ShapeDtypeStruct((t, D), F32), jax.ShapeDtypeStruct((8, D), F32),
                   *_exchange_out_shapes(to_scatter, True), *_exchange_out_shapes(to_gather, False)),
        in_specs=[row(D), row(D_IN), row(D), full(ada), full(n1w), _any()] + [_any() for _ in range(nx)],
        out_specs=(row(D), pl.BlockSpec((8, D), lambda i: (0, 0)), *[_any() for _ in range(nx)]),
        scratch_shapes=[pltpu.VMEM((D, D_IN), BF), pltpu.SemaphoreType.DMA((N_DEV,))] + _exchange_sems(ns)
        + (_exchange_sems(ng) if ng else []),
        compiler_params=_params(dimension_semantics=("arbitrary",), has_side_effects=True),
    )(x, dproj, dx1, ada, n1w, w_in_st, *to_scatter, *to_gather)
    return outs[0], outs[1], outs[2:2 + ns], outs[2 + ns:]


def _wgrad(a, b, name, a_spec, b_spec, out_shape, out_spec, grid, acc_shape, split=1, to_gather=()):
    n_j, n_t = grid
    ng = len(to_gather)

    def body(*refs):
        a_ref, b_ref = refs[:2]
        g_in = refs[2:2 + ng]
        o_ref = refs[2 + ng]
        g_out = refs[3 + ng:3 + 2 * ng]
        acc = refs[3 + 2 * ng]
        first = (pl.program_id(0) == 0) & (pl.program_id(1) == 0)
        last = (pl.program_id(0) == n_j - 1) & (pl.program_id(1) == n_t - 1)
        if ng:
            gx = _Exchange(g_in, g_out, *refs[4 + 2 * ng:], False)

            @pl.when(first)
            def _():
                gx.start()

        @pl.when(pl.program_id(1) == 0)
        def _():
            acc[...] = jnp.zeros_like(acc)

        acc[...] += _tn(a_ref[...], b_ref[...])

        @pl.when(pl.program_id(1) == n_t - 1)
        def _():
            if split == 1:
                o_ref[...] = acc[...].astype(BF)
            else:
                w = acc_shape[1] // split
                for s in range(split):
                    o_ref[s] = acc[:, s * w:(s + 1) * w].astype(BF)

        if ng:
            @pl.when(last)
            def _():
                gx.wait()

    outs = pl.pallas_call(
        body, name=name, grid=grid,
        out_shape=(jax.ShapeDtypeStruct(out_shape, BF), *_exchange_out_shapes(to_gather, False)),
        in_specs=[a_spec, b_spec] + [_any() for _ in to_gather], out_specs=(out_spec, *[_any() for _ in to_gather]),
        scratch_shapes=[pltpu.VMEM(acc_shape, F32)] + (_exchange_sems(ng) if ng else []),
        compiler_params=_params(dimension_semantics=("arbitrary", "arbitrary"), has_side_effects=bool(ng)),
    )(a, b, *to_gather)
    return (outs[0], outs[1:]) if ng else outs[0]


def _adamw_math(w, g, m, v):
    m = ADAM_B1 * m + (1.0 - ADAM_B1) * g
    v = ADAM_B2 * v + (1.0 - ADAM_B2) * (g * g)
    m_hat = m / (1.0 - ADAM_B1 ** ADAM_STEP)
    v_hat = v / (1.0 - ADAM_B2 ** ADAM_STEP)
    delta = -ADAM_LR * (m_hat / (jnp.sqrt(v_hat) + ADAM_EPS) + ADAM_WD * w)
    return delta, m, v


def _adamw_recv(w, m, v, recv, name, tr):
    r, c = w.shape

    def body(w_ref, m_ref, v_ref, r_ref, g_ref, d_ref, nm_ref, nv_ref):
        g = r_ref[0].astype(F32)
        for k in range(1, N_DEV):
            g = g + r_ref[k].astype(F32)
        g_ref[...] = g
        d_ref[...], nm_ref[...], nv_ref[...] = _adamw_math(w_ref[...], g, m_ref[...], v_ref[...])

    row = pl.BlockSpec((tr, c), lambda i: (i, 0))
    return pl.pallas_call(
        body, name=name, grid=(r // tr,),
        out_shape=tuple(jax.ShapeDtypeStruct((r, c), F32) for _ in range(4)),
        in_specs=[row, row, row, pl.BlockSpec((N_DEV, tr, c), lambda i: (0, i, 0))],
        out_specs=(row, row, row, row),
        compiler_params=_params(dimension_semantics=("arbitrary",)),
    )(w, m, v, recv)


def _adamw_ada(w, m, v, cact, dada_cols):
    r, c = w.shape
    tr = 256

    def body(w_ref, m_ref, v_ref, ca_ref, da_ref, g_ref, d_ref, nm_ref, nv_ref):
        g = _tn(ca_ref[...].astype(BF), da_ref[...].astype(BF))
        g_ref[...] = g
        d_ref[...], nm_ref[...], nv_ref[...] = _adamw_math(w_ref[...], g, m_ref[...], v_ref[...])

    row = pl.BlockSpec((tr, c), lambda i: (i, 0))
    return pl.pallas_call(
        body, name="adamw_ada", grid=(r // tr,),
        out_shape=tuple(jax.ShapeDtypeStruct((r, c), F32) for _ in range(4)),
        in_specs=[row, row, row, pl.BlockSpec((N_DEV, tr), lambda i: (0, i)), pl.BlockSpec(dada_cols.shape, lambda i: (0, 0))],
        out_specs=(row, row, row, row),
        compiler_params=_params(dimension_semantics=("arbitrary",)),
    )(w, m, v, cact, dada_cols)


def _adamw_small(gathered, wmv):
    n_g = len(gathered)
    n_p = len(wmv)
    flat = [a for trip in wmv for a in trip]

    def body(*refs):
        g_refs = refs[:n_g]
        p_refs = refs[n_g:n_g + 3 * n_p]
        o_refs = refs[n_g + 3 * n_p:]

        def total(ref):
            s = ref[0]
            for k in range(1, N_DEV):
                s = s + ref[k]
            return s

        f3, b3, b1, dws, dbs, dlnw, dlnb, dlo, dgn = [total(r) for r in g_refs]
        dada_rows = [b1[0:1], b1[1:2], b3[3:4], b3[0:1], b3[1:2], f3[0:1]]
        for r, g in enumerate(dada_rows):
            cs = slice(r * D, (r + 1) * D)
            w, m, v = p_refs[0][:, cs], p_refs[1][:, cs], p_refs[2][:, cs]
            o_refs[0][:, cs] = g
            o_refs[1][:, cs], o_refs[2][:, cs], o_refs[3][:, cs] = _adamw_math(w, g, m, v)
        grads = [None, b1[2:3], dws, dbs, dlnw, dlnb, dlo, dgn, b3[2:3], f3[1:2]]
        for i, g in enumerate(grads):
            if g is None:
                continue
            w, m, v = p_refs[3 * i][...], p_refs[3 * i + 1][...], p_refs[3 * i + 2][...]
            o_refs[4 * i][...] = g
            o_refs[4 * i + 1][...], o_refs[4 * i + 2][...], o_refs[4 * i + 3][...] = _adamw_math(w, g, m, v)
        o_refs[4 * n_p][...] = jnp.zeros((8, 128), F32) + f3[2:3, 0:128]

    out_shape = []
    for w, _, _ in wmv:
        out_shape += [jax.ShapeDtypeStruct(w.shape, F32)] * 4
    out_shape.append(jax.ShapeDtypeStruct((8, 128), F32))
    n_in = n_g + 3 * n_p
    return pl.pallas_call(
        body, name="adamw_small",
        out_shape=tuple(out_shape),
        in_specs=[_whole_vmem()] * n_in, out_specs=tuple(_whole_vmem() for _ in out_shape),
        compiler_params=_params(),
    )(*gathered, *flat)


def kernel(x, c, w_ada, b_ada, norm1_w, w_in, w_s, b_s, v_ln_w, v_ln_b, lower_bounds, gn_w, w_out, norm2_w, w_ffn_in, w_ffn_out, final_norm_w, loss_target, m_w_ada, m_b_ada, m_norm1_w, m_w_in, m_w_s, m_b_s, m_v_ln_w, m_v_ln_b, m_lower_bounds, m_gn_w, m_w_out, m_norm2_w, m_w_ffn_in, m_w_ffn_out, m_final_norm_w, v_w_ada, v_b_ada, v_norm1_w, v_w_in, v_w_s, v_b_s, v_v_ln_w, v_v_ln_b, v_lower_bounds, v_gn_w, v_w_out, v_norm2_w, v_w_ffn_in, v_w_ffn_out, v_final_norm_w):
    me = 4 * lax.axis_index("x") + 2 * lax.axis_index("y") + lax.axis_index("c")
    t = x.shape[1]
    x2d = x.reshape(t, D)
    tgt = loss_target.reshape(t, D)
    ada_cols = w_ada.shape[2]

    tp = lambda a: jnp.swapaxes(a[0], 0, 1)
    win_b, wout_b, wfi_b, wfo_b = _cast_bf16([w_in[0], w_out[0], tp(w_ffn_in), w_ffn_out[0]], "cast_weights")

    b_cols = lax.dynamic_slice(b_ada, (0, me * ada_cols), (1, ada_cols))
    win_st, ada_st, cact = _startup(win_b, c, w_ada[0], b_cols)
    ada = lax.dynamic_index_in_dim(ada_st, me, axis=1, keepdims=False).reshape(6, D)

    tables = _decay_tables()
    ws3 = w_s[0]
    bs_t = b_s[0].T

    proj, h1 = _fwd_in(x2d, ada, norm1_w, win_st)
    ycat, states, (wout_st, wfi_st, wfo_st) = _fwd_mix(proj, ws3, bs_t, v_ln_w, v_ln_b, lower_bounds, gn_w, tables,
                                                       [wout_b, wfi_b, wfo_b])
    w_out_full = wout_st.reshape(D, D)
    w_fo_full = wfo_st.reshape(D_FF, D)
    w_fi_full = wfi_st.reshape(2 * D_FF, D)
    x1, mixb, h2, act, gu, dffn, dx2, part_f = _fwd_ffn(x2d, ycat, tgt, ada, norm2_w, final_norm_w.reshape(1, D),
                                                        w_out_full, w_fi_full, w_fo_full)
    dgu, dx1, dmix, dycat, part_b3 = _bwd_ffn(x1, gu, dffn, dx2, mixb, ada, norm2_w, w_out_full, w_fi_full, w_fo_full)
    tk = _row_tile(t, 2048)
    n_t = t // tk
    win_cols = D_IN // N_DEV
    dwout = _wgrad(ycat, dmix, "wgrad_out",
                   pl.BlockSpec((tk, D), lambda j, i: (i, 0)), pl.BlockSpec((tk, D), lambda j, i: (i, 0)),
                   (D, D), pl.BlockSpec((D, D), lambda j, i: (0, 0)), (1, n_t), (D, D))
    dwfi = _wgrad(dgu, h2, "wgrad_ffn_in",
                  pl.BlockSpec((tk, FF_PAIR), lambda j, i: (i, j)), pl.BlockSpec((tk, D), lambda j, i: (i, 0)),
                  (4, FF_PAIR, D), pl.BlockSpec((None, FF_PAIR, D), lambda j, i: (j, 0, 0)), (4, n_t), (FF_PAIR, D))
    dwfo = _wgrad(act, dffn, "wgrad_ffn_out",
                  pl.BlockSpec((tk, FF_PAIR), lambda j, i: (i, j)), pl.BlockSpec((tk, D), lambda j, i: (i, 0)),
                  (2, FF_PAIR, D), pl.BlockSpec((None, FF_PAIR, D), lambda j, i: (j, 0, 0)), (2, n_t), (FF_PAIR, D))
    dproj, (dws, dbs_t, dlnw, dlnb, dlower, dgnw), (r_out, r_fi, r_fo) = _bwd_mix(
        proj, dycat, states, ws3, bs_t, v_ln_w, v_ln_b, lower_bounds, gn_w, tables,
        [dwout.reshape(N_DEV, D // N_DEV, D), dwfi.reshape(N_DEV, FF_BLK, D), dwfo.reshape(N_DEV, D_FF // N_DEV, D)])
    dwin, early = _wgrad(h1, dproj, "wgrad_in",
                         pl.BlockSpec((tk, D), lambda j, i: (i, 0)), pl.BlockSpec((tk, 4 * win_cols), lambda j, i: (i, j)),
                         (N_DEV, D, win_cols), pl.BlockSpec((4, D, win_cols), lambda j, i: (j, 0, 0)), (N_DEV // 4, n_t),
                         (D, 4 * win_cols), split=4, to_gather=[part_f, part_b3, dws, dbs_t, dlnw, dlnb, dlower, dgnw])
    grad_x, part_b1, (r_in,), _ = _bwd_in(x2d, dproj, dx1, ada, norm1_w, win_st, [dwin], [])

    g_w_in, d_w_in, nm_w_in, nv_w_in = _adamw_recv(w_in[0], m_w_in[0], v_w_in[0], r_in, "adamw_w_in", 256)
    g_w_out, d_w_out, nm_w_out, nv_w_out = _adamw_recv(w_out[0], m_w_out[0], v_w_out[0], r_out, "adamw_w_out", 128)
    fi_t = _adamw_recv(tp(w_ffn_in), tp(m_w_ffn_in), tp(v_w_ffn_in), r_fi, "adamw_w_ffn_in", 176)
    g_w_fi, d_w_fi, nm_w_fi, nv_w_fi = [jnp.swapaxes(a, 0, 1) for a in fi_t]
    g_w_fo, d_w_fo, nm_w_fo, nv_w_fo = _adamw_recv(w_ffn_out[0], m_w_ffn_out[0], v_w_ffn_out[0], r_fo, "adamw_w_ffn_out", 176)

    (b1_all,) = _exchange([part_b1], "gather_small", False, True)
    gathered = [early[0], early[1], b1_all, *early[2:]]
    f3_all, b3_all = gathered[0], gathered[1]
    dada_all = jnp.stack([b1_all[:, 0], b1_all[:, 1], b3_all[:, 3], b3_all[:, 0], b3_all[:, 1], f3_all[:, 0]], axis=1)
    dada_cols = lax.dynamic_slice(dada_all.reshape(N_DEV, 6 * D), (0, me * ada_cols), (N_DEV, ada_cols))
    g_w_ada, d_w_ada, nm_w_ada, nv_w_ada = _adamw_ada(w_ada[0], m_w_ada[0], v_w_ada[0], cact, dada_cols)

    r1 = lambda a: a.reshape(1, D)
    tr = lambda a: a[0].T
    wmv = [
        (b_ada, m_b_ada, v_b_ada),
        (norm1_w, m_norm1_w, v_norm1_w),
        (w_s[0], m_w_s[0], v_w_s[0]),
        (tr(b_s), tr(m_b_s), tr(v_b_s)),
        (v_ln_w, m_v_ln_w, v_v_ln_w),
        (v_ln_b, m_v_ln_b, v_v_ln_b),
        (lower_bounds, m_lower_bounds, v_lower_bounds),
        (gn_w, m_gn_w, v_gn_w),
        (norm2_w, m_norm2_w, v_norm2_w),
        (r1(final_norm_w), r1(m_final_norm_w), r1(v_final_norm_w)),
    ]
    small = _adamw_small(gathered, wmv)
    loss = small[-1][0, 0]

    def unshape(i, a):
        if i == 2:
            return a.reshape(1, NH, BLK, BLK)
        if i == 3:
            return a.T.reshape(1, NH, BLK)
        if i == 9:
            return a.reshape(D)
        return a

    def small_out(kind):
        return [unshape(i, small[4 * i + kind]) for i in range(len(wmv))]

    e3 = lambda a: a[None]
    big = {
        0: (e3(g_w_ada), e3(g_w_in), e3(g_w_out), e3(g_w_fi), e3(g_w_fo)),
        1: (e3(d_w_ada), e3(d_w_in), e3(d_w_out), e3(d_w_fi), e3(d_w_fo)),
        2: (e3(nm_w_ada), e3(nm_w_in), e3(nm_w_out), e3(nm_w_fi), e3(nm_w_fo)),
        3: (e3(nv_w_ada), e3(nv_w_in), e3(nv_w_out), e3(nv_w_fi), e3(nv_w_fo)),
    }

    def ordered(kind):
        s = small_out(kind)
        b_ = big[kind]
        return [b_[0], s[0], s[1], b_[1], s[2], s[3], s[4], s[5], s[6], s[7], b_[2], s[8], b_[3], b_[4], s[9]]

    return (loss, grad_x.reshape(1, t, D), *ordered(0), *ordered(1), *ordered(2), *ordered(3))
```

```python
import numpy as np
import jax
import jax.numpy as jnp
from jax import lax
from jax.experimental import pallas as pl
from jax.experimental.pallas import tpu as pltpu

F32 = jnp.float32
BF = jnp.bfloat16
MESH = pl.DeviceIdType.MESH

N_DEV = 8
D = 1024
D_IN = 3072
D_FF = 2816
FF_BLK = D_FF // 4
FF_CHUNKS = ((0, D_FF),)
FF_PAIR = 2 * FF_BLK
CH = 64
BLK = 128
MIX_TILE = 512
NH = 4
HD = 128
EPS = 1e-6
LEVELS = (32, 16, 8, 4, 2, 1)

ADAM_LR = 0.001
ADAM_B1 = 0.9
ADAM_B2 = 0.999
ADAM_EPS = 1e-08
ADAM_WD = 0.01
ADAM_STEP = 10

VMEM_LIMIT = 56 * 1024 * 1024


def _params(**kw):
    return pltpu.CompilerParams(vmem_limit_bytes=VMEM_LIMIT, **kw)


def _whole_vmem():
    return pl.BlockSpec(memory_space=pltpu.VMEM)


def _any():
    return pl.BlockSpec(memory_space=pl.ANY)


class _Exchange:
    def __init__(self, ins, outs, send_sems, recv_sems, local_sems, scatter):
        self.ins, self.outs, self.scatter = ins, outs, scatter
        self.send_sems, self.recv_sems, self.local_sems = send_sems, recv_sems, local_sems
        x, y, c = lax.axis_index("x"), lax.axis_index("y"), lax.axis_index("c")
        self.me = 4 * x + 2 * y + c
        self.peers = []
        for k in range(1, N_DEV):
            peer = (1 - x if (k >> 2) & 1 else x, 1 - y if (k >> 1) & 1 else y, 1 - c if k & 1 else c)
            self.peers.append((peer, 4 * peer[0] + 2 * peer[1] + peer[2]))

    def _src(self, a, idx):
        return self.ins[a].at[idx] if self.scatter else self.ins[a]

    def _local(self):
        return [pltpu.make_async_copy(self._src(a, self.me), self.outs[a].at[self.me], self.local_sems.at[a])
                for a in range(len(self.ins))]

    def _remote(self, a, k, dst_slot):
        peer, peer_idx = self.peers[k]
        return pltpu.make_async_remote_copy(
            src_ref=self._src(a, peer_idx), dst_ref=self.outs[a].at[dst_slot],
            send_sem=self.send_sems.at[a, k], recv_sem=self.recv_sems.at[a, k],
            device_id=peer, device_id_type=MESH)

    def start(self):
        for cp in self._local():
            cp.start()
        for k in range(N_DEV - 1):
            for a in range(len(self.ins)):
                self._remote(a, k, self.me).start()

    def wait(self):
        for k in range(N_DEV - 1):
            for a in range(len(self.ins)):
                self._remote(a, k, self.peers[k][1]).wait()
        for cp in self._local():
            cp.wait()


class _Gather2:
    def __init__(self, ins, outs, send_sems, recv_sems, local_sems):
        self.ins, self.outs = ins, outs
        self.send_sems, self.recv_sems, self.local_sems = send_sems, recv_sems, local_sems
        x, y, c = lax.axis_index("x"), lax.axis_index("y"), lax.axis_index("c")
        self.c = c
        self.me = 4 * x + 2 * y + c
        self.sibling = (x, y, 1 - c)
        self.chips = [(1 - x, y), (x, 1 - y), (1 - x, 1 - y)]

    @staticmethod
    def _idx(px, py, pc):
        return 4 * px + 2 * py + pc

    def _copy(self, a, k, slot, to, own):
        src = self.ins[a] if own else self.outs[a].at[slot]
        return pltpu.make_async_remote_copy(
            src_ref=src, dst_ref=self.outs[a].at[slot],
            send_sem=self.send_sems.at[a, k], recv_sem=self.recv_sems.at[a, k],
            device_id=to, device_id_type=MESH)

    def _local(self):
        return [pltpu.make_async_copy(self.ins[a], self.outs[a].at[self.me], self.local_sems.at[a])
                for a in range(len(self.ins))]

    def start(self):
        for cp in self._local():
            cp.start()
        for a in range(len(self.ins)):
            self._copy(a, 0, self.me, self.sibling, True).start()
            for j, chip in enumerate(self.chips):
                self._copy(a, 1 + j, self.me, (*chip, self.c), True).start()

    def forward(self):
        for j, chip in enumerate(self.chips):
            for a in range(len(self.ins)):
                slot = self._idx(*chip, self.c)
                self._copy(a, 1 + j, slot, (*chip, self.c), True).wait_recv()
                self._copy(a, 4 + j, slot, self.sibling, False).start()

    def finish(self):
        for a in range(len(self.ins)):
            self._copy(a, 0, self._idx(*self.sibling), self.sibling, True).wait_recv()
            for j, chip in enumerate(self.chips):
                self._copy(a, 4 + j, self._idx(*chip, 1 - self.c), self.sibling, False).wait_recv()
            self._copy(a, 0, self.me, self.sibling, True).wait_send()
            for j, chip in enumerate(self.chips):
                self._copy(a, 1 + j, self.me, (*chip, self.c), True).wait_send()
                self._copy(a, 4 + j, self._idx(*chip, self.c), self.sibling, False).wait_send()
        for cp in self._local():
            cp.wait()


def _exchange_sems(n):
    return [pltpu.SemaphoreType.DMA((n, N_DEV - 1)), pltpu.SemaphoreType.DMA((n, N_DEV - 1)), pltpu.SemaphoreType.DMA((n,))]


def _exchange_out_shapes(arrays, scatter):
    return [jax.ShapeDtypeStruct(a.shape if scatter else (N_DEV,) + a.shape, a.dtype) for a in arrays]


def _exchange(arrays, name, scatter, in_vmem):
    n = len(arrays)

    def body(*refs):
        ex = _Exchange(refs[:n], refs[n:2 * n], *refs[2 * n:], scatter)
        ex.start()
        ex.wait()

    spec = _whole_vmem if in_vmem else _any
    return pl.pallas_call(
        body, name=name,
        out_shape=tuple(_exchange_out_shapes(arrays, scatter)),
        in_specs=[spec() for _ in arrays],
        out_specs=tuple(spec() for _ in arrays),
        scratch_shapes=_exchange_sems(n),
        compiler_params=_params(has_side_effects=True),
    )(*arrays)


def _cast_bf16(arrays, name):
    n = len(arrays)

    def body(*refs):
        for i in range(n):
            refs[n + i][...] = refs[i][...].astype(BF)

    return pl.pallas_call(
        body, name=name,
        out_shape=tuple(jax.ShapeDtypeStruct(a.shape, BF) for a in arrays),
        in_specs=[_whole_vmem() for _ in arrays],
        out_specs=tuple(_whole_vmem() for _ in arrays),
        compiler_params=_params(),
    )(*arrays)


def _load_columns(stacked_ref, full_ref, sems):
    c = stacked_ref.shape[2]
    cps = [pltpu.make_async_copy(stacked_ref.at[d], full_ref.at[:, pl.ds(d * c, c)], sems.at[d]) for d in range(N_DEV)]
    for cp in cps:
        cp.start()
    for cp in cps:
        cp.wait()


def _sig(v):
    return 0.5 * jnp.tanh(0.5 * v) + 0.5


@jax.custom_vjp
def _silu(v):
    return v * _sig(v)


def _silu_fwd(v):
    return _silu(v), v


def _silu_bwd(v, g):
    s = _sig(v)
    return (g * (s * (1.0 + v * (1.0 - s))),)


_silu.defvjp(_silu_fwd, _silu_bwd)


@jax.custom_vjp
def _sigmoid_rel(v):
    e = jnp.exp(-jnp.abs(v))
    d = 1.0 + e
    r = pl.reciprocal(d, approx=True)
    r = r * (2.0 - d * r)
    r = r * (2.0 - d * r)
    return jnp.where(v >= 0.0, r, e * r)


def _sigmoid_rel_fwd(v):
    s = _sigmoid_rel(v)
    return s, s


def _sigmoid_rel_bwd(s, g):
    return (g * (s * (1.0 - s)),)


_sigmoid_rel.defvjp(_sigmoid_rel_fwd, _sigmoid_rel_bwd)


def _gelu(v):
    return 0.5 * v * (1.0 + lax.erf(v * 0.7071067811865476))


def _dot(a, b, ca, cb):
    return lax.dot_general(a, b, (((ca,), (cb,)), ((), ())), preferred_element_type=F32)


@jax.custom_vjp
def _mm(a, b):
    return _dot(a, b, 1, 0)


def _mm_fwd(a, b):
    return _dot(a, b, 1, 0), (a, b)


def _mm_bwd(res, g):
    a, b = res
    gb = g.astype(BF)
    return _dot(gb, b, 1, 1).astype(a.dtype), _dot(a, gb, 0, 0).astype(b.dtype)


_mm.defvjp(_mm_fwd, _mm_bwd)


@jax.custom_vjp
def _nt(a, b):
    return _dot(a, b, 1, 1)


def _nt_fwd(a, b):
    return _dot(a, b, 1, 1), (a, b)


def _nt_bwd(res, g):
    a, b = res
    gb = g.astype(BF)
    return _dot(gb, b, 1, 0).astype(a.dtype), _dot(gb, a, 0, 0).astype(b.dtype)


_nt.defvjp(_nt_fwd, _nt_bwd)


@jax.custom_vjp
def _tn(a, b):
    return _dot(a, b, 0, 0)


def _tn_fwd(a, b):
    return _dot(a, b, 0, 0), (a, b)


def _tn_bwd(res, g):
    a, b = res
    gb = g.astype(BF)
    return _dot(b, gb, 1, 1).astype(a.dtype), _dot(a, gb, 1, 0).astype(b.dtype)


_tn.defvjp(_tn_fwd, _tn_bwd)


def _make_cast_dot(ca, cb, da_dims, db_dims):
    @jax.custom_vjp
    def dot(a, b):
        return _dot(a.astype(BF), b.astype(BF), ca, cb)

    def fwd(a, b):
        ab, bb = a.astype(BF), b.astype(BF)
        return _dot(ab, bb, ca, cb), (ab, bb)

    def bwd(res, g):
        ops = {"a": res[0], "b": res[1], "g": g.astype(BF)}
        return (_dot(ops[da_dims[0]], ops[da_dims[1]], da_dims[2], da_dims[3]),
                _dot(ops[db_dims[0]], ops[db_dims[1]], db_dims[2], db_dims[3]))

    dot.defvjp(fwd, bwd)
    return dot


_mm_c = _make_cast_dot(1, 0, ("g", "b", 1, 1), ("a", "g", 0, 0))
_nt_c = _make_cast_dot(1, 1, ("g", "b", 1, 0), ("g", "a", 0, 0))
_tn_c = _make_cast_dot(0, 0, ("b", "g", 1, 1), ("a", "g", 1, 0))


def _startup(win_b, c, w_ada, b_cols):
    ncol = w_ada.shape[1]

    def body(win_ref, c_ref, w_ref, b_ref, winst_ref, adast_ref, cact_ref, call_ref, blk_ref,
             ws, wr, wl, cs, cr, cl, as_, ar, al):
        big = _Gather2([win_ref], [winst_ref], ws, wr, wl)
        big.start()
        gc = _Exchange([c_ref], [call_ref], cs, cr, cl, False)
        gc.start()
        gc.wait()
        ca = _silu(call_ref[...].reshape(N_DEV, D))
        cact_ref[...] = ca
        blk_ref[...] = _mm(ca.astype(BF), w_ref[...].astype(BF)) + b_ref[...]
        ga = _Exchange([blk_ref], [adast_ref], as_, ar, al, False)
        ga.start()
        ga.wait()
        big.forward()
        big.finish()

    return pl.pallas_call(
        body, name="startup",
        out_shape=(jax.ShapeDtypeStruct((N_DEV,) + win_b.shape, BF), jax.ShapeDtypeStruct((N_DEV, N_DEV, ncol), F32),
                   jax.ShapeDtypeStruct((N_DEV, D), F32)),
        in_specs=[_any(), _whole_vmem(), _whole_vmem(), _whole_vmem()],
        out_specs=(_any(), _whole_vmem(), _whole_vmem()),
        scratch_shapes=[pltpu.VMEM((N_DEV, 1, D), F32), pltpu.VMEM((N_DEV, ncol), F32)]
        + _exchange_sems(1) + _exchange_sems(1) + _exchange_sems(1),
        compiler_params=_params(has_side_effects=True),
    )(win_b, c, w_ada, b_cols)


def _decay_tables():
    t = np.arange(CH)
    tri = (t[None, :] <= t[:, None]).astype(np.float32)
    masks = []
    for h in LEVELS:
        m = (t // (2 * h)) * (2 * h) + h
        upper = t >= m
        same = (t[:, None] // (2 * h)) == (t[None, :] // (2 * h))
        masks.append(same & upper[:, None] & (~upper)[None, :])
    masks.append(np.eye(CH, dtype=bool))
    lv = [np.where((t % (2 * h)) >= h, 1.0, -1.0) for h in LEVELS[:4]]
    m4 = t % 4
    lv += [(m4 == 0) * 1.0, (m4 >= 2) * 1.0, (m4 == 3) * 1.0, (t % 2 == 1) * 1.0]
    lvl = np.broadcast_to(np.stack(lv)[:, :, None], (8, CH, D // 2)).astype(np.float32)
    cid = np.arange(BLK) // CH
    gmask = (cid[:, None] >= cid[None, :]).astype(np.float32)
    return (jnp.asarray(tri, BF), jnp.asarray(np.stack(masks).astype(np.float32)), jnp.asarray(gmask), jnp.asarray(lvl))


def _split2(v):
    v1 = v.astype(BF)
    return v1, (v - v1.astype(F32)).astype(BF)


@jax.custom_vjp
def _cumsum_mm(tri, v):
    p1, p2 = _split2(v)
    return _dot(tri, p1, 1, 0) + _dot(tri, p2, 1, 0)


def _cumsum_mm_fwd(tri, v):
    return _cumsum_mm(tri, v), tri


def _cumsum_mm_bwd(tri, g):
    p1, p2 = _split2(g)
    return jnp.zeros_like(tri), _dot(tri, p1, 0, 0) + _dot(tri, p2, 0, 0)


_cumsum_mm.defvjp(_cumsum_mm_fwd, _cumsum_mm_bwd)


def _make_row_roll(shift):
    @jax.custom_vjp
    def roll(x):
        return pltpu.roll(x, shift % CH, 0)

    def fwd(x):
        return roll(x), None

    def bwd(_, g):
        return (pltpu.roll(g, (-shift) % CH, 0),)

    roll.defvjp(fwd, bwd)
    return roll


_prev_row = _make_row_roll(1)
_next_row = _make_row_roll(-1)


def _mix_tile(proj, state, w_s, b_s_t, ln_w, ln_b, lower, gn_w, consts):
    tri, masks, gmask, lvl = consts
    mt = proj.shape[0]
    u = proj[:, 0:512]
    v = proj[:, 512:1024]
    q = proj[:, 1024:1536]
    fl = proj[:, 1536:2048]
    inp = proj[:, 2048:2560]
    g = proj[:, 2560:3072]

    ug = _gelu(u)
    vg = _gelu(v)
    mu = jnp.mean(vg, axis=-1, keepdims=True)
    vc = vg - mu
    var = jnp.mean(vc * vc, axis=-1, keepdims=True)
    vn = vc * lax.rsqrt(var + EPS) * ln_w + ln_b
    wsm = [w_s[h] * gmask for h in range(NH)]
    ya = [[None] * NH for _ in range(mt // BLK)]
    for bi in range(mt // BLK):
        rows = slice(bi * BLK, (bi + 1) * BLK)
        for h in range(NH):
            cols = slice(h * HD, (h + 1) * HD)
            ya[bi][h] = ug[rows, cols] * (_mm_c(wsm[h], vn[rows, cols]) + b_s_t[:, h:h + 1])
    ya_full = jnp.concatenate([jnp.concatenate(r, axis=1) for r in ya], axis=0)

    l0 = lower[0:1, :]
    l1 = lower[1:2, :]
    mx = jnp.maximum(l0, l1)
    e0 = jnp.exp(l0 - mx)
    e1 = jnp.exp(l1 - mx)
    lb = e0 / (e0 + e1)
    qf = _silu(q)
    f = lb + (1.0 - lb) * _sigmoid_rel(fl)
    logf = jnp.log(f)
    kk = 1.0 - f
    gate = _silu(g)
    nl = len(LEVELS)
    half = D // 2
    st = list(state)
    yb = [[None] * NH for _ in range(mt // CH)]
    for ci in range(mt // CH):
        rows = slice(ci * CH, (ci + 1) * CH)
        lc = logf[rows]
        b = _cumsum_mm(tri, lc)
        xb = jnp.exp(b)
        xinv = jnp.exp(b[CH - 1:CH, :] - b)
        xl = []
        for i, hs in enumerate(LEVELS[:3]):
            refs = [jnp.broadcast_to(b[r:r + 1, :], (2 * hs, half)) for r in range(hs - 1, CH, 2 * hs)]
            bref = refs[0] if len(refs) == 1 else jnp.concatenate(refs, axis=0)
            xl.append(jnp.exp(lvl[i] * (b - bref)))
        b3 = b.reshape(CH // 8, 8, half)
        bref = jnp.broadcast_to(b3[:, 3:4, :], (CH // 8, 8, half)).reshape(CH, half)
        xl.append(jnp.exp(lvl[3] * (b - bref)))
        xl.append(jnp.exp(lvl[4] * _next_row(lc) + lvl[5] * lc + lvl[6] * _prev_row(lc)))
        xl.append(jnp.exp(lvl[7] * lc))
        qc = qf[rows]
        kc = kk[rows]
        zsrc = [jnp.concatenate([(qc if (r0 // hs) % 2 == 1 else kc)[r0:r0 + hs] for r0 in range(0, CH, hs)], axis=0)
                for hs in LEVELS[:3]]
        for h in range(NH):
            cols = slice(h * HD, (h + 1) * HD)
            qh = qc[:, cols]
            kh = kc[:, cols]
            vh = inp[rows, cols]
            inter = _nt_c(qh * xb[:, cols], st[h])
            attn = masks[nl] * _nt_c(qh, kh)
            for li in range(nl):
                if li < 3:
                    z = zsrc[li][:, cols] * xl[li][:, cols]
                    pairs = _nt_c(z, z)
                else:
                    xx = xl[li][:, cols]
                    pairs = _nt_c(qh * xx, kh * xx)
                attn = attn + masks[li] * pairs
            o = inter + _mm_c(attn, vh)
            st[h] = st[h] * xb[CH - 1:CH, cols] + _tn_c(vh, kh * xinv[:, cols])
            rs = lax.rsqrt(jnp.mean(o * o, axis=-1, keepdims=True) + EPS)
            yb[ci][h] = o * rs * gn_w * gate[rows, cols]
    yb_full = jnp.concatenate([jnp.concatenate(r, axis=1) for r in yb], axis=0)
    return jnp.concatenate([ya_full, yb_full], axis=1), tuple(st)


def _fwd_mix(proj, w_s, b_s_t, ln_w, ln_b, lower, gn_w, tables, to_gather):
    t = proj.shape[0]
    mt = _row_tile(t, MIX_TILE)
    nt_ = t // mt
    nc = len(tables)
    ng = len(to_gather)

    def body(*refs):
        proj_ref, ws_ref, bs_ref, lw_ref, lb_ref, lo_ref, gn_ref = refs[:7]
        c_refs = refs[7:7 + nc]
        refs = refs[7 + nc:]
        g_in = refs[:ng]
        y_ref, st_ref = refs[ng:2 + ng]
        g_out = refs[2 + ng:2 + 2 * ng]
        state = refs[2 + 2 * ng]
        ga = _Gather2(g_in, g_out, *refs[3 + 2 * ng:])

        @pl.when(pl.program_id(0) == 0)
        def _():
            state[...] = jnp.zeros_like(state)
            ga.start()

        @pl.when(pl.program_id(0) == (3 * nt_) // 4)
        def _():
            ga.forward()

        st = tuple(state[h] for h in range(NH))
        for h in range(NH):
            st_ref[h] = st[h]
        y, new = _mix_tile(proj_ref[...], st, ws_ref[...], bs_ref[...], lw_ref[...], lb_ref[...], lo_ref[...], gn_ref[...],
                           tuple(r[...] for r in c_refs))
        y_ref[...] = y.astype(BF)
        for h in range(NH):
            state[h] = new[h]

        @pl.when(pl.program_id(0) == nt_ - 1)
        def _():
            ga.finish()

    full = lambda a: pl.BlockSpec(a.shape, lambda i, nd=a.ndim: (0,) * nd)
    outs = pl.pallas_call(
        body, name="fwd_mix", grid=(nt_,),
        out_shape=(jax.ShapeDtypeStruct((t, D), BF), jax.ShapeDtypeStruct((nt_, NH, HD, HD), F32),
                   *_exchange_out_shapes(to_gather, False)),
        in_specs=[pl.BlockSpec((mt, D_IN), lambda i: (i, 0)), full(w_s), full(b_s_t), full(ln_w), full(ln_b), full(lower),
                  full(gn_w)] + [full(a) for a in tables] + [_any() for _ in to_gather],
        out_specs=(pl.BlockSpec((mt, D), lambda i: (i, 0)), pl.BlockSpec((None, NH, HD, HD), lambda i: (i, 0, 0, 0)),
                   *[_any() for _ in to_gather]),
        scratch_shapes=[pltpu.VMEM((NH, HD, HD), F32)] + _exchange_sems(ng),
        compiler_params=_params(dimension_semantics=("arbitrary",), has_side_effects=True),
    )(proj, w_s, b_s_t, ln_w, ln_b, lower, gn_w, *tables, *to_gather)
    return outs[0], outs[1], outs[2:]


def _bwd_mix(proj, dycat, states, w_s, b_s_t, ln_w, ln_b, lower, gn_w, tables, to_scatter):
    t = proj.shape[0]
    mt = _row_tile(t, MIX_TILE)
    nt_ = t // mt
    nc = len(tables)
    ns = len(to_scatter)

    def body(*refs):
        proj_ref, dy_ref, st_ref, ws_ref, bs_ref, lw_ref, lb_ref, lo_ref, gn_ref = refs[:9]
        c_refs = refs[9:9 + nc]
        refs = refs[9 + nc:]
        s_in = refs[:ns]
        dproj_ref = refs[ns]
        acc = refs[1 + ns:7 + ns]
        s_out = refs[7 + ns:7 + 2 * ns]
        dstate = refs[7 + 2 * ns]
        ex = _Exchange(s_in, s_out, *refs[8 + 2 * ns:], True)

        @pl.when(pl.program_id(0) == 0)
        def _():
            dstate[...] = jnp.zeros_like(dstate)
            for r in acc:
                r[...] = jnp.zeros_like(r)
            ex.start()

        consts = tuple(r[...] for r in c_refs)

        def f(p, s, ws, bs, lw, lb_, lo, gn):
            return _mix_tile(p, s, ws, bs, lw, lb_, lo, gn, consts)

        st = tuple(st_ref[h] for h in range(NH))
        _, vjp = jax.vjp(f, proj_ref[...], st, ws_ref[...], bs_ref[...], lw_ref[...], lb_ref[...], lo_ref[...], gn_ref[...])
        grads = vjp((dy_ref[...], tuple(dstate[h] for h in range(NH))))
        dproj_ref[...] = grads[0].astype(BF)
        for h in range(NH):
            dstate[h] = grads[1][h]
        for r, gval in zip(acc, grads[2:]):
            r[...] += gval

        @pl.when(pl.program_id(0) == nt_ - 1)
        def _():
            ex.wait()

    full = lambda a: pl.BlockSpec(a.shape, lambda i, nd=a.ndim: (0,) * nd)
    rev = lambda i: (nt_ - 1 - i, 0)
    smalls = (w_s, b_s_t, ln_w, ln_b, lower, gn_w)
    outs = pl.pallas_call(
        body, name="bwd_mix", grid=(nt_,),
        out_shape=(jax.ShapeDtypeStruct((t, D_IN), BF), *[jax.ShapeDtypeStruct(a.shape, F32) for a in smalls],
                   *_exchange_out_shapes(to_scatter, True)),
        in_specs=[pl.BlockSpec((mt, D_IN), rev), pl.BlockSpec((mt, D), rev),
                  pl.BlockSpec((None, NH, HD, HD), lambda i: (nt_ - 1 - i, 0, 0, 0))]
        + [full(a) for a in smalls] + [full(a) for a in tables] + [_any() for _ in to_scatter],
        out_specs=(pl.BlockSpec((mt, D_IN), rev), *[full(a) for a in smalls], *[_any() for _ in to_scatter]),
        scratch_shapes=[pltpu.VMEM((NH, HD, HD), F32)] + _exchange_sems(ns),
        compiler_params=_params(dimension_semantics=("arbitrary",), has_side_effects=True),
    )(proj, dycat, states, w_s, b_s_t, ln_w, ln_b, lower, gn_w, *tables, *to_scatter)
    return outs[0], outs[1:7], outs[7:]


def _row_tile(t, want):
    return want if t % want == 0 else t


def _rms(v):
    rstd = lax.rsqrt(jnp.mean(v * v, axis=-1, keepdims=True) + EPS)
    return v * rstd, rstd


def _rms_bwd(dxhat, xhat, rstd):
    return rstd * (dxhat - xhat * jnp.mean(dxhat * xhat, axis=-1, keepdims=True))


def _colsum(v):
    return jnp.sum(v, axis=0, keepdims=True)


def _fwd_in(x, ada, n1w, w_in_st):
    t = x.shape[0]
    tm = _row_tile(t, 512)

    def body(x_ref, ada_ref, n1_ref, wst_ref, proj_ref, h1_ref, w_ref, sems):
        @pl.when(pl.program_id(0) == 0)
        def _():
            _load_columns(wst_ref, w_ref, sems)

        xh, _ = _rms(x_ref[...])
        h1 = (xh * n1_ref[...]) * (1.0 + ada_ref[1:2, :]) + ada_ref[0:1, :]
        h1b = h1.astype(BF)
        h1_ref[...] = h1b
        proj_ref[...] = _mm(h1b, w_ref[...])

    full = lambda a: pl.BlockSpec(a.shape, lambda i, nd=a.ndim: (0,) * nd)
    return pl.pallas_call(
        body, name="fwd_in", grid=(t // tm,),
        out_shape=(jax.ShapeDtypeStruct((t, D_IN), F32), jax.ShapeDtypeStruct((t, D), BF)),
        in_specs=[pl.BlockSpec((tm, D), lambda i: (i, 0)), full(ada), full(n1w), _any()],
        out_specs=(pl.BlockSpec((tm, D_IN), lambda i: (i, 0)), pl.BlockSpec((tm, D), lambda i: (i, 0))),
        scratch_shapes=[pltpu.VMEM((D, D_IN), BF), pltpu.SemaphoreType.DMA((N_DEV,))],
        compiler_params=_params(dimension_semantics=("arbitrary",)),
    )(x, ada, n1w, w_in_st)


def _fwd_ffn(x, ycat, tgt, ada, n2w, fw, w_out, w_fi, w_fo):
    t = x.shape[0]
    tm = _row_tile(t, 256)

    def body(x_ref, y_ref, t_ref, ada_ref, n2_ref, fw_ref, wo_ref, wi_ref, wf_ref,
             x1_ref, mix_ref, h2_ref, act_ref, gu_ref, dffn_ref, dx2_ref, part_ref):
        @pl.when(pl.program_id(0) == 0)
        def _():
            part_ref[...] = jnp.zeros_like(part_ref)

        g1, sh2, sc2, g2 = ada_ref[2:3, :], ada_ref[3:4, :], ada_ref[4:5, :], ada_ref[5:6, :]
        mix = _mm(y_ref[...], wo_ref[...])
        mix_ref[...] = mix.astype(BF)
        x1 = x_ref[...] + g1 * mix
        x1_ref[...] = x1
        xh2, _ = _rms(x1)
        h2b = ((xh2 * n2_ref[...]) * (1.0 + sc2) + sh2).astype(BF)
        h2_ref[...] = h2b
        ffn = jnp.zeros((tm, D), F32)
        for lo, hi in FF_CHUNKS:
            gate = _nt(h2b, wi_ref[lo:hi, :])
            up = _nt(h2b, wi_ref[D_FF + lo:D_FF + hi, :])
            gu_ref[:, lo:hi] = gate.astype(BF)
            gu_ref[:, D_FF + lo:D_FF + hi] = up.astype(BF)
            a = (_silu(gate) * up).astype(BF)
            act_ref[:, lo:hi] = a
            ffn = ffn + _mm(a, wf_ref[lo:hi, :])
        x2 = x1 + g2 * ffn
        xh3, rstd3 = _rms(x2)
        err = xh3 * fw_ref[...] - t_ref[...]
        dy = err * (1.0 / D)
        dx2 = _rms_bwd(dy * fw_ref[...], xh3, rstd3)
        dx2_ref[...] = dx2
        dffn_ref[...] = (g2 * dx2).astype(BF)
        part_ref[0:1, :] += _colsum(dx2 * ffn)
        part_ref[1:2, :] += _colsum(dy * xh3)
        part_ref[2:3, :] += jnp.zeros((1, D), F32) + (0.5 / D) * jnp.sum(err * err)

    full = lambda a: pl.BlockSpec(a.shape, lambda i, nd=a.ndim: (0,) * nd)
    row = lambda w: pl.BlockSpec((tm, w), lambda i: (i, 0))
    return pl.pallas_call(
        body, name="fwd_ffn", grid=(t // tm,),
        out_shape=(jax.ShapeDtypeStruct((t, D), F32), jax.ShapeDtypeStruct((t, D), BF), jax.ShapeDtypeStruct((t, D), BF),
                   jax.ShapeDtypeStruct((t, D_FF), BF), jax.ShapeDtypeStruct((t, 2 * D_FF), BF),
                   jax.ShapeDtypeStruct((t, D), BF), jax.ShapeDtypeStruct((t, D), F32), jax.ShapeDtypeStruct((8, D), F32)),
        in_specs=[row(D), row(D), row(D), full(ada), full(n2w), full(fw), _whole_vmem(), _whole_vmem(), _whole_vmem()],
        out_specs=(row(D), row(D), row(D), row(D_FF), row(2 * D_FF), row(D), row(D),
                   pl.BlockSpec((8, D), lambda i: (0, 0))),
        compiler_params=_params(dimension_semantics=("arbitrary",)),
    )(x, ycat, tgt, ada, n2w, fw, w_out, w_fi, w_fo)


def _bwd_ffn(x1, gu, dffn, dx2, mix, ada, n2w, w_out, w_fi, w_fo):
    t = x1.shape[0]
    tm = _row_tile(t, 256)

    def body(x1_ref, gu_ref, dffn_ref, dx2_ref, mix_ref, ada_ref, n2_ref, wo_ref, wi_ref, wf_ref,
             dgu_ref, dx1_ref, dmix_ref, dycat_ref, part_ref):
        @pl.when(pl.program_id(0) == 0)
        def _():
            part_ref[...] = jnp.zeros_like(part_ref)

        g1, sc2 = ada_ref[2:3, :], ada_ref[4:5, :]
        dffn = dffn_ref[...]
        dh2 = jnp.zeros((tm, D), F32)
        for lo, hi in FF_CHUNKS:
            gate = gu_ref[:, lo:hi].astype(F32)
            up = gu_ref[:, D_FF + lo:D_FF + hi].astype(F32)
            dact = _nt(dffn, wf_ref[lo:hi, :])
            sg = _sig(gate)
            dgate = (dact * up * (sg * (1.0 + gate * (1.0 - sg)))).astype(BF)
            dup = (dact * (gate * sg)).astype(BF)
            dgu_ref[:, lo:hi] = dgate
            dgu_ref[:, D_FF + lo:D_FF + hi] = dup
            dh2 = dh2 + _mm(dgate, wi_ref[lo:hi, :]) + _mm(dup, wi_ref[D_FF + lo:D_FF + hi, :])
        x1 = x1_ref[...]
        xh2, rstd2 = _rms(x1)
        xn2 = xh2 * n2_ref[...]
        dxn2 = dh2 * (1.0 + sc2)
        dx1 = dx2_ref[...] + _rms_bwd(dxn2 * n2_ref[...], xh2, rstd2)
        dx1_ref[...] = dx1
        dmix = (g1 * dx1).astype(BF)
        dmix_ref[...] = dmix
        dycat_ref[...] = _nt(dmix, wo_ref[...])
        part_ref[0:1, :] += _colsum(dh2)
        part_ref[1:2, :] += _colsum(dh2 * xn2)
        part_ref[2:3, :] += _colsum(dxn2 * xh2)
        part_ref[3:4, :] += _colsum(dx1 * mix_ref[...].astype(F32))

    full = lambda a: pl.BlockSpec(a.shape, lambda i, nd=a.ndim: (0,) * nd)
    row = lambda w: pl.BlockSpec((tm, w), lambda i: (i, 0))
    return pl.pallas_call(
        body, name="bwd_ffn", grid=(t // tm,),
        out_shape=(jax.ShapeDtypeStruct((t, 2 * D_FF), BF), jax.ShapeDtypeStruct((t, D), F32),
                   jax.ShapeDtypeStruct((t, D), BF), jax.ShapeDtypeStruct((t, D), F32), jax.ShapeDtypeStruct((8, D), F32)),
        in_specs=[row(D), row(2 * D_FF), row(D), row(D), row(D), full(ada), full(n2w),
                  _whole_vmem(), _whole_vmem(), _whole_vmem()],
        out_specs=(row(2 * D_FF), row(D), row(D), row(D), pl.BlockSpec((8, D), lambda i: (0, 0))),
        compiler_params=_params(dimension_semantics=("arbitrary",)),
    )(x1, gu, dffn, dx2, mix, ada, n2w, w_out, w_fi, w_fo)


def _bwd_in(x, dproj, dx1, ada, n1w, w_in_st):
    t = x.shape[0]
    tm = _row_tile(t, 1024)
    n_t = t // tm

    def body(x_ref, dp_ref, dx1_ref, ada_ref, n1_ref, wst_ref, gx_ref, part_ref, w_ref, sems):
        @pl.when(pl.program_id(0) == 0)
        def _():
            part_ref[...] = jnp.zeros_like(part_ref)
            _load_columns(wst_ref, w_ref, sems)

        dh1 = _nt(dp_ref[...], w_ref[...])
        xh, rstd = _rms(x_ref[...])
        xn = xh * n1_ref[...]
        dxn = dh1 * (1.0 + ada_ref[1:2, :])
        gx_ref[...] = dx1_ref[...] + _rms_bwd(dxn * n1_ref[...], xh, rstd)
        part_ref[0:1, :] += _colsum(dh1)
        part_ref[1:2, :] += _colsum(dh1 * xn)
        part_ref[2:3, :] += _colsum(dxn * xh)

    full = lambda a: pl.BlockSpec(a.shape, lambda i, nd=a.ndim: (0,) * nd)
    row = lambda w: pl.BlockSpec((tm, w), lambda i: (i, 0))
    return pl.pallas_call(
        body, name="bwd_in", grid=(n_t,),
        out_shape=(jax.ShapeDtypeStruct((t, D), F32), jax.ShapeDtypeStruct((8, D), F32)),
        in_specs=[row(D), row(D_IN), row(D), full(ada), full(n1w), _any()],
        out_specs=(row(D), pl.BlockSpec((8, D), lambda i: (0, 0))),
        scratch_shapes=[pltpu.VMEM((D, D_IN), BF), pltpu.SemaphoreType.DMA((N_DEV,))],
        compiler_params=_params(dimension_semantics=("arbitrary",)),
    )(x, dproj, dx1, ada, n1w, w_in_st)


def _wgrad(a, b, name, a_spec, b_spec, out_shape, out_spec, grid, acc_shape):
    n_t = grid[1]

    def body(a_ref, b_ref, o_ref, acc):
        @pl.when(pl.program_id(1) == 0)
        def _():
            acc[...] = jnp.zeros_like(acc)

        acc[...] += _tn(a_ref[...], b_ref[...])

        @pl.when(pl.program_id(1) == n_t - 1)
        def _():
            o_ref[...] = acc[...].astype(BF)

    return pl.pallas_call(
        body, name=name, grid=grid,
        out_shape=jax.ShapeDtypeStruct(out_shape, BF),
        in_specs=[a_spec, b_spec], out_specs=out_spec,
        scratch_shapes=[pltpu.VMEM(acc_shape, F32)],
        compiler_params=_params(dimension_semantics=("arbitrary", "arbitrary")),
    )(a, b)


W_IN_COLS = D_IN // N_DEV
PAIR_COLS = 2 * W_IN_COLS


def _wgrad_in_scatter(h1, dproj, to_gather):
    t = h1.shape[0]
    tk = _row_tile(t, 2048)
    n_t = t // tk
    n_g = N_DEV // 2
    ng = len(to_gather)

    def body(*refs):
        a_ref, b_ref = refs[:2]
        g_in = refs[2:2 + ng]
        r_ref = refs[2 + ng]
        g_out = refs[3 + ng:3 + 2 * ng]
        acc, stage, send_sems, recv_sems, local_sem = refs[3 + 2 * ng:8 + 2 * ng]
        gx = _Exchange(g_in, g_out, *refs[8 + 2 * ng:], False)
        grp, step = pl.program_id(0), pl.program_id(1)
        me = 4 * lax.axis_index("x") + 2 * lax.axis_index("y") + lax.axis_index("c")

        def to_device(d):
            return pltpu.make_async_remote_copy(
                src_ref=stage.at[d], dst_ref=r_ref.at[me], send_sem=send_sems.at[d], recv_sem=recv_sems.at[me],
                device_id=(d >> 2, (d >> 1) & 1, d & 1), device_id_type=MESH)

        def from_device(s):
            return pltpu.make_async_remote_copy(
                src_ref=stage.at[s], dst_ref=r_ref.at[s], send_sem=send_sems.at[s], recv_sem=recv_sems.at[s],
                device_id=(s >> 2, (s >> 1) & 1, s & 1), device_id_type=MESH)

        def own(d):
            return pltpu.make_async_copy(stage.at[d], r_ref.at[d], local_sem)

        @pl.when((grp == 0) & (step == 0))
        def _():
            gx.start()

        @pl.when(step == 0)
        def _():
            acc[...] = jnp.zeros_like(acc)

        acc[...] += _tn(a_ref[...], b_ref[...])

        for gg in range(n_g):
            @pl.when((grp == gg) & (step == n_t - 1))
            def _(gg=gg):
                for half in range(2):
                    d = 2 * gg + half
                    stage[d] = acc[:, half * W_IN_COLS:(half + 1) * W_IN_COLS].astype(BF)

                    @pl.when(me == d)
                    def _(d=d):
                        own(d).start()

                    @pl.when(me != d)
                    def _(d=d):
                        to_device(d).start()

        @pl.when((grp == n_g - 1) & (step == n_t - 1))
        def _():
            for d in range(N_DEV):
                @pl.when(me == d)
                def _(d=d):
                    own(d).wait()

                @pl.when(me != d)
                def _(d=d):
                    to_device(d).wait_send()
                    from_device(d).wait_recv()
            gx.wait()

    outs = pl.pallas_call(
        body, name="wgrad_in", grid=(n_g, n_t),
        out_shape=(jax.ShapeDtypeStruct((N_DEV, D, W_IN_COLS), BF), *_exchange_out_shapes(to_gather, False)),
        in_specs=[pl.BlockSpec((tk, D), lambda j, i: (i, 0)), pl.BlockSpec((tk, PAIR_COLS), lambda j, i: (i, j))]
        + [_any() for _ in to_gather],
        out_specs=(_any(), *[_any() for _ in to_gather]),
        scratch_shapes=[pltpu.VMEM((D, PAIR_COLS), F32), pltpu.VMEM((N_DEV, D, W_IN_COLS), BF),
                        pltpu.SemaphoreType.DMA((N_DEV,)), pltpu.SemaphoreType.DMA((N_DEV,)), pltpu.SemaphoreType.DMA]
        + _exchange_sems(ng),
        compiler_params=_params(dimension_semantics=("arbitrary", "arbitrary"), has_side_effects=True),
    )(h1, dproj, *to_gather)
    return outs[0], outs[1:]


def _adamw_math(w, g, m, v):
    m = ADAM_B1 * m + (1.0 - ADAM_B1) * g
    v = ADAM_B2 * v + (1.0 - ADAM_B2) * (g * g)
    m_hat = m / (1.0 - ADAM_B1 ** ADAM_STEP)
    v_hat = v / (1.0 - ADAM_B2 ** ADAM_STEP)
    delta = -ADAM_LR * (m_hat / (jnp.sqrt(v_hat) + ADAM_EPS) + ADAM_WD * w)
    return delta, m, v


def _adamw_recv(w, m, v, recv, name, tr):
    r, c = w.shape

    def body(w_ref, m_ref, v_ref, r_ref, g_ref, d_ref, nm_ref, nv_ref):
        g = r_ref[0].astype(F32)
        for k in range(1, N_DEV):
            g = g + r_ref[k].astype(F32)
        g_ref[...] = g
        d_ref[...], nm_ref[...], nv_ref[...] = _adamw_math(w_ref[...], g, m_ref[...], v_ref[...])

    row = pl.BlockSpec((tr, c), lambda i: (i, 0))
    return pl.pallas_call(
        body, name=name, grid=(r // tr,),
        out_shape=tuple(jax.ShapeDtypeStruct((r, c), F32) for _ in range(4)),
        in_specs=[row, row, row, pl.BlockSpec((N_DEV, tr, c), lambda i: (0, i, 0))],
        out_specs=(row, row, row, row),
        compiler_params=_params(dimension_semantics=("arbitrary",)),
    )(w, m, v, recv)


def _adamw_ada(w, m, v, cact, dada_cols):
    r, c = w.shape
    tr = 256

    def body(w_ref, m_ref, v_ref, ca_ref, da_ref, g_ref, d_ref, nm_ref, nv_ref):
        g = _tn(ca_ref[...].astype(BF), da_ref[...].astype(BF))
        g_ref[...] = g
        d_ref[...], nm_ref[...], nv_ref[...] = _adamw_math(w_ref[...], g, m_ref[...], v_ref[...])

    row = pl.BlockSpec((tr, c), lambda i: (i, 0))
    return pl.pallas_call(
        body, name="adamw_ada", grid=(r // tr,),
        out_shape=tuple(jax.ShapeDtypeStruct((r, c), F32) for _ in range(4)),
        in_specs=[row, row, row, pl.BlockSpec((N_DEV, tr), lambda i: (0, i)), pl.BlockSpec(dada_cols.shape, lambda i: (0, 0))],
        out_specs=(row, row, row, row),
        compiler_params=_params(dimension_semantics=("arbitrary",)),
    )(w, m, v, cact, dada_cols)


def _adamw_small(gathered, wmv):
    n_g = len(gathered)
    n_p = len(wmv)
    flat = [a for trip in wmv for a in trip]

    def body(*refs):
        g_refs = refs[:n_g]
        p_refs = refs[n_g:n_g + 3 * n_p]
        o_refs = refs[n_g + 3 * n_p:]

        def total(ref):
            s = ref[0]
            for k in range(1, N_DEV):
                s = s + ref[k]
            return s

        f3, b3, b1, dws, dbs, dlnw, dlnb, dlo, dgn = [total(r) for r in g_refs]
        dada_rows = [b1[0:1], b1[1:2], b3[3:4], b3[0:1], b3[1:2], f3[0:1]]
        for r, g in enumerate(dada_rows):
            cs = slice(r * D, (r + 1) * D)
            w, m, v = p_refs[0][:, cs], p_refs[1][:, cs], p_refs[2][:, cs]
            o_refs[0][:, cs] = g
            o_refs[1][:, cs], o_refs[2][:, cs], o_refs[3][:, cs] = _adamw_math(w, g, m, v)
        grads = [None, b1[2:3], dws, dbs, dlnw, dlnb, dlo, dgn, b3[2:3], f3[1:2]]
        for i, g in enumerate(grads):
            if g is None:
                continue
            w, m, v = p_refs[3 * i][...], p_refs[3 * i + 1][...], p_refs[3 * i + 2][...]
            o_refs[4 * i][...] = g
            o_refs[4 * i + 1][...], o_refs[4 * i + 2][...], o_refs[4 * i + 3][...] = _adamw_math(w, g, m, v)
        o_refs[4 * n_p][...] = jnp.zeros((8, 128), F32) + f3[2:3, 0:128]

    out_shape = []
    for w, _, _ in wmv:
        out_shape += [jax.ShapeDtypeStruct(w.shape, F32)] * 4
    out_shape.append(jax.ShapeDtypeStruct((8, 128), F32))
    n_in = n_g + 3 * n_p
    return pl.pallas_call(
        body, name="adamw_small",
        out_shape=tuple(out_shape),
        in_specs=[_whole_vmem()] * n_in, out_specs=tuple(_whole_vmem() for _ in out_shape),
        compiler_params=_params(),
    )(*gathered, *flat)


def kernel(x, c, w_ada, b_ada, norm1_w, w_in, w_s, b_s, v_ln_w, v_ln_b, lower_bounds, gn_w, w_out, norm2_w, w_ffn_in, w_ffn_out, final_norm_w, loss_target, m_w_ada, m_b_ada, m_norm1_w, m_w_in, m_w_s, m_b_s, m_v_ln_w, m_v_ln_b, m_lower_bounds, m_gn_w, m_w_out, m_norm2_w, m_w_ffn_in, m_w_ffn_out, m_final_norm_w, v_w_ada, v_b_ada, v_norm1_w, v_w_in, v_w_s, v_b_s, v_v_ln_w, v_v_ln_b, v_lower_bounds, v_gn_w, v_w_out, v_norm2_w, v_w_ffn_in, v_w_ffn_out, v_final_norm_w):
    me = 4 * lax.axis_index("x") + 2 * lax.axis_index("y") + lax.axis_index("c")
    t = x.shape[1]
    x2d = x.reshape(t, D)
    tgt = loss_target.reshape(t, D)
    ada_cols = w_ada.shape[2]

    tp = lambda a: jnp.swapaxes(a[0], 0, 1)
    win_b, wout_b, wfi_b, wfo_b = _cast_bf16([w_in[0], w_out[0], tp(w_ffn_in), w_ffn_out[0]], "cast_weights")

    b_cols = lax.dynamic_slice(b_ada, (0, me * ada_cols), (1, ada_cols))
    win_st, ada_st, cact = _startup(win_b, c, w_ada[0], b_cols)
    ada = lax.dynamic_index_in_dim(ada_st, me, axis=1, keepdims=False).reshape(6, D)

    tables = _decay_tables()
    ws3 = w_s[0]
    bs_t = b_s[0].T

    proj, h1 = _fwd_in(x2d, ada, norm1_w, win_st)
    ycat, states, (wout_st, wfi_st, wfo_st) = _fwd_mix(proj, ws3, bs_t, v_ln_w, v_ln_b, lower_bounds, gn_w, tables,
                                                       [wout_b, wfi_b, wfo_b])
    w_out_full = wout_st.reshape(D, D)
    w_fo_full = wfo_st.reshape(D_FF, D)
    w_fi_full = wfi_st.reshape(2 * D_FF, D)
    x1, mixb, h2, act, gu, dffn, dx2, part_f = _fwd_ffn(x2d, ycat, tgt, ada, norm2_w, final_norm_w.reshape(1, D),
                                                        w_out_full, w_fi_full, w_fo_full)
    dgu, dx1, dmix, dycat, part_b3 = _bwd_ffn(x1, gu, dffn, dx2, mixb, ada, norm2_w, w_out_full, w_fi_full, w_fo_full)
    tk = _row_tile(t, 2048)
    n_t = t // tk
    dwout = _wgrad(ycat, dmix, "wgrad_out",
                   pl.BlockSpec((tk, D), lambda j, i: (i, 0)), pl.BlockSpec((tk, D), lambda j, i: (i, 0)),
                   (D, D), pl.BlockSpec((D, D), lambda j, i: (0, 0)), (1, n_t), (D, D))
    dwfi = _wgrad(dgu, h2, "wgrad_ffn_in",
                  pl.BlockSpec((tk, FF_PAIR), lambda j, i: (i, j)), pl.BlockSpec((tk, D), lambda j, i: (i, 0)),
                  (4, FF_PAIR, D), pl.BlockSpec((None, FF_PAIR, D), lambda j, i: (j, 0, 0)), (4, n_t), (FF_PAIR, D))
    dwfo = _wgrad(act, dffn, "wgrad_ffn_out",
                  pl.BlockSpec((tk, FF_PAIR), lambda j, i: (i, j)), pl.BlockSpec((tk, D), lambda j, i: (i, 0)),
                  (2, FF_PAIR, D), pl.BlockSpec((None, FF_PAIR, D), lambda j, i: (j, 0, 0)), (2, n_t), (FF_PAIR, D))
    dproj, (dws, dbs_t, dlnw, dlnb, dlower, dgnw), (r_out, r_fi, r_fo) = _bwd_mix(
        proj, dycat, states, ws3, bs_t, v_ln_w, v_ln_b, lower_bounds, gn_w, tables,
        [dwout.reshape(N_DEV, D // N_DEV, D), dwfi.reshape(N_DEV, FF_BLK, D), dwfo.reshape(N_DEV, D_FF // N_DEV, D)])
    grad_x, part_b1 = _bwd_in(x2d, dproj, dx1, ada, norm1_w, win_st)
    r_in, early = _wgrad_in_scatter(h1, dproj, [part_f, part_b3, dws, dbs_t, dlnw, dlnb, dlower, dgnw])

    g_w_in, d_w_in, nm_w_in, nv_w_in = _adamw_recv(w_in[0], m_w_in[0], v_w_in[0], r_in, "adamw_w_in", 256)
    g_w_out, d_w_out, nm_w_out, nv_w_out = _adamw_recv(w_out[0], m_w_out[0], v_w_out[0], r_out, "adamw_w_out", 128)
    fi_t = _adamw_recv(tp(w_ffn_in), tp(m_w_ffn_in), tp(v_w_ffn_in), r_fi, "adamw_w_ffn_in", 176)
    g_w_fi, d_w_fi, nm_w_fi, nv_w_fi = [jnp.swapaxes(a, 0, 1) for a in fi_t]
    g_w_fo, d_w_fo, nm_w_fo, nv_w_fo = _adamw_recv(w_ffn_out[0], m_w_ffn_out[0], v_w_ffn_out[0], r_fo, "adamw_w_ffn_out", 176)

    (b1_all,) = _exchange([part_b1], "gather_small", False, True)
    gathered = [early[0], early[1], b1_all, *early[2:]]
    f3_all, b3_all = gathered[0], gathered[1]
    dada_all = jnp.stack([b1_all[:, 0], b1_all[:, 1], b3_all[:, 3], b3_all[:, 0], b3_all[:, 1], f3_all[:, 0]], axis=1)
    dada_cols = lax.dynamic_slice(dada_all.reshape(N_DEV, 6 * D), (0, me * ada_cols), (N_DEV, ada_cols))
    g_w_ada, d_w_ada, nm_w_ada, nv_w_ada = _adamw_ada(w_ada[0], m_w_ada[0], v_w_ada[0], cact, dada_cols)

    r1 = lambda a: a.reshape(1, D)
    tr = lambda a: a[0].T
    wmv = [
        (b_ada, m_b_ada, v_b_ada),
        (norm1_w, m_norm1_w, v_norm1_w),
        (w_s[0], m_w_s[0], v_w_s[0]),
        (tr(b_s), tr(m_b_s), tr(v_b_s)),
        (v_ln_w, m_v_ln_w, v_v_ln_w),
        (v_ln_b, m_v_ln_b, v_v_ln_b),
        (lower_bounds, m_lower_bounds, v_lower_bounds),
        (gn_w, m_gn_w, v_gn_w),
        (norm2_w, m_norm2_w, v_norm2_w),
        (r1(final_norm_w), r1(m_final_norm_w), r1(v_final_norm_w)),
    ]
    small = _adamw_small(gathered, wmv)
    loss = small[-1][0, 0]

    def unshape(i, a):
        if i == 2:
            return a.reshape(1, NH, BLK, BLK)
        if i == 3:
            return a.T.reshape(1, NH, BLK)
        if i == 9:
            return a.reshape(D)
        return a

    def small_out(kind):
        return [unshape(i, small[4 * i + kind]) for i in range(len(wmv))]

    e3 = lambda a: a[None]
    big = {
        0: (e3(g_w_ada), e3(g_w_in), e3(g_w_out), e3(g_w_fi), e3(g_w_fo)),
        1: (e3(d_w_ada), e3(d_w_in), e3(d_w_out), e3(d_w_fi), e3(d_w_fo)),
        2: (e3(nm_w_ada), e3(nm_w_in), e3(nm_w_out), e3(nm_w_fi), e3(nm_w_fo)),
        3: (e3(nv_w_ada), e3(nv_w_in), e3(nv_w_out), e3(nv_w_fi), e3(nv_w_fo)),
    }

    def ordered(kind):
        s = small_out(kind)
        b_ = big[kind]
        return [b_[0], s[0], s[1], b_[1], s[2], s[3], s[4], s[5], s[6], s[7], b_[2], s[8], b_[3], b_[4], s[9]]

    return (loss, grad_x.reshape(1, t, D), *ordered(0), *ordered(1), *ordered(2), *ordered(3))
```

```python
import numpy as np
import jax
import jax.numpy as jnp
from jax import lax
from jax.experimental import pallas as pl
from jax.experimental.pallas import tpu as pltpu

F32 = jnp.float32
BF = jnp.bfloat16
MESH = pl.DeviceIdType.MESH

N_DEV = 8
D = 1024
D_IN = 3072
D_FF = 2816
FF_BLK = D_FF // 4
FF_CHUNKS = ((0, D_FF),)
FF_PAIR = 2 * FF_BLK
CH = 64
BLK = 128
MIX_TILE = 256
NH = 4
HD = 128
EPS = 1e-6
LEVELS = (32, 16, 8, 4, 2, 1)

ADAM_LR = 0.001
ADAM_B1 = 0.9
ADAM_B2 = 0.999
ADAM_EPS = 1e-08
ADAM_WD = 0.01
ADAM_STEP = 10

VMEM_LIMIT = 56 * 1024 * 1024


def _params(**kw):
    return pltpu.CompilerParams(vmem_limit_bytes=VMEM_LIMIT, **kw)


def _whole_vmem():
    return pl.BlockSpec(memory_space=pltpu.VMEM)


def _any():
    return pl.BlockSpec(memory_space=pl.ANY)


class _Exchange:
    def __init__(self, ins, outs, send_sems, recv_sems, local_sems, scatter):
        self.ins, self.outs, self.scatter = ins, outs, scatter
        self.send_sems, self.recv_sems, self.local_sems = send_sems, recv_sems, local_sems
        x, y, c = lax.axis_index("x"), lax.axis_index("y"), lax.axis_index("c")
        self.me = 4 * x + 2 * y + c
        self.peers = []
        for k in range(1, N_DEV):
            peer = (1 - x if (k >> 2) & 1 else x, 1 - y if (k >> 1) & 1 else y, 1 - c if k & 1 else c)
            self.peers.append((peer, 4 * peer[0] + 2 * peer[1] + peer[2]))

    def _src(self, a, idx):
        return self.ins[a].at[idx] if self.scatter else self.ins[a]

    def _local(self):
        return [pltpu.make_async_copy(self._src(a, self.me), self.outs[a].at[self.me], self.local_sems.at[a])
                for a in range(len(self.ins))]

    def _remote(self, a, k, dst_slot):
        peer, peer_idx = self.peers[k]
        return pltpu.make_async_remote_copy(
            src_ref=self._src(a, peer_idx), dst_ref=self.outs[a].at[dst_slot],
            send_sem=self.send_sems.at[a, k], recv_sem=self.recv_sems.at[a, k],
            device_id=peer, device_id_type=MESH)

    def start(self):
        for cp in self._local():
            cp.start()
        for k in range(N_DEV - 1):
            for a in range(len(self.ins)):
                self._remote(a, k, self.me).start()

    def wait(self):
        for k in range(N_DEV - 1):
            for a in range(len(self.ins)):
                self._remote(a, k, self.peers[k][1]).wait()
        for cp in self._local():
            cp.wait()


class _Gather2:
    def __init__(self, ins, outs, send_sems, recv_sems, local_sems):
        self.ins, self.outs = ins, outs
        self.send_sems, self.recv_sems, self.local_sems = send_sems, recv_sems, local_sems
        x, y, c = lax.axis_index("x"), lax.axis_index("y"), lax.axis_index("c")
        self.c = c
        self.me = 4 * x + 2 * y + c
        self.sibling = (x, y, 1 - c)
        self.chips = [(1 - x, y), (x, 1 - y), (1 - x, 1 - y)]

    @staticmethod
    def _idx(px, py, pc):
        return 4 * px + 2 * py + pc

    def _copy(self, a, k, slot, to, own):
        src = self.ins[a] if own else self.outs[a].at[slot]
        return pltpu.make_async_remote_copy(
            src_ref=src, dst_ref=self.outs[a].at[slot],
            send_sem=self.send_sems.at[a, k], recv_sem=self.recv_sems.at[a, k],
            device_id=to, device_id_type=MESH)

    def _local(self):
        return [pltpu.make_async_copy(self.ins[a], self.outs[a].at[self.me], self.local_sems.at[a])
                for a in range(len(self.ins))]

    def start(self):
        for cp in self._local():
            cp.start()
        for a in range(len(self.ins)):
            self._copy(a, 0, self.me, self.sibling, True).start()
            for j, chip in enumerate(self.chips):
                self._copy(a, 1 + j, self.me, (*chip, self.c), True).start()

    def forward(self):
        for j, chip in enumerate(self.chips):
            for a in range(len(self.ins)):
                slot = self._idx(*chip, self.c)
                self._copy(a, 1 + j, slot, (*chip, self.c), True).wait_recv()
                self._copy(a, 4 + j, slot, self.sibling, False).start()

    def finish(self):
        for a in range(len(self.ins)):
            self._copy(a, 0, self._idx(*self.sibling), self.sibling, True).wait_recv()
            for j, chip in enumerate(self.chips):
                self._copy(a, 4 + j, self._idx(*chip, 1 - self.c), self.sibling, False).wait_recv()
            self._copy(a, 0, self.me, self.sibling, True).wait_send()
            for j, chip in enumerate(self.chips):
                self._copy(a, 1 + j, self.me, (*chip, self.c), True).wait_send()
                self._copy(a, 4 + j, self._idx(*chip, self.c), self.sibling, False).wait_send()
        for cp in self._local():
            cp.wait()


def _exchange_sems(n):
    return [pltpu.SemaphoreType.DMA((n, N_DEV - 1)), pltpu.SemaphoreType.DMA((n, N_DEV - 1)), pltpu.SemaphoreType.DMA((n,))]


def _exchange_out_shapes(arrays, scatter):
    return [jax.ShapeDtypeStruct(a.shape if scatter else (N_DEV,) + a.shape, a.dtype) for a in arrays]


def _exchange(arrays, name, scatter, in_vmem):
    n = len(arrays)

    def body(*refs):
        ex = _Exchange(refs[:n], refs[n:2 * n], *refs[2 * n:], scatter)
        ex.start()
        ex.wait()

    spec = _whole_vmem if in_vmem else _any
    return pl.pallas_call(
        body, name=name,
        out_shape=tuple(_exchange_out_shapes(arrays, scatter)),
        in_specs=[spec() for _ in arrays],
        out_specs=tuple(spec() for _ in arrays),
        scratch_shapes=_exchange_sems(n),
        compiler_params=_params(has_side_effects=True),
    )(*arrays)


def _cast_bf16(arrays, name):
    n = len(arrays)

    def body(*refs):
        for i in range(n):
            refs[n + i][...] = refs[i][...].astype(BF)

    return pl.pallas_call(
        body, name=name,
        out_shape=tuple(jax.ShapeDtypeStruct(a.shape, BF) for a in arrays),
        in_specs=[_whole_vmem() for _ in arrays],
        out_specs=tuple(_whole_vmem() for _ in arrays),
        compiler_params=_params(),
    )(*arrays)


def _load_columns(stacked_ref, full_ref, sems):
    c = stacked_ref.shape[2]
    cps = [pltpu.make_async_copy(stacked_ref.at[d], full_ref.at[:, pl.ds(d * c, c)], sems.at[d]) for d in range(N_DEV)]
    for cp in cps:
        cp.start()
    for cp in cps:
        cp.wait()


def _sig(v):
    return 0.5 * jnp.tanh(0.5 * v) + 0.5


@jax.custom_vjp
def _silu(v):
    return v * _sig(v)


def _silu_fwd(v):
    return _silu(v), v


def _silu_bwd(v, g):
    s = _sig(v)
    return (g * (s * (1.0 + v * (1.0 - s))),)


_silu.defvjp(_silu_fwd, _silu_bwd)


@jax.custom_vjp
def _sigmoid_rel(v):
    e = jnp.exp(-jnp.abs(v))
    d = 1.0 + e
    r = pl.reciprocal(d, approx=True)
    r = r * (2.0 - d * r)
    r = r * (2.0 - d * r)
    return jnp.where(v >= 0.0, r, e * r)


def _sigmoid_rel_fwd(v):
    s = _sigmoid_rel(v)
    return s, s


def _sigmoid_rel_bwd(s, g):
    return (g * (s * (1.0 - s)),)


_sigmoid_rel.defvjp(_sigmoid_rel_fwd, _sigmoid_rel_bwd)


def _gelu(v):
    return 0.5 * v * (1.0 + lax.erf(v * 0.7071067811865476))


def _dot(a, b, ca, cb):
    return lax.dot_general(a, b, (((ca,), (cb,)), ((), ())), preferred_element_type=F32)


@jax.custom_vjp
def _mm(a, b):
    return _dot(a, b, 1, 0)


def _mm_fwd(a, b):
    return _dot(a, b, 1, 0), (a, b)


def _mm_bwd(res, g):
    a, b = res
    gb = g.astype(BF)
    return _dot(gb, b, 1, 1).astype(a.dtype), _dot(a, gb, 0, 0).astype(b.dtype)


_mm.defvjp(_mm_fwd, _mm_bwd)


@jax.custom_vjp
def _nt(a, b):
    return _dot(a, b, 1, 1)


def _nt_fwd(a, b):
    return _dot(a, b, 1, 1), (a, b)


def _nt_bwd(res, g):
    a, b = res
    gb = g.astype(BF)
    return _dot(gb, b, 1, 0).astype(a.dtype), _dot(gb, a, 0, 0).astype(b.dtype)


_nt.defvjp(_nt_fwd, _nt_bwd)


@jax.custom_vjp
def _tn(a, b):
    return _dot(a, b, 0, 0)


def _tn_fwd(a, b):
    return _dot(a, b, 0, 0), (a, b)


def _tn_bwd(res, g):
    a, b = res
    gb = g.astype(BF)
    return _dot(b, gb, 1, 1).astype(a.dtype), _dot(a, gb, 1, 0).astype(b.dtype)


_tn.defvjp(_tn_fwd, _tn_bwd)


def _make_cast_dot(ca, cb, da_dims, db_dims):
    @jax.custom_vjp
    def dot(a, b):
        return _dot(a.astype(BF), b.astype(BF), ca, cb)

    def fwd(a, b):
        ab, bb = a.astype(BF), b.astype(BF)
        return _dot(ab, bb, ca, cb), (ab, bb)

    def bwd(res, g):
        ops = {"a": res[0], "b": res[1], "g": g.astype(BF)}
        return (_dot(ops[da_dims[0]], ops[da_dims[1]], da_dims[2], da_dims[3]),
                _dot(ops[db_dims[0]], ops[db_dims[1]], db_dims[2], db_dims[3]))

    dot.defvjp(fwd, bwd)
    return dot


_mm_c = _make_cast_dot(1, 0, ("g", "b", 1, 1), ("a", "g", 0, 0))
_nt_c = _make_cast_dot(1, 1, ("g", "b", 1, 0), ("g", "a", 0, 0))
_tn_c = _make_cast_dot(0, 0, ("b", "g", 1, 1), ("a", "g", 1, 0))


def _startup(win_b, c, w_ada, b_cols):
    ncol = w_ada.shape[1]

    def body(win_ref, c_ref, w_ref, b_ref, winst_ref, adast_ref, cact_ref, call_ref, blk_ref,
             ws, wr, wl, cs, cr, cl, as_, ar, al):
        big = _Gather2([win_ref], [winst_ref], ws, wr, wl)
        big.start()
        gc = _Exchange([c_ref], [call_ref], cs, cr, cl, False)
        gc.start()
        gc.wait()
        ca = _silu(call_ref[...].reshape(N_DEV, D))
        cact_ref[...] = ca
        blk_ref[...] = _mm(ca.astype(BF), w_ref[...].astype(BF)) + b_ref[...]
        ga = _Exchange([blk_ref], [adast_ref], as_, ar, al, False)
        ga.start()
        ga.wait()
        big.forward()
        big.finish()

    return pl.pallas_call(
        body, name="startup",
        out_shape=(jax.ShapeDtypeStruct((N_DEV,) + win_b.shape, BF), jax.ShapeDtypeStruct((N_DEV, N_DEV, ncol), F32),
                   jax.ShapeDtypeStruct((N_DEV, D), F32)),
        in_specs=[_any(), _whole_vmem(), _whole_vmem(), _whole_vmem()],
        out_specs=(_any(), _whole_vmem(), _whole_vmem()),
        scratch_shapes=[pltpu.VMEM((N_DEV, 1, D), F32), pltpu.VMEM((N_DEV, ncol), F32)]
        + _exchange_sems(1) + _exchange_sems(1) + _exchange_sems(1),
        compiler_params=_params(has_side_effects=True),
    )(win_b, c, w_ada, b_cols)


def _decay_tables():
    t = np.arange(CH)
    tri = (t[None, :] <= t[:, None]).astype(np.float32)
    masks = []
    for h in LEVELS:
        m = (t // (2 * h)) * (2 * h) + h
        upper = t >= m
        same = (t[:, None] // (2 * h)) == (t[None, :] // (2 * h))
        masks.append(same & upper[:, None] & (~upper)[None, :])
    masks.append(np.eye(CH, dtype=bool))
    lv = [np.where((t % (2 * h)) >= h, 1.0, -1.0) for h in LEVELS[:4]]
    m4 = t % 4
    lv += [(m4 == 0) * 1.0, (m4 >= 2) * 1.0, (m4 == 3) * 1.0, (t % 2 == 1) * 1.0]
    lvl = np.broadcast_to(np.stack(lv)[:, :, None], (8, CH, D // 2)).astype(np.float32)
    cid = np.arange(BLK) // CH
    gmask = (cid[:, None] >= cid[None, :]).astype(np.float32)
    masks = np.stack(masks).astype(np.float32)
    stacked = np.zeros((masks.shape[0], NH * CH, NH * CH), np.float32)
    for h in range(NH):
        stacked[:, h * CH:(h + 1) * CH, h * CH:(h + 1) * CH] = masks
    return (jnp.asarray(tri, BF), jnp.asarray(stacked), jnp.asarray(gmask), jnp.asarray(lvl))


def _split2(v):
    v1 = v.astype(BF)
    return v1, (v - v1.astype(F32)).astype(BF)


@jax.custom_vjp
def _cumsum_mm(tri, v):
    p1, p2 = _split2(v)
    return _dot(tri, p1, 1, 0) + _dot(tri, p2, 1, 0)


def _cumsum_mm_fwd(tri, v):
    return _cumsum_mm(tri, v), tri


def _cumsum_mm_bwd(tri, g):
    p1, p2 = _split2(g)
    return jnp.zeros_like(tri), _dot(tri, p1, 0, 0) + _dot(tri, p2, 0, 0)


_cumsum_mm.defvjp(_cumsum_mm_fwd, _cumsum_mm_bwd)


def _make_row_roll(shift):
    @jax.custom_vjp
    def roll(x):
        return pltpu.roll(x, shift % CH, 0)

    def fwd(x):
        return roll(x), None

    def bwd(_, g):
        return (pltpu.roll(g, (-shift) % CH, 0),)

    roll.defvjp(fwd, bwd)
    return roll


_prev_row = _make_row_roll(1)
_next_row = _make_row_roll(-1)


def _mix_tile(proj, state, w_s, b_s_t, ln_w, ln_b, lower, gn_w, consts):
    tri, masks, gmask, lvl = consts
    mt = proj.shape[0]
    u = proj[:, 0:512]
    v = proj[:, 512:1024]
    q = proj[:, 1024:1536]
    fl = proj[:, 1536:2048]
    inp = proj[:, 2048:2560]
    g = proj[:, 2560:3072]

    ug = _gelu(u)
    vg = _gelu(v)
    mu = jnp.mean(vg, axis=-1, keepdims=True)
    vc = vg - mu
    var = jnp.mean(vc * vc, axis=-1, keepdims=True)
    vn = vc * lax.rsqrt(var + EPS) * ln_w + ln_b
    wsm = [w_s[h] * gmask for h in range(NH)]
    ya = [[None] * NH for _ in range(mt // BLK)]
    for bi in range(mt // BLK):
        rows = slice(bi * BLK, (bi + 1) * BLK)
        for h in range(NH):
            cols = slice(h * HD, (h + 1) * HD)
            ya[bi][h] = ug[rows, cols] * (_mm_c(wsm[h], vn[rows, cols]) + b_s_t[:, h:h + 1])
    ya_full = jnp.concatenate([jnp.concatenate(r, axis=1) for r in ya], axis=0)

    l0 = lower[0:1, :]
    l1 = lower[1:2, :]
    mx = jnp.maximum(l0, l1)
    e0 = jnp.exp(l0 - mx)
    e1 = jnp.exp(l1 - mx)
    lb = e0 / (e0 + e1)
    qf = _silu(q)
    f = lb + (1.0 - lb) * _sigmoid_rel(fl)
    logf = jnp.log(f)
    kk = 1.0 - f
    gate = _silu(g)
    nl = len(LEVELS)
    half = D // 2
    heads_to_rows = lambda a: jnp.concatenate([a[:, h * HD:(h + 1) * HD] for h in range(NH)], axis=0)
    st = list(state)
    yb = [[None] * NH for _ in range(mt // CH)]
    for ci in range(mt // CH):
        rows = slice(ci * CH, (ci + 1) * CH)
        lc = logf[rows]
        b = _cumsum_mm(tri, lc)
        xb = jnp.exp(b)
        xinv = jnp.exp(b[CH - 1:CH, :] - b)
        xl = []
        for i, hs in enumerate(LEVELS[:3]):
            refs = [jnp.broadcast_to(b[r:r + 1, :], (2 * hs, half)) for r in range(hs - 1, CH, 2 * hs)]
            bref = refs[0] if len(refs) == 1 else jnp.concatenate(refs, axis=0)
            xl.append(jnp.exp(lvl[i] * (b - bref)))
        b3 = b.reshape(CH // 8, 8, half)
        bref = jnp.broadcast_to(b3[:, 3:4, :], (CH // 8, 8, half)).reshape(CH, half)
        xl.append(jnp.exp(lvl[3] * (b - bref)))
        xl.append(jnp.exp(lvl[4] * _next_row(lc) + lvl[5] * lc + lvl[6] * _prev_row(lc)))
        xl.append(jnp.exp(lvl[7] * lc))
        qc = qf[rows]
        kc = kk[rows]
        ic = inp[rows]
        zsrc = [jnp.concatenate([(qc if (r0 // hs) % 2 == 1 else kc)[r0:r0 + hs] for r0 in range(0, CH, hs)], axis=0)
                for hs in LEVELS[:3]]
        qs, ks = heads_to_rows(qc), heads_to_rows(kc)
        carried = _nt_c(qs * heads_to_rows(xb), jnp.concatenate(st, axis=0))
        inter = jnp.concatenate([carried[h * CH:(h + 1) * CH, h * HD:(h + 1) * HD] for h in range(NH)], axis=0)
        attn = masks[nl] * _nt_c(qs, ks)
        for li in range(nl):
            xs = heads_to_rows(xl[li])
            if li < 3:
                z = heads_to_rows(zsrc[li]) * xs
                pairs = _nt_c(z, z)
            else:
                pairs = _nt_c(qs * xs, ks * xs)
            attn = attn + masks[li] * pairs
        o = inter + _mm_c(attn, heads_to_rows(ic))
        update = _tn_c(ic, kc * xinv)
        decay = xb[CH - 1:CH, :]
        st = [st[h] * decay[:, h * HD:(h + 1) * HD] + update[h * HD:(h + 1) * HD, h * HD:(h + 1) * HD] for h in range(NH)]
        rs = lax.rsqrt(jnp.mean(o * o, axis=-1, keepdims=True) + EPS)
        ys = o * rs * gn_w * heads_to_rows(gate[rows])
        yb[ci] = [ys[h * CH:(h + 1) * CH] for h in range(NH)]
    yb_full = jnp.concatenate([jnp.concatenate(r, axis=1) for r in yb], axis=0)
    return jnp.concatenate([ya_full, yb_full], axis=1), tuple(st)


def _fwd_mix(proj, w_s, b_s_t, ln_w, ln_b, lower, gn_w, tables, to_gather):
    t = proj.shape[0]
    mt = _row_tile(t, MIX_TILE)
    nt_ = t // mt
    nc = len(tables)
    ng = len(to_gather)

    def body(*refs):
        proj_ref, ws_ref, bs_ref, lw_ref, lb_ref, lo_ref, gn_ref = refs[:7]
        c_refs = refs[7:7 + nc]
        refs = refs[7 + nc:]
        g_in = refs[:ng]
        y_ref, st_ref = refs[ng:2 + ng]
        g_out = refs[2 + ng:2 + 2 * ng]
        state = refs[2 + 2 * ng]
        ga = _Gather2(g_in, g_out, *refs[3 + 2 * ng:])

        @pl.when(pl.program_id(0) == 0)
        def _():
            state[...] = jnp.zeros_like(state)
            ga.start()

        @pl.when(pl.program_id(0) == (3 * nt_) // 4)
        def _():
            ga.forward()

        st = tuple(state[h] for h in range(NH))
        for h in range(NH):
            st_ref[h] = st[h]
        y, new = _mix_tile(proj_ref[...], st, ws_ref[...], bs_ref[...], lw_ref[...], lb_ref[...], lo_ref[...], gn_ref[...],
                           tuple(r[...] for r in c_refs))
        y_ref[...] = y.astype(BF)
        for h in range(NH):
            state[h] = new[h]

        @pl.when(pl.program_id(0) == nt_ - 1)
        def _():
            ga.finish()

    full = lambda a: pl.BlockSpec(a.shape, lambda i, nd=a.ndim: (0,) * nd)
    outs = pl.pallas_call(
        body, name="fwd_mix", grid=(nt_,),
        out_shape=(jax.ShapeDtypeStruct((t, D), BF), jax.ShapeDtypeStruct((nt_, NH, HD, HD), F32),
                   *_exchange_out_shapes(to_gather, False)),
        in_specs=[pl.BlockSpec((mt, D_IN), lambda i: (i, 0)), full(w_s), full(b_s_t), full(ln_w), full(ln_b), full(lower),
                  full(gn_w)] + [full(a) for a in tables] + [_any() for _ in to_gather],
        out_specs=(pl.BlockSpec((mt, D), lambda i: (i, 0)), pl.BlockSpec((None, NH, HD, HD), lambda i: (i, 0, 0, 0)),
                   *[_any() for _ in to_gather]),
        scratch_shapes=[pltpu.VMEM((NH, HD, HD), F32)] + _exchange_sems(ng),
        compiler_params=_params(dimension_semantics=("arbitrary",), has_side_effects=True),
    )(proj, w_s, b_s_t, ln_w, ln_b, lower, gn_w, *tables, *to_gather)
    return outs[0], outs[1], outs[2:]


def _bwd_mix(proj, dycat, states, w_s, b_s_t, ln_w, ln_b, lower, gn_w, tables, to_scatter):
    t = proj.shape[0]
    mt = _row_tile(t, MIX_TILE)
    nt_ = t // mt
    nc = len(tables)
    ns = len(to_scatter)

    def body(*refs):
        proj_ref, dy_ref, st_ref, ws_ref, bs_ref, lw_ref, lb_ref, lo_ref, gn_ref = refs[:9]
        c_refs = refs[9:9 + nc]
        refs = refs[9 + nc:]
        s_in = refs[:ns]
        dproj_ref = refs[ns]
        acc = refs[1 + ns:7 + ns]
        s_out = refs[7 + ns:7 + 2 * ns]
        dstate = refs[7 + 2 * ns]
        ex = _Exchange(s_in, s_out, *refs[8 + 2 * ns:], True)

        @pl.when(pl.program_id(0) == 0)
        def _():
            dstate[...] = jnp.zeros_like(dstate)
            for r in acc:
                r[...] = jnp.zeros_like(r)
            ex.start()

        consts = tuple(r[...] for r in c_refs)

        def f(p, s, ws, bs, lw, lb_, lo, gn):
            return _mix_tile(p, s, ws, bs, lw, lb_, lo, gn, consts)

        st = tuple(st_ref[h] for h in range(NH))
        _, vjp = jax.vjp(f, proj_ref[...], st, ws_ref[...], bs_ref[...], lw_ref[...], lb_ref[...], lo_ref[...], gn_ref[...])
        grads = vjp((dy_ref[...], tuple(dstate[h] for h in range(NH))))
        dproj_ref[...] = grads[0].astype(BF)
        for h in range(NH):
            dstate[h] = grads[1][h]
        for r, gval in zip(acc, grads[2:]):
            r[...] += gval

        @pl.when(pl.program_id(0) == nt_ - 1)
        def _():
            ex.wait()

    full = lambda a: pl.BlockSpec(a.shape, lambda i, nd=a.ndim: (0,) * nd)
    rev = lambda i: (nt_ - 1 - i, 0)
    smalls = (w_s, b_s_t, ln_w, ln_b, lower, gn_w)
    outs = pl.pallas_call(
        body, name="bwd_mix", grid=(nt_,),
        out_shape=(jax.ShapeDtypeStruct((t, D_IN), BF), *[jax.ShapeDtypeStruct(a.shape, F32) for a in smalls],
                   *_exchange_out_shapes(to_scatter, True)),
        in_specs=[pl.BlockSpec((mt, D_IN), rev), pl.BlockSpec((mt, D), rev),
                  pl.BlockSpec((None, NH, HD, HD), lambda i: (nt_ - 1 - i, 0, 0, 0))]
        + [full(a) for a in smalls] + [full(a) for a in tables] + [_any() for _ in to_scatter],
        out_specs=(pl.BlockSpec((mt, D_IN), rev), *[full(a) for a in smalls], *[_any() for _ in to_scatter]),
        scratch_shapes=[pltpu.VMEM((NH, HD, HD), F32)] + _exchange_sems(ns),
        compiler_params=_params(dimension_semantics=("arbitrary",), has_side_effects=True),
    )(proj, dycat, states, w_s, b_s_t, ln_w, ln_b, lower, gn_w, *tables, *to_scatter)
    return outs[0], outs[1:7], outs[7:]


def _row_tile(t, want):
    return want if t % want == 0 else t


def _rms(v):
    rstd = lax.rsqrt(jnp.mean(v * v, axis=-1, keepdims=True) + EPS)
    return v * rstd, rstd


def _rms_bwd(dxhat, xhat, rstd):
    return rstd * (dxhat - xhat * jnp.mean(dxhat * xhat, axis=-1, keepdims=True))


def _colsum(v):
    return jnp.sum(v, axis=0, keepdims=True)


def _fwd_in(x, ada, n1w, w_in_st):
    t = x.shape[0]
    tm = _row_tile(t, 512)

    def body(x_ref, ada_ref, n1_ref, wst_ref, proj_ref, h1_ref, w_ref, sems):
        @pl.when(pl.program_id(0) == 0)
        def _():
            _load_columns(wst_ref, w_ref, sems)

        xh, _ = _rms(x_ref[...])
        h1 = (xh * n1_ref[...]) * (1.0 + ada_ref[1:2, :]) + ada_ref[0:1, :]
        h1b = h1.astype(BF)
        h1_ref[...] = h1b
        proj_ref[...] = _mm(h1b, w_ref[...])

    full = lambda a: pl.BlockSpec(a.shape, lambda i, nd=a.ndim: (0,) * nd)
    return pl.pallas_call(
        body, name="fwd_in", grid=(t // tm,),
        out_shape=(jax.ShapeDtypeStruct((t, D_IN), F32), jax.ShapeDtypeStruct((t, D), BF)),
        in_specs=[pl.BlockSpec((tm, D), lambda i: (i, 0)), full(ada), full(n1w), _any()],
        out_specs=(pl.BlockSpec((tm, D_IN), lambda i: (i, 0)), pl.BlockSpec((tm, D), lambda i: (i, 0))),
        scratch_shapes=[pltpu.VMEM((D, D_IN), BF), pltpu.SemaphoreType.DMA((N_DEV,))],
        compiler_params=_params(dimension_semantics=("arbitrary",)),
    )(x, ada, n1w, w_in_st)


def _fwd_ffn(x, ycat, tgt, ada, n2w, fw, w_out, w_fi, w_fo):
    t = x.shape[0]
    tm = _row_tile(t, 256)

    def body(x_ref, y_ref, t_ref, ada_ref, n2_ref, fw_ref, wo_ref, wi_ref, wf_ref,
             x1_ref, mix_ref, h2_ref, act_ref, gu_ref, dffn_ref, dx2_ref, part_ref):
        @pl.when(pl.program_id(0) == 0)
        def _():
            part_ref[...] = jnp.zeros_like(part_ref)

        g1, sh2, sc2, g2 = ada_ref[2:3, :], ada_ref[3:4, :], ada_ref[4:5, :], ada_ref[5:6, :]
        mix = _mm(y_ref[...], wo_ref[...])
        mix_ref[...] = mix.astype(BF)
        x1 = x_ref[...] + g1 * mix
        x1_ref[...] = x1
        xh2, _ = _rms(x1)
        h2b = ((xh2 * n2_ref[...]) * (1.0 + sc2) + sh2).astype(BF)
        h2_ref[...] = h2b
        ffn = jnp.zeros((tm, D), F32)
        for lo, hi in FF_CHUNKS:
            gate = _nt(h2b, wi_ref[lo:hi, :])
            up = _nt(h2b, wi_ref[D_FF + lo:D_FF + hi, :])
            gu_ref[:, lo:hi] = gate.astype(BF)
            gu_ref[:, D_FF + lo:D_FF + hi] = up.astype(BF)
            a = (_silu(gate) * up).astype(BF)
            act_ref[:, lo:hi] = a
            ffn = ffn + _mm(a, wf_ref[lo:hi, :])
        x2 = x1 + g2 * ffn
        xh3, rstd3 = _rms(x2)
        err = xh3 * fw_ref[...] - t_ref[...]
        dy = err * (1.0 / D)
        dx2 = _rms_bwd(dy * fw_ref[...], xh3, rstd3)
        dx2_ref[...] = dx2
        dffn_ref[...] = (g2 * dx2).astype(BF)
        part_ref[0:1, :] += _colsum(dx2 * ffn)
        part_ref[1:2, :] += _colsum(dy * xh3)
        part_ref[2:3, :] += jnp.zeros((1, D), F32) + (0.5 / D) * jnp.sum(err * err)

    full = lambda a: pl.BlockSpec(a.shape, lambda i, nd=a.ndim: (0,) * nd)
    row = lambda w: pl.BlockSpec((tm, w), lambda i: (i, 0))
    return pl.pallas_call(
        body, name="fwd_ffn", grid=(t // tm,),
        out_shape=(jax.ShapeDtypeStruct((t, D), F32), jax.ShapeDtypeStruct((t, D), BF), jax.ShapeDtypeStruct((t, D), BF),
                   jax.ShapeDtypeStruct((t, D_FF), BF), jax.ShapeDtypeStruct((t, 2 * D_FF), BF),
                   jax.ShapeDtypeStruct((t, D), BF), jax.ShapeDtypeStruct((t, D), F32), jax.ShapeDtypeStruct((8, D), F32)),
        in_specs=[row(D), row(D), row(D), full(ada), full(n2w), full(fw), _whole_vmem(), _whole_vmem(), _whole_vmem()],
        out_specs=(row(D), row(D), row(D), row(D_FF), row(2 * D_FF), row(D), row(D),
                   pl.BlockSpec((8, D), lambda i: (0, 0))),
        compiler_params=_params(dimension_semantics=("arbitrary",)),
    )(x, ycat, tgt, ada, n2w, fw, w_out, w_fi, w_fo)


def _bwd_ffn(x1, gu, dffn, dx2, mix, ada, n2w, w_out, w_fi, w_fo):
    t = x1.shape[0]
    tm = _row_tile(t, 256)

    def body(x1_ref, gu_ref, dffn_ref, dx2_ref, mix_ref, ada_ref, n2_ref, wo_ref, wi_ref, wf_ref,
             dgu_ref, dx1_ref, dmix_ref, dycat_ref, part_ref):
        @pl.when(pl.program_id(0) == 0)
        def _():
            part_ref[...] = jnp.zeros_like(part_ref)

        g1, sc2 = ada_ref[2:3, :], ada_ref[4:5, :]
        dffn = dffn_ref[...]
        dh2 = jnp.zeros((tm, D), F32)
        for lo, hi in FF_CHUNKS:
            gate = gu_ref[:, lo:hi].astype(F32)
            up = gu_ref[:, D_FF + lo:D_FF + hi].astype(F32)
            dact = _nt(dffn, wf_ref[lo:hi, :])
            sg = _sig(gate)
            dgate = (dact * up * (sg * (1.0 + gate * (1.0 - sg)))).astype(BF)
            dup = (dact * (gate * sg)).astype(BF)
            dgu_ref[:, lo:hi] = dgate
            dgu_ref[:, D_FF + lo:D_FF + hi] = dup
            dh2 = dh2 + _mm(dgate, wi_ref[lo:hi, :]) + _mm(dup, wi_ref[D_FF + lo:D_FF + hi, :])
        x1 = x1_ref[...]
        xh2, rstd2 = _rms(x1)
        xn2 = xh2 * n2_ref[...]
        dxn2 = dh2 * (1.0 + sc2)
        dx1 = dx2_ref[...] + _rms_bwd(dxn2 * n2_ref[...], xh2, rstd2)
        dx1_ref[...] = dx1
        dmix = (g1 * dx1).astype(BF)
        dmix_ref[...] = dmix
        dycat_ref[...] = _nt(dmix, wo_ref[...])
        part_ref[0:1, :] += _colsum(dh2)
        part_ref[1:2, :] += _colsum(dh2 * xn2)
        part_ref[2:3, :] += _colsum(dxn2 * xh2)
        part_ref[3:4, :] += _colsum(dx1 * mix_ref[...].astype(F32))

    full = lambda a: pl.BlockSpec(a.shape, lambda i, nd=a.ndim: (0,) * nd)
    row = lambda w: pl.BlockSpec((tm, w), lambda i: (i, 0))
    return pl.pallas_call(
        body, name="bwd_ffn", grid=(t // tm,),
        out_shape=(jax.ShapeDtypeStruct((t, 2 * D_FF), BF), jax.ShapeDtypeStruct((t, D), F32),
                   jax.ShapeDtypeStruct((t, D), BF), jax.ShapeDtypeStruct((t, D), F32), jax.ShapeDtypeStruct((8, D), F32)),
        in_specs=[row(D), row(2 * D_FF), row(D), row(D), row(D), full(ada), full(n2w),
                  _whole_vmem(), _whole_vmem(), _whole_vmem()],
        out_specs=(row(2 * D_FF), row(D), row(D), row(D), pl.BlockSpec((8, D), lambda i: (0, 0))),
        compiler_params=_params(dimension_semantics=("arbitrary",)),
    )(x1, gu, dffn, dx2, mix, ada, n2w, w_out, w_fi, w_fo)


def _bwd_in(x, dproj, dx1, ada, n1w, w_in_st, to_scatter, to_gather):
    t = x.shape[0]
    tm = _row_tile(t, 1024)
    n_t = t // tm
    ns = len(to_scatter)
    ng = len(to_gather)
    nx = ns + ng

    def body(*refs):
        x_ref, dp_ref, dx1_ref, ada_ref, n1_ref, wst_ref = refs[:6]
        x_in = refs[6:6 + nx]
        gx_ref, part_ref = refs[6 + nx:8 + nx]
        x_out = refs[8 + nx:8 + 2 * nx]
        w_ref, sems = refs[8 + 2 * nx:10 + 2 * nx]
        sem_refs = refs[10 + 2 * nx:]
        ex = _Exchange(x_in[:ns], x_out[:ns], *sem_refs[:3], True)
        gx = _Exchange(x_in[ns:], x_out[ns:], *sem_refs[3:], False) if ng else None

        @pl.when(pl.program_id(0) == 0)
        def _():
            ex.start()
            if ng:
                gx.start()
            part_ref[...] = jnp.zeros_like(part_ref)
            _load_columns(wst_ref, w_ref, sems)

        dh1 = _nt(dp_ref[...], w_ref[...])
        xh, rstd = _rms(x_ref[...])
        xn = xh * n1_ref[...]
        dxn = dh1 * (1.0 + ada_ref[1:2, :])
        gx_ref[...] = dx1_ref[...] + _rms_bwd(dxn * n1_ref[...], xh, rstd)
        part_ref[0:1, :] += _colsum(dh1)
        part_ref[1:2, :] += _colsum(dh1 * xn)
        part_ref[2:3, :] += _colsum(dxn * xh)

        @pl.when(pl.program_id(0) == n_t - 1)
        def _():
            ex.wait()
            if ng:
                gx.wait()

    full = lambda a: pl.BlockSpec(a.shape, lambda i, nd=a.ndim: (0,) * nd)
    row = lambda w: pl.BlockSpec((tm, w), lambda i: (i, 0))
    outs = pl.pallas_call(
        body, name="bwd_in", grid=(n_t,),
        out_shape=(jax.ShapeDtypeStruct((t, D), F32), jax.ShapeDtypeStruct((8, D), F32),
                   *_exchange_out_shapes(to_scatter, True), *_exchange_out_shapes(to_gather, False)),
        in_specs=[row(D), row(D_IN), row(D), full(ada), full(n1w), _any()] + [_any() for _ in range(nx)],
        out_specs=(row(D), pl.BlockSpec((8, D), lambda i: (0, 0)), *[_any() for _ in range(nx)]),
        scratch_shapes=[pltpu.VMEM((D, D_IN), BF), pltpu.SemaphoreType.DMA((N_DEV,))] + _exchange_sems(ns)
        + (_exchange_sems(ng) if ng else []),
        compiler_params=_params(dimension_semantics=("arbitrary",), has_side_effects=True),
    )(x, dproj, dx1, ada, n1w, w_in_st, *to_scatter, *to_gather)
    return outs[0], outs[1], outs[2:2 + ns], outs[2 + ns:]


def _wgrad(a, b, name, a_spec, b_spec, out_shape, out_spec, grid, acc_shape, split=1, to_gather=()):
    n_j, n_t = grid
    ng = len(to_gather)

    def body(*refs):
        a_ref, b_ref = refs[:2]
        g_in = refs[2:2 + ng]
        o_ref = refs[2 + ng]
        g_out = refs[3 + ng:3 + 2 * ng]
        acc = refs[3 + 2 * ng]
        first = (pl.program_id(0) == 0) & (pl.program_id(1) == 0)
        last = (pl.program_id(0) == n_j - 1) & (pl.program_id(1) == n_t - 1)
        if ng:
            gx = _Exchange(g_in, g_out, *refs[4 + 2 * ng:], False)

            @pl.when(first)
            def _():
                gx.start()

        @pl.when(pl.program_id(1) == 0)
        def _():
            acc[...] = jnp.zeros_like(acc)

        acc[...] += _tn(a_ref[...], b_ref[...])

        @pl.when(pl.program_id(1) == n_t - 1)
        def _():
            if split == 1:
                o_ref[...] = acc[...].astype(BF)
            else:
                w = acc_shape[1] // split
                for s in range(split):
                    o_ref[s] = acc[:, s * w:(s + 1) * w].astype(BF)

        if ng:
            @pl.when(last)
            def _():
                gx.wait()

    outs = pl.pallas_call(
        body, name=name, grid=grid,
        out_shape=(jax.ShapeDtypeStruct(out_shape, BF), *_exchange_out_shapes(to_gather, False)),
        in_specs=[a_spec, b_spec] + [_any() for _ in to_gather], out_specs=(out_spec, *[_any() for _ in to_gather]),
        scratch_shapes=[pltpu.VMEM(acc_shape, F32)] + (_exchange_sems(ng) if ng else []),
        compiler_params=_params(dimension_semantics=("arbitrary", "arbitrary"), has_side_effects=bool(ng)),
    )(a, b, *to_gather)
    return (outs[0], outs[1:]) if ng else outs[0]


def _adamw_math(w, g, m, v):
    m = ADAM_B1 * m + (1.0 - ADAM_B1) * g
    v = ADAM_B2 * v + (1.0 - ADAM_B2) * (g * g)
    m_hat = m / (1.0 - ADAM_B1 ** ADAM_STEP)
    v_hat = v / (1.0 - ADAM_B2 ** ADAM_STEP)
    delta = -ADAM_LR * (m_hat / (jnp.sqrt(v_hat) + ADAM_EPS) + ADAM_WD * w)
    return delta, m, v


def _adamw_recv(w, m, v, recv, name, tr):
    r, c = w.shape

    def body(w_ref, m_ref, v_ref, r_ref, g_ref, d_ref, nm_ref, nv_ref):
        g = r_ref[0].astype(F32)
        for k in range(1, N_DEV):
            g = g + r_ref[k].astype(F32)
        g_ref[...] = g
        d_ref[...], nm_ref[...], nv_ref[...] = _adamw_math(w_ref[...], g, m_ref[...], v_ref[...])

    row = pl.BlockSpec((tr, c), lambda i: (i, 0))
    return pl.pallas_call(
        body, name=name, grid=(r // tr,),
        out_shape=tuple(jax.ShapeDtypeStruct((r, c), F32) for _ in range(4)),
        in_specs=[row, row, row, pl.BlockSpec((N_DEV, tr, c), lambda i: (0, i, 0))],
        out_specs=(row, row, row, row),
        compiler_params=_params(dimension_semantics=("arbitrary",)),
    )(w, m, v, recv)


def _adamw_ada(w, m, v, cact, dada_cols):
    r, c = w.shape
    tr = 256

    def body(w_ref, m_ref, v_ref, ca_ref, da_ref, g_ref, d_ref, nm_ref, nv_ref):
        g = _tn(ca_ref[...].astype(BF), da_ref[...].astype(BF))
        g_ref[...] = g
        d_ref[...], nm_ref[...], nv_ref[...] = _adamw_math(w_ref[...], g, m_ref[...], v_ref[...])

    row = pl.BlockSpec((tr, c), lambda i: (i, 0))
    return pl.pallas_call(
        body, name="adamw_ada", grid=(r // tr,),
        out_shape=tuple(jax.ShapeDtypeStruct((r, c), F32) for _ in range(4)),
        in_specs=[row, row, row, pl.BlockSpec((N_DEV, tr), lambda i: (0, i)), pl.BlockSpec(dada_cols.shape, lambda i: (0, 0))],
        out_specs=(row, row, row, row),
        compiler_params=_params(dimension_semantics=("arbitrary",)),
    )(w, m, v, cact, dada_cols)


def _adamw_small(gathered, wmv):
    n_g = len(gathered)
    n_p = len(wmv)
    flat = [a for trip in wmv for a in trip]

    def body(*refs):
        g_refs = refs[:n_g]
        p_refs = refs[n_g:n_g + 3 * n_p]
        o_refs = refs[n_g + 3 * n_p:]

        def total(ref):
            s = ref[0]
            for k in range(1, N_DEV):
                s = s + ref[k]
            return s

        f3, b3, b1, dws, dbs, dlnw, dlnb, dlo, dgn = [total(r) for r in g_refs]
        dada_rows = [b1[0:1], b1[1:2], b3[3:4], b3[0:1], b3[1:2], f3[0:1]]
        for r, g in enumerate(dada_rows):
            cs = slice(r * D, (r + 1) * D)
            w, m, v = p_refs[0][:, cs], p_refs[1][:, cs], p_refs[2][:, cs]
            o_refs[0][:, cs] = g
            o_refs[1][:, cs], o_refs[2][:, cs], o_refs[3][:, cs] = _adamw_math(w, g, m, v)
        grads = [None, b1[2:3], dws, dbs, dlnw, dlnb, dlo, dgn, b3[2:3], f3[1:2]]
        for i, g in enumerate(grads):
            if g is None:
                continue
            w, m, v = p_refs[3 * i][...], p_refs[3 * i + 1][...], p_refs[3 * i + 2][...]
            o_refs[4 * i][...] = g
            o_refs[4 * i + 1][...], o_refs[4 * i + 2][...], o_refs[4 * i + 3][...] = _adamw_math(w, g, m, v)
        o_refs[4 * n_p][...] = jnp.zeros((8, 128), F32) + f3[2:3, 0:128]

    out_shape = []
    for w, _, _ in wmv:
        out_shape += [jax.ShapeDtypeStruct(w.shape, F32)] * 4
    out_shape.append(jax.ShapeDtypeStruct((8, 128), F32))
    n_in = n_g + 3 * n_p
    return pl.pallas_call(
        body, name="adamw_small",
        out_shape=tuple(out_shape),
        in_specs=[_whole_vmem()] * n_in, out_specs=tuple(_whole_vmem() for _ in out_shape),
        compiler_params=_params(),
    )(*gathered, *flat)


def kernel(x, c, w_ada, b_ada, norm1_w, w_in, w_s, b_s, v_ln_w, v_ln_b, lower_bounds, gn_w, w_out, norm2_w, w_ffn_in, w_ffn_out, final_norm_w, loss_target, m_w_ada, m_b_ada, m_norm1_w, m_w_in, m_w_s, m_b_s, m_v_ln_w, m_v_ln_b, m_lower_bounds, m_gn_w, m_w_out, m_norm2_w, m_w_ffn_in, m_w_ffn_out, m_final_norm_w, v_w_ada, v_b_ada, v_norm1_w, v_w_in, v_w_s, v_b_s, v_v_ln_w, v_v_ln_b, v_lower_bounds, v_gn_w, v_w_out, v_norm2_w, v_w_ffn_in, v_w_ffn_out, v_final_norm_w):
    me = 4 * lax.axis_index("x") + 2 * lax.axis_index("y") + lax.axis_index("c")
    t = x.shape[1]
    x2d = x.reshape(t, D)
    tgt = loss_target.reshape(t, D)
    ada_cols = w_ada.shape[2]

    tp = lambda a: jnp.swapaxes(a[0], 0, 1)
    win_b, wout_b, wfi_b, wfo_b = _cast_bf16([w_in[0], w_out[0], tp(w_ffn_in), w_ffn_out[0]], "cast_weights")

    b_cols = lax.dynamic_slice(b_ada, (0, me * ada_cols), (1, ada_cols))
    win_st, ada_st, cact = _startup(win_b, c, w_ada[0], b_cols)
    ada = lax.dynamic_index_in_dim(ada_st, me, axis=1, keepdims=False).reshape(6, D)

    tables = _decay_tables()
    ws3 = w_s[0]
    bs_t = b_s[0].T

    proj, h1 = _fwd_in(x2d, ada, norm1_w, win_st)
    ycat, states, (wout_st, wfi_st, wfo_st) = _fwd_mix(proj, ws3, bs_t, v_ln_w, v_ln_b, lower_bounds, gn_w, tables,
                                                       [wout_b, wfi_b, wfo_b])
    w_out_full = wout_st.reshape(D, D)
    w_fo_full = wfo_st.reshape(D_FF, D)
    w_fi_full = wfi_st.reshape(2 * D_FF, D)
    x1, mixb, h2, act, gu, dffn, dx2, part_f = _fwd_ffn(x2d, ycat, tgt, ada, norm2_w, final_norm_w.reshape(1, D),
                                                        w_out_full, w_fi_full, w_fo_full)
    dgu, dx1, dmix, dycat, part_b3 = _bwd_ffn(x1, gu, dffn, dx2, mixb, ada, norm2_w, w_out_full, w_fi_full, w_fo_full)
    tk = _row_tile(t, 2048)
    n_t = t // tk
    win_cols = D_IN // N_DEV
    dwout = _wgrad(ycat, dmix, "wgrad_out",
                   pl.BlockSpec((tk, D), lambda j, i: (i, 0)), pl.BlockSpec((tk, D), lambda j, i: (i, 0)),
                   (D, D), pl.BlockSpec((D, D), lambda j, i: (0, 0)), (1, n_t), (D, D))
    dwfi = _wgrad(dgu, h2, "wgrad_ffn_in",
                  pl.BlockSpec((tk, FF_PAIR), lambda j, i: (i, j)), pl.BlockSpec((tk, D), lambda j, i: (i, 0)),
                  (4, FF_PAIR, D), pl.BlockSpec((None, FF_PAIR, D), lambda j, i: (j, 0, 0)), (4, n_t), (FF_PAIR, D))
    dwfo = _wgrad(act, dffn, "wgrad_ffn_out",
                  pl.BlockSpec((tk, FF_PAIR), lambda j, i: (i, j)), pl.BlockSpec((tk, D), lambda j, i: (i, 0)),
                  (2, FF_PAIR, D), pl.BlockSpec((None, FF_PAIR, D), lambda j, i: (j, 0, 0)), (2, n_t), (FF_PAIR, D))
    dproj, (dws, dbs_t, dlnw, dlnb, dlower, dgnw), (r_out, r_fi, r_fo) = _bwd_mix(
        proj, dycat, states, ws3, bs_t, v_ln_w, v_ln_b, lower_bounds, gn_w, tables,
        [dwout.reshape(N_DEV, D // N_DEV, D), dwfi.reshape(N_DEV, FF_BLK, D), dwfo.reshape(N_DEV, D_FF // N_DEV, D)])
    dwin, early = _wgrad(h1, dproj, "wgrad_in",
                         pl.BlockSpec((tk, D), lambda j, i: (i, 0)), pl.BlockSpec((tk, 4 * win_cols), lambda j, i: (i, j)),
                         (N_DEV, D, win_cols), pl.BlockSpec((4, D, win_cols), lambda j, i: (j, 0, 0)), (N_DEV // 4, n_t),
                         (D, 4 * win_cols), split=4, to_gather=[part_f, part_b3, dws, dbs_t, dlnw, dlnb, dlower, dgnw])
    grad_x, part_b1, (r_in,), _ = _bwd_in(x2d, dproj, dx1, ada, norm1_w, win_st, [dwin], [])

    g_w_in, d_w_in, nm_w_in, nv_w_in = _adamw_recv(w_in[0], m_w_in[0], v_w_in[0], r_in, "adamw_w_in", 256)
    g_w_out, d_w_out, nm_w_out, nv_w_out = _adamw_recv(w_out[0], m_w_out[0], v_w_out[0], r_out, "adamw_w_out", 128)
    fi_t = _adamw_recv(tp(w_ffn_in), tp(m_w_ffn_in), tp(v_w_ffn_in), r_fi, "adamw_w_ffn_in", 176)
    g_w_fi, d_w_fi, nm_w_fi, nv_w_fi = [jnp.swapaxes(a, 0, 1) for a in fi_t]
    g_w_fo, d_w_fo, nm_w_fo, nv_w_fo = _adamw_recv(w_ffn_out[0], m_w_ffn_out[0], v_w_ffn_out[0], r_fo, "adamw_w_ffn_out", 176)

    (b1_all,) = _exchange([part_b1], "gather_small", False, True)
    gathered = [early[0], early[1], b1_all, *early[2:]]
    f3_all, b3_all = gathered[0], gathered[1]
    dada_all = jnp.stack([b1_all[:, 0], b1_all[:, 1], b3_all[:, 3], b3_all[:, 0], b3_all[:, 1], f3_all[:, 0]], axis=1)
    dada_cols = lax.dynamic_slice(dada_all.reshape(N_DEV, 6 * D), (0, me * ada_cols), (N_DEV, ada_cols))
    g_w_ada, d_w_ada, nm_w_ada, nv_w_ada = _adamw_ada(w_ada[0], m_w_ada[0], v_w_ada[0], cact, dada_cols)

    r1 = lambda a: a.reshape(1, D)
    tr = lambda a: a[0].T
    wmv = [
        (b_ada, m_b_ada, v_b_ada),
        (norm1_w, m_norm1_w, v_norm1_w),
        (w_s[0], m_w_s[0], v_w_s[0]),
        (tr(b_s), tr(m_b_s), tr(v_b_s)),
        (v_ln_w, m_v_ln_w, v_v_ln_w),
        (v_ln_b, m_v_ln_b, v_v_ln_b),
        (lower_bounds, m_lower_bounds, v_lower_bounds),
        (gn_w, m_gn_w, v_gn_w),
        (norm2_w, m_norm2_w, v_norm2_w),
        (r1(final_norm_w), r1(m_final_norm_w), r1(v_final_norm_w)),
    ]
    small = _adamw_small(gathered, wmv)
    loss = small[-1][0, 0]

    def unshape(i, a):
        if i == 2:
            return a.reshape(1, NH, BLK, BLK)
        if i == 3:
            return a.T.reshape(1, NH, BLK)
        if i == 9:
            return a.reshape(D)
        return a

    def small_out(kind):
        return [unshape(i, small[4 * i + kind]) for i in range(len(wmv))]

    e3 = lambda a: a[None]
    big = {
        0: (e3(g_w_ada), e3(g_w_in), e3(g_w_out), e3(g_w_fi), e3(g_w_fo)),
        1: (e3(d_w_ada), e3(d_w_in), e3(d_w_out), e3(d_w_fi), e3(d_w_fo)),
        2: (e3(nm_w_ada), e3(nm_w_in), e3(nm_w_out), e3(nm_w_fi), e3(nm_w_fo)),
        3: (e3(nv_w_ada), e3(nv_w_in), e3(nv_w_out), e3(nv_w_fi), e3(nv_w_fo)),
    }

    def ordered(kind):
        s = small_out(kind)
        b_ = big[kind]
        return [b_[0], s[0], s[1], b_[1], s[2], s[3], s[4], s[5], s[6], s[7], b_[2], s[8], b_[3], b_[4], s[9]]

    return (loss, grad_x.reshape(1, t, D), *ordered(0), *ordered(1), *ordered(2), *ordered(3))
```

```python
import numpy as np
import jax
import jax.numpy as jnp
from jax import lax
from jax.experimental import pallas as pl
from jax.experimental.pallas import tpu as pltpu

F32 = jnp.float32
BF = jnp.bfloat16
MESH = pl.DeviceIdType.MESH

N_DEV = 8
D = 1024
D_IN = 3072
D_FF = 2816
FF_BLK = D_FF // 4
FF_CHUNKS = ((0, D_FF),)
FF_PAIR = 2 * FF_BLK
CH = 64
BLK = 128
MIX_TILE = 256
NH = 4
HD = 128
EPS = 1e-6
LEVELS = (32, 16, 8, 4, 2, 1)

ADAM_LR = 0.001
ADAM_B1 = 0.9
ADAM_B2 = 0.999
ADAM_EPS = 1e-08
ADAM_WD = 0.01
ADAM_STEP = 10

VMEM_LIMIT = 56 * 1024 * 1024


def _params(**kw):
    return pltpu.CompilerParams(vmem_limit_bytes=VMEM_LIMIT, **kw)


def _whole_vmem():
    return pl.BlockSpec(memory_space=pltpu.VMEM)


def _any():
    return pl.BlockSpec(memory_space=pl.ANY)


class _Exchange:
    def __init__(self, ins, outs, send_sems, recv_sems, local_sems, scatter):
        self.ins, self.outs, self.scatter = ins, outs, scatter
        self.send_sems, self.recv_sems, self.local_sems = send_sems, recv_sems, local_sems
        x, y, c = lax.axis_index("x"), lax.axis_index("y"), lax.axis_index("c")
        self.me = 4 * x + 2 * y + c
        self.peers = []
        for k in range(1, N_DEV):
            peer = (1 - x if (k >> 2) & 1 else x, 1 - y if (k >> 1) & 1 else y, 1 - c if k & 1 else c)
            self.peers.append((peer, 4 * peer[0] + 2 * peer[1] + peer[2]))

    def _src(self, a, idx):
        return self.ins[a].at[idx] if self.scatter else self.ins[a]

    def _local(self):
        return [pltpu.make_async_copy(self._src(a, self.me), self.outs[a].at[self.me], self.local_sems.at[a])
                for a in range(len(self.ins))]

    def _remote(self, a, k, dst_slot):
        peer, peer_idx = self.peers[k]
        return pltpu.make_async_remote_copy(
            src_ref=self._src(a, peer_idx), dst_ref=self.outs[a].at[dst_slot],
            send_sem=self.send_sems.at[a, k], recv_sem=self.recv_sems.at[a, k],
            device_id=peer, device_id_type=MESH)

    def start(self):
        for cp in self._local():
            cp.start()
        for k in range(N_DEV - 1):
            for a in range(len(self.ins)):
                self._remote(a, k, self.me).start()

    def wait(self):
        for k in range(N_DEV - 1):
            for a in range(len(self.ins)):
                self._remote(a, k, self.peers[k][1]).wait()
        for cp in self._local():
            cp.wait()


class _Gather2:
    def __init__(self, ins, outs, send_sems, recv_sems, local_sems):
        self.ins, self.outs = ins, outs
        self.send_sems, self.recv_sems, self.local_sems = send_sems, recv_sems, local_sems
        x, y, c = lax.axis_index("x"), lax.axis_index("y"), lax.axis_index("c")
        self.c = c
        self.me = 4 * x + 2 * y + c
        self.sibling = (x, y, 1 - c)
        self.chips = [(1 - x, y), (x, 1 - y), (1 - x, 1 - y)]

    @staticmethod
    def _idx(px, py, pc):
        return 4 * px + 2 * py + pc

    def _copy(self, a, k, slot, to, own):
        src = self.ins[a] if own else self.outs[a].at[slot]
        return pltpu.make_async_remote_copy(
            src_ref=src, dst_ref=self.outs[a].at[slot],
            send_sem=self.send_sems.at[a, k], recv_sem=self.recv_sems.at[a, k],
            device_id=to, device_id_type=MESH)

    def _local(self):
        return [pltpu.make_async_copy(self.ins[a], self.outs[a].at[self.me], self.local_sems.at[a])
                for a in range(len(self.ins))]

    def start(self):
        for cp in self._local():
            cp.start()
        for a in range(len(self.ins)):
            self._copy(a, 0, self.me, self.sibling, True).start()
            for j, chip in enumerate(self.chips):
                self._copy(a, 1 + j, self.me, (*chip, self.c), True).start()

    def forward(self):
        for j, chip in enumerate(self.chips):
            for a in range(len(self.ins)):
                slot = self._idx(*chip, self.c)
                self._copy(a, 1 + j, slot, (*chip, self.c), True).wait_recv()
                self._copy(a, 4 + j, slot, self.sibling, False).start()

    def finish(self):
        for a in range(len(self.ins)):
            self._copy(a, 0, self._idx(*self.sibling), self.sibling, True).wait_recv()
            for j, chip in enumerate(self.chips):
                self._copy(a, 4 + j, self._idx(*chip, 1 - self.c), self.sibling, False).wait_recv()
            self._copy(a, 0, self.me, self.sibling, True).wait_send()
            for j, chip in enumerate(self.chips):
                self._copy(a, 1 + j, self.me, (*chip, self.c), True).wait_send()
                self._copy(a, 4 + j, self._idx(*chip, self.c), self.sibling, False).wait_send()
        for cp in self._local():
            cp.wait()


def _exchange_sems(n):
    return [pltpu.SemaphoreType.DMA((n, N_DEV - 1)), pltpu.SemaphoreType.DMA((n, N_DEV - 1)), pltpu.SemaphoreType.DMA((n,))]


def _exchange_out_shapes(arrays, scatter):
    return [jax.ShapeDtypeStruct(a.shape if scatter else (N_DEV,) + a.shape, a.dtype) for a in arrays]


def _exchange(arrays, name, scatter, in_vmem):
    n = len(arrays)

    def body(*refs):
        ex = _Exchange(refs[:n], refs[n:2 * n], *refs[2 * n:], scatter)
        ex.start()
        ex.wait()

    spec = _whole_vmem if in_vmem else _any
    return pl.pallas_call(
        body, name=name,
        out_shape=tuple(_exchange_out_shapes(arrays, scatter)),
        in_specs=[spec() for _ in arrays],
        out_specs=tuple(spec() for _ in arrays),
        scratch_shapes=_exchange_sems(n),
        compiler_params=_params(has_side_effects=True),
    )(*arrays)


def _cast_bf16(arrays, name):
    n = len(arrays)

    def body(*refs):
        for i in range(n):
            refs[n + i][...] = refs[i][...].astype(BF)

    return pl.pallas_call(
        body, name=name,
        out_shape=tuple(jax.ShapeDtypeStruct(a.shape, BF) for a in arrays),
        in_specs=[_whole_vmem() for _ in arrays],
        out_specs=tuple(_whole_vmem() for _ in arrays),
        compiler_params=_params(),
    )(*arrays)


def _load_columns(stacked_ref, full_ref, sems):
    c = stacked_ref.shape[2]
    cps = [pltpu.make_async_copy(stacked_ref.at[d], full_ref.at[:, pl.ds(d * c, c)], sems.at[d]) for d in range(N_DEV)]
    for cp in cps:
        cp.start()
    for cp in cps:
        cp.wait()


def _sig(v):
    return 0.5 * jnp.tanh(0.5 * v) + 0.5


@jax.custom_vjp
def _silu(v):
    return v * _sig(v)


def _silu_fwd(v):
    return _silu(v), v


def _silu_bwd(v, g):
    s = _sig(v)
    return (g * (s * (1.0 + v * (1.0 - s))),)


_silu.defvjp(_silu_fwd, _silu_bwd)


@jax.custom_vjp
def _sigmoid_rel(v):
    e = jnp.exp(-jnp.abs(v))
    d = 1.0 + e
    r = pl.reciprocal(d, approx=True)
    r = r * (2.0 - d * r)
    r = r * (2.0 - d * r)
    return jnp.where(v >= 0.0, r, e * r)


def _sigmoid_rel_fwd(v):
    s = _sigmoid_rel(v)
    return s, s


def _sigmoid_rel_bwd(s, g):
    return (g * (s * (1.0 - s)),)


_sigmoid_rel.defvjp(_sigmoid_rel_fwd, _sigmoid_rel_bwd)


def _gelu(v):
    return 0.5 * v * (1.0 + lax.erf(v * 0.7071067811865476))


def _dot(a, b, ca, cb):
    return lax.dot_general(a, b, (((ca,), (cb,)), ((), ())), preferred_element_type=F32)


@jax.custom_vjp
def _mm(a, b):
    return _dot(a, b, 1, 0)


def _mm_fwd(a, b):
    return _dot(a, b, 1, 0), (a, b)


def _mm_bwd(res, g):
    a, b = res
    gb = g.astype(BF)
    return _dot(gb, b, 1, 1).astype(a.dtype), _dot(a, gb, 0, 0).astype(b.dtype)


_mm.defvjp(_mm_fwd, _mm_bwd)


@jax.custom_vjp
def _nt(a, b):
    return _dot(a, b, 1, 1)


def _nt_fwd(a, b):
    return _dot(a, b, 1, 1), (a, b)


def _nt_bwd(res, g):
    a, b = res
    gb = g.astype(BF)
    return _dot(gb, b, 1, 0).astype(a.dtype), _dot(gb, a, 0, 0).astype(b.dtype)


_nt.defvjp(_nt_fwd, _nt_bwd)


@jax.custom_vjp
def _tn(a, b):
    return _dot(a, b, 0, 0)


def _tn_fwd(a, b):
    return _dot(a, b, 0, 0), (a, b)


def _tn_bwd(res, g):
    a, b = res
    gb = g.astype(BF)
    return _dot(b, gb, 1, 1).astype(a.dtype), _dot(a, gb, 1, 0).astype(b.dtype)


_tn.defvjp(_tn_fwd, _tn_bwd)


def _make_cast_dot(ca, cb, da_dims, db_dims):
    @jax.custom_vjp
    def dot(a, b):
        return _dot(a.astype(BF), b.astype(BF), ca, cb)

    def fwd(a, b):
        ab, bb = a.astype(BF), b.astype(BF)
        return _dot(ab, bb, ca, cb), (ab, bb)

    def bwd(res, g):
        ops = {"a": res[0], "b": res[1], "g": g.astype(BF)}
        return (_dot(ops[da_dims[0]], ops[da_dims[1]], da_dims[2], da_dims[3]),
                _dot(ops[db_dims[0]], ops[db_dims[1]], db_dims[2], db_dims[3]))

    dot.defvjp(fwd, bwd)
    return dot


_mm_c = _make_cast_dot(1, 0, ("g", "b", 1, 1), ("a", "g", 0, 0))
_nt_c = _make_cast_dot(1, 1, ("g", "b", 1, 0), ("g", "a", 0, 0))
_tn_c = _make_cast_dot(0, 0, ("b", "g", 1, 1), ("a", "g", 1, 0))


def _startup(win_b, c, w_ada, b_cols):
    ncol = w_ada.shape[1]

    def body(win_ref, c_ref, w_ref, b_ref, winst_ref, adast_ref, cact_ref, call_ref, blk_ref,
             ws, wr, wl, cs, cr, cl, as_, ar, al):
        big = _Gather2([win_ref], [winst_ref], ws, wr, wl)
        big.start()
        gc = _Exchange([c_ref], [call_ref], cs, cr, cl, False)
        gc.start()
        gc.wait()
        ca = _silu(call_ref[...].reshape(N_DEV, D))
        cact_ref[...] = ca
        blk_ref[...] = _mm(ca.astype(BF), w_ref[...].astype(BF)) + b_ref[...]
        ga = _Exchange([blk_ref], [adast_ref], as_, ar, al, False)
        ga.start()
        ga.wait()
        big.forward()
        big.finish()

    return pl.pallas_call(
        body, name="startup",
        out_shape=(jax.ShapeDtypeStruct((N_DEV,) + win_b.shape, BF), jax.ShapeDtypeStruct((N_DEV, N_DEV, ncol), F32),
                   jax.ShapeDtypeStruct((N_DEV, D), F32)),
        in_specs=[_any(), _whole_vmem(), _whole_vmem(), _whole_vmem()],
        out_specs=(_any(), _whole_vmem(), _whole_vmem()),
        scratch_shapes=[pltpu.VMEM((N_DEV, 1, D), F32), pltpu.VMEM((N_DEV, ncol), F32)]
        + _exchange_sems(1) + _exchange_sems(1) + _exchange_sems(1),
        compiler_params=_params(has_side_effects=True),
    )(win_b, c, w_ada, b_cols)


def _decay_tables():
    t = np.arange(CH)
    tri = (t[None, :] <= t[:, None]).astype(np.float32)
    masks = []
    for h in LEVELS:
        m = (t // (2 * h)) * (2 * h) + h
        upper = t >= m
        same = (t[:, None] // (2 * h)) == (t[None, :] // (2 * h))
        masks.append(same & upper[:, None] & (~upper)[None, :])
    masks.append(np.eye(CH, dtype=bool))
    lv = [np.where((t % (2 * h)) >= h, 1.0, -1.0) for h in LEVELS[:4]]
    m4 = t % 4
    lv += [(m4 == 0) * 1.0, (m4 >= 2) * 1.0, (m4 == 3) * 1.0, (t % 2 == 1) * 1.0]
    lvl = np.broadcast_to(np.stack(lv)[:, :, None], (8, CH, D // 2)).astype(np.float32)
    cid = np.arange(BLK) // CH
    gmask = (cid[:, None] >= cid[None, :]).astype(np.float32)
    masks = np.stack(masks).astype(np.float32)
    stacked = np.zeros((masks.shape[0], NH * CH, NH * CH), np.float32)
    for h in range(NH):
        stacked[:, h * CH:(h + 1) * CH, h * CH:(h + 1) * CH] = masks
    return (jnp.asarray(tri, BF), jnp.asarray(stacked), jnp.asarray(gmask), jnp.asarray(lvl))


def _split2(v):
    v1 = v.astype(BF)
    return v1, (v - v1.astype(F32)).astype(BF)


@jax.custom_vjp
def _cumsum_mm(tri, v):
    p1, p2 = _split2(v)
    return _dot(tri, p1, 1, 0) + _dot(tri, p2, 1, 0)


def _cumsum_mm_fwd(tri, v):
    return _cumsum_mm(tri, v), tri


def _cumsum_mm_bwd(tri, g):
    p1, p2 = _split2(g)
    return jnp.zeros_like(tri), _dot(tri, p1, 0, 0) + _dot(tri, p2, 0, 0)


_cumsum_mm.defvjp(_cumsum_mm_fwd, _cumsum_mm_bwd)


def _make_row_roll(shift):
    @jax.custom_vjp
    def roll(x):
        return pltpu.roll(x, shift % CH, 0)

    def fwd(x):
        return roll(x), None

    def bwd(_, g):
        return (pltpu.roll(g, (-shift) % CH, 0),)

    roll.defvjp(fwd, bwd)
    return roll


_prev_row = _make_row_roll(1)
_next_row = _make_row_roll(-1)


def _mix_tile(proj, state, w_s, b_s_t, ln_w, ln_b, lower, gn_w, consts):
    tri, masks, gmask, lvl = consts
    mt = proj.shape[0]
    u = proj[:, 0:512]
    v = proj[:, 512:1024]
    q = proj[:, 1024:1536]
    fl = proj[:, 1536:2048]
    inp = proj[:, 2048:2560]
    g = proj[:, 2560:3072]

    ug = _gelu(u)
    vg = _gelu(v)
    mu = jnp.mean(vg, axis=-1, keepdims=True)
    vc = vg - mu
    var = jnp.mean(vc * vc, axis=-1, keepdims=True)
    vn = vc * lax.rsqrt(var + EPS) * ln_w + ln_b
    wsm = [w_s[h] * gmask for h in range(NH)]
    ya = [[None] * NH for _ in range(mt // BLK)]
    for bi in range(mt // BLK):
        rows = slice(bi * BLK, (bi + 1) * BLK)
        for h in range(NH):
            cols = slice(h * HD, (h + 1) * HD)
            ya[bi][h] = ug[rows, cols] * (_mm_c(wsm[h], vn[rows, cols]) + b_s_t[:, h:h + 1])
    ya_full = jnp.concatenate([jnp.concatenate(r, axis=1) for r in ya], axis=0)

    l0 = lower[0:1, :]
    l1 = lower[1:2, :]
    mx = jnp.maximum(l0, l1)
    e0 = jnp.exp(l0 - mx)
    e1 = jnp.exp(l1 - mx)
    lb = e0 / (e0 + e1)
    qf = _silu(q)
    f = lb + (1.0 - lb) * _sigmoid_rel(fl)
    logf = jnp.log(f)
    kk = 1.0 - f
    gate = _silu(g)
    nl = len(LEVELS)
    half = D // 2
    heads_to_rows = lambda a: jnp.concatenate([a[:, h * HD:(h + 1) * HD] for h in range(NH)], axis=0)
    st = list(state)
    yb = [[None] * NH for _ in range(mt // CH)]
    for ci in range(mt // CH):
        rows = slice(ci * CH, (ci + 1) * CH)
        lc = logf[rows]
        b = _cumsum_mm(tri, lc)
        xb = jnp.exp(b)
        xinv = jnp.exp(b[CH - 1:CH, :] - b)
        xl = []
        for i, hs in enumerate(LEVELS[:3]):
            refs = [jnp.broadcast_to(b[r:r + 1, :], (2 * hs, half)) for r in range(hs - 1, CH, 2 * hs)]
            bref = refs[0] if len(refs) == 1 else jnp.concatenate(refs, axis=0)
            xl.append(jnp.exp(lvl[i] * (b - bref)))
        b3 = b.reshape(CH // 8, 8, half)
        bref = jnp.broadcast_to(b3[:, 3:4, :], (CH // 8, 8, half)).reshape(CH, half)
        xl.append(jnp.exp(lvl[3] * (b - bref)))
        xl.append(jnp.exp(lvl[4] * _next_row(lc) + lvl[5] * lc + lvl[6] * _prev_row(lc)))
        xl.append(jnp.exp(lvl[7] * lc))
        qc = qf[rows]
        kc = kk[rows]
        ic = inp[rows]
        zsrc = [jnp.concatenate([(qc if (r0 // hs) % 2 == 1 else kc)[r0:r0 + hs] for r0 in range(0, CH, hs)], axis=0)
                for hs in LEVELS[:3]]
        qs, ks = heads_to_rows(qc), heads_to_rows(kc)
        carried = _nt_c(qs * heads_to_rows(xb), jnp.concatenate(st, axis=0))
        inter = jnp.concatenate([carried[h * CH:(h + 1) * CH, h * HD:(h + 1) * HD] for h in range(NH)], axis=0)
        attn = masks[nl] * _nt_c(qs, ks)
        for li in range(nl):
            xs = heads_to_rows(xl[li])
            if li < 3:
                z = heads_to_rows(zsrc[li]) * xs
                pairs = _nt_c(z, z)
            else:
                pairs = _nt_c(qs * xs, ks * xs)
            attn = attn + masks[li] * pairs
        o = inter + _mm_c(attn, heads_to_rows(ic))
        update = _tn_c(ic, kc * xinv)
        decay = xb[CH - 1:CH, :]
        st = [st[h] * decay[:, h * HD:(h + 1) * HD] + update[h * HD:(h + 1) * HD, h * HD:(h + 1) * HD] for h in range(NH)]
        rs = lax.rsqrt(jnp.mean(o * o, axis=-1, keepdims=True) + EPS)
        ys = o * rs * gn_w * heads_to_rows(gate[rows])
        yb[ci] = [ys[h * CH:(h + 1) * CH] for h in range(NH)]
    yb_full = jnp.concatenate([jnp.concatenate(r, axis=1) for r in yb], axis=0)
    return jnp.concatenate([ya_full, yb_full], axis=1), tuple(st)


def _fwd_mix(proj, w_s, b_s_t, ln_w, ln_b, lower, gn_w, tables, to_gather):
    t = proj.shape[0]
    mt = _row_tile(t, MIX_TILE)
    nt_ = t // mt
    nc = len(tables)
    ng = len(to_gather)

    def body(*refs):
        proj_ref, ws_ref, bs_ref, lw_ref, lb_ref, lo_ref, gn_ref = refs[:7]
        c_refs = refs[7:7 + nc]
        refs = refs[7 + nc:]
        g_in = refs[:ng]
        y_ref, st_ref = refs[ng:2 + ng]
        g_out = refs[2 + ng:2 + 2 * ng]
        state = refs[2 + 2 * ng]
        ga = _Gather2(g_in, g_out, *refs[3 + 2 * ng:])

        @pl.when(pl.program_id(0) == 0)
        def _():
            state[...] = jnp.zeros_like(state)
            ga.start()

        @pl.when(pl.program_id(0) == (3 * nt_) // 4)
        def _():
            ga.forward()

        st = tuple(state[h] for h in range(NH))
        for h in range(NH):
            st_ref[h] = st[h]
        y, new = _mix_tile(proj_ref[...], st, ws_ref[...], bs_ref[...], lw_ref[...], lb_ref[...], lo_ref[...], gn_ref[...],
                           tuple(r[...] for r in c_refs))
        y_ref[...] = y.astype(BF)
        for h in range(NH):
            state[h] = new[h]

        @pl.when(pl.program_id(0) == nt_ - 1)
        def _():
            ga.finish()

    full = lambda a: pl.BlockSpec(a.shape, lambda i, nd=a.ndim: (0,) * nd)
    outs = pl.pallas_call(
        body, name="fwd_mix", grid=(nt_,),
        out_shape=(jax.ShapeDtypeStruct((t, D), BF), jax.ShapeDtypeStruct((nt_, NH, HD, HD), F32),
                   *_exchange_out_shapes(to_gather, False)),
        in_specs=[pl.BlockSpec((mt, D_IN), lambda i: (i, 0)), full(w_s), full(b_s_t), full(ln_w), full(ln_b), full(lower),
                  full(gn_w)] + [full(a) for a in tables] + [_any() for _ in to_gather],
        out_specs=(pl.BlockSpec((mt, D), lambda i: (i, 0)), pl.BlockSpec((None, NH, HD, HD), lambda i: (i, 0, 0, 0)),
                   *[_any() for _ in to_gather]),
        scratch_shapes=[pltpu.VMEM((NH, HD, HD), F32)] + _exchange_sems(ng),
        compiler_params=_params(dimension_semantics=("arbitrary",), has_side_effects=True),
    )(proj, w_s, b_s_t, ln_w, ln_b, lower, gn_w, *tables, *to_gather)
    return outs[0], outs[1], outs[2:]


def _bwd_mix(proj, dycat, states, w_s, b_s_t, ln_w, ln_b, lower, gn_w, tables, to_scatter):
    t = proj.shape[0]
    mt = _row_tile(t, MIX_TILE)
    nt_ = t // mt
    nc = len(tables)
    ns = len(to_scatter)

    def body(*refs):
        proj_ref, dy_ref, st_ref, ws_ref, bs_ref, lw_ref, lb_ref, lo_ref, gn_ref = refs[:9]
        c_refs = refs[9:9 + nc]
        refs = refs[9 + nc:]
        s_in = refs[:ns]
        dproj_ref = refs[ns]
        acc = refs[1 + ns:7 + ns]
        s_out = refs[7 + ns:7 + 2 * ns]
        dstate = refs[7 + 2 * ns]
        ex = _Exchange(s_in, s_out, *refs[8 + 2 * ns:], True)

        @pl.when(pl.program_id(0) == 0)
        def _():
            dstate[...] = jnp.zeros_like(dstate)
            for r in acc:
                r[...] = jnp.zeros_like(r)
            ex.start()

        consts = tuple(r[...] for r in c_refs)

        def f(p, s, ws, bs, lw, lb_, lo, gn):
            return _mix_tile(p, s, ws, bs, lw, lb_, lo, gn, consts)

        st = tuple(st_ref[h] for h in range(NH))
        _, vjp = jax.vjp(f, proj_ref[...], st, ws_ref[...], bs_ref[...], lw_ref[...], lb_ref[...], lo_ref[...], gn_ref[...])
        grads = vjp((dy_ref[...], tuple(dstate[h] for h in range(NH))))
        dproj_ref[...] = grads[0].astype(BF)
        for h in range(NH):
            dstate[h] = grads[1][h]
        for r, gval in zip(acc, grads[2:]):
            r[...] += gval

        @pl.when(pl.program_id(0) == nt_ - 1)
        def _():
            ex.wait()

    full = lambda a: pl.BlockSpec(a.shape, lambda i, nd=a.ndim: (0,) * nd)
    rev = lambda i: (nt_ - 1 - i, 0)
    smalls = (w_s, b_s_t, ln_w, ln_b, lower, gn_w)
    outs = pl.pallas_call(
        body, name="bwd_mix", grid=(nt_,),
        out_shape=(jax.ShapeDtypeStruct((t, D_IN), BF), *[jax.ShapeDtypeStruct(a.shape, F32) for a in smalls],
                   *_exchange_out_shapes(to_scatter, True)),
        in_specs=[pl.BlockSpec((mt, D_IN), rev), pl.BlockSpec((mt, D), rev),
                  pl.BlockSpec((None, NH, HD, HD), lambda i: (nt_ - 1 - i, 0, 0, 0))]
        + [full(a) for a in smalls] + [full(a) for a in tables] + [_any() for _ in to_scatter],
        out_specs=(pl.BlockSpec((mt, D_IN), rev), *[full(a) for a in smalls], *[_any() for _ in to_scatter]),
        scratch_shapes=[pltpu.VMEM((NH, HD, HD), F32)] + _exchange_sems(ns),
        compiler_params=_params(dimension_semantics=("arbitrary",), has_side_effects=True),
    )(proj, dycat, states, w_s, b_s_t, ln_w, ln_b, lower, gn_w, *tables, *to_scatter)
    return outs[0], outs[1:7], outs[7:]


def _row_tile(t, want):
    return want if t % want == 0 else t


def _rms(v):
    rstd = lax.rsqrt(jnp.mean(v * v, axis=-1, keepdims=True) + EPS)
    return v * rstd, rstd


def _rms_bwd(dxhat, xhat, rstd):
    return rstd * (dxhat - xhat * jnp.mean(dxhat * xhat, axis=-1, keepdims=True))


def _colsum(v):
    return jnp.sum(v, axis=0, keepdims=True)


def _fwd_in(x, ada, n1w, w_in_st, to_gather):
    t = x.shape[0]
    tm = _row_tile(t, 512)
    n_t = t // tm
    ng = len(to_gather)

    def body(*refs):
        x_ref, ada_ref, n1_ref, wst_ref = refs[:4]
        g_in = refs[4:4 + ng]
        proj_ref, h1_ref = refs[4 + ng:6 + ng]
        g_out = refs[6 + ng:6 + 2 * ng]
        w_ref, sems = refs[6 + 2 * ng:8 + 2 * ng]
        ga = _Gather2(g_in, g_out, *refs[8 + 2 * ng:])

        @pl.when(pl.program_id(0) == 0)
        def _():
            ga.start()
            _load_columns(wst_ref, w_ref, sems)

        @pl.when(pl.program_id(0) == (3 * n_t) // 4)
        def _():
            ga.forward()

        xh, _ = _rms(x_ref[...])
        h1 = (xh * n1_ref[...]) * (1.0 + ada_ref[1:2, :]) + ada_ref[0:1, :]
        h1b = h1.astype(BF)
        h1_ref[...] = h1b
        proj_ref[...] = _mm(h1b, w_ref[...])

        @pl.when(pl.program_id(0) == n_t - 1)
        def _():
            ga.finish()

    full = lambda a: pl.BlockSpec(a.shape, lambda i, nd=a.ndim: (0,) * nd)
    outs = pl.pallas_call(
        body, name="fwd_in", grid=(n_t,),
        out_shape=(jax.ShapeDtypeStruct((t, D_IN), F32), jax.ShapeDtypeStruct((t, D), BF),
                   *_exchange_out_shapes(to_gather, False)),
        in_specs=[pl.BlockSpec((tm, D), lambda i: (i, 0)), full(ada), full(n1w), _any()] + [_any() for _ in to_gather],
        out_specs=(pl.BlockSpec((tm, D_IN), lambda i: (i, 0)), pl.BlockSpec((tm, D), lambda i: (i, 0)),
                   *[_any() for _ in to_gather]),
        scratch_shapes=[pltpu.VMEM((D, D_IN), BF), pltpu.SemaphoreType.DMA((N_DEV,))] + _exchange_sems(ng),
        compiler_params=_params(dimension_semantics=("arbitrary",), has_side_effects=True),
    )(x, ada, n1w, w_in_st, *to_gather)
    return outs[0], outs[1], outs[2:]


def _fwd_ffn(x, ycat, tgt, ada, n2w, fw, w_out, w_fi, w_fo):
    t = x.shape[0]
    tm = _row_tile(t, 256)

    def body(x_ref, y_ref, t_ref, ada_ref, n2_ref, fw_ref, wo_ref, wi_ref, wf_ref,
             x1_ref, mix_ref, h2_ref, act_ref, gu_ref, dffn_ref, dx2_ref, part_ref):
        @pl.when(pl.program_id(0) == 0)
        def _():
            part_ref[...] = jnp.zeros_like(part_ref)

        g1, sh2, sc2, g2 = ada_ref[2:3, :], ada_ref[3:4, :], ada_ref[4:5, :], ada_ref[5:6, :]
        mix = _mm(y_ref[...], wo_ref[...])
        mix_ref[...] = mix.astype(BF)
        x1 = x_ref[...] + g1 * mix
        x1_ref[...] = x1
        xh2, _ = _rms(x1)
        h2b = ((xh2 * n2_ref[...]) * (1.0 + sc2) + sh2).astype(BF)
        h2_ref[...] = h2b
        ffn = jnp.zeros((tm, D), F32)
        for lo, hi in FF_CHUNKS:
            gate = _nt(h2b, wi_ref[lo:hi, :])
            up = _nt(h2b, wi_ref[D_FF + lo:D_FF + hi, :])
            gu_ref[:, lo:hi] = gate.astype(BF)
            gu_ref[:, D_FF + lo:D_FF + hi] = up.astype(BF)
            a = (_silu(gate) * up).astype(BF)
            act_ref[:, lo:hi] = a
            ffn = ffn + _mm(a, wf_ref[lo:hi, :])
        x2 = x1 + g2 * ffn
        xh3, rstd3 = _rms(x2)
        err = xh3 * fw_ref[...] - t_ref[...]
        dy = err * (1.0 / D)
        dx2 = _rms_bwd(dy * fw_ref[...], xh3, rstd3)
        dx2_ref[...] = dx2
        dffn_ref[...] = (g2 * dx2).astype(BF)
        part_ref[0:1, :] += _colsum(dx2 * ffn)
        part_ref[1:2, :] += _colsum(dy * xh3)
        part_ref[2:3, :] += jnp.zeros((1, D), F32) + (0.5 / D) * jnp.sum(err * err)

    full = lambda a: pl.BlockSpec(a.shape, lambda i, nd=a.ndim: (0,) * nd)
    row = lambda w: pl.BlockSpec((tm, w), lambda i: (i, 0))
    return pl.pallas_call(
        body, name="fwd_ffn", grid=(t // tm,),
        out_shape=(jax.ShapeDtypeStruct((t, D), F32), jax.ShapeDtypeStruct((t, D), BF), jax.ShapeDtypeStruct((t, D), BF),
                   jax.ShapeDtypeStruct((t, D_FF), BF), jax.ShapeDtypeStruct((t, 2 * D_FF), BF),
                   jax.ShapeDtypeStruct((t, D), BF), jax.ShapeDtypeStruct((t, D), F32), jax.ShapeDtypeStruct((8, D), F32)),
        in_specs=[row(D), row(D), row(D), full(ada), full(n2w), full(fw), _whole_vmem(), _whole_vmem(), _whole_vmem()],
        out_specs=(row(D), row(D), row(D), row(D_FF), row(2 * D_FF), row(D), row(D),
                   pl.BlockSpec((8, D), lambda i: (0, 0))),
        compiler_params=_params(dimension_semantics=("arbitrary",)),
    )(x, ycat, tgt, ada, n2w, fw, w_out, w_fi, w_fo)


def _bwd_ffn(x1, gu, dffn, dx2, mix, ada, n2w, w_out, w_fi, w_fo):
    t = x1.shape[0]
    tm = _row_tile(t, 256)

    def body(x1_ref, gu_ref, dffn_ref, dx2_ref, mix_ref, ada_ref, n2_ref, wo_ref, wi_ref, wf_ref,
             dgu_ref, dx1_ref, dmix_ref, dycat_ref, part_ref):
        @pl.when(pl.program_id(0) == 0)
        def _():
            part_ref[...] = jnp.zeros_like(part_ref)

        g1, sc2 = ada_ref[2:3, :], ada_ref[4:5, :]
        dffn = dffn_ref[...]
        dh2 = jnp.zeros((tm, D), F32)
        for lo, hi in FF_CHUNKS:
            gate = gu_ref[:, lo:hi].astype(F32)
            up = gu_ref[:, D_FF + lo:D_FF + hi].astype(F32)
            dact = _nt(dffn, wf_ref[lo:hi, :])
            sg = _sig(gate)
            dgate = (dact * up * (sg * (1.0 + gate * (1.0 - sg)))).astype(BF)
            dup = (dact * (gate * sg)).astype(BF)
            dgu_ref[:, lo:hi] = dgate
            dgu_ref[:, D_FF + lo:D_FF + hi] = dup
            dh2 = dh2 + _mm(dgate, wi_ref[lo:hi, :]) + _mm(dup, wi_ref[D_FF + lo:D_FF + hi, :])
        x1 = x1_ref[...]
        xh2, rstd2 = _rms(x1)
        xn2 = xh2 * n2_ref[...]
        dxn2 = dh2 * (1.0 + sc2)
        dx1 = dx2_ref[...] + _rms_bwd(dxn2 * n2_ref[...], xh2, rstd2)
        dx1_ref[...] = dx1
        dmix = (g1 * dx1).astype(BF)
        dmix_ref[...] = dmix
        dycat_ref[...] = _nt(dmix, wo_ref[...])
        part_ref[0:1, :] += _colsum(dh2)
        part_ref[1:2, :] += _colsum(dh2 * xn2)
        part_ref[2:3, :] += _colsum(dxn2 * xh2)
        part_ref[3:4, :] += _colsum(dx1 * mix_ref[...].astype(F32))

    full = lambda a: pl.BlockSpec(a.shape, lambda i, nd=a.ndim: (0,) * nd)
    row = lambda w: pl.BlockSpec((tm, w), lambda i: (i, 0))
    return pl.pallas_call(
        body, name="bwd_ffn", grid=(t // tm,),
        out_shape=(jax.ShapeDtypeStruct((t, 2 * D_FF), BF), jax.ShapeDtypeStruct((t, D), F32),
                   jax.ShapeDtypeStruct((t, D), BF), jax.ShapeDtypeStruct((t, D), F32), jax.ShapeDtypeStruct((8, D), F32)),
        in_specs=[row(D), row(2 * D_FF), row(D), row(D), row(D), full(ada), full(n2w),
                  _whole_vmem(), _whole_vmem(), _whole_vmem()],
        out_specs=(row(2 * D_FF), row(D), row(D), row(D), pl.BlockSpec((8, D), lambda i: (0, 0))),
        compiler_params=_params(dimension_semantics=("arbitrary",)),
    )(x1, gu, dffn, dx2, mix, ada, n2w, w_out, w_fi, w_fo)


def _bwd_in(x, dproj, dx1, ada, n1w, w_in_st, to_scatter, to_gather):
    t = x.shape[0]
    tm = _row_tile(t, 1024)
    n_t = t // tm
    ns = len(to_scatter)
    ng = len(to_gather)
    nx = ns + ng

    def body(*refs):
        x_ref, dp_ref, dx1_ref, ada_ref, n1_ref, wst_ref = refs[:6]
        x_in = refs[6:6 + nx]
        gx_ref, part_ref = refs[6 + nx:8 + nx]
        x_out = refs[8 + nx:8 + 2 * nx]
        w_ref, sems = refs[8 + 2 * nx:10 + 2 * nx]
        sem_refs = refs[10 + 2 * nx:]
        ex = _Exchange(x_in[:ns], x_out[:ns], *sem_refs[:3], True)
        gx = _Exchange(x_in[ns:], x_out[ns:], *sem_refs[3:], False) if ng else None

        @pl.when(pl.program_id(0) == 0)
        def _():
            ex.start()
            if ng:
                gx.start()
            part_ref[...] = jnp.zeros_like(part_ref)
            _load_columns(wst_ref, w_ref, sems)

        dh1 = _nt(dp_ref[...], w_ref[...])
        xh, rstd = _rms(x_ref[...])
        xn = xh * n1_ref[...]
        dxn = dh1 * (1.0 + ada_ref[1:2, :])
        gx_ref[...] = dx1_ref[...] + _rms_bwd(dxn * n1_ref[...], xh, rstd)
        part_ref[0:1, :] += _colsum(dh1)
        part_ref[1:2, :] += _colsum(dh1 * xn)
        part_ref[2:3, :] += _colsum(dxn * xh)

        @pl.when(pl.program_id(0) == n_t - 1)
        def _():
            ex.wait()
            if ng:
                gx.wait()

    full = lambda a: pl.BlockSpec(a.shape, lambda i, nd=a.ndim: (0,) * nd)
    row = lambda w: pl.BlockSpec((tm, w), lambda i: (i, 0))
    outs = pl.pallas_call(
        body, name="bwd_in", grid=(n_t,),
        out_shape=(jax.ShapeDtypeStruct((t, D), F32), jax.ShapeDtypeStruct((8, D), F32),
                   *_exchange_out_shapes(to_scatter, True), *_exchange_out_shapes(to_gather, False)),
        in_specs=[row(D), row(D_IN), row(D), full(ada), full(n1w), _any()] + [_any() for _ in range(nx)],
        out_specs=(row(D), pl.BlockSpec((8, D), lambda i: (0, 0)), *[_any() for _ in range(nx)]),
        scratch_shapes=[pltpu.VMEM((D, D_IN), BF), pltpu.SemaphoreType.DMA((N_DEV,))] + _exchange_sems(ns)
        + (_exchange_sems(ng) if ng else []),
        compiler_params=_params(dimension_semantics=("arbitrary",), has_side_effects=True),
    )(x, dproj, dx1, ada, n1w, w_in_st, *to_scatter, *to_gather)
    return outs[0], outs[1], outs[2:2 + ns], outs[2 + ns:]


def _wgrad(a, b, name, a_spec, b_spec, out_shape, out_spec, grid, acc_shape, split=1, to_gather=()):
    n_j, n_t = grid
    ng = len(to_gather)

    def body(*refs):
        a_ref, b_ref = refs[:2]
        g_in = refs[2:2 + ng]
        o_ref = refs[2 + ng]
        g_out = refs[3 + ng:3 + 2 * ng]
        acc = refs[3 + 2 * ng]
        first = (pl.program_id(0) == 0) & (pl.program_id(1) == 0)
        last = (pl.program_id(0) == n_j - 1) & (pl.program_id(1) == n_t - 1)
        if ng:
            gx = _Exchange(g_in, g_out, *refs[4 + 2 * ng:], False)

            @pl.when(first)
            def _():
                gx.start()

        @pl.when(pl.program_id(1) == 0)
        def _():
            acc[...] = jnp.zeros_like(acc)

        acc[...] += _tn(a_ref[...], b_ref[...])

        @pl.when(pl.program_id(1) == n_t - 1)
        def _():
            if split == 1:
                o_ref[...] = acc[...].astype(BF)
            else:
                w = acc_shape[1] // split
                for s in range(split):
                    o_ref[s] = acc[:, s * w:(s + 1) * w].astype(BF)

        if ng:
            @pl.when(last)
            def _():
                gx.wait()

    outs = pl.pallas_call(
        body, name=name, grid=grid,
        out_shape=(jax.ShapeDtypeStruct(out_shape, BF), *_exchange_out_shapes(to_gather, False)),
        in_specs=[a_spec, b_spec] + [_any() for _ in to_gather], out_specs=(out_spec, *[_any() for _ in to_gather]),
        scratch_shapes=[pltpu.VMEM(acc_shape, F32)] + (_exchange_sems(ng) if ng else []),
        compiler_params=_params(dimension_semantics=("arbitrary", "arbitrary"), has_side_effects=bool(ng)),
    )(a, b, *to_gather)
    return (outs[0], outs[1:]) if ng else outs[0]


def _adamw_math(w, g, m, v):
    m = ADAM_B1 * m + (1.0 - ADAM_B1) * g
    v = ADAM_B2 * v + (1.0 - ADAM_B2) * (g * g)
    m_hat = m / (1.0 - ADAM_B1 ** ADAM_STEP)
    v_hat = v / (1.0 - ADAM_B2 ** ADAM_STEP)
    delta = -ADAM_LR * (m_hat / (jnp.sqrt(v_hat) + ADAM_EPS) + ADAM_WD * w)
    return delta, m, v


def _adamw_recv(w, m, v, recv, name, tr):
    r, c = w.shape

    def body(w_ref, m_ref, v_ref, r_ref, g_ref, d_ref, nm_ref, nv_ref):
        g = r_ref[0].astype(F32)
        for k in range(1, N_DEV):
            g = g + r_ref[k].astype(F32)
        g_ref[...] = g
        d_ref[...], nm_ref[...], nv_ref[...] = _adamw_math(w_ref[...], g, m_ref[...], v_ref[...])

    row = pl.BlockSpec((tr, c), lambda i: (i, 0))
    return pl.pallas_call(
        body, name=name, grid=(r // tr,),
        out_shape=tuple(jax.ShapeDtypeStruct((r, c), F32) for _ in range(4)),
        in_specs=[row, row, row, pl.BlockSpec((N_DEV, tr, c), lambda i: (0, i, 0))],
        out_specs=(row, row, row, row),
        compiler_params=_params(dimension_semantics=("arbitrary",)),
    )(w, m, v, recv)


def _adamw_ada(w, m, v, cact, dada_cols):
    r, c = w.shape
    tr = 256

    def body(w_ref, m_ref, v_ref, ca_ref, da_ref, g_ref, d_ref, nm_ref, nv_ref):
        g = _tn(ca_ref[...].astype(BF), da_ref[...].astype(BF))
        g_ref[...] = g
        d_ref[...], nm_ref[...], nv_ref[...] = _adamw_math(w_ref[...], g, m_ref[...], v_ref[...])

    row = pl.BlockSpec((tr, c), lambda i: (i, 0))
    return pl.pallas_call(
        body, name="adamw_ada", grid=(r // tr,),
        out_shape=tuple(jax.ShapeDtypeStruct((r, c), F32) for _ in range(4)),
        in_specs=[row, row, row, pl.BlockSpec((N_DEV, tr), lambda i: (0, i)), pl.BlockSpec(dada_cols.shape, lambda i: (0, 0))],
        out_specs=(row, row, row, row),
        compiler_params=_params(dimension_semantics=("arbitrary",)),
    )(w, m, v, cact, dada_cols)


def _adamw_small(gathered, wmv):
    n_g = len(gathered)
    n_p = len(wmv)
    flat = [a for trip in wmv for a in trip]

    def body(*refs):
        g_refs = refs[:n_g]
        p_refs = refs[n_g:n_g + 3 * n_p]
        o_refs = refs[n_g + 3 * n_p:]

        def total(ref):
            s = ref[0]
            for k in range(1, N_DEV):
                s = s + ref[k]
            return s

        f3, b3, b1, dws, dbs, dlnw, dlnb, dlo, dgn = [total(r) for r in g_refs]
        dada_rows = [b1[0:1], b1[1:2], b3[3:4], b3[0:1], b3[1:2], f3[0:1]]
        for r, g in enumerate(dada_rows):
            cs = slice(r * D, (r + 1) * D)
            w, m, v = p_refs[0][:, cs], p_refs[1][:, cs], p_refs[2][:, cs]
            o_refs[0][:, cs] = g
            o_refs[1][:, cs], o_refs[2][:, cs], o_refs[3][:, cs] = _adamw_math(w, g, m, v)
        grads = [None, b1[2:3], dws, dbs, dlnw, dlnb, dlo, dgn, b3[2:3], f3[1:2]]
        for i, g in enumerate(grads):
            if g is None:
                continue
            w, m, v = p_refs[3 * i][...], p_refs[3 * i + 1][...], p_refs[3 * i + 2][...]
            o_refs[4 * i][...] = g
            o_refs[4 * i + 1][...], o_refs[4 * i + 2][...], o_refs[4 * i + 3][...] = _adamw_math(w, g, m, v)
        o_refs[4 * n_p][...] = jnp.zeros((8, 128), F32) + f3[2:3, 0:128]

    out_shape = []
    for w, _, _ in wmv:
        out_shape += [jax.ShapeDtypeStruct(w.shape, F32)] * 4
    out_shape.append(jax.ShapeDtypeStruct((8, 128), F32))
    n_in = n_g + 3 * n_p
    return pl.pallas_call(
        body, name="adamw_small",
        out_shape=tuple(out_shape),
        in_specs=[_whole_vmem()] * n_in, out_specs=tuple(_whole_vmem() for _ in out_shape),
        compiler_params=_params(),
    )(*gathered, *flat)


def kernel(x, c, w_ada, b_ada, norm1_w, w_in, w_s, b_s, v_ln_w, v_ln_b, lower_bounds, gn_w, w_out, norm2_w, w_ffn_in, w_ffn_out, final_norm_w, loss_target, m_w_ada, m_b_ada, m_norm1_w, m_w_in, m_w_s, m_b_s, m_v_ln_w, m_v_ln_b, m_lower_bounds, m_gn_w, m_w_out, m_norm2_w, m_w_ffn_in, m_w_ffn_out, m_final_norm_w, v_w_ada, v_b_ada, v_norm1_w, v_w_in, v_w_s, v_b_s, v_v_ln_w, v_v_ln_b, v_lower_bounds, v_gn_w, v_w_out, v_norm2_w, v_w_ffn_in, v_w_ffn_out, v_final_norm_w):
    me = 4 * lax.axis_index("x") + 2 * lax.axis_index("y") + lax.axis_index("c")
    t = x.shape[1]
    x2d = x.reshape(t, D)
    tgt = loss_target.reshape(t, D)
    ada_cols = w_ada.shape[2]

    tp = lambda a: jnp.swapaxes(a[0], 0, 1)
    win_b, wout_b, wfi_b, wfo_b = _cast_bf16([w_in[0], w_out[0], tp(w_ffn_in), w_ffn_out[0]], "cast_weights")

    b_cols = lax.dynamic_slice(b_ada, (0, me * ada_cols), (1, ada_cols))
    win_st, ada_st, cact = _startup(win_b, c, w_ada[0], b_cols)
    ada = lax.dynamic_index_in_dim(ada_st, me, axis=1, keepdims=False).reshape(6, D)

    tables = _decay_tables()
    ws3 = w_s[0]
    bs_t = b_s[0].T

    proj, h1, (wout_st, wfo_st) = _fwd_in(x2d, ada, norm1_w, win_st, [wout_b, wfo_b])
    ycat, states, (wfi_st,) = _fwd_mix(proj, ws3, bs_t, v_ln_w, v_ln_b, lower_bounds, gn_w, tables, [wfi_b])
    w_out_full = wout_st.reshape(D, D)
    w_fo_full = wfo_st.reshape(D_FF, D)
    w_fi_full = wfi_st.reshape(2 * D_FF, D)
    x1, mixb, h2, act, gu, dffn, dx2, part_f = _fwd_ffn(x2d, ycat, tgt, ada, norm2_w, final_norm_w.reshape(1, D),
                                                        w_out_full, w_fi_full, w_fo_full)
    dgu, dx1, dmix, dycat, part_b3 = _bwd_ffn(x1, gu, dffn, dx2, mixb, ada, norm2_w, w_out_full, w_fi_full, w_fo_full)
    tk = _row_tile(t, 2048)
    n_t = t // tk
    win_cols = D_IN // N_DEV
    dwout = _wgrad(ycat, dmix, "wgrad_out",
                   pl.BlockSpec((tk, D), lambda j, i: (i, 0)), pl.BlockSpec((tk, D), lambda j, i: (i, 0)),
                   (D, D), pl.BlockSpec((D, D), lambda j, i: (0, 0)), (1, n_t), (D, D))
    dwfi = _wgrad(dgu, h2, "wgrad_ffn_in",
                  pl.BlockSpec((tk, FF_PAIR), lambda j, i: (i, j)), pl.BlockSpec((tk, D), lambda j, i: (i, 0)),
                  (4, FF_PAIR, D), pl.BlockSpec((None, FF_PAIR, D), lambda j, i: (j, 0, 0)), (4, n_t), (FF_PAIR, D))
    dwfo = _wgrad(act, dffn, "wgrad_ffn_out",
                  pl.BlockSpec((tk, FF_PAIR), lambda j, i: (i, j)), pl.BlockSpec((tk, D), lambda j, i: (i, 0)),
                  (2, FF_PAIR, D), pl.BlockSpec((None, FF_PAIR, D), lambda j, i: (j, 0, 0)), (2, n_t), (FF_PAIR, D))
    dproj, (dws, dbs_t, dlnw, dlnb, dlower, dgnw), (r_out, r_fi, r_fo) = _bwd_mix(
        proj, dycat, states, ws3, bs_t, v_ln_w, v_ln_b, lower_bounds, gn_w, tables,
        [dwout.reshape(N_DEV, D // N_DEV, D), dwfi.reshape(N_DEV, FF_BLK, D), dwfo.reshape(N_DEV, D_FF // N_DEV, D)])
    dwin, early = _wgrad(h1, dproj, "wgrad_in",
                         pl.BlockSpec((tk, D), lambda j, i: (i, 0)), pl.BlockSpec((tk, 4 * win_cols), lambda j, i: (i, j)),
                         (N_DEV, D, win_cols), pl.BlockSpec((4, D, win_cols), lambda j, i: (j, 0, 0)), (N_DEV // 4, n_t),
                         (D, 4 * win_cols), split=4, to_gather=[part_f, part_b3, dws, dbs_t, dlnw, dlnb, dlower, dgnw])
    grad_x, part_b1, (r_in,), _ = _bwd_in(x2d, dproj, dx1, ada, norm1_w, win_st, [dwin], [])

    g_w_in, d_w_in, nm_w_in, nv_w_in = _adamw_recv(w_in[0], m_w_in[0], v_w_in[0], r_in, "adamw_w_in", 256)
    g_w_out, d_w_out, nm_w_out, nv_w_out = _adamw_recv(w_out[0], m_w_out[0], v_w_out[0], r_out, "adamw_w_out", 128)
    fi_t = _adamw_recv(tp(w_ffn_in), tp(m_w_ffn_in), tp(v_w_ffn_in), r_fi, "adamw_w_ffn_in", 176)
    g_w_fi, d_w_fi, nm_w_fi, nv_w_fi = [jnp.swapaxes(a, 0, 1) for a in fi_t]
    g_w_fo, d_w_fo, nm_w_fo, nv_w_fo = _adamw_recv(w_ffn_out[0], m_w_ffn_out[0], v_w_ffn_out[0], r_fo, "adamw_w_ffn_out", 176)

    (b1_all,) = _exchange([part_b1], "gather_small", False, True)
    gathered = [early[0], early[1], b1_all, *early[2:]]
    f3_all, b3_all = gathered[0], gathered[1]
    dada_all = jnp.stack([b1_all[:, 0], b1_all[:, 1], b3_all[:, 3], b3_all[:, 0], b3_all[:, 1], f3_all[:, 0]], axis=1)
    dada_cols = lax.dynamic_slice(dada_all.reshape(N_DEV, 6 * D), (0, me * ada_cols), (N_DEV, ada_cols))
    g_w_ada, d_w_ada, nm_w_ada, nv_w_ada = _adamw_ada(w_ada[0], m_w_ada[0], v_w_ada[0], cact, dada_cols)

    r1 = lambda a: a.reshape(1, D)
    tr = lambda a: a[0].T
    wmv = [
        (b_ada, m_b_ada, v_b_ada),
        (norm1_w, m_norm1_w, v_norm1_w),
        (w_s[0], m_w_s[0], v_w_s[0]),
        (tr(b_s), tr(m_b_s), tr(v_b_s)),
        (v_ln_w, m_v_ln_w, v_v_ln_w),
        (v_ln_b, m_v_ln_b, v_v_ln_b),
        (lower_bounds, m_lower_bounds, v_lower_bounds),
        (gn_w, m_gn_w, v_gn_w),
        (norm2_w, m_norm2_w, v_norm2_w),
        (r1(final_norm_w), r1(m_final_norm_w), r1(v_final_norm_w)),
    ]
    small = _adamw_small(gathered, wmv)
    loss = small[-1][0, 0]

    def unshape(i, a):
        if i == 2:
            return a.reshape(1, NH, BLK, BLK)
        if i == 3:
            return a.T.reshape(1, NH, BLK)
        if i == 9:
            return a.reshape(D)
        return a

    def small_out(kind):
        return [unshape(i, small[4 * i + kind]) for i in range(len(wmv))]

    e3 = lambda a: a[None]
    big = {
        0: (e3(g_w_ada), e3(g_w_in), e3(g_w_out), e3(g_w_fi), e3(g_w_fo)),
        1: (e3(d_w_ada), e3(d_w_in), e3(d_w_out), e3(d_w_fi), e3(d_w_fo)),
        2: (e3(nm_w_ada), e3(nm_w_in), e3(nm_w_out), e3(nm_w_fi), e3(nm_w_fo)),
        3: (e3(nv_w_ada), e3(nv_w_in), e3(nv_w_out), e3(nv_w_fi), e3(nv_w_fo)),
    }

    def ordered(kind):
        s = small_out(kind)
        b_ = big[kind]
        return [b_[0], s[0], s[1], b_[1], s[2], s[3], s[4], s[5], s[6], s[7], b_[2], s[8], b_[3], b_[4], s[9]]

    return (loss, grad_x.reshape(1, t, D), *ordered(0), *ordered(1), *ordered(2), *ordered(3))
```

```python
import numpy as np
import jax
import jax.numpy as jnp
from jax import lax
from jax.experimental import pallas as pl
from jax.experimental.pallas import tpu as pltpu

F32 = jnp.float32
BF = jnp.bfloat16
MESH = pl.DeviceIdType.MESH

N_DEV = 8
D = 1024
D_IN = 3072
D_FF = 2816
FF_BLK = D_FF // 4
FF_CHUNKS = ((0, D_FF),)
FF_PAIR = 2 * FF_BLK
CH = 64
BLK = 128
MIX_TILE = 256
FWD_MIX_SUBTILES = 4
BWD_MIX_SUBTILES = 2
NH = 4
HD = 128
EPS = 1e-6
LEVELS = (32, 16, 8, 4, 2, 1)

ADAM_LR = 0.001
ADAM_B1 = 0.9
ADAM_B2 = 0.999
ADAM_EPS = 1e-08
ADAM_WD = 0.01
ADAM_STEP = 10

VMEM_LIMIT = 56 * 1024 * 1024


def _params(**kw):
    return pltpu.CompilerParams(vmem_limit_bytes=VMEM_LIMIT, **kw)


def _whole_vmem():
    return pl.BlockSpec(memory_space=pltpu.VMEM)


def _any():
    return pl.BlockSpec(memory_space=pl.ANY)


class _Exchange:
    def __init__(self, ins, outs, send_sems, recv_sems, local_sems, scatter):
        self.ins, self.outs, self.scatter = ins, outs, scatter
        self.send_sems, self.recv_sems, self.local_sems = send_sems, recv_sems, local_sems
        x, y, c = lax.axis_index("x"), lax.axis_index("y"), lax.axis_index("c")
        self.me = 4 * x + 2 * y + c
        self.peers = []
        for k in range(1, N_DEV):
            peer = (1 - x if (k >> 2) & 1 else x, 1 - y if (k >> 1) & 1 else y, 1 - c if k & 1 else c)
            self.peers.append((peer, 4 * peer[0] + 2 * peer[1] + peer[2]))

    def _src(self, a, idx):
        return self.ins[a].at[idx] if self.scatter else self.ins[a]

    def _local(self):
        return [pltpu.make_async_copy(self._src(a, self.me), self.outs[a].at[self.me], self.local_sems.at[a])
                for a in range(len(self.ins))]

    def _remote(self, a, k, dst_slot):
        peer, peer_idx = self.peers[k]
        return pltpu.make_async_remote_copy(
            src_ref=self._src(a, peer_idx), dst_ref=self.outs[a].at[dst_slot],
            send_sem=self.send_sems.at[a, k], recv_sem=self.recv_sems.at[a, k],
            device_id=peer, device_id_type=MESH)

    def start(self):
        for cp in self._local():
            cp.start()
        for k in range(N_DEV - 1):
            for a in range(len(self.ins)):
                self._remote(a, k, self.me).start()

    def wait(self):
        for k in range(N_DEV - 1):
            for a in range(len(self.ins)):
                self._remote(a, k, self.peers[k][1]).wait()
        for cp in self._local():
            cp.wait()


class _Gather2:
    def __init__(self, ins, outs, send_sems, recv_sems, local_sems):
        self.ins, self.outs = ins, outs
        self.send_sems, self.recv_sems, self.local_sems = send_sems, recv_sems, local_sems
        x, y, c = lax.axis_index("x"), lax.axis_index("y"), lax.axis_index("c")
        self.c = c
        self.me = 4 * x + 2 * y + c
        self.sibling = (x, y, 1 - c)
        self.chips = [(1 - x, y), (x, 1 - y), (1 - x, 1 - y)]

    @staticmethod
    def _idx(px, py, pc):
        return 4 * px + 2 * py + pc

    def _copy(self, a, k, slot, to, own):
        src = self.ins[a] if own else self.outs[a].at[slot]
        return pltpu.make_async_remote_copy(
            src_ref=src, dst_ref=self.outs[a].at[slot],
            send_sem=self.send_sems.at[a, k], recv_sem=self.recv_sems.at[a, k],
            device_id=to, device_id_type=MESH)

    def _local(self):
        return [pltpu.make_async_copy(self.ins[a], self.outs[a].at[self.me], self.local_sems.at[a])
                for a in range(len(self.ins))]

    def start(self):
        for cp in self._local():
            cp.start()
        for a in range(len(self.ins)):
            self._copy(a, 0, self.me, self.sibling, True).start()
            for j, chip in enumerate(self.chips):
                self._copy(a, 1 + j, self.me, (*chip, self.c), True).start()

    def forward(self):
        for j, chip in enumerate(self.chips):
            for a in range(len(self.ins)):
                slot = self._idx(*chip, self.c)
                self._copy(a, 1 + j, slot, (*chip, self.c), True).wait_recv()
                self._copy(a, 4 + j, slot, self.sibling, False).start()

    def finish(self):
        for a in range(len(self.ins)):
            self._copy(a, 0, self._idx(*self.sibling), self.sibling, True).wait_recv()
            for j, chip in enumerate(self.chips):
                self._copy(a, 4 + j, self._idx(*chip, 1 - self.c), self.sibling, False).wait_recv()
            self._copy(a, 0, self.me, self.sibling, True).wait_send()
            for j, chip in enumerate(self.chips):
                self._copy(a, 1 + j, self.me, (*chip, self.c), True).wait_send()
                self._copy(a, 4 + j, self._idx(*chip, self.c), self.sibling, False).wait_send()
        for cp in self._local():
            cp.wait()


def _exchange_sems(n):
    return [pltpu.SemaphoreType.DMA((n, N_DEV - 1)), pltpu.SemaphoreType.DMA((n, N_DEV - 1)), pltpu.SemaphoreType.DMA((n,))]


def _exchange_out_shapes(arrays, scatter):
    return [jax.ShapeDtypeStruct(a.shape if scatter else (N_DEV,) + a.shape, a.dtype) for a in arrays]


def _exchange(arrays, name, scatter, in_vmem):
    n = len(arrays)

    def body(*refs):
        ex = _Exchange(refs[:n], refs[n:2 * n], *refs[2 * n:], scatter)
        ex.start()
        ex.wait()

    spec = _whole_vmem if in_vmem else _any
    return pl.pallas_call(
        body, name=name,
        out_shape=tuple(_exchange_out_shapes(arrays, scatter)),
        in_specs=[spec() for _ in arrays],
        out_specs=tuple(spec() for _ in arrays),
        scratch_shapes=_exchange_sems(n),
        compiler_params=_params(has_side_effects=True),
    )(*arrays)


def _cast_bf16(arrays, name):
    n = len(arrays)

    def body(*refs):
        for i in range(n):
            refs[n + i][...] = refs[i][...].astype(BF)

    return pl.pallas_call(
        body, name=name,
        out_shape=tuple(jax.ShapeDtypeStruct(a.shape, BF) for a in arrays),
        in_specs=[_whole_vmem() for _ in arrays],
        out_specs=tuple(_whole_vmem() for _ in arrays),
        compiler_params=_params(),
    )(*arrays)


def _load_columns(stacked_ref, full_ref, sems):
    c = stacked_ref.shape[2]
    cps = [pltpu.make_async_copy(stacked_ref.at[d], full_ref.at[:, pl.ds(d * c, c)], sems.at[d]) for d in range(N_DEV)]
    for cp in cps:
        cp.start()
    for cp in cps:
        cp.wait()


def _sig(v):
    return 0.5 * jnp.tanh(0.5 * v) + 0.5


@jax.custom_vjp
def _silu(v):
    return v * _sig(v)


def _silu_fwd(v):
    return _silu(v), v


def _silu_bwd(v, g):
    s = _sig(v)
    return (g * (s * (1.0 + v * (1.0 - s))),)


_silu.defvjp(_silu_fwd, _silu_bwd)


@jax.custom_vjp
def _sigmoid_rel(v):
    e = jnp.exp(-jnp.abs(v))
    d = 1.0 + e
    r = pl.reciprocal(d, approx=True)
    r = r * (2.0 - d * r)
    r = r * (2.0 - d * r)
    return jnp.where(v >= 0.0, r, e * r)


def _sigmoid_rel_fwd(v):
    s = _sigmoid_rel(v)
    return s, s


def _sigmoid_rel_bwd(s, g):
    return (g * (s * (1.0 - s)),)


_sigmoid_rel.defvjp(_sigmoid_rel_fwd, _sigmoid_rel_bwd)


def _gelu(v):
    return 0.5 * v * (1.0 + lax.erf(v * 0.7071067811865476))


def _dot(a, b, ca, cb):
    return lax.dot_general(a, b, (((ca,), (cb,)), ((), ())), preferred_element_type=F32)


@jax.custom_vjp
def _mm(a, b):
    return _dot(a, b, 1, 0)


def _mm_fwd(a, b):
    return _dot(a, b, 1, 0), (a, b)


def _mm_bwd(res, g):
    a, b = res
    gb = g.astype(BF)
    return _dot(gb, b, 1, 1).astype(a.dtype), _dot(a, gb, 0, 0).astype(b.dtype)


_mm.defvjp(_mm_fwd, _mm_bwd)


@jax.custom_vjp
def _nt(a, b):
    return _dot(a, b, 1, 1)


def _nt_fwd(a, b):
    return _dot(a, b, 1, 1), (a, b)


def _nt_bwd(res, g):
    a, b = res
    gb = g.astype(BF)
    return _dot(gb, b, 1, 0).astype(a.dtype), _dot(gb, a, 0, 0).astype(b.dtype)


_nt.defvjp(_nt_fwd, _nt_bwd)


@jax.custom_vjp
def _tn(a, b):
    return _dot(a, b, 0, 0)


def _tn_fwd(a, b):
    return _dot(a, b, 0, 0), (a, b)


def _tn_bwd(res, g):
    a, b = res
    gb = g.astype(BF)
    return _dot(b, gb, 1, 1).astype(a.dtype), _dot(a, gb, 1, 0).astype(b.dtype)


_tn.defvjp(_tn_fwd, _tn_bwd)


def _make_cast_dot(ca, cb, da_dims, db_dims):
    @jax.custom_vjp
    def dot(a, b):
        return _dot(a.astype(BF), b.astype(BF), ca, cb)

    def fwd(a, b):
        ab, bb = a.astype(BF), b.astype(BF)
        return _dot(ab, bb, ca, cb), (ab, bb)

    def bwd(res, g):
        ops = {"a": res[0], "b": res[1], "g": g.astype(BF)}
        return (_dot(ops[da_dims[0]], ops[da_dims[1]], da_dims[2], da_dims[3]),
                _dot(ops[db_dims[0]], ops[db_dims[1]], db_dims[2], db_dims[3]))

    dot.defvjp(fwd, bwd)
    return dot


_mm_c = _make_cast_dot(1, 0, ("g", "b", 1, 1), ("a", "g", 0, 0))
_nt_c = _make_cast_dot(1, 1, ("g", "b", 1, 0), ("g", "a", 0, 0))
_tn_c = _make_cast_dot(0, 0, ("b", "g", 1, 1), ("a", "g", 1, 0))


def _startup(win_b, c, w_ada, b_cols):
    ncol = w_ada.shape[1]

    def body(win_ref, c_ref, w_ref, b_ref, winst_ref, adast_ref, cact_ref, call_ref, blk_ref,
             ws, wr, wl, cs, cr, cl, as_, ar, al):
        big = _Gather2([win_ref], [winst_ref], ws, wr, wl)
        big.start()
        gc = _Exchange([c_ref], [call_ref], cs, cr, cl, False)
        gc.start()
        gc.wait()
        ca = _silu(call_ref[...].reshape(N_DEV, D))
        cact_ref[...] = ca
        blk_ref[...] = _mm(ca.astype(BF), w_ref[...].astype(BF)) + b_ref[...]
        ga = _Exchange([blk_ref], [adast_ref], as_, ar, al, False)
        ga.start()
        ga.wait()
        big.forward()
        big.finish()

    return pl.pallas_call(
        body, name="startup",
        out_shape=(jax.ShapeDtypeStruct((N_DEV,) + win_b.shape, BF), jax.ShapeDtypeStruct((N_DEV, N_DEV, ncol), F32),
                   jax.ShapeDtypeStruct((N_DEV, D), F32)),
        in_specs=[_any(), _whole_vmem(), _whole_vmem(), _whole_vmem()],
        out_specs=(_any(), _whole_vmem(), _whole_vmem()),
        scratch_shapes=[pltpu.VMEM((N_DEV, 1, D), F32), pltpu.VMEM((N_DEV, ncol), F32)]
        + _exchange_sems(1) + _exchange_sems(1) + _exchange_sems(1),
        compiler_params=_params(has_side_effects=True),
    )(win_b, c, w_ada, b_cols)


def _decay_tables():
    t = np.arange(CH)
    tri = (t[None, :] <= t[:, None]).astype(np.float32)
    masks = []
    for h in LEVELS:
        m = (t // (2 * h)) * (2 * h) + h
        upper = t >= m
        same = (t[:, None] // (2 * h)) == (t[None, :] // (2 * h))
        masks.append(same & upper[:, None] & (~upper)[None, :])
    masks.append(np.eye(CH, dtype=bool))
    lv = [np.where((t % (2 * h)) >= h, 1.0, -1.0) for h in LEVELS[:4]]
    m4 = t % 4
    lv += [(m4 == 0) * 1.0, (m4 >= 2) * 1.0, (m4 == 3) * 1.0, (t % 2 == 1) * 1.0]
    lvl = np.broadcast_to(np.stack(lv)[:, :, None], (8, CH, D // 2)).astype(np.float32)
    cid = np.arange(BLK) // CH
    gmask = (cid[:, None] >= cid[None, :]).astype(np.float32)
    masks = np.stack(masks).astype(np.float32)
    stacked = np.zeros((masks.shape[0], NH * CH, NH * CH), np.float32)
    for h in range(NH):
        stacked[:, h * CH:(h + 1) * CH, h * CH:(h + 1) * CH] = masks
    return (jnp.asarray(tri, BF), jnp.asarray(stacked), jnp.asarray(gmask), jnp.asarray(lvl))


def _split2(v):
    v1 = v.astype(BF)
    return v1, (v - v1.astype(F32)).astype(BF)


@jax.custom_vjp
def _cumsum_mm(tri, v):
    p1, p2 = _split2(v)
    return _dot(tri, p1, 1, 0) + _dot(tri, p2, 1, 0)


def _cumsum_mm_fwd(tri, v):
    return _cumsum_mm(tri, v), tri


def _cumsum_mm_bwd(tri, g):
    p1, p2 = _split2(g)
    return jnp.zeros_like(tri), _dot(tri, p1, 0, 0) + _dot(tri, p2, 0, 0)


_cumsum_mm.defvjp(_cumsum_mm_fwd, _cumsum_mm_bwd)


def _make_row_roll(shift):
    @jax.custom_vjp
    def roll(x):
        return pltpu.roll(x, shift % CH, 0)

    def fwd(x):
        return roll(x), None

    def bwd(_, g):
        return (pltpu.roll(g, (-shift) % CH, 0),)

    roll.defvjp(fwd, bwd)
    return roll


_prev_row = _make_row_roll(1)
_next_row = _make_row_roll(-1)


def _mix_tile(proj, state, w_s, b_s_t, ln_w, ln_b, lower, gn_w, consts):
    tri, masks, gmask, lvl = consts
    mt = proj.shape[0]
    u = proj[:, 0:512]
    v = proj[:, 512:1024]
    q = proj[:, 1024:1536]
    fl = proj[:, 1536:2048]
    inp = proj[:, 2048:2560]
    g = proj[:, 2560:3072]

    ug = _gelu(u)
    vg = _gelu(v)
    mu = jnp.mean(vg, axis=-1, keepdims=True)
    vc = vg - mu
    var = jnp.mean(vc * vc, axis=-1, keepdims=True)
    vn = vc * lax.rsqrt(var + EPS) * ln_w + ln_b
    wsm = [w_s[h] * gmask for h in range(NH)]
    ya = [[None] * NH for _ in range(mt // BLK)]
    for bi in range(mt // BLK):
        rows = slice(bi * BLK, (bi + 1) * BLK)
        for h in range(NH):
            cols = slice(h * HD, (h + 1) * HD)
            ya[bi][h] = ug[rows, cols] * (_mm_c(wsm[h], vn[rows, cols]) + b_s_t[:, h:h + 1])
    ya_full = jnp.concatenate([jnp.concatenate(r, axis=1) for r in ya], axis=0)

    l0 = lower[0:1, :]
    l1 = lower[1:2, :]
    mx = jnp.maximum(l0, l1)
    e0 = jnp.exp(l0 - mx)
    e1 = jnp.exp(l1 - mx)
    lb = e0 / (e0 + e1)
    qf = _silu(q)
    f = lb + (1.0 - lb) * _sigmoid_rel(fl)
    logf = jnp.log(f)
    kk = 1.0 - f
    gate = _silu(g)
    nl = len(LEVELS)
    half = D // 2
    heads_to_rows = lambda a: jnp.concatenate([a[:, h * HD:(h + 1) * HD] for h in range(NH)], axis=0)
    st = list(state)
    yb = [[None] * NH for _ in range(mt // CH)]
    for ci in range(mt // CH):
        rows = slice(ci * CH, (ci + 1) * CH)
        lc = logf[rows]
        b = _cumsum_mm(tri, lc)
        xb = jnp.exp(b)
        xinv = jnp.exp(b[CH - 1:CH, :] - b)
        xl = []
        for i, hs in enumerate(LEVELS[:3]):
            refs = [jnp.broadcast_to(b[r:r + 1, :], (2 * hs, half)) for r in range(hs - 1, CH, 2 * hs)]
            bref = refs[0] if len(refs) == 1 else jnp.concatenate(refs, axis=0)
            xl.append(jnp.exp(lvl[i] * (b - bref)))
        b3 = b.reshape(CH // 8, 8, half)
        bref = jnp.broadcast_to(b3[:, 3:4, :], (CH // 8, 8, half)).reshape(CH, half)
        xl.append(jnp.exp(lvl[3] * (b - bref)))
        xl.append(jnp.exp(lvl[4] * _next_row(lc) + lvl[5] * lc + lvl[6] * _prev_row(lc)))
        xl.append(jnp.exp(lvl[7] * lc))
        qc = qf[rows]
        kc = kk[rows]
        ic = inp[rows]
        zsrc = [jnp.concatenate([(qc if (r0 // hs) % 2 == 1 else kc)[r0:r0 + hs] for r0 in range(0, CH, hs)], axis=0)
                for hs in LEVELS[:3]]
        qs, ks = heads_to_rows(qc), heads_to_rows(kc)
        carried = _nt_c(qs * heads_to_rows(xb), jnp.concatenate(st, axis=0))
        inter = jnp.concatenate([carried[h * CH:(h + 1) * CH, h * HD:(h + 1) * HD] for h in range(NH)], axis=0)
        attn = masks[nl] * _nt_c(qs, ks)
        for li in range(nl):
            xs = heads_to_rows(xl[li])
            if li < 3:
                z = heads_to_rows(zsrc[li]) * xs
                pairs = _nt_c(z, z)
            else:
                pairs = _nt_c(qs * xs, ks * xs)
            attn = attn + masks[li] * pairs
        o = inter + _mm_c(attn, heads_to_rows(ic))
        update = _tn_c(ic, kc * xinv)
        decay = xb[CH - 1:CH, :]
        st = [st[h] * decay[:, h * HD:(h + 1) * HD] + update[h * HD:(h + 1) * HD, h * HD:(h + 1) * HD] for h in range(NH)]
        rs = lax.rsqrt(jnp.mean(o * o, axis=-1, keepdims=True) + EPS)
        ys = o * rs * gn_w * heads_to_rows(gate[rows])
        yb[ci] = [ys[h * CH:(h + 1) * CH] for h in range(NH)]
    yb_full = jnp.concatenate([jnp.concatenate(r, axis=1) for r in yb], axis=0)
    return jnp.concatenate([ya_full, yb_full], axis=1), tuple(st)


def _fwd_mix(proj, w_s, b_s_t, ln_w, ln_b, lower, gn_w, tables, to_gather):
    t = proj.shape[0]
    sub = _row_tile(t, MIX_TILE)
    n_sub = FWD_MIX_SUBTILES if t % (FWD_MIX_SUBTILES * sub) == 0 else 1
    mt = n_sub * sub
    nt_ = t // mt
    nc = len(tables)
    ng = len(to_gather)

    def body(*refs):
        proj_ref, ws_ref, bs_ref, lw_ref, lb_ref, lo_ref, gn_ref = refs[:7]
        c_refs = refs[7:7 + nc]
        refs = refs[7 + nc:]
        g_in = refs[:ng]
        y_ref, st_ref = refs[ng:2 + ng]
        g_out = refs[2 + ng:2 + 2 * ng]
        state = refs[2 + 2 * ng]
        ga = _Gather2(g_in, g_out, *refs[3 + 2 * ng:])

        @pl.when(pl.program_id(0) == 0)
        def _():
            state[...] = jnp.zeros_like(state)
            ga.start()

        @pl.when(pl.program_id(0) == (3 * nt_) // 4)
        def _():
            ga.forward()

        st = tuple(state[h] for h in range(NH))
        consts = tuple(r[...] for r in c_refs)
        for s in range(n_sub):
            rows = slice(s * sub, (s + 1) * sub)
            for h in range(NH):
                st_ref[s, h] = st[h]
            y, st = _mix_tile(proj_ref[rows, :], st, ws_ref[...], bs_ref[...], lw_ref[...], lb_ref[...], lo_ref[...],
                              gn_ref[...], consts)
            y_ref[rows, :] = y.astype(BF)
        for h in range(NH):
            state[h] = st[h]

        @pl.when(pl.program_id(0) == nt_ - 1)
        def _():
            ga.finish()

    full = lambda a: pl.BlockSpec(a.shape, lambda i, nd=a.ndim: (0,) * nd)
    outs = pl.pallas_call(
        body, name="fwd_mix", grid=(nt_,),
        out_shape=(jax.ShapeDtypeStruct((t, D), BF), jax.ShapeDtypeStruct((t // sub, NH, HD, HD), F32),
                   *_exchange_out_shapes(to_gather, False)),
        in_specs=[pl.BlockSpec((mt, D_IN), lambda i: (i, 0)), full(w_s), full(b_s_t), full(ln_w), full(ln_b), full(lower),
                  full(gn_w)] + [full(a) for a in tables] + [_any() for _ in to_gather],
        out_specs=(pl.BlockSpec((mt, D), lambda i: (i, 0)), pl.BlockSpec((n_sub, NH, HD, HD), lambda i: (i, 0, 0, 0)),
                   *[_any() for _ in to_gather]),
        scratch_shapes=[pltpu.VMEM((NH, HD, HD), F32)] + _exchange_sems(ng),
        compiler_params=_params(dimension_semantics=("arbitrary",), has_side_effects=True),
    )(proj, w_s, b_s_t, ln_w, ln_b, lower, gn_w, *tables, *to_gather)
    return outs[0], outs[1], outs[2:]


def _bwd_mix(proj, dycat, states, w_s, b_s_t, ln_w, ln_b, lower, gn_w, tables, to_scatter):
    t = proj.shape[0]
    sub = _row_tile(t, MIX_TILE)
    n_sub = BWD_MIX_SUBTILES if t % (BWD_MIX_SUBTILES * sub) == 0 else 1
    mt = n_sub * sub
    nt_ = t // mt
    nc = len(tables)
    ns = len(to_scatter)

    def body(*refs):
        proj_ref, dy_ref, st_ref, ws_ref, bs_ref, lw_ref, lb_ref, lo_ref, gn_ref = refs[:9]
        c_refs = refs[9:9 + nc]
        refs = refs[9 + nc:]
        s_in = refs[:ns]
        dproj_ref = refs[ns]
        acc = refs[1 + ns:7 + ns]
        s_out = refs[7 + ns:7 + 2 * ns]
        dstate = refs[7 + 2 * ns]
        ex = _Exchange(s_in, s_out, *refs[8 + 2 * ns:], True)

        @pl.when(pl.program_id(0) == 0)
        def _():
            dstate[...] = jnp.zeros_like(dstate)
            for r in acc:
                r[...] = jnp.zeros_like(r)
            ex.start()

        consts = tuple(r[...] for r in c_refs)

        def f(p, s, ws, bs, lw, lb_, lo, gn):
            return _mix_tile(p, s, ws, bs, lw, lb_, lo, gn, consts)

        dst = tuple(dstate[h] for h in range(NH))
        for s in reversed(range(n_sub)):
            rows = slice(s * sub, (s + 1) * sub)
            st = tuple(st_ref[s, h] for h in range(NH))
            _, vjp = jax.vjp(f, proj_ref[rows, :], st, ws_ref[...], bs_ref[...], lw_ref[...], lb_ref[...], lo_ref[...],
                             gn_ref[...])
            grads = vjp((dy_ref[rows, :], dst))
            dproj_ref[rows, :] = grads[0].astype(BF)
            dst = grads[1]
            for r, gval in zip(acc, grads[2:]):
                r[...] += gval
        for h in range(NH):
            dstate[h] = dst[h]

        @pl.when(pl.program_id(0) == nt_ - 1)
        def _():
            ex.wait()

    full = lambda a: pl.BlockSpec(a.shape, lambda i, nd=a.ndim: (0,) * nd)
    rev = lambda i: (nt_ - 1 - i, 0)
    smalls = (w_s, b_s_t, ln_w, ln_b, lower, gn_w)
    outs = pl.pallas_call(
        body, name="bwd_mix", grid=(nt_,),
        out_shape=(jax.ShapeDtypeStruct((t, D_IN), BF), *[jax.ShapeDtypeStruct(a.shape, F32) for a in smalls],
                   *_exchange_out_shapes(to_scatter, True)),
        in_specs=[pl.BlockSpec((mt, D_IN), rev), pl.BlockSpec((mt, D), rev),
                  pl.BlockSpec((n_sub, NH, HD, HD), lambda i: (nt_ - 1 - i, 0, 0, 0))]
        + [full(a) for a in smalls] + [full(a) for a in tables] + [_any() for _ in to_scatter],
        out_specs=(pl.BlockSpec((mt, D_IN), rev), *[full(a) for a in smalls], *[_any() for _ in to_scatter]),
        scratch_shapes=[pltpu.VMEM((NH, HD, HD), F32)] + _exchange_sems(ns),
        compiler_params=_params(dimension_semantics=("arbitrary",), has_side_effects=True),
    )(proj, dycat, states, w_s, b_s_t, ln_w, ln_b, lower, gn_w, *tables, *to_scatter)
    return outs[0], outs[1:7], outs[7:]


def _row_tile(t, want):
    return want if t % want == 0 else t


def _rms(v):
    rstd = lax.rsqrt(jnp.mean(v * v, axis=-1, keepdims=True) + EPS)
    return v * rstd, rstd


def _rms_bwd(dxhat, xhat, rstd):
    return rstd * (dxhat - xhat * jnp.mean(dxhat * xhat, axis=-1, keepdims=True))


def _colsum(v):
    return jnp.sum(v, axis=0, keepdims=True)


def _fwd_in(x, ada, n1w, w_in_st, to_gather):
    t = x.shape[0]
    tm = _row_tile(t, 512)
    n_t = t // tm
    ng = len(to_gather)

    def body(*refs):
        x_ref, ada_ref, n1_ref, wst_ref = refs[:4]
        g_in = refs[4:4 + ng]
        proj_ref, h1_ref = refs[4 + ng:6 + ng]
        g_out = refs[6 + ng:6 + 2 * ng]
        w_ref, sems = refs[6 + 2 * ng:8 + 2 * ng]
        ga = _Gather2(g_in, g_out, *refs[8 + 2 * ng:])

        @pl.when(pl.program_id(0) == 0)
        def _():
            ga.start()
            _load_columns(wst_ref, w_ref, sems)

        @pl.when(pl.program_id(0) == (3 * n_t) // 4)
        def _():
            ga.forward()

        xh, _ = _rms(x_ref[...])
        h1 = (xh * n1_ref[...]) * (1.0 + ada_ref[1:2, :]) + ada_ref[0:1, :]
        h1b = h1.astype(BF)
        h1_ref[...] = h1b
        proj_ref[...] = _mm(h1b, w_ref[...])

        @pl.when(pl.program_id(0) == n_t - 1)
        def _():
            ga.finish()

    full = lambda a: pl.BlockSpec(a.shape, lambda i, nd=a.ndim: (0,) * nd)
    outs = pl.pallas_call(
        body, name="fwd_in", grid=(n_t,),
        out_shape=(jax.ShapeDtypeStruct((t, D_IN), F32), jax.ShapeDtypeStruct((t, D), BF),
                   *_exchange_out_shapes(to_gather, False)),
        in_specs=[pl.BlockSpec((tm, D), lambda i: (i, 0)), full(ada), full(n1w), _any()] + [_any() for _ in to_gather],
        out_specs=(pl.BlockSpec((tm, D_IN), lambda i: (i, 0)), pl.BlockSpec((tm, D), lambda i: (i, 0)),
                   *[_any() for _ in to_gather]),
        scratch_shapes=[pltpu.VMEM((D, D_IN), BF), pltpu.SemaphoreType.DMA((N_DEV,))] + _exchange_sems(ng),
        compiler_params=_params(dimension_semantics=("arbitrary",), has_side_effects=True),
    )(x, ada, n1w, w_in_st, *to_gather)
    return outs[0], outs[1], outs[2:]


def _fwd_ffn(x, ycat, tgt, ada, n2w, fw, w_out, w_fi, w_fo):
    t = x.shape[0]
    tm = _row_tile(t, 256)

    def body(x_ref, y_ref, t_ref, ada_ref, n2_ref, fw_ref, wo_ref, wi_ref, wf_ref,
             x1_ref, mix_ref, h2_ref, act_ref, gu_ref, dffn_ref, dx2_ref, part_ref):
        @pl.when(pl.program_id(0) == 0)
        def _():
            part_ref[...] = jnp.zeros_like(part_ref)

        g1, sh2, sc2, g2 = ada_ref[2:3, :], ada_ref[3:4, :], ada_ref[4:5, :], ada_ref[5:6, :]
        mix = _mm(y_ref[...], wo_ref[...])
        mix_ref[...] = mix.astype(BF)
        x1 = x_ref[...] + g1 * mix
        x1_ref[...] = x1
        xh2, _ = _rms(x1)
        h2b = ((xh2 * n2_ref[...]) * (1.0 + sc2) + sh2).astype(BF)
        h2_ref[...] = h2b
        ffn = jnp.zeros((tm, D), F32)
        for lo, hi in FF_CHUNKS:
            gate = _nt(h2b, wi_ref[lo:hi, :])
            up = _nt(h2b, wi_ref[D_FF + lo:D_FF + hi, :])
            gu_ref[:, lo:hi] = gate.astype(BF)
            gu_ref[:, D_FF + lo:D_FF + hi] = up.astype(BF)
            a = (_silu(gate) * up).astype(BF)
            act_ref[:, lo:hi] = a
            ffn = ffn + _mm(a, wf_ref[lo:hi, :])
        x2 = x1 + g2 * ffn
        xh3, rstd3 = _rms(x2)
        err = xh3 * fw_ref[...] - t_ref[...]
        dy = err * (1.0 / D)
        dx2 = _rms_bwd(dy * fw_ref[...], xh3, rstd3)
        dx2_ref[...] = dx2
        dffn_ref[...] = (g2 * dx2).astype(BF)
        part_ref[0:1, :] += _colsum(dx2 * ffn)
        part_ref[1:2, :] += _colsum(dy * xh3)
        part_ref[2:3, :] += jnp.zeros((1, D), F32) + (0.5 / D) * jnp.sum(err * err)

    full = lambda a: pl.BlockSpec(a.shape, lambda i, nd=a.ndim: (0,) * nd)
    row = lambda w: pl.BlockSpec((tm, w), lambda i: (i, 0))
    return pl.pallas_call(
        body, name="fwd_ffn", grid=(t // tm,),
        out_shape=(jax.ShapeDtypeStruct((t, D), F32), jax.ShapeDtypeStruct((t, D), BF), jax.ShapeDtypeStruct((t, D), BF),
                   jax.ShapeDtypeStruct((t, D_FF), BF), jax.ShapeDtypeStruct((t, 2 * D_FF), BF),
                   jax.ShapeDtypeStruct((t, D), BF), jax.ShapeDtypeStruct((t, D), F32), jax.ShapeDtypeStruct((8, D), F32)),
        in_specs=[row(D), row(D), row(D), full(ada), full(n2w), full(fw), _whole_vmem(), _whole_vmem(), _whole_vmem()],
        out_specs=(row(D), row(D), row(D), row(D_FF), row(2 * D_FF), row(D), row(D),
                   pl.BlockSpec((8, D), lambda i: (0, 0))),
        compiler_params=_params(dimension_semantics=("arbitrary",)),
    )(x, ycat, tgt, ada, n2w, fw, w_out, w_fi, w_fo)


def _bwd_ffn(x1, gu, dffn, dx2, mix, ada, n2w, w_out, w_fi, w_fo):
    t = x1.shape[0]
    tm = _row_tile(t, 256)

    def body(x1_ref, gu_ref, dffn_ref, dx2_ref, mix_ref, ada_ref, n2_ref, wo_ref, wi_ref, wf_ref,
             dgu_ref, dx1_ref, dmix_ref, dycat_ref, part_ref):
        @pl.when(pl.program_id(0) == 0)
        def _():
            part_ref[...] = jnp.zeros_like(part_ref)

        g1, sc2 = ada_ref[2:3, :], ada_ref[4:5, :]
        dffn = dffn_ref[...]
        dh2 = jnp.zeros((tm, D), F32)
        for lo, hi in FF_CHUNKS:
            gate = gu_ref[:, lo:hi].astype(F32)
            up = gu_ref[:, D_FF + lo:D_FF + hi].astype(F32)
            dact = _nt(dffn, wf_ref[lo:hi, :])
            sg = _sig(gate)
            dgate = (dact * up * (sg * (1.0 + gate * (1.0 - sg)))).astype(BF)
            dup = (dact * (gate * sg)).astype(BF)
            dgu_ref[:, lo:hi] = dgate
            dgu_ref[:, D_FF + lo:D_FF + hi] = dup
            dh2 = dh2 + _mm(dgate, wi_ref[lo:hi, :]) + _mm(dup, wi_ref[D_FF + lo:D_FF + hi, :])
        x1 = x1_ref[...]
        xh2, rstd2 = _rms(x1)
        xn2 = xh2 * n2_ref[...]
        dxn2 = dh2 * (1.0 + sc2)
        dx1 = dx2_ref[...] + _rms_bwd(dxn2 * n2_ref[...], xh2, rstd2)
        dx1_ref[...] = dx1
        dmix = (g1 * dx1).astype(BF)
        dmix_ref[...] = dmix
        dycat_ref[...] = _nt(dmix, wo_ref[...])
        part_ref[0:1, :] += _colsum(dh2)
        part_ref[1:2, :] += _colsum(dh2 * xn2)
        part_ref[2:3, :] += _colsum(dxn2 * xh2)
        part_ref[3:4, :] += _colsum(dx1 * mix_ref[...].astype(F32))

    full = lambda a: pl.BlockSpec(a.shape, lambda i, nd=a.ndim: (0,) * nd)
    row = lambda w: pl.BlockSpec((tm, w), lambda i: (i, 0))
    return pl.pallas_call(
        body, name="bwd_ffn", grid=(t // tm,),
        out_shape=(jax.ShapeDtypeStruct((t, 2 * D_FF), BF), jax.ShapeDtypeStruct((t, D), F32),
                   jax.ShapeDtypeStruct((t, D), BF), jax.ShapeDtypeStruct((t, D), F32), jax.ShapeDtypeStruct((8, D), F32)),
        in_specs=[row(D), row(2 * D_FF), row(D), row(D), row(D), full(ada), full(n2w),
                  _whole_vmem(), _whole_vmem(), _whole_vmem()],
        out_specs=(row(2 * D_FF), row(D), row(D), row(D), pl.BlockSpec((8, D), lambda i: (0, 0))),
        compiler_params=_params(dimension_semantics=("arbitrary",)),
    )(x1, gu, dffn, dx2, mix, ada, n2w, w_out, w_fi, w_fo)


def _bwd_in(x, dproj, dx1, ada, n1w, w_in_st, to_scatter, to_gather):
    t = x.shape[0]
    tm = _row_tile(t, 1024)
    n_t = t // tm
    ns = len(to_scatter)
    ng = len(to_gather)
    nx = ns + ng

    def body(*refs):
        x_ref, dp_ref, dx1_ref, ada_ref, n1_ref, wst_ref = refs[:6]
        x_in = refs[6:6 + nx]
        gx_ref, part_ref = refs[6 + nx:8 + nx]
        x_out = refs[8 + nx:8 + 2 * nx]
        w_ref, sems = refs[8 + 2 * nx:10 + 2 * nx]
        sem_refs = refs[10 + 2 * nx:]
        ex = _Exchange(x_in[:ns], x_out[:ns], *sem_refs[:3], True)
        gx = _Exchange(x_in[ns:], x_out[ns:], *sem_refs[3:], False) if ng else None

        @pl.when(pl.program_id(0) == 0)
        def _():
            ex.start()
            if ng:
                gx.start()
            part_ref[...] = jnp.zeros_like(part_ref)
            _load_columns(wst_ref, w_ref, sems)

        dh1 = _nt(dp_ref[...], w_ref[...])
        xh, rstd = _rms(x_ref[...])
        xn = xh * n1_ref[...]
        dxn = dh1 * (1.0 + ada_ref[1:2, :])
        gx_ref[...] = dx1_ref[...] + _rms_bwd(dxn * n1_ref[...], xh, rstd)
        part_ref[0:1, :] += _colsum(dh1)
        part_ref[1:2, :] += _colsum(dh1 * xn)
        part_ref[2:3, :] += _colsum(dxn * xh)

        @pl.when(pl.program_id(0) == n_t - 1)
        def _():
            ex.wait()
            if ng:
                gx.wait()

    full = lambda a: pl.BlockSpec(a.shape, lambda i, nd=a.ndim: (0,) * nd)
    row = lambda w: pl.BlockSpec((tm, w), lambda i: (i, 0))
    outs = pl.pallas_call(
        body, name="bwd_in", grid=(n_t,),
        out_shape=(jax.ShapeDtypeStruct((t, D), F32), jax.ShapeDtypeStruct((8, D), F32),
                   *_exchange_out_shapes(to_scatter, True), *_exchange_out_shapes(to_gather, False)),
        in_specs=[row(D), row(D_IN), row(D), full(ada), full(n1w), _any()] + [_any() for _ in range(nx)],
        out_specs=(row(D), pl.BlockSpec((8, D), lambda i: (0, 0)), *[_any() for _ in range(nx)]),
        scratch_shapes=[pltpu.VMEM((D, D_IN), BF), pltpu.SemaphoreType.DMA((N_DEV,))] + _exchange_sems(ns)
        + (_exchange_sems(ng) if ng else []),
        compiler_params=_params(dimension_semantics=("arbitrary",), has_side_effects=True),
    )(x, dproj, dx1, ada, n1w, w_in_st, *to_scatter, *to_gather)
    return outs[0], outs[1], outs[2:2 + ns], outs[2 + ns:]


def _wgrad(a, b, name, a_spec, b_spec, out_shape, out_spec, grid, acc_shape, split=1, to_gather=()):
    n_j, n_t = grid
    ng = len(to_gather)

    def body(*refs):
        a_ref, b_ref = refs[:2]
        g_in = refs[2:2 + ng]
        o_ref = refs[2 + ng]
        g_out = refs[3 + ng:3 + 2 * ng]
        acc = refs[3 + 2 * ng]
        first = (pl.program_id(0) == 0) & (pl.program_id(1) == 0)
        last = (pl.program_id(0) == n_j - 1) & (pl.program_id(1) == n_t - 1)
        if ng:
            gx = _Exchange(g_in, g_out, *refs[4 + 2 * ng:], False)

            @pl.when(first)
            def _():
                gx.start()

        @pl.when(pl.program_id(1) == 0)
        def _():
            acc[...] = jnp.zeros_like(acc)

        acc[...] += _tn(a_ref[...], b_ref[...])

        @pl.when(pl.program_id(1) == n_t - 1)
        def _():
            if split == 1:
                o_ref[...] = acc[...].astype(BF)
            else:
                w = acc_shape[1] // split
                for s in range(split):
                    o_ref[s] = acc[:, s * w:(s + 1) * w].astype(BF)

        if ng:
            @pl.when(last)
            def _():
                gx.wait()

    outs = pl.pallas_call(
        body, name=name, grid=grid,
        out_shape=(jax.ShapeDtypeStruct(out_shape, BF), *_exchange_out_shapes(to_gather, False)),
        in_specs=[a_spec, b_spec] + [_any() for _ in to_gather], out_specs=(out_spec, *[_any() for _ in to_gather]),
        scratch_shapes=[pltpu.VMEM(acc_shape, F32)] + (_exchange_sems(ng) if ng else []),
        compiler_params=_params(dimension_semantics=("arbitrary", "arbitrary"), has_side_effects=bool(ng)),
    )(a, b, *to_gather)
    return (outs[0], outs[1:]) if ng else outs[0]


def _adamw_math(w, g, m, v):
    m = ADAM_B1 * m + (1.0 - ADAM_B1) * g
    v = ADAM_B2 * v + (1.0 - ADAM_B2) * (g * g)
    m_hat = m / (1.0 - ADAM_B1 ** ADAM_STEP)
    v_hat = v / (1.0 - ADAM_B2 ** ADAM_STEP)
    delta = -ADAM_LR * (m_hat / (jnp.sqrt(v_hat) + ADAM_EPS) + ADAM_WD * w)
    return delta, m, v


def _adamw_recv(w, m, v, recv, name, tr):
    r, c = w.shape

    def body(w_ref, m_ref, v_ref, r_ref, g_ref, d_ref, nm_ref, nv_ref):
        g = r_ref[0].astype(F32)
        for k in range(1, N_DEV):
            g = g + r_ref[k].astype(F32)
        g_ref[...] = g
        d_ref[...], nm_ref[...], nv_ref[...] = _adamw_math(w_ref[...], g, m_ref[...], v_ref[...])

    row = pl.BlockSpec((tr, c), lambda i: (i, 0))
    return pl.pallas_call(
        body, name=name, grid=(r // tr,),
        out_shape=tuple(jax.ShapeDtypeStruct((r, c), F32) for _ in range(4)),
        in_specs=[row, row, row, pl.BlockSpec((N_DEV, tr, c), lambda i: (0, i, 0))],
        out_specs=(row, row, row, row),
        compiler_params=_params(dimension_semantics=("arbitrary",)),
    )(w, m, v, recv)


def _adamw_ada(w, m, v, cact, dada_cols):
    r, c = w.shape
    tr = 256

    def body(w_ref, m_ref, v_ref, ca_ref, da_ref, g_ref, d_ref, nm_ref, nv_ref):
        g = _tn(ca_ref[...].astype(BF), da_ref[...].astype(BF))
        g_ref[...] = g
        d_ref[...], nm_ref[...], nv_ref[...] = _adamw_math(w_ref[...], g, m_ref[...], v_ref[...])

    row = pl.BlockSpec((tr, c), lambda i: (i, 0))
    return pl.pallas_call(
        body, name="adamw_ada", grid=(r // tr,),
        out_shape=tuple(jax.ShapeDtypeStruct((r, c), F32) for _ in range(4)),
        in_specs=[row, row, row, pl.BlockSpec((N_DEV, tr), lambda i: (0, i)), pl.BlockSpec(dada_cols.shape, lambda i: (0, 0))],
        out_specs=(row, row, row, row),
        compiler_params=_params(dimension_semantics=("arbitrary",)),
    )(w, m, v, cact, dada_cols)


def _adamw_small(gathered, wmv):
    n_g = len(gathered)
    n_p = len(wmv)
    flat = [a for trip in wmv for a in trip]

    def body(*refs):
        g_refs = refs[:n_g]
        p_refs = refs[n_g:n_g + 3 * n_p]
        o_refs = refs[n_g + 3 * n_p:]

        def total(ref):
            s = ref[0]
            for k in range(1, N_DEV):
                s = s + ref[k]
            return s

        f3, b3, b1, dws, dbs, dlnw, dlnb, dlo, dgn = [total(r) for r in g_refs]
        dada_rows = [b1[0:1], b1[1:2], b3[3:4], b3[0:1], b3[1:2], f3[0:1]]
        for r, g in enumerate(dada_rows):
            cs = slice(r * D, (r + 1) * D)
            w, m, v = p_refs[0][:, cs], p_refs[1][:, cs], p_refs[2][:, cs]
            o_refs[0][:, cs] = g
            o_refs[1][:, cs], o_refs[2][:, cs], o_refs[3][:, cs] = _adamw_math(w, g, m, v)
        grads = [None, b1[2:3], dws, dbs, dlnw, dlnb, dlo, dgn, b3[2:3], f3[1:2]]
        for i, g in enumerate(grads):
            if g is None:
                continue
            w, m, v = p_refs[3 * i][...], p_refs[3 * i + 1][...], p_refs[3 * i + 2][...]
            o_refs[4 * i][...] = g
            o_refs[4 * i + 1][...], o_refs[4 * i + 2][...], o_refs[4 * i + 3][...] = _adamw_math(w, g, m, v)
        o_refs[4 * n_p][...] = jnp.zeros((8, 128), F32) + f3[2:3, 0:128]

    out_shape = []
    for w, _, _ in wmv:
        out_shape += [jax.ShapeDtypeStruct(w.shape, F32)] * 4
    out_shape.append(jax.ShapeDtypeStruct((8, 128), F32))
    n_in = n_g + 3 * n_p
    return pl.pallas_call(
        body, name="adamw_small",
        out_shape=tuple(out_shape),
        in_specs=[_whole_vmem()] * n_in, out_specs=tuple(_whole_vmem() for _ in out_shape),
        compiler_params=_params(),
    )(*gathered, *flat)


def kernel(x, c, w_ada, b_ada, norm1_w, w_in, w_s, b_s, v_ln_w, v_ln_b, lower_bounds, gn_w, w_out, norm2_w, w_ffn_in, w_ffn_out, final_norm_w, loss_target, m_w_ada, m_b_ada, m_norm1_w, m_w_in, m_w_s, m_b_s, m_v_ln_w, m_v_ln_b, m_lower_bounds, m_gn_w, m_w_out, m_norm2_w, m_w_ffn_in, m_w_ffn_out, m_final_norm_w, v_w_ada, v_b_ada, v_norm1_w, v_w_in, v_w_s, v_b_s, v_v_ln_w, v_v_ln_b, v_lower_bounds, v_gn_w, v_w_out, v_norm2_w, v_w_ffn_in, v_w_ffn_out, v_final_norm_w):
    me = 4 * lax.axis_index("x") + 2 * lax.axis_index("y") + lax.axis_index("c")
    t = x.shape[1]
    x2d = x.reshape(t, D)
    tgt = loss_target.reshape(t, D)
    ada_cols = w_ada.shape[2]

    tp = lambda a: jnp.swapaxes(a[0], 0, 1)
    win_b, wout_b, wfi_b, wfo_b = _cast_bf16([w_in[0], w_out[0], tp(w_ffn_in), w_ffn_out[0]], "cast_weights")

    b_cols = lax.dynamic_slice(b_ada, (0, me * ada_cols), (1, ada_cols))
    win_st, ada_st, cact = _startup(win_b, c, w_ada[0], b_cols)
    ada = lax.dynamic_index_in_dim(ada_st, me, axis=1, keepdims=False).reshape(6, D)

    tables = _decay_tables()
    ws3 = w_s[0]
    bs_t = b_s[0].T

    proj, h1, (wout_st, wfo_st) = _fwd_in(x2d, ada, norm1_w, win_st, [wout_b, wfo_b])
    ycat, states, (wfi_st,) = _fwd_mix(proj, ws3, bs_t, v_ln_w, v_ln_b, lower_bounds, gn_w, tables, [wfi_b])
    w_out_full = wout_st.reshape(D, D)
    w_fo_full = wfo_st.reshape(D_FF, D)
    w_fi_full = wfi_st.reshape(2 * D_FF, D)
    x1, mixb, h2, act, gu, dffn, dx2, part_f = _fwd_ffn(x2d, ycat, tgt, ada, norm2_w, final_norm_w.reshape(1, D),
                                                        w_out_full, w_fi_full, w_fo_full)
    dgu, dx1, dmix, dycat, part_b3 = _bwd_ffn(x1, gu, dffn, dx2, mixb, ada, norm2_w, w_out_full, w_fi_full, w_fo_full)
    tk = _row_tile(t, 2048)
    n_t = t // tk
    win_cols = D_IN // N_DEV
    dwout = _wgrad(ycat, dmix, "wgrad_out",
                   pl.BlockSpec((tk, D), lambda j, i: (i, 0)), pl.BlockSpec((tk, D), lambda j, i: (i, 0)),
                   (D, D), pl.BlockSpec((D, D), lambda j, i: (0, 0)), (1, n_t), (D, D))
    dwfi = _wgrad(dgu, h2, "wgrad_ffn_in",
                  pl.BlockSpec((tk, FF_PAIR), lambda j, i: (i, j)), pl.BlockSpec((tk, D), lambda j, i: (i, 0)),
                  (4, FF_PAIR, D), pl.BlockSpec((None, FF_PAIR, D), lambda j, i: (j, 0, 0)), (4, n_t), (FF_PAIR, D))
    dwfo = _wgrad(act, dffn, "wgrad_ffn_out",
                  pl.BlockSpec((tk, FF_PAIR), lambda j, i: (i, j)), pl.BlockSpec((tk, D), lambda j, i: (i, 0)),
                  (2, FF_PAIR, D), pl.BlockSpec((None, FF_PAIR, D), lambda j, i: (j, 0, 0)), (2, n_t), (FF_PAIR, D))
    dproj, (dws, dbs_t, dlnw, dlnb, dlower, dgnw), (r_out, r_fi, r_fo) = _bwd_mix(
        proj, dycat, states, ws3, bs_t, v_ln_w, v_ln_b, lower_bounds, gn_w, tables,
        [dwout.reshape(N_DEV, D // N_DEV, D), dwfi.reshape(N_DEV, FF_BLK, D), dwfo.reshape(N_DEV, D_FF // N_DEV, D)])
    dwin, early = _wgrad(h1, dproj, "wgrad_in",
                         pl.BlockSpec((tk, D), lambda j, i: (i, 0)), pl.BlockSpec((tk, 4 * win_cols), lambda j, i: (i, j)),
                         (N_DEV, D, win_cols), pl.BlockSpec((4, D, win_cols), lambda j, i: (j, 0, 0)), (N_DEV // 4, n_t),
                         (D, 4 * win_cols), split=4, to_gather=[part_f, part_b3, dws, dbs_t, dlnw, dlnb, dlower, dgnw])
    grad_x, part_b1, (r_in,), _ = _bwd_in(x2d, dproj, dx1, ada, norm1_w, win_st, [dwin], [])

    g_w_in, d_w_in, nm_w_in, nv_w_in = _adamw_recv(w_in[0], m_w_in[0], v_w_in[0], r_in, "adamw_w_in", 256)
    g_w_out, d_w_out, nm_w_out, nv_w_out = _adamw_recv(w_out[0], m_w_out[0], v_w_out[0], r_out, "adamw_w_out", 128)
    fi_t = _adamw_recv(tp(w_ffn_in), tp(m_w_ffn_in), tp(v_w_ffn_in), r_fi, "adamw_w_ffn_in", 176)
    g_w_fi, d_w_fi, nm_w_fi, nv_w_fi = [jnp.swapaxes(a, 0, 1) for a in fi_t]
    g_w_fo, d_w_fo, nm_w_fo, nv_w_fo = _adamw_recv(w_ffn_out[0], m_w_ffn_out[0], v_w_ffn_out[0], r_fo, "adamw_w_ffn_out", 176)

    (b1_all,) = _exchange([part_b1], "gather_small", False, True)
    gathered = [early[0], early[1], b1_all, *early[2:]]
    f3_all, b3_all = gathered[0], gathered[1]
    dada_all = jnp.stack([b1_all[:, 0], b1_all[:, 1], b3_all[:, 3], b3_all[:, 0], b3_all[:, 1], f3_all[:, 0]], axis=1)
    dada_cols = lax.dynamic_slice(dada_all.reshape(N_DEV, 6 * D), (0, me * ada_cols), (N_DEV, ada_cols))
    g_w_ada, d_w_ada, nm_w_ada, nv_w_ada = _adamw_ada(w_ada[0], m_w_ada[0], v_w_ada[0], cact, dada_cols)

    r1 = lambda a: a.reshape(1, D)
    tr = lambda a: a[0].T
    wmv = [
        (b_ada, m_b_ada, v_b_ada),
        (norm1_w, m_norm1_w, v_norm1_w),
        (w_s[0], m_w_s[0], v_w_s[0]),
        (tr(b_s), tr(m_b_s), tr(v_b_s)),
        (v_ln_w, m_v_ln_w, v_v_ln_w),
        (v_ln_b, m_v_ln_b, v_v_ln_b),
        (lower_bounds, m_lower_bounds, v_lower_bounds),
        (gn_w, m_gn_w, v_gn_w),
        (norm2_w, m_norm2_w, v_norm2_w),
        (r1(final_norm_w), r1(m_final_norm_w), r1(v_final_norm_w)),
    ]
    small = _adamw_small(gathered, wmv)
    loss = small[-1][0, 0]

    def unshape(i, a):
        if i == 2:
            return a.reshape(1, NH, BLK, BLK)
        if i == 3:
            return a.T.reshape(1, NH, BLK)
        if i == 9:
            return a.reshape(D)
        return a

    def small_out(kind):
        return [unshape(i, small[4 * i + kind]) for i in range(len(wmv))]

    e3 = lambda a: a[None]
    big = {
        0: (e3(g_w_ada), e3(g_w_in), e3(g_w_out), e3(g_w_fi), e3(g_w_fo)),
        1: (e3(d_w_ada), e3(d_w_in), e3(d_w_out), e3(d_w_fi), e3(d_w_fo)),
        2: (e3(nm_w_ada), e3(nm_w_in), e3(nm_w_out), e3(nm_w_fi), e3(nm_w_fo)),
        3: (e3(nv_w_ada), e3(nv_w_in), e3(nv_w_out), e3(nv_w_fi), e3(nv_w_fo)),
    }

    def ordered(kind):
        s = small_out(kind)
        b_ = big[kind]
        return [b_[0], s[0], s[1], b_[1], s[2], s[3], s[4], s[5], s[6], s[7], b_[2], s[8], b_[3], b_[4], s[9]]

    return (loss, grad_x.reshape(1, t, D), *ordered(0), *ordered(1), *ordered(2), *ordered(3))
```

```python
import numpy as np
import jax
import jax.numpy as jnp
from jax import lax
from jax.experimental import pallas as pl
from jax.experimental.pallas import tpu as pltpu

F32 = jnp.float32
BF = jnp.bfloat16
MESH = pl.DeviceIdType.MESH

N_DEV = 8
D = 1024
D_IN = 3072
D_FF = 2816
FF_BLK = D_FF // 4
FF_CHUNKS = ((0, D_FF),)
FF_PAIR = 2 * FF_BLK
CH = 64
BLK = 128
MIX_TILE = 256
FWD_MIX_SUBTILES = 4
BWD_MIX_SUBTILES = 2
NH = 4
HD = 128
EPS = 1e-6
LEVELS = (32, 16, 8, 4, 2, 1)

ADAM_LR = 0.001
ADAM_B1 = 0.9
ADAM_B2 = 0.999
ADAM_EPS = 1e-08
ADAM_WD = 0.01
ADAM_STEP = 10

VMEM_LIMIT = 56 * 1024 * 1024


def _params(**kw):
    return pltpu.CompilerParams(vmem_limit_bytes=VMEM_LIMIT, **kw)


def _whole_vmem():
    return pl.BlockSpec(memory_space=pltpu.VMEM)


def _any():
    return pl.BlockSpec(memory_space=pl.ANY)


class _Exchange:
    def __init__(self, ins, outs, send_sems, recv_sems, local_sems, scatter):
        self.ins, self.outs, self.scatter = ins, outs, scatter
        self.send_sems, self.recv_sems, self.local_sems = send_sems, recv_sems, local_sems
        x, y, c = lax.axis_index("x"), lax.axis_index("y"), lax.axis_index("c")
        self.me = 4 * x + 2 * y + c
        self.peers = []
        for k in range(1, N_DEV):
            peer = (1 - x if (k >> 2) & 1 else x, 1 - y if (k >> 1) & 1 else y, 1 - c if k & 1 else c)
            self.peers.append((peer, 4 * peer[0] + 2 * peer[1] + peer[2]))

    def _src(self, a, idx):
        return self.ins[a].at[idx] if self.scatter else self.ins[a]

    def _local(self):
        return [pltpu.make_async_copy(self._src(a, self.me), self.outs[a].at[self.me], self.local_sems.at[a])
                for a in range(len(self.ins))]

    def _remote(self, a, k, dst_slot):
        peer, peer_idx = self.peers[k]
        return pltpu.make_async_remote_copy(
            src_ref=self._src(a, peer_idx), dst_ref=self.outs[a].at[dst_slot],
            send_sem=self.send_sems.at[a, k], recv_sem=self.recv_sems.at[a, k],
            device_id=peer, device_id_type=MESH)

    def start(self):
        for cp in self._local():
            cp.start()
        for k in range(N_DEV - 1):
            for a in range(len(self.ins)):
                self._remote(a, k, self.me).start()

    def wait(self):
        for k in range(N_DEV - 1):
            for a in range(len(self.ins)):
                self._remote(a, k, self.peers[k][1]).wait()
        for cp in self._local():
            cp.wait()


class _Gather2:
    def __init__(self, ins, outs, send_sems, recv_sems, local_sems):
        self.ins, self.outs = ins, outs
        self.send_sems, self.recv_sems, self.local_sems = send_sems, recv_sems, local_sems
        x, y, c = lax.axis_index("x"), lax.axis_index("y"), lax.axis_index("c")
        self.c = c
        self.me = 4 * x + 2 * y + c
        self.sibling = (x, y, 1 - c)
        self.chips = [(1 - x, y), (x, 1 - y), (1 - x, 1 - y)]

    @staticmethod
    def _idx(px, py, pc):
        return 4 * px + 2 * py + pc

    def _copy(self, a, k, slot, to, own):
        src = self.ins[a] if own else self.outs[a].at[slot]
        return pltpu.make_async_remote_copy(
            src_ref=src, dst_ref=self.outs[a].at[slot],
            send_sem=self.send_sems.at[a, k], recv_sem=self.recv_sems.at[a, k],
            device_id=to, device_id_type=MESH)

    def _local(self):
        return [pltpu.make_async_copy(self.ins[a], self.outs[a].at[self.me], self.local_sems.at[a])
                for a in range(len(self.ins))]

    def start(self):
        for cp in self._local():
            cp.start()
        for a in range(len(self.ins)):
            self._copy(a, 0, self.me, self.sibling, True).start()
            for j, chip in enumerate(self.chips):
                self._copy(a, 1 + j, self.me, (*chip, self.c), True).start()

    def forward(self):
        for j, chip in enumerate(self.chips):
            for a in range(len(self.ins)):
                slot = self._idx(*chip, self.c)
                self._copy(a, 1 + j, slot, (*chip, self.c), True).wait_recv()
                self._copy(a, 4 + j, slot, self.sibling, False).start()

    def finish(self):
        for a in range(len(self.ins)):
            self._copy(a, 0, self._idx(*self.sibling), self.sibling, True).wait_recv()
            for j, chip in enumerate(self.chips):
                self._copy(a, 4 + j, self._idx(*chip, 1 - self.c), self.sibling, False).wait_recv()
            self._copy(a, 0, self.me, self.sibling, True).wait_send()
            for j, chip in enumerate(self.chips):
                self._copy(a, 1 + j, self.me, (*chip, self.c), True).wait_send()
                self._copy(a, 4 + j, self._idx(*chip, self.c), self.sibling, False).wait_send()
        for cp in self._local():
            cp.wait()


def _exchange_sems(n):
    return [pltpu.SemaphoreType.DMA((n, N_DEV - 1)), pltpu.SemaphoreType.DMA((n, N_DEV - 1)), pltpu.SemaphoreType.DMA((n,))]


def _exchange_out_shapes(arrays, scatter):
    return [jax.ShapeDtypeStruct(a.shape if scatter else (N_DEV,) + a.shape, a.dtype) for a in arrays]


def _exchange(arrays, name, scatter, in_vmem):
    n = len(arrays)

    def body(*refs):
        ex = _Exchange(refs[:n], refs[n:2 * n], *refs[2 * n:], scatter)
        ex.start()
        ex.wait()

    spec = _whole_vmem if in_vmem else _any
    return pl.pallas_call(
        body, name=name,
        out_shape=tuple(_exchange_out_shapes(arrays, scatter)),
        in_specs=[spec() for _ in arrays],
        out_specs=tuple(spec() for _ in arrays),
        scratch_shapes=_exchange_sems(n),
        compiler_params=_params(has_side_effects=True),
    )(*arrays)


def _cast_bf16(arrays, name):
    n = len(arrays)

    def body(*refs):
        for i in range(n):
            refs[n + i][...] = refs[i][...].astype(BF)

    return pl.pallas_call(
        body, name=name,
        out_shape=tuple(jax.ShapeDtypeStruct(a.shape, BF) for a in arrays),
        in_specs=[_whole_vmem() for _ in arrays],
        out_specs=tuple(_whole_vmem() for _ in arrays),
        compiler_params=_params(),
    )(*arrays)


def _load_columns(stacked_ref, full_ref, sems):
    c = stacked_ref.shape[2]
    cps = [pltpu.make_async_copy(stacked_ref.at[d], full_ref.at[:, pl.ds(d * c, c)], sems.at[d]) for d in range(N_DEV)]
    for cp in cps:
        cp.start()
    for cp in cps:
        cp.wait()


def _sig(v):
    return 0.5 * jnp.tanh(0.5 * v) + 0.5


@jax.custom_vjp
def _silu(v):
    return v * _sig(v)


def _silu_fwd(v):
    return _silu(v), v


def _silu_bwd(v, g):
    s = _sig(v)
    return (g * (s * (1.0 + v * (1.0 - s))),)


_silu.defvjp(_silu_fwd, _silu_bwd)


@jax.custom_vjp
def _sigmoid_rel(v):
    e = jnp.exp(-jnp.abs(v))
    d = 1.0 + e
    r = pl.reciprocal(d, approx=True)
    r = r * (2.0 - d * r)
    r = r * (2.0 - d * r)
    return jnp.where(v >= 0.0, r, e * r)


def _sigmoid_rel_fwd(v):
    s = _sigmoid_rel(v)
    return s, s


def _sigmoid_rel_bwd(s, g):
    return (g * (s * (1.0 - s)),)


_sigmoid_rel.defvjp(_sigmoid_rel_fwd, _sigmoid_rel_bwd)


def _gelu(v):
    return 0.5 * v * (1.0 + lax.erf(v * 0.7071067811865476))


def _dot(a, b, ca, cb):
    return lax.dot_general(a, b, (((ca,), (cb,)), ((), ())), preferred_element_type=F32)


@jax.custom_vjp
def _mm(a, b):
    return _dot(a, b, 1, 0)


def _mm_fwd(a, b):
    return _dot(a, b, 1, 0), (a, b)


def _mm_bwd(res, g):
    a, b = res
    gb = g.astype(BF)
    return _dot(gb, b, 1, 1).astype(a.dtype), _dot(a, gb, 0, 0).astype(b.dtype)


_mm.defvjp(_mm_fwd, _mm_bwd)


@jax.custom_vjp
def _nt(a, b):
    return _dot(a, b, 1, 1)


def _nt_fwd(a, b):
    return _dot(a, b, 1, 1), (a, b)


def _nt_bwd(res, g):
    a, b = res
    gb = g.astype(BF)
    return _dot(gb, b, 1, 0).astype(a.dtype), _dot(gb, a, 0, 0).astype(b.dtype)


_nt.defvjp(_nt_fwd, _nt_bwd)


@jax.custom_vjp
def _tn(a, b):
    return _dot(a, b, 0, 0)


def _tn_fwd(a, b):
    return _dot(a, b, 0, 0), (a, b)


def _tn_bwd(res, g):
    a, b = res
    gb = g.astype(BF)
    return _dot(b, gb, 1, 1).astype(a.dtype), _dot(a, gb, 1, 0).astype(b.dtype)


_tn.defvjp(_tn_fwd, _tn_bwd)


def _make_cast_dot(ca, cb, da_dims, db_dims):
    @jax.custom_vjp
    def dot(a, b):
        return _dot(a.astype(BF), b.astype(BF), ca, cb)

    def fwd(a, b):
        ab, bb = a.astype(BF), b.astype(BF)
        return _dot(ab, bb, ca, cb), (ab, bb)

    def bwd(res, g):
        ops = {"a": res[0], "b": res[1], "g": g.astype(BF)}
        return (_dot(ops[da_dims[0]], ops[da_dims[1]], da_dims[2], da_dims[3]),
                _dot(ops[db_dims[0]], ops[db_dims[1]], db_dims[2], db_dims[3]))

    dot.defvjp(fwd, bwd)
    return dot


_mm_c = _make_cast_dot(1, 0, ("g", "b", 1, 1), ("a", "g", 0, 0))
_nt_c = _make_cast_dot(1, 1, ("g", "b", 1, 0), ("g", "a", 0, 0))
_tn_c = _make_cast_dot(0, 0, ("b", "g", 1, 1), ("a", "g", 1, 0))


def _startup(win_b, c, w_ada, b_cols):
    ncol = w_ada.shape[1]

    def body(win_ref, c_ref, w_ref, b_ref, winst_ref, adast_ref, cact_ref, call_ref, blk_ref,
             ws, wr, wl, cs, cr, cl, as_, ar, al):
        big = _Gather2([win_ref], [winst_ref], ws, wr, wl)
        big.start()
        gc = _Exchange([c_ref], [call_ref], cs, cr, cl, False)
        gc.start()
        gc.wait()
        ca = _silu(call_ref[...].reshape(N_DEV, D))
        cact_ref[...] = ca
        blk_ref[...] = _mm(ca.astype(BF), w_ref[...].astype(BF)) + b_ref[...]
        ga = _Exchange([blk_ref], [adast_ref], as_, ar, al, False)
        ga.start()
        ga.wait()
        big.forward()
        big.finish()

    return pl.pallas_call(
        body, name="startup",
        out_shape=(jax.ShapeDtypeStruct((N_DEV,) + win_b.shape, BF), jax.ShapeDtypeStruct((N_DEV, N_DEV, ncol), F32),
                   jax.ShapeDtypeStruct((N_DEV, D), F32)),
        in_specs=[_any(), _whole_vmem(), _whole_vmem(), _whole_vmem()],
        out_specs=(_any(), _whole_vmem(), _whole_vmem()),
        scratch_shapes=[pltpu.VMEM((N_DEV, 1, D), F32), pltpu.VMEM((N_DEV, ncol), F32)]
        + _exchange_sems(1) + _exchange_sems(1) + _exchange_sems(1),
        compiler_params=_params(has_side_effects=True),
    )(win_b, c, w_ada, b_cols)


def _decay_tables():
    t = np.arange(CH)
    tri = (t[None, :] <= t[:, None]).astype(np.float32)
    masks = []
    for h in LEVELS:
        m = (t // (2 * h)) * (2 * h) + h
        upper = t >= m
        same = (t[:, None] // (2 * h)) == (t[None, :] // (2 * h))
        masks.append(same & upper[:, None] & (~upper)[None, :])
    masks.append(np.eye(CH, dtype=bool))
    lv = [np.where((t % (2 * h)) >= h, 1.0, -1.0) for h in LEVELS[:4]]
    m4 = t % 4
    lv += [(m4 == 0) * 1.0, (m4 >= 2) * 1.0, (m4 == 3) * 1.0, (t % 2 == 1) * 1.0]
    lvl = np.broadcast_to(np.stack(lv)[:, :, None], (8, CH, D // 2)).astype(np.float32)
    cid = np.arange(BLK) // CH
    gmask = (cid[:, None] >= cid[None, :]).astype(np.float32)
    masks = np.stack(masks).astype(np.float32)
    stacked = np.zeros((masks.shape[0], NH * CH, NH * CH), np.float32)
    for h in range(NH):
        stacked[:, h * CH:(h + 1) * CH, h * CH:(h + 1) * CH] = masks
    return (jnp.asarray(tri, BF), jnp.asarray(stacked), jnp.asarray(gmask), jnp.asarray(lvl))


def _split2(v):
    v1 = v.astype(BF)
    return v1, (v - v1.astype(F32)).astype(BF)


@jax.custom_vjp
def _cumsum_mm(tri, v):
    p1, p2 = _split2(v)
    return _dot(tri, p1, 1, 0) + _dot(tri, p2, 1, 0)


def _cumsum_mm_fwd(tri, v):
    return _cumsum_mm(tri, v), tri


def _cumsum_mm_bwd(tri, g):
    p1, p2 = _split2(g)
    return jnp.zeros_like(tri), _dot(tri, p1, 0, 0) + _dot(tri, p2, 0, 0)


_cumsum_mm.defvjp(_cumsum_mm_fwd, _cumsum_mm_bwd)


def _make_row_roll(shift):
    @jax.custom_vjp
    def roll(x):
        return pltpu.roll(x, shift % CH, 0)

    def fwd(x):
        return roll(x), None

    def bwd(_, g):
        return (pltpu.roll(g, (-shift) % CH, 0),)

    roll.defvjp(fwd, bwd)
    return roll


_prev_row = _make_row_roll(1)
_next_row = _make_row_roll(-1)


def _mix_tile(proj, state, w_s, b_s_t, ln_w, ln_b, lower, gn_w, consts):
    tri, masks, gmask, lvl = consts
    mt = proj.shape[0]
    u = proj[:, 0:512]
    v = proj[:, 512:1024]
    q = proj[:, 1024:1536]
    fl = proj[:, 1536:2048]
    inp = proj[:, 2048:2560]
    g = proj[:, 2560:3072]

    ug = _gelu(u)
    vg = _gelu(v)
    mu = jnp.mean(vg, axis=-1, keepdims=True)
    vc = vg - mu
    var = jnp.mean(vc * vc, axis=-1, keepdims=True)
    vn = vc * lax.rsqrt(var + EPS) * ln_w + ln_b
    wsm = [w_s[h] * gmask for h in range(NH)]
    ya = [[None] * NH for _ in range(mt // BLK)]
    for bi in range(mt // BLK):
        rows = slice(bi * BLK, (bi + 1) * BLK)
        for h in range(NH):
            cols = slice(h * HD, (h + 1) * HD)
            ya[bi][h] = ug[rows, cols] * (_mm_c(wsm[h], vn[rows, cols]) + b_s_t[:, h:h + 1])
    ya_full = jnp.concatenate([jnp.concatenate(r, axis=1) for r in ya], axis=0)

    l0 = lower[0:1, :]
    l1 = lower[1:2, :]
    mx = jnp.maximum(l0, l1)
    e0 = jnp.exp(l0 - mx)
    e1 = jnp.exp(l1 - mx)
    lb = e0 / (e0 + e1)
    qf = _silu(q)
    f = lb + (1.0 - lb) * _sigmoid_rel(fl)
    logf = jnp.log(f)
    kk = 1.0 - f
    gate = _silu(g)
    nl = len(LEVELS)
    half = D // 2
    heads_to_rows = lambda a: jnp.concatenate([a[:, h * HD:(h + 1) * HD] for h in range(NH)], axis=0)
    st = list(state)
    yb = [[None] * NH for _ in range(mt // CH)]
    for ci in range(mt // CH):
        rows = slice(ci * CH, (ci + 1) * CH)
        lc = logf[rows]
        b = _cumsum_mm(tri, lc)
        xb = jnp.exp(b)
        xinv = jnp.exp(b[CH - 1:CH, :] - b)
        xl = []
        for i, hs in enumerate(LEVELS[:3]):
            refs = [jnp.broadcast_to(b[r:r + 1, :], (2 * hs, half)) for r in range(hs - 1, CH, 2 * hs)]
            bref = refs[0] if len(refs) == 1 else jnp.concatenate(refs, axis=0)
            xl.append(jnp.exp(lvl[i] * (b - bref)))
        b3 = b.reshape(CH // 8, 8, half)
        bref = jnp.broadcast_to(b3[:, 3:4, :], (CH // 8, 8, half)).reshape(CH, half)
        xl.append(jnp.exp(lvl[3] * (b - bref)))
        xl.append(jnp.exp(lvl[4] * _next_row(lc) + lvl[5] * lc + lvl[6] * _prev_row(lc)))
        xl.append(jnp.exp(lvl[7] * lc))
        qc = qf[rows]
        kc = kk[rows]
        ic = inp[rows]
        zsrc = [jnp.concatenate([(qc if (r0 // hs) % 2 == 1 else kc)[r0:r0 + hs] for r0 in range(0, CH, hs)], axis=0)
                for hs in LEVELS[:3]]
        qs, ks = heads_to_rows(qc), heads_to_rows(kc)
        qx = qc * xb
        inter = jnp.concatenate([_nt_c(qx[:, h * HD:(h + 1) * HD], st[h]) for h in range(NH)], axis=0)
        attn = masks[nl] * _nt_c(qs, ks)
        for li in range(nl):
            xs = heads_to_rows(xl[li])
            if li < 3:
                z = heads_to_rows(zsrc[li]) * xs
                pairs = _nt_c(z, z)
            else:
                pairs = _nt_c(qs * xs, ks * xs)
            attn = attn + masks[li] * pairs
        o = inter + _mm_c(attn, heads_to_rows(ic))
        kx = kc * xinv
        decay = xb[CH - 1:CH, :]
        st = [st[h] * decay[:, h * HD:(h + 1) * HD] + _tn_c(ic[:, h * HD:(h + 1) * HD], kx[:, h * HD:(h + 1) * HD])
              for h in range(NH)]
        rs = lax.rsqrt(jnp.mean(o * o, axis=-1, keepdims=True) + EPS)
        ys = o * rs * gn_w * heads_to_rows(gate[rows])
        yb[ci] = [ys[h * CH:(h + 1) * CH] for h in range(NH)]
    yb_full = jnp.concatenate([jnp.concatenate(r, axis=1) for r in yb], axis=0)
    return jnp.concatenate([ya_full, yb_full], axis=1), tuple(st)


def _fwd_mix(proj, w_s, b_s_t, ln_w, ln_b, lower, gn_w, tables, to_gather):
    t = proj.shape[0]
    sub = _row_tile(t, MIX_TILE)
    n_sub = FWD_MIX_SUBTILES if t % (FWD_MIX_SUBTILES * sub) == 0 else 1
    mt = n_sub * sub
    nt_ = t // mt
    nc = len(tables)
    ng = len(to_gather)

    def body(*refs):
        proj_ref, ws_ref, bs_ref, lw_ref, lb_ref, lo_ref, gn_ref = refs[:7]
        c_refs = refs[7:7 + nc]
        refs = refs[7 + nc:]
        g_in = refs[:ng]
        y_ref, st_ref = refs[ng:2 + ng]
        g_out = refs[2 + ng:2 + 2 * ng]
        state = refs[2 + 2 * ng]
        ga = _Gather2(g_in, g_out, *refs[3 + 2 * ng:])

        @pl.when(pl.program_id(0) == 0)
        def _():
            state[...] = jnp.zeros_like(state)
            ga.start()

        @pl.when(pl.program_id(0) == (3 * nt_) // 4)
        def _():
            ga.forward()

        st = tuple(state[h] for h in range(NH))
        consts = tuple(r[...] for r in c_refs)
        for s in range(n_sub):
            rows = slice(s * sub, (s + 1) * sub)
            for h in range(NH):
                st_ref[s, h] = st[h]
            y, st = _mix_tile(proj_ref[rows, :], st, ws_ref[...], bs_ref[...], lw_ref[...], lb_ref[...], lo_ref[...],
                              gn_ref[...], consts)
            y_ref[rows, :] = y.astype(BF)
        for h in range(NH):
            state[h] = st[h]

        @pl.when(pl.program_id(0) == nt_ - 1)
        def _():
            ga.finish()

    full = lambda a: pl.BlockSpec(a.shape, lambda i, nd=a.ndim: (0,) * nd)
    outs = pl.pallas_call(
        body, name="fwd_mix", grid=(nt_,),
        out_shape=(jax.ShapeDtypeStruct((t, D), BF), jax.ShapeDtypeStruct((t // sub, NH, HD, HD), F32),
                   *_exchange_out_shapes(to_gather, False)),
        in_specs=[pl.BlockSpec((mt, D_IN), lambda i: (i, 0)), full(w_s), full(b_s_t), full(ln_w), full(ln_b), full(lower),
                  full(gn_w)] + [full(a) for a in tables] + [_any() for _ in to_gather],
        out_specs=(pl.BlockSpec((mt, D), lambda i: (i, 0)), pl.BlockSpec((n_sub, NH, HD, HD), lambda i: (i, 0, 0, 0)),
                   *[_any() for _ in to_gather]),
        scratch_shapes=[pltpu.VMEM((NH, HD, HD), F32)] + _exchange_sems(ng),
        compiler_params=_params(dimension_semantics=("arbitrary",), has_side_effects=True),
    )(proj, w_s, b_s_t, ln_w, ln_b, lower, gn_w, *tables, *to_gather)
    return outs[0], outs[1], outs[2:]


def _bwd_mix(proj, dycat, states, w_s, b_s_t, ln_w, ln_b, lower, gn_w, tables, to_scatter):
    t = proj.shape[0]
    sub = _row_tile(t, MIX_TILE)
    n_sub = BWD_MIX_SUBTILES if t % (BWD_MIX_SUBTILES * sub) == 0 else 1
    mt = n_sub * sub
    nt_ = t // mt
    nc = len(tables)
    ns = len(to_scatter)

    def body(*refs):
        proj_ref, dy_ref, st_ref, ws_ref, bs_ref, lw_ref, lb_ref, lo_ref, gn_ref = refs[:9]
        c_refs = refs[9:9 + nc]
        refs = refs[9 + nc:]
        s_in = refs[:ns]
        dproj_ref = refs[ns]
        acc = refs[1 + ns:7 + ns]
        s_out = refs[7 + ns:7 + 2 * ns]
        dstate = refs[7 + 2 * ns]
        ex = _Exchange(s_in, s_out, *refs[8 + 2 * ns:], True)

        @pl.when(pl.program_id(0) == 0)
        def _():
            dstate[...] = jnp.zeros_like(dstate)
            for r in acc:
                r[...] = jnp.zeros_like(r)
            ex.start()

        consts = tuple(r[...] for r in c_refs)

        def f(p, s, ws, bs, lw, lb_, lo, gn):
            return _mix_tile(p, s, ws, bs, lw, lb_, lo, gn, consts)

        dst = tuple(dstate[h] for h in range(NH))
        for s in reversed(range(n_sub)):
            rows = slice(s * sub, (s + 1) * sub)
            st = tuple(st_ref[s, h] for h in range(NH))
            _, vjp = jax.vjp(f, proj_ref[rows, :], st, ws_ref[...], bs_ref[...], lw_ref[...], lb_ref[...], lo_ref[...],
                             gn_ref[...])
            grads = vjp((dy_ref[rows, :], dst))
            dproj_ref[rows, :] = grads[0].astype(BF)
            dst = grads[1]
            for r, gval in zip(acc, grads[2:]):
                r[...] += gval
        for h in range(NH):
            dstate[h] = dst[h]

        @pl.when(pl.program_id(0) == nt_ - 1)
        def _():
            ex.wait()

    full = lambda a: pl.BlockSpec(a.shape, lambda i, nd=a.ndim: (0,) * nd)
    rev = lambda i: (nt_ - 1 - i, 0)
    smalls = (w_s, b_s_t, ln_w, ln_b, lower, gn_w)
    outs = pl.pallas_call(
        body, name="bwd_mix", grid=(nt_,),
        out_shape=(jax.ShapeDtypeStruct((t, D_IN), BF), *[jax.ShapeDtypeStruct(a.shape, F32) for a in smalls],
                   *_exchange_out_shapes(to_scatter, True)),
        in_specs=[pl.BlockSpec((mt, D_IN), rev), pl.BlockSpec((mt, D), rev),
                  pl.BlockSpec((n_sub, NH, HD, HD), lambda i: (nt_ - 1 - i, 0, 0, 0))]
        + [full(a) for a in smalls] + [full(a) for a in tables] + [_any() for _ in to_scatter],
        out_specs=(pl.BlockSpec((mt, D_IN), rev), *[full(a) for a in smalls], *[_any() for _ in to_scatter]),
        scratch_shapes=[pltpu.VMEM((NH, HD, HD), F32)] + _exchange_sems(ns),
        compiler_params=_params(dimension_semantics=("arbitrary",), has_side_effects=True),
    )(proj, dycat, states, w_s, b_s_t, ln_w, ln_b, lower, gn_w, *tables, *to_scatter)
    return outs[0], outs[1:7], outs[7:]


def _row_tile(t, want):
    return want if t % want == 0 else t


def _rms(v):
    rstd = lax.rsqrt(jnp.mean(v * v, axis=-1, keepdims=True) + EPS)
    return v * rstd, rstd


def _rms_bwd(dxhat, xhat, rstd):
    return rstd * (dxhat - xhat * jnp.mean(dxhat * xhat, axis=-1, keepdims=True))


def _colsum(v):
    return jnp.sum(v, axis=0, keepdims=True)


def _fwd_in(x, ada, n1w, w_in_st, to_gather):
    t = x.shape[0]
    tm = _row_tile(t, 512)
    n_t = t // tm
    ng = len(to_gather)

    def body(*refs):
        x_ref, ada_ref, n1_ref, wst_ref = refs[:4]
        g_in = refs[4:4 + ng]
        proj_ref, h1_ref = refs[4 + ng:6 + ng]
        g_out = refs[6 + ng:6 + 2 * ng]
        w_ref, sems = refs[6 + 2 * ng:8 + 2 * ng]
        ga = _Gather2(g_in, g_out, *refs[8 + 2 * ng:])

        @pl.when(pl.program_id(0) == 0)
        def _():
            ga.start()
            _load_columns(wst_ref, w_ref, sems)

        @pl.when(pl.program_id(0) == (3 * n_t) // 4)
        def _():
            ga.forward()

        xh, _ = _rms(x_ref[...])
        h1 = (xh * n1_ref[...]) * (1.0 + ada_ref[1:2, :]) + ada_ref[0:1, :]
        h1b = h1.astype(BF)
        h1_ref[...] = h1b
        proj_ref[...] = _mm(h1b, w_ref[...])

        @pl.when(pl.program_id(0) == n_t - 1)
        def _():
            ga.finish()

    full = lambda a: pl.BlockSpec(a.shape, lambda i, nd=a.ndim: (0,) * nd)
    outs = pl.pallas_call(
        body, name="fwd_in", grid=(n_t,),
        out_shape=(jax.ShapeDtypeStruct((t, D_IN), F32), jax.ShapeDtypeStruct((t, D), BF),
                   *_exchange_out_shapes(to_gather, False)),
        in_specs=[pl.BlockSpec((tm, D), lambda i: (i, 0)), full(ada), full(n1w), _any()] + [_any() for _ in to_gather],
        out_specs=(pl.BlockSpec((tm, D_IN), lambda i: (i, 0)), pl.BlockSpec((tm, D), lambda i: (i, 0)),
                   *[_any() for _ in to_gather]),
        scratch_shapes=[pltpu.VMEM((D, D_IN), BF), pltpu.SemaphoreType.DMA((N_DEV,))] + _exchange_sems(ng),
        compiler_params=_params(dimension_semantics=("arbitrary",), has_side_effects=True),
    )(x, ada, n1w, w_in_st, *to_gather)
    return outs[0], outs[1], outs[2:]


def _fwd_ffn(x, ycat, tgt, ada, n2w, fw, w_out, w_fi, w_fo):
    t = x.shape[0]
    tm = _row_tile(t, 256)

    def body(x_ref, y_ref, t_ref, ada_ref, n2_ref, fw_ref, wo_ref, wi_ref, wf_ref,
             x1_ref, mix_ref, h2_ref, act_ref, gu_ref, dffn_ref, dx2_ref, part_ref):
        @pl.when(pl.program_id(0) == 0)
        def _():
            part_ref[...] = jnp.zeros_like(part_ref)

        g1, sh2, sc2, g2 = ada_ref[2:3, :], ada_ref[3:4, :], ada_ref[4:5, :], ada_ref[5:6, :]
        mix = _mm(y_ref[...], wo_ref[...])
        mix_ref[...] = mix.astype(BF)
        x1 = x_ref[...] + g1 * mix
        x1_ref[...] = x1
        xh2, _ = _rms(x1)
        h2b = ((xh2 * n2_ref[...]) * (1.0 + sc2) + sh2).astype(BF)
        h2_ref[...] = h2b
        ffn = jnp.zeros((tm, D), F32)
        for lo, hi in FF_CHUNKS:
            gate = _nt(h2b, wi_ref[lo:hi, :])
            up = _nt(h2b, wi_ref[D_FF + lo:D_FF + hi, :])
            gu_ref[:, lo:hi] = gate.astype(BF)
            gu_ref[:, D_FF + lo:D_FF + hi] = up.astype(BF)
            a = (_silu(gate) * up).astype(BF)
            act_ref[:, lo:hi] = a
            ffn = ffn + _mm(a, wf_ref[lo:hi, :])
        x2 = x1 + g2 * ffn
        xh3, rstd3 = _rms(x2)
        err = xh3 * fw_ref[...] - t_ref[...]
        dy = err * (1.0 / D)
        dx2 = _rms_bwd(dy * fw_ref[...], xh3, rstd3)
        dx2_ref[...] = dx2
        dffn_ref[...] = (g2 * dx2).astype(BF)
        part_ref[0:1, :] += _colsum(dx2 * ffn)
        part_ref[1:2, :] += _colsum(dy * xh3)
        part_ref[2:3, :] += jnp.zeros((1, D), F32) + (0.5 / D) * jnp.sum(err * err)

    full = lambda a: pl.BlockSpec(a.shape, lambda i, nd=a.ndim: (0,) * nd)
    row = lambda w: pl.BlockSpec((tm, w), lambda i: (i, 0))
    return pl.pallas_call(
        body, name="fwd_ffn", grid=(t // tm,),
        out_shape=(jax.ShapeDtypeStruct((t, D), F32), jax.ShapeDtypeStruct((t, D), BF), jax.ShapeDtypeStruct((t, D), BF),
                   jax.ShapeDtypeStruct((t, D_FF), BF), jax.ShapeDtypeStruct((t, 2 * D_FF), BF),
                   jax.ShapeDtypeStruct((t, D), BF), jax.ShapeDtypeStruct((t, D), F32), jax.ShapeDtypeStruct((8, D), F32)),
        in_specs=[row(D), row(D), row(D), full(ada), full(n2w), full(fw), _whole_vmem(), _whole_vmem(), _whole_vmem()],
        out_specs=(row(D), row(D), row(D), row(D_FF), row(2 * D_FF), row(D), row(D),
                   pl.BlockSpec((8, D), lambda i: (0, 0))),
        compiler_params=_params(dimension_semantics=("arbitrary",)),
    )(x, ycat, tgt, ada, n2w, fw, w_out, w_fi, w_fo)


def _bwd_ffn(x1, gu, dffn, dx2, mix, ada, n2w, w_out, w_fi, w_fo):
    t = x1.shape[0]
    tm = _row_tile(t, 256)

    def body(x1_ref, gu_ref, dffn_ref, dx2_ref, mix_ref, ada_ref, n2_ref, wo_ref, wi_ref, wf_ref,
             dgu_ref, dx1_ref, dmix_ref, dycat_ref, part_ref):
        @pl.when(pl.program_id(0) == 0)
        def _():
            part_ref[...] = jnp.zeros_like(part_ref)

        g1, sc2 = ada_ref[2:3, :], ada_ref[4:5, :]
        dffn = dffn_ref[...]
        dh2 = jnp.zeros((tm, D), F32)
        for lo, hi in FF_CHUNKS:
            gate = gu_ref[:, lo:hi].astype(F32)
            up = gu_ref[:, D_FF + lo:D_FF + hi].astype(F32)
            dact = _nt(dffn, wf_ref[lo:hi, :])
            sg = _sig(gate)
            dgate = (dact * up * (sg * (1.0 + gate * (1.0 - sg)))).astype(BF)
            dup = (dact * (gate * sg)).astype(BF)
            dgu_ref[:, lo:hi] = dgate
            dgu_ref[:, D_FF + lo:D_FF + hi] = dup
            dh2 = dh2 + _mm(dgate, wi_ref[lo:hi, :]) + _mm(dup, wi_ref[D_FF + lo:D_FF + hi, :])
        x1 = x1_ref[...]
        xh2, rstd2 = _rms(x1)
        xn2 = xh2 * n2_ref[...]
        dxn2 = dh2 * (1.0 + sc2)
        dx1 = dx2_ref[...] + _rms_bwd(dxn2 * n2_ref[...], xh2, rstd2)
        dx1_ref[...] = dx1
        dmix = (g1 * dx1).astype(BF)
        dmix_ref[...] = dmix
        dycat_ref[...] = _nt(dmix, wo_ref[...])
        part_ref[0:1, :] += _colsum(dh2)
        part_ref[1:2, :] += _colsum(dh2 * xn2)
        part_ref[2:3, :] += _colsum(dxn2 * xh2)
        part_ref[3:4, :] += _colsum(dx1 * mix_ref[...].astype(F32))

    full = lambda a: pl.BlockSpec(a.shape, lambda i, nd=a.ndim: (0,) * nd)
    row = lambda w: pl.BlockSpec((tm, w), lambda i: (i, 0))
    return pl.pallas_call(
        body, name="bwd_ffn", grid=(t // tm,),
        out_shape=(jax.ShapeDtypeStruct((t, 2 * D_FF), BF), jax.ShapeDtypeStruct((t, D), F32),
                   jax.ShapeDtypeStruct((t, D), BF), jax.ShapeDtypeStruct((t, D), F32), jax.ShapeDtypeStruct((8, D), F32)),
        in_specs=[row(D), row(2 * D_FF), row(D), row(D), row(D), full(ada), full(n2w),
                  _whole_vmem(), _whole_vmem(), _whole_vmem()],
        out_specs=(row(2 * D_FF), row(D), row(D), row(D), pl.BlockSpec((8, D), lambda i: (0, 0))),
        compiler_params=_params(dimension_semantics=("arbitrary",)),
    )(x1, gu, dffn, dx2, mix, ada, n2w, w_out, w_fi, w_fo)


def _bwd_in(x, dproj, dx1, ada, n1w, w_in_st, to_scatter, to_gather):
    t = x.shape[0]
    tm = _row_tile(t, 1024)
    n_t = t // tm
    ns = len(to_scatter)
    ng = len(to_gather)
    nx = ns + ng

    def body(*refs):
        x_ref, dp_ref, dx1_ref, ada_ref, n1_ref, wst_ref = refs[:6]
        x_in = refs[6:6 + nx]
        gx_ref, part_ref = refs[6 + nx:8 + nx]
        x_out = refs[8 + nx:8 + 2 * nx]
        w_ref, sems = refs[8 + 2 * nx:10 + 2 * nx]
        sem_refs = refs[10 + 2 * nx:]
        ex = _Exchange(x_in[:ns], x_out[:ns], *sem_refs[:3], True)
        gx = _Exchange(x_in[ns:], x_out[ns:], *sem_refs[3:], False) if ng else None

        @pl.when(pl.program_id(0) == 0)
        def _():
            ex.start()
            if ng:
                gx.start()
            part_ref[...] = jnp.zeros_like(part_ref)
            _load_columns(wst_ref, w_ref, sems)

        dh1 = _nt(dp_ref[...], w_ref[...])
        xh, rstd = _rms(x_ref[...])
        xn = xh * n1_ref[...]
        dxn = dh1 * (1.0 + ada_ref[1:2, :])
        gx_ref[...] = dx1_ref[...] + _rms_bwd(dxn * n1_ref[...], xh, rstd)
        part_ref[0:1, :] += _colsum(dh1)
        part_ref[1:2, :] += _colsum(dh1 * xn)
        part_ref[2:3, :] += _colsum(dxn * xh)

        @pl.when(pl.program_id(0) == n_t - 1)
        def _():
            ex.wait()
            if ng:
                gx.wait()

    full = lambda a: pl.BlockSpec(a.shape, lambda i, nd=a.ndim: (0,) * nd)
    row = lambda w: pl.BlockSpec((tm, w), lambda i: (i, 0))
    outs = pl.pallas_call(
        body, name="bwd_in", grid=(n_t,),
        out_shape=(jax.ShapeDtypeStruct((t, D), F32), jax.ShapeDtypeStruct((8, D), F32),
                   *_exchange_out_shapes(to_scatter, True), *_exchange_out_shapes(to_gather, False)),
        in_specs=[row(D), row(D_IN), row(D), full(ada), full(n1w), _any()] + [_any() for _ in range(nx)],
        out_specs=(row(D), pl.BlockSpec((8, D), lambda i: (0, 0)), *[_any() for _ in range(nx)]),
        scratch_shapes=[pltpu.VMEM((D, D_IN), BF), pltpu.SemaphoreType.DMA((N_DEV,))] + _exchange_sems(ns)
        + (_exchange_sems(ng) if ng else []),
        compiler_params=_params(dimension_semantics=("arbitrary",), has_side_effects=True),
    )(x, dproj, dx1, ada, n1w, w_in_st, *to_scatter, *to_gather)
    return outs[0], outs[1], outs[2:2 + ns], outs[2 + ns:]


def _wgrad(a, b, name, a_spec, b_spec, out_shape, out_spec, grid, acc_shape, split=1, to_gather=()):
    n_j, n_t = grid
    ng = len(to_gather)

    def body(*refs):
        a_ref, b_ref = refs[:2]
        g_in = refs[2:2 + ng]
        o_ref = refs[2 + ng]
        g_out = refs[3 + ng:3 + 2 * ng]
        acc = refs[3 + 2 * ng]
        first = (pl.program_id(0) == 0) & (pl.program_id(1) == 0)
        last = (pl.program_id(0) == n_j - 1) & (pl.program_id(1) == n_t - 1)
        if ng:
            gx = _Exchange(g_in, g_out, *refs[4 + 2 * ng:], False)

            @pl.when(first)
            def _():
                gx.start()

        @pl.when(pl.program_id(1) == 0)
        def _():
            acc[...] = jnp.zeros_like(acc)

        acc[...] += _tn(a_ref[...], b_ref[...])

        @pl.when(pl.program_id(1) == n_t - 1)
        def _():
            if split == 1:
                o_ref[...] = acc[...].astype(BF)
            else:
                w = acc_shape[1] // split
                for s in range(split):
                    o_ref[s] = acc[:, s * w:(s + 1) * w].astype(BF)

        if ng:
            @pl.when(last)
            def _():
                gx.wait()

    outs = pl.pallas_call(
        body, name=name, grid=grid,
        out_shape=(jax.ShapeDtypeStruct(out_shape, BF), *_exchange_out_shapes(to_gather, False)),
        in_specs=[a_spec, b_spec] + [_any() for _ in to_gather], out_specs=(out_spec, *[_any() for _ in to_gather]),
        scratch_shapes=[pltpu.VMEM(acc_shape, F32)] + (_exchange_sems(ng) if ng else []),
        compiler_params=_params(dimension_semantics=("arbitrary", "arbitrary"), has_side_effects=bool(ng)),
    )(a, b, *to_gather)
    return (outs[0], outs[1:]) if ng else outs[0]


def _adamw_math(w, g, m, v):
    m = ADAM_B1 * m + (1.0 - ADAM_B1) * g
    v = ADAM_B2 * v + (1.0 - ADAM_B2) * (g * g)
    m_hat = m / (1.0 - ADAM_B1 ** ADAM_STEP)
    v_hat = v / (1.0 - ADAM_B2 ** ADAM_STEP)
    delta = -ADAM_LR * (m_hat / (jnp.sqrt(v_hat) + ADAM_EPS) + ADAM_WD * w)
    return delta, m, v


def _adamw_recv(w, m, v, recv, name, tr):
    r, c = w.shape

    def body(w_ref, m_ref, v_ref, r_ref, g_ref, d_ref, nm_ref, nv_ref):
        g = r_ref[0].astype(F32)
        for k in range(1, N_DEV):
            g = g + r_ref[k].astype(F32)
        g_ref[...] = g
        d_ref[...], nm_ref[...], nv_ref[...] = _adamw_math(w_ref[...], g, m_ref[...], v_ref[...])

    row = pl.BlockSpec((tr, c), lambda i: (i, 0))
    return pl.pallas_call(
        body, name=name, grid=(r // tr,),
        out_shape=tuple(jax.ShapeDtypeStruct((r, c), F32) for _ in range(4)),
        in_specs=[row, row, row, pl.BlockSpec((N_DEV, tr, c), lambda i: (0, i, 0))],
        out_specs=(row, row, row, row),
        compiler_params=_params(dimension_semantics=("arbitrary",)),
    )(w, m, v, recv)


def _adamw_ada(w, m, v, cact, dada_cols):
    r, c = w.shape
    tr = 256

    def body(w_ref, m_ref, v_ref, ca_ref, da_ref, g_ref, d_ref, nm_ref, nv_ref):
        g = _tn(ca_ref[...].astype(BF), da_ref[...].astype(BF))
        g_ref[...] = g
        d_ref[...], nm_ref[...], nv_ref[...] = _adamw_math(w_ref[...], g, m_ref[...], v_ref[...])

    row = pl.BlockSpec((tr, c), lambda i: (i, 0))
    return pl.pallas_call(
        body, name="adamw_ada", grid=(r // tr,),
        out_shape=tuple(jax.ShapeDtypeStruct((r, c), F32) for _ in range(4)),
        in_specs=[row, row, row, pl.BlockSpec((N_DEV, tr), lambda i: (0, i)), pl.BlockSpec(dada_cols.shape, lambda i: (0, 0))],
        out_specs=(row, row, row, row),
        compiler_params=_params(dimension_semantics=("arbitrary",)),
    )(w, m, v, cact, dada_cols)


def _adamw_small(gathered, wmv):
    n_g = len(gathered)
    n_p = len(wmv)
    flat = [a for trip in wmv for a in trip]

    def body(*refs):
        g_refs = refs[:n_g]
        p_refs = refs[n_g:n_g + 3 * n_p]
        o_refs = refs[n_g + 3 * n_p:]

        def total(ref):
            s = ref[0]
            for k in range(1, N_DEV):
                s = s + ref[k]
            return s

        f3, b3, b1, dws, dbs, dlnw, dlnb, dlo, dgn = [total(r) for r in g_refs]
        dada_rows = [b1[0:1], b1[1:2], b3[3:4], b3[0:1], b3[1:2], f3[0:1]]
        for r, g in enumerate(dada_rows):
            cs = slice(r * D, (r + 1) * D)
            w, m, v = p_refs[0][:, cs], p_refs[1][:, cs], p_refs[2][:, cs]
            o_refs[0][:, cs] = g
            o_refs[1][:, cs], o_refs[2][:, cs], o_refs[3][:, cs] = _adamw_math(w, g, m, v)
        grads = [None, b1[2:3], dws, dbs, dlnw, dlnb, dlo, dgn, b3[2:3], f3[1:2]]
        for i, g in enumerate(grads):
            if g is None:
                continue
            w, m, v = p_refs[3 * i][...], p_refs[3 * i + 1][...], p_refs[3 * i + 2][...]
            o_refs[4 * i][...] = g
            o_refs[4 * i + 1][...], o_refs[4 * i + 2][...], o_refs[4 * i + 3][...] = _adamw_math(w, g, m, v)
        o_refs[4 * n_p][...] = jnp.zeros((8, 128), F32) + f3[2:3, 0:128]

    out_shape = []
    for w, _, _ in wmv:
        out_shape += [jax.ShapeDtypeStruct(w.shape, F32)] * 4
    out_shape.append(jax.ShapeDtypeStruct((8, 128), F32))
    n_in = n_g + 3 * n_p
    return pl.pallas_call(
        body, name="adamw_small",
        out_shape=tuple(out_shape),
        in_specs=[_whole_vmem()] * n_in, out_specs=tuple(_whole_vmem() for _ in out_shape),
        compiler_params=_params(),
    )(*gathered, *flat)


def kernel(x, c, w_ada, b_ada, norm1_w, w_in, w_s, b_s, v_ln_w, v_ln_b, lower_bounds, gn_w, w_out, norm2_w, w_ffn_in, w_ffn_out, final_norm_w, loss_target, m_w_ada, m_b_ada, m_norm1_w, m_w_in, m_w_s, m_b_s, m_v_ln_w, m_v_ln_b, m_lower_bounds, m_gn_w, m_w_out, m_norm2_w, m_w_ffn_in, m_w_ffn_out, m_final_norm_w, v_w_ada, v_b_ada, v_norm1_w, v_w_in, v_w_s, v_b_s, v_v_ln_w, v_v_ln_b, v_lower_bounds, v_gn_w, v_w_out, v_norm2_w, v_w_ffn_in, v_w_ffn_out, v_final_norm_w):
    me = 4 * lax.axis_index("x") + 2 * lax.axis_index("y") + lax.axis_index("c")
    t = x.shape[1]
    x2d = x.reshape(t, D)
    tgt = loss_target.reshape(t, D)
    ada_cols = w_ada.shape[2]

    tp = lambda a: jnp.swapaxes(a[0], 0, 1)
    win_b, wout_b, wfi_b, wfo_b = _cast_bf16([w_in[0], w_out[0], tp(w_ffn_in), w_ffn_out[0]], "cast_weights")

    b_cols = lax.dynamic_slice(b_ada, (0, me * ada_cols), (1, ada_cols))
    win_st, ada_st, cact = _startup(win_b, c, w_ada[0], b_cols)
    ada = lax.dynamic_index_in_dim(ada_st, me, axis=1, keepdims=False).reshape(6, D)

    tables = _decay_tables()
    ws3 = w_s[0]
    bs_t = b_s[0].T

    proj, h1, (wout_st, wfo_st) = _fwd_in(x2d, ada, norm1_w, win_st, [wout_b, wfo_b])
    ycat, states, (wfi_st,) = _fwd_mix(proj, ws3, bs_t, v_ln_w, v_ln_b, lower_bounds, gn_w, tables, [wfi_b])
    w_out_full = wout_st.reshape(D, D)
    w_fo_full = wfo_st.reshape(D_FF, D)
    w_fi_full = wfi_st.reshape(2 * D_FF, D)
    x1, mixb, h2, act, gu, dffn, dx2, part_f = _fwd_ffn(x2d, ycat, tgt, ada, norm2_w, final_norm_w.reshape(1, D),
                                                        w_out_full, w_fi_full, w_fo_full)
    dgu, dx1, dmix, dycat, part_b3 = _bwd_ffn(x1, gu, dffn, dx2, mixb, ada, norm2_w, w_out_full, w_fi_full, w_fo_full)
    tk = _row_tile(t, 2048)
    n_t = t // tk
    win_cols = D_IN // N_DEV
    dwout = _wgrad(ycat, dmix, "wgrad_out",
                   pl.BlockSpec((tk, D), lambda j, i: (i, 0)), pl.BlockSpec((tk, D), lambda j, i: (i, 0)),
                   (D, D), pl.BlockSpec((D, D), lambda j, i: (0, 0)), (1, n_t), (D, D))
    dwfi = _wgrad(dgu, h2, "wgrad_ffn_in",
                  pl.BlockSpec((tk, FF_PAIR), lambda j, i: (i, j)), pl.BlockSpec((tk, D), lambda j, i: (i, 0)),
                  (4, FF_PAIR, D), pl.BlockSpec((None, FF_PAIR, D), lambda j, i: (j, 0, 0)), (4, n_t), (FF_PAIR, D))
    dwfo = _wgrad(act, dffn, "wgrad_ffn_out",
                  pl.BlockSpec((tk, FF_PAIR), lambda j, i: (i, j)), pl.BlockSpec((tk, D), lambda j, i: (i, 0)),
                  (2, FF_PAIR, D), pl.BlockSpec((None, FF_PAIR, D), lambda j, i: (j, 0, 0)), (2, n_t), (FF_PAIR, D))
    dproj, (dws, dbs_t, dlnw, dlnb, dlower, dgnw), (r_out, r_fi, r_fo) = _bwd_mix(
        proj, dycat, states, ws3, bs_t, v_ln_w, v_ln_b, lower_bounds, gn_w, tables,
        [dwout.reshape(N_DEV, D // N_DEV, D), dwfi.reshape(N_DEV, FF_BLK, D), dwfo.reshape(N_DEV, D_FF // N_DEV, D)])
    dwin, early = _wgrad(h1, dproj, "wgrad_in",
                         pl.BlockSpec((tk, D), lambda j, i: (i, 0)), pl.BlockSpec((tk, 4 * win_cols), lambda j, i: (i, j)),
                         (N_DEV, D, win_cols), pl.BlockSpec((4, D, win_cols), lambda j, i: (j, 0, 0)), (N_DEV // 4, n_t),
                         (D, 4 * win_cols), split=4, to_gather=[part_f, part_b3, dws, dbs_t, dlnw, dlnb, dlower, dgnw])
    grad_x, part_b1, (r_in,), _ = _bwd_in(x2d, dproj, dx1, ada, norm1_w, win_st, [dwin], [])

    g_w_in, d_w_in, nm_w_in, nv_w_in = _adamw_recv(w_in[0], m_w_in[0], v_w_in[0], r_in, "adamw_w_in", 256)
    g_w_out, d_w_out, nm_w_out, nv_w_out = _adamw_recv(w_out[0], m_w_out[0], v_w_out[0], r_out, "adamw_w_out", 128)
    fi_t = _adamw_recv(tp(w_ffn_in), tp(m_w_ffn_in), tp(v_w_ffn_in), r_fi, "adamw_w_ffn_in", 176)
    g_w_fi, d_w_fi, nm_w_fi, nv_w_fi = [jnp.swapaxes(a, 0, 1) for a in fi_t]
    g_w_fo, d_w_fo, nm_w_fo, nv_w_fo = _adamw_recv(w_ffn_out[0], m_w_ffn_out[0], v_w_ffn_out[0], r_fo, "adamw_w_ffn_out", 176)

    (b1_all,) = _exchange([part_b1], "gather_small", False, True)
    gathered = [early[0], early[1], b1_all, *early[2:]]
    f3_all, b3_all = gathered[0], gathered[1]
    dada_all = jnp.stack([b1_all[:, 0], b1_all[:, 1], b3_all[:, 3], b3_all[:, 0], b3_all[:, 1], f3_all[:, 0]], axis=1)
    dada_cols = lax.dynamic_slice(dada_all.reshape(N_DEV, 6 * D), (0, me * ada_cols), (N_DEV, ada_cols))
    g_w_ada, d_w_ada, nm_w_ada, nv_w_ada = _adamw_ada(w_ada[0], m_w_ada[0], v_w_ada[0], cact, dada_cols)

    r1 = lambda a: a.reshape(1, D)
    tr = lambda a: a[0].T
    wmv = [
        (b_ada, m_b_ada, v_b_ada),
        (norm1_w, m_norm1_w, v_norm1_w),
        (w_s[0], m_w_s[0], v_w_s[0]),
        (tr(b_s), tr(m_b_s), tr(v_b_s)),
        (v_ln_w, m_v_ln_w, v_v_ln_w),
        (v_ln_b, m_v_ln_b, v_v_ln_b),
        (lower_bounds, m_lower_bounds, v_lower_bounds),
        (gn_w, m_gn_w, v_gn_w),
        (norm2_w, m_norm2_w, v_norm2_w),
        (r1(final_norm_w), r1(m_final_norm_w), r1(v_final_norm_w)),
    ]
    small = _adamw_small(gathered, wmv)
    loss = small[-1][0, 0]

    def unshape(i, a):
        if i == 2:
            return a.reshape(1, NH, BLK, BLK)
        if i == 3:
            return a.T.reshape(1, NH, BLK)
        if i == 9:
            return a.reshape(D)
        return a

    def small_out(kind):
        return [unshape(i, small[4 * i + kind]) for i in range(len(wmv))]

    e3 = lambda a: a[None]
    big = {
        0: (e3(g_w_ada), e3(g_w_in), e3(g_w_out), e3(g_w_fi), e3(g_w_fo)),
        1: (e3(d_w_ada), e3(d_w_in), e3(d_w_out), e3(d_w_fi), e3(d_w_fo)),
        2: (e3(nm_w_ada), e3(nm_w_in), e3(nm_w_out), e3(nm_w_fi), e3(nm_w_fo)),
        3: (e3(nv_w_ada), e3(nv_w_in), e3(nv_w_out), e3(nv_w_fi), e3(nv_w_fo)),
    }

    def ordered(kind):
        s = small_out(kind)
        b_ = big[kind]
        return [b_[0], s[0], s[1], b_[1], s[2], s[3], s[4], s[5], s[6], s[7], b_[2], s[8], b_[3], b_[4], s[9]]

    return (loss, grad_x.reshape(1, t, D), *ordered(0), *ordered(1), *ordered(2), *ordered(3))
```

```python
import numpy as np
import jax
import jax.numpy as jnp
from jax import lax
from jax.experimental import pallas as pl
from jax.experimental.pallas import tpu as pltpu

F32 = jnp.float32
BF = jnp.bfloat16
MESH = pl.DeviceIdType.MESH

N_DEV = 8
D = 1024
D_IN = 3072
D_FF = 2816
FF_BLK = D_FF // 4
FF_CHUNKS = ((0, D_FF),)
FF_PAIR = 2 * FF_BLK
CH = 64
BLK = 128
MIX_TILE = 256
FWD_MIX_SUBTILES = 4
BWD_MIX_SUBTILES = 2
NH = 4
HD = 128
EPS = 1e-6
LEVELS = (32, 16, 8, 4, 2, 1)

ADAM_LR = 0.001
ADAM_B1 = 0.9
ADAM_B2 = 0.999
ADAM_EPS = 1e-08
ADAM_WD = 0.01
ADAM_STEP = 10

VMEM_LIMIT = 56 * 1024 * 1024


def _params(**kw):
    return pltpu.CompilerParams(vmem_limit_bytes=VMEM_LIMIT, **kw)


def _whole_vmem():
    return pl.BlockSpec(memory_space=pltpu.VMEM)


def _any():
    return pl.BlockSpec(memory_space=pl.ANY)


class _Exchange:
    def __init__(self, ins, outs, send_sems, recv_sems, local_sems, scatter):
        self.ins, self.outs, self.scatter = ins, outs, scatter
        self.send_sems, self.recv_sems, self.local_sems = send_sems, recv_sems, local_sems
        x, y, c = lax.axis_index("x"), lax.axis_index("y"), lax.axis_index("c")
        self.me = 4 * x + 2 * y + c
        self.peers = []
        for k in range(1, N_DEV):
            peer = (1 - x if (k >> 2) & 1 else x, 1 - y if (k >> 1) & 1 else y, 1 - c if k & 1 else c)
            self.peers.append((peer, 4 * peer[0] + 2 * peer[1] + peer[2]))

    def _src(self, a, idx):
        return self.ins[a].at[idx] if self.scatter else self.ins[a]

    def _local(self):
        return [pltpu.make_async_copy(self._src(a, self.me), self.outs[a].at[self.me], self.local_sems.at[a])
                for a in range(len(self.ins))]

    def _remote(self, a, k, dst_slot):
        peer, peer_idx = self.peers[k]
        return pltpu.make_async_remote_copy(
            src_ref=self._src(a, peer_idx), dst_ref=self.outs[a].at[dst_slot],
            send_sem=self.send_sems.at[a, k], recv_sem=self.recv_sems.at[a, k],
            device_id=peer, device_id_type=MESH)

    def start(self):
        for cp in self._local():
            cp.start()
        for k in range(N_DEV - 1):
            for a in range(len(self.ins)):
                self._remote(a, k, self.me).start()

    def wait(self):
        for k in range(N_DEV - 1):
            for a in range(len(self.ins)):
                self._remote(a, k, self.peers[k][1]).wait()
        for cp in self._local():
            cp.wait()


class _Gather2:
    def __init__(self, ins, outs, send_sems, recv_sems, local_sems):
        self.ins, self.outs = ins, outs
        self.send_sems, self.recv_sems, self.local_sems = send_sems, recv_sems, local_sems
        x, y, c = lax.axis_index("x"), lax.axis_index("y"), lax.axis_index("c")
        self.c = c
        self.me = 4 * x + 2 * y + c
        self.sibling = (x, y, 1 - c)
        self.chips = [(1 - x, y), (x, 1 - y), (1 - x, 1 - y)]

    @staticmethod
    def _idx(px, py, pc):
        return 4 * px + 2 * py + pc

    def _copy(self, a, k, slot, to, own):
        src = self.ins[a] if own else self.outs[a].at[slot]
        return pltpu.make_async_remote_copy(
            src_ref=src, dst_ref=self.outs[a].at[slot],
            send_sem=self.send_sems.at[a, k], recv_sem=self.recv_sems.at[a, k],
            device_id=to, device_id_type=MESH)

    def _local(self):
        return [pltpu.make_async_copy(self.ins[a], self.outs[a].at[self.me], self.local_sems.at[a])
                for a in range(len(self.ins))]

    def start(self):
        for cp in self._local():
            cp.start()
        for a in range(len(self.ins)):
            self._copy(a, 0, self.me, self.sibling, True).start()
            for j, chip in enumerate(self.chips):
                self._copy(a, 1 + j, self.me, (*chip, self.c), True).start()

    def forward(self):
        for j, chip in enumerate(self.chips):
            for a in range(len(self.ins)):
                slot = self._idx(*chip, self.c)
                self._copy(a, 1 + j, slot, (*chip, self.c), True).wait_recv()
                self._copy(a, 4 + j, slot, self.sibling, False).start()

    def finish(self):
        for a in range(len(self.ins)):
            self._copy(a, 0, self._idx(*self.sibling), self.sibling, True).wait_recv()
            for j, chip in enumerate(self.chips):
                self._copy(a, 4 + j, self._idx(*chip, 1 - self.c), self.sibling, False).wait_recv()
            self._copy(a, 0, self.me, self.sibling, True).wait_send()
            for j, chip in enumerate(self.chips):
                self._copy(a, 1 + j, self.me, (*chip, self.c), True).wait_send()
                self._copy(a, 4 + j, self._idx(*chip, self.c), self.sibling, False).wait_send()
        for cp in self._local():
            cp.wait()


def _exchange_sems(n):
    return [pltpu.SemaphoreType.DMA((n, N_DEV - 1)), pltpu.SemaphoreType.DMA((n, N_DEV - 1)), pltpu.SemaphoreType.DMA((n,))]


def _exchange_out_shapes(arrays, scatter):
    return [jax.ShapeDtypeStruct(a.shape if scatter else (N_DEV,) + a.shape, a.dtype) for a in arrays]


def _exchange(arrays, name, scatter, in_vmem):
    n = len(arrays)

    def body(*refs):
        ex = _Exchange(refs[:n], refs[n:2 * n], *refs[2 * n:], scatter)
        ex.start()
        ex.wait()

    spec = _whole_vmem if in_vmem else _any
    return pl.pallas_call(
        body, name=name,
        out_shape=tuple(_exchange_out_shapes(arrays, scatter)),
        in_specs=[spec() for _ in arrays],
        out_specs=tuple(spec() for _ in arrays),
        scratch_shapes=_exchange_sems(n),
        compiler_params=_params(has_side_effects=True),
    )(*arrays)


def _cast_bf16(arrays, name):
    n = len(arrays)

    def body(*refs):
        for i in range(n):
            refs[n + i][...] = refs[i][...].astype(BF)

    return pl.pallas_call(
        body, name=name,
        out_shape=tuple(jax.ShapeDtypeStruct(a.shape, BF) for a in arrays),
        in_specs=[_whole_vmem() for _ in arrays],
        out_specs=tuple(_whole_vmem() for _ in arrays),
        compiler_params=_params(),
    )(*arrays)


def _load_columns(stacked_ref, full_ref, sems):
    c = stacked_ref.shape[2]
    cps = [pltpu.make_async_copy(stacked_ref.at[d], full_ref.at[:, pl.ds(d * c, c)], sems.at[d]) for d in range(N_DEV)]
    for cp in cps:
        cp.start()
    for cp in cps:
        cp.wait()


def _sig(v):
    return 0.5 * jnp.tanh(0.5 * v) + 0.5


@jax.custom_vjp
def _silu(v):
    return v * _sig(v)


def _silu_fwd(v):
    return _silu(v), v


def _silu_bwd(v, g):
    s = _sig(v)
    return (g * (s * (1.0 + v * (1.0 - s))),)


_silu.defvjp(_silu_fwd, _silu_bwd)


@jax.custom_vjp
def _sigmoid_rel(v):
    e = jnp.exp(-jnp.abs(v))
    d = 1.0 + e
    r = pl.reciprocal(d, approx=True)
    r = r * (2.0 - d * r)
    r = r * (2.0 - d * r)
    return jnp.where(v >= 0.0, r, e * r)


def _sigmoid_rel_fwd(v):
    s = _sigmoid_rel(v)
    return s, s


def _sigmoid_rel_bwd(s, g):
    return (g * (s * (1.0 - s)),)


_sigmoid_rel.defvjp(_sigmoid_rel_fwd, _sigmoid_rel_bwd)


def _dot(a, b, ca, cb):
    return lax.dot_general(a, b, (((ca,), (cb,)), ((), ())), preferred_element_type=F32)


@jax.custom_vjp
def _mm(a, b):
    return _dot(a, b, 1, 0)


def _mm_fwd(a, b):
    return _dot(a, b, 1, 0), (a, b)


def _mm_bwd(res, g):
    a, b = res
    gb = g.astype(BF)
    return _dot(gb, b, 1, 1).astype(a.dtype), _dot(a, gb, 0, 0).astype(b.dtype)


_mm.defvjp(_mm_fwd, _mm_bwd)


@jax.custom_vjp
def _nt(a, b):
    return _dot(a, b, 1, 1)


def _nt_fwd(a, b):
    return _dot(a, b, 1, 1), (a, b)


def _nt_bwd(res, g):
    a, b = res
    gb = g.astype(BF)
    return _dot(gb, b, 1, 0).astype(a.dtype), _dot(gb, a, 0, 0).astype(b.dtype)


_nt.defvjp(_nt_fwd, _nt_bwd)


@jax.custom_vjp
def _tn(a, b):
    return _dot(a, b, 0, 0)


def _tn_fwd(a, b):
    return _dot(a, b, 0, 0), (a, b)


def _tn_bwd(res, g):
    a, b = res
    gb = g.astype(BF)
    return _dot(b, gb, 1, 1).astype(a.dtype), _dot(a, gb, 1, 0).astype(b.dtype)


_tn.defvjp(_tn_fwd, _tn_bwd)


def _make_cast_dot(ca, cb, da_dims, db_dims):
    @jax.custom_vjp
    def dot(a, b):
        return _dot(a.astype(BF), b.astype(BF), ca, cb)

    def fwd(a, b):
        ab, bb = a.astype(BF), b.astype(BF)
        return _dot(ab, bb, ca, cb), (ab, bb)

    def bwd(res, g):
        ops = {"a": res[0], "b": res[1], "g": g.astype(BF)}
        return (_dot(ops[da_dims[0]], ops[da_dims[1]], da_dims[2], da_dims[3]),
                _dot(ops[db_dims[0]], ops[db_dims[1]], db_dims[2], db_dims[3]))

    dot.defvjp(fwd, bwd)
    return dot


_mm_c = _make_cast_dot(1, 0, ("g", "b", 1, 1), ("a", "g", 0, 0))
_nt_c = _make_cast_dot(1, 1, ("g", "b", 1, 0), ("g", "a", 0, 0))
_tn_c = _make_cast_dot(0, 0, ("b", "g", 1, 1), ("a", "g", 1, 0))


def _startup(win_b, c, w_ada, b_cols):
    ncol = w_ada.shape[1]

    def body(win_ref, c_ref, w_ref, b_ref, winst_ref, adast_ref, cact_ref, call_ref, blk_ref,
             ws, wr, wl, cs, cr, cl, as_, ar, al):
        big = _Gather2([win_ref], [winst_ref], ws, wr, wl)
        big.start()
        gc = _Exchange([c_ref], [call_ref], cs, cr, cl, False)
        gc.start()
        gc.wait()
        ca = _silu(call_ref[...].reshape(N_DEV, D))
        cact_ref[...] = ca
        blk_ref[...] = _mm(ca.astype(BF), w_ref[...].astype(BF)) + b_ref[...]
        ga = _Exchange([blk_ref], [adast_ref], as_, ar, al, False)
        ga.start()
        ga.wait()
        big.forward()
        big.finish()

    return pl.pallas_call(
        body, name="startup",
        out_shape=(jax.ShapeDtypeStruct((N_DEV,) + win_b.shape, BF), jax.ShapeDtypeStruct((N_DEV, N_DEV, ncol), F32),
                   jax.ShapeDtypeStruct((N_DEV, D), F32)),
        in_specs=[_any(), _whole_vmem(), _whole_vmem(), _whole_vmem()],
        out_specs=(_any(), _whole_vmem(), _whole_vmem()),
        scratch_shapes=[pltpu.VMEM((N_DEV, 1, D), F32), pltpu.VMEM((N_DEV, ncol), F32)]
        + _exchange_sems(1) + _exchange_sems(1) + _exchange_sems(1),
        compiler_params=_params(has_side_effects=True),
    )(win_b, c, w_ada, b_cols)


def _decay_tables():
    t = np.arange(CH)
    tri = (t[None, :] <= t[:, None]).astype(np.float32)
    masks = []
    for h in LEVELS:
        m = (t // (2 * h)) * (2 * h) + h
        upper = t >= m
        same = (t[:, None] // (2 * h)) == (t[None, :] // (2 * h))
        masks.append(same & upper[:, None] & (~upper)[None, :])
    masks.append(np.eye(CH, dtype=bool))
    lv = [np.where((t % (2 * h)) >= h, 1.0, -1.0) for h in LEVELS[:4]]
    m4 = t % 4
    lv += [(m4 == 0) * 1.0, (m4 >= 2) * 1.0, (m4 == 3) * 1.0, (t % 2 == 1) * 1.0]
    lvl = np.broadcast_to(np.stack(lv)[:, :, None], (8, CH, D // 2)).astype(np.float32)
    cid = np.arange(BLK) // CH
    gmask = (cid[:, None] >= cid[None, :]).astype(np.float32)
    masks = np.stack(masks).astype(np.float32)
    stacked = np.zeros((masks.shape[0], NH * CH, NH * CH), np.float32)
    for h in range(NH):
        stacked[:, h * CH:(h + 1) * CH, h * CH:(h + 1) * CH] = masks
    return (jnp.asarray(tri, BF), jnp.asarray(stacked), jnp.asarray(gmask), jnp.asarray(lvl))


def _split2(v):
    v1 = v.astype(BF)
    return v1, (v - v1.astype(F32)).astype(BF)


@jax.custom_vjp
def _cumsum_mm(tri, v):
    p1, p2 = _split2(v)
    return _dot(tri, p1, 1, 0) + _dot(tri, p2, 1, 0)


def _cumsum_mm_fwd(tri, v):
    return _cumsum_mm(tri, v), tri


def _cumsum_mm_bwd(tri, g):
    p1, p2 = _split2(g)
    return jnp.zeros_like(tri), _dot(tri, p1, 0, 0) + _dot(tri, p2, 0, 0)


_cumsum_mm.defvjp(_cumsum_mm_fwd, _cumsum_mm_bwd)


def _make_row_roll(shift):
    @jax.custom_vjp
    def roll(x):
        return pltpu.roll(x, shift % CH, 0)

    def fwd(x):
        return roll(x), None

    def bwd(_, g):
        return (pltpu.roll(g, (-shift) % CH, 0),)

    roll.defvjp(fwd, bwd)
    return roll


_prev_row = _make_row_roll(1)
_next_row = _make_row_roll(-1)


def _mix_tile(guv, ph, state, w_s, b_s_t, ln_w, ln_b, lower, gn_w, consts):
    tri, masks, gmask, lvl = consts
    mt = guv.shape[0]
    ug = guv[:, 0:512]
    vg = guv[:, 512:1024]
    q = ph[:, 0:512]
    fl = ph[:, 512:1024]
    inp = ph[:, 1024:1536]
    g = ph[:, 1536:2048]

    mu = jnp.mean(vg, axis=-1, keepdims=True)
    vc = vg - mu
    var = jnp.mean(vc * vc, axis=-1, keepdims=True)
    vn = vc * lax.rsqrt(var + EPS) * ln_w + ln_b
    wsm = [w_s[h] * gmask for h in range(NH)]
    ya = [[None] * NH for _ in range(mt // BLK)]
    for bi in range(mt // BLK):
        rows = slice(bi * BLK, (bi + 1) * BLK)
        for h in range(NH):
            cols = slice(h * HD, (h + 1) * HD)
            ya[bi][h] = ug[rows, cols] * (_mm_c(wsm[h], vn[rows, cols]) + b_s_t[:, h:h + 1])
    ya_full = jnp.concatenate([jnp.concatenate(r, axis=1) for r in ya], axis=0)

    l0 = lower[0:1, :]
    l1 = lower[1:2, :]
    mx = jnp.maximum(l0, l1)
    e0 = jnp.exp(l0 - mx)
    e1 = jnp.exp(l1 - mx)
    lb = e0 / (e0 + e1)
    qf = _silu(q)
    f = lb + (1.0 - lb) * _sigmoid_rel(fl)
    logf = jnp.log(f)
    kk = 1.0 - f
    gate = _silu(g)
    nl = len(LEVELS)
    half = D // 2
    heads_to_rows = lambda a: jnp.concatenate([a[:, h * HD:(h + 1) * HD] for h in range(NH)], axis=0)
    st = list(state)
    yb = [[None] * NH for _ in range(mt // CH)]
    for ci in range(mt // CH):
        rows = slice(ci * CH, (ci + 1) * CH)
        lc = logf[rows]
        b = _cumsum_mm(tri, lc)
        xb = jnp.exp(b)
        xinv = jnp.exp(b[CH - 1:CH, :] - b)
        xl = []
        for i, hs in enumerate(LEVELS[:3]):
            refs = [jnp.broadcast_to(b[r:r + 1, :], (2 * hs, half)) for r in range(hs - 1, CH, 2 * hs)]
            bref = refs[0] if len(refs) == 1 else jnp.concatenate(refs, axis=0)
            xl.append(jnp.exp(lvl[i] * (b - bref)))
        b3 = b.reshape(CH // 8, 8, half)
        bref = jnp.broadcast_to(b3[:, 3:4, :], (CH // 8, 8, half)).reshape(CH, half)
        xl.append(jnp.exp(lvl[3] * (b - bref)))
        xl.append(jnp.exp(lvl[4] * _next_row(lc) + lvl[5] * lc + lvl[6] * _prev_row(lc)))
        xl.append(jnp.exp(lvl[7] * lc))
        qc = qf[rows]
        kc = kk[rows]
        ic = inp[rows]
        zsrc = [jnp.concatenate([(qc if (r0 // hs) % 2 == 1 else kc)[r0:r0 + hs] for r0 in range(0, CH, hs)], axis=0)
                for hs in LEVELS[:3]]
        qs, ks = heads_to_rows(qc), heads_to_rows(kc)
        qx = qc * xb
        inter = jnp.concatenate([_nt_c(qx[:, h * HD:(h + 1) * HD], st[h]) for h in range(NH)], axis=0)
        attn = masks[nl] * _nt_c(qs, ks)
        for li in range(nl):
            xs = heads_to_rows(xl[li])
            if li < 3:
                z = heads_to_rows(zsrc[li]) * xs
                pairs = _nt_c(z, z)
            else:
                pairs = _nt_c(qs * xs, ks * xs)
            attn = attn + masks[li] * pairs
        o = inter + _mm_c(attn, heads_to_rows(ic))
        kx = kc * xinv
        decay = xb[CH - 1:CH, :]
        st = [st[h] * decay[:, h * HD:(h + 1) * HD] + _tn_c(ic[:, h * HD:(h + 1) * HD], kx[:, h * HD:(h + 1) * HD])
              for h in range(NH)]
        rs = lax.rsqrt(jnp.mean(o * o, axis=-1, keepdims=True) + EPS)
        ys = o * rs * gn_w * heads_to_rows(gate[rows])
        yb[ci] = [ys[h * CH:(h + 1) * CH] for h in range(NH)]
    yb_full = jnp.concatenate([jnp.concatenate(r, axis=1) for r in yb], axis=0)
    return jnp.concatenate([ya_full, yb_full], axis=1), tuple(st)


def _fwd_mix(gelu_uv, proj_h, w_s, b_s_t, ln_w, ln_b, lower, gn_w, tables, to_gather):
    t = proj_h.shape[0]
    sub = _row_tile(t, MIX_TILE)
    n_sub = FWD_MIX_SUBTILES if t % (FWD_MIX_SUBTILES * sub) == 0 else 1
    mt = n_sub * sub
    nt_ = t // mt
    nc = len(tables)
    ng = len(to_gather)

    def body(*refs):
        guv_ref, ph_ref, ws_ref, bs_ref, lw_ref, lb_ref, lo_ref, gn_ref = refs[:8]
        c_refs = refs[8:8 + nc]
        refs = refs[8 + nc:]
        g_in = refs[:ng]
        y_ref, st_ref = refs[ng:2 + ng]
        g_out = refs[2 + ng:2 + 2 * ng]
        state = refs[2 + 2 * ng]
        ga = _Gather2(g_in, g_out, *refs[3 + 2 * ng:])

        @pl.when(pl.program_id(0) == 0)
        def _():
            state[...] = jnp.zeros_like(state)
            ga.start()

        @pl.when(pl.program_id(0) == (3 * nt_) // 4)
        def _():
            ga.forward()

        st = tuple(state[h] for h in range(NH))
        consts = tuple(r[...] for r in c_refs)
        for s in range(n_sub):
            rows = slice(s * sub, (s + 1) * sub)
            for h in range(NH):
                st_ref[s, h] = st[h]
            y, st = _mix_tile(guv_ref[rows, :].astype(F32), ph_ref[rows, :], st, ws_ref[...], bs_ref[...], lw_ref[...],
                              lb_ref[...], lo_ref[...], gn_ref[...], consts)
            y_ref[rows, :] = y.astype(BF)
        for h in range(NH):
            state[h] = st[h]

        @pl.when(pl.program_id(0) == nt_ - 1)
        def _():
            ga.finish()

    full = lambda a: pl.BlockSpec(a.shape, lambda i, nd=a.ndim: (0,) * nd)
    outs = pl.pallas_call(
        body, name="fwd_mix", grid=(nt_,),
        out_shape=(jax.ShapeDtypeStruct((t, D), BF), jax.ShapeDtypeStruct((t // sub, NH, HD, HD), F32),
                   *_exchange_out_shapes(to_gather, False)),
        in_specs=[pl.BlockSpec((mt, D), lambda i: (i, 0)), pl.BlockSpec((mt, D_IN - D), lambda i: (i, 0)), full(w_s), full(b_s_t),
                  full(ln_w), full(ln_b), full(lower), full(gn_w)] + [full(a) for a in tables] + [_any() for _ in to_gather],
        out_specs=(pl.BlockSpec((mt, D), lambda i: (i, 0)), pl.BlockSpec((n_sub, NH, HD, HD), lambda i: (i, 0, 0, 0)),
                   *[_any() for _ in to_gather]),
        scratch_shapes=[pltpu.VMEM((NH, HD, HD), F32)] + _exchange_sems(ng),
        compiler_params=_params(dimension_semantics=("arbitrary",), has_side_effects=True),
    )(gelu_uv, proj_h, w_s, b_s_t, ln_w, ln_b, lower, gn_w, *tables, *to_gather)
    return outs[0], outs[1], outs[2:]


def _bwd_mix(gelu_uv, dgelu_uv, proj_h, dycat, states, w_s, b_s_t, ln_w, ln_b, lower, gn_w, tables, to_scatter):
    t = proj_h.shape[0]
    sub = _row_tile(t, MIX_TILE)
    n_sub = BWD_MIX_SUBTILES if t % (BWD_MIX_SUBTILES * sub) == 0 else 1
    mt = n_sub * sub
    nt_ = t // mt
    nc = len(tables)
    ns = len(to_scatter)

    def body(*refs):
        guv_ref, gd_ref, ph_ref, dy_ref, st_ref, ws_ref, bs_ref, lw_ref, lb_ref, lo_ref, gn_ref = refs[:11]
        c_refs = refs[11:11 + nc]
        refs = refs[11 + nc:]
        s_in = refs[:ns]
        dproj_ref = refs[ns]
        acc = refs[1 + ns:7 + ns]
        s_out = refs[7 + ns:7 + 2 * ns]
        dstate = refs[7 + 2 * ns]
        ex = _Exchange(s_in, s_out, *refs[8 + 2 * ns:], True)

        @pl.when(pl.program_id(0) == 0)
        def _():
            dstate[...] = jnp.zeros_like(dstate)
            for r in acc:
                r[...] = jnp.zeros_like(r)
            ex.start()

        consts = tuple(r[...] for r in c_refs)

        def f(a, p, s, ws, bs, lw, lb_, lo, gn):
            return _mix_tile(a, p, s, ws, bs, lw, lb_, lo, gn, consts)

        dst = tuple(dstate[h] for h in range(NH))
        for s in reversed(range(n_sub)):
            rows = slice(s * sub, (s + 1) * sub)
            st = tuple(st_ref[s, h] for h in range(NH))
            _, vjp = jax.vjp(f, guv_ref[rows, :].astype(F32), ph_ref[rows, :], st, ws_ref[...], bs_ref[...], lw_ref[...],
                             lb_ref[...], lo_ref[...], gn_ref[...])
            grads = vjp((dy_ref[rows, :], dst))
            dproj_ref[rows, 0:D] = (grads[0] * gd_ref[rows, :].astype(F32)).astype(BF)
            dproj_ref[rows, D:D_IN] = grads[1].astype(BF)
            dst = grads[2]
            for r, gval in zip(acc, grads[3:]):
                r[...] += gval
        for h in range(NH):
            dstate[h] = dst[h]

        @pl.when(pl.program_id(0) == nt_ - 1)
        def _():
            ex.wait()

    full = lambda a: pl.BlockSpec(a.shape, lambda i, nd=a.ndim: (0,) * nd)
    rev = lambda i: (nt_ - 1 - i, 0)
    smalls = (w_s, b_s_t, ln_w, ln_b, lower, gn_w)
    outs = pl.pallas_call(
        body, name="bwd_mix", grid=(nt_,),
        out_shape=(jax.ShapeDtypeStruct((t, D_IN), BF), *[jax.ShapeDtypeStruct(a.shape, F32) for a in smalls],
                   *_exchange_out_shapes(to_scatter, True)),
        in_specs=[pl.BlockSpec((mt, D), rev), pl.BlockSpec((mt, D), rev), pl.BlockSpec((mt, D_IN - D), rev),
                  pl.BlockSpec((mt, D), rev), pl.BlockSpec((n_sub, NH, HD, HD), lambda i: (nt_ - 1 - i, 0, 0, 0))]
        + [full(a) for a in smalls] + [full(a) for a in tables] + [_any() for _ in to_scatter],
        out_specs=(pl.BlockSpec((mt, D_IN), rev), *[full(a) for a in smalls], *[_any() for _ in to_scatter]),
        scratch_shapes=[pltpu.VMEM((NH, HD, HD), F32)] + _exchange_sems(ns),
        compiler_params=_params(dimension_semantics=("arbitrary",), has_side_effects=True),
    )(gelu_uv, dgelu_uv, proj_h, dycat, states, w_s, b_s_t, ln_w, ln_b, lower, gn_w, *tables, *to_scatter)
    return outs[0], outs[1:7], outs[7:]


def _row_tile(t, want):
    return want if t % want == 0 else t


def _rms(v):
    rstd = lax.rsqrt(jnp.mean(v * v, axis=-1, keepdims=True) + EPS)
    return v * rstd, rstd


def _rms_bwd(dxhat, xhat, rstd):
    return rstd * (dxhat - xhat * jnp.mean(dxhat * xhat, axis=-1, keepdims=True))


def _colsum(v):
    return jnp.sum(v, axis=0, keepdims=True)


def _fwd_in(x, ada, n1w, w_in_st, to_gather):
    t = x.shape[0]
    tm = _row_tile(t, 512)
    n_t = t // tm
    ng = len(to_gather)

    def body(*refs):
        x_ref, ada_ref, n1_ref, wst_ref = refs[:4]
        g_in = refs[4:4 + ng]
        ph_ref, ga_ref, gd_ref, h1_ref = refs[4 + ng:8 + ng]
        g_out = refs[8 + ng:8 + 2 * ng]
        w_ref, sems = refs[8 + 2 * ng:10 + 2 * ng]
        ga = _Gather2(g_in, g_out, *refs[10 + 2 * ng:])

        @pl.when(pl.program_id(0) == 0)
        def _():
            ga.start()
            _load_columns(wst_ref, w_ref, sems)

        @pl.when(pl.program_id(0) == (3 * n_t) // 4)
        def _():
            ga.forward()

        xh, _ = _rms(x_ref[...])
        h1 = (xh * n1_ref[...]) * (1.0 + ada_ref[1:2, :]) + ada_ref[0:1, :]
        h1b = h1.astype(BF)
        h1_ref[...] = h1b
        uv = _mm(h1b, w_ref[:, 0:D])
        cdf = 0.5 * (1.0 + lax.erf(uv * 0.7071067811865476))
        ga_ref[...] = (uv * cdf).astype(BF)
        gd_ref[...] = (cdf + uv * (jnp.exp(-0.5 * uv * uv) * 0.3989422804014327)).astype(BF)
        ph_ref[...] = _mm(h1b, w_ref[:, D:D_IN])

        @pl.when(pl.program_id(0) == n_t - 1)
        def _():
            ga.finish()

    full = lambda a: pl.BlockSpec(a.shape, lambda i, nd=a.ndim: (0,) * nd)
    row = lambda w: pl.BlockSpec((tm, w), lambda i: (i, 0))
    outs = pl.pallas_call(
        body, name="fwd_in", grid=(n_t,),
        out_shape=(jax.ShapeDtypeStruct((t, D_IN - D), F32), jax.ShapeDtypeStruct((t, D), BF), jax.ShapeDtypeStruct((t, D), BF),
                   jax.ShapeDtypeStruct((t, D), BF), *_exchange_out_shapes(to_gather, False)),
        in_specs=[row(D), full(ada), full(n1w), _any()] + [_any() for _ in to_gather],
        out_specs=(row(D_IN - D), row(D), row(D), row(D), *[_any() for _ in to_gather]),
        scratch_shapes=[pltpu.VMEM((D, D_IN), BF), pltpu.SemaphoreType.DMA((N_DEV,))] + _exchange_sems(ng),
        compiler_params=_params(dimension_semantics=("arbitrary",), has_side_effects=True),
    )(x, ada, n1w, w_in_st, *to_gather)
    return outs[0], outs[1], outs[2], outs[3], outs[4:]


def _fwd_ffn(x, ycat, tgt, ada, n2w, fw, w_out, w_fi, w_fo):
    t = x.shape[0]
    tm = _row_tile(t, 256)

    def body(x_ref, y_ref, t_ref, ada_ref, n2_ref, fw_ref, wo_ref, wi_ref, wf_ref,
             x1_ref, mix_ref, h2_ref, act_ref, gu_ref, dffn_ref, dx2_ref, part_ref):
        @pl.when(pl.program_id(0) == 0)
        def _():
            part_ref[...] = jnp.zeros_like(part_ref)

        g1, sh2, sc2, g2 = ada_ref[2:3, :], ada_ref[3:4, :], ada_ref[4:5, :], ada_ref[5:6, :]
        mix = _mm(y_ref[...], wo_ref[...])
        mix_ref[...] = mix.astype(BF)
        x1 = x_ref[...] + g1 * mix
        x1_ref[...] = x1
        xh2, _ = _rms(x1)
        h2b = ((xh2 * n2_ref[...]) * (1.0 + sc2) + sh2).astype(BF)
        h2_ref[...] = h2b
        ffn = jnp.zeros((tm, D), F32)
        for lo, hi in FF_CHUNKS:
            gate = _nt(h2b, wi_ref[lo:hi, :])
            up = _nt(h2b, wi_ref[D_FF + lo:D_FF + hi, :])
            gu_ref[:, lo:hi] = gate.astype(BF)
            gu_ref[:, D_FF + lo:D_FF + hi] = up.astype(BF)
            a = (_silu(gate) * up).astype(BF)
            act_ref[:, lo:hi] = a
            ffn = ffn + _mm(a, wf_ref[lo:hi, :])
        x2 = x1 + g2 * ffn
        xh3, rstd3 = _rms(x2)
        err = xh3 * fw_ref[...] - t_ref[...]
        dy = err * (1.0 / D)
        dx2 = _rms_bwd(dy * fw_ref[...], xh3, rstd3)
        dx2_ref[...] = dx2
        dffn_ref[...] = (g2 * dx2).astype(BF)
        part_ref[0:1, :] += _colsum(dx2 * ffn)
        part_ref[1:2, :] += _colsum(dy * xh3)
        part_ref[2:3, :] += jnp.zeros((1, D), F32) + (0.5 / D) * jnp.sum(err * err)

    full = lambda a: pl.BlockSpec(a.shape, lambda i, nd=a.ndim: (0,) * nd)
    row = lambda w: pl.BlockSpec((tm, w), lambda i: (i, 0))
    return pl.pallas_call(
        body, name="fwd_ffn", grid=(t // tm,),
        out_shape=(jax.ShapeDtypeStruct((t, D), F32), jax.ShapeDtypeStruct((t, D), BF), jax.ShapeDtypeStruct((t, D), BF),
                   jax.ShapeDtypeStruct((t, D_FF), BF), jax.ShapeDtypeStruct((t, 2 * D_FF), BF),
                   jax.ShapeDtypeStruct((t, D), BF), jax.ShapeDtypeStruct((t, D), F32), jax.ShapeDtypeStruct((8, D), F32)),
        in_specs=[row(D), row(D), row(D), full(ada), full(n2w), full(fw), _whole_vmem(), _whole_vmem(), _whole_vmem()],
        out_specs=(row(D), row(D), row(D), row(D_FF), row(2 * D_FF), row(D), row(D),
                   pl.BlockSpec((8, D), lambda i: (0, 0))),
        compiler_params=_params(dimension_semantics=("arbitrary",)),
    )(x, ycat, tgt, ada, n2w, fw, w_out, w_fi, w_fo)


def _bwd_ffn(x1, gu, dffn, dx2, mix, ada, n2w, w_out, w_fi, w_fo):
    t = x1.shape[0]
    tm = _row_tile(t, 256)

    def body(x1_ref, gu_ref, dffn_ref, dx2_ref, mix_ref, ada_ref, n2_ref, wo_ref, wi_ref, wf_ref,
             dgu_ref, dx1_ref, dmix_ref, dycat_ref, part_ref):
        @pl.when(pl.program_id(0) == 0)
        def _():
            part_ref[...] = jnp.zeros_like(part_ref)

        g1, sc2 = ada_ref[2:3, :], ada_ref[4:5, :]
        dffn = dffn_ref[...]
        dh2 = jnp.zeros((tm, D), F32)
        for lo, hi in FF_CHUNKS:
            gate = gu_ref[:, lo:hi].astype(F32)
            up = gu_ref[:, D_FF + lo:D_FF + hi].astype(F32)
            dact = _nt(dffn, wf_ref[lo:hi, :])
            sg = _sig(gate)
            dgate = (dact * up * (sg * (1.0 + gate * (1.0 - sg)))).astype(BF)
            dup = (dact * (gate * sg)).astype(BF)
            dgu_ref[:, lo:hi] = dgate
            dgu_ref[:, D_FF + lo:D_FF + hi] = dup
            dh2 = dh2 + _mm(dgate, wi_ref[lo:hi, :]) + _mm(dup, wi_ref[D_FF + lo:D_FF + hi, :])
        x1 = x1_ref[...]
        xh2, rstd2 = _rms(x1)
        xn2 = xh2 * n2_ref[...]
        dxn2 = dh2 * (1.0 + sc2)
        dx1 = dx2_ref[...] + _rms_bwd(dxn2 * n2_ref[...], xh2, rstd2)
        dx1_ref[...] = dx1
        dmix = (g1 * dx1).astype(BF)
        dmix_ref[...] = dmix
        dycat_ref[...] = _nt(dmix, wo_ref[...])
        part_ref[0:1, :] += _colsum(dh2)
        part_ref[1:2, :] += _colsum(dh2 * xn2)
        part_ref[2:3, :] += _colsum(dxn2 * xh2)
        part_ref[3:4, :] += _colsum(dx1 * mix_ref[...].astype(F32))

    full = lambda a: pl.BlockSpec(a.shape, lambda i, nd=a.ndim: (0,) * nd)
    row = lambda w: pl.BlockSpec((tm, w), lambda i: (i, 0))
    return pl.pallas_call(
        body, name="bwd_ffn", grid=(t // tm,),
        out_shape=(jax.ShapeDtypeStruct((t, 2 * D_FF), BF), jax.ShapeDtypeStruct((t, D), F32),
                   jax.ShapeDtypeStruct((t, D), BF), jax.ShapeDtypeStruct((t, D), F32), jax.ShapeDtypeStruct((8, D), F32)),
        in_specs=[row(D), row(2 * D_FF), row(D), row(D), row(D), full(ada), full(n2w),
                  _whole_vmem(), _whole_vmem(), _whole_vmem()],
        out_specs=(row(2 * D_FF), row(D), row(D), row(D), pl.BlockSpec((8, D), lambda i: (0, 0))),
        compiler_params=_params(dimension_semantics=("arbitrary",)),
    )(x1, gu, dffn, dx2, mix, ada, n2w, w_out, w_fi, w_fo)


def _bwd_in(x, dproj, dx1, ada, n1w, w_in_st, to_scatter, to_gather):
    t = x.shape[0]
    tm = _row_tile(t, 1024)
    n_t = t // tm
    ns = len(to_scatter)
    ng = len(to_gather)
    nx = ns + ng

    def body(*refs):
        x_ref, dp_ref, dx1_ref, ada_ref, n1_ref, wst_ref = refs[:6]
        x_in = refs[6:6 + nx]
        gx_ref, part_ref = refs[6 + nx:8 + nx]
        x_out = refs[8 + nx:8 + 2 * nx]
        w_ref, sems = refs[8 + 2 * nx:10 + 2 * nx]
        sem_refs = refs[10 + 2 * nx:]
        ex = _Exchange(x_in[:ns], x_out[:ns], *sem_refs[:3], True)
        gx = _Exchange(x_in[ns:], x_out[ns:], *sem_refs[3:], False) if ng else None

        @pl.when(pl.program_id(0) == 0)
        def _():
            ex.start()
            if ng:
                gx.start()
            part_ref[...] = jnp.zeros_like(part_ref)
            _load_columns(wst_ref, w_ref, sems)

        dh1 = _nt(dp_ref[...], w_ref[...])
        xh, rstd = _rms(x_ref[...])
        xn = xh * n1_ref[...]
        dxn = dh1 * (1.0 + ada_ref[1:2, :])
        gx_ref[...] = dx1_ref[...] + _rms_bwd(dxn * n1_ref[...], xh, rstd)
        part_ref[0:1, :] += _colsum(dh1)
        part_ref[1:2, :] += _colsum(dh1 * xn)
        part_ref[2:3, :] += _colsum(dxn * xh)

        @pl.when(pl.program_id(0) == n_t - 1)
        def _():
            ex.wait()
            if ng:
                gx.wait()

    full = lambda a: pl.BlockSpec(a.shape, lambda i, nd=a.ndim: (0,) * nd)
    row = lambda w: pl.BlockSpec((tm, w), lambda i: (i, 0))
    outs = pl.pallas_call(
        body, name="bwd_in", grid=(n_t,),
        out_shape=(jax.ShapeDtypeStruct((t, D), F32), jax.ShapeDtypeStruct((8, D), F32),
                   *_exchange_out_shapes(to_scatter, True), *_exchange_out_shapes(to_gather, False)),
        in_specs=[row(D), row(D_IN), row(D), full(ada), full(n1w), _any()] + [_any() for _ in range(nx)],
        out_specs=(row(D), pl.BlockSpec((8, D), lambda i: (0, 0)), *[_any() for _ in range(nx)]),
        scratch_shapes=[pltpu.VMEM((D, D_IN), BF), pltpu.SemaphoreType.DMA((N_DEV,))] + _exchange_sems(ns)
        + (_exchange_sems(ng) if ng else []),
        compiler_params=_params(dimension_semantics=("arbitrary",), has_side_effects=True),
    )(x, dproj, dx1, ada, n1w, w_in_st, *to_scatter, *to_gather)
    return outs[0], outs[1], outs[2:2 + ns], outs[2 + ns:]


def _wgrad(a, b, name, a_spec, b_spec, out_shape, out_spec, grid, acc_shape, split=1, to_gather=()):
    n_j, n_t = grid
    ng = len(to_gather)

    def body(*refs):
        a_ref, b_ref = refs[:2]
        g_in = refs[2:2 + ng]
        o_ref = refs[2 + ng]
        g_out = refs[3 + ng:3 + 2 * ng]
        acc = refs[3 + 2 * ng]
        first = (pl.program_id(0) == 0) & (pl.program_id(1) == 0)
        last = (pl.program_id(0) == n_j - 1) & (pl.program_id(1) == n_t - 1)
        if ng:
            gx = _Exchange(g_in, g_out, *refs[4 + 2 * ng:], False)

            @pl.when(first)
            def _():
                gx.start()

        @pl.when(pl.program_id(1) == 0)
        def _():
            acc[...] = jnp.zeros_like(acc)

        acc[...] += _tn(a_ref[...], b_ref[...])

        @pl.when(pl.program_id(1) == n_t - 1)
        def _():
            if split == 1:
                o_ref[...] = acc[...].astype(BF)
            else:
                w = acc_shape[1] // split
                for s in range(split):
                    o_ref[s] = acc[:, s * w:(s + 1) * w].astype(BF)

        if ng:
            @pl.when(last)
            def _():
                gx.wait()

    outs = pl.pallas_call(
        body, name=name, grid=grid,
        out_shape=(jax.ShapeDtypeStruct(out_shape, BF), *_exchange_out_shapes(to_gather, False)),
        in_specs=[a_spec, b_spec] + [_any() for _ in to_gather], out_specs=(out_spec, *[_any() for _ in to_gather]),
        scratch_shapes=[pltpu.VMEM(acc_shape, F32)] + (_exchange_sems(ng) if ng else []),
        compiler_params=_params(dimension_semantics=("arbitrary", "arbitrary"), has_side_effects=bool(ng)),
    )(a, b, *to_gather)
    return (outs[0], outs[1:]) if ng else outs[0]


def _adamw_math(w, g, m, v):
    m = ADAM_B1 * m + (1.0 - ADAM_B1) * g
    v = ADAM_B2 * v + (1.0 - ADAM_B2) * (g * g)
    m_hat = m / (1.0 - ADAM_B1 ** ADAM_STEP)
    v_hat = v / (1.0 - ADAM_B2 ** ADAM_STEP)
    delta = -ADAM_LR * (m_hat / (jnp.sqrt(v_hat) + ADAM_EPS) + ADAM_WD * w)
    return delta, m, v


def _adamw_recv(w, m, v, recv, name, tr):
    r, c = w.shape

    def body(w_ref, m_ref, v_ref, r_ref, g_ref, d_ref, nm_ref, nv_ref):
        g = r_ref[0].astype(F32)
        for k in range(1, N_DEV):
            g = g + r_ref[k].astype(F32)
        g_ref[...] = g
        d_ref[...], nm_ref[...], nv_ref[...] = _adamw_math(w_ref[...], g, m_ref[...], v_ref[...])

    row = pl.BlockSpec((tr, c), lambda i: (i, 0))
    return pl.pallas_call(
        body, name=name, grid=(r // tr,),
        out_shape=tuple(jax.ShapeDtypeStruct((r, c), F32) for _ in range(4)),
        in_specs=[row, row, row, pl.BlockSpec((N_DEV, tr, c), lambda i: (0, i, 0))],
        out_specs=(row, row, row, row),
        compiler_params=_params(dimension_semantics=("arbitrary",)),
    )(w, m, v, recv)


def _adamw_ada(w, m, v, cact, dada_cols):
    r, c = w.shape
    tr = 256

    def body(w_ref, m_ref, v_ref, ca_ref, da_ref, g_ref, d_ref, nm_ref, nv_ref):
        g = _tn(ca_ref[...].astype(BF), da_ref[...].astype(BF))
        g_ref[...] = g
        d_ref[...], nm_ref[...], nv_ref[...] = _adamw_math(w_ref[...], g, m_ref[...], v_ref[...])

    row = pl.BlockSpec((tr, c), lambda i: (i, 0))
    return pl.pallas_call(
        body, name="adamw_ada", grid=(r // tr,),
        out_shape=tuple(jax.ShapeDtypeStruct((r, c), F32) for _ in range(4)),
        in_specs=[row, row, row, pl.BlockSpec((N_DEV, tr), lambda i: (0, i)), pl.BlockSpec(dada_cols.shape, lambda i: (0, 0))],
        out_specs=(row, row, row, row),
        compiler_params=_params(dimension_semantics=("arbitrary",)),
    )(w, m, v, cact, dada_cols)


def _adamw_small(gathered, wmv):
    n_g = len(gathered)
    n_p = len(wmv)
    flat = [a for trip in wmv for a in trip]

    def body(*refs):
        g_refs = refs[:n_g]
        p_refs = refs[n_g:n_g + 3 * n_p]
        o_refs = refs[n_g + 3 * n_p:]

        def total(ref):
            s = ref[0]
            for k in range(1, N_DEV):
                s = s + ref[k]
            return s

        f3, b3, b1, dws, dbs, dlnw, dlnb, dlo, dgn = [total(r) for r in g_refs]
        dada_rows = [b1[0:1], b1[1:2], b3[3:4], b3[0:1], b3[1:2], f3[0:1]]
        for r, g in enumerate(dada_rows):
            cs = slice(r * D, (r + 1) * D)
            w, m, v = p_refs[0][:, cs], p_refs[1][:, cs], p_refs[2][:, cs]
            o_refs[0][:, cs] = g
            o_refs[1][:, cs], o_refs[2][:, cs], o_refs[3][:, cs] = _adamw_math(w, g, m, v)
        grads = [None, b1[2:3], dws, dbs, dlnw, dlnb, dlo, dgn, b3[2:3], f3[1:2]]
        for i, g in enumerate(grads):
            if g is None:
                continue
            w, m, v = p_refs[3 * i][...], p_refs[3 * i + 1][...], p_refs[3 * i + 2][...]
            o_refs[4 * i][...] = g
            o_refs[4 * i + 1][...], o_refs[4 * i + 2][...], o_refs[4 * i + 3][...] = _adamw_math(w, g, m, v)
        o_refs[4 * n_p][...] = jnp.zeros((8, 128), F32) + f3[2:3, 0:128]

    out_shape = []
    for w, _, _ in wmv:
        out_shape += [jax.ShapeDtypeStruct(w.shape, F32)] * 4
    out_shape.append(jax.ShapeDtypeStruct((8, 128), F32))
    n_in = n_g + 3 * n_p
    return pl.pallas_call(
        body, name="adamw_small",
        out_shape=tuple(out_shape),
        in_specs=[_whole_vmem()] * n_in, out_specs=tuple(_whole_vmem() for _ in out_shape),
        compiler_params=_params(),
    )(*gathered, *flat)


def kernel(x, c, w_ada, b_ada, norm1_w, w_in, w_s, b_s, v_ln_w, v_ln_b, lower_bounds, gn_w, w_out, norm2_w, w_ffn_in, w_ffn_out, final_norm_w, loss_target, m_w_ada, m_b_ada, m_norm1_w, m_w_in, m_w_s, m_b_s, m_v_ln_w, m_v_ln_b, m_lower_bounds, m_gn_w, m_w_out, m_norm2_w, m_w_ffn_in, m_w_ffn_out, m_final_norm_w, v_w_ada, v_b_ada, v_norm1_w, v_w_in, v_w_s, v_b_s, v_v_ln_w, v_v_ln_b, v_lower_bounds, v_gn_w, v_w_out, v_norm2_w, v_w_ffn_in, v_w_ffn_out, v_final_norm_w):
    me = 4 * lax.axis_index("x") + 2 * lax.axis_index("y") + lax.axis_index("c")
    t = x.shape[1]
    x2d = x.reshape(t, D)
    tgt = loss_target.reshape(t, D)
    ada_cols = w_ada.shape[2]

    tp = lambda a: jnp.swapaxes(a[0], 0, 1)
    win_b, wout_b, wfi_b, wfo_b = _cast_bf16([w_in[0], w_out[0], tp(w_ffn_in), w_ffn_out[0]], "cast_weights")

    b_cols = lax.dynamic_slice(b_ada, (0, me * ada_cols), (1, ada_cols))
    win_st, ada_st, cact = _startup(win_b, c, w_ada[0], b_cols)
    ada = lax.dynamic_index_in_dim(ada_st, me, axis=1, keepdims=False).reshape(6, D)

    tables = _decay_tables()
    ws3 = w_s[0]
    bs_t = b_s[0].T

    proj_h, gelu_uv, dgelu_uv, h1, (wout_st, wfo_st) = _fwd_in(x2d, ada, norm1_w, win_st, [wout_b, wfo_b])
    ycat, states, (wfi_st,) = _fwd_mix(gelu_uv, proj_h, ws3, bs_t, v_ln_w, v_ln_b, lower_bounds, gn_w, tables, [wfi_b])
    w_out_full = wout_st.reshape(D, D)
    w_fo_full = wfo_st.reshape(D_FF, D)
    w_fi_full = wfi_st.reshape(2 * D_FF, D)
    x1, mixb, h2, act, gu, dffn, dx2, part_f = _fwd_ffn(x2d, ycat, tgt, ada, norm2_w, final_norm_w.reshape(1, D),
                                                        w_out_full, w_fi_full, w_fo_full)
    dgu, dx1, dmix, dycat, part_b3 = _bwd_ffn(x1, gu, dffn, dx2, mixb, ada, norm2_w, w_out_full, w_fi_full, w_fo_full)
    tk = _row_tile(t, 2048)
    n_t = t // tk
    win_cols = D_IN // N_DEV
    dwout = _wgrad(ycat, dmix, "wgrad_out",
                   pl.BlockSpec((tk, D), lambda j, i: (i, 0)), pl.BlockSpec((tk, D), lambda j, i: (i, 0)),
                   (D, D), pl.BlockSpec((D, D), lambda j, i: (0, 0)), (1, n_t), (D, D))
    dwfi = _wgrad(dgu, h2, "wgrad_ffn_in",
                  pl.BlockSpec((tk, FF_PAIR), lambda j, i: (i, j)), pl.BlockSpec((tk, D), lambda j, i: (i, 0)),
                  (4, FF_PAIR, D), pl.BlockSpec((None, FF_PAIR, D), lambda j, i: (j, 0, 0)), (4, n_t), (FF_PAIR, D))
    dwfo = _wgrad(act, dffn, "wgrad_ffn_out",
                  pl.BlockSpec((tk, FF_PAIR), lambda j, i: (i, j)), pl.BlockSpec((tk, D), lambda j, i: (i, 0)),
                  (2, FF_PAIR, D), pl.BlockSpec((None, FF_PAIR, D), lambda j, i: (j, 0, 0)), (2, n_t), (FF_PAIR, D))
    dproj, (dws, dbs_t, dlnw, dlnb, dlower, dgnw), (r_out, r_fi, r_fo) = _bwd_mix(
        gelu_uv, dgelu_uv, proj_h, dycat, states, ws3, bs_t, v_ln_w, v_ln_b, lower_bounds, gn_w, tables,
        [dwout.reshape(N_DEV, D // N_DEV, D), dwfi.reshape(N_DEV, FF_BLK, D), dwfo.reshape(N_DEV, D_FF // N_DEV, D)])
    dwin, early = _wgrad(h1, dproj, "wgrad_in",
                         pl.BlockSpec((tk, D), lambda j, i: (i, 0)), pl.BlockSpec((tk, 4 * win_cols), lambda j, i: (i, j)),
                         (N_DEV, D, win_cols), pl.BlockSpec((4, D, win_cols), lambda j, i: (j, 0, 0)), (N_DEV // 4, n_t),
                         (D, 4 * win_cols), split=4, to_gather=[part_f, part_b3, dws, dbs_t, dlnw, dlnb, dlower, dgnw])
    grad_x, part_b1, (r_in,), _ = _bwd_in(x2d, dproj, dx1, ada, norm1_w, win_st, [dwin], [])

    g_w_in, d_w_in, nm_w_in, nv_w_in = _adamw_recv(w_in[0], m_w_in[0], v_w_in[0], r_in, "adamw_w_in", 256)
    g_w_out, d_w_out, nm_w_out, nv_w_out = _adamw_recv(w_out[0], m_w_out[0], v_w_out[0], r_out, "adamw_w_out", 128)
    fi_t = _adamw_recv(tp(w_ffn_in), tp(m_w_ffn_in), tp(v_w_ffn_in), r_fi, "adamw_w_ffn_in", 176)
    g_w_fi, d_w_fi, nm_w_fi, nv_w_fi = [jnp.swapaxes(a, 0, 1) for a in fi_t]
    g_w_fo, d_w_fo, nm_w_fo, nv_w_fo = _adamw_recv(w_ffn_out[0], m_w_ffn_out[0], v_w_ffn_out[0], r_fo, "adamw_w_ffn_out", 176)

    (b1_all,) = _exchange([part_b1], "gather_small", False, True)
    gathered = [early[0], early[1], b1_all, *early[2:]]
    f3_all, b3_all = gathered[0], gathered[1]
    dada_all = jnp.stack([b1_all[:, 0], b1_all[:, 1], b3_all[:, 3], b3_all[:, 0], b3_all[:, 1], f3_all[:, 0]], axis=1)
    dada_cols = lax.dynamic_slice(dada_all.reshape(N_DEV, 6 * D), (0, me * ada_cols), (N_DEV, ada_cols))
    g_w_ada, d_w_ada, nm_w_ada, nv_w_ada = _adamw_ada(w_ada[0], m_w_ada[0], v_w_ada[0], cact, dada_cols)

    r1 = lambda a: a.reshape(1, D)
    tr = lambda a: a[0].T
    wmv = [
        (b_ada, m_b_ada, v_b_ada),
        (norm1_w, m_norm1_w, v_norm1_w),
        (w_s[0], m_w_s[0], v_w_s[0]),
        (tr(b_s), tr(m_b_s), tr(v_b_s)),
        (v_ln_w, m_v_ln_w, v_v_ln_w),
        (v_ln_b, m_v_ln_b, v_v_ln_b),
        (lower_bounds, m_lower_bounds, v_lower_bounds),
        (gn_w, m_gn_w, v_gn_w),
        (norm2_w, m_norm2_w, v_norm2_w),
        (r1(final_norm_w), r1(m_final_norm_w), r1(v_final_norm_w)),
    ]
    small = _adamw_small(gathered, wmv)
    loss = small[-1][0, 0]

    def unshape(i, a):
        if i == 2:
            return a.reshape(1, NH, BLK, BLK)
        if i == 3:
            return a.T.reshape(1, NH, BLK)
        if i == 9:
            return a.reshape(D)
        return a

    def small_out(kind):
        return [unshape(i, small[4 * i + kind]) for i in range(len(wmv))]

    e3 = lambda a: a[None]
    big = {
        0: (e3(g_w_ada), e3(g_w_in), e3(g_w_out), e3(g_w_fi), e3(g_w_fo)),
        1: (e3(d_w_ada), e3(d_w_in), e3(d_w_out), e3(d_w_fi), e3(d_w_fo)),
        2: (e3(nm_w_ada), e3(nm_w_in), e3(nm_w_out), e3(nm_w_fi), e3(nm_w_fo)),
        3: (e3(nv_w_ada), e3(nv_w_in), e3(nv_w_out), e3(nv_w_fi), e3(nv_w_fo)),
    }

    def ordered(kind):
        s = small_out(kind)
        b_ = big[kind]
        return [b_[0], s[0], s[1], b_[1], s[2], s[3], s[4], s[5], s[6], s[7], b_[2], s[8], b_[3], b_[4], s[9]]

    return (loss, grad_x.reshape(1, t, D), *ordered(0), *ordered(1), *ordered(2), *ordered(3))
```

```python
import numpy as np
import jax
import jax.numpy as jnp
from jax import lax
from jax.experimental import pallas as pl
from jax.experimental.pallas import tpu as pltpu

F32 = jnp.float32
BF = jnp.bfloat16
MESH = pl.DeviceIdType.MESH

N_DEV = 8
D = 1024
D_IN = 3072
D_FF = 2816
FF_BLK = D_FF // 4
FF_CHUNKS = ((0, D_FF),)
FF_PAIR = 2 * FF_BLK
CH = 64
BLK = 128
MIX_TILE = 256
FWD_MIX_SUBTILES = 4
BWD_MIX_SUBTILES = 2
NH = 4
HD = 128
EPS = 1e-6
LEVELS = (32, 16, 8, 4, 2, 1)

ADAM_LR = 0.001
ADAM_B1 = 0.9
ADAM_B2 = 0.999
ADAM_EPS = 1e-08
ADAM_WD = 0.01
ADAM_STEP = 10

VMEM_LIMIT = 56 * 1024 * 1024


def _params(**kw):
    return pltpu.CompilerParams(vmem_limit_bytes=VMEM_LIMIT, **kw)


def _whole_vmem():
    return pl.BlockSpec(memory_space=pltpu.VMEM)


def _any():
    return pl.BlockSpec(memory_space=pl.ANY)


class _Exchange:
    def __init__(self, ins, outs, send_sems, recv_sems, local_sems, scatter):
        self.ins, self.outs, self.scatter = ins, outs, scatter
        self.send_sems, self.recv_sems, self.local_sems = send_sems, recv_sems, local_sems
        x, y, c = lax.axis_index("x"), lax.axis_index("y"), lax.axis_index("c")
        self.me = 4 * x + 2 * y + c
        self.peers = []
        for k in range(1, N_DEV):
            peer = (1 - x if (k >> 2) & 1 else x, 1 - y if (k >> 1) & 1 else y, 1 - c if k & 1 else c)
            self.peers.append((peer, 4 * peer[0] + 2 * peer[1] + peer[2]))

    def _src(self, a, idx):
        return self.ins[a].at[idx] if self.scatter else self.ins[a]

    def _local(self):
        return [pltpu.make_async_copy(self._src(a, self.me), self.outs[a].at[self.me], self.local_sems.at[a])
                for a in range(len(self.ins))]

    def _remote(self, a, k, dst_slot):
        peer, peer_idx = self.peers[k]
        return pltpu.make_async_remote_copy(
            src_ref=self._src(a, peer_idx), dst_ref=self.outs[a].at[dst_slot],
            send_sem=self.send_sems.at[a, k], recv_sem=self.recv_sems.at[a, k],
            device_id=peer, device_id_type=MESH)

    def start(self):
        for cp in self._local():
            cp.start()
        for k in range(N_DEV - 1):
            for a in range(len(self.ins)):
                self._remote(a, k, self.me).start()

    def wait(self):
        for k in range(N_DEV - 1):
            for a in range(len(self.ins)):
                self._remote(a, k, self.peers[k][1]).wait()
        for cp in self._local():
            cp.wait()


class _Gather2:
    def __init__(self, ins, outs, send_sems, recv_sems, local_sems):
        self.ins, self.outs = ins, outs
        self.send_sems, self.recv_sems, self.local_sems = send_sems, recv_sems, local_sems
        x, y, c = lax.axis_index("x"), lax.axis_index("y"), lax.axis_index("c")
        self.c = c
        self.me = 4 * x + 2 * y + c
        self.sibling = (x, y, 1 - c)
        self.chips = [(1 - x, y), (x, 1 - y), (1 - x, 1 - y)]

    @staticmethod
    def _idx(px, py, pc):
        return 4 * px + 2 * py + pc

    def _copy(self, a, k, slot, to, own):
        src = self.ins[a] if own else self.outs[a].at[slot]
        return pltpu.make_async_remote_copy(
            src_ref=src, dst_ref=self.outs[a].at[slot],
            send_sem=self.send_sems.at[a, k], recv_sem=self.recv_sems.at[a, k],
            device_id=to, device_id_type=MESH)

    def _local(self):
        return [pltpu.make_async_copy(self.ins[a], self.outs[a].at[self.me], self.local_sems.at[a])
                for a in range(len(self.ins))]

    def start(self):
        for cp in self._local():
            cp.start()
        for a in range(len(self.ins)):
            self._copy(a, 0, self.me, self.sibling, True).start()
            for j, chip in enumerate(self.chips):
                self._copy(a, 1 + j, self.me, (*chip, self.c), True).start()

    def forward(self):
        for j, chip in enumerate(self.chips):
            for a in range(len(self.ins)):
                slot = self._idx(*chip, self.c)
                self._copy(a, 1 + j, slot, (*chip, self.c), True).wait_recv()
                self._copy(a, 4 + j, slot, self.sibling, False).start()

    def finish(self):
        for a in range(len(self.ins)):
            self._copy(a, 0, self._idx(*self.sibling), self.sibling, True).wait_recv()
            for j, chip in enumerate(self.chips):
                self._copy(a, 4 + j, self._idx(*chip, 1 - self.c), self.sibling, False).wait_recv()
            self._copy(a, 0, self.me, self.sibling, True).wait_send()
            for j, chip in enumerate(self.chips):
                self._copy(a, 1 + j, self.me, (*chip, self.c), True).wait_send()
                self._copy(a, 4 + j, self._idx(*chip, self.c), self.sibling, False).wait_send()
        for cp in self._local():
            cp.wait()


def _exchange_sems(n):
    return [pltpu.SemaphoreType.DMA((n, N_DEV - 1)), pltpu.SemaphoreType.DMA((n, N_DEV - 1)), pltpu.SemaphoreType.DMA((n,))]


def _exchange_out_shapes(arrays, scatter):
    return [jax.ShapeDtypeStruct(a.shape if scatter else (N_DEV,) + a.shape, a.dtype) for a in arrays]


def _exchange(arrays, name, scatter, in_vmem):
    n = len(arrays)

    def body(*refs):
        ex = _Exchange(refs[:n], refs[n:2 * n], *refs[2 * n:], scatter)
        ex.start()
        ex.wait()

    spec = _whole_vmem if in_vmem else _any
    return pl.pallas_call(
        body, name=name,
        out_shape=tuple(_exchange_out_shapes(arrays, scatter)),
        in_specs=[spec() for _ in arrays],
        out_specs=tuple(spec() for _ in arrays),
        scratch_shapes=_exchange_sems(n),
        compiler_params=_params(has_side_effects=True),
    )(*arrays)


def _cast_bf16(arrays, name):
    n = len(arrays)

    def body(*refs):
        for i in range(n):
            refs[n + i][...] = refs[i][...].astype(BF)

    return pl.pallas_call(
        body, name=name,
        out_shape=tuple(jax.ShapeDtypeStruct(a.shape, BF) for a in arrays),
        in_specs=[_whole_vmem() for _ in arrays],
        out_specs=tuple(_whole_vmem() for _ in arrays),
        compiler_params=_params(),
    )(*arrays)


RING = 3


def _ring_fetch(hbm_refs, bufs, sems, tm, n_steps):
    i = pl.program_id(0)

    def copies(tile, slot):
        rows = pl.ds(pl.multiple_of(tile * tm, tm), tm)
        return [pltpu.make_async_copy(h.at[rows, :], b.at[slot], sems.at[k, slot]) for k, (h, b) in enumerate(zip(hbm_refs, bufs))]

    @pl.when(i == 0)
    def _():
        for k in range(min(RING - 1, n_steps)):
            for cp in copies(k, k):
                cp.start()

    @pl.when(i + (RING - 1) < n_steps)
    def _():
        for cp in copies(i + (RING - 1), (i + (RING - 1)) % RING):
            cp.start()

    slot = i % RING
    for cp in copies(i, slot):
        cp.wait()
    return slot


def _load_columns(stacked_ref, full_ref, sems):
    c = stacked_ref.shape[2]
    cps = [pltpu.make_async_copy(stacked_ref.at[d], full_ref.at[:, pl.ds(d * c, c)], sems.at[d]) for d in range(N_DEV)]
    for cp in cps:
        cp.start()
    for cp in cps:
        cp.wait()


def _sig(v):
    return 0.5 * jnp.tanh(0.5 * v) + 0.5


@jax.custom_vjp
def _silu(v):
    return v * _sig(v)


def _silu_fwd(v):
    return _silu(v), v


def _silu_bwd(v, g):
    s = _sig(v)
    return (g * (s * (1.0 + v * (1.0 - s))),)


_silu.defvjp(_silu_fwd, _silu_bwd)


@jax.custom_vjp
def _sigmoid_rel(v):
    e = jnp.exp(-jnp.abs(v))
    d = 1.0 + e
    r = pl.reciprocal(d, approx=True)
    r = r * (2.0 - d * r)
    r = r * (2.0 - d * r)
    return jnp.where(v >= 0.0, r, e * r)


def _sigmoid_rel_fwd(v):
    s = _sigmoid_rel(v)
    return s, s


def _sigmoid_rel_bwd(s, g):
    return (g * (s * (1.0 - s)),)


_sigmoid_rel.defvjp(_sigmoid_rel_fwd, _sigmoid_rel_bwd)


def _dot(a, b, ca, cb):
    return lax.dot_general(a, b, (((ca,), (cb,)), ((), ())), preferred_element_type=F32)


@jax.custom_vjp
def _mm(a, b):
    return _dot(a, b, 1, 0)


def _mm_fwd(a, b):
    return _dot(a, b, 1, 0), (a, b)


def _mm_bwd(res, g):
    a, b = res
    gb = g.astype(BF)
    return _dot(gb, b, 1, 1).astype(a.dtype), _dot(a, gb, 0, 0).astype(b.dtype)


_mm.defvjp(_mm_fwd, _mm_bwd)


@jax.custom_vjp
def _nt(a, b):
    return _dot(a, b, 1, 1)


def _nt_fwd(a, b):
    return _dot(a, b, 1, 1), (a, b)


def _nt_bwd(res, g):
    a, b = res
    gb = g.astype(BF)
    return _dot(gb, b, 1, 0).astype(a.dtype), _dot(gb, a, 0, 0).astype(b.dtype)


_nt.defvjp(_nt_fwd, _nt_bwd)


@jax.custom_vjp
def _tn(a, b):
    return _dot(a, b, 0, 0)


def _tn_fwd(a, b):
    return _dot(a, b, 0, 0), (a, b)


def _tn_bwd(res, g):
    a, b = res
    gb = g.astype(BF)
    return _dot(b, gb, 1, 1).astype(a.dtype), _dot(a, gb, 1, 0).astype(b.dtype)


_tn.defvjp(_tn_fwd, _tn_bwd)


def _make_cast_dot(ca, cb, da_dims, db_dims):
    @jax.custom_vjp
    def dot(a, b):
        return _dot(a.astype(BF), b.astype(BF), ca, cb)

    def fwd(a, b):
        ab, bb = a.astype(BF), b.astype(BF)
        return _dot(ab, bb, ca, cb), (ab, bb)

    def bwd(res, g):
        ops = {"a": res[0], "b": res[1], "g": g.astype(BF)}
        return (_dot(ops[da_dims[0]], ops[da_dims[1]], da_dims[2], da_dims[3]),
                _dot(ops[db_dims[0]], ops[db_dims[1]], db_dims[2], db_dims[3]))

    dot.defvjp(fwd, bwd)
    return dot


_mm_c = _make_cast_dot(1, 0, ("g", "b", 1, 1), ("a", "g", 0, 0))
_nt_c = _make_cast_dot(1, 1, ("g", "b", 1, 0), ("g", "a", 0, 0))
_tn_c = _make_cast_dot(0, 0, ("b", "g", 1, 1), ("a", "g", 1, 0))


def _startup(win_b, c, w_ada, b_cols):
    ncol = w_ada.shape[1]

    def body(win_ref, c_ref, w_ref, b_ref, winst_ref, adast_ref, cact_ref, call_ref, blk_ref,
             ws, wr, wl, cs, cr, cl, as_, ar, al):
        big = _Gather2([win_ref], [winst_ref], ws, wr, wl)
        big.start()
        gc = _Exchange([c_ref], [call_ref], cs, cr, cl, False)
        gc.start()
        gc.wait()
        ca = _silu(call_ref[...].reshape(N_DEV, D))
        cact_ref[...] = ca
        blk_ref[...] = _mm(ca.astype(BF), w_ref[...].astype(BF)) + b_ref[...]
        ga = _Exchange([blk_ref], [adast_ref], as_, ar, al, False)
        ga.start()
        ga.wait()
        big.forward()
        big.finish()

    return pl.pallas_call(
        body, name="startup",
        out_shape=(jax.ShapeDtypeStruct((N_DEV,) + win_b.shape, BF), jax.ShapeDtypeStruct((N_DEV, N_DEV, ncol), F32),
                   jax.ShapeDtypeStruct((N_DEV, D), F32)),
        in_specs=[_any(), _whole_vmem(), _whole_vmem(), _whole_vmem()],
        out_specs=(_any(), _whole_vmem(), _whole_vmem()),
        scratch_shapes=[pltpu.VMEM((N_DEV, 1, D), F32), pltpu.VMEM((N_DEV, ncol), F32)]
        + _exchange_sems(1) + _exchange_sems(1) + _exchange_sems(1),
        compiler_params=_params(has_side_effects=True),
    )(win_b, c, w_ada, b_cols)


def _decay_tables():
    t = np.arange(CH)
    tri = (t[None, :] <= t[:, None]).astype(np.float32)
    masks = []
    for h in LEVELS:
        m = (t // (2 * h)) * (2 * h) + h
        upper = t >= m
        same = (t[:, None] // (2 * h)) == (t[None, :] // (2 * h))
        masks.append(same & upper[:, None] & (~upper)[None, :])
    masks.append(np.eye(CH, dtype=bool))
    lv = [np.where((t % (2 * h)) >= h, 1.0, -1.0) for h in LEVELS[:4]]
    m4 = t % 4
    lv += [(m4 == 0) * 1.0, (m4 >= 2) * 1.0, (m4 == 3) * 1.0, (t % 2 == 1) * 1.0]
    lvl = np.broadcast_to(np.stack(lv)[:, :, None], (8, CH, D // 2)).astype(np.float32)
    cid = np.arange(BLK) // CH
    gmask = (cid[:, None] >= cid[None, :]).astype(np.float32)
    masks = np.stack(masks).astype(np.float32)
    stacked = np.zeros((masks.shape[0], NH * CH, NH * CH), np.float32)
    for h in range(NH):
        stacked[:, h * CH:(h + 1) * CH, h * CH:(h + 1) * CH] = masks
    return (jnp.asarray(tri, BF), jnp.asarray(stacked), jnp.asarray(gmask), jnp.asarray(lvl))


def _split2(v):
    v1 = v.astype(BF)
    return v1, (v - v1.astype(F32)).astype(BF)


@jax.custom_vjp
def _cumsum_mm(tri, v):
    p1, p2 = _split2(v)
    return _dot(tri, p1, 1, 0) + _dot(tri, p2, 1, 0)


def _cumsum_mm_fwd(tri, v):
    return _cumsum_mm(tri, v), tri


def _cumsum_mm_bwd(tri, g):
    p1, p2 = _split2(g)
    return jnp.zeros_like(tri), _dot(tri, p1, 0, 0) + _dot(tri, p2, 0, 0)


_cumsum_mm.defvjp(_cumsum_mm_fwd, _cumsum_mm_bwd)


def _make_row_roll(shift):
    @jax.custom_vjp
    def roll(x):
        return pltpu.roll(x, shift % CH, 0)

    def fwd(x):
        return roll(x), None

    def bwd(_, g):
        return (pltpu.roll(g, (-shift) % CH, 0),)

    roll.defvjp(fwd, bwd)
    return roll


_prev_row = _make_row_roll(1)
_next_row = _make_row_roll(-1)


def _mix_tile(guv, ph, state, w_s, b_s_t, ln_w, ln_b, lower, gn_w, consts):
    tri, masks, gmask, lvl = consts
    mt = guv.shape[0]
    ug = guv[:, 0:512]
    vg = guv[:, 512:1024]
    q = ph[:, 0:512]
    fl = ph[:, 512:1024]
    inp = ph[:, 1024:1536]
    g = ph[:, 1536:2048]

    mu = jnp.mean(vg, axis=-1, keepdims=True)
    vc = vg - mu
    var = jnp.mean(vc * vc, axis=-1, keepdims=True)
    vn = vc * lax.rsqrt(var + EPS) * ln_w + ln_b
    wsm = [w_s[h] * gmask for h in range(NH)]
    ya = [[None] * NH for _ in range(mt // BLK)]
    for bi in range(mt // BLK):
        rows = slice(bi * BLK, (bi + 1) * BLK)
        for h in range(NH):
            cols = slice(h * HD, (h + 1) * HD)
            ya[bi][h] = ug[rows, cols] * (_mm_c(wsm[h], vn[rows, cols]) + b_s_t[:, h:h + 1])
    ya_full = jnp.concatenate([jnp.concatenate(r, axis=1) for r in ya], axis=0)

    l0 = lower[0:1, :]
    l1 = lower[1:2, :]
    mx = jnp.maximum(l0, l1)
    e0 = jnp.exp(l0 - mx)
    e1 = jnp.exp(l1 - mx)
    lb = e0 / (e0 + e1)
    qf = _silu(q)
    f = lb + (1.0 - lb) * _sigmoid_rel(fl)
    logf = jnp.log(f)
    kk = 1.0 - f
    gate = _silu(g)
    nl = len(LEVELS)
    half = D // 2
    heads_to_rows = lambda a: jnp.concatenate([a[:, h * HD:(h + 1) * HD] for h in range(NH)], axis=0)
    st = list(state)
    yb = [[None] * NH for _ in range(mt // CH)]
    for ci in range(mt // CH):
        rows = slice(ci * CH, (ci + 1) * CH)
        lc = logf[rows]
        b = _cumsum_mm(tri, lc)
        xb = jnp.exp(b)
        xinv = jnp.exp(b[CH - 1:CH, :] - b)
        xl = []
        for i, hs in enumerate(LEVELS[:3]):
            refs = [jnp.broadcast_to(b[r:r + 1, :], (2 * hs, half)) for r in range(hs - 1, CH, 2 * hs)]
            bref = refs[0] if len(refs) == 1 else jnp.concatenate(refs, axis=0)
            xl.append(jnp.exp(lvl[i] * (b - bref)))
        b3 = b.reshape(CH // 8, 8, half)
        bref = jnp.broadcast_to(b3[:, 3:4, :], (CH // 8, 8, half)).reshape(CH, half)
        xl.append(jnp.exp(lvl[3] * (b - bref)))
        xl.append(jnp.exp(lvl[4] * _next_row(lc) + lvl[5] * lc + lvl[6] * _prev_row(lc)))
        xl.append(jnp.exp(lvl[7] * lc))
        qc = qf[rows]
        kc = kk[rows]
        ic = inp[rows]
        zsrc = [jnp.concatenate([(qc if (r0 // hs) % 2 == 1 else kc)[r0:r0 + hs] for r0 in range(0, CH, hs)], axis=0)
                for hs in LEVELS[:3]]
        qs, ks = heads_to_rows(qc), heads_to_rows(kc)
        qx = qc * xb
        inter = jnp.concatenate([_nt_c(qx[:, h * HD:(h + 1) * HD], st[h]) for h in range(NH)], axis=0)
        attn = masks[nl] * _nt_c(qs, ks)
        for li in range(nl):
            xs = heads_to_rows(xl[li])
            if li < 3:
                z = heads_to_rows(zsrc[li]) * xs
                pairs = _nt_c(z, z)
            else:
                pairs = _nt_c(qs * xs, ks * xs)
            attn = attn + masks[li] * pairs
        o = inter + _mm_c(attn, heads_to_rows(ic))
        kx = kc * xinv
        decay = xb[CH - 1:CH, :]
        st = [st[h] * decay[:, h * HD:(h + 1) * HD] + _tn_c(ic[:, h * HD:(h + 1) * HD], kx[:, h * HD:(h + 1) * HD])
              for h in range(NH)]
        rs = lax.rsqrt(jnp.mean(o * o, axis=-1, keepdims=True) + EPS)
        ys = o * rs * gn_w * heads_to_rows(gate[rows])
        yb[ci] = [ys[h * CH:(h + 1) * CH] for h in range(NH)]
    yb_full = jnp.concatenate([jnp.concatenate(r, axis=1) for r in yb], axis=0)
    return jnp.concatenate([ya_full, yb_full], axis=1), tuple(st)


def _fwd_mix(gelu_uv, proj_h, w_s, b_s_t, ln_w, ln_b, lower, gn_w, tables, to_gather):
    t = proj_h.shape[0]
    sub = _row_tile(t, MIX_TILE)
    n_sub = FWD_MIX_SUBTILES if t % (FWD_MIX_SUBTILES * sub) == 0 else 1
    mt = n_sub * sub
    nt_ = t // mt
    nc = len(tables)
    ng = len(to_gather)

    def body(*refs):
        guv_ref, ph_ref, ws_ref, bs_ref, lw_ref, lb_ref, lo_ref, gn_ref = refs[:8]
        c_refs = refs[8:8 + nc]
        refs = refs[8 + nc:]
        g_in = refs[:ng]
        y_ref, st_ref = refs[ng:2 + ng]
        g_out = refs[2 + ng:2 + 2 * ng]
        state = refs[2 + 2 * ng]
        ga = _Gather2(g_in, g_out, *refs[3 + 2 * ng:])

        @pl.when(pl.program_id(0) == 0)
        def _():
            state[...] = jnp.zeros_like(state)
            ga.start()

        @pl.when(pl.program_id(0) == (3 * nt_) // 4)
        def _():
            ga.forward()

        st = tuple(state[h] for h in range(NH))
        consts = tuple(r[...] for r in c_refs)
        for s in range(n_sub):
            rows = slice(s * sub, (s + 1) * sub)
            for h in range(NH):
                st_ref[s, h] = st[h]
            y, st = _mix_tile(guv_ref[rows, :].astype(F32), ph_ref[rows, :], st, ws_ref[...], bs_ref[...], lw_ref[...],
                              lb_ref[...], lo_ref[...], gn_ref[...], consts)
            y_ref[rows, :] = y.astype(BF)
        for h in range(NH):
            state[h] = st[h]

        @pl.when(pl.program_id(0) == nt_ - 1)
        def _():
            ga.finish()

    full = lambda a: pl.BlockSpec(a.shape, lambda i, nd=a.ndim: (0,) * nd)
    outs = pl.pallas_call(
        body, name="fwd_mix", grid=(nt_,),
        out_shape=(jax.ShapeDtypeStruct((t, D), BF), jax.ShapeDtypeStruct((t // sub, NH, HD, HD), F32),
                   *_exchange_out_shapes(to_gather, False)),
        in_specs=[pl.BlockSpec((mt, D), lambda i: (i, 0)), pl.BlockSpec((mt, D_IN - D), lambda i: (i, 0)), full(w_s), full(b_s_t),
                  full(ln_w), full(ln_b), full(lower), full(gn_w)] + [full(a) for a in tables] + [_any() for _ in to_gather],
        out_specs=(pl.BlockSpec((mt, D), lambda i: (i, 0)), pl.BlockSpec((n_sub, NH, HD, HD), lambda i: (i, 0, 0, 0)),
                   *[_any() for _ in to_gather]),
        scratch_shapes=[pltpu.VMEM((NH, HD, HD), F32)] + _exchange_sems(ng),
        compiler_params=_params(dimension_semantics=("arbitrary",), has_side_effects=True),
    )(gelu_uv, proj_h, w_s, b_s_t, ln_w, ln_b, lower, gn_w, *tables, *to_gather)
    return outs[0], outs[1], outs[2:]


def _bwd_mix(gelu_uv, dgelu_uv, proj_h, dycat, states, w_s, b_s_t, ln_w, ln_b, lower, gn_w, tables, to_scatter):
    t = proj_h.shape[0]
    sub = _row_tile(t, MIX_TILE)
    n_sub = BWD_MIX_SUBTILES if t % (BWD_MIX_SUBTILES * sub) == 0 else 1
    mt = n_sub * sub
    nt_ = t // mt
    nc = len(tables)
    ns = len(to_scatter)

    def body(*refs):
        guv_ref, gd_ref, ph_ref, dy_ref, st_ref, ws_ref, bs_ref, lw_ref, lb_ref, lo_ref, gn_ref = refs[:11]
        c_refs = refs[11:11 + nc]
        refs = refs[11 + nc:]
        s_in = refs[:ns]
        dproj_ref = refs[ns]
        acc = refs[1 + ns:7 + ns]
        s_out = refs[7 + ns:7 + 2 * ns]
        dstate = refs[7 + 2 * ns]
        ex = _Exchange(s_in, s_out, *refs[8 + 2 * ns:], True)

        @pl.when(pl.program_id(0) == 0)
        def _():
            dstate[...] = jnp.zeros_like(dstate)
            for r in acc:
                r[...] = jnp.zeros_like(r)
            ex.start()

        consts = tuple(r[...] for r in c_refs)

        def f(a, p, s, ws, bs, lw, lb_, lo, gn):
            return _mix_tile(a, p, s, ws, bs, lw, lb_, lo, gn, consts)

        dst = tuple(dstate[h] for h in range(NH))
        for s in reversed(range(n_sub)):
            rows = slice(s * sub, (s + 1) * sub)
            st = tuple(st_ref[s, h] for h in range(NH))
            _, vjp = jax.vjp(f, guv_ref[rows, :].astype(F32), ph_ref[rows, :], st, ws_ref[...], bs_ref[...], lw_ref[...],
                             lb_ref[...], lo_ref[...], gn_ref[...])
            grads = vjp((dy_ref[rows, :], dst))
            dproj_ref[rows, 0:D] = (grads[0] * gd_ref[rows, :].astype(F32)).astype(BF)
            dproj_ref[rows, D:D_IN] = grads[1].astype(BF)
            dst = grads[2]
            for r, gval in zip(acc, grads[3:]):
                r[...] += gval
        for h in range(NH):
            dstate[h] = dst[h]

        @pl.when(pl.program_id(0) == nt_ - 1)
        def _():
            ex.wait()

    full = lambda a: pl.BlockSpec(a.shape, lambda i, nd=a.ndim: (0,) * nd)
    rev = lambda i: (nt_ - 1 - i, 0)
    smalls = (w_s, b_s_t, ln_w, ln_b, lower, gn_w)
    outs = pl.pallas_call(
        body, name="bwd_mix", grid=(nt_,),
        out_shape=(jax.ShapeDtypeStruct((t, D_IN), BF), *[jax.ShapeDtypeStruct(a.shape, F32) for a in smalls],
                   *_exchange_out_shapes(to_scatter, True)),
        in_specs=[pl.BlockSpec((mt, D), rev), pl.BlockSpec((mt, D), rev), pl.BlockSpec((mt, D_IN - D), rev),
                  pl.BlockSpec((mt, D), rev), pl.BlockSpec((n_sub, NH, HD, HD), lambda i: (nt_ - 1 - i, 0, 0, 0))]
        + [full(a) for a in smalls] + [full(a) for a in tables] + [_any() for _ in to_scatter],
        out_specs=(pl.BlockSpec((mt, D_IN), rev), *[full(a) for a in smalls], *[_any() for _ in to_scatter]),
        scratch_shapes=[pltpu.VMEM((NH, HD, HD), F32)] + _exchange_sems(ns),
        compiler_params=_params(dimension_semantics=("arbitrary",), has_side_effects=True),
    )(gelu_uv, dgelu_uv, proj_h, dycat, states, w_s, b_s_t, ln_w, ln_b, lower, gn_w, *tables, *to_scatter)
    return outs[0], outs[1:7], outs[7:]


def _row_tile(t, want):
    return want if t % want == 0 else t


def _rms(v):
    rstd = lax.rsqrt(jnp.mean(v * v, axis=-1, keepdims=True) + EPS)
    return v * rstd, rstd


def _rms_bwd(dxhat, xhat, rstd):
    return rstd * (dxhat - xhat * jnp.mean(dxhat * xhat, axis=-1, keepdims=True))


def _colsum(v):
    return jnp.sum(v, axis=0, keepdims=True)


def _fwd_in(x, ada, n1w, w_in_st, to_gather):
    t = x.shape[0]
    tm = _row_tile(t, 512)
    n_t = t // tm
    ng = len(to_gather)

    def body(*refs):
        x_ref, ada_ref, n1_ref, wst_ref = refs[:4]
        g_in = refs[4:4 + ng]
        ph_ref, ga_ref, gd_ref, h1_ref = refs[4 + ng:8 + ng]
        g_out = refs[8 + ng:8 + 2 * ng]
        w_ref, sems = refs[8 + 2 * ng:10 + 2 * ng]
        ga = _Gather2(g_in, g_out, *refs[10 + 2 * ng:])

        @pl.when(pl.program_id(0) == 0)
        def _():
            ga.start()
            _load_columns(wst_ref, w_ref, sems)

        @pl.when(pl.program_id(0) == (3 * n_t) // 4)
        def _():
            ga.forward()

        xh, _ = _rms(x_ref[...])
        h1 = (xh * n1_ref[...]) * (1.0 + ada_ref[1:2, :]) + ada_ref[0:1, :]
        h1b = h1.astype(BF)
        h1_ref[...] = h1b
        uv = _mm(h1b, w_ref[:, 0:D])
        cdf = 0.5 * (1.0 + lax.erf(uv * 0.7071067811865476))
        ga_ref[...] = (uv * cdf).astype(BF)
        gd_ref[...] = (cdf + uv * (jnp.exp(-0.5 * uv * uv) * 0.3989422804014327)).astype(BF)
        ph_ref[...] = _mm(h1b, w_ref[:, D:D_IN])

        @pl.when(pl.program_id(0) == n_t - 1)
        def _():
            ga.finish()

    full = lambda a: pl.BlockSpec(a.shape, lambda i, nd=a.ndim: (0,) * nd)
    row = lambda w: pl.BlockSpec((tm, w), lambda i: (i, 0))
    outs = pl.pallas_call(
        body, name="fwd_in", grid=(n_t,),
        out_shape=(jax.ShapeDtypeStruct((t, D_IN - D), F32), jax.ShapeDtypeStruct((t, D), BF), jax.ShapeDtypeStruct((t, D), BF),
                   jax.ShapeDtypeStruct((t, D), BF), *_exchange_out_shapes(to_gather, False)),
        in_specs=[row(D), full(ada), full(n1w), _any()] + [_any() for _ in to_gather],
        out_specs=(row(D_IN - D), row(D), row(D), row(D), *[_any() for _ in to_gather]),
        scratch_shapes=[pltpu.VMEM((D, D_IN), BF), pltpu.SemaphoreType.DMA((N_DEV,))] + _exchange_sems(ng),
        compiler_params=_params(dimension_semantics=("arbitrary",), has_side_effects=True),
    )(x, ada, n1w, w_in_st, *to_gather)
    return outs[0], outs[1], outs[2], outs[3], outs[4:]


def _fwd_ffn(x, ycat, tgt, ada, n2w, fw, w_out, w_fi, w_fo):
    t = x.shape[0]
    tm = _row_tile(t, 256)
    n_t = t // tm

    def body(x_hbm, y_hbm, t_hbm, ada_ref, n2_ref, fw_ref, wo_ref, wi_ref, wf_ref,
             x1_ref, mix_ref, h2_ref, act_ref, gu_ref, dffn_ref, dx2_ref, part_ref, x_buf, y_buf, t_buf, ring_sems):
        slot = _ring_fetch([x_hbm, y_hbm, t_hbm], [x_buf, y_buf, t_buf], ring_sems, tm, n_t)
        x_ref, y_ref, t_ref = x_buf.at[slot], y_buf.at[slot], t_buf.at[slot]

        @pl.when(pl.program_id(0) == 0)
        def _():
            part_ref[...] = jnp.zeros_like(part_ref)

        g1, sh2, sc2, g2 = ada_ref[2:3, :], ada_ref[3:4, :], ada_ref[4:5, :], ada_ref[5:6, :]
        mix = _mm(y_ref[...], wo_ref[...])
        mix_ref[...] = mix.astype(BF)
        x1 = x_ref[...] + g1 * mix
        x1_ref[...] = x1
        xh2, _ = _rms(x1)
        h2b = ((xh2 * n2_ref[...]) * (1.0 + sc2) + sh2).astype(BF)
        h2_ref[...] = h2b
        ffn = jnp.zeros((tm, D), F32)
        for lo, hi in FF_CHUNKS:
            gate = _nt(h2b, wi_ref[lo:hi, :])
            up = _nt(h2b, wi_ref[D_FF + lo:D_FF + hi, :])
            gu_ref[:, lo:hi] = gate.astype(BF)
            gu_ref[:, D_FF + lo:D_FF + hi] = up.astype(BF)
            a = (_silu(gate) * up).astype(BF)
            act_ref[:, lo:hi] = a
            ffn = ffn + _mm(a, wf_ref[lo:hi, :])
        x2 = x1 + g2 * ffn
        xh3, rstd3 = _rms(x2)
        err = xh3 * fw_ref[...] - t_ref[...]
        dy = err * (1.0 / D)
        dx2 = _rms_bwd(dy * fw_ref[...], xh3, rstd3)
        dx2_ref[...] = dx2
        dffn_ref[...] = (g2 * dx2).astype(BF)
        part_ref[0:1, :] += _colsum(dx2 * ffn)
        part_ref[1:2, :] += _colsum(dy * xh3)
        part_ref[2:3, :] += jnp.zeros((1, D), F32) + (0.5 / D) * jnp.sum(err * err)

    full = lambda a: pl.BlockSpec(a.shape, lambda i, nd=a.ndim: (0,) * nd)
    row = lambda w: pl.BlockSpec((tm, w), lambda i: (i, 0))
    return pl.pallas_call(
        body, name="fwd_ffn", grid=(t // tm,),
        out_shape=(jax.ShapeDtypeStruct((t, D), F32), jax.ShapeDtypeStruct((t, D), BF), jax.ShapeDtypeStruct((t, D), BF),
                   jax.ShapeDtypeStruct((t, D_FF), BF), jax.ShapeDtypeStruct((t, 2 * D_FF), BF),
                   jax.ShapeDtypeStruct((t, D), BF), jax.ShapeDtypeStruct((t, D), F32), jax.ShapeDtypeStruct((8, D), F32)),
        in_specs=[_any(), _any(), _any(), full(ada), full(n2w), full(fw), _whole_vmem(), _whole_vmem(), _whole_vmem()],
        out_specs=(row(D), row(D), row(D), row(D_FF), row(2 * D_FF), row(D), row(D),
                   pl.BlockSpec((8, D), lambda i: (0, 0))),
        scratch_shapes=[pltpu.VMEM((RING, tm, D), F32), pltpu.VMEM((RING, tm, D), BF), pltpu.VMEM((RING, tm, D), F32),
                        pltpu.SemaphoreType.DMA((3, RING))],
        compiler_params=_params(dimension_semantics=("arbitrary",)),
    )(x, ycat, tgt, ada, n2w, fw, w_out, w_fi, w_fo)


def _bwd_ffn(x1, gu, dffn, dx2, mix, ada, n2w, w_out, w_fi, w_fo):
    t = x1.shape[0]
    tm = _row_tile(t, 256)
    n_t = t // tm

    def body(x1_hbm, gu_hbm, dffn_ref, dx2_hbm, mix_ref, ada_ref, n2_ref, wo_ref, wi_ref, wf_ref,
             dgu_ref, dx1_ref, dmix_ref, dycat_ref, part_ref, x1_buf, gu_buf, dx2_buf, ring_sems):
        slot = _ring_fetch([x1_hbm, gu_hbm, dx2_hbm], [x1_buf, gu_buf, dx2_buf], ring_sems, tm, n_t)
        x1_ref, gu_ref, dx2_ref = x1_buf.at[slot], gu_buf.at[slot], dx2_buf.at[slot]

        @pl.when(pl.program_id(0) == 0)
        def _():
            part_ref[...] = jnp.zeros_like(part_ref)

        g1, sc2 = ada_ref[2:3, :], ada_ref[4:5, :]
        dffn = dffn_ref[...]
        dh2 = jnp.zeros((tm, D), F32)
        for lo, hi in FF_CHUNKS:
            gate = gu_ref[:, lo:hi].astype(F32)
            up = gu_ref[:, D_FF + lo:D_FF + hi].astype(F32)
            dact = _nt(dffn, wf_ref[lo:hi, :])
            sg = _sig(gate)
            dgate = (dact * up * (sg * (1.0 + gate * (1.0 - sg)))).astype(BF)
            dup = (dact * (gate * sg)).astype(BF)
            dgu_ref[:, lo:hi] = dgate
            dgu_ref[:, D_FF + lo:D_FF + hi] = dup
            dh2 = dh2 + _mm(dgate, wi_ref[lo:hi, :]) + _mm(dup, wi_ref[D_FF + lo:D_FF + hi, :])
        x1 = x1_ref[...]
        xh2, rstd2 = _rms(x1)
        xn2 = xh2 * n2_ref[...]
        dxn2 = dh2 * (1.0 + sc2)
        dx1 = dx2_ref[...] + _rms_bwd(dxn2 * n2_ref[...], xh2, rstd2)
        dx1_ref[...] = dx1
        dmix = (g1 * dx1).astype(BF)
        dmix_ref[...] = dmix
        dycat_ref[...] = _nt(dmix, wo_ref[...])
        part_ref[0:1, :] += _colsum(dh2)
        part_ref[1:2, :] += _colsum(dh2 * xn2)
        part_ref[2:3, :] += _colsum(dxn2 * xh2)
        part_ref[3:4, :] += _colsum(dx1 * mix_ref[...].astype(F32))

    full = lambda a: pl.BlockSpec(a.shape, lambda i, nd=a.ndim: (0,) * nd)
    row = lambda w: pl.BlockSpec((tm, w), lambda i: (i, 0))
    return pl.pallas_call(
        body, name="bwd_ffn", grid=(t // tm,),
        out_shape=(jax.ShapeDtypeStruct((t, 2 * D_FF), BF), jax.ShapeDtypeStruct((t, D), F32),
                   jax.ShapeDtypeStruct((t, D), BF), jax.ShapeDtypeStruct((t, D), F32), jax.ShapeDtypeStruct((8, D), F32)),
        in_specs=[_any(), _any(), row(D), _any(), row(D), full(ada), full(n2w),
                  _whole_vmem(), _whole_vmem(), _whole_vmem()],
        out_specs=(row(2 * D_FF), row(D), row(D), row(D), pl.BlockSpec((8, D), lambda i: (0, 0))),
        scratch_shapes=[pltpu.VMEM((RING, tm, D), F32), pltpu.VMEM((RING, tm, 2 * D_FF), BF), pltpu.VMEM((RING, tm, D), F32),
                        pltpu.SemaphoreType.DMA((3, RING))],
        compiler_params=_params(dimension_semantics=("arbitrary",)),
    )(x1, gu, dffn, dx2, mix, ada, n2w, w_out, w_fi, w_fo)


def _bwd_in(x, dproj, dx1, ada, n1w, w_in_st, to_scatter, to_gather):
    t = x.shape[0]
    tm = _row_tile(t, 1024)
    n_t = t // tm
    ns = len(to_scatter)
    ng = len(to_gather)
    nx = ns + ng

    def body(*refs):
        x_ref, dp_ref, dx1_ref, ada_ref, n1_ref, wst_ref = refs[:6]
        x_in = refs[6:6 + nx]
        gx_ref, part_ref = refs[6 + nx:8 + nx]
        x_out = refs[8 + nx:8 + 2 * nx]
        w_ref, sems = refs[8 + 2 * nx:10 + 2 * nx]
        sem_refs = refs[10 + 2 * nx:]
        ex = _Exchange(x_in[:ns], x_out[:ns], *sem_refs[:3], True)
        gx = _Exchange(x_in[ns:], x_out[ns:], *sem_refs[3:], False) if ng else None

        @pl.when(pl.program_id(0) == 0)
        def _():
            ex.start()
            if ng:
                gx.start()
            part_ref[...] = jnp.zeros_like(part_ref)
            _load_columns(wst_ref, w_ref, sems)

        dh1 = _nt(dp_ref[...], w_ref[...])
        xh, rstd = _rms(x_ref[...])
        xn = xh * n1_ref[...]
        dxn = dh1 * (1.0 + ada_ref[1:2, :])
        gx_ref[...] = dx1_ref[...] + _rms_bwd(dxn * n1_ref[...], xh, rstd)
        part_ref[0:1, :] += _colsum(dh1)
        part_ref[1:2, :] += _colsum(dh1 * xn)
        part_ref[2:3, :] += _colsum(dxn * xh)

        @pl.when(pl.program_id(0) == n_t - 1)
        def _():
            ex.wait()
            if ng:
                gx.wait()

    full = lambda a: pl.BlockSpec(a.shape, lambda i, nd=a.ndim: (0,) * nd)
    row = lambda w: pl.BlockSpec((tm, w), lambda i: (i, 0))
    outs = pl.pallas_call(
        body, name="bwd_in", grid=(n_t,),
        out_shape=(jax.ShapeDtypeStruct((t, D), F32), jax.ShapeDtypeStruct((8, D), F32),
                   *_exchange_out_shapes(to_scatter, True), *_exchange_out_shapes(to_gather, False)),
        in_specs=[row(D), row(D_IN), row(D), full(ada), full(n1w), _any()] + [_any() for _ in range(nx)],
        out_specs=(row(D), pl.BlockSpec((8, D), lambda i: (0, 0)), *[_any() for _ in range(nx)]),
        scratch_shapes=[pltpu.VMEM((D, D_IN), BF), pltpu.SemaphoreType.DMA((N_DEV,))] + _exchange_sems(ns)
        + (_exchange_sems(ng) if ng else []),
        compiler_params=_params(dimension_semantics=("arbitrary",), has_side_effects=True),
    )(x, dproj, dx1, ada, n1w, w_in_st, *to_scatter, *to_gather)
    return outs[0], outs[1], outs[2:2 + ns], outs[2 + ns:]


def _wgrad(a, b, name, a_spec, b_spec, out_shape, out_spec, grid, acc_shape, split=1, to_gather=()):
    n_j, n_t = grid
    ng = len(to_gather)

    def body(*refs):
        a_ref, b_ref = refs[:2]
        g_in = refs[2:2 + ng]
        o_ref = refs[2 + ng]
        g_out = refs[3 + ng:3 + 2 * ng]
        acc = refs[3 + 2 * ng]
        first = (pl.program_id(0) == 0) & (pl.program_id(1) == 0)
        last = (pl.program_id(0) == n_j - 1) & (pl.program_id(1) == n_t - 1)
        if ng:
            gx = _Exchange(g_in, g_out, *refs[4 + 2 * ng:], False)

            @pl.when(first)
            def _():
                gx.start()

        @pl.when(pl.program_id(1) == 0)
        def _():
            acc[...] = jnp.zeros_like(acc)

        acc[...] += _tn(a_ref[...], b_ref[...])

        @pl.when(pl.program_id(1) == n_t - 1)
        def _():
            if split == 1:
                o_ref[...] = acc[...].astype(BF)
            else:
                w = acc_shape[1] // split
                for s in range(split):
                    o_ref[s] = acc[:, s * w:(s + 1) * w].astype(BF)

        if ng:
            @pl.when(last)
            def _():
                gx.wait()

    outs = pl.pallas_call(
        body, name=name, grid=grid,
        out_shape=(jax.ShapeDtypeStruct(out_shape, BF), *_exchange_out_shapes(to_gather, False)),
        in_specs=[a_spec, b_spec] + [_any() for _ in to_gather], out_specs=(out_spec, *[_any() for _ in to_gather]),
        scratch_shapes=[pltpu.VMEM(acc_shape, F32)] + (_exchange_sems(ng) if ng else []),
        compiler_params=_params(dimension_semantics=("arbitrary", "arbitrary"), has_side_effects=bool(ng)),
    )(a, b, *to_gather)
    return (outs[0], outs[1:]) if ng else outs[0]


def _adamw_math(w, g, m, v):
    m = ADAM_B1 * m + (1.0 - ADAM_B1) * g
    v = ADAM_B2 * v + (1.0 - ADAM_B2) * (g * g)
    m_hat = m / (1.0 - ADAM_B1 ** ADAM_STEP)
    v_hat = v / (1.0 - ADAM_B2 ** ADAM_STEP)
    delta = -ADAM_LR * (m_hat / (jnp.sqrt(v_hat) + ADAM_EPS) + ADAM_WD * w)
    return delta, m, v


def _adamw_recv(w, m, v, recv, name, tr):
    r, c = w.shape

    def body(w_ref, m_ref, v_ref, r_ref, g_ref, d_ref, nm_ref, nv_ref):
        g = r_ref[0].astype(F32)
        for k in range(1, N_DEV):
            g = g + r_ref[k].astype(F32)
        g_ref[...] = g
        d_ref[...], nm_ref[...], nv_ref[...] = _adamw_math(w_ref[...], g, m_ref[...], v_ref[...])

    row = pl.BlockSpec((tr, c), lambda i: (i, 0))
    return pl.pallas_call(
        body, name=name, grid=(r // tr,),
        out_shape=tuple(jax.ShapeDtypeStruct((r, c), F32) for _ in range(4)),
        in_specs=[row, row, row, pl.BlockSpec((N_DEV, tr, c), lambda i: (0, i, 0))],
        out_specs=(row, row, row, row),
        compiler_params=_params(dimension_semantics=("arbitrary",)),
    )(w, m, v, recv)


def _adamw_ada(w, m, v, cact, dada_cols):
    r, c = w.shape
    tr = 256

    def body(w_ref, m_ref, v_ref, ca_ref, da_ref, g_ref, d_ref, nm_ref, nv_ref):
        g = _tn(ca_ref[...].astype(BF), da_ref[...].astype(BF))
        g_ref[...] = g
        d_ref[...], nm_ref[...], nv_ref[...] = _adamw_math(w_ref[...], g, m_ref[...], v_ref[...])

    row = pl.BlockSpec((tr, c), lambda i: (i, 0))
    return pl.pallas_call(
        body, name="adamw_ada", grid=(r // tr,),
        out_shape=tuple(jax.ShapeDtypeStruct((r, c), F32) for _ in range(4)),
        in_specs=[row, row, row, pl.BlockSpec((N_DEV, tr), lambda i: (0, i)), pl.BlockSpec(dada_cols.shape, lambda i: (0, 0))],
        out_specs=(row, row, row, row),
        compiler_params=_params(dimension_semantics=("arbitrary",)),
    )(w, m, v, cact, dada_cols)


def _adamw_small(gathered, wmv):
    n_g = len(gathered)
    n_p = len(wmv)
    flat = [a for trip in wmv for a in trip]

    def body(*refs):
        g_refs = refs[:n_g]
        p_refs = refs[n_g:n_g + 3 * n_p]
        o_refs = refs[n_g + 3 * n_p:]

        def total(ref):
            s = ref[0]
            for k in range(1, N_DEV):
                s = s + ref[k]
            return s

        f3, b3, b1, dws, dbs, dlnw, dlnb, dlo, dgn = [total(r) for r in g_refs]
        dada_rows = [b1[0:1], b1[1:2], b3[3:4], b3[0:1], b3[1:2], f3[0:1]]
        for r, g in enumerate(dada_rows):
            cs = slice(r * D, (r + 1) * D)
            w, m, v = p_refs[0][:, cs], p_refs[1][:, cs], p_refs[2][:, cs]
            o_refs[0][:, cs] = g
            o_refs[1][:, cs], o_refs[2][:, cs], o_refs[3][:, cs] = _adamw_math(w, g, m, v)
        grads = [None, b1[2:3], dws, dbs, dlnw, dlnb, dlo, dgn, b3[2:3], f3[1:2]]
        for i, g in enumerate(grads):
            if g is None:
                continue
            w, m, v = p_refs[3 * i][...], p_refs[3 * i + 1][...], p_refs[3 * i + 2][...]
            o_refs[4 * i][...] = g
            o_refs[4 * i + 1][...], o_refs[4 * i + 2][...], o_refs[4 * i + 3][...] = _adamw_math(w, g, m, v)
        o_refs[4 * n_p][...] = jnp.zeros((8, 128), F32) + f3[2:3, 0:128]

    out_shape = []
    for w, _, _ in wmv:
        out_shape += [jax.ShapeDtypeStruct(w.shape, F32)] * 4
    out_shape.append(jax.ShapeDtypeStruct((8, 128), F32))
    n_in = n_g + 3 * n_p
    return pl.pallas_call(
        body, name="adamw_small",
        out_shape=tuple(out_shape),
        in_specs=[_whole_vmem()] * n_in, out_specs=tuple(_whole_vmem() for _ in out_shape),
        compiler_params=_params(),
    )(*gathered, *flat)


def kernel(x, c, w_ada, b_ada, norm1_w, w_in, w_s, b_s, v_ln_w, v_ln_b, lower_bounds, gn_w, w_out, norm2_w, w_ffn_in, w_ffn_out, final_norm_w, loss_target, m_w_ada, m_b_ada, m_norm1_w, m_w_in, m_w_s, m_b_s, m_v_ln_w, m_v_ln_b, m_lower_bounds, m_gn_w, m_w_out, m_norm2_w, m_w_ffn_in, m_w_ffn_out, m_final_norm_w, v_w_ada, v_b_ada, v_norm1_w, v_w_in, v_w_s, v_b_s, v_v_ln_w, v_v_ln_b, v_lower_bounds, v_gn_w, v_w_out, v_norm2_w, v_w_ffn_in, v_w_ffn_out, v_final_norm_w):
    me = 4 * lax.axis_index("x") + 2 * lax.axis_index("y") + lax.axis_index("c")
    t = x.shape[1]
    x2d = x.reshape(t, D)
    tgt = loss_target.reshape(t, D)
    ada_cols = w_ada.shape[2]

    tp = lambda a: jnp.swapaxes(a[0], 0, 1)
    win_b, wout_b, wfi_b, wfo_b = _cast_bf16([w_in[0], w_out[0], tp(w_ffn_in), w_ffn_out[0]], "cast_weights")

    b_cols = lax.dynamic_slice(b_ada, (0, me * ada_cols), (1, ada_cols))
    win_st, ada_st, cact = _startup(win_b, c, w_ada[0], b_cols)
    ada = lax.dynamic_index_in_dim(ada_st, me, axis=1, keepdims=False).reshape(6, D)

    tables = _decay_tables()
    ws3 = w_s[0]
    bs_t = b_s[0].T

    proj_h, gelu_uv, dgelu_uv, h1, (wout_st, wfo_st) = _fwd_in(x2d, ada, norm1_w, win_st, [wout_b, wfo_b])
    ycat, states, (wfi_st,) = _fwd_mix(gelu_uv, proj_h, ws3, bs_t, v_ln_w, v_ln_b, lower_bounds, gn_w, tables, [wfi_b])
    w_out_full = wout_st.reshape(D, D)
    w_fo_full = wfo_st.reshape(D_FF, D)
    w_fi_full = wfi_st.reshape(2 * D_FF, D)
    x1, mixb, h2, act, gu, dffn, dx2, part_f = _fwd_ffn(x2d, ycat, tgt, ada, norm2_w, final_norm_w.reshape(1, D),
                                                        w_out_full, w_fi_full, w_fo_full)
    dgu, dx1, dmix, dycat, part_b3 = _bwd_ffn(x1, gu, dffn, dx2, mixb, ada, norm2_w, w_out_full, w_fi_full, w_fo_full)
    tk = _row_tile(t, 2048)
    n_t = t // tk
    win_cols = D_IN // N_DEV
    dwout = _wgrad(ycat, dmix, "wgrad_out",
                   pl.BlockSpec((tk, D), lambda j, i: (i, 0)), pl.BlockSpec((tk, D), lambda j, i: (i, 0)),
                   (D, D), pl.BlockSpec((D, D), lambda j, i: (0, 0)), (1, n_t), (D, D))
    dwfi = _wgrad(dgu, h2, "wgrad_ffn_in",
                  pl.BlockSpec((tk, FF_PAIR), lambda j, i: (i, j)), pl.BlockSpec((tk, D), lambda j, i: (i, 0)),
                  (4, FF_PAIR, D), pl.BlockSpec((None, FF_PAIR, D), lambda j, i: (j, 0, 0)), (4, n_t), (FF_PAIR, D))
    dwfo = _wgrad(act, dffn, "wgrad_ffn_out",
                  pl.BlockSpec((tk, FF_PAIR), lambda j, i: (i, j)), pl.BlockSpec((tk, D), lambda j, i: (i, 0)),
                  (2, FF_PAIR, D), pl.BlockSpec((None, FF_PAIR, D), lambda j, i: (j, 0, 0)), (2, n_t), (FF_PAIR, D))
    dproj, (dws, dbs_t, dlnw, dlnb, dlower, dgnw), (r_out, r_fi, r_fo) = _bwd_mix(
        gelu_uv, dgelu_uv, proj_h, dycat, states, ws3, bs_t, v_ln_w, v_ln_b, lower_bounds, gn_w, tables,
        [dwout.reshape(N_DEV, D // N_DEV, D), dwfi.reshape(N_DEV, FF_BLK, D), dwfo.reshape(N_DEV, D_FF // N_DEV, D)])
    dwin, early = _wgrad(h1, dproj, "wgrad_in",
                         pl.BlockSpec((tk, D), lambda j, i: (i, 0)), pl.BlockSpec((tk, 4 * win_cols), lambda j, i: (i, j)),
                         (N_DEV, D, win_cols), pl.BlockSpec((4, D, win_cols), lambda j, i: (j, 0, 0)), (N_DEV // 4, n_t),
                         (D, 4 * win_cols), split=4, to_gather=[part_f, part_b3, dws, dbs_t, dlnw, dlnb, dlower, dgnw])
    grad_x, part_b1, (r_in,), _ = _bwd_in(x2d, dproj, dx1, ada, norm1_w, win_st, [dwin], [])

    g_w_in, d_w_in, nm_w_in, nv_w_in = _adamw_recv(w_in[0], m_w_in[0], v_w_in[0], r_in, "adamw_w_in", 256)
    g_w_out, d_w_out, nm_w_out, nv_w_out = _adamw_recv(w_out[0], m_w_out[0], v_w_out[0], r_out, "adamw_w_out", 128)
    fi_t = _adamw_recv(tp(w_ffn_in), tp(m_w_ffn_in), tp(v_w_ffn_in), r_fi, "adamw_w_ffn_in", 176)
    g_w_fi, d_w_fi, nm_w_fi, nv_w_fi = [jnp.swapaxes(a, 0, 1) for a in fi_t]
    g_w_fo, d_w_fo, nm_w_fo, nv_w_fo = _adamw_recv(w_ffn_out[0], m_w_ffn_out[0], v_w_ffn_out[0], r_fo, "adamw_w_ffn_out", 176)

    (b1_all,) = _exchange([part_b1], "gather_small", False, True)
    gathered = [early[0], early[1], b1_all, *early[2:]]
    f3_all, b3_all = gathered[0], gathered[1]
    dada_all = jnp.stack([b1_all[:, 0], b1_all[:, 1], b3_all[:, 3], b3_all[:, 0], b3_all[:, 1], f3_all[:, 0]], axis=1)
    dada_cols = lax.dynamic_slice(dada_all.reshape(N_DEV, 6 * D), (0, me * ada_cols), (N_DEV, ada_cols))
    g_w_ada, d_w_ada, nm_w_ada, nv_w_ada = _adamw_ada(w_ada[0], m_w_ada[0], v_w_ada[0], cact, dada_cols)

    r1 = lambda a: a.reshape(1, D)
    tr = lambda a: a[0].T
    wmv = [
        (b_ada, m_b_ada, v_b_ada),
        (norm1_w, m_norm1_w, v_norm1_w),
        (w_s[0], m_w_s[0], v_w_s[0]),
        (tr(b_s), tr(m_b_s), tr(v_b_s)),
        (v_ln_w, m_v_ln_w, v_v_ln_w),
        (v_ln_b, m_v_ln_b, v_v_ln_b),
        (lower_bounds, m_lower_bounds, v_lower_bounds),
        (gn_w, m_gn_w, v_gn_w),
        (norm2_w, m_norm2_w, v_norm2_w),
        (r1(final_norm_w), r1(m_final_norm_w), r1(v_final_norm_w)),
    ]
    small = _adamw_small(gathered, wmv)
    loss = small[-1][0, 0]

    def unshape(i, a):
        if i == 2:
            return a.reshape(1, NH, BLK, BLK)
        if i == 3:
            return a.T.reshape(1, NH, BLK)
        if i == 9:
            return a.reshape(D)
        return a

    def small_out(kind):
        return [unshape(i, small[4 * i + kind]) for i in range(len(wmv))]

    e3 = lambda a: a[None]
    big = {
        0: (e3(g_w_ada), e3(g_w_in), e3(g_w_out), e3(g_w_fi), e3(g_w_fo)),
        1: (e3(d_w_ada), e3(d_w_in), e3(d_w_out), e3(d_w_fi), e3(d_w_fo)),
        2: (e3(nm_w_ada), e3(nm_w_in), e3(nm_w_out), e3(nm_w_fi), e3(nm_w_fo)),
        3: (e3(nv_w_ada), e3(nv_w_in), e3(nv_w_out), e3(nv_w_fi), e3(nv_w_fo)),
    }

    def ordered(kind):
        s = small_out(kind)
        b_ = big[kind]
        return [b_[0], s[0], s[1], b_[1], s[2], s[3], s[4], s[5], s[6], s[7], b_[2], s[8], b_[3], b_[4], s[9]]

    return (loss, grad_x.reshape(1, t, D), *ordered(0), *ordered(1), *ordered(2), *ordered(3))
```

```python
import numpy as np
import jax
import jax.numpy as jnp
from jax import lax
from jax.experimental import pallas as pl
from jax.experimental.pallas import tpu as pltpu

F32 = jnp.float32
BF = jnp.bfloat16
MESH = pl.DeviceIdType.MESH

N_DEV = 8
D = 1024
D_IN = 3072
D_FF = 2816
FF_BLK = D_FF // 4
FF_CHUNKS = ((0, D_FF),)
FF_PAIR = 2 * FF_BLK
CH = 64
BLK = 128
MIX_TILE = 256
FWD_MIX_SUBTILES = 4
BWD_MIX_SUBTILES = 2
NH = 4
HD = 128
EPS = 1e-6
LEVELS = (32, 16, 8, 4, 2, 1)

ADAM_LR = 0.001
ADAM_B1 = 0.9
ADAM_B2 = 0.999
ADAM_EPS = 1e-08
ADAM_WD = 0.01
ADAM_STEP = 10

VMEM_LIMIT = 56 * 1024 * 1024


def _params(**kw):
    return pltpu.CompilerParams(vmem_limit_bytes=VMEM_LIMIT, **kw)


def _whole_vmem():
    return pl.BlockSpec(memory_space=pltpu.VMEM)


def _any():
    return pl.BlockSpec(memory_space=pl.ANY)


class _Exchange:
    def __init__(self, ins, outs, send_sems, recv_sems, local_sems, scatter):
        self.ins, self.outs, self.scatter = ins, outs, scatter
        self.send_sems, self.recv_sems, self.local_sems = send_sems, recv_sems, local_sems
        x, y, c = lax.axis_index("x"), lax.axis_index("y"), lax.axis_index("c")
        self.me = 4 * x + 2 * y + c
        self.peers = []
        for k in range(1, N_DEV):
            peer = (1 - x if (k >> 2) & 1 else x, 1 - y if (k >> 1) & 1 else y, 1 - c if k & 1 else c)
            self.peers.append((peer, 4 * peer[0] + 2 * peer[1] + peer[2]))

    def _src(self, a, idx):
        return self.ins[a].at[idx] if self.scatter else self.ins[a]

    def _local(self):
        return [pltpu.make_async_copy(self._src(a, self.me), self.outs[a].at[self.me], self.local_sems.at[a])
                for a in range(len(self.ins))]

    def _remote(self, a, k, dst_slot):
        peer, peer_idx = self.peers[k]
        return pltpu.make_async_remote_copy(
            src_ref=self._src(a, peer_idx), dst_ref=self.outs[a].at[dst_slot],
            send_sem=self.send_sems.at[a, k], recv_sem=self.recv_sems.at[a, k],
            device_id=peer, device_id_type=MESH)

    def start(self):
        for cp in self._local():
            cp.start()
        for k in range(N_DEV - 1):
            for a in range(len(self.ins)):
                self._remote(a, k, self.me).start()

    def wait(self):
        for k in range(N_DEV - 1):
            for a in range(len(self.ins)):
                self._remote(a, k, self.peers[k][1]).wait()
        for cp in self._local():
            cp.wait()


class _Gather2:
    def __init__(self, ins, outs, send_sems, recv_sems, local_sems):
        self.ins, self.outs = ins, outs
        self.send_sems, self.recv_sems, self.local_sems = send_sems, recv_sems, local_sems
        x, y, c = lax.axis_index("x"), lax.axis_index("y"), lax.axis_index("c")
        self.c = c
        self.me = 4 * x + 2 * y + c
        self.sibling = (x, y, 1 - c)
        self.chips = [(1 - x, y), (x, 1 - y), (1 - x, 1 - y)]

    @staticmethod
    def _idx(px, py, pc):
        return 4 * px + 2 * py + pc

    def _copy(self, a, k, slot, to, own):
        src = self.ins[a] if own else self.outs[a].at[slot]
        return pltpu.make_async_remote_copy(
            src_ref=src, dst_ref=self.outs[a].at[slot],
            send_sem=self.send_sems.at[a, k], recv_sem=self.recv_sems.at[a, k],
            device_id=to, device_id_type=MESH)

    def _local(self):
        return [pltpu.make_async_copy(self.ins[a], self.outs[a].at[self.me], self.local_sems.at[a])
                for a in range(len(self.ins))]

    def start(self):
        for cp in self._local():
            cp.start()
        for a in range(len(self.ins)):
            self._copy(a, 0, self.me, self.sibling, True).start()
            for j, chip in enumerate(self.chips):
                self._copy(a, 1 + j, self.me, (*chip, self.c), True).start()

    def forward(self):
        for j, chip in enumerate(self.chips):
            for a in range(len(self.ins)):
                slot = self._idx(*chip, self.c)
                self._copy(a, 1 + j, slot, (*chip, self.c), True).wait_recv()
                self._copy(a, 4 + j, slot, self.sibling, False).start()

    def finish(self):
        for a in range(len(self.ins)):
            self._copy(a, 0, self._idx(*self.sibling), self.sibling, True).wait_recv()
            for j, chip in enumerate(self.chips):
                self._copy(a, 4 + j, self._idx(*chip, 1 - self.c), self.sibling, False).wait_recv()
            self._copy(a, 0, self.me, self.sibling, True).wait_send()
            for j, chip in enumerate(self.chips):
                self._copy(a, 1 + j, self.me, (*chip, self.c), True).wait_send()
                self._copy(a, 4 + j, self._idx(*chip, self.c), self.sibling, False).wait_send()
        for cp in self._local():
            cp.wait()


def _exchange_sems(n):
    return [pltpu.SemaphoreType.DMA((n, N_DEV - 1)), pltpu.SemaphoreType.DMA((n, N_DEV - 1)), pltpu.SemaphoreType.DMA((n,))]


def _exchange_out_shapes(arrays, scatter):
    return [jax.ShapeDtypeStruct(a.shape if scatter else (N_DEV,) + a.shape, a.dtype) for a in arrays]


def _exchange(arrays, name, scatter, in_vmem):
    n = len(arrays)

    def body(*refs):
        ex = _Exchange(refs[:n], refs[n:2 * n], *refs[2 * n:], scatter)
        ex.start()
        ex.wait()

    spec = _whole_vmem if in_vmem else _any
    return pl.pallas_call(
        body, name=name,
        out_shape=tuple(_exchange_out_shapes(arrays, scatter)),
        in_specs=[spec() for _ in arrays],
        out_specs=tuple(spec() for _ in arrays),
        scratch_shapes=_exchange_sems(n),
        compiler_params=_params(has_side_effects=True),
    )(*arrays)


def _cast_bf16(arrays, name):
    n = len(arrays)

    def body(*refs):
        for i in range(n):
            refs[n + i][...] = refs[i][...].astype(BF)

    return pl.pallas_call(
        body, name=name,
        out_shape=tuple(jax.ShapeDtypeStruct(a.shape, BF) for a in arrays),
        in_specs=[_whole_vmem() for _ in arrays],
        out_specs=tuple(_whole_vmem() for _ in arrays),
        compiler_params=_params(),
    )(*arrays)


def _load_columns(stacked_ref, full_ref, sems):
    c = stacked_ref.shape[2]
    cps = [pltpu.make_async_copy(stacked_ref.at[d], full_ref.at[:, pl.ds(d * c, c)], sems.at[d]) for d in range(N_DEV)]
    for cp in cps:
        cp.start()
    for cp in cps:
        cp.wait()


def _sig(v):
    return 0.5 * jnp.tanh(0.5 * v) + 0.5


@jax.custom_vjp
def _silu(v):
    return v * _sig(v)


def _silu_fwd(v):
    return _silu(v), v


def _silu_bwd(v, g):
    s = _sig(v)
    return (g * (s * (1.0 + v * (1.0 - s))),)


_silu.defvjp(_silu_fwd, _silu_bwd)


@jax.custom_vjp
def _sigmoid_rel(v):
    e = jnp.exp(-jnp.abs(v))
    d = 1.0 + e
    r = pl.reciprocal(d, approx=True)
    r = r * (2.0 - d * r)
    r = r * (2.0 - d * r)
    return jnp.where(v >= 0.0, r, e * r)


def _sigmoid_rel_fwd(v):
    s = _sigmoid_rel(v)
    return s, s


def _sigmoid_rel_bwd(s, g):
    return (g * (s * (1.0 - s)),)


_sigmoid_rel.defvjp(_sigmoid_rel_fwd, _sigmoid_rel_bwd)


def _dot(a, b, ca, cb):
    return lax.dot_general(a, b, (((ca,), (cb,)), ((), ())), preferred_element_type=F32)


@jax.custom_vjp
def _mm(a, b):
    return _dot(a, b, 1, 0)


def _mm_fwd(a, b):
    return _dot(a, b, 1, 0), (a, b)


def _mm_bwd(res, g):
    a, b = res
    gb = g.astype(BF)
    return _dot(gb, b, 1, 1).astype(a.dtype), _dot(a, gb, 0, 0).astype(b.dtype)


_mm.defvjp(_mm_fwd, _mm_bwd)


@jax.custom_vjp
def _nt(a, b):
    return _dot(a, b, 1, 1)


def _nt_fwd(a, b):
    return _dot(a, b, 1, 1), (a, b)


def _nt_bwd(res, g):
    a, b = res
    gb = g.astype(BF)
    return _dot(gb, b, 1, 0).astype(a.dtype), _dot(gb, a, 0, 0).astype(b.dtype)


_nt.defvjp(_nt_fwd, _nt_bwd)


@jax.custom_vjp
def _tn(a, b):
    return _dot(a, b, 0, 0)


def _tn_fwd(a, b):
    return _dot(a, b, 0, 0), (a, b)


def _tn_bwd(res, g):
    a, b = res
    gb = g.astype(BF)
    return _dot(b, gb, 1, 1).astype(a.dtype), _dot(a, gb, 1, 0).astype(b.dtype)


_tn.defvjp(_tn_fwd, _tn_bwd)


def _make_cast_dot(ca, cb, da_dims, db_dims):
    @jax.custom_vjp
    def dot(a, b):
        return _dot(a.astype(BF), b.astype(BF), ca, cb)

    def fwd(a, b):
        ab, bb = a.astype(BF), b.astype(BF)
        return _dot(ab, bb, ca, cb), (ab, bb)

    def bwd(res, g):
        ops = {"a": res[0], "b": res[1], "g": g.astype(BF)}
        return (_dot(ops[da_dims[0]], ops[da_dims[1]], da_dims[2], da_dims[3]),
                _dot(ops[db_dims[0]], ops[db_dims[1]], db_dims[2], db_dims[3]))

    dot.defvjp(fwd, bwd)
    return dot


_mm_c = _make_cast_dot(1, 0, ("g", "b", 1, 1), ("a", "g", 0, 0))
_nt_c = _make_cast_dot(1, 1, ("g", "b", 1, 0), ("g", "a", 0, 0))
_tn_c = _make_cast_dot(0, 0, ("b", "g", 1, 1), ("a", "g", 1, 0))


def _startup(win_b, c, w_ada, b_cols):
    ncol = w_ada.shape[1]

    def body(win_ref, c_ref, w_ref, b_ref, winst_ref, adast_ref, cact_ref, call_ref, blk_ref,
             ws, wr, wl, cs, cr, cl, as_, ar, al):
        big = _Gather2([win_ref], [winst_ref], ws, wr, wl)
        big.start()
        gc = _Exchange([c_ref], [call_ref], cs, cr, cl, False)
        gc.start()
        gc.wait()
        ca = _silu(call_ref[...].reshape(N_DEV, D))
        cact_ref[...] = ca
        blk_ref[...] = _mm(ca.astype(BF), w_ref[...].astype(BF)) + b_ref[...]
        ga = _Exchange([blk_ref], [adast_ref], as_, ar, al, False)
        ga.start()
        ga.wait()
        big.forward()
        big.finish()

    return pl.pallas_call(
        body, name="startup",
        out_shape=(jax.ShapeDtypeStruct((N_DEV,) + win_b.shape, BF), jax.ShapeDtypeStruct((N_DEV, N_DEV, ncol), F32),
                   jax.ShapeDtypeStruct((N_DEV, D), F32)),
        in_specs=[_any(), _whole_vmem(), _whole_vmem(), _whole_vmem()],
        out_specs=(_any(), _whole_vmem(), _whole_vmem()),
        scratch_shapes=[pltpu.VMEM((N_DEV, 1, D), F32), pltpu.VMEM((N_DEV, ncol), F32)]
        + _exchange_sems(1) + _exchange_sems(1) + _exchange_sems(1),
        compiler_params=_params(has_side_effects=True),
    )(win_b, c, w_ada, b_cols)


def _decay_tables():
    t = np.arange(CH)
    tri = (t[None, :] <= t[:, None]).astype(np.float32)
    masks = []
    for h in LEVELS:
        m = (t // (2 * h)) * (2 * h) + h
        upper = t >= m
        same = (t[:, None] // (2 * h)) == (t[None, :] // (2 * h))
        masks.append(same & upper[:, None] & (~upper)[None, :])
    masks.append(np.eye(CH, dtype=bool))
    lv = [np.where((t % (2 * h)) >= h, 1.0, -1.0) for h in LEVELS[:4]]
    m4 = t % 4
    lv += [(m4 == 0) * 1.0, (m4 >= 2) * 1.0, (m4 == 3) * 1.0, (t % 2 == 1) * 1.0]
    lvl = np.broadcast_to(np.stack(lv)[:, :, None], (8, CH, D // 2)).astype(np.float32)
    cid = np.arange(BLK) // CH
    gmask = (cid[:, None] >= cid[None, :]).astype(np.float32)
    masks = np.stack(masks).astype(np.float32)
    stacked = np.zeros((masks.shape[0], NH * CH, NH * CH), np.float32)
    for h in range(NH):
        stacked[:, h * CH:(h + 1) * CH, h * CH:(h + 1) * CH] = masks
    return (jnp.asarray(tri, BF), jnp.asarray(stacked), jnp.asarray(gmask), jnp.asarray(lvl))


def _split2(v):
    v1 = v.astype(BF)
    return v1, (v - v1.astype(F32)).astype(BF)


@jax.custom_vjp
def _cumsum_mm(tri, v):
    p1, p2 = _split2(v)
    return _dot(tri, p1, 1, 0) + _dot(tri, p2, 1, 0)


def _cumsum_mm_fwd(tri, v):
    return _cumsum_mm(tri, v), tri


def _cumsum_mm_bwd(tri, g):
    p1, p2 = _split2(g)
    return jnp.zeros_like(tri), _dot(tri, p1, 0, 0) + _dot(tri, p2, 0, 0)


_cumsum_mm.defvjp(_cumsum_mm_fwd, _cumsum_mm_bwd)


def _make_row_roll(shift):
    @jax.custom_vjp
    def roll(x):
        return pltpu.roll(x, shift % CH, 0)

    def fwd(x):
        return roll(x), None

    def bwd(_, g):
        return (pltpu.roll(g, (-shift) % CH, 0),)

    roll.defvjp(fwd, bwd)
    return roll


_prev_row = _make_row_roll(1)
_next_row = _make_row_roll(-1)


def _mix_tile(guv, ph, state, w_s, b_s_t, ln_w, ln_b, lower, gn_w, consts):
    tri, masks, gmask, lvl = consts
    mt = guv.shape[0]
    half = D // 2

    wsm = [w_s[h] * gmask for h in range(NH)]
    ya = [[None] * NH for _ in range(mt // BLK)]
    for bi in range(mt // BLK):
        rows = slice(bi * BLK, (bi + 1) * BLK)
        ug = guv[rows, 0:half]
        vg = guv[rows, half:D]
        mu = jnp.mean(vg, axis=-1, keepdims=True)
        vc = vg - mu
        var = jnp.mean(vc * vc, axis=-1, keepdims=True)
        vn = vc * lax.rsqrt(var + EPS) * ln_w + ln_b
        for h in range(NH):
            cols = slice(h * HD, (h + 1) * HD)
            ya[bi][h] = ug[:, cols] * (_mm_c(wsm[h], vn[:, cols]) + b_s_t[:, h:h + 1])
    ya_full = jnp.concatenate([jnp.concatenate(r, axis=1) for r in ya], axis=0)

    l0 = lower[0:1, :]
    l1 = lower[1:2, :]
    mx = jnp.maximum(l0, l1)
    e0 = jnp.exp(l0 - mx)
    e1 = jnp.exp(l1 - mx)
    lb = e0 / (e0 + e1)
    nl = len(LEVELS)
    heads_to_rows = lambda a: jnp.concatenate([a[:, h * HD:(h + 1) * HD] for h in range(NH)], axis=0)
    st = list(state)
    yb = [[None] * NH for _ in range(mt // CH)]
    for ci in range(mt // CH):
        rows = slice(ci * CH, (ci + 1) * CH)
        qc = _silu(ph[rows, 0:half])
        f = lb + (1.0 - lb) * _sigmoid_rel(ph[rows, half:2 * half])
        lc = jnp.log(f)
        kc = 1.0 - f
        ic = ph[rows, 2 * half:3 * half]
        gate = _silu(ph[rows, 3 * half:4 * half])
        b = _cumsum_mm(tri, lc)
        xb = jnp.exp(b)
        xinv = jnp.exp(b[CH - 1:CH, :] - b)
        xl = []
        for i, hs in enumerate(LEVELS[:3]):
            refs = [jnp.broadcast_to(b[r:r + 1, :], (2 * hs, half)) for r in range(hs - 1, CH, 2 * hs)]
            bref = refs[0] if len(refs) == 1 else jnp.concatenate(refs, axis=0)
            xl.append(jnp.exp(lvl[i] * (b - bref)))
        b3 = b.reshape(CH // 8, 8, half)
        bref = jnp.broadcast_to(b3[:, 3:4, :], (CH // 8, 8, half)).reshape(CH, half)
        xl.append(jnp.exp(lvl[3] * (b - bref)))
        xl.append(jnp.exp(lvl[4] * _next_row(lc) + lvl[5] * lc + lvl[6] * _prev_row(lc)))
        xl.append(jnp.exp(lvl[7] * lc))
        zsrc = [jnp.concatenate([(qc if (r0 // hs) % 2 == 1 else kc)[r0:r0 + hs] for r0 in range(0, CH, hs)], axis=0)
                for hs in LEVELS[:3]]
        qs, ks = heads_to_rows(qc), heads_to_rows(kc)
        qx = qc * xb
        inter = jnp.concatenate([_nt_c(qx[:, h * HD:(h + 1) * HD], st[h]) for h in range(NH)], axis=0)
        attn = masks[nl] * _nt_c(qs, ks)
        for li in range(nl):
            xs = heads_to_rows(xl[li])
            if li < 3:
                z = heads_to_rows(zsrc[li]) * xs
                pairs = _nt_c(z, z)
            else:
                pairs = _nt_c(qs * xs, ks * xs)
            attn = attn + masks[li] * pairs
        o = inter + _mm_c(attn, heads_to_rows(ic))
        kx = kc * xinv
        decay = xb[CH - 1:CH, :]
        st = [st[h] * decay[:, h * HD:(h + 1) * HD] + _tn_c(ic[:, h * HD:(h + 1) * HD], kx[:, h * HD:(h + 1) * HD])
              for h in range(NH)]
        rs = lax.rsqrt(jnp.mean(o * o, axis=-1, keepdims=True) + EPS)
        ys = o * rs * gn_w * heads_to_rows(gate)
        yb[ci] = [ys[h * CH:(h + 1) * CH] for h in range(NH)]
    yb_full = jnp.concatenate([jnp.concatenate(r, axis=1) for r in yb], axis=0)
    return jnp.concatenate([ya_full, yb_full], axis=1), tuple(st)


def _fwd_mix(gelu_uv, proj_h, w_s, b_s_t, ln_w, ln_b, lower, gn_w, tables, to_gather):
    t = proj_h.shape[0]
    sub = _row_tile(t, MIX_TILE)
    n_sub = FWD_MIX_SUBTILES if t % (FWD_MIX_SUBTILES * sub) == 0 else 1
    mt = n_sub * sub
    nt_ = t // mt
    nc = len(tables)
    ng = len(to_gather)

    def body(*refs):
        guv_ref, ph_ref, ws_ref, bs_ref, lw_ref, lb_ref, lo_ref, gn_ref = refs[:8]
        c_refs = refs[8:8 + nc]
        refs = refs[8 + nc:]
        g_in = refs[:ng]
        y_ref, st_ref = refs[ng:2 + ng]
        g_out = refs[2 + ng:2 + 2 * ng]
        state = refs[2 + 2 * ng]
        ga = _Gather2(g_in, g_out, *refs[3 + 2 * ng:])

        @pl.when(pl.program_id(0) == 0)
        def _():
            state[...] = jnp.zeros_like(state)
            ga.start()

        @pl.when(pl.program_id(0) == (3 * nt_) // 4)
        def _():
            ga.forward()

        st = tuple(state[h] for h in range(NH))
        consts = tuple(r[...] for r in c_refs)
        for s in range(n_sub):
            rows = slice(s * sub, (s + 1) * sub)
            for h in range(NH):
                st_ref[s, h] = st[h]
            y, st = _mix_tile(guv_ref[rows, :].astype(F32), ph_ref[rows, :], st, ws_ref[...], bs_ref[...], lw_ref[...],
                              lb_ref[...], lo_ref[...], gn_ref[...], consts)
            y_ref[rows, :] = y.astype(BF)
        for h in range(NH):
            state[h] = st[h]

        @pl.when(pl.program_id(0) == nt_ - 1)
        def _():
            ga.finish()

    full = lambda a: pl.BlockSpec(a.shape, lambda i, nd=a.ndim: (0,) * nd)
    outs = pl.pallas_call(
        body, name="fwd_mix", grid=(nt_,),
        out_shape=(jax.ShapeDtypeStruct((t, D), BF), jax.ShapeDtypeStruct((t // sub, NH, HD, HD), F32),
                   *_exchange_out_shapes(to_gather, False)),
        in_specs=[pl.BlockSpec((mt, D), lambda i: (i, 0)), pl.BlockSpec((mt, D_IN - D), lambda i: (i, 0)), full(w_s), full(b_s_t),
                  full(ln_w), full(ln_b), full(lower), full(gn_w)] + [full(a) for a in tables] + [_any() for _ in to_gather],
        out_specs=(pl.BlockSpec((mt, D), lambda i: (i, 0)), pl.BlockSpec((n_sub, NH, HD, HD), lambda i: (i, 0, 0, 0)),
                   *[_any() for _ in to_gather]),
        scratch_shapes=[pltpu.VMEM((NH, HD, HD), F32)] + _exchange_sems(ng),
        compiler_params=_params(dimension_semantics=("arbitrary",), has_side_effects=True),
    )(gelu_uv, proj_h, w_s, b_s_t, ln_w, ln_b, lower, gn_w, *tables, *to_gather)
    return outs[0], outs[1], outs[2:]


def _bwd_mix(gelu_uv, dgelu_uv, proj_h, dycat, states, w_s, b_s_t, ln_w, ln_b, lower, gn_w, tables, to_scatter):
    t = proj_h.shape[0]
    sub = _row_tile(t, MIX_TILE)
    n_sub = BWD_MIX_SUBTILES if t % (BWD_MIX_SUBTILES * sub) == 0 else 1
    mt = n_sub * sub
    nt_ = t // mt
    nc = len(tables)
    ns = len(to_scatter)

    def body(*refs):
        guv_ref, gd_ref, ph_ref, dy_ref, st_ref, ws_ref, bs_ref, lw_ref, lb_ref, lo_ref, gn_ref = refs[:11]
        c_refs = refs[11:11 + nc]
        refs = refs[11 + nc:]
        s_in = refs[:ns]
        dproj_ref = refs[ns]
        acc = refs[1 + ns:7 + ns]
        s_out = refs[7 + ns:7 + 2 * ns]
        dstate = refs[7 + 2 * ns]
        ex = _Exchange(s_in, s_out, *refs[8 + 2 * ns:], True)

        @pl.when(pl.program_id(0) == 0)
        def _():
            dstate[...] = jnp.zeros_like(dstate)
            for r in acc:
                r[...] = jnp.zeros_like(r)
            ex.start()

        consts = tuple(r[...] for r in c_refs)

        def f(a, p, s, ws, bs, lw, lb_, lo, gn):
            return _mix_tile(a, p, s, ws, bs, lw, lb_, lo, gn, consts)

        dst = tuple(dstate[h] for h in range(NH))
        for s in reversed(range(n_sub)):
            rows = slice(s * sub, (s + 1) * sub)
            st = tuple(st_ref[s, h] for h in range(NH))
            _, vjp = jax.vjp(f, guv_ref[rows, :].astype(F32), ph_ref[rows, :], st, ws_ref[...], bs_ref[...], lw_ref[...],
                             lb_ref[...], lo_ref[...], gn_ref[...])
            grads = vjp((dy_ref[rows, :], dst))
            dproj_ref[rows, 0:D] = (grads[0] * gd_ref[rows, :].astype(F32)).astype(BF)
            dproj_ref[rows, D:D_IN] = grads[1].astype(BF)
            dst = grads[2]
            for r, gval in zip(acc, grads[3:]):
                r[...] += gval
        for h in range(NH):
            dstate[h] = dst[h]

        @pl.when(pl.program_id(0) == nt_ - 1)
        def _():
            ex.wait()

    full = lambda a: pl.BlockSpec(a.shape, lambda i, nd=a.ndim: (0,) * nd)
    rev = lambda i: (nt_ - 1 - i, 0)
    smalls = (w_s, b_s_t, ln_w, ln_b, lower, gn_w)
    outs = pl.pallas_call(
        body, name="bwd_mix", grid=(nt_,),
        out_shape=(jax.ShapeDtypeStruct((t, D_IN), BF), *[jax.ShapeDtypeStruct(a.shape, F32) for a in smalls],
                   *_exchange_out_shapes(to_scatter, True)),
        in_specs=[pl.BlockSpec((mt, D), rev), pl.BlockSpec((mt, D), rev), pl.BlockSpec((mt, D_IN - D), rev),
                  pl.BlockSpec((mt, D), rev), pl.BlockSpec((n_sub, NH, HD, HD), lambda i: (nt_ - 1 - i, 0, 0, 0))]
        + [full(a) for a in smalls] + [full(a) for a in tables] + [_any() for _ in to_scatter],
        out_specs=(pl.BlockSpec((mt, D_IN), rev), *[full(a) for a in smalls], *[_any() for _ in to_scatter]),
        scratch_shapes=[pltpu.VMEM((NH, HD, HD), F32)] + _exchange_sems(ns),
        compiler_params=_params(dimension_semantics=("arbitrary",), has_side_effects=True),
    )(gelu_uv, dgelu_uv, proj_h, dycat, states, w_s, b_s_t, ln_w, ln_b, lower, gn_w, *tables, *to_scatter)
    return outs[0], outs[1:7], outs[7:]


def _row_tile(t, want):
    return want if t % want == 0 else t


def _rms(v):
    rstd = lax.rsqrt(jnp.mean(v * v, axis=-1, keepdims=True) + EPS)
    return v * rstd, rstd


def _rms_bwd(dxhat, xhat, rstd):
    return rstd * (dxhat - xhat * jnp.mean(dxhat * xhat, axis=-1, keepdims=True))


def _colsum(v):
    return jnp.sum(v, axis=0, keepdims=True)


def _fwd_in(x, ada, n1w, w_in_st, to_gather):
    t = x.shape[0]
    tm = _row_tile(t, 512)
    n_t = t // tm
    ng = len(to_gather)

    def body(*refs):
        x_ref, ada_ref, n1_ref, wst_ref = refs[:4]
        g_in = refs[4:4 + ng]
        ph_ref, ga_ref, gd_ref, h1_ref = refs[4 + ng:8 + ng]
        g_out = refs[8 + ng:8 + 2 * ng]
        w_ref, sems = refs[8 + 2 * ng:10 + 2 * ng]
        ga = _Gather2(g_in, g_out, *refs[10 + 2 * ng:])

        @pl.when(pl.program_id(0) == 0)
        def _():
            ga.start()
            _load_columns(wst_ref, w_ref, sems)

        @pl.when(pl.program_id(0) == (3 * n_t) // 4)
        def _():
            ga.forward()

        xh, _ = _rms(x_ref[...])
        h1 = (xh * n1_ref[...]) * (1.0 + ada_ref[1:2, :]) + ada_ref[0:1, :]
        h1b = h1.astype(BF)
        h1_ref[...] = h1b
        uv = _mm(h1b, w_ref[:, 0:D])
        cdf = 0.5 * (1.0 + lax.erf(uv * 0.7071067811865476))
        ga_ref[...] = (uv * cdf).astype(BF)
        gd_ref[...] = (cdf + uv * (jnp.exp(-0.5 * uv * uv) * 0.3989422804014327)).astype(BF)
        ph_ref[...] = _mm(h1b, w_ref[:, D:D_IN])

        @pl.when(pl.program_id(0) == n_t - 1)
        def _():
            ga.finish()

    full = lambda a: pl.BlockSpec(a.shape, lambda i, nd=a.ndim: (0,) * nd)
    row = lambda w: pl.BlockSpec((tm, w), lambda i: (i, 0))
    outs = pl.pallas_call(
        body, name="fwd_in", grid=(n_t,),
        out_shape=(jax.ShapeDtypeStruct((t, D_IN - D), F32), jax.ShapeDtypeStruct((t, D), BF), jax.ShapeDtypeStruct((t, D), BF),
                   jax.ShapeDtypeStruct((t, D), BF), *_exchange_out_shapes(to_gather, False)),
        in_specs=[row(D), full(ada), full(n1w), _any()] + [_any() for _ in to_gather],
        out_specs=(row(D_IN - D), row(D), row(D), row(D), *[_any() for _ in to_gather]),
        scratch_shapes=[pltpu.VMEM((D, D_IN), BF), pltpu.SemaphoreType.DMA((N_DEV,))] + _exchange_sems(ng),
        compiler_params=_params(dimension_semantics=("arbitrary",), has_side_effects=True),
    )(x, ada, n1w, w_in_st, *to_gather)
    return outs[0], outs[1], outs[2], outs[3], outs[4:]


def _fwd_ffn(x, ycat, tgt, ada, n2w, fw, w_out, w_fi, w_fo):
    t = x.shape[0]
    tm = _row_tile(t, 256)

    def body(x_ref, y_ref, t_ref, ada_ref, n2_ref, fw_ref, wo_ref, wi_ref, wf_ref,
             x1_ref, mix_ref, h2_ref, act_ref, gu_ref, dffn_ref, dx2_ref, part_ref):
        @pl.when(pl.program_id(0) == 0)
        def _():
            part_ref[...] = jnp.zeros_like(part_ref)

        g1, sh2, sc2, g2 = ada_ref[2:3, :], ada_ref[3:4, :], ada_ref[4:5, :], ada_ref[5:6, :]
        mix = _mm(y_ref[...], wo_ref[...])
        mix_ref[...] = mix.astype(BF)
        x1 = x_ref[...] + g1 * mix
        x1_ref[...] = x1
        xh2, _ = _rms(x1)
        h2b = ((xh2 * n2_ref[...]) * (1.0 + sc2) + sh2).astype(BF)
        h2_ref[...] = h2b
        ffn = jnp.zeros((tm, D), F32)
        for lo, hi in FF_CHUNKS:
            gate = _nt(h2b, wi_ref[lo:hi, :])
            up = _nt(h2b, wi_ref[D_FF + lo:D_FF + hi, :])
            gu_ref[:, lo:hi] = gate.astype(BF)
            gu_ref[:, D_FF + lo:D_FF + hi] = up.astype(BF)
            a = (_silu(gate) * up).astype(BF)
            act_ref[:, lo:hi] = a
            ffn = ffn + _mm(a, wf_ref[lo:hi, :])
        x2 = x1 + g2 * ffn
        xh3, rstd3 = _rms(x2)
        err = xh3 * fw_ref[...] - t_ref[...]
        dy = err * (1.0 / D)
        dx2 = _rms_bwd(dy * fw_ref[...], xh3, rstd3)
        dx2_ref[...] = dx2
        dffn_ref[...] = (g2 * dx2).astype(BF)
        part_ref[0:1, :] += _colsum(dx2 * ffn)
        part_ref[1:2, :] += _colsum(dy * xh3)
        part_ref[2:3, :] += jnp.zeros((1, D), F32) + (0.5 / D) * jnp.sum(err * err)

    full = lambda a: pl.BlockSpec(a.shape, lambda i, nd=a.ndim: (0,) * nd)
    row = lambda w: pl.BlockSpec((tm, w), lambda i: (i, 0))
    return pl.pallas_call(
        body, name="fwd_ffn", grid=(t // tm,),
        out_shape=(jax.ShapeDtypeStruct((t, D), F32), jax.ShapeDtypeStruct((t, D), BF), jax.ShapeDtypeStruct((t, D), BF),
                   jax.ShapeDtypeStruct((t, D_FF), BF), jax.ShapeDtypeStruct((t, 2 * D_FF), BF),
                   jax.ShapeDtypeStruct((t, D), BF), jax.ShapeDtypeStruct((t, D), F32), jax.ShapeDtypeStruct((8, D), F32)),
        in_specs=[row(D), row(D), row(D), full(ada), full(n2w), full(fw), _whole_vmem(), _whole_vmem(), _whole_vmem()],
        out_specs=(row(D), row(D), row(D), row(D_FF), row(2 * D_FF), row(D), row(D),
                   pl.BlockSpec((8, D), lambda i: (0, 0))),
        compiler_params=_params(dimension_semantics=("arbitrary",)),
    )(x, ycat, tgt, ada, n2w, fw, w_out, w_fi, w_fo)


def _bwd_ffn(x1, gu, dffn, dx2, mix, ada, n2w, w_out, w_fi, w_fo):
    t = x1.shape[0]
    tm = _row_tile(t, 256)

    def body(x1_ref, gu_ref, dffn_ref, dx2_ref, mix_ref, ada_ref, n2_ref, wo_ref, wi_ref, wf_ref,
             dgu_ref, dx1_ref, dmix_ref, dycat_ref, part_ref):
        @pl.when(pl.program_id(0) == 0)
        def _():
            part_ref[...] = jnp.zeros_like(part_ref)

        g1, sc2 = ada_ref[2:3, :], ada_ref[4:5, :]
        dffn = dffn_ref[...]
        dh2 = jnp.zeros((tm, D), F32)
        for lo, hi in FF_CHUNKS:
            gate = gu_ref[:, lo:hi].astype(F32)
            up = gu_ref[:, D_FF + lo:D_FF + hi].astype(F32)
            dact = _nt(dffn, wf_ref[lo:hi, :])
            sg = _sig(gate)
            dgate = (dact * up * (sg * (1.0 + gate * (1.0 - sg)))).astype(BF)
            dup = (dact * (gate * sg)).astype(BF)
            dgu_ref[:, lo:hi] = dgate
            dgu_ref[:, D_FF + lo:D_FF + hi] = dup
            dh2 = dh2 + _mm(dgate, wi_ref[lo:hi, :]) + _mm(dup, wi_ref[D_FF + lo:D_FF + hi, :])
        x1 = x1_ref[...]
        xh2, rstd2 = _rms(x1)
        xn2 = xh2 * n2_ref[...]
        dxn2 = dh2 * (1.0 + sc2)
        dx1 = dx2_ref[...] + _rms_bwd(dxn2 * n2_ref[...], xh2, rstd2)
        dx1_ref[...] = dx1
        dmix = (g1 * dx1).astype(BF)
        dmix_ref[...] = dmix
        dycat_ref[...] = _nt(dmix, wo_ref[...])
        part_ref[0:1, :] += _colsum(dh2)
        part_ref[1:2, :] += _colsum(dh2 * xn2)
        part_ref[2:3, :] += _colsum(dxn2 * xh2)
        part_ref[3:4, :] += _colsum(dx1 * mix_ref[...].astype(F32))

    full = lambda a: pl.BlockSpec(a.shape, lambda i, nd=a.ndim: (0,) * nd)
    row = lambda w: pl.BlockSpec((tm, w), lambda i: (i, 0))
    return pl.pallas_call(
        body, name="bwd_ffn", grid=(t // tm,),
        out_shape=(jax.ShapeDtypeStruct((t, 2 * D_FF), BF), jax.ShapeDtypeStruct((t, D), F32),
                   jax.ShapeDtypeStruct((t, D), BF), jax.ShapeDtypeStruct((t, D), F32), jax.ShapeDtypeStruct((8, D), F32)),
        in_specs=[row(D), row(2 * D_FF), row(D), row(D), row(D), full(ada), full(n2w),
                  _whole_vmem(), _whole_vmem(), _whole_vmem()],
        out_specs=(row(2 * D_FF), row(D), row(D), row(D), pl.BlockSpec((8, D), lambda i: (0, 0))),
        compiler_params=_params(dimension_semantics=("arbitrary",)),
    )(x1, gu, dffn, dx2, mix, ada, n2w, w_out, w_fi, w_fo)


def _bwd_in(x, dproj, dx1, ada, n1w, w_in_st, to_scatter, to_gather):
    t = x.shape[0]
    tm = _row_tile(t, 1024)
    n_t = t // tm
    ns = len(to_scatter)
    ng = len(to_gather)
    nx = ns + ng

    def body(*refs):
        x_ref, dp_ref, dx1_ref, ada_ref, n1_ref, wst_ref = refs[:6]
        x_in = refs[6:6 + nx]
        gx_ref, part_ref = refs[6 + nx:8 + nx]
        x_out = refs[8 + nx:8 + 2 * nx]
        w_ref, sems = refs[8 + 2 * nx:10 + 2 * nx]
        sem_refs = refs[10 + 2 * nx:]
        ex = _Exchange(x_in[:ns], x_out[:ns], *sem_refs[:3], True)
        gx = _Exchange(x_in[ns:], x_out[ns:], *sem_refs[3:], False) if ng else None

        @pl.when(pl.program_id(0) == 0)
        def _():
            ex.start()
            if ng:
                gx.start()
            part_ref[...] = jnp.zeros_like(part_ref)
            _load_columns(wst_ref, w_ref, sems)

        dh1 = _nt(dp_ref[...], w_ref[...])
        xh, rstd = _rms(x_ref[...])
        xn = xh * n1_ref[...]
        dxn = dh1 * (1.0 + ada_ref[1:2, :])
        gx_ref[...] = dx1_ref[...] + _rms_bwd(dxn * n1_ref[...], xh, rstd)
        part_ref[0:1, :] += _colsum(dh1)
        part_ref[1:2, :] += _colsum(dh1 * xn)
        part_ref[2:3, :] += _colsum(dxn * xh)

        @pl.when(pl.program_id(0) == n_t - 1)
        def _():
            ex.wait()
            if ng:
                gx.wait()

    full = lambda a: pl.BlockSpec(a.shape, lambda i, nd=a.ndim: (0,) * nd)
    row = lambda w: pl.BlockSpec((tm, w), lambda i: (i, 0))
    outs = pl.pallas_call(
        body, name="bwd_in", grid=(n_t,),
        out_shape=(jax.ShapeDtypeStruct((t, D), F32), jax.ShapeDtypeStruct((8, D), F32),
                   *_exchange_out_shapes(to_scatter, True), *_exchange_out_shapes(to_gather, False)),
        in_specs=[row(D), row(D_IN), row(D), full(ada), full(n1w), _any()] + [_any() for _ in range(nx)],
        out_specs=(row(D), pl.BlockSpec((8, D), lambda i: (0, 0)), *[_any() for _ in range(nx)]),
        scratch_shapes=[pltpu.VMEM((D, D_IN), BF), pltpu.SemaphoreType.DMA((N_DEV,))] + _exchange_sems(ns)
        + (_exchange_sems(ng) if ng else []),
        compiler_params=_params(dimension_semantics=("arbitrary",), has_side_effects=True),
    )(x, dproj, dx1, ada, n1w, w_in_st, *to_scatter, *to_gather)
    return outs[0], outs[1], outs[2:2 + ns], outs[2 + ns:]


def _wgrad(a, b, name, a_spec, b_spec, out_shape, out_spec, grid, acc_shape, split=1, to_gather=()):
    n_j, n_t = grid
    ng = len(to_gather)

    def body(*refs):
        a_ref, b_ref = refs[:2]
        g_in = refs[2:2 + ng]
        o_ref = refs[2 + ng]
        g_out = refs[3 + ng:3 + 2 * ng]
        acc = refs[3 + 2 * ng]
        first = (pl.program_id(0) == 0) & (pl.program_id(1) == 0)
        last = (pl.program_id(0) == n_j - 1) & (pl.program_id(1) == n_t - 1)
        if ng:
            gx = _Exchange(g_in, g_out, *refs[4 + 2 * ng:], False)

            @pl.when(first)
            def _():
                gx.start()

        @pl.when(pl.program_id(1) == 0)
        def _():
            acc[...] = jnp.zeros_like(acc)

        acc[...] += _tn(a_ref[...], b_ref[...])

        @pl.when(pl.program_id(1) == n_t - 1)
        def _():
            if split == 1:
                o_ref[...] = acc[...].astype(BF)
            else:
                w = acc_shape[1] // split
                for s in range(split):
                    o_ref[s] = acc[:, s * w:(s + 1) * w].astype(BF)

        if ng:
            @pl.when(last)
            def _():
                gx.wait()

    outs = pl.pallas_call(
        body, name=name, grid=grid,
        out_shape=(jax.ShapeDtypeStruct(out_shape, BF), *_exchange_out_shapes(to_gather, False)),
        in_specs=[a_spec, b_spec] + [_any() for _ in to_gather], out_specs=(out_spec, *[_any() for _ in to_gather]),
        scratch_shapes=[pltpu.VMEM(acc_shape, F32)] + (_exchange_sems(ng) if ng else []),
        compiler_params=_params(dimension_semantics=("arbitrary", "arbitrary"), has_side_effects=bool(ng)),
    )(a, b, *to_gather)
    return (outs[0], outs[1:]) if ng else outs[0]


def _adamw_math(w, g, m, v):
    m = ADAM_B1 * m + (1.0 - ADAM_B1) * g
    v = ADAM_B2 * v + (1.0 - ADAM_B2) * (g * g)
    m_hat = m / (1.0 - ADAM_B1 ** ADAM_STEP)
    v_hat = v / (1.0 - ADAM_B2 ** ADAM_STEP)
    delta = -ADAM_LR * (m_hat / (jnp.sqrt(v_hat) + ADAM_EPS) + ADAM_WD * w)
    return delta, m, v


def _adamw_recv(w, m, v, recv, name, tr, carry=()):
    r, c = w.shape

    def body(w_ref, m_ref, v_ref, r_ref, *refs):
        g_ref, d_ref, nm_ref, nv_ref = refs[len(carry):len(carry) + 4]
        g = r_ref[0].astype(F32)
        for k in range(1, N_DEV):
            g = g + r_ref[k].astype(F32)
        g_ref[...] = g
        d_ref[...], nm_ref[...], nv_ref[...] = _adamw_math(w_ref[...], g, m_ref[...], v_ref[...])

    row = pl.BlockSpec((tr, c), lambda i: (i, 0))
    outs = pl.pallas_call(
        body, name=name, grid=(r // tr,),
        out_shape=tuple(jax.ShapeDtypeStruct((r, c), F32) for _ in range(4))
        + tuple(jax.ShapeDtypeStruct(a.shape, a.dtype) for a in carry),
        in_specs=[row, row, row, pl.BlockSpec((N_DEV, tr, c), lambda i: (0, i, 0))] + [_any() for _ in carry],
        out_specs=(row, row, row, row) + tuple(_any() for _ in carry),
        input_output_aliases={4 + i: 4 + i for i in range(len(carry))},
        compiler_params=_params(dimension_semantics=("arbitrary",)),
    )(w, m, v, recv, *carry)
    return tuple(outs)


def _adamw_ada(w, m, v, cact, dada_cols):
    r, c = w.shape
    tr = 256

    def body(w_ref, m_ref, v_ref, ca_ref, da_ref, g_ref, d_ref, nm_ref, nv_ref):
        g = _tn(ca_ref[...].astype(BF), da_ref[...].astype(BF))
        g_ref[...] = g
        d_ref[...], nm_ref[...], nv_ref[...] = _adamw_math(w_ref[...], g, m_ref[...], v_ref[...])

    row = pl.BlockSpec((tr, c), lambda i: (i, 0))
    return pl.pallas_call(
        body, name="adamw_ada", grid=(r // tr,),
        out_shape=tuple(jax.ShapeDtypeStruct((r, c), F32) for _ in range(4)),
        in_specs=[row, row, row, pl.BlockSpec((N_DEV, tr), lambda i: (0, i)), pl.BlockSpec(dada_cols.shape, lambda i: (0, 0))],
        out_specs=(row, row, row, row),
        compiler_params=_params(dimension_semantics=("arbitrary",)),
    )(w, m, v, cact, dada_cols)


def _adamw_small(gathered, wmv):
    n_g = len(gathered)
    n_p = len(wmv)
    flat = [a for trip in wmv for a in trip]

    def body(*refs):
        g_refs = refs[:n_g]
        p_refs = refs[n_g:n_g + 3 * n_p]
        o_refs = refs[n_g + 3 * n_p:]

        def total(ref):
            s = ref[0]
            for k in range(1, N_DEV):
                s = s + ref[k]
            return s

        f3, b3, b1, dws, dbs, dlnw, dlnb, dlo, dgn = [total(r) for r in g_refs]
        dada_rows = [b1[0:1], b1[1:2], b3[3:4], b3[0:1], b3[1:2], f3[0:1]]
        for r, g in enumerate(dada_rows):
            cs = slice(r * D, (r + 1) * D)
            w, m, v = p_refs[0][:, cs], p_refs[1][:, cs], p_refs[2][:, cs]
            o_refs[0][:, cs] = g
            o_refs[1][:, cs], o_refs[2][:, cs], o_refs[3][:, cs] = _adamw_math(w, g, m, v)
        grads = [None, b1[2:3], dws, dbs, dlnw, dlnb, dlo, dgn, b3[2:3], f3[1:2]]
        for i, g in enumerate(grads):
            if g is None:
                continue
            w, m, v = p_refs[3 * i][...], p_refs[3 * i + 1][...], p_refs[3 * i + 2][...]
            o_refs[4 * i][...] = g
            o_refs[4 * i + 1][...], o_refs[4 * i + 2][...], o_refs[4 * i + 3][...] = _adamw_math(w, g, m, v)
        o_refs[4 * n_p][...] = jnp.zeros((8, 128), F32) + f3[2:3, 0:128]

    out_shape = []
    for w, _, _ in wmv:
        out_shape += [jax.ShapeDtypeStruct(w.shape, F32)] * 4
    out_shape.append(jax.ShapeDtypeStruct((8, 128), F32))
    n_in = n_g + 3 * n_p
    return pl.pallas_call(
        body, name="adamw_small",
        out_shape=tuple(out_shape),
        in_specs=[_whole_vmem()] * n_in, out_specs=tuple(_whole_vmem() for _ in out_shape),
        compiler_params=_params(),
    )(*gathered, *flat)


def kernel(x, c, w_ada, b_ada, norm1_w, w_in, w_s, b_s, v_ln_w, v_ln_b, lower_bounds, gn_w, w_out, norm2_w, w_ffn_in, w_ffn_out, final_norm_w, loss_target, m_w_ada, m_b_ada, m_norm1_w, m_w_in, m_w_s, m_b_s, m_v_ln_w, m_v_ln_b, m_lower_bounds, m_gn_w, m_w_out, m_norm2_w, m_w_ffn_in, m_w_ffn_out, m_final_norm_w, v_w_ada, v_b_ada, v_norm1_w, v_w_in, v_w_s, v_b_s, v_v_ln_w, v_v_ln_b, v_lower_bounds, v_gn_w, v_w_out, v_norm2_w, v_w_ffn_in, v_w_ffn_out, v_final_norm_w):
    me = 4 * lax.axis_index("x") + 2 * lax.axis_index("y") + lax.axis_index("c")
    t = x.shape[1]
    x2d = x.reshape(t, D)
    tgt = loss_target.reshape(t, D)
    ada_cols = w_ada.shape[2]

    tp = lambda a: jnp.swapaxes(a[0], 0, 1)
    win_b, wout_b, wfi_b, wfo_b = _cast_bf16([w_in[0], w_out[0], tp(w_ffn_in), w_ffn_out[0]], "cast_weights")

    b_cols = lax.dynamic_slice(b_ada, (0, me * ada_cols), (1, ada_cols))
    win_st, ada_st, cact = _startup(win_b, c, w_ada[0], b_cols)
    ada = lax.dynamic_index_in_dim(ada_st, me, axis=1, keepdims=False).reshape(6, D)

    tables = _decay_tables()
    ws3 = w_s[0]
    bs_t = b_s[0].T

    proj_h, gelu_uv, dgelu_uv, h1, (wout_st, wfo_st) = _fwd_in(x2d, ada, norm1_w, win_st, [wout_b, wfo_b])
    ycat, states, (wfi_st,) = _fwd_mix(gelu_uv, proj_h, ws3, bs_t, v_ln_w, v_ln_b, lower_bounds, gn_w, tables, [wfi_b])
    w_out_full = wout_st.reshape(D, D)
    w_fo_full = wfo_st.reshape(D_FF, D)
    w_fi_full = wfi_st.reshape(2 * D_FF, D)
    x1, mixb, h2, act, gu, dffn, dx2, part_f = _fwd_ffn(x2d, ycat, tgt, ada, norm2_w, final_norm_w.reshape(1, D),
                                                        w_out_full, w_fi_full, w_fo_full)
    dgu, dx1, dmix, dycat, part_b3 = _bwd_ffn(x1, gu, dffn, dx2, mixb, ada, norm2_w, w_out_full, w_fi_full, w_fo_full)
    tk = _row_tile(t, 2048)
    n_t = t // tk
    win_cols = D_IN // N_DEV
    dwout = _wgrad(ycat, dmix, "wgrad_out",
                   pl.BlockSpec((tk, D), lambda j, i: (i, 0)), pl.BlockSpec((tk, D), lambda j, i: (i, 0)),
                   (D, D), pl.BlockSpec((D, D), lambda j, i: (0, 0)), (1, n_t), (D, D))
    dwfi = _wgrad(dgu, h2, "wgrad_ffn_in",
                  pl.BlockSpec((tk, FF_PAIR), lambda j, i: (i, j)), pl.BlockSpec((tk, D), lambda j, i: (i, 0)),
                  (4, FF_PAIR, D), pl.BlockSpec((None, FF_PAIR, D), lambda j, i: (j, 0, 0)), (4, n_t), (FF_PAIR, D))
    dwfo = _wgrad(act, dffn, "wgrad_ffn_out",
                  pl.BlockSpec((tk, FF_PAIR), lambda j, i: (i, j)), pl.BlockSpec((tk, D), lambda j, i: (i, 0)),
                  (2, FF_PAIR, D), pl.BlockSpec((None, FF_PAIR, D), lambda j, i: (j, 0, 0)), (2, n_t), (FF_PAIR, D))
    dproj, (dws, dbs_t, dlnw, dlnb, dlower, dgnw), (r_out, r_fi, r_fo) = _bwd_mix(
        gelu_uv, dgelu_uv, proj_h, dycat, states, ws3, bs_t, v_ln_w, v_ln_b, lower_bounds, gn_w, tables,
        [dwout.reshape(N_DEV, D // N_DEV, D), dwfi.reshape(N_DEV, FF_BLK, D), dwfo.reshape(N_DEV, D_FF // N_DEV, D)])
    dwin, early = _wgrad(h1, dproj, "wgrad_in",
                         pl.BlockSpec((tk, D), lambda j, i: (i, 0)), pl.BlockSpec((tk, 4 * win_cols), lambda j, i: (i, j)),
                         (N_DEV, D, win_cols), pl.BlockSpec((4, D, win_cols), lambda j, i: (j, 0, 0)), (N_DEV // 4, n_t),
                         (D, 4 * win_cols), split=4, to_gather=[part_f, part_b3, dws, dbs_t, dlnw, dlnb, dlower, dgnw])
    grad_x, part_b1, (r_in,), _ = _bwd_in(x2d, dproj, dx1, ada, norm1_w, win_st, [dwin], [])

    g_w_in, d_w_in, nm_w_in, nv_w_in, grad_x = _adamw_recv(w_in[0], m_w_in[0], v_w_in[0], r_in, "adamw_w_in", 256,
                                                           carry=(grad_x,))
    g_w_out, d_w_out, nm_w_out, nv_w_out = _adamw_recv(w_out[0], m_w_out[0], v_w_out[0], r_out, "adamw_w_out", 128)
    fi_t = _adamw_recv(tp(w_ffn_in), tp(m_w_ffn_in), tp(v_w_ffn_in), r_fi, "adamw_w_ffn_in", 176)
    g_w_fi, d_w_fi, nm_w_fi, nv_w_fi = [jnp.swapaxes(a, 0, 1) for a in fi_t]
    g_w_fo, d_w_fo, nm_w_fo, nv_w_fo = _adamw_recv(w_ffn_out[0], m_w_ffn_out[0], v_w_ffn_out[0], r_fo, "adamw_w_ffn_out", 176)

    (b1_all,) = _exchange([part_b1], "gather_small", False, True)
    gathered = [early[0], early[1], b1_all, *early[2:]]
    f3_all, b3_all = gathered[0], gathered[1]
    dada_all = jnp.stack([b1_all[:, 0], b1_all[:, 1], b3_all[:, 3], b3_all[:, 0], b3_all[:, 1], f3_all[:, 0]], axis=1)
    dada_cols = lax.dynamic_slice(dada_all.reshape(N_DEV, 6 * D), (0, me * ada_cols), (N_DEV, ada_cols))
    g_w_ada, d_w_ada, nm_w_ada, nv_w_ada = _adamw_ada(w_ada[0], m_w_ada[0], v_w_ada[0], cact, dada_cols)

    r1 = lambda a: a.reshape(1, D)
    tr = lambda a: a[0].T
    wmv = [
        (b_ada, m_b_ada, v_b_ada),
        (norm1_w, m_norm1_w, v_norm1_w),
        (w_s[0], m_w_s[0], v_w_s[0]),
        (tr(b_s), tr(m_b_s), tr(v_b_s)),
        (v_ln_w, m_v_ln_w, v_v_ln_w),
        (v_ln_b, m_v_ln_b, v_v_ln_b),
        (lower_bounds, m_lower_bounds, v_lower_bounds),
        (gn_w, m_gn_w, v_gn_w),
        (norm2_w, m_norm2_w, v_norm2_w),
        (r1(final_norm_w), r1(m_final_norm_w), r1(v_final_norm_w)),
    ]
    small = _adamw_small(gathered, wmv)
    loss = small[-1][0, 0]

    def unshape(i, a):
        if i == 2:
            return a.reshape(1, NH, BLK, BLK)
        if i == 3:
            return a.T.reshape(1, NH, BLK)
        if i == 9:
            return a.reshape(D)
        return a

    def small_out(kind):
        return [unshape(i, small[4 * i + kind]) for i in range(len(wmv))]

    e3 = lambda a: a[None]
    big = {
        0: (e3(g_w_ada), e3(g_w_in), e3(g_w_out), e3(g_w_fi), e3(g_w_fo)),
        1: (e3(d_w_ada), e3(d_w_in), e3(d_w_out), e3(d_w_fi), e3(d_w_fo)),
        2: (e3(nm_w_ada), e3(nm_w_in), e3(nm_w_out), e3(nm_w_fi), e3(nm_w_fo)),
        3: (e3(nv_w_ada), e3(nv_w_in), e3(nv_w_out), e3(nv_w_fi), e3(nv_w_fo)),
    }

    def ordered(kind):
        s = small_out(kind)
        b_ = big[kind]
        return [b_[0], s[0], s[1], b_[1], s[2], s[3], s[4], s[5], s[6], s[7], b_[2], s[8], b_[3], b_[4], s[9]]

    return (loss, grad_x.reshape(1, t, D), *ordered(0), *ordered(1), *ordered(2), *ordered(3))
```

```python
import numpy as np
import jax
import jax.numpy as jnp
from jax import lax
from jax.experimental import pallas as pl
from jax.experimental.pallas import tpu as pltpu

F32 = jnp.float32
BF = jnp.bfloat16
MESH = pl.DeviceIdType.MESH

N_DEV = 8
D = 1024
D_IN = 3072
D_FF = 2816
FF_BLK = D_FF // 4
FF_CHUNKS = ((0, D_FF),)
FF_PAIR = 2 * FF_BLK
CH = 64
BLK = 128
MIX_TILE = 256
FWD_MIX_SUBTILES = 4
BWD_MIX_SUBTILES = 2
NH = 4
HD = 128
EPS = 1e-6
LEVELS = (32, 16, 8, 4, 2, 1)

ADAM_LR = 0.001
ADAM_B1 = 0.9
ADAM_B2 = 0.999
ADAM_EPS = 1e-08
ADAM_WD = 0.01
ADAM_STEP = 10

VMEM_LIMIT = 56 * 1024 * 1024


def _params(**kw):
    return pltpu.CompilerParams(vmem_limit_bytes=VMEM_LIMIT, **kw)


def _whole_vmem():
    return pl.BlockSpec(memory_space=pltpu.VMEM)


def _any():
    return pl.BlockSpec(memory_space=pl.ANY)


class _Exchange:
    def __init__(self, ins, outs, send_sems, recv_sems, local_sems, scatter):
        self.ins, self.outs, self.scatter = ins, outs, scatter
        self.send_sems, self.recv_sems, self.local_sems = send_sems, recv_sems, local_sems
        x, y, c = lax.axis_index("x"), lax.axis_index("y"), lax.axis_index("c")
        self.me = 4 * x + 2 * y + c
        self.peers = []
        for k in range(1, N_DEV):
            peer = (1 - x if (k >> 2) & 1 else x, 1 - y if (k >> 1) & 1 else y, 1 - c if k & 1 else c)
            self.peers.append((peer, 4 * peer[0] + 2 * peer[1] + peer[2]))

    def _src(self, a, idx):
        return self.ins[a].at[idx] if self.scatter else self.ins[a]

    def _local(self):
        return [pltpu.make_async_copy(self._src(a, self.me), self.outs[a].at[self.me], self.local_sems.at[a])
                for a in range(len(self.ins))]

    def _remote(self, a, k, dst_slot):
        peer, peer_idx = self.peers[k]
        return pltpu.make_async_remote_copy(
            src_ref=self._src(a, peer_idx), dst_ref=self.outs[a].at[dst_slot],
            send_sem=self.send_sems.at[a, k], recv_sem=self.recv_sems.at[a, k],
            device_id=peer, device_id_type=MESH)

    def start(self):
        for cp in self._local():
            cp.start()
        for k in range(N_DEV - 1):
            for a in range(len(self.ins)):
                self._remote(a, k, self.me).start()

    def wait(self):
        for k in range(N_DEV - 1):
            for a in range(len(self.ins)):
                self._remote(a, k, self.peers[k][1]).wait()
        for cp in self._local():
            cp.wait()


class _Gather2:
    def __init__(self, ins, outs, send_sems, recv_sems, local_sems):
        self.ins, self.outs = ins, outs
        self.send_sems, self.recv_sems, self.local_sems = send_sems, recv_sems, local_sems
        x, y, c = lax.axis_index("x"), lax.axis_index("y"), lax.axis_index("c")
        self.c = c
        self.me = 4 * x + 2 * y + c
        self.sibling = (x, y, 1 - c)
        self.chips = [(1 - x, y), (x, 1 - y), (1 - x, 1 - y)]

    @staticmethod
    def _idx(px, py, pc):
        return 4 * px + 2 * py + pc

    def _copy(self, a, k, slot, to, own):
        src = self.ins[a] if own else self.outs[a].at[slot]
        return pltpu.make_async_remote_copy(
            src_ref=src, dst_ref=self.outs[a].at[slot],
            send_sem=self.send_sems.at[a, k], recv_sem=self.recv_sems.at[a, k],
            device_id=to, device_id_type=MESH)

    def _local(self):
        return [pltpu.make_async_copy(self.ins[a], self.outs[a].at[self.me], self.local_sems.at[a])
                for a in range(len(self.ins))]

    def start(self):
        for cp in self._local():
            cp.start()
        for a in range(len(self.ins)):
            self._copy(a, 0, self.me, self.sibling, True).start()
            for j, chip in enumerate(self.chips):
                self._copy(a, 1 + j, self.me, (*chip, self.c), True).start()

    def forward(self):
        for j, chip in enumerate(self.chips):
            for a in range(len(self.ins)):
                slot = self._idx(*chip, self.c)
                self._copy(a, 1 + j, slot, (*chip, self.c), True).wait_recv()
                self._copy(a, 4 + j, slot, self.sibling, False).start()

    def finish(self):
        for a in range(len(self.ins)):
            self._copy(a, 0, self._idx(*self.sibling), self.sibling, True).wait_recv()
            for j, chip in enumerate(self.chips):
                self._copy(a, 4 + j, self._idx(*chip, 1 - self.c), self.sibling, False).wait_recv()
            self._copy(a, 0, self.me, self.sibling, True).wait_send()
            for j, chip in enumerate(self.chips):
                self._copy(a, 1 + j, self.me, (*chip, self.c), True).wait_send()
                self._copy(a, 4 + j, self._idx(*chip, self.c), self.sibling, False).wait_send()
        for cp in self._local():
            cp.wait()


def _exchange_sems(n):
    return [pltpu.SemaphoreType.DMA((n, N_DEV - 1)), pltpu.SemaphoreType.DMA((n, N_DEV - 1)), pltpu.SemaphoreType.DMA((n,))]


def _exchange_out_shapes(arrays, scatter):
    return [jax.ShapeDtypeStruct(a.shape if scatter else (N_DEV,) + a.shape, a.dtype) for a in arrays]


def _exchange(arrays, name, scatter, in_vmem):
    n = len(arrays)

    def body(*refs):
        ex = _Exchange(refs[:n], refs[n:2 * n], *refs[2 * n:], scatter)
        ex.start()
        ex.wait()

    spec = _whole_vmem if in_vmem else _any
    return pl.pallas_call(
        body, name=name,
        out_shape=tuple(_exchange_out_shapes(arrays, scatter)),
        in_specs=[spec() for _ in arrays],
        out_specs=tuple(spec() for _ in arrays),
        scratch_shapes=_exchange_sems(n),
        compiler_params=_params(has_side_effects=True),
    )(*arrays)


def _cast_bf16(arrays, name):
    n = len(arrays)

    def body(*refs):
        for i in range(n):
            refs[n + i][...] = refs[i][...].astype(BF)

    return pl.pallas_call(
        body, name=name,
        out_shape=tuple(jax.ShapeDtypeStruct(a.shape, BF) for a in arrays),
        in_specs=[_whole_vmem() for _ in arrays],
        out_specs=tuple(_whole_vmem() for _ in arrays),
        compiler_params=_params(),
    )(*arrays)


def _load_columns(stacked_ref, full_ref, sems):
    c = stacked_ref.shape[2]
    cps = [pltpu.make_async_copy(stacked_ref.at[d], full_ref.at[:, pl.ds(d * c, c)], sems.at[d]) for d in range(N_DEV)]
    for cp in cps:
        cp.start()
    for cp in cps:
        cp.wait()


def _sig(v):
    return 0.5 * jnp.tanh(0.5 * v) + 0.5


@jax.custom_vjp
def _silu(v):
    return v * _sig(v)


def _silu_fwd(v):
    return _silu(v), v


def _silu_bwd(v, g):
    s = _sig(v)
    return (g * (s * (1.0 + v * (1.0 - s))),)


_silu.defvjp(_silu_fwd, _silu_bwd)


@jax.custom_vjp
def _sigmoid_rel(v):
    e = jnp.exp(-jnp.abs(v))
    d = 1.0 + e
    r = pl.reciprocal(d, approx=True)
    r = r * (2.0 - d * r)
    r = r * (2.0 - d * r)
    return jnp.where(v >= 0.0, r, e * r)


def _sigmoid_rel_fwd(v):
    s = _sigmoid_rel(v)
    return s, s


def _sigmoid_rel_bwd(s, g):
    return (g * (s * (1.0 - s)),)


_sigmoid_rel.defvjp(_sigmoid_rel_fwd, _sigmoid_rel_bwd)


def _dot(a, b, ca, cb):
    return lax.dot_general(a, b, (((ca,), (cb,)), ((), ())), preferred_element_type=F32)


@jax.custom_vjp
def _mm(a, b):
    return _dot(a, b, 1, 0)


def _mm_fwd(a, b):
    return _dot(a, b, 1, 0), (a, b)


def _mm_bwd(res, g):
    a, b = res
    gb = g.astype(BF)
    return _dot(gb, b, 1, 1).astype(a.dtype), _dot(a, gb, 0, 0).astype(b.dtype)


_mm.defvjp(_mm_fwd, _mm_bwd)


@jax.custom_vjp
def _nt(a, b):
    return _dot(a, b, 1, 1)


def _nt_fwd(a, b):
    return _dot(a, b, 1, 1), (a, b)


def _nt_bwd(res, g):
    a, b = res
    gb = g.astype(BF)
    return _dot(gb, b, 1, 0).astype(a.dtype), _dot(gb, a, 0, 0).astype(b.dtype)


_nt.defvjp(_nt_fwd, _nt_bwd)


@jax.custom_vjp
def _tn(a, b):
    return _dot(a, b, 0, 0)


def _tn_fwd(a, b):
    return _dot(a, b, 0, 0), (a, b)


def _tn_bwd(res, g):
    a, b = res
    gb = g.astype(BF)
    return _dot(b, gb, 1, 1).astype(a.dtype), _dot(a, gb, 1, 0).astype(b.dtype)


_tn.defvjp(_tn_fwd, _tn_bwd)


def _make_cast_dot(ca, cb, da_dims, db_dims):
    @jax.custom_vjp
    def dot(a, b):
        return _dot(a.astype(BF), b.astype(BF), ca, cb)

    def fwd(a, b):
        ab, bb = a.astype(BF), b.astype(BF)
        return _dot(ab, bb, ca, cb), (ab, bb)

    def bwd(res, g):
        ops = {"a": res[0], "b": res[1], "g": g.astype(BF)}
        return (_dot(ops[da_dims[0]], ops[da_dims[1]], da_dims[2], da_dims[3]),
                _dot(ops[db_dims[0]], ops[db_dims[1]], db_dims[2], db_dims[3]))

    dot.defvjp(fwd, bwd)
    return dot


_mm_c = _make_cast_dot(1, 0, ("g", "b", 1, 1), ("a", "g", 0, 0))
_nt_c = _make_cast_dot(1, 1, ("g", "b", 1, 0), ("g", "a", 0, 0))
_tn_c = _make_cast_dot(0, 0, ("b", "g", 1, 1), ("a", "g", 1, 0))


def _startup(win_b, c, w_ada, b_cols):
    ncol = w_ada.shape[1]

    def body(win_ref, c_ref, w_ref, b_ref, winst_ref, adast_ref, cact_ref, call_ref, blk_ref,
             ws, wr, wl, cs, cr, cl, as_, ar, al):
        big = _Gather2([win_ref], [winst_ref], ws, wr, wl)
        big.start()
        gc = _Exchange([c_ref], [call_ref], cs, cr, cl, False)
        gc.start()
        gc.wait()
        ca = _silu(call_ref[...].reshape(N_DEV, D))
        cact_ref[...] = ca
        blk_ref[...] = _mm(ca.astype(BF), w_ref[...].astype(BF)) + b_ref[...]
        ga = _Exchange([blk_ref], [adast_ref], as_, ar, al, False)
        ga.start()
        ga.wait()
        big.forward()
        big.finish()

    return pl.pallas_call(
        body, name="startup",
        out_shape=(jax.ShapeDtypeStruct((N_DEV,) + win_b.shape, BF), jax.ShapeDtypeStruct((N_DEV, N_DEV, ncol), F32),
                   jax.ShapeDtypeStruct((N_DEV, D), F32)),
        in_specs=[_any(), _whole_vmem(), _whole_vmem(), _whole_vmem()],
        out_specs=(_any(), _whole_vmem(), _whole_vmem()),
        scratch_shapes=[pltpu.VMEM((N_DEV, 1, D), F32), pltpu.VMEM((N_DEV, ncol), F32)]
        + _exchange_sems(1) + _exchange_sems(1) + _exchange_sems(1),
        compiler_params=_params(has_side_effects=True),
    )(win_b, c, w_ada, b_cols)


def _decay_tables():
    t = np.arange(CH)
    tri = (t[None, :] <= t[:, None]).astype(np.float32)
    masks = []
    for h in LEVELS:
        m = (t // (2 * h)) * (2 * h) + h
        upper = t >= m
        same = (t[:, None] // (2 * h)) == (t[None, :] // (2 * h))
        masks.append(same & upper[:, None] & (~upper)[None, :])
    masks.append(np.eye(CH, dtype=bool))
    lv = [np.where((t % (2 * h)) >= h, 1.0, -1.0) for h in LEVELS[:4]]
    m4 = t % 4
    lv += [(m4 == 0) * 1.0, (m4 >= 2) * 1.0, (m4 == 3) * 1.0, (t % 2 == 1) * 1.0]
    lvl = np.broadcast_to(np.stack(lv)[:, :, None], (8, CH, D // 2)).astype(np.float32)
    cid = np.arange(BLK) // CH
    gmask = (cid[:, None] >= cid[None, :]).astype(np.float32)
    masks = np.stack(masks).astype(np.float32)
    stacked = np.zeros((masks.shape[0], NH * CH, NH * CH), np.float32)
    for h in range(NH):
        stacked[:, h * CH:(h + 1) * CH, h * CH:(h + 1) * CH] = masks
    return (jnp.asarray(tri, BF), jnp.asarray(stacked), jnp.asarray(gmask), jnp.asarray(lvl))


def _split2(v):
    v1 = v.astype(BF)
    return v1, (v - v1.astype(F32)).astype(BF)


@jax.custom_vjp
def _cumsum_mm(tri, v):
    p1, p2 = _split2(v)
    return _dot(tri, p1, 1, 0) + _dot(tri, p2, 1, 0)


def _cumsum_mm_fwd(tri, v):
    return _cumsum_mm(tri, v), tri


def _cumsum_mm_bwd(tri, g):
    p1, p2 = _split2(g)
    return jnp.zeros_like(tri), _dot(tri, p1, 0, 0) + _dot(tri, p2, 0, 0)


_cumsum_mm.defvjp(_cumsum_mm_fwd, _cumsum_mm_bwd)


def _make_row_roll(shift):
    @jax.custom_vjp
    def roll(x):
        return pltpu.roll(x, shift % CH, 0)

    def fwd(x):
        return roll(x), None

    def bwd(_, g):
        return (pltpu.roll(g, (-shift) % CH, 0),)

    roll.defvjp(fwd, bwd)
    return roll


_prev_row = _make_row_roll(1)
_next_row = _make_row_roll(-1)


def _mix_tile(guv, ph, state, w_s, b_s_t, ln_w, ln_b, lower, gn_w, consts):
    tri, masks, gmask, lvl = consts
    mt = guv.shape[0]
    half = D // 2

    wsm = [w_s[h] * gmask for h in range(NH)]
    ya = [[None] * NH for _ in range(mt // BLK)]
    for bi in range(mt // BLK):
        rows = slice(bi * BLK, (bi + 1) * BLK)
        ug = guv[rows, 0:half]
        vg = guv[rows, half:D]
        mu = jnp.mean(vg, axis=-1, keepdims=True)
        vc = vg - mu
        var = jnp.mean(vc * vc, axis=-1, keepdims=True)
        vn = vc * lax.rsqrt(var + EPS) * ln_w + ln_b
        for h in range(NH):
            cols = slice(h * HD, (h + 1) * HD)
            ya[bi][h] = ug[:, cols] * (_mm_c(wsm[h], vn[:, cols]) + b_s_t[:, h:h + 1])
    ya_full = jnp.concatenate([jnp.concatenate(r, axis=1) for r in ya], axis=0)

    l0 = lower[0:1, :]
    l1 = lower[1:2, :]
    mx = jnp.maximum(l0, l1)
    e0 = jnp.exp(l0 - mx)
    e1 = jnp.exp(l1 - mx)
    lb = e0 / (e0 + e1)
    nl = len(LEVELS)
    heads_to_rows = lambda a: jnp.concatenate([a[:, h * HD:(h + 1) * HD] for h in range(NH)], axis=0)
    st = list(state)
    yb = [[None] * NH for _ in range(mt // CH)]
    for ci in range(mt // CH):
        rows = slice(ci * CH, (ci + 1) * CH)
        qc = _silu(ph[rows, 0:half])
        f = lb + (1.0 - lb) * _sigmoid_rel(ph[rows, half:2 * half])
        lc = jnp.log(f)
        kc = 1.0 - f
        ic = ph[rows, 2 * half:3 * half]
        gate = _silu(ph[rows, 3 * half:4 * half])
        b = _cumsum_mm(tri, lc)
        xb = jnp.exp(b)
        xinv = jnp.exp(b[CH - 1:CH, :] - b)
        xl = []
        for i, hs in enumerate(LEVELS[:3]):
            refs = [jnp.broadcast_to(b[r:r + 1, :], (2 * hs, half)) for r in range(hs - 1, CH, 2 * hs)]
            bref = refs[0] if len(refs) == 1 else jnp.concatenate(refs, axis=0)
            xl.append(jnp.exp(lvl[i] * (b - bref)))
        b3 = b.reshape(CH // 8, 8, half)
        bref = jnp.broadcast_to(b3[:, 3:4, :], (CH // 8, 8, half)).reshape(CH, half)
        xl.append(jnp.exp(lvl[3] * (b - bref)))
        xl.append(jnp.exp(lvl[4] * _next_row(lc) + lvl[5] * lc + lvl[6] * _prev_row(lc)))
        xl.append(jnp.exp(lvl[7] * lc))
        zsrc = [jnp.concatenate([(qc if (r0 // hs) % 2 == 1 else kc)[r0:r0 + hs] for r0 in range(0, CH, hs)], axis=0)
                for hs in LEVELS[:3]]
        qs, ks = heads_to_rows(qc), heads_to_rows(kc)
        qx = qc * xb
        inter = jnp.concatenate([_nt_c(qx[:, h * HD:(h + 1) * HD], st[h]) for h in range(NH)], axis=0)
        attn = masks[nl] * _nt_c(qs, ks)
        for li in range(nl):
            xs = heads_to_rows(xl[li])
            if li < 3:
                z = heads_to_rows(zsrc[li]) * xs
                pairs = _nt_c(z, z)
            else:
                pairs = _nt_c(qs * xs, ks * xs)
            attn = attn + masks[li] * pairs
        o = inter + _mm_c(attn, heads_to_rows(ic))
        kx = kc * xinv
        decay = xb[CH - 1:CH, :]
        st = [st[h] * decay[:, h * HD:(h + 1) * HD] + _tn_c(ic[:, h * HD:(h + 1) * HD], kx[:, h * HD:(h + 1) * HD])
              for h in range(NH)]
        rs = lax.rsqrt(jnp.mean(o * o, axis=-1, keepdims=True) + EPS)
        ys = o * rs * gn_w * heads_to_rows(gate)
        yb[ci] = [ys[h * CH:(h + 1) * CH] for h in range(NH)]
    yb_full = jnp.concatenate([jnp.concatenate(r, axis=1) for r in yb], axis=0)
    return jnp.concatenate([ya_full, yb_full], axis=1), tuple(st)


def _fwd_mix(gelu_uv, proj_h, w_s, b_s_t, ln_w, ln_b, lower, gn_w, tables, to_gather):
    t = proj_h.shape[0]
    sub = _row_tile(t, MIX_TILE)
    n_sub = FWD_MIX_SUBTILES if t % (FWD_MIX_SUBTILES * sub) == 0 else 1
    mt = n_sub * sub
    nt_ = t // mt
    nc = len(tables)
    ng = len(to_gather)

    def body(*refs):
        guv_ref, ph_ref, ws_ref, bs_ref, lw_ref, lb_ref, lo_ref, gn_ref = refs[:8]
        c_refs = refs[8:8 + nc]
        refs = refs[8 + nc:]
        g_in = refs[:ng]
        y_ref, st_ref = refs[ng:2 + ng]
        g_out = refs[2 + ng:2 + 2 * ng]
        state = refs[2 + 2 * ng]
        ga = _Gather2(g_in, g_out, *refs[3 + 2 * ng:])

        @pl.when(pl.program_id(0) == 0)
        def _():
            state[...] = jnp.zeros_like(state)
            ga.start()

        @pl.when(pl.program_id(0) == (3 * nt_) // 4)
        def _():
            ga.forward()

        st = tuple(state[h] for h in range(NH))
        consts = tuple(r[...] for r in c_refs)
        for s in range(n_sub):
            rows = slice(s * sub, (s + 1) * sub)
            for h in range(NH):
                st_ref[s, h] = st[h]
            y, st = _mix_tile(guv_ref[rows, :].astype(F32), ph_ref[rows, :], st, ws_ref[...], bs_ref[...], lw_ref[...],
                              lb_ref[...], lo_ref[...], gn_ref[...], consts)
            y_ref[rows, :] = y.astype(BF)
        for h in range(NH):
            state[h] = st[h]

        @pl.when(pl.program_id(0) == nt_ - 1)
        def _():
            ga.finish()

    full = lambda a: pl.BlockSpec(a.shape, lambda i, nd=a.ndim: (0,) * nd)
    outs = pl.pallas_call(
        body, name="fwd_mix", grid=(nt_,),
        out_shape=(jax.ShapeDtypeStruct((t, D), BF), jax.ShapeDtypeStruct((t // sub, NH, HD, HD), F32),
                   *_exchange_out_shapes(to_gather, False)),
        in_specs=[pl.BlockSpec((mt, D), lambda i: (i, 0)), pl.BlockSpec((mt, D_IN - D), lambda i: (i, 0)), full(w_s), full(b_s_t),
                  full(ln_w), full(ln_b), full(lower), full(gn_w)] + [full(a) for a in tables] + [_any() for _ in to_gather],
        out_specs=(pl.BlockSpec((mt, D), lambda i: (i, 0)), pl.BlockSpec((n_sub, NH, HD, HD), lambda i: (i, 0, 0, 0)),
                   *[_any() for _ in to_gather]),
        scratch_shapes=[pltpu.VMEM((NH, HD, HD), F32)] + _exchange_sems(ng),
        compiler_params=_params(dimension_semantics=("arbitrary",), has_side_effects=True),
    )(gelu_uv, proj_h, w_s, b_s_t, ln_w, ln_b, lower, gn_w, *tables, *to_gather)
    return outs[0], outs[1], outs[2:]


def _bwd_mix(gelu_uv, dgelu_uv, proj_h, dycat, states, w_s, b_s_t, ln_w, ln_b, lower, gn_w, tables, to_scatter):
    t = proj_h.shape[0]
    sub = _row_tile(t, MIX_TILE)
    n_sub = BWD_MIX_SUBTILES if t % (BWD_MIX_SUBTILES * sub) == 0 else 1
    mt = n_sub * sub
    nt_ = t // mt
    nc = len(tables)
    ns = len(to_scatter)

    def body(*refs):
        guv_ref, gd_ref, ph_ref, dy_ref, st_ref, ws_ref, bs_ref, lw_ref, lb_ref, lo_ref, gn_ref = refs[:11]
        c_refs = refs[11:11 + nc]
        refs = refs[11 + nc:]
        s_in = refs[:ns]
        dproj_ref = refs[ns]
        acc = refs[1 + ns:7 + ns]
        s_out = refs[7 + ns:7 + 2 * ns]
        dstate = refs[7 + 2 * ns]
        ex = _Exchange(s_in, s_out, *refs[8 + 2 * ns:], True)

        @pl.when(pl.program_id(0) == 0)
        def _():
            dstate[...] = jnp.zeros_like(dstate)
            for r in acc:
                r[...] = jnp.zeros_like(r)
            ex.start()

        consts = tuple(r[...] for r in c_refs)

        def f(a, p, s, ws, bs, lw, lb_, lo, gn):
            return _mix_tile(a, p, s, ws, bs, lw, lb_, lo, gn, consts)

        dst = tuple(dstate[h] for h in range(NH))
        for s in reversed(range(n_sub)):
            rows = slice(s * sub, (s + 1) * sub)
            st = tuple(st_ref[s, h] for h in range(NH))
            _, vjp = jax.vjp(f, guv_ref[rows, :].astype(F32), ph_ref[rows, :], st, ws_ref[...], bs_ref[...], lw_ref[...],
                             lb_ref[...], lo_ref[...], gn_ref[...])
            grads = vjp((dy_ref[rows, :], dst))
            dproj_ref[rows, 0:D] = (grads[0] * gd_ref[rows, :].astype(F32)).astype(BF)
            dproj_ref[rows, D:D_IN] = grads[1].astype(BF)
            dst = grads[2]
            for r, gval in zip(acc, grads[3:]):
                r[...] += gval
        for h in range(NH):
            dstate[h] = dst[h]

        @pl.when(pl.program_id(0) == nt_ - 1)
        def _():
            ex.wait()

    full = lambda a: pl.BlockSpec(a.shape, lambda i, nd=a.ndim: (0,) * nd)
    rev = lambda i: (nt_ - 1 - i, 0)
    smalls = (w_s, b_s_t, ln_w, ln_b, lower, gn_w)
    outs = pl.pallas_call(
        body, name="bwd_mix", grid=(nt_,),
        out_shape=(jax.ShapeDtypeStruct((t, D_IN), BF), *[jax.ShapeDtypeStruct(a.shape, F32) for a in smalls],
                   *_exchange_out_shapes(to_scatter, True)),
        in_specs=[pl.BlockSpec((mt, D), rev), pl.BlockSpec((mt, D), rev), pl.BlockSpec((mt, D_IN - D), rev),
                  pl.BlockSpec((mt, D), rev), pl.BlockSpec((n_sub, NH, HD, HD), lambda i: (nt_ - 1 - i, 0, 0, 0))]
        + [full(a) for a in smalls] + [full(a) for a in tables] + [_any() for _ in to_scatter],
        out_specs=(pl.BlockSpec((mt, D_IN), rev), *[full(a) for a in smalls], *[_any() for _ in to_scatter]),
        scratch_shapes=[pltpu.VMEM((NH, HD, HD), F32)] + _exchange_sems(ns),
        compiler_params=_params(dimension_semantics=("arbitrary",), has_side_effects=True),
    )(gelu_uv, dgelu_uv, proj_h, dycat, states, w_s, b_s_t, ln_w, ln_b, lower, gn_w, *tables, *to_scatter)
    return outs[0], outs[1:7], outs[7:]


def _row_tile(t, want):
    return want if t % want == 0 else t


def _rms(v):
    rstd = lax.rsqrt(jnp.mean(v * v, axis=-1, keepdims=True) + EPS)
    return v * rstd, rstd


def _rms_bwd(dxhat, xhat, rstd):
    return rstd * (dxhat - xhat * jnp.mean(dxhat * xhat, axis=-1, keepdims=True))


def _colsum(v):
    return jnp.sum(v, axis=0, keepdims=True)


def _fwd_in(x, ada, n1w, w_in_st, to_gather):
    t = x.shape[0]
    tm = _row_tile(t, 512)
    n_t = t // tm
    ng = len(to_gather)

    def body(*refs):
        x_ref, ada_ref, n1_ref, wst_ref = refs[:4]
        g_in = refs[4:4 + ng]
        ph_ref, ga_ref, gd_ref, h1_ref = refs[4 + ng:8 + ng]
        g_out = refs[8 + ng:8 + 2 * ng]
        w_ref, sems = refs[8 + 2 * ng:10 + 2 * ng]
        ga = _Gather2(g_in, g_out, *refs[10 + 2 * ng:])

        @pl.when(pl.program_id(0) == 0)
        def _():
            ga.start()
            _load_columns(wst_ref, w_ref, sems)

        @pl.when(pl.program_id(0) == (3 * n_t) // 4)
        def _():
            ga.forward()

        xh, _ = _rms(x_ref[...])
        h1 = (xh * n1_ref[...]) * (1.0 + ada_ref[1:2, :]) + ada_ref[0:1, :]
        h1b = h1.astype(BF)
        h1_ref[...] = h1b
        uv = _mm(h1b, w_ref[:, 0:D])
        cdf = 0.5 * (1.0 + lax.erf(uv * 0.7071067811865476))
        ga_ref[...] = (uv * cdf).astype(BF)
        gd_ref[...] = (cdf + uv * (jnp.exp(-0.5 * uv * uv) * 0.3989422804014327)).astype(BF)
        ph_ref[...] = _mm(h1b, w_ref[:, D:D_IN])

        @pl.when(pl.program_id(0) == n_t - 1)
        def _():
            ga.finish()

    full = lambda a: pl.BlockSpec(a.shape, lambda i, nd=a.ndim: (0,) * nd)
    row = lambda w: pl.BlockSpec((tm, w), lambda i: (i, 0))
    outs = pl.pallas_call(
        body, name="fwd_in", grid=(n_t,),
        out_shape=(jax.ShapeDtypeStruct((t, D_IN - D), F32), jax.ShapeDtypeStruct((t, D), BF), jax.ShapeDtypeStruct((t, D), BF),
                   jax.ShapeDtypeStruct((t, D), BF), *_exchange_out_shapes(to_gather, False)),
        in_specs=[row(D), full(ada), full(n1w), _any()] + [_any() for _ in to_gather],
        out_specs=(row(D_IN - D), row(D), row(D), row(D), *[_any() for _ in to_gather]),
        scratch_shapes=[pltpu.VMEM((D, D_IN), BF), pltpu.SemaphoreType.DMA((N_DEV,))] + _exchange_sems(ng),
        compiler_params=_params(dimension_semantics=("arbitrary",), has_side_effects=True),
    )(x, ada, n1w, w_in_st, *to_gather)
    return outs[0], outs[1], outs[2], outs[3], outs[4:]


def _fwd_ffn(x, ycat, tgt, ada, n2w, fw, w_out, w_fi, w_fo):
    t = x.shape[0]
    tm = _row_tile(t, 256)

    def body(x_ref, y_ref, t_ref, ada_ref, n2_ref, fw_ref, wo_ref, wi_ref, wf_ref,
             x1_ref, mix_ref, h2_ref, act_ref, gu_ref, dffn_ref, dx2_ref, part_ref):
        @pl.when(pl.program_id(0) == 0)
        def _():
            part_ref[...] = jnp.zeros_like(part_ref)

        g1, sh2, sc2, g2 = ada_ref[2:3, :], ada_ref[3:4, :], ada_ref[4:5, :], ada_ref[5:6, :]
        mix = _mm(y_ref[...], wo_ref[...])
        mix_ref[...] = mix.astype(BF)
        x1 = x_ref[...] + g1 * mix
        x1_ref[...] = x1
        xh2, _ = _rms(x1)
        h2b = ((xh2 * n2_ref[...]) * (1.0 + sc2) + sh2).astype(BF)
        h2_ref[...] = h2b
        ffn = jnp.zeros((tm, D), F32)
        for lo, hi in FF_CHUNKS:
            gate = _nt(h2b, wi_ref[lo:hi, :])
            up = _nt(h2b, wi_ref[D_FF + lo:D_FF + hi, :])
            gu_ref[:, lo:hi] = gate.astype(BF)
            gu_ref[:, D_FF + lo:D_FF + hi] = up.astype(BF)
            a = (_silu(gate) * up).astype(BF)
            act_ref[:, lo:hi] = a
            ffn = ffn + _mm(a, wf_ref[lo:hi, :])
        x2 = x1 + g2 * ffn
        xh3, rstd3 = _rms(x2)
        err = xh3 * fw_ref[...] - t_ref[...]
        dy = err * (1.0 / D)
        dx2 = _rms_bwd(dy * fw_ref[...], xh3, rstd3)
        dx2_ref[...] = dx2
        dffn_ref[...] = (g2 * dx2).astype(BF)
        part_ref[0:1, :] += _colsum(dx2 * ffn)
        part_ref[1:2, :] += _colsum(dy * xh3)
        part_ref[2:3, :] += jnp.zeros((1, D), F32) + (0.5 / D) * jnp.sum(err * err)

    full = lambda a: pl.BlockSpec(a.shape, lambda i, nd=a.ndim: (0,) * nd)
    row = lambda w: pl.BlockSpec((tm, w), lambda i: (i, 0))
    return pl.pallas_call(
        body, name="fwd_ffn", grid=(t // tm,),
        out_shape=(jax.ShapeDtypeStruct((t, D), F32), jax.ShapeDtypeStruct((t, D), BF), jax.ShapeDtypeStruct((t, D), BF),
                   jax.ShapeDtypeStruct((t, D_FF), BF), jax.ShapeDtypeStruct((t, 2 * D_FF), BF),
                   jax.ShapeDtypeStruct((t, D), BF), jax.ShapeDtypeStruct((t, D), F32), jax.ShapeDtypeStruct((8, D), F32)),
        in_specs=[row(D), row(D), row(D), full(ada), full(n2w), full(fw), _whole_vmem(), _whole_vmem(), _whole_vmem()],
        out_specs=(row(D), row(D), row(D), row(D_FF), row(2 * D_FF), row(D), row(D),
                   pl.BlockSpec((8, D), lambda i: (0, 0))),
        compiler_params=_params(dimension_semantics=("arbitrary",)),
    )(x, ycat, tgt, ada, n2w, fw, w_out, w_fi, w_fo)


def _bwd_ffn(x1, gu, dffn, dx2, mix, ada, n2w, w_out, w_fi, w_fo):
    t = x1.shape[0]
    tm = _row_tile(t, 256)

    def body(x1_ref, gu_ref, dffn_ref, dx2_ref, mix_ref, ada_ref, n2_ref, wo_ref, wi_ref, wf_ref,
             dgu_ref, dx1_ref, dmix_ref, dycat_ref, part_ref):
        @pl.when(pl.program_id(0) == 0)
        def _():
            part_ref[...] = jnp.zeros_like(part_ref)

        g1, sc2 = ada_ref[2:3, :], ada_ref[4:5, :]
        dffn = dffn_ref[...]
        dh2 = jnp.zeros((tm, D), F32)
        for lo, hi in FF_CHUNKS:
            gate = gu_ref[:, lo:hi].astype(F32)
            up = gu_ref[:, D_FF + lo:D_FF + hi].astype(F32)
            dact = _nt(dffn, wf_ref[lo:hi, :])
            sg = _sig(gate)
            dgate = (dact * up * (sg * (1.0 + gate * (1.0 - sg)))).astype(BF)
            dup = (dact * (gate * sg)).astype(BF)
            dgu_ref[:, lo:hi] = dgate
            dgu_ref[:, D_FF + lo:D_FF + hi] = dup
            dh2 = dh2 + _mm(dgate, wi_ref[lo:hi, :]) + _mm(dup, wi_ref[D_FF + lo:D_FF + hi, :])
        x1 = x1_ref[...]
        xh2, rstd2 = _rms(x1)
        xn2 = xh2 * n2_ref[...]
        dxn2 = dh2 * (1.0 + sc2)
        dx1 = dx2_ref[...] + _rms_bwd(dxn2 * n2_ref[...], xh2, rstd2)
        dx1_ref[...] = dx1
        dmix = (g1 * dx1).astype(BF)
        dmix_ref[...] = dmix
        dycat_ref[...] = _nt(dmix, wo_ref[...])
        part_ref[0:1, :] += _colsum(dh2)
        part_ref[1:2, :] += _colsum(dh2 * xn2)
        part_ref[2:3, :] += _colsum(dxn2 * xh2)
        part_ref[3:4, :] += _colsum(dx1 * mix_ref[...].astype(F32))

    full = lambda a: pl.BlockSpec(a.shape, lambda i, nd=a.ndim: (0,) * nd)
    row = lambda w: pl.BlockSpec((tm, w), lambda i: (i, 0))
    return pl.pallas_call(
        body, name="bwd_ffn", grid=(t // tm,),
        out_shape=(jax.ShapeDtypeStruct((t, 2 * D_FF), BF), jax.ShapeDtypeStruct((t, D), F32),
                   jax.ShapeDtypeStruct((t, D), BF), jax.ShapeDtypeStruct((t, D), F32), jax.ShapeDtypeStruct((8, D), F32)),
        in_specs=[row(D), row(2 * D_FF), row(D), row(D), row(D), full(ada), full(n2w),
                  _whole_vmem(), _whole_vmem(), _whole_vmem()],
        out_specs=(row(2 * D_FF), row(D), row(D), row(D), pl.BlockSpec((8, D), lambda i: (0, 0))),
        compiler_params=_params(dimension_semantics=("arbitrary",)),
    )(x1, gu, dffn, dx2, mix, ada, n2w, w_out, w_fi, w_fo)


def _bwd_in(x, dproj, dx1, ada, n1w, w_in_st, to_scatter, to_gather):
    t = x.shape[0]
    tm = _row_tile(t, 1024)
    n_t = t // tm
    ns = len(to_scatter)
    ng = len(to_gather)
    nx = ns + ng

    def body(*refs):
        x_ref, dp_ref, dx1_ref, ada_ref, n1_ref, wst_ref = refs[:6]
        x_in = refs[6:6 + nx]
        gx_ref, part_ref = refs[6 + nx:8 + nx]
        x_out = refs[8 + nx:8 + 2 * nx]
        w_ref, sems = refs[8 + 2 * nx:10 + 2 * nx]
        sem_refs = refs[10 + 2 * nx:]
        ex = _Exchange(x_in[:ns], x_out[:ns], *sem_refs[:3], True)
        gx = _Exchange(x_in[ns:], x_out[ns:], *sem_refs[3:], False) if ng else None

        @pl.when(pl.program_id(0) == 0)
        def _():
            ex.start()
            if ng:
                gx.start()
            part_ref[...] = jnp.zeros_like(part_ref)
            _load_columns(wst_ref, w_ref, sems)

        dh1 = _nt(dp_ref[...], w_ref[...])
        xh, rstd = _rms(x_ref[...])
        xn = xh * n1_ref[...]
        dxn = dh1 * (1.0 + ada_ref[1:2, :])
        gx_ref[...] = dx1_ref[...] + _rms_bwd(dxn * n1_ref[...], xh, rstd)
        part_ref[0:1, :] += _colsum(dh1)
        part_ref[1:2, :] += _colsum(dh1 * xn)
        part_ref[2:3, :] += _colsum(dxn * xh)

        @pl.when(pl.program_id(0) == n_t - 1)
        def _():
            ex.wait()
            if ng:
                gx.wait()

    full = lambda a: pl.BlockSpec(a.shape, lambda i, nd=a.ndim: (0,) * nd)
    row = lambda w: pl.BlockSpec((tm, w), lambda i: (i, 0))
    outs = pl.pallas_call(
        body, name="bwd_in", grid=(n_t,),
        out_shape=(jax.ShapeDtypeStruct((t, D), F32), jax.ShapeDtypeStruct((8, D), F32),
                   *_exchange_out_shapes(to_scatter, True), *_exchange_out_shapes(to_gather, False)),
        in_specs=[row(D), row(D_IN), row(D), full(ada), full(n1w), _any()] + [_any() for _ in range(nx)],
        out_specs=(row(D), pl.BlockSpec((8, D), lambda i: (0, 0)), *[_any() for _ in range(nx)]),
        scratch_shapes=[pltpu.VMEM((D, D_IN), BF), pltpu.SemaphoreType.DMA((N_DEV,))] + _exchange_sems(ns)
        + (_exchange_sems(ng) if ng else []),
        compiler_params=_params(dimension_semantics=("arbitrary",)),
    )(x, dproj, dx1, ada, n1w, w_in_st, *to_scatter, *to_gather)
    return outs[0], outs[1], outs[2:2 + ns], outs[2 + ns:]


def _wgrad(a, b, name, a_spec, b_spec, out_shape, out_spec, grid, acc_shape, split=1, to_gather=()):
    n_j, n_t = grid
    ng = len(to_gather)

    def body(*refs):
        a_ref, b_ref = refs[:2]
        g_in = refs[2:2 + ng]
        o_ref = refs[2 + ng]
        g_out = refs[3 + ng:3 + 2 * ng]
        acc = refs[3 + 2 * ng]
        first = (pl.program_id(0) == 0) & (pl.program_id(1) == 0)
        last = (pl.program_id(0) == n_j - 1) & (pl.program_id(1) == n_t - 1)
        if ng:
            gx = _Exchange(g_in, g_out, *refs[4 + 2 * ng:], False)

            @pl.when(first)
            def _():
                gx.start()

        @pl.when(pl.program_id(1) == 0)
        def _():
            acc[...] = jnp.zeros_like(acc)

        acc[...] += _tn(a_ref[...], b_ref[...])

        @pl.when(pl.program_id(1) == n_t - 1)
        def _():
            if split == 1:
                o_ref[...] = acc[...].astype(BF)
            else:
                w = acc_shape[1] // split
                for s in range(split):
                    o_ref[s] = acc[:, s * w:(s + 1) * w].astype(BF)

        if ng:
            @pl.when(last)
            def _():
                gx.wait()

    outs = pl.pallas_call(
        body, name=name, grid=grid,
        out_shape=(jax.ShapeDtypeStruct(out_shape, BF), *_exchange_out_shapes(to_gather, False)),
        in_specs=[a_spec, b_spec] + [_any() for _ in to_gather], out_specs=(out_spec, *[_any() for _ in to_gather]),
        scratch_shapes=[pltpu.VMEM(acc_shape, F32)] + (_exchange_sems(ng) if ng else []),
        compiler_params=_params(dimension_semantics=("arbitrary", "arbitrary"), has_side_effects=bool(ng)),
    )(a, b, *to_gather)
    return (outs[0], outs[1:]) if ng else outs[0]


def _adamw_math(w, g, m, v):
    m = ADAM_B1 * m + (1.0 - ADAM_B1) * g
    v = ADAM_B2 * v + (1.0 - ADAM_B2) * (g * g)
    m_hat = m / (1.0 - ADAM_B1 ** ADAM_STEP)
    v_hat = v / (1.0 - ADAM_B2 ** ADAM_STEP)
    delta = -ADAM_LR * (m_hat / (jnp.sqrt(v_hat) + ADAM_EPS) + ADAM_WD * w)
    return delta, m, v


def _adamw_recv(w, m, v, recv, name, tr, carry=()):
    r, c = w.shape

    def body(w_ref, m_ref, v_ref, r_ref, *refs):
        g_ref, d_ref, nm_ref, nv_ref = refs[len(carry):len(carry) + 4]
        g = r_ref[0].astype(F32)
        for k in range(1, N_DEV):
            g = g + r_ref[k].astype(F32)
        g_ref[...] = g
        d_ref[...], nm_ref[...], nv_ref[...] = _adamw_math(w_ref[...], g, m_ref[...], v_ref[...])

    row = pl.BlockSpec((tr, c), lambda i: (i, 0))
    outs = pl.pallas_call(
        body, name=name, grid=(r // tr,),
        out_shape=tuple(jax.ShapeDtypeStruct((r, c), F32) for _ in range(4))
        + tuple(jax.ShapeDtypeStruct(a.shape, a.dtype) for a in carry),
        in_specs=[row, row, row, pl.BlockSpec((N_DEV, tr, c), lambda i: (0, i, 0))] + [_any() for _ in carry],
        out_specs=(row, row, row, row) + tuple(_any() for _ in carry),
        input_output_aliases={4 + i: 4 + i for i in range(len(carry))},
        compiler_params=_params(dimension_semantics=("arbitrary",)),
    )(w, m, v, recv, *carry)
    return tuple(outs)


def _adamw_ada(w, m, v, cact, dada_cols):
    r, c = w.shape
    tr = 256

    def body(w_ref, m_ref, v_ref, ca_ref, da_ref, g_ref, d_ref, nm_ref, nv_ref):
        g = _tn(ca_ref[...].astype(BF), da_ref[...].astype(BF))
        g_ref[...] = g
        d_ref[...], nm_ref[...], nv_ref[...] = _adamw_math(w_ref[...], g, m_ref[...], v_ref[...])

    row = pl.BlockSpec((tr, c), lambda i: (i, 0))
    return pl.pallas_call(
        body, name="adamw_ada", grid=(r // tr,),
        out_shape=tuple(jax.ShapeDtypeStruct((r, c), F32) for _ in range(4)),
        in_specs=[row, row, row, pl.BlockSpec((N_DEV, tr), lambda i: (0, i)), pl.BlockSpec(dada_cols.shape, lambda i: (0, 0))],
        out_specs=(row, row, row, row),
        compiler_params=_params(dimension_semantics=("arbitrary",)),
    )(w, m, v, cact, dada_cols)


def _adamw_small(gathered, wmv):
    n_g = len(gathered)
    n_p = len(wmv)
    flat = [a for trip in wmv for a in trip]

    def body(*refs):
        g_refs = refs[:n_g]
        p_refs = refs[n_g:n_g + 3 * n_p]
        o_refs = refs[n_g + 3 * n_p:]

        def total(ref):
            s = ref[0]
            for k in range(1, N_DEV):
                s = s + ref[k]
            return s

        f3, b3, b1, dws, dbs, dlnw, dlnb, dlo, dgn = [total(r) for r in g_refs]
        dada_rows = [b1[0:1], b1[1:2], b3[3:4], b3[0:1], b3[1:2], f3[0:1]]
        for r, g in enumerate(dada_rows):
            cs = slice(r * D, (r + 1) * D)
            w, m, v = p_refs[0][:, cs], p_refs[1][:, cs], p_refs[2][:, cs]
            o_refs[0][:, cs] = g
            o_refs[1][:, cs], o_refs[2][:, cs], o_refs[3][:, cs] = _adamw_math(w, g, m, v)
        grads = [None, b1[2:3], dws, dbs, dlnw, dlnb, dlo, dgn, b3[2:3], f3[1:2]]
        for i, g in enumerate(grads):
            if g is None:
                continue
            w, m, v = p_refs[3 * i][...], p_refs[3 * i + 1][...], p_refs[3 * i + 2][...]
            o_refs[4 * i][...] = g
            o_refs[4 * i + 1][...], o_refs[4 * i + 2][...], o_refs[4 * i + 3][...] = _adamw_math(w, g, m, v)
        o_refs[4 * n_p][...] = jnp.zeros((8, 128), F32) + f3[2:3, 0:128]

    out_shape = []
    for w, _, _ in wmv:
        out_shape += [jax.ShapeDtypeStruct(w.shape, F32)] * 4
    out_shape.append(jax.ShapeDtypeStruct((8, 128), F32))
    n_in = n_g + 3 * n_p
    return pl.pallas_call(
        body, name="adamw_small",
        out_shape=tuple(out_shape),
        in_specs=[_whole_vmem()] * n_in, out_specs=tuple(_whole_vmem() for _ in out_shape),
        compiler_params=_params(),
    )(*gathered, *flat)


def kernel(x, c, w_ada, b_ada, norm1_w, w_in, w_s, b_s, v_ln_w, v_ln_b, lower_bounds, gn_w, w_out, norm2_w, w_ffn_in, w_ffn_out, final_norm_w, loss_target, m_w_ada, m_b_ada, m_norm1_w, m_w_in, m_w_s, m_b_s, m_v_ln_w, m_v_ln_b, m_lower_bounds, m_gn_w, m_w_out, m_norm2_w, m_w_ffn_in, m_w_ffn_out, m_final_norm_w, v_w_ada, v_b_ada, v_norm1_w, v_w_in, v_w_s, v_b_s, v_v_ln_w, v_v_ln_b, v_lower_bounds, v_gn_w, v_w_out, v_norm2_w, v_w_ffn_in, v_w_ffn_out, v_final_norm_w):
    me = 4 * lax.axis_index("x") + 2 * lax.axis_index("y") + lax.axis_index("c")
    t = x.shape[1]
    x2d = x.reshape(t, D)
    tgt = loss_target.reshape(t, D)
    ada_cols = w_ada.shape[2]

    tp = lambda a: jnp.swapaxes(a[0], 0, 1)
    win_b, wout_b, wfi_b, wfo_b = _cast_bf16([w_in[0], w_out[0], tp(w_ffn_in), w_ffn_out[0]], "cast_weights")

    b_cols = lax.dynamic_slice(b_ada, (0, me * ada_cols), (1, ada_cols))
    win_st, ada_st, cact = _startup(win_b, c, w_ada[0], b_cols)
    ada = lax.dynamic_index_in_dim(ada_st, me, axis=1, keepdims=False).reshape(6, D)

    tables = _decay_tables()
    ws3 = w_s[0]
    bs_t = b_s[0].T

    proj_h, gelu_uv, dgelu_uv, h1, (wout_st, wfo_st) = _fwd_in(x2d, ada, norm1_w, win_st, [wout_b, wfo_b])
    ycat, states, (wfi_st,) = _fwd_mix(gelu_uv, proj_h, ws3, bs_t, v_ln_w, v_ln_b, lower_bounds, gn_w, tables, [wfi_b])
    w_out_full = wout_st.reshape(D, D)
    w_fo_full = wfo_st.reshape(D_FF, D)
    w_fi_full = wfi_st.reshape(2 * D_FF, D)
    x1, mixb, h2, act, gu, dffn, dx2, part_f = _fwd_ffn(x2d, ycat, tgt, ada, norm2_w, final_norm_w.reshape(1, D),
                                                        w_out_full, w_fi_full, w_fo_full)
    dgu, dx1, dmix, dycat, part_b3 = _bwd_ffn(x1, gu, dffn, dx2, mixb, ada, norm2_w, w_out_full, w_fi_full, w_fo_full)
    tk = _row_tile(t, 2048)
    n_t = t // tk
    win_cols = D_IN // N_DEV
    dwout = _wgrad(ycat, dmix, "wgrad_out",
                   pl.BlockSpec((tk, D), lambda j, i: (i, 0)), pl.BlockSpec((tk, D), lambda j, i: (i, 0)),
                   (D, D), pl.BlockSpec((D, D), lambda j, i: (0, 0)), (1, n_t), (D, D))
    dwfi = _wgrad(dgu, h2, "wgrad_ffn_in",
                  pl.BlockSpec((tk, FF_PAIR), lambda j, i: (i, j)), pl.BlockSpec((tk, D), lambda j, i: (i, 0)),
                  (4, FF_PAIR, D), pl.BlockSpec((None, FF_PAIR, D), lambda j, i: (j, 0, 0)), (4, n_t), (FF_PAIR, D))
    dwfo = _wgrad(act, dffn, "wgrad_ffn_out",
                  pl.BlockSpec((tk, FF_PAIR), lambda j, i: (i, j)), pl.BlockSpec((tk, D), lambda j, i: (i, 0)),
                  (2, FF_PAIR, D), pl.BlockSpec((None, FF_PAIR, D), lambda j, i: (j, 0, 0)), (2, n_t), (FF_PAIR, D))
    dproj, (dws, dbs_t, dlnw, dlnb, dlower, dgnw), (r_out, r_fi, r_fo) = _bwd_mix(
        gelu_uv, dgelu_uv, proj_h, dycat, states, ws3, bs_t, v_ln_w, v_ln_b, lower_bounds, gn_w, tables,
        [dwout.reshape(N_DEV, D // N_DEV, D), dwfi.reshape(N_DEV, FF_BLK, D), dwfo.reshape(N_DEV, D_FF // N_DEV, D)])
    dwin, early = _wgrad(h1, dproj, "wgrad_in",
                         pl.BlockSpec((tk, D), lambda j, i: (i, 0)), pl.BlockSpec((tk, 4 * win_cols), lambda j, i: (i, j)),
                         (N_DEV, D, win_cols), pl.BlockSpec((4, D, win_cols), lambda j, i: (j, 0, 0)), (N_DEV // 4, n_t),
                         (D, 4 * win_cols), split=4, to_gather=[part_f, part_b3, dws, dbs_t, dlnw, dlnb, dlower, dgnw])
    grad_x, part_b1, (r_in,), _ = _bwd_in(x2d, dproj, dx1, ada, norm1_w, win_st, [dwin], [])

    g_w_in, d_w_in, nm_w_in, nv_w_in = _adamw_recv(w_in[0], m_w_in[0], v_w_in[0], r_in, "adamw_w_in", 256)
    g_w_out, d_w_out, nm_w_out, nv_w_out = _adamw_recv(w_out[0], m_w_out[0], v_w_out[0], r_out, "adamw_w_out", 128)
    fi_t = _adamw_recv(tp(w_ffn_in), tp(m_w_ffn_in), tp(v_w_ffn_in), r_fi, "adamw_w_ffn_in", 176)
    g_w_fi, d_w_fi, nm_w_fi, nv_w_fi = [jnp.swapaxes(a, 0, 1) for a in fi_t]
    g_w_fo, d_w_fo, nm_w_fo, nv_w_fo = _adamw_recv(w_ffn_out[0], m_w_ffn_out[0], v_w_ffn_out[0], r_fo, "adamw_w_ffn_out", 176)

    (b1_all,) = _exchange([part_b1], "gather_small", False, True)
    gathered = [early[0], early[1], b1_all, *early[2:]]
    f3_all, b3_all = gathered[0], gathered[1]
    dada_all = jnp.stack([b1_all[:, 0], b1_all[:, 1], b3_all[:, 3], b3_all[:, 0], b3_all[:, 1], f3_all[:, 0]], axis=1)
    dada_cols = lax.dynamic_slice(dada_all.reshape(N_DEV, 6 * D), (0, me * ada_cols), (N_DEV, ada_cols))
    g_w_ada, d_w_ada, nm_w_ada, nv_w_ada = _adamw_ada(w_ada[0], m_w_ada[0], v_w_ada[0], cact, dada_cols)

    r1 = lambda a: a.reshape(1, D)
    tr = lambda a: a[0].T
    wmv = [
        (b_ada, m_b_ada, v_b_ada),
        (norm1_w, m_norm1_w, v_norm1_w),
        (w_s[0], m_w_s[0], v_w_s[0]),
        (tr(b_s), tr(m_b_s), tr(v_b_s)),
        (v_ln_w, m_v_ln_w, v_v_ln_w),
        (v_ln_b, m_v_ln_b, v_v_ln_b),
        (lower_bounds, m_lower_bounds, v_lower_bounds),
        (gn_w, m_gn_w, v_gn_w),
        (norm2_w, m_norm2_w, v_norm2_w),
        (r1(final_norm_w), r1(m_final_norm_w), r1(v_final_norm_w)),
    ]
    small = _adamw_small(gathered, wmv)
    loss = small[-1][0, 0]

    def unshape(i, a):
        if i == 2:
            return a.reshape(1, NH, BLK, BLK)
        if i == 3:
            return a.T.reshape(1, NH, BLK)
        if i == 9:
            return a.reshape(D)
        return a

    def small_out(kind):
        return [unshape(i, small[4 * i + kind]) for i in range(len(wmv))]

    e3 = lambda a: a[None]
    big = {
        0: (e3(g_w_ada), e3(g_w_in), e3(g_w_out), e3(g_w_fi), e3(g_w_fo)),
        1: (e3(d_w_ada), e3(d_w_in), e3(d_w_out), e3(d_w_fi), e3(d_w_fo)),
        2: (e3(nm_w_ada), e3(nm_w_in), e3(nm_w_out), e3(nm_w_fi), e3(nm_w_fo)),
        3: (e3(nv_w_ada), e3(nv_w_in), e3(nv_w_out), e3(nv_w_fi), e3(nv_w_fo)),
    }

    def ordered(kind):
        s = small_out(kind)
        b_ = big[kind]
        return [b_[0], s[0], s[1], b_[1], s[2], s[3], s[4], s[5], s[6], s[7], b_[2], s[8], b_[3], b_[4], s[9]]

    return (loss, grad_x.reshape(1, t, D), *ordered(0), *ordered(1), *ordered(2), *ordered(3))
```

```python
import numpy as np
import jax
import jax.numpy as jnp
from jax import lax
from jax.experimental import pallas as pl
from jax.experimental.pallas import tpu as pltpu

F32 = jnp.float32
BF = jnp.bfloat16
MESH = pl.DeviceIdType.MESH

N_DEV = 8
D = 1024
D_IN = 3072
D_FF = 2816
FF_BLK = D_FF // 4
FF_CHUNKS = ((0, D_FF),)
FF_PAIR = 2 * FF_BLK
CH = 64
BLK = 128
MIX_TILE = 256
FWD_MIX_SUBTILES = 4
BWD_MIX_SUBTILES = 2
NH = 4
HD = 128
EPS = 1e-6
LEVELS = (32, 16, 8, 4, 2, 1)

ADAM_LR = 0.001
ADAM_B1 = 0.9
ADAM_B2 = 0.999
ADAM_EPS = 1e-08
ADAM_WD = 0.01
ADAM_STEP = 10

VMEM_LIMIT = 56 * 1024 * 1024


def _params(**kw):
    return pltpu.CompilerParams(vmem_limit_bytes=VMEM_LIMIT, **kw)


def _whole_vmem():
    return pl.BlockSpec(memory_space=pltpu.VMEM)


def _any():
    return pl.BlockSpec(memory_space=pl.ANY)


class _Exchange:
    def __init__(self, ins, outs, send_sems, recv_sems, local_sems, scatter):
        self.ins, self.outs, self.scatter = ins, outs, scatter
        self.send_sems, self.recv_sems, self.local_sems = send_sems, recv_sems, local_sems
        x, y, c = lax.axis_index("x"), lax.axis_index("y"), lax.axis_index("c")
        self.me = 4 * x + 2 * y + c
        self.peers = []
        for k in range(1, N_DEV):
            peer = (1 - x if (k >> 2) & 1 else x, 1 - y if (k >> 1) & 1 else y, 1 - c if k & 1 else c)
            self.peers.append((peer, 4 * peer[0] + 2 * peer[1] + peer[2]))

    def _src(self, a, idx):
        return self.ins[a].at[idx] if self.scatter else self.ins[a]

    def _local(self):
        return [pltpu.make_async_copy(self._src(a, self.me), self.outs[a].at[self.me], self.local_sems.at[a])
                for a in range(len(self.ins))]

    def _remote(self, a, k, dst_slot):
        peer, peer_idx = self.peers[k]
        return pltpu.make_async_remote_copy(
            src_ref=self._src(a, peer_idx), dst_ref=self.outs[a].at[dst_slot],
            send_sem=self.send_sems.at[a, k], recv_sem=self.recv_sems.at[a, k],
            device_id=peer, device_id_type=MESH)

    def start(self):
        for cp in self._local():
            cp.start()
        for k in range(N_DEV - 1):
            for a in range(len(self.ins)):
                self._remote(a, k, self.me).start()

    def wait(self):
        for k in range(N_DEV - 1):
            for a in range(len(self.ins)):
                self._remote(a, k, self.peers[k][1]).wait()
        for cp in self._local():
            cp.wait()


class _Gather2:
    def __init__(self, ins, outs, send_sems, recv_sems, local_sems):
        self.ins, self.outs = ins, outs
        self.send_sems, self.recv_sems, self.local_sems = send_sems, recv_sems, local_sems
        x, y, c = lax.axis_index("x"), lax.axis_index("y"), lax.axis_index("c")
        self.c = c
        self.me = 4 * x + 2 * y + c
        self.sibling = (x, y, 1 - c)
        self.chips = [(1 - x, y), (x, 1 - y), (1 - x, 1 - y)]

    @staticmethod
    def _idx(px, py, pc):
        return 4 * px + 2 * py + pc

    def _copy(self, a, k, slot, to, own):
        src = self.ins[a] if own else self.outs[a].at[slot]
        return pltpu.make_async_remote_copy(
            src_ref=src, dst_ref=self.outs[a].at[slot],
            send_sem=self.send_sems.at[a, k], recv_sem=self.recv_sems.at[a, k],
            device_id=to, device_id_type=MESH)

    def _local(self):
        return [pltpu.make_async_copy(self.ins[a], self.outs[a].at[self.me], self.local_sems.at[a])
                for a in range(len(self.ins))]

    def start(self):
        for cp in self._local():
            cp.start()
        for a in range(len(self.ins)):
            self._copy(a, 0, self.me, self.sibling, True).start()
            for j, chip in enumerate(self.chips):
                self._copy(a, 1 + j, self.me, (*chip, self.c), True).start()

    def forward(self):
        for j, chip in enumerate(self.chips):
            for a in range(len(self.ins)):
                slot = self._idx(*chip, self.c)
                self._copy(a, 1 + j, slot, (*chip, self.c), True).wait_recv()
                self._copy(a, 4 + j, slot, self.sibling, False).start()

    def finish(self):
        for a in range(len(self.ins)):
            self._copy(a, 0, self._idx(*self.sibling), self.sibling, True).wait_recv()
            for j, chip in enumerate(self.chips):
                self._copy(a, 4 + j, self._idx(*chip, 1 - self.c), self.sibling, False).wait_recv()
            self._copy(a, 0, self.me, self.sibling, True).wait_send()
            for j, chip in enumerate(self.chips):
                self._copy(a, 1 + j, self.me, (*chip, self.c), True).wait_send()
                self._copy(a, 4 + j, self._idx(*chip, self.c), self.sibling, False).wait_send()
        for cp in self._local():
            cp.wait()


def _exchange_sems(n):
    return [pltpu.SemaphoreType.DMA((n, N_DEV - 1)), pltpu.SemaphoreType.DMA((n, N_DEV - 1)), pltpu.SemaphoreType.DMA((n,))]


def _exchange_out_shapes(arrays, scatter):
    return [jax.ShapeDtypeStruct(a.shape if scatter else (N_DEV,) + a.shape, a.dtype) for a in arrays]


def _exchange(arrays, name, scatter, in_vmem):
    n = len(arrays)

    def body(*refs):
        ex = _Exchange(refs[:n], refs[n:2 * n], *refs[2 * n:], scatter)
        ex.start()
        ex.wait()

    spec = _whole_vmem if in_vmem else _any
    return pl.pallas_call(
        body, name=name,
        out_shape=tuple(_exchange_out_shapes(arrays, scatter)),
        in_specs=[spec() for _ in arrays],
        out_specs=tuple(spec() for _ in arrays),
        scratch_shapes=_exchange_sems(n),
        compiler_params=_params(has_side_effects=True),
    )(*arrays)


def _cast_bf16(arrays, name):
    n = len(arrays)

    def body(*refs):
        for i in range(n):
            refs[n + i][...] = refs[i][...].astype(BF)

    return pl.pallas_call(
        body, name=name,
        out_shape=tuple(jax.ShapeDtypeStruct(a.shape, BF) for a in arrays),
        in_specs=[_whole_vmem() for _ in arrays],
        out_specs=tuple(_whole_vmem() for _ in arrays),
        compiler_params=_params(),
    )(*arrays)


def _load_columns(stacked_ref, full_ref, sems):
    c = stacked_ref.shape[2]
    cps = [pltpu.make_async_copy(stacked_ref.at[d], full_ref.at[:, pl.ds(d * c, c)], sems.at[d]) for d in range(N_DEV)]
    for cp in cps:
        cp.start()
    for cp in cps:
        cp.wait()


def _sig(v):
    return 0.5 * jnp.tanh(0.5 * v) + 0.5


@jax.custom_vjp
def _silu(v):
    return v * _sig(v)


def _silu_fwd(v):
    return _silu(v), v


def _silu_bwd(v, g):
    s = _sig(v)
    return (g * (s * (1.0 + v * (1.0 - s))),)


_silu.defvjp(_silu_fwd, _silu_bwd)


@jax.custom_vjp
def _sigmoid_rel(v):
    e = jnp.exp(-jnp.abs(v))
    d = 1.0 + e
    r = pl.reciprocal(d, approx=True)
    r = r * (2.0 - d * r)
    r = r * (2.0 - d * r)
    return jnp.where(v >= 0.0, r, e * r)


def _sigmoid_rel_fwd(v):
    s = _sigmoid_rel(v)
    return s, s


def _sigmoid_rel_bwd(s, g):
    return (g * (s * (1.0 - s)),)


_sigmoid_rel.defvjp(_sigmoid_rel_fwd, _sigmoid_rel_bwd)


def _dot(a, b, ca, cb):
    return lax.dot_general(a, b, (((ca,), (cb,)), ((), ())), preferred_element_type=F32)


@jax.custom_vjp
def _mm(a, b):
    return _dot(a, b, 1, 0)


def _mm_fwd(a, b):
    return _dot(a, b, 1, 0), (a, b)


def _mm_bwd(res, g):
    a, b = res
    gb = g.astype(BF)
    return _dot(gb, b, 1, 1).astype(a.dtype), _dot(a, gb, 0, 0).astype(b.dtype)


_mm.defvjp(_mm_fwd, _mm_bwd)


@jax.custom_vjp
def _nt(a, b):
    return _dot(a, b, 1, 1)


def _nt_fwd(a, b):
    return _dot(a, b, 1, 1), (a, b)


def _nt_bwd(res, g):
    a, b = res
    gb = g.astype(BF)
    return _dot(gb, b, 1, 0).astype(a.dtype), _dot(gb, a, 0, 0).astype(b.dtype)


_nt.defvjp(_nt_fwd, _nt_bwd)


@jax.custom_vjp
def _tn(a, b):
    return _dot(a, b, 0, 0)


def _tn_fwd(a, b):
    return _dot(a, b, 0, 0), (a, b)


def _tn_bwd(res, g):
    a, b = res
    gb = g.astype(BF)
    return _dot(b, gb, 1, 1).astype(a.dtype), _dot(a, gb, 1, 0).astype(b.dtype)


_tn.defvjp(_tn_fwd, _tn_bwd)


def _make_cast_dot(ca, cb, da_dims, db_dims):
    @jax.custom_vjp
    def dot(a, b):
        return _dot(a.astype(BF), b.astype(BF), ca, cb)

    def fwd(a, b):
        ab, bb = a.astype(BF), b.astype(BF)
        return _dot(ab, bb, ca, cb), (ab, bb)

    def bwd(res, g):
        ops = {"a": res[0], "b": res[1], "g": g.astype(BF)}
        return (_dot(ops[da_dims[0]], ops[da_dims[1]], da_dims[2], da_dims[3]),
                _dot(ops[db_dims[0]], ops[db_dims[1]], db_dims[2], db_dims[3]))

    dot.defvjp(fwd, bwd)
    return dot


_mm_c = _make_cast_dot(1, 0, ("g", "b", 1, 1), ("a", "g", 0, 0))
_nt_c = _make_cast_dot(1, 1, ("g", "b", 1, 0), ("g", "a", 0, 0))
_tn_c = _make_cast_dot(0, 0, ("b", "g", 1, 1), ("a", "g", 1, 0))


def _startup(win_b, c, w_ada, b_cols):
    ncol = w_ada.shape[1]

    def body(win_ref, c_ref, w_ref, b_ref, winst_ref, adast_ref, cact_ref, call_ref, blk_ref,
             ws, wr, wl, cs, cr, cl, as_, ar, al):
        big = _Gather2([win_ref], [winst_ref], ws, wr, wl)
        big.start()
        gc = _Exchange([c_ref], [call_ref], cs, cr, cl, False)
        gc.start()
        gc.wait()
        ca = _silu(call_ref[...].reshape(N_DEV, D))
        cact_ref[...] = ca
        blk_ref[...] = _mm(ca.astype(BF), w_ref[...].astype(BF)) + b_ref[...]
        ga = _Exchange([blk_ref], [adast_ref], as_, ar, al, False)
        ga.start()
        ga.wait()
        big.forward()
        big.finish()

    return pl.pallas_call(
        body, name="startup",
        out_shape=(jax.ShapeDtypeStruct((N_DEV,) + win_b.shape, BF), jax.ShapeDtypeStruct((N_DEV, N_DEV, ncol), F32),
                   jax.ShapeDtypeStruct((N_DEV, D), F32)),
        in_specs=[_any(), _whole_vmem(), _whole_vmem(), _whole_vmem()],
        out_specs=(_any(), _whole_vmem(), _whole_vmem()),
        scratch_shapes=[pltpu.VMEM((N_DEV, 1, D), F32), pltpu.VMEM((N_DEV, ncol), F32)]
        + _exchange_sems(1) + _exchange_sems(1) + _exchange_sems(1),
        compiler_params=_params(has_side_effects=True),
    )(win_b, c, w_ada, b_cols)


def _decay_tables():
    t = np.arange(CH)
    tri = (t[None, :] <= t[:, None]).astype(np.float32)
    masks = []
    for h in LEVELS:
        m = (t // (2 * h)) * (2 * h) + h
        upper = t >= m
        same = (t[:, None] // (2 * h)) == (t[None, :] // (2 * h))
        masks.append(same & upper[:, None] & (~upper)[None, :])
    masks.append(np.eye(CH, dtype=bool))
    lv = [np.where((t % (2 * h)) >= h, 1.0, -1.0) for h in LEVELS[:4]]
    m4 = t % 4
    lv += [(m4 == 0) * 1.0, (m4 >= 2) * 1.0, (m4 == 3) * 1.0, (t % 2 == 1) * 1.0]
    lvl = np.broadcast_to(np.stack(lv)[:, :, None], (8, CH, D // 2)).astype(np.float32)
    cid = np.arange(BLK) // CH
    gmask = (cid[:, None] >= cid[None, :]).astype(np.float32)
    masks = np.stack(masks).astype(np.float32)
    stacked = np.zeros((masks.shape[0], NH * CH, NH * CH), np.float32)
    for h in range(NH):
        stacked[:, h * CH:(h + 1) * CH, h * CH:(h + 1) * CH] = masks
    return (jnp.asarray(tri, BF), jnp.asarray(stacked), jnp.asarray(gmask), jnp.asarray(lvl))


def _split2(v):
    v1 = v.astype(BF)
    return v1, (v - v1.astype(F32)).astype(BF)


@jax.custom_vjp
def _cumsum_mm(tri, v):
    p1, p2 = _split2(v)
    return _dot(tri, p1, 1, 0) + _dot(tri, p2, 1, 0)


def _cumsum_mm_fwd(tri, v):
    return _cumsum_mm(tri, v), tri


def _cumsum_mm_bwd(tri, g):
    p1, p2 = _split2(g)
    return jnp.zeros_like(tri), _dot(tri, p1, 0, 0) + _dot(tri, p2, 0, 0)


_cumsum_mm.defvjp(_cumsum_mm_fwd, _cumsum_mm_bwd)


def _make_row_roll(shift):
    @jax.custom_vjp
    def roll(x):
        return pltpu.roll(x, shift % CH, 0)

    def fwd(x):
        return roll(x), None

    def bwd(_, g):
        return (pltpu.roll(g, (-shift) % CH, 0),)

    roll.defvjp(fwd, bwd)
    return roll


_prev_row = _make_row_roll(1)
_next_row = _make_row_roll(-1)


def _mix_tile(guv, ph, state, w_s, b_s_t, ln_w, ln_b, lower, gn_w, consts):
    tri, masks, gmask, lvl = consts
    mt = guv.shape[0]
    half = D // 2

    wsm = [w_s[h] * gmask for h in range(NH)]
    ya = [[None] * NH for _ in range(mt // BLK)]
    for bi in range(mt // BLK):
        rows = slice(bi * BLK, (bi + 1) * BLK)
        ug = guv[rows, 0:half]
        vg = guv[rows, half:D]
        mu = jnp.mean(vg, axis=-1, keepdims=True)
        vc = vg - mu
        var = jnp.mean(vc * vc, axis=-1, keepdims=True)
        vn = vc * lax.rsqrt(var + EPS) * ln_w + ln_b
        for h in range(NH):
            cols = slice(h * HD, (h + 1) * HD)
            ya[bi][h] = ug[:, cols] * (_mm_c(wsm[h], vn[:, cols]) + b_s_t[:, h:h + 1])
    ya_full = jnp.concatenate([jnp.concatenate(r, axis=1) for r in ya], axis=0)

    l0 = lower[0:1, :]
    l1 = lower[1:2, :]
    mx = jnp.maximum(l0, l1)
    e0 = jnp.exp(l0 - mx)
    e1 = jnp.exp(l1 - mx)
    lb = e0 / (e0 + e1)
    nl = len(LEVELS)
    heads_to_rows = lambda a: jnp.concatenate([a[:, h * HD:(h + 1) * HD] for h in range(NH)], axis=0)
    st = list(state)
    yb = [[None] * NH for _ in range(mt // CH)]
    for ci in range(mt // CH):
        rows = slice(ci * CH, (ci + 1) * CH)
        qc = _silu(ph[rows, 0:half])
        f = lb + (1.0 - lb) * _sigmoid_rel(ph[rows, half:2 * half])
        lc = jnp.log(f)
        kc = 1.0 - f
        ic = ph[rows, 2 * half:3 * half]
        gate = _silu(ph[rows, 3 * half:4 * half])
        b = _cumsum_mm(tri, lc)
        xb = jnp.exp(b)
        xinv = jnp.exp(b[CH - 1:CH, :] - b)
        xl = []
        for i, hs in enumerate(LEVELS[:3]):
            refs = [jnp.broadcast_to(b[r:r + 1, :], (2 * hs, half)) for r in range(hs - 1, CH, 2 * hs)]
            bref = refs[0] if len(refs) == 1 else jnp.concatenate(refs, axis=0)
            xl.append(jnp.exp(lvl[i] * (b - bref)))
        b3 = b.reshape(CH // 8, 8, half)
        bref = jnp.broadcast_to(b3[:, 3:4, :], (CH // 8, 8, half)).reshape(CH, half)
        xl.append(jnp.exp(lvl[3] * (b - bref)))
        xl.append(jnp.exp(lvl[4] * _next_row(lc) + lvl[5] * lc + lvl[6] * _prev_row(lc)))
        xl.append(jnp.exp(lvl[7] * lc))
        zsrc = [jnp.concatenate([(qc if (r0 // hs) % 2 == 1 else kc)[r0:r0 + hs] for r0 in range(0, CH, hs)], axis=0)
                for hs in LEVELS[:3]]
        qs, ks = heads_to_rows(qc), heads_to_rows(kc)
        qx = qc * xb
        inter = jnp.concatenate([_nt_c(qx[:, h * HD:(h + 1) * HD], st[h]) for h in range(NH)], axis=0)
        attn = masks[nl] * _nt_c(qs, ks)
        for li in range(nl):
            xs = heads_to_rows(xl[li])
            if li < 3:
                z = heads_to_rows(zsrc[li]) * xs
                pairs = _nt_c(z, z)
            else:
                pairs = _nt_c(qs * xs, ks * xs)
            attn = attn + masks[li] * pairs
        o = inter + _mm_c(attn, heads_to_rows(ic))
        kx = kc * xinv
        decay = xb[CH - 1:CH, :]
        st = [st[h] * decay[:, h * HD:(h + 1) * HD] + _tn_c(ic[:, h * HD:(h + 1) * HD], kx[:, h * HD:(h + 1) * HD])
              for h in range(NH)]
        rs = lax.rsqrt(jnp.mean(o * o, axis=-1, keepdims=True) + EPS)
        ys = o * rs * gn_w * heads_to_rows(gate)
        yb[ci] = [ys[h * CH:(h + 1) * CH] for h in range(NH)]
    yb_full = jnp.concatenate([jnp.concatenate(r, axis=1) for r in yb], axis=0)
    return jnp.concatenate([ya_full, yb_full], axis=1), tuple(st)


def _fwd_mix(gelu_uv, proj_h, w_s, b_s_t, ln_w, ln_b, lower, gn_w, tables, to_gather):
    t = proj_h.shape[0]
    sub = _row_tile(t, MIX_TILE)
    n_sub = FWD_MIX_SUBTILES if t % (FWD_MIX_SUBTILES * sub) == 0 else 1
    mt = n_sub * sub
    nt_ = t // mt
    nc = len(tables)
    ng = len(to_gather)

    def body(*refs):
        guv_ref, ph_ref, ws_ref, bs_ref, lw_ref, lb_ref, lo_ref, gn_ref = refs[:8]
        c_refs = refs[8:8 + nc]
        refs = refs[8 + nc:]
        g_in = refs[:ng]
        y_ref, st_ref = refs[ng:2 + ng]
        g_out = refs[2 + ng:2 + 2 * ng]
        state = refs[2 + 2 * ng]
        ga = _Gather2(g_in, g_out, *refs[3 + 2 * ng:])

        @pl.when(pl.program_id(0) == 0)
        def _():
            state[...] = jnp.zeros_like(state)
            ga.start()

        @pl.when(pl.program_id(0) == (3 * nt_) // 4)
        def _():
            ga.forward()

        st = tuple(state[h] for h in range(NH))
        consts = tuple(r[...] for r in c_refs)
        for s in range(n_sub):
            rows = slice(s * sub, (s + 1) * sub)
            for h in range(NH):
                st_ref[s, h] = st[h]
            y, st = _mix_tile(guv_ref[rows, :].astype(F32), ph_ref[rows, :], st, ws_ref[...], bs_ref[...], lw_ref[...],
                              lb_ref[...], lo_ref[...], gn_ref[...], consts)
            y_ref[rows, :] = y.astype(BF)
        for h in range(NH):
            state[h] = st[h]

        @pl.when(pl.program_id(0) == nt_ - 1)
        def _():
            ga.finish()

    full = lambda a: pl.BlockSpec(a.shape, lambda i, nd=a.ndim: (0,) * nd)
    outs = pl.pallas_call(
        body, name="fwd_mix", grid=(nt_,),
        out_shape=(jax.ShapeDtypeStruct((t, D), BF), jax.ShapeDtypeStruct((t // sub, NH, HD, HD), F32),
                   *_exchange_out_shapes(to_gather, False)),
        in_specs=[pl.BlockSpec((mt, D), lambda i: (i, 0)), pl.BlockSpec((mt, D_IN - D), lambda i: (i, 0)), full(w_s), full(b_s_t),
                  full(ln_w), full(ln_b), full(lower), full(gn_w)] + [full(a) for a in tables] + [_any() for _ in to_gather],
        out_specs=(pl.BlockSpec((mt, D), lambda i: (i, 0)), pl.BlockSpec((n_sub, NH, HD, HD), lambda i: (i, 0, 0, 0)),
                   *[_any() for _ in to_gather]),
        scratch_shapes=[pltpu.VMEM((NH, HD, HD), F32)] + _exchange_sems(ng),
        compiler_params=_params(dimension_semantics=("arbitrary",), has_side_effects=True),
    )(gelu_uv, proj_h, w_s, b_s_t, ln_w, ln_b, lower, gn_w, *tables, *to_gather)
    return outs[0], outs[1], outs[2:]


def _bwd_mix(gelu_uv, dgelu_uv, proj_h, dycat, states, w_s, b_s_t, ln_w, ln_b, lower, gn_w, tables, to_scatter):
    t = proj_h.shape[0]
    sub = _row_tile(t, MIX_TILE)
    n_sub = BWD_MIX_SUBTILES if t % (BWD_MIX_SUBTILES * sub) == 0 else 1
    mt = n_sub * sub
    nt_ = t // mt
    nc = len(tables)
    ns = len(to_scatter)

    def body(*refs):
        guv_ref, gd_ref, ph_ref, dy_ref, st_ref, ws_ref, bs_ref, lw_ref, lb_ref, lo_ref, gn_ref = refs[:11]
        c_refs = refs[11:11 + nc]
        refs = refs[11 + nc:]
        s_in = refs[:ns]
        dproj_ref = refs[ns]
        acc = refs[1 + ns:7 + ns]
        s_out = refs[7 + ns:7 + 2 * ns]
        dstate = refs[7 + 2 * ns]
        ex = _Exchange(s_in, s_out, *refs[8 + 2 * ns:], True)

        @pl.when(pl.program_id(0) == 0)
        def _():
            dstate[...] = jnp.zeros_like(dstate)
            for r in acc:
                r[...] = jnp.zeros_like(r)
            ex.start()

        consts = tuple(r[...] for r in c_refs)

        def f(a, p, s, ws, bs, lw, lb_, lo, gn):
            return _mix_tile(a, p, s, ws, bs, lw, lb_, lo, gn, consts)

        dst = tuple(dstate[h] for h in range(NH))
        for s in reversed(range(n_sub)):
            rows = slice(s * sub, (s + 1) * sub)
            st = tuple(st_ref[s, h] for h in range(NH))
            _, vjp = jax.vjp(f, guv_ref[rows, :].astype(F32), ph_ref[rows, :], st, ws_ref[...], bs_ref[...], lw_ref[...],
                             lb_ref[...], lo_ref[...], gn_ref[...])
            grads = vjp((dy_ref[rows, :], dst))
            dproj_ref[rows, 0:D] = (grads[0] * gd_ref[rows, :].astype(F32)).astype(BF)
            dproj_ref[rows, D:D_IN] = grads[1].astype(BF)
            dst = grads[2]
            for r, gval in zip(acc, grads[3:]):
                r[...] += gval
        for h in range(NH):
            dstate[h] = dst[h]

        @pl.when(pl.program_id(0) == nt_ - 1)
        def _():
            ex.wait()

    full = lambda a: pl.BlockSpec(a.shape, lambda i, nd=a.ndim: (0,) * nd)
    rev = lambda i: (nt_ - 1 - i, 0)
    smalls = (w_s, b_s_t, ln_w, ln_b, lower, gn_w)
    outs = pl.pallas_call(
        body, name="bwd_mix", grid=(nt_,),
        out_shape=(jax.ShapeDtypeStruct((t, D_IN), BF), *[jax.ShapeDtypeStruct(a.shape, F32) for a in smalls],
                   *_exchange_out_shapes(to_scatter, True)),
        in_specs=[pl.BlockSpec((mt, D), rev), pl.BlockSpec((mt, D), rev), pl.BlockSpec((mt, D_IN - D), rev),
                  pl.BlockSpec((mt, D), rev), pl.BlockSpec((n_sub, NH, HD, HD), lambda i: (nt_ - 1 - i, 0, 0, 0))]
        + [full(a) for a in smalls] + [full(a) for a in tables] + [_any() for _ in to_scatter],
        out_specs=(pl.BlockSpec((mt, D_IN), rev), *[full(a) for a in smalls], *[_any() for _ in to_scatter]),
        scratch_shapes=[pltpu.VMEM((NH, HD, HD), F32)] + _exchange_sems(ns),
        compiler_params=_params(dimension_semantics=("arbitrary",), has_side_effects=True),
    )(gelu_uv, dgelu_uv, proj_h, dycat, states, w_s, b_s_t, ln_w, ln_b, lower, gn_w, *tables, *to_scatter)
    return outs[0], outs[1:7], outs[7:]


def _row_tile(t, want):
    return want if t % want == 0 else t


def _rms(v):
    rstd = lax.rsqrt(jnp.mean(v * v, axis=-1, keepdims=True) + EPS)
    return v * rstd, rstd


def _rms_bwd(dxhat, xhat, rstd):
    return rstd * (dxhat - xhat * jnp.mean(dxhat * xhat, axis=-1, keepdims=True))


def _colsum(v):
    return jnp.sum(v, axis=0, keepdims=True)


def _fwd_in(x, ada, n1w, w_in_st, to_gather):
    t = x.shape[0]
    tm = _row_tile(t, 512)
    n_t = t // tm
    ng = len(to_gather)

    def body(*refs):
        x_ref, ada_ref, n1_ref, wst_ref = refs[:4]
        g_in = refs[4:4 + ng]
        ph_ref, ga_ref, gd_ref, h1_ref = refs[4 + ng:8 + ng]
        g_out = refs[8 + ng:8 + 2 * ng]
        w_ref, sems = refs[8 + 2 * ng:10 + 2 * ng]
        ga = _Gather2(g_in, g_out, *refs[10 + 2 * ng:])

        @pl.when(pl.program_id(0) == 0)
        def _():
            ga.start()
            _load_columns(wst_ref, w_ref, sems)

        @pl.when(pl.program_id(0) == (3 * n_t) // 4)
        def _():
            ga.forward()

        xh, _ = _rms(x_ref[...])
        h1 = (xh * n1_ref[...]) * (1.0 + ada_ref[1:2, :]) + ada_ref[0:1, :]
        h1b = h1.astype(BF)
        h1_ref[...] = h1b
        uv = _mm(h1b, w_ref[:, 0:D])
        cdf = 0.5 * (1.0 + lax.erf(uv * 0.7071067811865476))
        ga_ref[...] = (uv * cdf).astype(BF)
        gd_ref[...] = (cdf + uv * (jnp.exp(-0.5 * uv * uv) * 0.3989422804014327)).astype(BF)
        ph_ref[...] = _mm(h1b, w_ref[:, D:D_IN])

        @pl.when(pl.program_id(0) == n_t - 1)
        def _():
            ga.finish()

    full = lambda a: pl.BlockSpec(a.shape, lambda i, nd=a.ndim: (0,) * nd)
    row = lambda w: pl.BlockSpec((tm, w), lambda i: (i, 0))
    outs = pl.pallas_call(
        body, name="fwd_in", grid=(n_t,),
        out_shape=(jax.ShapeDtypeStruct((t, D_IN - D), F32), jax.ShapeDtypeStruct((t, D), BF), jax.ShapeDtypeStruct((t, D), BF),
                   jax.ShapeDtypeStruct((t, D), BF), *_exchange_out_shapes(to_gather, False)),
        in_specs=[row(D), full(ada), full(n1w), _any()] + [_any() for _ in to_gather],
        out_specs=(row(D_IN - D), row(D), row(D), row(D), *[_any() for _ in to_gather]),
        scratch_shapes=[pltpu.VMEM((D, D_IN), BF), pltpu.SemaphoreType.DMA((N_DEV,))] + _exchange_sems(ng),
        compiler_params=_params(dimension_semantics=("arbitrary",), has_side_effects=True),
    )(x, ada, n1w, w_in_st, *to_gather)
    return outs[0], outs[1], outs[2], outs[3], outs[4:]


def _fwd_ffn(x, ycat, tgt, ada, n2w, fw, w_out, w_fi, w_fo):
    t = x.shape[0]
    tm = _row_tile(t, 256)

    def body(x_ref, y_ref, t_ref, ada_ref, n2_ref, fw_ref, wo_ref, wi_ref, wf_ref,
             x1_ref, mix_ref, h2_ref, act_ref, gu_ref, dffn_ref, dx2_ref, part_ref):
        @pl.when(pl.program_id(0) == 0)
        def _():
            part_ref[...] = jnp.zeros_like(part_ref)

        g1, sh2, sc2, g2 = ada_ref[2:3, :], ada_ref[3:4, :], ada_ref[4:5, :], ada_ref[5:6, :]
        mix = _mm(y_ref[...], wo_ref[...])
        mix_ref[...] = mix.astype(BF)
        x1 = x_ref[...] + g1 * mix
        x1_ref[...] = x1
        xh2, _ = _rms(x1)
        h2b = ((xh2 * n2_ref[...]) * (1.0 + sc2) + sh2).astype(BF)
        h2_ref[...] = h2b
        ffn = jnp.zeros((tm, D), F32)
        for lo, hi in FF_CHUNKS:
            gate = _nt(h2b, wi_ref[lo:hi, :])
            up = _nt(h2b, wi_ref[D_FF + lo:D_FF + hi, :])
            gu_ref[:, lo:hi] = gate.astype(BF)
            gu_ref[:, D_FF + lo:D_FF + hi] = up.astype(BF)
            a = (_silu(gate) * up).astype(BF)
            act_ref[:, lo:hi] = a
            ffn = ffn + _mm(a, wf_ref[lo:hi, :])
        x2 = x1 + g2 * ffn
        xh3, rstd3 = _rms(x2)
        err = xh3 * fw_ref[...] - t_ref[...]
        dy = err * (1.0 / D)
        dx2 = _rms_bwd(dy * fw_ref[...], xh3, rstd3)
        dx2_ref[...] = dx2
        dffn_ref[...] = (g2 * dx2).astype(BF)
        part_ref[0:1, :] += _colsum(dx2 * ffn)
        part_ref[1:2, :] += _colsum(dy * xh3)
        part_ref[2:3, :] += jnp.zeros((1, D), F32) + (0.5 / D) * jnp.sum(err * err)

    full = lambda a: pl.BlockSpec(a.shape, lambda i, nd=a.ndim: (0,) * nd)
    row = lambda w: pl.BlockSpec((tm, w), lambda i: (i, 0))
    return pl.pallas_call(
        body, name="fwd_ffn", grid=(t // tm,),
        out_shape=(jax.ShapeDtypeStruct((t, D), F32), jax.ShapeDtypeStruct((t, D), BF), jax.ShapeDtypeStruct((t, D), BF),
                   jax.ShapeDtypeStruct((t, D_FF), BF), jax.ShapeDtypeStruct((t, 2 * D_FF), BF),
                   jax.ShapeDtypeStruct((t, D), BF), jax.ShapeDtypeStruct((t, D), F32), jax.ShapeDtypeStruct((8, D), F32)),
        in_specs=[row(D), row(D), row(D), full(ada), full(n2w), full(fw), _whole_vmem(), _whole_vmem(), _whole_vmem()],
        out_specs=(row(D), row(D), row(D), row(D_FF), row(2 * D_FF), row(D), row(D),
                   pl.BlockSpec((8, D), lambda i: (0, 0))),
        compiler_params=_params(dimension_semantics=("arbitrary",)),
    )(x, ycat, tgt, ada, n2w, fw, w_out, w_fi, w_fo)


def _bwd_ffn(x1, gu, dffn, dx2, mix, ada, n2w, w_out, w_fi, w_fo):
    t = x1.shape[0]
    tm = _row_tile(t, 256)

    def body(x1_ref, gu_ref, dffn_ref, dx2_ref, mix_ref, ada_ref, n2_ref, wo_ref, wi_ref, wf_ref,
             dgu_ref, dx1_ref, dmix_ref, dycat_ref, part_ref):
        @pl.when(pl.program_id(0) == 0)
        def _():
            part_ref[...] = jnp.zeros_like(part_ref)

        g1, sc2 = ada_ref[2:3, :], ada_ref[4:5, :]
        dffn = dffn_ref[...]
        dh2 = jnp.zeros((tm, D), F32)
        for lo, hi in FF_CHUNKS:
            gate = gu_ref[:, lo:hi].astype(F32)
            up = gu_ref[:, D_FF + lo:D_FF + hi].astype(F32)
            dact = _nt(dffn, wf_ref[lo:hi, :])
            sg = _sig(gate)
            dgate = (dact * up * (sg * (1.0 + gate * (1.0 - sg)))).astype(BF)
            dup = (dact * (gate * sg)).astype(BF)
            dgu_ref[:, lo:hi] = dgate
            dgu_ref[:, D_FF + lo:D_FF + hi] = dup
            dh2 = dh2 + _mm(dgate, wi_ref[lo:hi, :]) + _mm(dup, wi_ref[D_FF + lo:D_FF + hi, :])
        x1 = x1_ref[...]
        xh2, rstd2 = _rms(x1)
        xn2 = xh2 * n2_ref[...]
        dxn2 = dh2 * (1.0 + sc2)
        dx1 = dx2_ref[...] + _rms_bwd(dxn2 * n2_ref[...], xh2, rstd2)
        dx1_ref[...] = dx1
        dmix = (g1 * dx1).astype(BF)
        dmix_ref[...] = dmix
        dycat_ref[...] = _nt(dmix, wo_ref[...])
        part_ref[0:1, :] += _colsum(dh2)
        part_ref[1:2, :] += _colsum(dh2 * xn2)
        part_ref[2:3, :] += _colsum(dxn2 * xh2)
        part_ref[3:4, :] += _colsum(dx1 * mix_ref[...].astype(F32))

    full = lambda a: pl.BlockSpec(a.shape, lambda i, nd=a.ndim: (0,) * nd)
    row = lambda w: pl.BlockSpec((tm, w), lambda i: (i, 0))
    return pl.pallas_call(
        body, name="bwd_ffn", grid=(t // tm,),
        out_shape=(jax.ShapeDtypeStruct((t, 2 * D_FF), BF), jax.ShapeDtypeStruct((t, D), F32),
                   jax.ShapeDtypeStruct((t, D), BF), jax.ShapeDtypeStruct((t, D), F32), jax.ShapeDtypeStruct((8, D), F32)),
        in_specs=[row(D), row(2 * D_FF), row(D), row(D), row(D), full(ada), full(n2w),
                  _whole_vmem(), _whole_vmem(), _whole_vmem()],
        out_specs=(row(2 * D_FF), row(D), row(D), row(D), pl.BlockSpec((8, D), lambda i: (0, 0))),
        compiler_params=_params(dimension_semantics=("arbitrary",)),
    )(x1, gu, dffn, dx2, mix, ada, n2w, w_out, w_fi, w_fo)


def _bwd_in(x, dproj, dx1, ada, n1w, w_in_st, to_scatter, to_gather):
    t = x.shape[0]
    tm = _row_tile(t, 1024)
    n_t = t // tm
    ns = len(to_scatter)
    ng = len(to_gather)
    nx = ns + ng

    def body(*refs):
        x_ref, dp_ref, dx1_ref, ada_ref, n1_ref, wst_ref = refs[:6]
        x_in = refs[6:6 + nx]
        gx_ref, part_ref = refs[6 + nx:8 + nx]
        x_out = refs[8 + nx:8 + 2 * nx]
        w_ref, sems = refs[8 + 2 * nx:10 + 2 * nx]
        sem_refs = refs[10 + 2 * nx:]
        ex = _Exchange(x_in[:ns], x_out[:ns], *sem_refs[:3], True)
        gx = _Exchange(x_in[ns:], x_out[ns:], *sem_refs[3:], False) if ng else None

        @pl.when(pl.program_id(0) == 0)
        def _():
            ex.start()
            if ng:
                gx.start()
            part_ref[...] = jnp.zeros_like(part_ref)
            _load_columns(wst_ref, w_ref, sems)

        dh1 = _nt(dp_ref[...], w_ref[...])
        xh, rstd = _rms(x_ref[...])
        xn = xh * n1_ref[...]
        dxn = dh1 * (1.0 + ada_ref[1:2, :])
        gx_ref[...] = dx1_ref[...] + _rms_bwd(dxn * n1_ref[...], xh, rstd)
        part_ref[0:1, :] += _colsum(dh1)
        part_ref[1:2, :] += _colsum(dh1 * xn)
        part_ref[2:3, :] += _colsum(dxn * xh)

        @pl.when(pl.program_id(0) == n_t - 1)
        def _():
            ex.wait()
            if ng:
                gx.wait()

    full = lambda a: pl.BlockSpec(a.shape, lambda i, nd=a.ndim: (0,) * nd)
    row = lambda w: pl.BlockSpec((tm, w), lambda i: (i, 0))
    outs = pl.pallas_call(
        body, name="bwd_in", grid=(n_t,),
        out_shape=(jax.ShapeDtypeStruct((t, D), F32), jax.ShapeDtypeStruct((8, D), F32),
                   *_exchange_out_shapes(to_scatter, True), *_exchange_out_shapes(to_gather, False)),
        in_specs=[row(D), row(D_IN), row(D), full(ada), full(n1w), _any()] + [_any() for _ in range(nx)],
        out_specs=(row(D), pl.BlockSpec((8, D), lambda i: (0, 0)), *[_any() for _ in range(nx)]),
        scratch_shapes=[pltpu.VMEM((D, D_IN), BF), pltpu.SemaphoreType.DMA((N_DEV,))] + _exchange_sems(ns)
        + (_exchange_sems(ng) if ng else []),
        input_output_aliases={2: 0},
        compiler_params=_params(dimension_semantics=("arbitrary",)),
    )(x, dproj, dx1, ada, n1w, w_in_st, *to_scatter, *to_gather)
    return outs[0], outs[1], outs[2:2 + ns], outs[2 + ns:]


def _wgrad(a, b, name, a_spec, b_spec, out_shape, out_spec, grid, acc_shape, split=1, to_gather=()):
    n_j, n_t = grid
    ng = len(to_gather)

    def body(*refs):
        a_ref, b_ref = refs[:2]
        g_in = refs[2:2 + ng]
        o_ref = refs[2 + ng]
        g_out = refs[3 + ng:3 + 2 * ng]
        acc = refs[3 + 2 * ng]
        first = (pl.program_id(0) == 0) & (pl.program_id(1) == 0)
        last = (pl.program_id(0) == n_j - 1) & (pl.program_id(1) == n_t - 1)
        if ng:
            gx = _Exchange(g_in, g_out, *refs[4 + 2 * ng:], False)

            @pl.when(first)
            def _():
                gx.start()

        @pl.when(pl.program_id(1) == 0)
        def _():
            acc[...] = jnp.zeros_like(acc)

        acc[...] += _tn(a_ref[...], b_ref[...])

        @pl.when(pl.program_id(1) == n_t - 1)
        def _():
            if split == 1:
                o_ref[...] = acc[...].astype(BF)
            else:
                w = acc_shape[1] // split
                for s in range(split):
                    o_ref[s] = acc[:, s * w:(s + 1) * w].astype(BF)

        if ng:
            @pl.when(last)
            def _():
                gx.wait()

    outs = pl.pallas_call(
        body, name=name, grid=grid,
        out_shape=(jax.ShapeDtypeStruct(out_shape, BF), *_exchange_out_shapes(to_gather, False)),
        in_specs=[a_spec, b_spec] + [_any() for _ in to_gather], out_specs=(out_spec, *[_any() for _ in to_gather]),
        scratch_shapes=[pltpu.VMEM(acc_shape, F32)] + (_exchange_sems(ng) if ng else []),
        compiler_params=_params(dimension_semantics=("arbitrary", "arbitrary"), has_side_effects=bool(ng)),
    )(a, b, *to_gather)
    return (outs[0], outs[1:]) if ng else outs[0]


def _adamw_math(w, g, m, v):
    m = ADAM_B1 * m + (1.0 - ADAM_B1) * g
    v = ADAM_B2 * v + (1.0 - ADAM_B2) * (g * g)
    m_hat = m / (1.0 - ADAM_B1 ** ADAM_STEP)
    v_hat = v / (1.0 - ADAM_B2 ** ADAM_STEP)
    delta = -ADAM_LR * (m_hat / (jnp.sqrt(v_hat) + ADAM_EPS) + ADAM_WD * w)
    return delta, m, v


def _adamw_recv(w, m, v, recv, name, tr, carry=()):
    r, c = w.shape

    def body(w_ref, m_ref, v_ref, r_ref, *refs):
        g_ref, d_ref, nm_ref, nv_ref = refs[len(carry):len(carry) + 4]
        g = r_ref[0].astype(F32)
        for k in range(1, N_DEV):
            g = g + r_ref[k].astype(F32)
        g_ref[...] = g
        d_ref[...], nm_ref[...], nv_ref[...] = _adamw_math(w_ref[...], g, m_ref[...], v_ref[...])

    row = pl.BlockSpec((tr, c), lambda i: (i, 0))
    outs = pl.pallas_call(
        body, name=name, grid=(r // tr,),
        out_shape=tuple(jax.ShapeDtypeStruct((r, c), F32) for _ in range(4))
        + tuple(jax.ShapeDtypeStruct(a.shape, a.dtype) for a in carry),
        in_specs=[row, row, row, pl.BlockSpec((N_DEV, tr, c), lambda i: (0, i, 0))] + [_any() for _ in carry],
        out_specs=(row, row, row, row) + tuple(_any() for _ in carry),
        input_output_aliases={4 + i: 4 + i for i in range(len(carry))},
        compiler_params=_params(dimension_semantics=("arbitrary",)),
    )(w, m, v, recv, *carry)
    return tuple(outs)


def _adamw_ada(w, m, v, cact, dada_cols):
    r, c = w.shape
    tr = 256

    def body(w_ref, m_ref, v_ref, ca_ref, da_ref, g_ref, d_ref, nm_ref, nv_ref):
        g = _tn(ca_ref[...].astype(BF), da_ref[...].astype(BF))
        g_ref[...] = g
        d_ref[...], nm_ref[...], nv_ref[...] = _adamw_math(w_ref[...], g, m_ref[...], v_ref[...])

    row = pl.BlockSpec((tr, c), lambda i: (i, 0))
    return pl.pallas_call(
        body, name="adamw_ada", grid=(r // tr,),
        out_shape=tuple(jax.ShapeDtypeStruct((r, c), F32) for _ in range(4)),
        in_specs=[row, row, row, pl.BlockSpec((N_DEV, tr), lambda i: (0, i)), pl.BlockSpec(dada_cols.shape, lambda i: (0, 0))],
        out_specs=(row, row, row, row),
        compiler_params=_params(dimension_semantics=("arbitrary",)),
    )(w, m, v, cact, dada_cols)


def _adamw_small(gathered, wmv):
    n_g = len(gathered)
    n_p = len(wmv)
    flat = [a for trip in wmv for a in trip]

    def body(*refs):
        g_refs = refs[:n_g]
        p_refs = refs[n_g:n_g + 3 * n_p]
        o_refs = refs[n_g + 3 * n_p:]

        def total(ref):
            s = ref[0]
            for k in range(1, N_DEV):
                s = s + ref[k]
            return s

        f3, b3, b1, dws, dbs, dlnw, dlnb, dlo, dgn = [total(r) for r in g_refs]
        dada_rows = [b1[0:1], b1[1:2], b3[3:4], b3[0:1], b3[1:2], f3[0:1]]
        for r, g in enumerate(dada_rows):
            cs = slice(r * D, (r + 1) * D)
            w, m, v = p_refs[0][:, cs], p_refs[1][:, cs], p_refs[2][:, cs]
            o_refs[0][:, cs] = g
            o_refs[1][:, cs], o_refs[2][:, cs], o_refs[3][:, cs] = _adamw_math(w, g, m, v)
        grads = [None, b1[2:3], dws, dbs, dlnw, dlnb, dlo, dgn, b3[2:3], f3[1:2]]
        for i, g in enumerate(grads):
            if g is None:
                continue
            w, m, v = p_refs[3 * i][...], p_refs[3 * i + 1][...], p_refs[3 * i + 2][...]
            o_refs[4 * i][...] = g
            o_refs[4 * i + 1][...], o_refs[4 * i + 2][...], o_refs[4 * i + 3][...] = _adamw_math(w, g, m, v)
        o_refs[4 * n_p][...] = jnp.zeros((8, 128), F32) + f3[2:3, 0:128]

    out_shape = []
    for w, _, _ in wmv:
        out_shape += [jax.ShapeDtypeStruct(w.shape, F32)] * 4
    out_shape.append(jax.ShapeDtypeStruct((8, 128), F32))
    n_in = n_g + 3 * n_p
    return pl.pallas_call(
        body, name="adamw_small",
        out_shape=tuple(out_shape),
        in_specs=[_whole_vmem()] * n_in, out_specs=tuple(_whole_vmem() for _ in out_shape),
        compiler_params=_params(),
    )(*gathered, *flat)


def kernel(x, c, w_ada, b_ada, norm1_w, w_in, w_s, b_s, v_ln_w, v_ln_b, lower_bounds, gn_w, w_out, norm2_w, w_ffn_in, w_ffn_out, final_norm_w, loss_target, m_w_ada, m_b_ada, m_norm1_w, m_w_in, m_w_s, m_b_s, m_v_ln_w, m_v_ln_b, m_lower_bounds, m_gn_w, m_w_out, m_norm2_w, m_w_ffn_in, m_w_ffn_out, m_final_norm_w, v_w_ada, v_b_ada, v_norm1_w, v_w_in, v_w_s, v_b_s, v_v_ln_w, v_v_ln_b, v_lower_bounds, v_gn_w, v_w_out, v_norm2_w, v_w_ffn_in, v_w_ffn_out, v_final_norm_w):
    me = 4 * lax.axis_index("x") + 2 * lax.axis_index("y") + lax.axis_index("c")
    t = x.shape[1]
    x2d = x.reshape(t, D)
    tgt = loss_target.reshape(t, D)
    ada_cols = w_ada.shape[2]

    tp = lambda a: jnp.swapaxes(a[0], 0, 1)
    win_b, wout_b, wfi_b, wfo_b = _cast_bf16([w_in[0], w_out[0], tp(w_ffn_in), w_ffn_out[0]], "cast_weights")

    b_cols = lax.dynamic_slice(b_ada, (0, me * ada_cols), (1, ada_cols))
    win_st, ada_st, cact = _startup(win_b, c, w_ada[0], b_cols)
    ada = lax.dynamic_index_in_dim(ada_st, me, axis=1, keepdims=False).reshape(6, D)

    tables = _decay_tables()
    ws3 = w_s[0]
    bs_t = b_s[0].T

    proj_h, gelu_uv, dgelu_uv, h1, (wout_st, wfo_st) = _fwd_in(x2d, ada, norm1_w, win_st, [wout_b, wfo_b])
    ycat, states, (wfi_st,) = _fwd_mix(gelu_uv, proj_h, ws3, bs_t, v_ln_w, v_ln_b, lower_bounds, gn_w, tables, [wfi_b])
    w_out_full = wout_st.reshape(D, D)
    w_fo_full = wfo_st.reshape(D_FF, D)
    w_fi_full = wfi_st.reshape(2 * D_FF, D)
    x1, mixb, h2, act, gu, dffn, dx2, part_f = _fwd_ffn(x2d, ycat, tgt, ada, norm2_w, final_norm_w.reshape(1, D),
                                                        w_out_full, w_fi_full, w_fo_full)
    dgu, dx1, dmix, dycat, part_b3 = _bwd_ffn(x1, gu, dffn, dx2, mixb, ada, norm2_w, w_out_full, w_fi_full, w_fo_full)
    tk = _row_tile(t, 2048)
    n_t = t // tk
    win_cols = D_IN // N_DEV
    dwout = _wgrad(ycat, dmix, "wgrad_out",
                   pl.BlockSpec((tk, D), lambda j, i: (i, 0)), pl.BlockSpec((tk, D), lambda j, i: (i, 0)),
                   (D, D), pl.BlockSpec((D, D), lambda j, i: (0, 0)), (1, n_t), (D, D))
    dwfi = _wgrad(dgu, h2, "wgrad_ffn_in",
                  pl.BlockSpec((tk, FF_PAIR), lambda j, i: (i, j)), pl.BlockSpec((tk, D), lambda j, i: (i, 0)),
                  (4, FF_PAIR, D), pl.BlockSpec((None, FF_PAIR, D), lambda j, i: (j, 0, 0)), (4, n_t), (FF_PAIR, D))
    dwfo = _wgrad(act, dffn, "wgrad_ffn_out",
                  pl.BlockSpec((tk, FF_PAIR), lambda j, i: (i, j)), pl.BlockSpec((tk, D), lambda j, i: (i, 0)),
                  (2, FF_PAIR, D), pl.BlockSpec((None, FF_PAIR, D), lambda j, i: (j, 0, 0)), (2, n_t), (FF_PAIR, D))
    dproj, (dws, dbs_t, dlnw, dlnb, dlower, dgnw), (r_out, r_fi, r_fo) = _bwd_mix(
        gelu_uv, dgelu_uv, proj_h, dycat, states, ws3, bs_t, v_ln_w, v_ln_b, lower_bounds, gn_w, tables,
        [dwout.reshape(N_DEV, D // N_DEV, D), dwfi.reshape(N_DEV, FF_BLK, D), dwfo.reshape(N_DEV, D_FF // N_DEV, D)])
    dwin, early = _wgrad(h1, dproj, "wgrad_in",
                         pl.BlockSpec((tk, D), lambda j, i: (i, 0)), pl.BlockSpec((tk, 4 * win_cols), lambda j, i: (i, j)),
                         (N_DEV, D, win_cols), pl.BlockSpec((4, D, win_cols), lambda j, i: (j, 0, 0)), (N_DEV // 4, n_t),
                         (D, 4 * win_cols), split=4, to_gather=[part_f, part_b3, dws, dbs_t, dlnw, dlnb, dlower, dgnw])
    grad_x, part_b1, (r_in,), _ = _bwd_in(x2d, dproj, dx1, ada, norm1_w, win_st, [dwin], [])

    g_w_in, d_w_in, nm_w_in, nv_w_in = _adamw_recv(w_in[0], m_w_in[0], v_w_in[0], r_in, "adamw_w_in", 256)
    g_w_out, d_w_out, nm_w_out, nv_w_out = _adamw_recv(w_out[0], m_w_out[0], v_w_out[0], r_out, "adamw_w_out", 128)
    fi_t = _adamw_recv(tp(w_ffn_in), tp(m_w_ffn_in), tp(v_w_ffn_in), r_fi, "adamw_w_ffn_in", 176)
    g_w_fi, d_w_fi, nm_w_fi, nv_w_fi = [jnp.swapaxes(a, 0, 1) for a in fi_t]
    g_w_fo, d_w_fo, nm_w_fo, nv_w_fo = _adamw_recv(w_ffn_out[0], m_w_ffn_out[0], v_w_ffn_out[0], r_fo, "adamw_w_ffn_out", 176)

    (b1_all,) = _exchange([part_b1], "gather_small", False, True)
    gathered = [early[0], early[1], b1_all, *early[2:]]
    f3_all, b3_all = gathered[0], gathered[1]
    dada_all = jnp.stack([b1_all[:, 0], b1_all[:, 1], b3_all[:, 3], b3_all[:, 0], b3_all[:, 1], f3_all[:, 0]], axis=1)
    dada_cols = lax.dynamic_slice(dada_all.reshape(N_DEV, 6 * D), (0, me * ada_cols), (N_DEV, ada_cols))
    g_w_ada, d_w_ada, nm_w_ada, nv_w_ada = _adamw_ada(w_ada[0], m_w_ada[0], v_w_ada[0], cact, dada_cols)

    r1 = lambda a: a.reshape(1, D)
    tr = lambda a: a[0].T
    wmv = [
        (b_ada, m_b_ada, v_b_ada),
        (norm1_w, m_norm1_w, v_norm1_w),
        (w_s[0], m_w_s[0], v_w_s[0]),
        (tr(b_s), tr(m_b_s), tr(v_b_s)),
        (v_ln_w, m_v_ln_w, v_v_ln_w),
        (v_ln_b, m_v_ln_b, v_v_ln_b),
        (lower_bounds, m_lower_bounds, v_lower_bounds),
        (gn_w, m_gn_w, v_gn_w),
        (norm2_w, m_norm2_w, v_norm2_w),
        (r1(final_norm_w), r1(m_final_norm_w), r1(v_final_norm_w)),
    ]
    small = _adamw_small(gathered, wmv)
    loss = small[-1][0, 0]

    def unshape(i, a):
        if i == 2:
            return a.reshape(1, NH, BLK, BLK)
        if i == 3:
            return a.T.reshape(1, NH, BLK)
        if i == 9:
            return a.reshape(D)
        return a

    def small_out(kind):
        return [unshape(i, small[4 * i + kind]) for i in range(len(wmv))]

    e3 = lambda a: a[None]
    big = {
        0: (e3(g_w_ada), e3(g_w_in), e3(g_w_out), e3(g_w_fi), e3(g_w_fo)),
        1: (e3(d_w_ada), e3(d_w_in), e3(d_w_out), e3(d_w_fi), e3(d_w_fo)),
        2: (e3(nm_w_ada), e3(nm_w_in), e3(nm_w_out), e3(nm_w_fi), e3(nm_w_fo)),
        3: (e3(nv_w_ada), e3(nv_w_in), e3(nv_w_out), e3(nv_w_fi), e3(nv_w_fo)),
    }

    def ordered(kind):
        s = small_out(kind)
        b_ = big[kind]
        return [b_[0], s[0], s[1], b_[1], s[2], s[3], s[4], s[5], s[6], s[7], b_[2], s[8], b_[3], b_[4], s[9]]

    return (loss, grad_x.reshape(1, t, D), *ordered(0), *ordered(1), *ordered(2), *ordered(3))
```
